```python
import jax, jax.numpy as jnp
from jax import lax
import numpy as np

D_MODEL = 1024
BATCH = 8
SEQ = 16384
DEPTH = 4

N_A_LAYERS = DEPTH // 2
N_B_LAYERS = DEPTH - N_A_LAYERS

POOL_WINDOWS = (2, 4, 8, 16)
N_POOL_GROUPS = len(POOL_WINDOWS)
POOL_GROUP = D_MODEL // N_POOL_GROUPS

N_HEADS = 16
QK_NOPE_DIM = 128
QK_ROPE_DIM = 64
QK_DIM = QK_NOPE_DIM + QK_ROPE_DIM
V_DIM = 128
Q_LORA_RANK = 256
KV_LORA_RANK = 128
ROPE_THETA = 10000.0
Q_BLOCK = 128

D_FF = 2816
RMS_EPS = 1e-6

kernel_name = "yoco_pool_mla_macaron_trunk"


def rmsnorm(x, g):
    xf = x.astype(jnp.float32)
    y = xf * lax.rsqrt(jnp.mean(xf * xf, axis=-1, keepdims=True) + RMS_EPS)
    return (y * g.astype(jnp.float32)).astype(x.dtype)


def swiglu(u, wg, wu, wd):
    return (jax.nn.silu(u @ wg) * (u @ wu)) @ wd


def rope_tables(seq):
    pos = jnp.arange(seq, dtype=jnp.float32)
    inv_freq = ROPE_THETA ** (-jnp.arange(0, QK_ROPE_DIM, 2, dtype=jnp.float32) / QK_ROPE_DIM)
    ang = pos[:, None] * inv_freq[None, :]
    return jnp.cos(ang), jnp.sin(ang)


def apply_rope(x, cos, sin):
    half = x.shape[-1] // 2
    x1, x2 = x[..., :half], x[..., half:]
    c, s = cos.astype(x.dtype), sin.astype(x.dtype)
    return jnp.concatenate([x1 * c - x2 * s, x1 * s + x2 * c], axis=-1)


def pool_mixer(u, w_groups, scale):
    B, S, D = u.shape
    uf = u.astype(jnp.float32)
    c = jnp.cumsum(uf, axis=1)
    count = jnp.arange(1, S + 1, dtype=jnp.float32)[None, :, None]
    outs = []
    for g, w in enumerate(POOL_WINDOWS):
        sl = slice(g * POOL_GROUP, (g + 1) * POOL_GROUP)
        cg = c[..., sl]
        lag = jnp.pad(cg, ((0, 0), (w, 0), (0, 0)))[:, :S]
        mean = (cg - lag) / jnp.minimum(count, float(w))
        outs.append(mean - uf[..., sl])
    y = jnp.stack(outs, axis=2).astype(u.dtype)
    z = jnp.einsum('bsgc,gcd->bsgd', y, w_groups).reshape(B, S, D)
    return z * scale


def mla_shared_kv(h, kv_in_norm, w_dkv, ckv_norm, w_uk, w_uv, cos, sin):
    u = rmsnorm(h, kv_in_norm)
    kv_a = u @ w_dkv
    c_kv = rmsnorm(kv_a[..., :KV_LORA_RANK], ckv_norm)
    k_rope = apply_rope(kv_a[..., KV_LORA_RANK:], cos[None], sin[None])
    k_nope = jnp.einsum('bsr,rhd->bshd', c_kv, w_uk)
    v = jnp.einsum('bsr,rhd->bshd', c_kv, w_uv)
    return k_nope, k_rope, v


def mla_attention(u, q_lora_norm, w_dq, w_uq, w_o, k_nope, k_rope, v, cos, sin):
    B, S, _ = u.shape
    cq = rmsnorm(u @ w_dq, q_lora_norm)
    q = jnp.einsum('bsr,rhd->bshd', cq, w_uq)
    q_nope = q[..., :QK_NOPE_DIM]
    q_rope = apply_rope(q[..., QK_NOPE_DIM:], cos[None, :, None], sin[None, :, None])
    nblk = S // Q_BLOCK
    qn = q_nope.reshape(B, nblk, Q_BLOCK, N_HEADS, QK_NOPE_DIM).transpose(1, 0, 2, 3, 4)
    qr = q_rope.reshape(B, nblk, Q_BLOCK, N_HEADS, QK_ROPE_DIM).transpose(1, 0, 2, 3, 4)
    scale = QK_DIM ** -0.5
    key_pos = jnp.arange(S)

    def block(args):
        i, qn_b, qr_b = args
        s = (jnp.einsum('bqhd,bkhd->bhqk', qn_b, k_nope)
             + jnp.einsum('bqhr,bkr->bhqk', qr_b, k_rope))
        s = s.astype(jnp.float32) * scale
        q_pos = i * Q_BLOCK + jnp.arange(Q_BLOCK)
        mask = key_pos[None, :] <= q_pos[:, None]
        s = jnp.where(mask[None, None], s, -jnp.inf)
        p = jax.nn.softmax(s, axis=-1).astype(v.dtype)
        return jnp.einsum('bhqk,bkhd->bqhd', p, v)

    o = lax.map(block, (jnp.arange(nblk), qn, qr))
    o = o.transpose(1, 0, 2, 3, 4).reshape(B, S, N_HEADS * V_DIM)
    return o @ w_o


def _fwd_setup_inputs(seed: int = 0) -> dict:
    key = jax.random.key(seed)
    ks = jax.random.split(key, 32)
    f32 = jnp.float32

    def w(k, shape, fan_in):
        return jax.random.normal(k, shape, f32) * (fan_in ** -0.5)

    def gain(k, shape):
        return 1.0 + 0.02 * jax.random.normal(k, shape, f32)

    D, F, G, Dg = D_MODEL, D_FF, N_POOL_GROUPS, POOL_GROUP
    return {
        "x": jax.random.normal(ks[0], (BATCH, SEQ, D), f32),
        "ffn_pre_norm": gain(ks[1], (DEPTH, D)),
        "ffn_pre_wg": w(ks[2], (DEPTH, D, F), D),
        "ffn_pre_wu": w(ks[3], (DEPTH, D, F), D),
        "ffn_pre_wd": w(ks[4], (DEPTH, F, D), F),
        "mix_norm": gain(ks[5], (DEPTH, D)),
        "ffn_post_norm": gain(ks[6], (DEPTH, D)),
        "ffn_post_wg": w(ks[7], (DEPTH, D, F), D),
        "ffn_post_wu": w(ks[8], (DEPTH, D, F), D),
        "ffn_post_wd": w(ks[9], (DEPTH, F, D), F),
        "pool_w": w(ks[10], (N_A_LAYERS, G, Dg, Dg), Dg),
        "pool_scale": gain(ks[11], (N_A_LAYERS, D)),
        "kv_in_norm": gain(ks[12], (D,)),
        "w_dkv": w(ks[13], (D, KV_LORA_RANK + QK_ROPE_DIM), D),
        "ckv_norm": gain(ks[14], (KV_LORA_RANK,)),
        "w_uk": w(ks[15], (KV_LORA_RANK, N_HEADS, QK_NOPE_DIM), KV_LORA_RANK),
        "w_uv": w(ks[16], (KV_LORA_RANK, N_HEADS, V_DIM), KV_LORA_RANK),
        "q_lora_norm": gain(ks[17], (N_B_LAYERS, Q_LORA_RANK)),
        "w_dq": w(ks[18], (N_B_LAYERS, D, Q_LORA_RANK), D),
        "w_uq": w(ks[19], (N_B_LAYERS, Q_LORA_RANK, N_HEADS, QK_DIM), Q_LORA_RANK),
        "w_o": w(ks[20], (N_B_LAYERS, N_HEADS * V_DIM, D), N_HEADS * V_DIM),
        "final_norm": gain(ks[21], (D,)),
    }


def _fwd_reference(x, ffn_pre_norm, ffn_pre_wg, ffn_pre_wu, ffn_pre_wd, mix_norm,
              ffn_post_norm, ffn_post_wg, ffn_post_wu, ffn_post_wd,
              pool_w, pool_scale, kv_in_norm, w_dkv, ckv_norm, w_uk, w_uv,
              q_lora_norm, w_dq, w_uq, w_o, final_norm):
    S = x.shape[1]
    cos, sin = rope_tables(S)
    h = x
    k_nope = k_rope = v = None
    for l in range(DEPTH):
        h = h + 0.5 * swiglu(rmsnorm(h, ffn_pre_norm[l]), ffn_pre_wg[l], ffn_pre_wu[l], ffn_pre_wd[l])
        u = rmsnorm(h, mix_norm[l])
        if l < N_A_LAYERS:
            h = h + pool_mixer(u, pool_w[l], pool_scale[l])
        else:
            j = l - N_A_LAYERS
            h = h + mla_attention(u, q_lora_norm[j], w_dq[j], w_uq[j], w_o[j],
                                  k_nope, k_rope, v, cos, sin)
        h = h + 0.5 * swiglu(rmsnorm(h, ffn_post_norm[l]), ffn_post_wg[l], ffn_post_wu[l], ffn_post_wd[l])
        if l == N_A_LAYERS - 1:
            k_nope, k_rope, v = mla_shared_kv(h, kv_in_norm, w_dkv, ckv_norm, w_uk, w_uv, cos, sin)
    return rmsnorm(h, final_norm)


import jax as _jax
import jax.numpy as _jnp

TWIN_FORMAT = 'train_step'
FWD_PARAMS = ['x', 'ffn_pre_norm', 'ffn_pre_wg', 'ffn_pre_wu', 'ffn_pre_wd', 'mix_norm', 'ffn_post_norm', 'ffn_post_wg', 'ffn_post_wu', 'ffn_post_wd', 'pool_w', 'pool_scale', 'kv_in_norm', 'w_dkv', 'ckv_norm', 'w_uk', 'w_uv', 'q_lora_norm', 'w_dq', 'w_uq', 'w_o', 'final_norm']
TWIN_WEIGHTS = ['ffn_pre_norm', 'ffn_pre_wg', 'ffn_pre_wu', 'ffn_pre_wd', 'mix_norm', 'ffn_post_norm', 'ffn_post_wg', 'ffn_post_wu', 'ffn_post_wd', 'pool_w', 'pool_scale', 'kv_in_norm', 'w_dkv', 'ckv_norm', 'w_uk', 'w_uv', 'q_lora_norm', 'w_dq', 'w_uq', 'w_o', 'final_norm']
TWIN_DIFF_INPUT = 'x'
TWIN_INPUTS = ['x', 'ffn_pre_norm', 'ffn_pre_wg', 'ffn_pre_wu', 'ffn_pre_wd', 'mix_norm', 'ffn_post_norm', 'ffn_post_wg', 'ffn_post_wu', 'ffn_post_wd', 'pool_w', 'pool_scale', 'kv_in_norm', 'w_dkv', 'ckv_norm', 'w_uk', 'w_uv', 'q_lora_norm', 'w_dq', 'w_uq', 'w_o', 'final_norm', 'loss_target', 'm_ffn_pre_norm', 'm_ffn_pre_wg', 'm_ffn_pre_wu', 'm_ffn_pre_wd', 'm_mix_norm', 'm_ffn_post_norm', 'm_ffn_post_wg', 'm_ffn_post_wu', 'm_ffn_post_wd', 'm_pool_w', 'm_pool_scale', 'm_kv_in_norm', 'm_w_dkv', 'm_ckv_norm', 'm_w_uk', 'm_w_uv', 'm_q_lora_norm', 'm_w_dq', 'm_w_uq', 'm_w_o', 'm_final_norm', 'v_ffn_pre_norm', 'v_ffn_pre_wg', 'v_ffn_pre_wu', 'v_ffn_pre_wd', 'v_mix_norm', 'v_ffn_post_norm', 'v_ffn_post_wg', 'v_ffn_post_wu', 'v_ffn_post_wd', 'v_pool_w', 'v_pool_scale', 'v_kv_in_norm', 'v_w_dkv', 'v_ckv_norm', 'v_w_uk', 'v_w_uv', 'v_q_lora_norm', 'v_w_dq', 'v_w_uq', 'v_w_o', 'v_final_norm']
TWIN_OUTPUTS = ['loss', 'grad_x', 'grad_ffn_pre_norm', 'grad_ffn_pre_wg', 'grad_ffn_pre_wu', 'grad_ffn_pre_wd', 'grad_mix_norm', 'grad_ffn_post_norm', 'grad_ffn_post_wg', 'grad_ffn_post_wu', 'grad_ffn_post_wd', 'grad_pool_w', 'grad_pool_scale', 'grad_kv_in_norm', 'grad_w_dkv', 'grad_ckv_norm', 'grad_w_uk', 'grad_w_uv', 'grad_q_lora_norm', 'grad_w_dq', 'grad_w_uq', 'grad_w_o', 'grad_final_norm', 'delta_ffn_pre_norm', 'delta_ffn_pre_wg', 'delta_ffn_pre_wu', 'delta_ffn_pre_wd', 'delta_mix_norm', 'delta_ffn_post_norm', 'delta_ffn_post_wg', 'delta_ffn_post_wu', 'delta_ffn_post_wd', 'delta_pool_w', 'delta_pool_scale', 'delta_kv_in_norm', 'delta_w_dkv', 'delta_ckv_norm', 'delta_w_uk', 'delta_w_uv', 'delta_q_lora_norm', 'delta_w_dq', 'delta_w_uq', 'delta_w_o', 'delta_final_norm', 'new_m_ffn_pre_norm', 'new_m_ffn_pre_wg', 'new_m_ffn_pre_wu', 'new_m_ffn_pre_wd', 'new_m_mix_norm', 'new_m_ffn_post_norm', 'new_m_ffn_post_wg', 'new_m_ffn_post_wu', 'new_m_ffn_post_wd', 'new_m_pool_w', 'new_m_pool_scale', 'new_m_kv_in_norm', 'new_m_w_dkv', 'new_m_ckv_norm', 'new_m_w_uk', 'new_m_w_uv', 'new_m_q_lora_norm', 'new_m_w_dq', 'new_m_w_uq', 'new_m_w_o', 'new_m_final_norm', 'new_v_ffn_pre_norm', 'new_v_ffn_pre_wg', 'new_v_ffn_pre_wu', 'new_v_ffn_pre_wd', 'new_v_mix_norm', 'new_v_ffn_post_norm', 'new_v_ffn_post_wg', 'new_v_ffn_post_wu', 'new_v_ffn_post_wd', 'new_v_pool_w', 'new_v_pool_scale', 'new_v_kv_in_norm', 'new_v_w_dkv', 'new_v_ckv_norm', 'new_v_w_uk', 'new_v_w_uv', 'new_v_q_lora_norm', 'new_v_w_dq', 'new_v_w_uq', 'new_v_w_o', 'new_v_final_norm']
TWIN_LEAF_KINDS = {'loss': 'loss', 'grad_x': 'grad_x', 'grad_ffn_pre_norm': 'grad_w', 'grad_ffn_pre_wg': 'grad_w', 'grad_ffn_pre_wu': 'grad_w', 'grad_ffn_pre_wd': 'grad_w', 'grad_mix_norm': 'grad_w', 'grad_ffn_post_norm': 'grad_w', 'grad_ffn_post_wg': 'grad_w', 'grad_ffn_post_wu': 'grad_w', 'grad_ffn_post_wd': 'grad_w', 'grad_pool_w': 'grad_w', 'grad_pool_scale': 'grad_w', 'grad_kv_in_norm': 'grad_w', 'grad_w_dkv': 'grad_w', 'grad_ckv_norm': 'grad_w', 'grad_w_uk': 'grad_w', 'grad_w_uv': 'grad_w', 'grad_q_lora_norm': 'grad_w', 'grad_w_dq': 'grad_w', 'grad_w_uq': 'grad_w', 'grad_w_o': 'grad_w', 'grad_final_norm': 'grad_w', 'delta_ffn_pre_norm': 'delta_w', 'delta_ffn_pre_wg': 'delta_w', 'delta_ffn_pre_wu': 'delta_w', 'delta_ffn_pre_wd': 'delta_w', 'delta_mix_norm': 'delta_w', 'delta_ffn_post_norm': 'delta_w', 'delta_ffn_post_wg': 'delta_w', 'delta_ffn_post_wu': 'delta_w', 'delta_ffn_post_wd': 'delta_w', 'delta_pool_w': 'delta_w', 'delta_pool_scale': 'delta_w', 'delta_kv_in_norm': 'delta_w', 'delta_w_dkv': 'delta_w', 'delta_ckv_norm': 'delta_w', 'delta_w_uk': 'delta_w', 'delta_w_uv': 'delta_w', 'delta_q_lora_norm': 'delta_w', 'delta_w_dq': 'delta_w', 'delta_w_uq': 'delta_w', 'delta_w_o': 'delta_w', 'delta_final_norm': 'delta_w', 'new_m_ffn_pre_norm': 'new_m', 'new_m_ffn_pre_wg': 'new_m', 'new_m_ffn_pre_wu': 'new_m', 'new_m_ffn_pre_wd': 'new_m', 'new_m_mix_norm': 'new_m', 'new_m_ffn_post_norm': 'new_m', 'new_m_ffn_post_wg': 'new_m', 'new_m_ffn_post_wu': 'new_m', 'new_m_ffn_post_wd': 'new_m', 'new_m_pool_w': 'new_m', 'new_m_pool_scale': 'new_m', 'new_m_kv_in_norm': 'new_m', 'new_m_w_dkv': 'new_m', 'new_m_ckv_norm': 'new_m', 'new_m_w_uk': 'new_m', 'new_m_w_uv': 'new_m', 'new_m_q_lora_norm': 'new_m', 'new_m_w_dq': 'new_m', 'new_m_w_uq': 'new_m', 'new_m_w_o': 'new_m', 'new_m_final_norm': 'new_m', 'new_v_ffn_pre_norm': 'new_v', 'new_v_ffn_pre_wg': 'new_v', 'new_v_ffn_pre_wu': 'new_v', 'new_v_ffn_pre_wd': 'new_v', 'new_v_mix_norm': 'new_v', 'new_v_ffn_post_norm': 'new_v', 'new_v_ffn_post_wg': 'new_v', 'new_v_ffn_post_wu': 'new_v', 'new_v_ffn_post_wd': 'new_v', 'new_v_pool_w': 'new_v', 'new_v_pool_scale': 'new_v', 'new_v_kv_in_norm': 'new_v', 'new_v_w_dkv': 'new_v', 'new_v_ckv_norm': 'new_v', 'new_v_w_uk': 'new_v', 'new_v_w_uv': 'new_v', 'new_v_q_lora_norm': 'new_v', 'new_v_w_dq': 'new_v', 'new_v_w_uq': 'new_v', 'new_v_w_o': 'new_v', 'new_v_final_norm': 'new_v'}


def _forward(args):
    return _fwd_reference(*[args[k] for k in FWD_PARAMS])


def _output_shape():
    def fwd():
        inp = _fwd_setup_inputs(0)
        return _fwd_reference(*[inp[k] for k in FWD_PARAMS])
    out = _jax.eval_shape(fwd)
    return out.shape, out.dtype

N_MICROBATCH = 1
ADAM_LR = 0.001
ADAM_B1 = 0.9
ADAM_B2 = 0.999
ADAM_EPS = 1e-08
ADAM_WD = 0.01
ADAM_STEP = 10
PER_EXAMPLE_BATCH_AXIS = {'x': 0, 'loss_target': 0}
SHARED_INPUTS = []
_WEIGHT_DTYPES = {'ffn_pre_norm': _jnp.float32, 'ffn_pre_wg': _jnp.float32, 'ffn_pre_wu': _jnp.float32, 'ffn_pre_wd': _jnp.float32, 'mix_norm': _jnp.float32, 'ffn_post_norm': _jnp.float32, 'ffn_post_wg': _jnp.float32, 'ffn_post_wu': _jnp.float32, 'ffn_post_wd': _jnp.float32, 'pool_w': _jnp.float32, 'pool_scale': _jnp.float32, 'kv_in_norm': _jnp.float32, 'w_dkv': _jnp.float32, 'ckv_norm': _jnp.float32, 'w_uk': _jnp.float32, 'w_uv': _jnp.float32, 'q_lora_norm': _jnp.float32, 'w_dq': _jnp.float32, 'w_uq': _jnp.float32, 'w_o': _jnp.float32, 'final_norm': _jnp.float32}
MOMENT_SCALE = {'ffn_pre_norm': 1.204832e-01, 'ffn_pre_wg': 5.182516e-02, 'ffn_pre_wu': 5.013203e-02, 'ffn_pre_wd': 8.311666e-02, 'mix_norm': 1.694335e-01, 'ffn_post_norm': 9.843490e-02, 'ffn_post_wg': 4.253869e-02, 'ffn_post_wu': 4.120031e-02, 'ffn_post_wd': 6.834816e-02, 'pool_w': 2.436295e-01, 'pool_scale': 7.137278e-01, 'kv_in_norm': 7.939054e-02, 'w_dkv': 1.805405e-01, 'ckv_norm': 2.248766e-01, 'w_uk': 3.306919e-02, 'w_uv': 4.358515e-02, 'q_lora_norm': 7.543997e-02, 'w_dq': 7.857981e-02, 'w_uq': 2.262300e-02, 'w_o': 4.362519e-02, 'final_norm': 1.291279e+02}


def _to_microbatches(a, axis):
    t = _jnp.moveaxis(a, axis, 0)
    t = t.reshape((N_MICROBATCH, t.shape[0] // N_MICROBATCH) + t.shape[1:])
    return _jnp.moveaxis(t, 1, axis + 1)


def setup_inputs(seed: int = 0) -> dict:
    inp = _fwd_setup_inputs(seed)
    key = _jax.random.fold_in(_jax.random.key(seed), 7919)
    shape, _ = _output_shape()
    out = dict(inp)
    out["loss_target"] = _jax.random.normal(_jax.random.fold_in(key, 0), shape, _jnp.float32)
    for i, name in enumerate(TWIN_WEIGHTS):
        w = inp[name].astype(_jnp.float32)
        if MOMENT_SCALE is None:
            s = _jnp.sqrt(_jnp.mean(_jnp.square(w)) + 1e-30)
        else:
            s = MOMENT_SCALE[name]
        km, kv = _jax.random.split(_jax.random.fold_in(key, i + 1))
        out[name] = w
        out["m_" + name] = s * _jax.random.normal(km, w.shape, _jnp.float32)
        out["v_" + name] = (s * s) * _jax.random.uniform(kv, w.shape, _jnp.float32, 0.5, 1.5)
    if N_MICROBATCH > 1:
        for name, axis in PER_EXAMPLE_BATCH_AXIS.items():
            out[name] = _to_microbatches(out[name], axis)
    return {'x': out['x'], 'ffn_pre_norm': out['ffn_pre_norm'], 'ffn_pre_wg': out['ffn_pre_wg'], 'ffn_pre_wu': out['ffn_pre_wu'], 'ffn_pre_wd': out['ffn_pre_wd'], 'mix_norm': out['mix_norm'], 'ffn_post_norm': out['ffn_post_norm'], 'ffn_post_wg': out['ffn_post_wg'], 'ffn_post_wu': out['ffn_post_wu'], 'ffn_post_wd': out['ffn_post_wd'], 'pool_w': out['pool_w'], 'pool_scale': out['pool_scale'], 'kv_in_norm': out['kv_in_norm'], 'w_dkv': out['w_dkv'], 'ckv_norm': out['ckv_norm'], 'w_uk': out['w_uk'], 'w_uv': out['w_uv'], 'q_lora_norm': out['q_lora_norm'], 'w_dq': out['w_dq'], 'w_uq': out['w_uq'], 'w_o': out['w_o'], 'final_norm': out['final_norm'], 'loss_target': out['loss_target'], 'm_ffn_pre_norm': out['m_ffn_pre_norm'], 'm_ffn_pre_wg': out['m_ffn_pre_wg'], 'm_ffn_pre_wu': out['m_ffn_pre_wu'], 'm_ffn_pre_wd': out['m_ffn_pre_wd'], 'm_mix_norm': out['m_mix_norm'], 'm_ffn_post_norm': out['m_ffn_post_norm'], 'm_ffn_post_wg': out['m_ffn_post_wg'], 'm_ffn_post_wu': out['m_ffn_post_wu'], 'm_ffn_post_wd': out['m_ffn_post_wd'], 'm_pool_w': out['m_pool_w'], 'm_pool_scale': out['m_pool_scale'], 'm_kv_in_norm': out['m_kv_in_norm'], 'm_w_dkv': out['m_w_dkv'], 'm_ckv_norm': out['m_ckv_norm'], 'm_w_uk': out['m_w_uk'], 'm_w_uv': out['m_w_uv'], 'm_q_lora_norm': out['m_q_lora_norm'], 'm_w_dq': out['m_w_dq'], 'm_w_uq': out['m_w_uq'], 'm_w_o': out['m_w_o'], 'm_final_norm': out['m_final_norm'], 'v_ffn_pre_norm': out['v_ffn_pre_norm'], 'v_ffn_pre_wg': out['v_ffn_pre_wg'], 'v_ffn_pre_wu': out['v_ffn_pre_wu'], 'v_ffn_pre_wd': out['v_ffn_pre_wd'], 'v_mix_norm': out['v_mix_norm'], 'v_ffn_post_norm': out['v_ffn_post_norm'], 'v_ffn_post_wg': out['v_ffn_post_wg'], 'v_ffn_post_wu': out['v_ffn_post_wu'], 'v_ffn_post_wd': out['v_ffn_post_wd'], 'v_pool_w': out['v_pool_w'], 'v_pool_scale': out['v_pool_scale'], 'v_kv_in_norm': out['v_kv_in_norm'], 'v_w_dkv': out['v_w_dkv'], 'v_ckv_norm': out['v_ckv_norm'], 'v_w_uk': out['v_w_uk'], 'v_w_uv': out['v_w_uv'], 'v_q_lora_norm': out['v_q_lora_norm'], 'v_w_dq': out['v_w_dq'], 'v_w_uq': out['v_w_uq'], 'v_w_o': out['v_w_o'], 'v_final_norm': out['v_final_norm']}


def _loss(weights, diff, rest, loss_target):
    with _jax.named_scope("forward"):
        args = {**rest, TWIN_DIFF_INPUT: diff, **{k: w.astype(_WEIGHT_DTYPES[k]) for k, w in weights.items()}}
        y = _forward(args)
    with _jax.named_scope("loss_head"):
        err = _jnp.square(y.astype(_jnp.float32) - loss_target)
        return 0.5 * _jnp.sum(_jnp.mean(err, axis=-1)) if err.ndim else 0.5 * err


def _adamw(w, g, m, v):
    m = ADAM_B1 * m + (1.0 - ADAM_B1) * g
    v = ADAM_B2 * v + (1.0 - ADAM_B2) * _jnp.square(g)
    m_hat = m / (1.0 - ADAM_B1 ** ADAM_STEP)
    v_hat = v / (1.0 - ADAM_B2 ** ADAM_STEP)
    delta = -ADAM_LR * (m_hat / (_jnp.sqrt(v_hat) + ADAM_EPS) + ADAM_WD * w)
    return delta, m, v


def reference(x, ffn_pre_norm, ffn_pre_wg, ffn_pre_wu, ffn_pre_wd, mix_norm, ffn_post_norm, ffn_post_wg, ffn_post_wu, ffn_post_wd, pool_w, pool_scale, kv_in_norm, w_dkv, ckv_norm, w_uk, w_uv, q_lora_norm, w_dq, w_uq, w_o, final_norm, loss_target, m_ffn_pre_norm, m_ffn_pre_wg, m_ffn_pre_wu, m_ffn_pre_wd, m_mix_norm, m_ffn_post_norm, m_ffn_post_wg, m_ffn_post_wu, m_ffn_post_wd, m_pool_w, m_pool_scale, m_kv_in_norm, m_w_dkv, m_ckv_norm, m_w_uk, m_w_uv, m_q_lora_norm, m_w_dq, m_w_uq, m_w_o, m_final_norm, v_ffn_pre_norm, v_ffn_pre_wg, v_ffn_pre_wu, v_ffn_pre_wd, v_mix_norm, v_ffn_post_norm, v_ffn_post_wg, v_ffn_post_wu, v_ffn_post_wd, v_pool_w, v_pool_scale, v_kv_in_norm, v_w_dkv, v_ckv_norm, v_w_uk, v_w_uv, v_q_lora_norm, v_w_dq, v_w_uq, v_w_o, v_final_norm):
    given = dict(x=x, ffn_pre_norm=ffn_pre_norm, ffn_pre_wg=ffn_pre_wg, ffn_pre_wu=ffn_pre_wu, ffn_pre_wd=ffn_pre_wd, mix_norm=mix_norm, ffn_post_norm=ffn_post_norm, ffn_post_wg=ffn_post_wg, ffn_post_wu=ffn_post_wu, ffn_post_wd=ffn_post_wd, pool_w=pool_w, pool_scale=pool_scale, kv_in_norm=kv_in_norm, w_dkv=w_dkv, ckv_norm=ckv_norm, w_uk=w_uk, w_uv=w_uv, q_lora_norm=q_lora_norm, w_dq=w_dq, w_uq=w_uq, w_o=w_o, final_norm=final_norm, loss_target=loss_target, m_ffn_pre_norm=m_ffn_pre_norm, m_ffn_pre_wg=m_ffn_pre_wg, m_ffn_pre_wu=m_ffn_pre_wu, m_ffn_pre_wd=m_ffn_pre_wd, m_mix_norm=m_mix_norm, m_ffn_post_norm=m_ffn_post_norm, m_ffn_post_wg=m_ffn_post_wg, m_ffn_post_wu=m_ffn_post_wu, m_ffn_post_wd=m_ffn_post_wd, m_pool_w=m_pool_w, m_pool_scale=m_pool_scale, m_kv_in_norm=m_kv_in_norm, m_w_dkv=m_w_dkv, m_ckv_norm=m_ckv_norm, m_w_uk=m_w_uk, m_w_uv=m_w_uv, m_q_lora_norm=m_q_lora_norm, m_w_dq=m_w_dq, m_w_uq=m_w_uq, m_w_o=m_w_o, m_final_norm=m_final_norm, v_ffn_pre_norm=v_ffn_pre_norm, v_ffn_pre_wg=v_ffn_pre_wg, v_ffn_pre_wu=v_ffn_pre_wu, v_ffn_pre_wd=v_ffn_pre_wd, v_mix_norm=v_mix_norm, v_ffn_post_norm=v_ffn_post_norm, v_ffn_post_wg=v_ffn_post_wg, v_ffn_post_wu=v_ffn_post_wu, v_ffn_post_wd=v_ffn_post_wd, v_pool_w=v_pool_w, v_pool_scale=v_pool_scale, v_kv_in_norm=v_kv_in_norm, v_w_dkv=v_w_dkv, v_ckv_norm=v_ckv_norm, v_w_uk=v_w_uk, v_w_uv=v_w_uv, v_q_lora_norm=v_q_lora_norm, v_w_dq=v_w_dq, v_w_uq=v_w_uq, v_w_o=v_w_o, v_final_norm=v_final_norm)
    weights = {n: given[n] for n in TWIN_WEIGHTS}
    shared = {n: given[n] for n in SHARED_INPUTS}
    per_example = {n: given[n] for n in ['x']}
    grad_fn = _jax.value_and_grad(_loss, argnums=(0, 1))

    def one_microbatch(ex, loss_target):
        ex = dict(ex)
        diff = ex.pop(TWIN_DIFF_INPUT)
        return grad_fn(weights, diff, {**shared, **ex}, loss_target)

    if N_MICROBATCH == 1:
        loss, (grad_w, grad_x) = one_microbatch(per_example, given["loss_target"])
    else:
        def body(carry, xs):
            loss_sum, grad_sum = carry
            l_k, (gw_k, gx_k) = one_microbatch(xs[0], xs[1])
            with _jax.named_scope("update"):
                return (loss_sum + l_k, _jax.tree.map(_jnp.add, grad_sum, gw_k)), gx_k

        init = (_jnp.zeros((), _jnp.float32), _jax.tree.map(_jnp.zeros_like, weights))
        (loss, grad_w), grad_x = _jax.lax.scan(body, init, (per_example, given["loss_target"]))
    with _jax.named_scope("update"):
        delta_w, new_m, new_v = {}, {}, {}
        for n in TWIN_WEIGHTS:
            delta_w[n], new_m[n], new_v[n] = _adamw(weights[n], grad_w[n], given["m_" + n], given["v_" + n])
    return (loss, grad_x, *[grad_w[n] for n in TWIN_WEIGHTS], *[delta_w[n] for n in TWIN_WEIGHTS],
            *[new_m[n] for n in TWIN_WEIGHTS], *[new_v[n] for n in TWIN_WEIGHTS])
```

```python
import functools

import numpy as np
import jax
import jax.numpy as jnp
from jax import lax
from jax.experimental import pallas as pl
from jax.experimental.pallas import tpu as pltpu

F32, BF16 = jnp.float32, jnp.bfloat16
N_DEV = 8
RMS_EPS = 1e-6
N_HEADS = 16
D_NOPE, D_ROPE, D_V = 128, 64, 128
D_QK = D_NOPE + D_ROPE
D_HEAD_PAD = 256
ROPE_THETA = 10000.0
POOL_WINDOWS = (2, 4, 8, 16)
POOL_HALO = 16
ADAM_LR, ADAM_B1, ADAM_B2, ADAM_EPS, ADAM_WD, ADAM_STEP = 0.001, 0.9, 0.999, 1e-08, 0.01, 10
NEG_BIG = -1e30
V7X_VMEM_LIMIT = 56 * 1024 * 1024
TOKEN_TILE = 512
ATTN_TILE = 512
FFN_TILE = 256
MISC_ROWS = 864
REP_COLS = 1024


def _cparams(*sem):
    return pltpu.CompilerParams(dimension_semantics=sem, vmem_limit_bytes=V7X_VMEM_LIMIT)


def _dot(a, b):
    return lax.dot_general(a, b, (((1,), (0,)), ((), ())), preferred_element_type=F32)


def _dot_nt(a, b):
    return lax.dot_general(a, b, (((1,), (1,)), ((), ())), preferred_element_type=F32)


def _dot_tn(a, b):
    return lax.dot_general(a, b, (((0,), (0,)), ((), ())), preferred_element_type=F32)


def _rms_fwd(x, g):
    r = lax.rsqrt(jnp.mean(x * x, axis=-1, keepdims=True) + RMS_EPS)
    return (x * r) * g


def _rms_bwd(du, x, g):
    r = lax.rsqrt(jnp.mean(x * x, axis=-1, keepdims=True) + RMS_EPS)
    xh = x * r
    dg = jnp.sum(du * xh, axis=0, keepdims=True)
    dxh = du * g
    dx = r * (dxh - xh * jnp.mean(dxh * xh, axis=-1, keepdims=True))
    return dx, dg


def _sigmoid(x):
    return 1.0 / (1.0 + jnp.exp(-x))


def _split_bf16(x):
    hi = x.astype(BF16)
    lo = (x - hi.astype(F32)).astype(BF16)
    return hi, lo


def _full(shape):
    return pl.BlockSpec(shape, lambda *_: (0,) * len(shape))


def _row_tile(rows, cap):
    for t in range(min(cap, rows) // 8 * 8, 0, -8):
        if rows % t == 0:
            return t
    return rows


def _peers():
    x, y, c = lax.axis_index("x"), lax.axis_index("y"), lax.axis_index("c")
    out = []
    for k in range(1, N_DEV):
        px = 1 - x if (k >> 2) & 1 else x
        py = 1 - y if (k >> 1) & 1 else y
        pc = 1 - c if k & 1 else c
        out.append(((px, py, pc), 4 * px + 2 * py + pc))
    return 4 * x + 2 * y + c, out


def _exchange(arrays, src_of, dst_of, out_shapes, name):
    n = len(arrays)

    def body(*refs):
        ins, outs = refs[:n], refs[n:2 * n]
        send_sems, recv_sems, loc_sems = refs[2 * n:]
        me, peers = _peers()
        own = [pltpu.make_async_copy(src_of(j, ins[j], me), dst_of(j, outs[j], me), loc_sems.at[j])
               for j in range(n)]
        for cp in own:
            cp.start()
        sends = []
        for k, (peer, pidx) in enumerate(peers):
            for j in range(n):
                sends.append(pltpu.make_async_remote_copy(
                    src_ref=src_of(j, ins[j], pidx), dst_ref=dst_of(j, outs[j], me),
                    send_sem=send_sems.at[j, k], recv_sem=recv_sems.at[j, k],
                    device_id=peer, device_id_type=pl.DeviceIdType.MESH))
        for cp in sends:
            cp.start()
        for k, (peer, pidx) in enumerate(peers):
            for j in range(n):
                pltpu.make_async_remote_copy(
                    src_ref=src_of(j, ins[j], pidx), dst_ref=dst_of(j, outs[j], pidx),
                    send_sem=send_sems.at[j, k], recv_sem=recv_sems.at[j, k],
                    device_id=peer, device_id_type=pl.DeviceIdType.MESH).wait_recv()
        for cp in sends:
            cp.wait_send()
        for cp in own:
            cp.wait()

    any_spec = pl.BlockSpec(memory_space=pl.ANY)
    return pl.pallas_call(
        body, name=name,
        out_shape=[jax.ShapeDtypeStruct(s, a.dtype) for s, a in zip(out_shapes, arrays)],
        in_specs=[any_spec] * n, out_specs=[any_spec] * n,
        scratch_shapes=[pltpu.SemaphoreType.DMA((n, N_DEV - 1)), pltpu.SemaphoreType.DMA((n, N_DEV - 1)),
                        pltpu.SemaphoreType.DMA((n,))],
    )(*arrays)


def _all_gather(ffn_local, misc_local):
    w, r, d = ffn_local.shape

    def src_of(j, ref, idx):
        return ref

    def dst_of(j, ref, idx):
        return ref.at[:, idx] if j == 0 else ref.at[idx]

    return _exchange([ffn_local, misc_local], src_of, dst_of,
                     [(w, N_DEV, r, d), (N_DEV,) + misc_local.shape], "comm_all_gather")


def _grad_exchange(ffn_g, misc_g, rep_g):
    w, _, r, d = ffn_g.shape

    def src_of(j, ref, idx):
        return (ref.at[:, idx], ref.at[idx], ref)[j]

    def dst_of(j, ref, idx):
        return ref.at[idx]

    return _exchange([ffn_g, misc_g, rep_g], src_of, dst_of,
                     [(N_DEV, w, r, d), misc_g.shape, (N_DEV,) + rep_g.shape], "comm_grad_exchange")


def _ffn_fwd(h, g, wall, e):
    T, D = h.shape
    F = wall.shape[2]
    tm, tf = min(TOKEN_TILE, T), FFN_TILE
    nf = F // tf

    def body(h_ref, g_ref, wg_ref, wu_ref, wd_ref, ho_ref, gate_ref, up_ref, u_sc, acc_sc):
        f = pl.program_id(1)

        @pl.when(f == 0)
        def _():
            u_sc[...] = _rms_fwd(h_ref[...], g_ref[...]).astype(BF16)
            acc_sc[...] = jnp.zeros_like(acc_sc)

        u = u_sc[...]
        gate = _dot_nt(u, wg_ref[...])
        up = _dot_nt(u, wu_ref[...])
        gate_ref[...] = gate.astype(BF16)
        up_ref[...] = up.astype(BF16)
        act = (gate * _sigmoid(gate) * up).astype(BF16)
        acc_sc[...] += _dot(act, wd_ref[...])

        @pl.when(f == nf - 1)
        def _():
            ho_ref[...] = h_ref[...] + 0.5 * acc_sc[...]

    def wspec(kind):
        return pl.BlockSpec((None, None, tf, D), lambda i, f: (e, kind, f, 0))

    return pl.pallas_call(
        body, name="ffn_fwd", grid=(T // tm, nf),
        in_specs=[pl.BlockSpec((tm, D), lambda i, f: (i, 0)), _full((1, D)), wspec(0), wspec(1), wspec(2)],
        out_specs=[pl.BlockSpec((tm, D), lambda i, f: (i, 0)),
                   pl.BlockSpec((tm, tf), lambda i, f: (i, f)), pl.BlockSpec((tm, tf), lambda i, f: (i, f))],
        out_shape=[jax.ShapeDtypeStruct((T, D), F32), jax.ShapeDtypeStruct((T, F), BF16),
                   jax.ShapeDtypeStruct((T, F), BF16)],
        scratch_shapes=[pltpu.VMEM((tm, D), BF16), pltpu.VMEM((tm, D), F32)],
        compiler_params=_cparams("parallel", "arbitrary"),
    )(h, g, wall, wall, wall)


def _ffn_bwd_dx(dho, h, g, gate, up, wall, e):
    T, D = h.shape
    F = wall.shape[2]
    tm, tf = min(TOKEN_TILE, T), FFN_TILE
    nf = F // tf

    def body(dho_ref, h_ref, g_ref, gate_ref, up_ref, wg_ref, wu_ref, wd_ref,
             dhi_ref, dgate_ref, dup_ref, u_ref, dy_ref, dg_ref, dy_sc, acc_sc):
        i, f = pl.program_id(0), pl.program_id(1)

        @pl.when(f == 0)
        def _():
            dy = (0.5 * dho_ref[...]).astype(BF16)
            dy_sc[...] = dy
            dy_ref[...] = dy
            u_ref[...] = _rms_fwd(h_ref[...], g_ref[...]).astype(BF16)
            acc_sc[...] = jnp.zeros_like(acc_sc)

        dact = _dot_nt(dy_sc[...], wd_ref[...])
        gt = gate_ref[...].astype(F32)
        sig = _sigmoid(gt)
        dup = (dact * (gt * sig)).astype(BF16)
        dgate = (dact * up_ref[...].astype(F32) * (sig * (1.0 + gt * (1.0 - sig)))).astype(BF16)
        dup_ref[...] = dup
        dgate_ref[...] = dgate
        acc_sc[...] += _dot(dgate, wg_ref[...]) + _dot(dup, wu_ref[...])

        @pl.when(f == nf - 1)
        def _():
            dx, dg = _rms_bwd(acc_sc[...], h_ref[...], g_ref[...])
            dhi_ref[...] = dho_ref[...] + dx

            @pl.when(i == 0)
            def _():
                dg_ref[...] = jnp.zeros_like(dg_ref)

            dg_ref[...] += dg

    def wspec(kind):
        return pl.BlockSpec((None, None, tf, D), lambda i, f: (e, kind, f, 0))

    row = pl.BlockSpec((tm, D), lambda i, f: (i, 0))
    blk = pl.BlockSpec((tm, tf), lambda i, f: (i, f))
    return pl.pallas_call(
        body, name="ffn_bwd_dx", grid=(T // tm, nf),
        in_specs=[row, row, _full((1, D)), blk, blk, wspec(0), wspec(1), wspec(2)],
        out_specs=[row, blk, blk, row, row, _full((1, D))],
        out_shape=[jax.ShapeDtypeStruct((T, D), F32), jax.ShapeDtypeStruct((T, F), BF16),
                   jax.ShapeDtypeStruct((T, F), BF16), jax.ShapeDtypeStruct((T, D), BF16),
                   jax.ShapeDtypeStruct((T, D), BF16), jax.ShapeDtypeStruct((1, D), F32)],
        scratch_shapes=[pltpu.VMEM((tm, D), BF16), pltpu.VMEM((tm, D), F32)],
        compiler_params=_cparams("arbitrary", "arbitrary"),
    )(dho, h, g, gate, up, wall, wall, wall)


def _ffn_bwd_dw(dgate, dup, gate, up, u, dy, gbuf, e):
    T, F = gate.shape
    D = u.shape[1]
    tfw = F // 2
    tk = min(256, T)
    nk = T // tk

    def body(dgate_ref, dup_ref, gate_ref, up_ref, u_ref, dy_ref, gbuf_ref, out_ref, acc_sc):
        k = pl.program_id(1)

        @pl.when(k == 0)
        def _():
            acc_sc[...] = jnp.zeros_like(acc_sc)

        uu = u_ref[...]
        gt = gate_ref[...].astype(F32)
        act = (gt * _sigmoid(gt) * up_ref[...].astype(F32)).astype(BF16)
        acc_sc[0] += _dot_tn(dgate_ref[...], uu)
        acc_sc[1] += _dot_tn(dup_ref[...], uu)
        acc_sc[2] += _dot_tn(act, dy_ref[...])

        @pl.when(k == nk - 1)
        def _():
            out_ref[...] = acc_sc[...].astype(BF16)

    blk = pl.BlockSpec((tk, tfw), lambda j, k: (k, j))
    row = pl.BlockSpec((tk, D), lambda j, k: (k, 0))
    return pl.pallas_call(
        body, name="ffn_bwd_dw", grid=(F // tfw, nk),
        in_specs=[blk, blk, blk, blk, row, row, pl.BlockSpec(memory_space=pl.ANY)],
        out_specs=pl.BlockSpec((None, 3, tfw, D), lambda j, k: (e, 0, j, 0)),
        out_shape=jax.ShapeDtypeStruct(gbuf.shape, BF16),
        scratch_shapes=[pltpu.VMEM((3, tfw, D), F32)],
        input_output_aliases={6: 0},
        compiler_params=_cparams("parallel", "arbitrary"),
    )(dgate, dup, gate, up, u, dy, gbuf)


def _mm_rows(a, b, *, nt, out_dtype, norm_g=None, res=None, name):
    T, K = a.shape
    N = b.shape[0] if nt else b.shape[1]
    tm = min(TOKEN_TILE, T)
    has_g, has_r = norm_g is not None, res is not None

    def body(*refs):
        a_ref, b_ref = refs[0], refs[1]
        o_ref = refs[-1]
        x = a_ref[...]
        if has_g:
            x = _rms_fwd(x, refs[2][...])
        x = x.astype(BF16)
        acc = _dot_nt(x, b_ref[...]) if nt else _dot(x, b_ref[...])
        if has_r:
            acc = refs[2 + has_g][...] + acc
        o_ref[...] = acc.astype(out_dtype)

    ins, specs = [a, b], [pl.BlockSpec((tm, K), lambda i: (i, 0)), _full(b.shape)]
    if has_g:
        ins.append(norm_g)
        specs.append(_full((1, K)))
    if has_r:
        ins.append(res)
        specs.append(pl.BlockSpec((tm, N), lambda i: (i, 0)))
    return pl.pallas_call(
        body, name=name, grid=(T // tm,), in_specs=specs,
        out_specs=pl.BlockSpec((tm, N), lambda i: (i, 0)),
        out_shape=jax.ShapeDtypeStruct((T, N), out_dtype),
        compiler_params=_cparams("parallel"),
    )(*ins)


def _mm_tn(a, b, *, norm_g=None, name):
    T, M = a.shape
    N = b.shape[1]
    tk = min(TOKEN_TILE, T)
    has_g = norm_g is not None

    def body(*refs):
        a_ref, b_ref, o_ref = refs[0], refs[1], refs[-1]

        @pl.when(pl.program_id(0) == 0)
        def _():
            o_ref[...] = jnp.zeros_like(o_ref)

        x = a_ref[...]
        if has_g:
            x = _rms_fwd(x, refs[2][...])
        o_ref[...] += _dot_tn(x.astype(BF16), b_ref[...].astype(BF16))

    ins = [a, b]
    specs = [pl.BlockSpec((tk, M), lambda k: (k, 0)), pl.BlockSpec((tk, N), lambda k: (k, 0))]
    if has_g:
        ins.append(norm_g)
        specs.append(_full((1, M)))
    return pl.pallas_call(
        body, name=name, grid=(T // tk,), in_specs=specs, out_specs=_full((M, N)),
        out_shape=jax.ShapeDtypeStruct((M, N), F32),
        compiler_params=_cparams("arbitrary"),
    )(*ins)


def _proj_bwd(dz, w, h, g, dh, name):
    T, D = h.shape
    N = w.shape[1]
    tm = min(TOKEN_TILE, T)

    def body(dz_ref, w_ref, h_ref, g_ref, dh_ref, o_ref, dg_ref):
        du = _dot_nt(dz_ref[...].astype(BF16), w_ref[...])
        dx, dg = _rms_bwd(du, h_ref[...], g_ref[...])
        o_ref[...] = dh_ref[...] + dx

        @pl.when(pl.program_id(0) == 0)
        def _():
            dg_ref[...] = jnp.zeros_like(dg_ref)

        dg_ref[...] += dg

    row = pl.BlockSpec((tm, D), lambda i: (i, 0))
    return pl.pallas_call(
        body, name=name, grid=(T // tm,),
        in_specs=[pl.BlockSpec((tm, N), lambda i: (i, 0)), _full((D, N)), row, _full((1, D)), row],
        out_specs=[row, _full((1, D))],
        out_shape=[jax.ShapeDtypeStruct((T, D), F32), jax.ShapeDtypeStruct((1, D), F32)],
        compiler_params=_cparams("arbitrary"),
    )(dz, w, h, g, dh)


def _pool_bands(tm):
    r = np.arange(tm)[:, None]
    c = np.arange(tm)[None, :]
    j = np.arange(POOL_HALO)[None, :]
    main, halo, main_t, halo_t = [], [], [], []
    for w in POOL_WINDOWS:
        main.append(((r - c >= 0) & (r - c < w)) / w)
        halo.append((r + POOL_HALO - j < w) / w)
        main_t.append(((c - r >= 0) & (c - r < w)) / w)
        halo_t.append((tm + j - r < w) / w)
    return tuple(jnp.asarray(np.stack(m), BF16) for m in (main, halo, main_t, halo_t))


def _pool_count_scale(i, tm, w):
    t = i * tm + lax.broadcasted_iota(jnp.int32, (tm, 1), 0)
    return w / jnp.minimum(t + 1, w).astype(F32)


def _pool_fwd(h, g, wp, scale):
    T, D = h.shape
    G, dg = len(POOL_WINDOWS), D // len(POOL_WINDOWS)
    tm = min(TOKEN_TILE, T)
    hb = tm // POOL_HALO
    bm, bh, _, _ = _pool_bands(tm)

    def body(h_ref, hh_ref, g_ref, wp_ref, sc_ref, bm_ref, bh_ref, ho_ref, y_ref):
        i = pl.program_id(0)
        x = h_ref[...]
        u = _rms_fwd(x, g_ref[...])
        uh = _rms_fwd(hh_ref[...], g_ref[...]) * (i > 0).astype(F32)
        for gi, w in enumerate(POOL_WINDOWS):
            cols = slice(gi * dg, (gi + 1) * dg)
            ug = u[:, cols]
            hi, lo = _split_bf16(ug)
            hhi, hlo = _split_bf16(uh[:, cols])
            s = (_dot(bm_ref[gi], hi) + _dot(bm_ref[gi], lo)
                 + _dot(bh_ref[gi], hhi) + _dot(bh_ref[gi], hlo))
            y = (s * _pool_count_scale(i, tm, w) - ug).astype(BF16)
            y_ref[:, cols] = y
            ho_ref[:, cols] = x[:, cols] + _dot(y, wp_ref[gi]) * sc_ref[:, cols]

    row = pl.BlockSpec((tm, D), lambda i: (i, 0))
    return pl.pallas_call(
        body, name="pool_fwd", grid=(T // tm,),
        in_specs=[row, pl.BlockSpec((POOL_HALO, D), lambda i: (jnp.maximum(i * hb - 1, 0), 0)),
                  _full((1, D)), _full((G, dg, dg)), _full((1, D)),
                  _full((G, tm, tm)), _full((G, tm, POOL_HALO))],
        out_specs=[row, row],
        out_shape=[jax.ShapeDtypeStruct((T, D), F32), jax.ShapeDtypeStruct((T, D), BF16)],
        compiler_params=_cparams("parallel"),
    )(h, h, g, wp, scale, bm, bh)


def _pool_bwd(dh, h, g, y, wp, scale):
    T, D = h.shape
    G, dg = len(POOL_WINDOWS), D // len(POOL_WINDOWS)
    tm = min(TOKEN_TILE, T)
    hb = tm // POOL_HALO
    nt = T // tm
    _, _, bmt, bht = _pool_bands(tm)

    def body(dh_ref, dhn_ref, h_ref, g_ref, y_ref, wp_ref, sc_ref, bmt_ref, bht_ref,
             o_ref, dg_ref, dwp_ref, dsc_ref, du_sc):
        i = pl.program_id(0)

        @pl.when(i == 0)
        def _():
            dg_ref[...] = jnp.zeros_like(dg_ref)
            dwp_ref[...] = jnp.zeros_like(dwp_ref)
            dsc_ref[...] = jnp.zeros_like(dsc_ref)

        dho = dh_ref[...]
        dz = dho * sc_ref[...]
        dzn = dhn_ref[...] * sc_ref[...] * (i < nt - 1).astype(F32)
        for gi, w in enumerate(POOL_WINDOWS):
            cols = slice(gi * dg, (gi + 1) * dg)
            yg = y_ref[:, cols]
            dzg = dz[:, cols].astype(BF16)
            dsc_ref[:, cols] += jnp.sum(dho[:, cols] * _dot(yg, wp_ref[gi]), axis=0, keepdims=True)
            dwp_ref[gi] += _dot_tn(yg, dzg)
            dy = _dot_nt(dzg, wp_ref[gi])
            dyn = _dot_nt(dzn[:, cols].astype(BF16), wp_ref[gi])
            hi, lo = _split_bf16(dy * _pool_count_scale(i, tm, w))
            nhi, nlo = _split_bf16(dyn)
            du_sc[:, cols] = (_dot(bmt_ref[gi], hi) + _dot(bmt_ref[gi], lo)
                              + _dot(bht_ref[gi], nhi) + _dot(bht_ref[gi], nlo) - dy)
        dx, dgp = _rms_bwd(du_sc[...], h_ref[...], g_ref[...])
        o_ref[...] = dho + dx
        dg_ref[...] += dgp

    row = pl.BlockSpec((tm, D), lambda i: (i, 0))
    return pl.pallas_call(
        body, name="pool_bwd", grid=(nt,),
        in_specs=[row, pl.BlockSpec((POOL_HALO, D), lambda i: (jnp.minimum((i + 1) * hb, T // POOL_HALO - 1), 0)),
                  row, _full((1, D)), row, _full((G, dg, dg)), _full((1, D)),
                  _full((G, tm, tm)), _full((G, tm, POOL_HALO))],
        out_specs=[row, _full((1, D)), _full((G, dg, dg)), _full((1, D))],
        out_shape=[jax.ShapeDtypeStruct((T, D), F32), jax.ShapeDtypeStruct((1, D), F32),
                   jax.ShapeDtypeStruct((G, dg, dg), F32), jax.ShapeDtypeStruct((1, D), F32)],
        scratch_shapes=[pltpu.VMEM((tm, D), F32)],
        compiler_params=_cparams("arbitrary"),
    )(dh, dh, h, g, y, wp, scale, bmt, bht)


def _rope_tables(T):
    pos = jnp.arange(T, dtype=F32)
    inv_freq = ROPE_THETA ** (-jnp.arange(0, D_ROPE, 2, dtype=F32) / D_ROPE)
    ang = pos[:, None] * inv_freq[None, :]
    cos2 = jnp.tile(jnp.cos(ang), (1, 2))
    sin2 = jnp.tile(jnp.sin(ang), (1, 2))
    pad = jnp.zeros((T, D_HEAD_PAD - D_QK), F32)
    ca_q = jnp.concatenate([jnp.ones((T, D_NOPE), F32), cos2, pad], axis=1)
    ca_k = jnp.concatenate([jnp.zeros((T, D_NOPE), F32), cos2, pad], axis=1)
    sb = jnp.concatenate([jnp.zeros((T, D_NOPE), F32), sin2, pad], axis=1)
    return ca_q, ca_k, sb


def _rope_weight_pair(w):
    half = D_ROPE // 2
    z_pad = jnp.zeros(w.shape[:-1] + (D_HEAD_PAD - D_QK,), w.dtype)
    z_nope = jnp.zeros(w.shape[:-1] + (D_NOPE,), w.dtype)
    wa = jnp.concatenate([w, z_pad], axis=-1)
    wb = jnp.concatenate([z_nope, -w[..., D_NOPE + half:], w[..., D_NOPE:D_NOPE + half], z_pad], axis=-1)
    return wa, wb


def _rope_weight_pair_grad(dwa, dwb):
    half = D_ROPE // 2
    d1 = dwa[..., D_NOPE:D_NOPE + half] + dwb[..., D_NOPE + half:D_QK]
    d2 = dwa[..., D_NOPE + half:D_QK] - dwb[..., D_NOPE:D_NOPE + half]
    return jnp.concatenate([dwa[..., :D_NOPE], d1, d2], axis=-1)


def _q_proj(cq, qg, wa, wb, ca, sb):
    T, R = cq.shape
    tm = min(TOKEN_TILE, T)
    P = D_HEAD_PAD

    def body(cq_ref, qg_ref, wa_ref, wb_ref, ca_ref, sb_ref, q_ref):
        c = _rms_fwd(cq_ref[...], qg_ref[...]).astype(BF16)
        q_ref[...] = (_dot(c, wa_ref[...]) * ca_ref[...] + _dot(c, wb_ref[...]) * sb_ref[...]).astype(BF16)

    tok = pl.BlockSpec((tm, P), lambda i, hh: (i, 0))
    wsp = pl.BlockSpec((R, P), lambda i, hh: (0, hh))
    return pl.pallas_call(
        body, name="q_proj", grid=(T // tm, N_HEADS),
        in_specs=[pl.BlockSpec((tm, R), lambda i, hh: (i, 0)), _full((1, R)), wsp, wsp, tok, tok],
        out_specs=pl.BlockSpec((tm, P), lambda i, hh: (i, hh)),
        out_shape=jax.ShapeDtypeStruct((T, N_HEADS * P), BF16),
        compiler_params=_cparams("parallel", "arbitrary"),
    )(cq, qg, wa, wb, ca, sb)


def _q_proj_bwd(dq, cq, qg, wa, wb, ca, sb):
    T, R = cq.shape
    tm = min(TOKEN_TILE, T)
    P = D_HEAD_PAD

    def body(dq_ref, cq_ref, qg_ref, wa_ref, wb_ref, ca_ref, sb_ref,
             da_ref, db_ref, cqn_ref, dcq_ref, dqg_ref, acc_sc):
        i, hh = pl.program_id(0), pl.program_id(1)

        @pl.when(hh == 0)
        def _():
            acc_sc[...] = jnp.zeros_like(acc_sc)
            cqn_ref[...] = _rms_fwd(cq_ref[...], qg_ref[...]).astype(BF16)

        d = dq_ref[...]
        da = (d * ca_ref[...]).astype(BF16)
        db = (d * sb_ref[...]).astype(BF16)
        da_ref[...] = da
        db_ref[...] = db
        acc_sc[...] += _dot_nt(da, wa_ref[...]) + _dot_nt(db, wb_ref[...])

        @pl.when(hh == N_HEADS - 1)
        def _():
            dx, dg = _rms_bwd(acc_sc[...], cq_ref[...], qg_ref[...])
            dcq_ref[...] = dx

            @pl.when(i == 0)
            def _():
                dqg_ref[...] = jnp.zeros_like(dqg_ref)

            dqg_ref[...] += dg

    tok = pl.BlockSpec((tm, P), lambda i, hh: (i, 0))
    hd = pl.BlockSpec((tm, P), lambda i, hh: (i, hh))
    wsp = pl.BlockSpec((R, P), lambda i, hh: (0, hh))
    rr = pl.BlockSpec((tm, R), lambda i, hh: (i, 0))
    return pl.pallas_call(
        body, name="q_proj_bwd", grid=(T // tm, N_HEADS),
        in_specs=[hd, rr, _full((1, R)), wsp, wsp, tok, tok],
        out_specs=[hd, hd, rr, rr, _full((1, R))],
        out_shape=[jax.ShapeDtypeStruct((T, N_HEADS * P), BF16), jax.ShapeDtypeStruct((T, N_HEADS * P), BF16),
                   jax.ShapeDtypeStruct((T, R), BF16), jax.ShapeDtypeStruct((T, R), F32),
                   jax.ShapeDtypeStruct((1, R), F32)],
        scratch_shapes=[pltpu.VMEM((tm, R), F32)],
        compiler_params=_cparams("arbitrary", "arbitrary"),
    )(dq, cq, qg, wa, wb, ca, sb)


def _kv_proj(h, g_in, wka, wkb, g_c, wuk, wuv, ca, sb):
    T, D = h.shape
    tm = min(TOKEN_TILE, T)
    P, C = D_HEAD_PAD, D_NOPE

    def body(h_ref, gi_ref, wka_ref, wkb_ref, gc_ref, wuk_ref, wuv_ref, ca_ref, sb_ref, k_ref, v_ref, craw_ref):
        u = _rms_fwd(h_ref[...], gi_ref[...]).astype(BF16)
        kva = _dot(u, wka_ref[...])
        kvb = _dot(u, wkb_ref[...])
        craw = kva[:, :C]
        craw_ref[...] = craw
        c = _rms_fwd(craw, gc_ref[...]).astype(BF16)
        kr = kva * ca_ref[...] + kvb * sb_ref[...]
        kn = _dot(c, wuk_ref[...])
        for hh in range(N_HEADS):
            k_ref[:, hh * P:(hh + 1) * P] = (kn[:, hh * P:(hh + 1) * P] + kr).astype(BF16)
        v_ref[...] = _dot(c, wuv_ref[...]).astype(BF16)

    tok = pl.BlockSpec((tm, P), lambda i: (i, 0))
    return pl.pallas_call(
        body, name="kv_proj", grid=(T // tm,),
        in_specs=[pl.BlockSpec((tm, D), lambda i: (i, 0)), _full((1, D)), _full((D, P)), _full((D, P)),
                  _full((1, C)), _full(wuk.shape), _full(wuv.shape), tok, tok],
        out_specs=[pl.BlockSpec((tm, N_HEADS * P), lambda i: (i, 0)),
                   pl.BlockSpec((tm, N_HEADS * D_V), lambda i: (i, 0)), pl.BlockSpec((tm, C), lambda i: (i, 0))],
        out_shape=[jax.ShapeDtypeStruct((T, N_HEADS * P), BF16), jax.ShapeDtypeStruct((T, N_HEADS * D_V), BF16),
                   jax.ShapeDtypeStruct((T, C), F32)],
        compiler_params=_cparams("parallel"),
    )(h, g_in, wka, wkb, g_c, wuk, wuv, ca, sb)


def _kv_proj_bwd(dk, dv, dh, h, g_in, wka, wkb, craw, g_c, wuk, wuv, ca, sb):
    T, D = h.shape
    tm = min(TOKEN_TILE // 2, T)
    P, C = D_HEAD_PAD, D_NOPE

    def body(dk_ref, dv_ref, dh_ref, h_ref, gi_ref, wka_ref, wkb_ref, craw_ref, gc_ref, wuk_ref, wuv_ref,
             ca_ref, sb_ref, o_ref, dgi_ref, dwka_ref, dwkb_ref, dgc_ref, dwuk_ref, dwuv_ref):
        @pl.when(pl.program_id(0) == 0)
        def _():
            for r in (dgi_ref, dwka_ref, dwkb_ref, dgc_ref, dwuk_ref, dwuv_ref):
                r[...] = jnp.zeros_like(r)

        x = h_ref[...]
        u = _rms_fwd(x, gi_ref[...]).astype(BF16)
        craw = craw_ref[...]
        c = _rms_fwd(craw, gc_ref[...]).astype(BF16)
        dkf = dk_ref[...]
        dkb = dkf.astype(BF16)
        dvb = dv_ref[...].astype(BF16)
        dwuk_ref[...] += _dot_tn(c, dkb)
        dwuv_ref[...] += _dot_tn(c, dvb)
        dc = _dot_nt(dkb, wuk_ref[...]) + _dot_nt(dvb, wuv_ref[...])
        dkr = dkf[:, :P]
        for hh in range(1, N_HEADS):
            dkr = dkr + dkf[:, hh * P:(hh + 1) * P]
        dcraw, dgc = _rms_bwd(dc, craw, gc_ref[...])
        dgc_ref[...] += dgc
        dkva = jnp.concatenate([dcraw, (dkr * ca_ref[...])[:, C:]], axis=1).astype(BF16)
        dkvb = (dkr * sb_ref[...]).astype(BF16)
        dwka_ref[...] += _dot_tn(u, dkva)
        dwkb_ref[...] += _dot_tn(u, dkvb)
        du = _dot_nt(dkva, wka_ref[...]) + _dot_nt(dkvb, wkb_ref[...])
        dx, dgi = _rms_bwd(du, x, gi_ref[...])
        dgi_ref[...] += dgi
        o_ref[...] = dh_ref[...] + dx

    row = pl.BlockSpec((tm, D), lambda i: (i, 0))
    tok = pl.BlockSpec((tm, P), lambda i: (i, 0))
    return pl.pallas_call(
        body, name="kv_proj_bwd", grid=(T // tm,),
        in_specs=[pl.BlockSpec((tm, N_HEADS * P), lambda i: (i, 0)),
                  pl.BlockSpec((tm, N_HEADS * D_V), lambda i: (i, 0)), row, row, _full((1, D)),
                  _full((D, P)), _full((D, P)), pl.BlockSpec((tm, C), lambda i: (i, 0)), _full((1, C)),
                  _full(wuk.shape), _full(wuv.shape), tok, tok],
        out_specs=[row, _full((1, D)), _full((D, P)), _full((D, P)), _full((1, C)),
                   _full(wuk.shape), _full(wuv.shape)],
        out_shape=[jax.ShapeDtypeStruct((T, D), F32), jax.ShapeDtypeStruct((1, D), F32),
                   jax.ShapeDtypeStruct((D, P), F32), jax.ShapeDtypeStruct((D, P), F32),
                   jax.ShapeDtypeStruct((1, C), F32), jax.ShapeDtypeStruct(wuk.shape, F32),
                   jax.ShapeDtypeStruct(wuv.shape, F32)],
        compiler_params=_cparams("arbitrary"),
    )(dk, dv, dh, h, g_in, wka, wkb, craw, g_c, wuk, wuv, ca, sb)


_ATTN_SCALE = D_QK ** -0.5


def _causal_mask(t):
    return lax.broadcasted_iota(jnp.int32, (t, t), 1) <= lax.broadcasted_iota(jnp.int32, (t, t), 0)


def _causal_mask_t(t):
    return lax.broadcasted_iota(jnp.int32, (t, t), 0) <= lax.broadcasted_iota(jnp.int32, (t, t), 1)


def _flash_fwd(q, k, v):
    T = q.shape[0]
    t = min(ATTN_TILE, T)
    P = D_HEAD_PAD

    def body(q_ref, k_ref, v_ref, o_ref, lse_ref, m_sc, l_sc, acc_sc):
        qi = pl.program_id(1)
        qq = q_ref[...]
        m_sc[...] = jnp.full_like(m_sc, NEG_BIG)
        l_sc[...] = jnp.zeros_like(l_sc)
        acc_sc[...] = jnp.zeros_like(acc_sc)

        def chunk(ki, diagonal):
            rows = pl.ds(pl.multiple_of(ki * t, t), t)
            s = _dot_nt(qq, k_ref[rows, :])
            if diagonal:
                s = jnp.where(_causal_mask(t), s, NEG_BIG)
            m_prev = m_sc[...]
            m_new = jnp.maximum(m_prev, jnp.max(s, axis=1, keepdims=True))
            p = jnp.exp((s - m_new) * _ATTN_SCALE)
            alpha = jnp.exp((m_prev - m_new) * _ATTN_SCALE)
            l_sc[...] = alpha * l_sc[...] + jnp.sum(p, axis=1, keepdims=True)
            acc_sc[...] = alpha * acc_sc[...] + _dot(p.astype(BF16), v_ref[rows, :])
            m_sc[...] = m_new

        def loop_body(ki, carry):
            chunk(ki, False)
            return carry

        lax.fori_loop(0, qi, loop_body, 0)
        chunk(qi, True)
        l = l_sc[...]
        o_ref[...] = (acc_sc[...] / l).astype(BF16)
        lse_ref[...] = jnp.broadcast_to(m_sc[...] * _ATTN_SCALE + jnp.log(l), lse_ref.shape)

    return pl.pallas_call(
        body, name="flash_fwd", grid=(N_HEADS, T // t),
        in_specs=[pl.BlockSpec((t, P), lambda hh, i: (i, hh)), pl.BlockSpec((T, P), lambda hh, i: (0, hh)),
                  pl.BlockSpec((T, D_V), lambda hh, i: (0, hh))],
        out_specs=[pl.BlockSpec((t, D_V), lambda hh, i: (i, hh)),
                   pl.BlockSpec((None, t, 128), lambda hh, i: (hh, i, 0))],
        out_shape=[jax.ShapeDtypeStruct((T, N_HEADS * D_V), BF16), jax.ShapeDtypeStruct((N_HEADS, T, 128), F32)],
        scratch_shapes=[pltpu.VMEM((t, 1), F32), pltpu.VMEM((t, 1), F32), pltpu.VMEM((t, D_V), F32)],
        compiler_params=_cparams("parallel", "arbitrary"),
    )(q, k, v)


def _flash_dq(q, k, v, o, do, lse):
    T = q.shape[0]
    t = min(ATTN_TILE, T)
    P = D_HEAD_PAD

    def body(q_ref, k_ref, v_ref, o_ref, do_ref, lse_ref, dq_ref, delta_ref, acc_sc):
        qi = pl.program_id(1)
        qq = q_ref[...]
        dd = do_ref[...]
        delta = jnp.sum(o_ref[...].astype(F32) * dd.astype(F32), axis=1, keepdims=True)
        delta_ref[...] = jnp.broadcast_to(delta, delta_ref.shape)
        lse = lse_ref[:, 0:1]
        acc_sc[...] = jnp.zeros_like(acc_sc)

        def chunk(ki, diagonal):
            rows = pl.ds(pl.multiple_of(ki * t, t), t)
            kk = k_ref[rows, :]
            p = jnp.exp(_dot_nt(qq, kk) * _ATTN_SCALE - lse)
            if diagonal:
                p = jnp.where(_causal_mask(t), p, 0.0)
            dp = _dot_nt(dd, v_ref[rows, :])
            ds = (p * (dp - delta) * _ATTN_SCALE).astype(BF16)
            acc_sc[...] += _dot(ds, kk)

        def loop_body(ki, carry):
            chunk(ki, False)
            return carry

        lax.fori_loop(0, qi, loop_body, 0)
        chunk(qi, True)
        dq_ref[...] = acc_sc[...]

    hd_v = pl.BlockSpec((t, D_V), lambda hh, i: (i, hh))
    stat = pl.BlockSpec((None, t, 128), lambda hh, i: (hh, i, 0))
    return pl.pallas_call(
        body, name="flash_dq", grid=(N_HEADS, T // t),
        in_specs=[pl.BlockSpec((t, P), lambda hh, i: (i, hh)), pl.BlockSpec((T, P), lambda hh, i: (0, hh)),
                  pl.BlockSpec((T, D_V), lambda hh, i: (0, hh)), hd_v, hd_v, stat],
        out_specs=[pl.BlockSpec((t, P), lambda hh, i: (i, hh)), stat],
        out_shape=[jax.ShapeDtypeStruct((T, N_HEADS * P), F32), jax.ShapeDtypeStruct((N_HEADS, T, 128), F32)],
        scratch_shapes=[pltpu.VMEM((t, P), F32)],
        compiler_params=_cparams("parallel", "arbitrary"),
    )(q, k, v, o, do, lse)


def _flash_dkv(q, k, v, do, lse_row, delta_row, dk_prev, dv_prev):
    T = q.shape[0]
    t = min(ATTN_TILE, T)
    nq = T // t
    P = D_HEAD_PAD

    def body(k_ref, v_ref, q_ref, do_ref, lse_ref, delta_ref, dkp_ref, dvp_ref, dk_ref, dv_ref, dk_sc, dv_sc):
        ki = pl.program_id(1)
        kk = k_ref[...]
        vv = v_ref[...]
        dk_sc[...] = dkp_ref[...]
        dv_sc[...] = dvp_ref[...]

        def chunk(qi, diagonal):
            rows = pl.ds(pl.multiple_of(qi * t, t), t)
            qq = q_ref[rows, :]
            dd = do_ref[rows, :]
            p_t = jnp.exp(_dot_nt(kk, qq) * _ATTN_SCALE - lse_ref[qi])
            if diagonal:
                p_t = jnp.where(_causal_mask_t(t), p_t, 0.0)
            dv_sc[...] += _dot(p_t.astype(BF16), dd)
            dp_t = _dot_nt(vv, dd)
            ds_t = (p_t * (dp_t - delta_ref[qi]) * _ATTN_SCALE).astype(BF16)
            dk_sc[...] += _dot(ds_t, qq)

        def loop_body(qi, carry):
            chunk(qi, False)
            return carry

        chunk(ki, True)
        lax.fori_loop(ki + 1, nq, loop_body, 0)
        dk_ref[...] = dk_sc[...]
        dv_ref[...] = dv_sc[...]

    kb = pl.BlockSpec((t, P), lambda hh, i: (i, hh))
    vb = pl.BlockSpec((t, D_V), lambda hh, i: (i, hh))
    stat = pl.BlockSpec((None, nq, 1, t), lambda hh, i: (hh, 0, 0, 0))
    return pl.pallas_call(
        body, name="flash_dkv", grid=(N_HEADS, nq),
        in_specs=[kb, vb, pl.BlockSpec((T, P), lambda hh, i: (0, hh)), pl.BlockSpec((T, D_V), lambda hh, i: (0, hh)),
                  stat, stat, kb, vb],
        out_specs=[kb, vb],
        out_shape=[jax.ShapeDtypeStruct((T, N_HEADS * P), F32), jax.ShapeDtypeStruct((T, N_HEADS * D_V), F32)],
        scratch_shapes=[pltpu.VMEM((t, P), F32), pltpu.VMEM((t, D_V), F32)],
        compiler_params=_cparams("parallel", "arbitrary"),
    )(k, v, q, do, lse_row, delta_row, dk_prev, dv_prev)


def _loss_head(h, g, target):
    T, D = h.shape
    tm = min(TOKEN_TILE, T)

    def body(h_ref, g_ref, t_ref, dh_ref, loss_ref, dg_ref):
        @pl.when(pl.program_id(0) == 0)
        def _():
            loss_ref[...] = jnp.zeros_like(loss_ref)
            dg_ref[...] = jnp.zeros_like(dg_ref)

        x = h_ref[...]
        err = _rms_fwd(x, g_ref[...]) - t_ref[...]
        per_tok = jnp.mean(err * err, axis=-1, keepdims=True)
        loss_ref[...] += 0.5 * jnp.sum(per_tok, axis=0, keepdims=True)
        dx, dg = _rms_bwd(err * (1.0 / D), x, g_ref[...])
        dh_ref[...] = dx
        dg_ref[...] += dg

    row = pl.BlockSpec((tm, D), lambda i: (i, 0))
    return pl.pallas_call(
        body, name="loss_head", grid=(T // tm,),
        in_specs=[row, _full((1, D)), row], out_specs=[row, _full((1, 128)), _full((1, D))],
        out_shape=[jax.ShapeDtypeStruct((T, D), F32), jax.ShapeDtypeStruct((1, 128), F32),
                   jax.ShapeDtypeStruct((1, D), F32)],
        compiler_params=_cparams("arbitrary"),
    )(h, g, target)


def _sum_parts(parts, tr, name):
    _, R, C = parts.shape

    def body(p_ref, o_ref):
        acc = p_ref[0].astype(F32)
        for j in range(1, N_DEV):
            acc = acc + p_ref[j].astype(F32)
        o_ref[...] = acc

    return pl.pallas_call(
        body, name=name, grid=(R // tr,),
        in_specs=[pl.BlockSpec((N_DEV, tr, C), lambda i: (0, i, 0))],
        out_specs=pl.BlockSpec((tr, C), lambda i: (i, 0)),
        out_shape=jax.ShapeDtypeStruct((R, C), F32),
        compiler_params=_cparams("parallel"),
    )(parts)


def _adamw(w, g, m, v):
    R, C = w.shape
    tr = _row_tile(R, TOKEN_TILE)

    def body(w_ref, g_ref, m_ref, v_ref, d_ref, mo_ref, vo_ref):
        gg = g_ref[...]
        mn = ADAM_B1 * m_ref[...] + (1.0 - ADAM_B1) * gg
        vn = ADAM_B2 * v_ref[...] + (1.0 - ADAM_B2) * (gg * gg)
        m_hat = mn / (1.0 - ADAM_B1 ** ADAM_STEP)
        v_hat = vn / (1.0 - ADAM_B2 ** ADAM_STEP)
        d_ref[...] = -ADAM_LR * (m_hat / (jnp.sqrt(v_hat) + ADAM_EPS) + ADAM_WD * w_ref[...])
        mo_ref[...] = mn
        vo_ref[...] = vn

    blk = pl.BlockSpec((tr, C), lambda i: (i, 0))
    return pl.pallas_call(
        body, name="adamw", grid=(R // tr,), in_specs=[blk] * 4, out_specs=[blk] * 3,
        out_shape=[jax.ShapeDtypeStruct((R, C), F32)] * 3,
        compiler_params=_cparams("parallel"),
    )(w, g, m, v)


def _adamw_nd(w, g, m, v):
    shape = w.shape
    two_d = (1, shape[0]) if len(shape) == 1 else (int(np.prod(shape[:-1])), shape[-1])
    outs = _adamw(w.reshape(two_d), g.reshape(two_d), m.reshape(two_d), v.reshape(two_d))
    return tuple(o.reshape(shape) for o in outs)


def _f32_as_bf16_pairs(a):
    return lax.bitcast_convert_type(a, BF16).reshape(a.shape[:-1] + (a.shape[-1] * 2,))


def _bf16_pairs_as_f32(a):
    return lax.bitcast_convert_type(a.reshape(a.shape[:-1] + (a.shape[-1] // 2, 2)), F32)


def _pack_misc(w_o, w_dq, w_uq, w_dkv, pool_w, pool_scale):
    lead = w_o.shape[:-3]
    rows = [w_o, w_dq, w_uq, w_dkv, pool_w]
    flat = [r.astype(BF16).reshape(lead + (-1, REP_COLS)) for r in rows]
    ps = _f32_as_bf16_pairs(pool_scale.astype(F32)).reshape(lead + (1, -1))
    ps = jnp.concatenate([ps, jnp.zeros(lead + (1, REP_COLS - ps.shape[-1]), BF16)], axis=-1)
    used = sum(f.shape[-2] for f in flat) + 1
    pad = jnp.zeros(lead + (MISC_ROWS - used, REP_COLS), BF16)
    return jnp.concatenate(flat + [ps, pad], axis=-2)


def _unpack_misc(buf, shapes):
    out, r0 = [], 0
    for shp in shapes[:-1]:
        n = int(np.prod(shp)) // REP_COLS
        out.append(buf[:, r0:r0 + n].reshape((N_DEV,) + shp))
        r0 += n
    n_ps = int(np.prod(shapes[-1]))
    out.append(_bf16_pairs_as_f32(buf[:, r0, :2 * n_ps]).reshape((N_DEV,) + shapes[-1]))
    return out


def _cat_dev(a, axis):
    a = jnp.moveaxis(a, 0, axis)
    return a.reshape(a.shape[:axis] + (a.shape[axis] * a.shape[axis + 1],) + a.shape[axis + 2:])


def _split_dev(a, axis):
    a = a.reshape(a.shape[:axis] + (N_DEV, a.shape[axis] // N_DEV) + a.shape[axis + 1:])
    return jnp.moveaxis(a, axis, 0)


def kernel(x, ffn_pre_norm, ffn_pre_wg, ffn_pre_wu, ffn_pre_wd, mix_norm, ffn_post_norm, ffn_post_wg, ffn_post_wu, ffn_post_wd, pool_w, pool_scale, kv_in_norm, w_dkv, ckv_norm, w_uk, w_uv, q_lora_norm, w_dq, w_uq, w_o, final_norm, loss_target, m_ffn_pre_norm, m_ffn_pre_wg, m_ffn_pre_wu, m_ffn_pre_wd, m_mix_norm, m_ffn_post_norm, m_ffn_post_wg, m_ffn_post_wu, m_ffn_post_wd, m_pool_w, m_pool_scale, m_kv_in_norm, m_w_dkv, m_ckv_norm, m_w_uk, m_w_uv, m_q_lora_norm, m_w_dq, m_w_uq, m_w_o, m_final_norm, v_ffn_pre_norm, v_ffn_pre_wg, v_ffn_pre_wu, v_ffn_pre_wd, v_mix_norm, v_ffn_post_norm, v_ffn_post_wg, v_ffn_post_wu, v_ffn_post_wd, v_pool_w, v_pool_scale, v_kv_in_norm, v_w_dkv, v_ckv_norm, v_w_uk, v_w_uv, v_q_lora_norm, v_w_dq, v_w_uq, v_w_o, v_final_norm):
    weights = dict(ffn_pre_norm=ffn_pre_norm, ffn_pre_wg=ffn_pre_wg, ffn_pre_wu=ffn_pre_wu, ffn_pre_wd=ffn_pre_wd,
                   mix_norm=mix_norm, ffn_post_norm=ffn_post_norm, ffn_post_wg=ffn_post_wg,
                   ffn_post_wu=ffn_post_wu, ffn_post_wd=ffn_post_wd, pool_w=pool_w, pool_scale=pool_scale,
                   kv_in_norm=kv_in_norm, w_dkv=w_dkv, ckv_norm=ckv_norm, w_uk=w_uk, w_uv=w_uv,
                   q_lora_norm=q_lora_norm, w_dq=w_dq, w_uq=w_uq, w_o=w_o, final_norm=final_norm)
    moments_m = dict(ffn_pre_norm=m_ffn_pre_norm, ffn_pre_wg=m_ffn_pre_wg, ffn_pre_wu=m_ffn_pre_wu,
                     ffn_pre_wd=m_ffn_pre_wd, mix_norm=m_mix_norm, ffn_post_norm=m_ffn_post_norm,
                     ffn_post_wg=m_ffn_post_wg, ffn_post_wu=m_ffn_post_wu, ffn_post_wd=m_ffn_post_wd,
                     pool_w=m_pool_w, pool_scale=m_pool_scale, kv_in_norm=m_kv_in_norm, w_dkv=m_w_dkv,
                     ckv_norm=m_ckv_norm, w_uk=m_w_uk, w_uv=m_w_uv, q_lora_norm=m_q_lora_norm, w_dq=m_w_dq,
                     w_uq=m_w_uq, w_o=m_w_o, final_norm=m_final_norm)
    moments_v = dict(ffn_pre_norm=v_ffn_pre_norm, ffn_pre_wg=v_ffn_pre_wg, ffn_pre_wu=v_ffn_pre_wu,
                     ffn_pre_wd=v_ffn_pre_wd, mix_norm=v_mix_norm, ffn_post_norm=v_ffn_post_norm,
                     ffn_post_wg=v_ffn_post_wg, ffn_post_wu=v_ffn_post_wu, ffn_post_wd=v_ffn_post_wd,
                     pool_w=v_pool_w, pool_scale=v_pool_scale, kv_in_norm=v_kv_in_norm, w_dkv=v_w_dkv,
                     ckv_norm=v_ckv_norm, w_uk=v_w_uk, w_uv=v_w_uv, q_lora_norm=v_q_lora_norm, w_dq=v_w_dq,
                     w_uq=v_w_uq, w_o=v_w_o, final_norm=v_final_norm)
    order = list(weights)

    T, D = x.shape[1], x.shape[2]
    depth = ffn_pre_norm.shape[0]
    n_a = pool_w.shape[0]
    n_b = depth - n_a
    fs = ffn_pre_wd.shape[1]
    F = fs * N_DEV
    n_ffn = 2 * depth
    t_attn = min(ATTN_TILE, T)

    ffn_local = jnp.stack([
        jnp.stack([jnp.swapaxes(wg[l], 0, 1), jnp.swapaxes(wu[l], 0, 1), wd[l]])
        for l in range(depth)
        for wg, wu, wd in ((ffn_pre_wg, ffn_pre_wu, ffn_pre_wd), (ffn_post_wg, ffn_post_wu, ffn_post_wd))
    ]).astype(BF16)
    misc_local = _pack_misc(w_o, w_dq, w_uq.reshape(n_b, w_uq.shape[1], -1), w_dkv, pool_w, pool_scale)
    misc_shapes = (w_o.shape, w_dq.shape, (n_b, w_uq.shape[1], N_HEADS * D_QK), w_dkv.shape, pool_w.shape,
                   pool_scale.shape)
    ffn_all, misc_all = _all_gather(ffn_local.reshape(n_ffn * 3, fs, D), misc_local)
    wall = ffn_all.reshape(n_ffn, 3, F, D)
    o_blk, dq_blk, uq_blk, dkv_blk, pw_blk, ps_blk = _unpack_misc(misc_all, misc_shapes)
    w_o_f = _cat_dev(o_blk, 1)
    w_dq_f = _cat_dev(dq_blk, 1)
    w_uq_f = _cat_dev(uq_blk, 1).reshape(n_b, -1, N_HEADS, D_QK)
    w_dkv_f = _cat_dev(dkv_blk, 0)
    pool_w_f = _cat_dev(pw_blk, 2)
    pool_scale_f = _cat_dev(ps_blk, 1)
    rq = w_dq_f.shape[2]
    wqa, wqb = _rope_weight_pair(w_uq_f)
    wqa = wqa.reshape(n_b, rq, N_HEADS * D_HEAD_PAD)
    wqb = wqb.reshape(n_b, rq, N_HEADS * D_HEAD_PAD)
    wka, wkb = _rope_weight_pair(w_dkv_f)
    wuk = jnp.concatenate([w_uk, jnp.zeros_like(w_uk)], axis=-1).astype(BF16).reshape(D_NOPE, N_HEADS * D_HEAD_PAD)
    wuv = w_uv.astype(BF16).reshape(D_NOPE, N_HEADS * D_V)
    ca_q, ca_k, sb = _rope_tables(T)

    def vec(a):
        return a.reshape(1, -1)

    h = x.reshape(T, D)
    saved = []
    k_all = v_all = craw = h_kv = None
    for l in range(depth):
        s = {"h0": h}
        h, s["g1"], s["u1"] = _ffn_fwd(h, vec(ffn_pre_norm[l]), wall, 2 * l)
        s["h1"] = h
        if l < n_a:
            h, s["y"] = _pool_fwd(h, vec(mix_norm[l]), pool_w_f[l], vec(pool_scale_f[l]))
        else:
            j = l - n_a
            s["cq"] = _mm_rows(h, w_dq_f[j], nt=False, out_dtype=F32, norm_g=vec(mix_norm[l]), name="q_down")
            s["q"] = _q_proj(s["cq"], vec(q_lora_norm[j]), wqa[j], wqb[j], ca_q, sb)
            s["o"], s["lse"] = _flash_fwd(s["q"], k_all, v_all)
            h = _mm_rows(s["o"], w_o_f[j], nt=False, out_dtype=F32, res=h, name="attn_out")
        s["h2"] = h
        h, s["g2"], s["u2"] = _ffn_fwd(h, vec(ffn_post_norm[l]), wall, 2 * l + 1)
        if l == n_a - 1:
            h_kv = h
            k_all, v_all, craw = _kv_proj(h, vec(kv_in_norm), wka, wkb, vec(ckv_norm), wuk, wuv, ca_k, sb)
        saved.append(s)

    dh, loss_part, d_final = _loss_head(h, vec(final_norm), loss_target.reshape(T, D))

    gbuf = jnp.zeros((n_ffn, 3, F, D), BF16)
    grads = {}
    d_pre, d_post, d_mix = [None] * depth, [None] * depth, [None] * depth
    d_pool_w, d_pool_scale = [None] * n_a, [None] * n_a
    d_qln, d_wdq, d_wuq, d_wo = [None] * n_b, [None] * n_b, [None] * n_b, [None] * n_b
    dk_acc = jnp.zeros((T, N_HEADS * D_HEAD_PAD), F32)
    dv_acc = jnp.zeros((T, N_HEADS * D_V), F32)
    for l in reversed(range(depth)):
        s = saved[l]
        if l == n_a - 1:
            (dh, grads["kv_in_norm"], dwka, dwkb, grads["ckv_norm"], dwuk, dwuv) = _kv_proj_bwd(
                dk_acc, dv_acc, dh, h_kv, vec(kv_in_norm), wka, wkb, craw, vec(ckv_norm), wuk, wuv, ca_k, sb)
            grads["w_dkv"] = _rope_weight_pair_grad(dwka, dwkb)
            grads["w_uk"] = dwuk.reshape(D_NOPE, N_HEADS, D_HEAD_PAD)[..., :D_NOPE]
            grads["w_uv"] = dwuv.reshape(D_NOPE, N_HEADS, D_V)
        dh, dgt, dup, u_b, dy_b, d_post[l] = _ffn_bwd_dx(dh, s["h2"], vec(ffn_post_norm[l]), s["g2"], s["u2"],
                                                       wall, 2 * l + 1)
        gbuf = _ffn_bwd_dw(dgt, dup, s["g2"], s["u2"], u_b, dy_b, gbuf, 2 * l + 1)
        if l < n_a:
            dh, d_mix[l], d_pool_w[l], d_pool_scale[l] = _pool_bwd(
                dh, s["h1"], vec(mix_norm[l]), s["y"], pool_w_f[l], vec(pool_scale_f[l]))
        else:
            j = l - n_a
            d_wo[j] = _mm_tn(s["o"], dh, name="attn_out_dw")
            do = _mm_rows(dh, w_o_f[j], nt=True, out_dtype=BF16, name="attn_out_dx")
            dq, delta = _flash_dq(s["q"], k_all, v_all, s["o"], do, s["lse"])
            lse_row = s["lse"][:, :, 0].reshape(N_HEADS, T // t_attn, 1, t_attn)
            delta_row = delta[:, :, 0].reshape(N_HEADS, T // t_attn, 1, t_attn)
            dk_acc, dv_acc = _flash_dkv(s["q"], k_all, v_all, do, lse_row, delta_row, dk_acc, dv_acc)
            da, db, cqn, dcq, d_qln[j] = _q_proj_bwd(dq, s["cq"], vec(q_lora_norm[j]), wqa[j], wqb[j], ca_q, sb)
            dwa = _mm_tn(cqn, da, name="q_up_dw")
            dwb = _mm_tn(cqn, db, name="q_up_dw")
            d_wuq[j] = _rope_weight_pair_grad(dwa.reshape(rq, N_HEADS, D_HEAD_PAD),
                                              dwb.reshape(rq, N_HEADS, D_HEAD_PAD))
            d_wdq[j] = _mm_tn(s["h1"], dcq, norm_g=vec(mix_norm[l]), name="q_down_dw")
            dh, d_mix[l] = _proj_bwd(dcq, w_dq_f[j], s["h1"], vec(mix_norm[l]), dh, "q_down_dx")
        dh, dgt, dup, u_b, dy_b, d_pre[l] = _ffn_bwd_dx(dh, s["h0"], vec(ffn_pre_norm[l]), s["g1"], s["u1"],
                                                      wall, 2 * l)
        gbuf = _ffn_bwd_dw(dgt, dup, s["g1"], s["u1"], u_b, dy_b, gbuf, 2 * l)
    grad_x = dh.reshape(x.shape)

    rep_names = ["ffn_pre_norm", "mix_norm", "ffn_post_norm", "kv_in_norm", "ckv_norm", "q_lora_norm",
                 "final_norm", "w_uk", "w_uv"]
    grads["ffn_pre_norm"] = jnp.concatenate(d_pre, axis=0)
    grads["mix_norm"] = jnp.concatenate(d_mix, axis=0)
    grads["ffn_post_norm"] = jnp.concatenate(d_post, axis=0)
    grads["q_lora_norm"] = jnp.concatenate(d_qln, axis=0)
    grads["final_norm"] = d_final
    rep_flat = jnp.concatenate([grads[n].reshape(-1) for n in rep_names] + [loss_part[0, :1]])
    n_rep = rep_flat.shape[0]
    rep_rows = -(-n_rep // (8 * REP_COLS)) * 8
    rep_g = jnp.concatenate([rep_flat, jnp.zeros((rep_rows * REP_COLS - n_rep,), F32)]).reshape(rep_rows, REP_COLS)
    misc_g = _pack_misc(_split_dev(jnp.stack(d_wo), 1), _split_dev(jnp.stack(d_wdq), 1),
                        _split_dev(jnp.stack(d_wuq).reshape(n_b, rq, -1), 1), _split_dev(grads["w_dkv"], 0),
                        _split_dev(jnp.stack(d_pool_w), 2),
                        _split_dev(jnp.concatenate(d_pool_scale, axis=0), 1))
    ffn_parts, misc_parts, rep_parts = _grad_exchange(gbuf.reshape(n_ffn * 3, N_DEV, fs, D), misc_g, rep_g)
    ffn_sum = _sum_parts(ffn_parts.reshape(N_DEV, n_ffn * 3 * fs, D), fs, "sum_ffn").reshape(n_ffn, 3, fs, D)
    misc_sum_parts = _unpack_misc(misc_parts, misc_shapes)
    rep_sum = _sum_parts(rep_parts, _row_tile(rep_rows, 128), "sum_rep").reshape(-1)

    def sum_small(p):
        shp = p.shape[1:]
        two_d = (int(np.prod(shp[:-1])), shp[-1])
        return _sum_parts(p.reshape((N_DEV,) + two_d), two_d[0], "sum_misc").reshape(shp)

    g_wo, g_wdq, g_wuq, g_wdkv, g_pw, g_ps = [sum_small(p) for p in misc_sum_parts]
    grads.update(w_o=g_wo, w_dq=g_wdq, w_uq=g_wuq.reshape(w_uq.shape), w_dkv=g_wdkv, pool_w=g_pw, pool_scale=g_ps)
    for kind, (npre, npost) in enumerate((("ffn_pre_wg", "ffn_post_wg"), ("ffn_pre_wu", "ffn_post_wu"),
                                          ("ffn_pre_wd", "ffn_post_wd"))):
        pre = ffn_sum[0::2, kind]
        post = ffn_sum[1::2, kind]
        if kind < 2:
            pre, post = jnp.swapaxes(pre, 1, 2), jnp.swapaxes(post, 1, 2)
        grads[npre], grads[npost] = pre, post
    off = 0
    for n in rep_names:
        size = int(np.prod(weights[n].shape))
        grads[n] = rep_sum[off:off + size].reshape(weights[n].shape)
        off += size
    loss = rep_sum[off]

    deltas, new_m, new_v = {}, {}, {}
    for n in order:
        deltas[n], new_m[n], new_v[n] = _adamw_nd(weights[n], grads[n], moments_m[n], moments_v[n])
    return (loss, grad_x, *[grads[n] for n in order], *[deltas[n] for n in order],
            *[new_m[n] for n in order], *[new_v[n] for n in order])
```

```python
import functools

import numpy as np
import jax
import jax.numpy as jnp
from jax import lax
from jax.experimental import pallas as pl
from jax.experimental.pallas import tpu as pltpu

F32, BF16 = jnp.float32, jnp.bfloat16
N_DEV = 8
RMS_EPS = 1e-6
N_HEADS = 16
D_NOPE, D_ROPE, D_V = 128, 64, 128
D_QK = D_NOPE + D_ROPE
D_HEAD_PAD = 256
ROPE_THETA = 10000.0
POOL_WINDOWS = (2, 4, 8, 16)
POOL_HALO = 16
ADAM_LR, ADAM_B1, ADAM_B2, ADAM_EPS, ADAM_WD, ADAM_STEP = 0.001, 0.9, 0.999, 1e-08, 0.01, 10
NEG_BIG = -1e30
V7X_VMEM_LIMIT = 56 * 1024 * 1024
TOKEN_TILE = 512
ATTN_TILE = 512
FFN_TILE = 256
MISC_ROWS = 864
REP_COLS = 1024


def _cparams(*sem):
    return pltpu.CompilerParams(dimension_semantics=sem, vmem_limit_bytes=V7X_VMEM_LIMIT)


def _dot(a, b):
    return lax.dot_general(a, b, (((1,), (0,)), ((), ())), preferred_element_type=F32)


def _dot_nt(a, b):
    return lax.dot_general(a, b, (((1,), (1,)), ((), ())), preferred_element_type=F32)


def _dot_tn(a, b):
    return lax.dot_general(a, b, (((0,), (0,)), ((), ())), preferred_element_type=F32)


def _rms_fwd(x, g):
    r = lax.rsqrt(jnp.mean(x * x, axis=-1, keepdims=True) + RMS_EPS)
    return (x * r) * g


def _rms_bwd(du, x, g):
    r = lax.rsqrt(jnp.mean(x * x, axis=-1, keepdims=True) + RMS_EPS)
    xh = x * r
    dg = jnp.sum(du * xh, axis=0, keepdims=True)
    dxh = du * g
    dx = r * (dxh - xh * jnp.mean(dxh * xh, axis=-1, keepdims=True))
    return dx, dg


def _sigmoid(x):
    return 1.0 / (1.0 + jnp.exp(-x))


def _split_bf16(x):
    hi = x.astype(BF16)
    lo = (x - hi.astype(F32)).astype(BF16)
    return hi, lo


def _full(shape):
    return pl.BlockSpec(shape, lambda *_: (0,) * len(shape))


def _row_tile(rows, cap):
    for t in range(min(cap, rows) // 8 * 8, 0, -8):
        if rows % t == 0:
            return t
    return rows


def _peers():
    x, y, c = lax.axis_index("x"), lax.axis_index("y"), lax.axis_index("c")
    out = []
    for k in range(1, N_DEV):
        px = 1 - x if (k >> 2) & 1 else x
        py = 1 - y if (k >> 1) & 1 else y
        pc = 1 - c if k & 1 else c
        out.append(((px, py, pc), 4 * px + 2 * py + pc))
    return 4 * x + 2 * y + c, out


def _exchange(arrays, src_of, dst_of, out_shapes, name):
    n = len(arrays)

    def body(*refs):
        ins, outs = refs[:n], refs[n:2 * n]
        send_sems, recv_sems, loc_sems = refs[2 * n:]
        me, peers = _peers()
        own = [pltpu.make_async_copy(src_of(j, ins[j], me), dst_of(j, outs[j], me), loc_sems.at[j])
               for j in range(n)]
        for cp in own:
            cp.start()
        sends = []
        for k, (peer, pidx) in enumerate(peers):
            for j in range(n):
                sends.append(pltpu.make_async_remote_copy(
                    src_ref=src_of(j, ins[j], pidx), dst_ref=dst_of(j, outs[j], me),
                    send_sem=send_sems.at[j, k], recv_sem=recv_sems.at[j, k],
                    device_id=peer, device_id_type=pl.DeviceIdType.MESH))
        for cp in sends:
            cp.start()
        for k, (peer, pidx) in enumerate(peers):
            for j in range(n):
                pltpu.make_async_remote_copy(
                    src_ref=src_of(j, ins[j], pidx), dst_ref=dst_of(j, outs[j], pidx),
                    send_sem=send_sems.at[j, k], recv_sem=recv_sems.at[j, k],
                    device_id=peer, device_id_type=pl.DeviceIdType.MESH).wait_recv()
        for cp in sends:
            cp.wait_send()
        for cp in own:
            cp.wait()

    any_spec = pl.BlockSpec(memory_space=pl.ANY)
    return pl.pallas_call(
        body, name=name,
        out_shape=[jax.ShapeDtypeStruct(s, a.dtype) for s, a in zip(out_shapes, arrays)],
        in_specs=[any_spec] * n, out_specs=[any_spec] * n,
        scratch_shapes=[pltpu.SemaphoreType.DMA((n, N_DEV - 1)), pltpu.SemaphoreType.DMA((n, N_DEV - 1)),
                        pltpu.SemaphoreType.DMA((n,))],
    )(*arrays)


def _all_gather(ffn_local, misc_local):
    w, r, d = ffn_local.shape

    def src_of(j, ref, idx):
        return ref

    def dst_of(j, ref, idx):
        return ref.at[:, idx] if j == 0 else ref.at[idx]

    return _exchange([ffn_local, misc_local], src_of, dst_of,
                     [(w, N_DEV, r, d), (N_DEV,) + misc_local.shape], "comm_all_gather")


def _grad_exchange(ffn_g, misc_g, rep_g):
    w, _, r, d = ffn_g.shape

    def src_of(j, ref, idx):
        return (ref.at[:, idx], ref.at[idx], ref)[j]

    def dst_of(j, ref, idx):
        return ref.at[idx]

    return _exchange([ffn_g, misc_g, rep_g], src_of, dst_of,
                     [(N_DEV, w, r, d), misc_g.shape, (N_DEV,) + rep_g.shape], "comm_grad_exchange")


def _ffn_fwd(h, g, wall, e):
    T, D = h.shape
    F = wall.shape[2]
    tm, tf = min(TOKEN_TILE, T), FFN_TILE
    nf = F // tf

    def body(h_ref, g_ref, wg_ref, wu_ref, wd_ref, ho_ref, gate_ref, up_ref, u_sc, acc_sc):
        f = pl.program_id(1)

        @pl.when(f == 0)
        def _():
            u_sc[...] = _rms_fwd(h_ref[...], g_ref[...]).astype(BF16)
            acc_sc[...] = jnp.zeros_like(acc_sc)

        u = u_sc[...]
        gate = _dot_nt(u, wg_ref[...])
        up = _dot_nt(u, wu_ref[...])
        gate_ref[...] = gate.astype(BF16)
        up_ref[...] = up.astype(BF16)
        act = (gate * _sigmoid(gate) * up).astype(BF16)
        acc_sc[...] += _dot(act, wd_ref[...])

        @pl.when(f == nf - 1)
        def _():
            ho_ref[...] = h_ref[...] + 0.5 * acc_sc[...]

    def wspec(kind):
        return pl.BlockSpec((None, None, tf, D), lambda i, f: (e, kind, f, 0))

    return pl.pallas_call(
        body, name="ffn_fwd", grid=(T // tm, nf),
        in_specs=[pl.BlockSpec((tm, D), lambda i, f: (i, 0)), _full((1, D)), wspec(0), wspec(1), wspec(2)],
        out_specs=[pl.BlockSpec((tm, D), lambda i, f: (i, 0)),
                   pl.BlockSpec((tm, tf), lambda i, f: (i, f)), pl.BlockSpec((tm, tf), lambda i, f: (i, f))],
        out_shape=[jax.ShapeDtypeStruct((T, D), F32), jax.ShapeDtypeStruct((T, F), BF16),
                   jax.ShapeDtypeStruct((T, F), BF16)],
        scratch_shapes=[pltpu.VMEM((tm, D), BF16), pltpu.VMEM((tm, D), F32)],
        compiler_params=_cparams("parallel", "arbitrary"),
    )(h, g, wall, wall, wall)


def _ffn_bwd_dx(dho, h, g, gate, up, wall, e):
    T, D = h.shape
    F = wall.shape[2]
    tm, tf = min(TOKEN_TILE, T), FFN_TILE
    nf = F // tf

    def body(dho_ref, h_ref, g_ref, gate_ref, up_ref, wg_ref, wu_ref, wd_ref,
             dhi_ref, dgate_ref, dup_ref, u_ref, dy_ref, dg_ref, dy_sc, acc_sc):
        i, f = pl.program_id(0), pl.program_id(1)

        @pl.when(f == 0)
        def _():
            dy = (0.5 * dho_ref[...]).astype(BF16)
            dy_sc[...] = dy
            dy_ref[...] = dy
            u_ref[...] = _rms_fwd(h_ref[...], g_ref[...]).astype(BF16)
            acc_sc[...] = jnp.zeros_like(acc_sc)

        dact = _dot_nt(dy_sc[...], wd_ref[...])
        gt = gate_ref[...].astype(F32)
        sig = _sigmoid(gt)
        dup = (dact * (gt * sig)).astype(BF16)
        dgate = (dact * up_ref[...].astype(F32) * (sig * (1.0 + gt * (1.0 - sig)))).astype(BF16)
        dup_ref[...] = dup
        dgate_ref[...] = dgate
        acc_sc[...] += _dot(dgate, wg_ref[...]) + _dot(dup, wu_ref[...])

        @pl.when(f == nf - 1)
        def _():
            dx, dg = _rms_bwd(acc_sc[...], h_ref[...], g_ref[...])
            dhi_ref[...] = dho_ref[...] + dx

            @pl.when(i == 0)
            def _():
                dg_ref[...] = jnp.zeros_like(dg_ref)

            dg_ref[...] += dg

    def wspec(kind):
        return pl.BlockSpec((None, None, tf, D), lambda i, f: (e, kind, f, 0))

    row = pl.BlockSpec((tm, D), lambda i, f: (i, 0))
    blk = pl.BlockSpec((tm, tf), lambda i, f: (i, f))
    return pl.pallas_call(
        body, name="ffn_bwd_dx", grid=(T // tm, nf),
        in_specs=[row, row, _full((1, D)), blk, blk, wspec(0), wspec(1), wspec(2)],
        out_specs=[row, blk, blk, row, row, _full((1, D))],
        out_shape=[jax.ShapeDtypeStruct((T, D), F32), jax.ShapeDtypeStruct((T, F), BF16),
                   jax.ShapeDtypeStruct((T, F), BF16), jax.ShapeDtypeStruct((T, D), BF16),
                   jax.ShapeDtypeStruct((T, D), BF16), jax.ShapeDtypeStruct((1, D), F32)],
        scratch_shapes=[pltpu.VMEM((tm, D), BF16), pltpu.VMEM((tm, D), F32)],
        compiler_params=_cparams("arbitrary", "arbitrary"),
    )(dho, h, g, gate, up, wall, wall, wall)


def _ffn_bwd_dw(dgate, dup, gate, up, u, dy, gbuf, e):
    T, F = gate.shape
    D = u.shape[1]
    tfw = F // 2
    tk = min(256, T)
    nk = T // tk

    def body(dgate_ref, dup_ref, gate_ref, up_ref, u_ref, dy_ref, gbuf_ref, out_ref, acc_sc):
        k = pl.program_id(1)

        @pl.when(k == 0)
        def _():
            acc_sc[...] = jnp.zeros_like(acc_sc)

        uu = u_ref[...]
        gt = gate_ref[...].astype(F32)
        act = (gt * _sigmoid(gt) * up_ref[...].astype(F32)).astype(BF16)
        acc_sc[0] += _dot_tn(dgate_ref[...], uu)
        acc_sc[1] += _dot_tn(dup_ref[...], uu)
        acc_sc[2] += _dot_tn(act, dy_ref[...])

        @pl.when(k == nk - 1)
        def _():
            out_ref[...] = acc_sc[...].astype(BF16)

    blk = pl.BlockSpec((tk, tfw), lambda j, k: (k, j))
    row = pl.BlockSpec((tk, D), lambda j, k: (k, 0))
    return pl.pallas_call(
        body, name="ffn_bwd_dw", grid=(F // tfw, nk),
        in_specs=[blk, blk, blk, blk, row, row, pl.BlockSpec(memory_space=pl.ANY)],
        out_specs=pl.BlockSpec((None, 3, tfw, D), lambda j, k: (e, 0, j, 0)),
        out_shape=jax.ShapeDtypeStruct(gbuf.shape, BF16),
        scratch_shapes=[pltpu.VMEM((3, tfw, D), F32)],
        input_output_aliases={6: 0},
        compiler_params=_cparams("parallel", "arbitrary"),
    )(dgate, dup, gate, up, u, dy, gbuf)


def _mm_rows(a, b, *, nt, out_dtype, norm_g=None, res=None, name):
    T, K = a.shape
    N = b.shape[0] if nt else b.shape[1]
    tm = min(TOKEN_TILE, T)
    has_g, has_r = norm_g is not None, res is not None

    def body(*refs):
        a_ref, b_ref = refs[0], refs[1]
        o_ref = refs[-1]
        x = a_ref[...]
        if has_g:
            x = _rms_fwd(x, refs[2][...])
        x = x.astype(BF16)
        acc = _dot_nt(x, b_ref[...]) if nt else _dot(x, b_ref[...])
        if has_r:
            acc = refs[2 + has_g][...] + acc
        o_ref[...] = acc.astype(out_dtype)

    ins, specs = [a, b], [pl.BlockSpec((tm, K), lambda i: (i, 0)), _full(b.shape)]
    if has_g:
        ins.append(norm_g)
        specs.append(_full((1, K)))
    if has_r:
        ins.append(res)
        specs.append(pl.BlockSpec((tm, N), lambda i: (i, 0)))
    return pl.pallas_call(
        body, name=name, grid=(T // tm,), in_specs=specs,
        out_specs=pl.BlockSpec((tm, N), lambda i: (i, 0)),
        out_shape=jax.ShapeDtypeStruct((T, N), out_dtype),
        compiler_params=_cparams("parallel"),
    )(*ins)


def _mm_tn(a, b, *, norm_g=None, name):
    T, M = a.shape
    N = b.shape[1]
    tk = min(TOKEN_TILE, T)
    has_g = norm_g is not None

    def body(*refs):
        a_ref, b_ref, o_ref = refs[0], refs[1], refs[-1]

        @pl.when(pl.program_id(0) == 0)
        def _():
            o_ref[...] = jnp.zeros_like(o_ref)

        x = a_ref[...]
        if has_g:
            x = _rms_fwd(x, refs[2][...])
        o_ref[...] += _dot_tn(x.astype(BF16), b_ref[...].astype(BF16))

    ins = [a, b]
    specs = [pl.BlockSpec((tk, M), lambda k: (k, 0)), pl.BlockSpec((tk, N), lambda k: (k, 0))]
    if has_g:
        ins.append(norm_g)
        specs.append(_full((1, M)))
    return pl.pallas_call(
        body, name=name, grid=(T // tk,), in_specs=specs, out_specs=_full((M, N)),
        out_shape=jax.ShapeDtypeStruct((M, N), F32),
        compiler_params=_cparams("arbitrary"),
    )(*ins)


def _proj_bwd(dz, w, h, g, dh, name):
    T, D = h.shape
    N = w.shape[1]
    tm = min(TOKEN_TILE, T)

    def body(dz_ref, w_ref, h_ref, g_ref, dh_ref, o_ref, dg_ref):
        du = _dot_nt(dz_ref[...].astype(BF16), w_ref[...])
        dx, dg = _rms_bwd(du, h_ref[...], g_ref[...])
        o_ref[...] = dh_ref[...] + dx

        @pl.when(pl.program_id(0) == 0)
        def _():
            dg_ref[...] = jnp.zeros_like(dg_ref)

        dg_ref[...] += dg

    row = pl.BlockSpec((tm, D), lambda i: (i, 0))
    return pl.pallas_call(
        body, name=name, grid=(T // tm,),
        in_specs=[pl.BlockSpec((tm, N), lambda i: (i, 0)), _full((D, N)), row, _full((1, D)), row],
        out_specs=[row, _full((1, D))],
        out_shape=[jax.ShapeDtypeStruct((T, D), F32), jax.ShapeDtypeStruct((1, D), F32)],
        compiler_params=_cparams("arbitrary"),
    )(dz, w, h, g, dh)


def _pool_bands(tm):
    r = np.arange(tm)[:, None]
    c = np.arange(tm)[None, :]
    j = np.arange(POOL_HALO)[None, :]
    main, halo, main_t, halo_t = [], [], [], []
    for w in POOL_WINDOWS:
        main.append(((r - c >= 0) & (r - c < w)) / w)
        halo.append((r + POOL_HALO - j < w) / w)
        main_t.append(((c - r >= 0) & (c - r < w)) / w)
        halo_t.append((tm + j - r < w) / w)
    return tuple(jnp.asarray(np.stack(m), BF16) for m in (main, halo, main_t, halo_t))


def _pool_count_scale(i, tm, w):
    t = i * tm + lax.broadcasted_iota(jnp.int32, (tm, 1), 0)
    return w / jnp.minimum(t + 1, w).astype(F32)


def _pool_fwd(h, g, wp, scale):
    T, D = h.shape
    G, dg = len(POOL_WINDOWS), D // len(POOL_WINDOWS)
    tm = min(TOKEN_TILE, T)
    hb = tm // POOL_HALO
    bm, bh, _, _ = _pool_bands(tm)

    def body(h_ref, hh_ref, g_ref, wp_ref, sc_ref, bm_ref, bh_ref, ho_ref, y_ref):
        i = pl.program_id(0)
        x = h_ref[...]
        u = _rms_fwd(x, g_ref[...])
        uh = _rms_fwd(hh_ref[...], g_ref[...]) * (i > 0).astype(F32)
        for gi, w in enumerate(POOL_WINDOWS):
            cols = slice(gi * dg, (gi + 1) * dg)
            ug = u[:, cols]
            hi, lo = _split_bf16(ug)
            hhi, hlo = _split_bf16(uh[:, cols])
            s = (_dot(bm_ref[gi], hi) + _dot(bm_ref[gi], lo)
                 + _dot(bh_ref[gi], hhi) + _dot(bh_ref[gi], hlo))
            y = (s * _pool_count_scale(i, tm, w) - ug).astype(BF16)
            y_ref[:, cols] = y
            ho_ref[:, cols] = x[:, cols] + _dot(y, wp_ref[gi]) * sc_ref[:, cols]

    row = pl.BlockSpec((tm, D), lambda i: (i, 0))
    return pl.pallas_call(
        body, name="pool_fwd", grid=(T // tm,),
        in_specs=[row, pl.BlockSpec((POOL_HALO, D), lambda i: (jnp.maximum(i * hb - 1, 0), 0)),
                  _full((1, D)), _full((G, dg, dg)), _full((1, D)),
                  _full((G, tm, tm)), _full((G, tm, POOL_HALO))],
        out_specs=[row, row],
        out_shape=[jax.ShapeDtypeStruct((T, D), F32), jax.ShapeDtypeStruct((T, D), BF16)],
        compiler_params=_cparams("parallel"),
    )(h, h, g, wp, scale, bm, bh)


def _pool_bwd(dh, h, g, y, wp, scale):
    T, D = h.shape
    G, dg = len(POOL_WINDOWS), D // len(POOL_WINDOWS)
    tm = min(TOKEN_TILE, T)
    hb = tm // POOL_HALO
    nt = T // tm
    _, _, bmt, bht = _pool_bands(tm)

    def body(dh_ref, dhn_ref, h_ref, g_ref, y_ref, wp_ref, sc_ref, bmt_ref, bht_ref,
             o_ref, dg_ref, dwp_ref, dsc_ref, du_sc):
        i = pl.program_id(0)

        @pl.when(i == 0)
        def _():
            dg_ref[...] = jnp.zeros_like(dg_ref)
            dwp_ref[...] = jnp.zeros_like(dwp_ref)
            dsc_ref[...] = jnp.zeros_like(dsc_ref)

        dho = dh_ref[...]
        dz = dho * sc_ref[...]
        dzn = dhn_ref[...] * sc_ref[...] * (i < nt - 1).astype(F32)
        for gi, w in enumerate(POOL_WINDOWS):
            cols = slice(gi * dg, (gi + 1) * dg)
            yg = y_ref[:, cols]
            dzg = dz[:, cols].astype(BF16)
            dsc_ref[:, cols] += jnp.sum(dho[:, cols] * _dot(yg, wp_ref[gi]), axis=0, keepdims=True)
            dwp_ref[gi] += _dot_tn(yg, dzg)
            dy = _dot_nt(dzg, wp_ref[gi])
            dyn = _dot_nt(dzn[:, cols].astype(BF16), wp_ref[gi])
            hi, lo = _split_bf16(dy * _pool_count_scale(i, tm, w))
            nhi, nlo = _split_bf16(dyn)
            du_sc[:, cols] = (_dot(bmt_ref[gi], hi) + _dot(bmt_ref[gi], lo)
                              + _dot(bht_ref[gi], nhi) + _dot(bht_ref[gi], nlo) - dy)
        dx, dgp = _rms_bwd(du_sc[...], h_ref[...], g_ref[...])
        o_ref[...] = dho + dx
        dg_ref[...] += dgp

    row = pl.BlockSpec((tm, D), lambda i: (i, 0))
    return pl.pallas_call(
        body, name="pool_bwd", grid=(nt,),
        in_specs=[row, pl.BlockSpec((POOL_HALO, D), lambda i: (jnp.minimum((i + 1) * hb, T // POOL_HALO - 1), 0)),
                  row, _full((1, D)), row, _full((G, dg, dg)), _full((1, D)),
                  _full((G, tm, tm)), _full((G, tm, POOL_HALO))],
        out_specs=[row, _full((1, D)), _full((G, dg, dg)), _full((1, D))],
        out_shape=[jax.ShapeDtypeStruct((T, D), F32), jax.ShapeDtypeStruct((1, D), F32),
                   jax.ShapeDtypeStruct((G, dg, dg), F32), jax.ShapeDtypeStruct((1, D), F32)],
        scratch_shapes=[pltpu.VMEM((tm, D), F32)],
        compiler_params=_cparams("arbitrary"),
    )(dh, dh, h, g, y, wp, scale, bmt, bht)


def _rope_tables(T):
    pos = jnp.arange(T, dtype=F32)
    inv_freq = ROPE_THETA ** (-jnp.arange(0, D_ROPE, 2, dtype=F32) / D_ROPE)
    ang = pos[:, None] * inv_freq[None, :]
    cos2 = jnp.tile(jnp.cos(ang), (1, 2))
    sin2 = jnp.tile(jnp.sin(ang), (1, 2))
    pad = jnp.zeros((T, D_HEAD_PAD - D_QK), F32)
    ca_q = jnp.concatenate([jnp.ones((T, D_NOPE), F32), cos2, pad], axis=1)
    ca_k = jnp.concatenate([jnp.zeros((T, D_NOPE), F32), cos2, pad], axis=1)
    sb = jnp.concatenate([jnp.zeros((T, D_NOPE), F32), sin2, pad], axis=1)
    return ca_q, ca_k, sb


def _rope_weight_pair(w):
    half = D_ROPE // 2
    z_pad = jnp.zeros(w.shape[:-1] + (D_HEAD_PAD - D_QK,), w.dtype)
    z_nope = jnp.zeros(w.shape[:-1] + (D_NOPE,), w.dtype)
    wa = jnp.concatenate([w, z_pad], axis=-1)
    wb = jnp.concatenate([z_nope, -w[..., D_NOPE + half:], w[..., D_NOPE:D_NOPE + half], z_pad], axis=-1)
    return wa, wb


def _rope_weight_pair_grad(dwa, dwb):
    half = D_ROPE // 2
    d1 = dwa[..., D_NOPE:D_NOPE + half] + dwb[..., D_NOPE + half:D_QK]
    d2 = dwa[..., D_NOPE + half:D_QK] - dwb[..., D_NOPE:D_NOPE + half]
    return jnp.concatenate([dwa[..., :D_NOPE], d1, d2], axis=-1)


def _q_proj(cq, qg, wa, wb, ca, sb):
    T, R = cq.shape
    tm = min(TOKEN_TILE, T)
    P = D_HEAD_PAD

    def body(cq_ref, qg_ref, wa_ref, wb_ref, ca_ref, sb_ref, q_ref):
        c = _rms_fwd(cq_ref[...], qg_ref[...]).astype(BF16)
        q_ref[...] = (_dot(c, wa_ref[...]) * ca_ref[...] + _dot(c, wb_ref[...]) * sb_ref[...]).astype(BF16)

    tok = pl.BlockSpec((tm, P), lambda i, hh: (i, 0))
    wsp = pl.BlockSpec((R, P), lambda i, hh: (0, hh))
    return pl.pallas_call(
        body, name="q_proj", grid=(T // tm, N_HEADS),
        in_specs=[pl.BlockSpec((tm, R), lambda i, hh: (i, 0)), _full((1, R)), wsp, wsp, tok, tok],
        out_specs=pl.BlockSpec((tm, P), lambda i, hh: (i, hh)),
        out_shape=jax.ShapeDtypeStruct((T, N_HEADS * P), BF16),
        compiler_params=_cparams("parallel", "arbitrary"),
    )(cq, qg, wa, wb, ca, sb)


def _q_proj_bwd(dq, cq, qg, wa, wb, ca, sb):
    T, R = cq.shape
    tm = min(TOKEN_TILE, T)
    P = D_HEAD_PAD

    def body(dq_ref, cq_ref, qg_ref, wa_ref, wb_ref, ca_ref, sb_ref,
             da_ref, db_ref, cqn_ref, dcq_ref, dqg_ref, acc_sc):
        i, hh = pl.program_id(0), pl.program_id(1)

        @pl.when(hh == 0)
        def _():
            acc_sc[...] = jnp.zeros_like(acc_sc)
            cqn_ref[...] = _rms_fwd(cq_ref[...], qg_ref[...]).astype(BF16)

        d = dq_ref[...].T
        da = (d * ca_ref[...]).astype(BF16)
        db = (d * sb_ref[...]).astype(BF16)
        da_ref[...] = da
        db_ref[...] = db
        acc_sc[...] += _dot_nt(da, wa_ref[...]) + _dot_nt(db, wb_ref[...])

        @pl.when(hh == N_HEADS - 1)
        def _():
            dx, dg = _rms_bwd(acc_sc[...], cq_ref[...], qg_ref[...])
            dcq_ref[...] = dx

            @pl.when(i == 0)
            def _():
                dqg_ref[...] = jnp.zeros_like(dqg_ref)

            dqg_ref[...] += dg

    tok = pl.BlockSpec((tm, P), lambda i, hh: (i, 0))
    hd = pl.BlockSpec((tm, P), lambda i, hh: (i, hh))
    wsp = pl.BlockSpec((R, P), lambda i, hh: (0, hh))
    rr = pl.BlockSpec((tm, R), lambda i, hh: (i, 0))
    return pl.pallas_call(
        body, name="q_proj_bwd", grid=(T // tm, N_HEADS),
        in_specs=[pl.BlockSpec((P, tm), lambda i, hh: (hh, i)), rr, _full((1, R)), wsp, wsp, tok, tok],
        out_specs=[hd, hd, rr, rr, _full((1, R))],
        out_shape=[jax.ShapeDtypeStruct((T, N_HEADS * P), BF16), jax.ShapeDtypeStruct((T, N_HEADS * P), BF16),
                   jax.ShapeDtypeStruct((T, R), BF16), jax.ShapeDtypeStruct((T, R), F32),
                   jax.ShapeDtypeStruct((1, R), F32)],
        scratch_shapes=[pltpu.VMEM((tm, R), F32)],
        compiler_params=_cparams("arbitrary", "arbitrary"),
    )(dq, cq, qg, wa, wb, ca, sb)


def _kv_proj(h, g_in, wka, wkb, g_c, wuk, wuv, ca, sb):
    T, D = h.shape
    tm = min(TOKEN_TILE, T)
    P, C = D_HEAD_PAD, D_NOPE

    def body(h_ref, gi_ref, wka_ref, wkb_ref, gc_ref, wuk_ref, wuv_ref, ca_ref, sb_ref, k_ref, v_ref, craw_ref):
        u = _rms_fwd(h_ref[...], gi_ref[...]).astype(BF16)
        kva = _dot(u, wka_ref[...])
        kvb = _dot(u, wkb_ref[...])
        craw = kva[:, :C]
        craw_ref[...] = craw
        c = _rms_fwd(craw, gc_ref[...]).astype(BF16)
        kr = kva * ca_ref[...] + kvb * sb_ref[...]
        kn = _dot(c, wuk_ref[...])
        for hh in range(N_HEADS):
            k_ref[:, hh * P:(hh + 1) * P] = (kn[:, hh * P:(hh + 1) * P] + kr).astype(BF16)
        v_ref[...] = _dot(c, wuv_ref[...]).astype(BF16)

    tok = pl.BlockSpec((tm, P), lambda i: (i, 0))
    return pl.pallas_call(
        body, name="kv_proj", grid=(T // tm,),
        in_specs=[pl.BlockSpec((tm, D), lambda i: (i, 0)), _full((1, D)), _full((D, P)), _full((D, P)),
                  _full((1, C)), _full(wuk.shape), _full(wuv.shape), tok, tok],
        out_specs=[pl.BlockSpec((tm, N_HEADS * P), lambda i: (i, 0)),
                   pl.BlockSpec((tm, N_HEADS * D_V), lambda i: (i, 0)), pl.BlockSpec((tm, C), lambda i: (i, 0))],
        out_shape=[jax.ShapeDtypeStruct((T, N_HEADS * P), BF16), jax.ShapeDtypeStruct((T, N_HEADS * D_V), BF16),
                   jax.ShapeDtypeStruct((T, C), F32)],
        compiler_params=_cparams("parallel"),
    )(h, g_in, wka, wkb, g_c, wuk, wuv, ca, sb)


def _kv_proj_bwd(dk, dv, dh, h, g_in, wka, wkb, craw, g_c, wuk, wuv, ca, sb):
    T, D = h.shape
    tm = min(TOKEN_TILE // 2, T)
    P, C = D_HEAD_PAD, D_NOPE

    def body(dk_ref, dv_ref, dh_ref, h_ref, gi_ref, wka_ref, wkb_ref, craw_ref, gc_ref, wuk_ref, wuv_ref,
             ca_ref, sb_ref, o_ref, dgi_ref, dwka_ref, dwkb_ref, dgc_ref, dwuk_ref, dwuv_ref):
        @pl.when(pl.program_id(0) == 0)
        def _():
            for r in (dgi_ref, dwka_ref, dwkb_ref, dgc_ref, dwuk_ref, dwuv_ref):
                r[...] = jnp.zeros_like(r)

        x = h_ref[...]
        u = _rms_fwd(x, gi_ref[...]).astype(BF16)
        craw = craw_ref[...]
        c = _rms_fwd(craw, gc_ref[...]).astype(BF16)
        dkf = dk_ref[...]
        dkb = dkf.astype(BF16)
        dvb = dv_ref[...].astype(BF16)
        dwuk_ref[...] += _dot_tn(c, dkb)
        dwuv_ref[...] += _dot_tn(c, dvb)
        dc = _dot_nt(dkb, wuk_ref[...]) + _dot_nt(dvb, wuv_ref[...])
        dkr = dkf[:, :P]
        for hh in range(1, N_HEADS):
            dkr = dkr + dkf[:, hh * P:(hh + 1) * P]
        dcraw, dgc = _rms_bwd(dc, craw, gc_ref[...])
        dgc_ref[...] += dgc
        dkva = jnp.concatenate([dcraw, (dkr * ca_ref[...])[:, C:]], axis=1).astype(BF16)
        dkvb = (dkr * sb_ref[...]).astype(BF16)
        dwka_ref[...] += _dot_tn(u, dkva)
        dwkb_ref[...] += _dot_tn(u, dkvb)
        du = _dot_nt(dkva, wka_ref[...]) + _dot_nt(dkvb, wkb_ref[...])
        dx, dgi = _rms_bwd(du, x, gi_ref[...])
        dgi_ref[...] += dgi
        o_ref[...] = dh_ref[...] + dx

    row = pl.BlockSpec((tm, D), lambda i: (i, 0))
    tok = pl.BlockSpec((tm, P), lambda i: (i, 0))
    return pl.pallas_call(
        body, name="kv_proj_bwd", grid=(T // tm,),
        in_specs=[pl.BlockSpec((tm, N_HEADS * P), lambda i: (i, 0)),
                  pl.BlockSpec((tm, N_HEADS * D_V), lambda i: (i, 0)), row, row, _full((1, D)),
                  _full((D, P)), _full((D, P)), pl.BlockSpec((tm, C), lambda i: (i, 0)), _full((1, C)),
                  _full(wuk.shape), _full(wuv.shape), tok, tok],
        out_specs=[row, _full((1, D)), _full((D, P)), _full((D, P)), _full((1, C)),
                   _full(wuk.shape), _full(wuv.shape)],
        out_shape=[jax.ShapeDtypeStruct((T, D), F32), jax.ShapeDtypeStruct((1, D), F32),
                   jax.ShapeDtypeStruct((D, P), F32), jax.ShapeDtypeStruct((D, P), F32),
                   jax.ShapeDtypeStruct((1, C), F32), jax.ShapeDtypeStruct(wuk.shape, F32),
                   jax.ShapeDtypeStruct(wuv.shape, F32)],
        compiler_params=_cparams("arbitrary"),
    )(dk, dv, dh, h, g_in, wka, wkb, craw, g_c, wuk, wuv, ca, sb)


_ATTN_SCALE = D_QK ** -0.5
_LOG2_E = 1.4426950408889634
_ATTN_SCALE_LOG2 = _ATTN_SCALE * _LOG2_E


def _causal_mask(t):
    return lax.broadcasted_iota(jnp.int32, (t, t), 1) <= lax.broadcasted_iota(jnp.int32, (t, t), 0)


def _causal_mask_t(t):
    return lax.broadcasted_iota(jnp.int32, (t, t), 0) <= lax.broadcasted_iota(jnp.int32, (t, t), 1)


def _flash_fwd(q, k, v):
    T = q.shape[0]
    t = min(ATTN_TILE, T)
    P = D_HEAD_PAD

    def body(q_ref, k_ref, v_ref, o_ref, lse_ref, m_sc, l_sc, acc_sc):
        qi = pl.program_id(1)
        qq = q_ref[...]
        m_sc[...] = jnp.full_like(m_sc, NEG_BIG)
        l_sc[...] = jnp.zeros_like(l_sc)
        acc_sc[...] = jnp.zeros_like(acc_sc)

        def rows_of(ki):
            return pl.ds(pl.multiple_of(ki * t, t), t)

        def scores(ki):
            return _dot_nt(k_ref[rows_of(ki), :], qq)

        def accumulate(ki, s_t):
            m_prev = m_sc[...]
            m_new = jnp.maximum(m_prev, jnp.max(s_t, axis=0, keepdims=True))
            p_t = jnp.exp2((s_t - m_new) * _ATTN_SCALE_LOG2)
            alpha = jnp.exp2((m_prev - m_new) * _ATTN_SCALE_LOG2)
            l_sc[...] = alpha * l_sc[...] + jnp.sum(p_t, axis=0, keepdims=True)
            acc_sc[...] = alpha * acc_sc[...] + _dot_tn(v_ref[rows_of(ki), :], p_t.astype(BF16))
            m_sc[...] = m_new

        def loop_body(ki, s_t):
            s_next = scores(ki + 1)
            accumulate(ki, s_t)
            return s_next

        s_last = lax.fori_loop(0, qi, loop_body, scores(0))
        accumulate(qi, jnp.where(_causal_mask_t(t), s_last, NEG_BIG))
        l = l_sc[...]
        o_ref[...] = (acc_sc[...] / l).T.astype(BF16)
        lse_ref[...] = m_sc[...] * _ATTN_SCALE + jnp.log(l)

    return pl.pallas_call(
        body, name="flash_fwd", grid=(N_HEADS, T // t),
        in_specs=[pl.BlockSpec((t, P), lambda hh, i: (i, hh)), pl.BlockSpec((T, P), lambda hh, i: (0, hh)),
                  pl.BlockSpec((T, D_V), lambda hh, i: (0, hh))],
        out_specs=[pl.BlockSpec((t, D_V), lambda hh, i: (i, hh)),
                   pl.BlockSpec((None, None, 1, t), lambda hh, i: (hh, i, 0, 0))],
        out_shape=[jax.ShapeDtypeStruct((T, N_HEADS * D_V), BF16),
                   jax.ShapeDtypeStruct((N_HEADS, T // t, 1, t), F32)],
        scratch_shapes=[pltpu.VMEM((1, t), F32), pltpu.VMEM((1, t), F32), pltpu.VMEM((D_V, t), F32)],
        compiler_params=_cparams("parallel", "arbitrary"),
    )(q, k, v)


def _attn_delta(o, do):
    T = o.shape[0]
    t = min(ATTN_TILE, T)

    def body(o_ref, do_ref, out_ref):
        hi, lo = _split_bf16(o_ref[...].astype(F32) * do_ref[...].astype(F32))
        ones = jnp.ones((8, D_V), BF16)
        out_ref[...] = (_dot_nt(ones, hi) + _dot_nt(ones, lo))[0:1]

    hd_v = pl.BlockSpec((t, D_V), lambda hh, i: (i, hh))
    return pl.pallas_call(
        body, name="attn_delta", grid=(N_HEADS, T // t), in_specs=[hd_v, hd_v],
        out_specs=pl.BlockSpec((None, None, 1, t), lambda hh, i: (hh, i, 0, 0)),
        out_shape=jax.ShapeDtypeStruct((N_HEADS, T // t, 1, t), F32),
        compiler_params=_cparams("parallel", "parallel"),
    )(o, do)


def _flash_bwd(q, k, v, do, lse_row, delta_row, dk_prev, dv_prev):
    T = q.shape[0]
    t = min(ATTN_TILE, T)
    nq = T // t
    P = D_HEAD_PAD

    def body(k_ref, v_ref, q_ref, do_ref, lse_ref, delta_ref, dkp_ref, dvp_ref,
             dqt_ref, dk_ref, dv_ref, dk_sc, dv_sc, dqt_sc):
        ki = pl.program_id(1)
        kk = k_ref[...]
        vv = v_ref[...]
        kk_t = kk.astype(F32).T.astype(BF16)
        dk_sc[...] = dkp_ref[...]
        dv_sc[...] = dvp_ref[...]

        @pl.when(ki == 0)
        def _():
            dqt_sc[...] = jnp.zeros_like(dqt_sc)

        def chunk(qi, diagonal):
            rows = pl.ds(pl.multiple_of(qi * t, t), t)
            qq = q_ref[rows, :]
            dd = do_ref[rows, :]
            p_t = jnp.exp2(_dot_nt(kk, qq) * _ATTN_SCALE_LOG2 - lse_ref[qi] * _LOG2_E)
            if diagonal:
                p_t = jnp.where(_causal_mask_t(t), p_t, 0.0)
            dv_sc[...] += _dot(p_t.astype(BF16), dd)
            dp_t = _dot_nt(vv, dd)
            ds_t = (p_t * (dp_t - delta_ref[qi]) * _ATTN_SCALE).astype(BF16)
            dk_sc[...] += _dot(ds_t, qq)
            dqt_sc[qi] += _dot(kk_t, ds_t)

        def loop_body(qi, carry):
            chunk(qi, False)
            return carry

        chunk(ki, True)
        lax.fori_loop(ki + 1, nq, loop_body, 0)
        dqt_ref[...] = dqt_sc[ki]
        dk_ref[...] = dk_sc[...]
        dv_ref[...] = dv_sc[...]

    kb = pl.BlockSpec((t, P), lambda hh, i: (i, hh))
    vb = pl.BlockSpec((t, D_V), lambda hh, i: (i, hh))
    stat = pl.BlockSpec((None, nq, 1, t), lambda hh, i: (hh, 0, 0, 0))
    return pl.pallas_call(
        body, name="flash_bwd", grid=(N_HEADS, nq),
        in_specs=[kb, vb, pl.BlockSpec((T, P), lambda hh, i: (0, hh)), pl.BlockSpec((T, D_V), lambda hh, i: (0, hh)),
                  stat, stat, kb, vb],
        out_specs=[pl.BlockSpec((P, t), lambda hh, i: (hh, i)), kb, vb],
        out_shape=[jax.ShapeDtypeStruct((N_HEADS * P, T), F32), jax.ShapeDtypeStruct((T, N_HEADS * P), F32),
                   jax.ShapeDtypeStruct((T, N_HEADS * D_V), F32)],
        scratch_shapes=[pltpu.VMEM((t, P), F32), pltpu.VMEM((t, D_V), F32), pltpu.VMEM((nq, P, t), F32)],
        compiler_params=_cparams("arbitrary", "arbitrary"),
    )(k, v, q, do, lse_row, delta_row, dk_prev, dv_prev)


def _loss_head(h, g, target):
    T, D = h.shape
    tm = min(TOKEN_TILE, T)

    def body(h_ref, g_ref, t_ref, dh_ref, loss_ref, dg_ref):
        @pl.when(pl.program_id(0) == 0)
        def _():
            loss_ref[...] = jnp.zeros_like(loss_ref)
            dg_ref[...] = jnp.zeros_like(dg_ref)

        x = h_ref[...]
        err = _rms_fwd(x, g_ref[...]) - t_ref[...]
        per_tok = jnp.mean(err * err, axis=-1, keepdims=True)
        loss_ref[...] += 0.5 * jnp.sum(per_tok, axis=0, keepdims=True)
        dx, dg = _rms_bwd(err * (1.0 / D), x, g_ref[...])
        dh_ref[...] = dx
        dg_ref[...] += dg

    row = pl.BlockSpec((tm, D), lambda i: (i, 0))
    return pl.pallas_call(
        body, name="loss_head", grid=(T // tm,),
        in_specs=[row, _full((1, D)), row], out_specs=[row, _full((1, 128)), _full((1, D))],
        out_shape=[jax.ShapeDtypeStruct((T, D), F32), jax.ShapeDtypeStruct((1, 128), F32),
                   jax.ShapeDtypeStruct((1, D), F32)],
        compiler_params=_cparams("arbitrary"),
    )(h, g, target)


def _sum_parts(parts, tr, name):
    _, R, C = parts.shape

    def body(p_ref, o_ref):
        acc = p_ref[0].astype(F32)
        for j in range(1, N_DEV):
            acc = acc + p_ref[j].astype(F32)
        o_ref[...] = acc

    return pl.pallas_call(
        body, name=name, grid=(R // tr,),
        in_specs=[pl.BlockSpec((N_DEV, tr, C), lambda i: (0, i, 0))],
        out_specs=pl.BlockSpec((tr, C), lambda i: (i, 0)),
        out_shape=jax.ShapeDtypeStruct((R, C), F32),
        compiler_params=_cparams("parallel"),
    )(parts)


def _adamw(w, g, m, v):
    R, C = w.shape
    tr = _row_tile(R, TOKEN_TILE)

    def body(w_ref, g_ref, m_ref, v_ref, d_ref, mo_ref, vo_ref):
        gg = g_ref[...]
        mn = ADAM_B1 * m_ref[...] + (1.0 - ADAM_B1) * gg
        vn = ADAM_B2 * v_ref[...] + (1.0 - ADAM_B2) * (gg * gg)
        m_hat = mn / (1.0 - ADAM_B1 ** ADAM_STEP)
        v_hat = vn / (1.0 - ADAM_B2 ** ADAM_STEP)
        d_ref[...] = -ADAM_LR * (m_hat / (jnp.sqrt(v_hat) + ADAM_EPS) + ADAM_WD * w_ref[...])
        mo_ref[...] = mn
        vo_ref[...] = vn

    blk = pl.BlockSpec((tr, C), lambda i: (i, 0))
    return pl.pallas_call(
        body, name="adamw", grid=(R // tr,), in_specs=[blk] * 4, out_specs=[blk] * 3,
        out_shape=[jax.ShapeDtypeStruct((R, C), F32)] * 3,
        compiler_params=_cparams("parallel"),
    )(w, g, m, v)


def _adamw_nd(w, g, m, v):
    shape = w.shape
    two_d = (1, shape[0]) if len(shape) == 1 else (int(np.prod(shape[:-1])), shape[-1])
    outs = _adamw(w.reshape(two_d), g.reshape(two_d), m.reshape(two_d), v.reshape(two_d))
    return tuple(o.reshape(shape) for o in outs)


def _f32_as_bf16_pairs(a):
    return lax.bitcast_convert_type(a, BF16).reshape(a.shape[:-1] + (a.shape[-1] * 2,))


def _bf16_pairs_as_f32(a):
    return lax.bitcast_convert_type(a.reshape(a.shape[:-1] + (a.shape[-1] // 2, 2)), F32)


def _pack_misc(w_o, w_dq, w_uq, w_dkv, pool_w, pool_scale):
    lead = w_o.shape[:-3]
    rows = [w_o, w_dq, w_uq, w_dkv, pool_w]
    flat = [r.astype(BF16).reshape(lead + (-1, REP_COLS)) for r in rows]
    ps = _f32_as_bf16_pairs(pool_scale.astype(F32)).reshape(lead + (1, -1))
    ps = jnp.concatenate([ps, jnp.zeros(lead + (1, REP_COLS - ps.shape[-1]), BF16)], axis=-1)
    used = sum(f.shape[-2] for f in flat) + 1
    pad = jnp.zeros(lead + (MISC_ROWS - used, REP_COLS), BF16)
    return jnp.concatenate(flat + [ps, pad], axis=-2)


def _unpack_misc(buf, shapes):
    out, r0 = [], 0
    for shp in shapes[:-1]:
        n = int(np.prod(shp)) // REP_COLS
        out.append(buf[:, r0:r0 + n].reshape((N_DEV,) + shp))
        r0 += n
    n_ps = int(np.prod(shapes[-1]))
    out.append(_bf16_pairs_as_f32(buf[:, r0, :2 * n_ps]).reshape((N_DEV,) + shapes[-1]))
    return out


def _cat_dev(a, axis):
    a = jnp.moveaxis(a, 0, axis)
    return a.reshape(a.shape[:axis] + (a.shape[axis] * a.shape[axis + 1],) + a.shape[axis + 2:])


def _split_dev(a, axis):
    a = a.reshape(a.shape[:axis] + (N_DEV, a.shape[axis] // N_DEV) + a.shape[axis + 1:])
    return jnp.moveaxis(a, axis, 0)


def kernel(x, ffn_pre_norm, ffn_pre_wg, ffn_pre_wu, ffn_pre_wd, mix_norm, ffn_post_norm, ffn_post_wg, ffn_post_wu, ffn_post_wd, pool_w, pool_scale, kv_in_norm, w_dkv, ckv_norm, w_uk, w_uv, q_lora_norm, w_dq, w_uq, w_o, final_norm, loss_target, m_ffn_pre_norm, m_ffn_pre_wg, m_ffn_pre_wu, m_ffn_pre_wd, m_mix_norm, m_ffn_post_norm, m_ffn_post_wg, m_ffn_post_wu, m_ffn_post_wd, m_pool_w, m_pool_scale, m_kv_in_norm, m_w_dkv, m_ckv_norm, m_w_uk, m_w_uv, m_q_lora_norm, m_w_dq, m_w_uq, m_w_o, m_final_norm, v_ffn_pre_norm, v_ffn_pre_wg, v_ffn_pre_wu, v_ffn_pre_wd, v_mix_norm, v_ffn_post_norm, v_ffn_post_wg, v_ffn_post_wu, v_ffn_post_wd, v_pool_w, v_pool_scale, v_kv_in_norm, v_w_dkv, v_ckv_norm, v_w_uk, v_w_uv, v_q_lora_norm, v_w_dq, v_w_uq, v_w_o, v_final_norm):
    weights = dict(ffn_pre_norm=ffn_pre_norm, ffn_pre_wg=ffn_pre_wg, ffn_pre_wu=ffn_pre_wu, ffn_pre_wd=ffn_pre_wd,
                   mix_norm=mix_norm, ffn_post_norm=ffn_post_norm, ffn_post_wg=ffn_post_wg,
                   ffn_post_wu=ffn_post_wu, ffn_post_wd=ffn_post_wd, pool_w=pool_w, pool_scale=pool_scale,
                   kv_in_norm=kv_in_norm, w_dkv=w_dkv, ckv_norm=ckv_norm, w_uk=w_uk, w_uv=w_uv,
                   q_lora_norm=q_lora_norm, w_dq=w_dq, w_uq=w_uq, w_o=w_o, final_norm=final_norm)
    moments_m = dict(ffn_pre_norm=m_ffn_pre_norm, ffn_pre_wg=m_ffn_pre_wg, ffn_pre_wu=m_ffn_pre_wu,
                     ffn_pre_wd=m_ffn_pre_wd, mix_norm=m_mix_norm, ffn_post_norm=m_ffn_post_norm,
                     ffn_post_wg=m_ffn_post_wg, ffn_post_wu=m_ffn_post_wu, ffn_post_wd=m_ffn_post_wd,
                     pool_w=m_pool_w, pool_scale=m_pool_scale, kv_in_norm=m_kv_in_norm, w_dkv=m_w_dkv,
                     ckv_norm=m_ckv_norm, w_uk=m_w_uk, w_uv=m_w_uv, q_lora_norm=m_q_lora_norm, w_dq=m_w_dq,
                     w_uq=m_w_uq, w_o=m_w_o, final_norm=m_final_norm)
    moments_v = dict(ffn_pre_norm=v_ffn_pre_norm, ffn_pre_wg=v_ffn_pre_wg, ffn_pre_wu=v_ffn_pre_wu,
                     ffn_pre_wd=v_ffn_pre_wd, mix_norm=v_mix_norm, ffn_post_norm=v_ffn_post_norm,
                     ffn_post_wg=v_ffn_post_wg, ffn_post_wu=v_ffn_post_wu, ffn_post_wd=v_ffn_post_wd,
                     pool_w=v_pool_w, pool_scale=v_pool_scale, kv_in_norm=v_kv_in_norm, w_dkv=v_w_dkv,
                     ckv_norm=v_ckv_norm, w_uk=v_w_uk, w_uv=v_w_uv, q_lora_norm=v_q_lora_norm, w_dq=v_w_dq,
                     w_uq=v_w_uq, w_o=v_w_o, final_norm=v_final_norm)
    order = list(weights)

    T, D = x.shape[1], x.shape[2]
    depth = ffn_pre_norm.shape[0]
    n_a = pool_w.shape[0]
    n_b = depth - n_a
    fs = ffn_pre_wd.shape[1]
    F = fs * N_DEV
    n_ffn = 2 * depth
    t_attn = min(ATTN_TILE, T)

    ffn_local = jnp.stack([
        jnp.stack([jnp.swapaxes(wg[l], 0, 1), jnp.swapaxes(wu[l], 0, 1), wd[l]])
        for l in range(depth)
        for wg, wu, wd in ((ffn_pre_wg, ffn_pre_wu, ffn_pre_wd), (ffn_post_wg, ffn_post_wu, ffn_post_wd))
    ]).astype(BF16)
    misc_local = _pack_misc(w_o, w_dq, w_uq.reshape(n_b, w_uq.shape[1], -1), w_dkv, pool_w, pool_scale)
    misc_shapes = (w_o.shape, w_dq.shape, (n_b, w_uq.shape[1], N_HEADS * D_QK), w_dkv.shape, pool_w.shape,
                   pool_scale.shape)
    ffn_all, misc_all = _all_gather(ffn_local.reshape(n_ffn * 3, fs, D), misc_local)
    wall = ffn_all.reshape(n_ffn, 3, F, D)
    o_blk, dq_blk, uq_blk, dkv_blk, pw_blk, ps_blk = _unpack_misc(misc_all, misc_shapes)
    w_o_f = _cat_dev(o_blk, 1)
    w_dq_f = _cat_dev(dq_blk, 1)
    w_uq_f = _cat_dev(uq_blk, 1).reshape(n_b, -1, N_HEADS, D_QK)
    w_dkv_f = _cat_dev(dkv_blk, 0)
    pool_w_f = _cat_dev(pw_blk, 2)
    pool_scale_f = _cat_dev(ps_blk, 1)
    rq = w_dq_f.shape[2]
    wqa, wqb = _rope_weight_pair(w_uq_f)
    wqa = wqa.reshape(n_b, rq, N_HEADS * D_HEAD_PAD)
    wqb = wqb.reshape(n_b, rq, N_HEADS * D_HEAD_PAD)
    wka, wkb = _rope_weight_pair(w_dkv_f)
    wuk = jnp.concatenate([w_uk, jnp.zeros_like(w_uk)], axis=-1).astype(BF16).reshape(D_NOPE, N_HEADS * D_HEAD_PAD)
    wuv = w_uv.astype(BF16).reshape(D_NOPE, N_HEADS * D_V)
    ca_q, ca_k, sb = _rope_tables(T)

    def vec(a):
        return a.reshape(1, -1)

    h = x.reshape(T, D)
    saved = []
    k_all = v_all = craw = h_kv = None
    for l in range(depth):
        s = {"h0": h}
        h, s["g1"], s["u1"] = _ffn_fwd(h, vec(ffn_pre_norm[l]), wall, 2 * l)
        s["h1"] = h
        if l < n_a:
            h, s["y"] = _pool_fwd(h, vec(mix_norm[l]), pool_w_f[l], vec(pool_scale_f[l]))
        else:
            j = l - n_a
            s["cq"] = _mm_rows(h, w_dq_f[j], nt=False, out_dtype=F32, norm_g=vec(mix_norm[l]), name="q_down")
            s["q"] = _q_proj(s["cq"], vec(q_lora_norm[j]), wqa[j], wqb[j], ca_q, sb)
            s["o"], s["lse"] = _flash_fwd(s["q"], k_all, v_all)
            h = _mm_rows(s["o"], w_o_f[j], nt=False, out_dtype=F32, res=h, name="attn_out")
        s["h2"] = h
        h, s["g2"], s["u2"] = _ffn_fwd(h, vec(ffn_post_norm[l]), wall, 2 * l + 1)
        if l == n_a - 1:
            h_kv = h
            k_all, v_all, craw = _kv_proj(h, vec(kv_in_norm), wka, wkb, vec(ckv_norm), wuk, wuv, ca_k, sb)
        saved.append(s)

    dh, loss_part, d_final = _loss_head(h, vec(final_norm), loss_target.reshape(T, D))

    gbuf = jnp.zeros((n_ffn, 3, F, D), BF16)
    grads = {}
    d_pre, d_post, d_mix = [None] * depth, [None] * depth, [None] * depth
    d_pool_w, d_pool_scale = [None] * n_a, [None] * n_a
    d_qln, d_wdq, d_wuq, d_wo = [None] * n_b, [None] * n_b, [None] * n_b, [None] * n_b
    dk_acc = jnp.zeros((T, N_HEADS * D_HEAD_PAD), F32)
    dv_acc = jnp.zeros((T, N_HEADS * D_V), F32)
    for l in reversed(range(depth)):
        s = saved[l]
        if l == n_a - 1:
            (dh, grads["kv_in_norm"], dwka, dwkb, grads["ckv_norm"], dwuk, dwuv) = _kv_proj_bwd(
                dk_acc, dv_acc, dh, h_kv, vec(kv_in_norm), wka, wkb, craw, vec(ckv_norm), wuk, wuv, ca_k, sb)
            grads["w_dkv"] = _rope_weight_pair_grad(dwka, dwkb)
            grads["w_uk"] = dwuk.reshape(D_NOPE, N_HEADS, D_HEAD_PAD)[..., :D_NOPE]
            grads["w_uv"] = dwuv.reshape(D_NOPE, N_HEADS, D_V)
        dh, dgt, dup, u_b, dy_b, d_post[l] = _ffn_bwd_dx(dh, s["h2"], vec(ffn_post_norm[l]), s["g2"], s["u2"],
                                                       wall, 2 * l + 1)
        gbuf = _ffn_bwd_dw(dgt, dup, s["g2"], s["u2"], u_b, dy_b, gbuf, 2 * l + 1)
        if l < n_a:
            dh, d_mix[l], d_pool_w[l], d_pool_scale[l] = _pool_bwd(
                dh, s["h1"], vec(mix_norm[l]), s["y"], pool_w_f[l], vec(pool_scale_f[l]))
        else:
            j = l - n_a
            d_wo[j] = _mm_tn(s["o"], dh, name="attn_out_dw")
            do = _mm_rows(dh, w_o_f[j], nt=True, out_dtype=BF16, name="attn_out_dx")
            delta_row = _attn_delta(s["o"], do)
            dq_t, dk_acc, dv_acc = _flash_bwd(s["q"], k_all, v_all, do, s["lse"], delta_row, dk_acc, dv_acc)
            da, db, cqn, dcq, d_qln[j] = _q_proj_bwd(dq_t, s["cq"], vec(q_lora_norm[j]), wqa[j], wqb[j], ca_q, sb)
            dwa = _mm_tn(cqn, da, name="q_up_dw")
            dwb = _mm_tn(cqn, db, name="q_up_dw")
            d_wuq[j] = _rope_weight_pair_grad(dwa.reshape(rq, N_HEADS, D_HEAD_PAD),
                                              dwb.reshape(rq, N_HEADS, D_HEAD_PAD))
            d_wdq[j] = _mm_tn(s["h1"], dcq, norm_g=vec(mix_norm[l]), name="q_down_dw")
            dh, d_mix[l] = _proj_bwd(dcq, w_dq_f[j], s["h1"], vec(mix_norm[l]), dh, "q_down_dx")
        dh, dgt, dup, u_b, dy_b, d_pre[l] = _ffn_bwd_dx(dh, s["h0"], vec(ffn_pre_norm[l]), s["g1"], s["u1"],
                                                      wall, 2 * l)
        gbuf = _ffn_bwd_dw(dgt, dup, s["g1"], s["u1"], u_b, dy_b, gbuf, 2 * l)
    grad_x = dh.reshape(x.shape)

    rep_names = ["ffn_pre_norm", "mix_norm", "ffn_post_norm", "kv_in_norm", "ckv_norm", "q_lora_norm",
                 "final_norm", "w_uk", "w_uv"]
    grads["ffn_pre_norm"] = jnp.concatenate(d_pre, axis=0)
    grads["mix_norm"] = jnp.concatenate(d_mix, axis=0)
    grads["ffn_post_norm"] = jnp.concatenate(d_post, axis=0)
    grads["q_lora_norm"] = jnp.concatenate(d_qln, axis=0)
    grads["final_norm"] = d_final
    rep_flat = jnp.concatenate([grads[n].reshape(-1) for n in rep_names] + [loss_part[0, :1]])
    n_rep = rep_flat.shape[0]
    rep_rows = -(-n_rep // (8 * REP_COLS)) * 8
    rep_g = jnp.concatenate([rep_flat, jnp.zeros((rep_rows * REP_COLS - n_rep,), F32)]).reshape(rep_rows, REP_COLS)
    misc_g = _pack_misc(_split_dev(jnp.stack(d_wo), 1), _split_dev(jnp.stack(d_wdq), 1),
                        _split_dev(jnp.stack(d_wuq).reshape(n_b, rq, -1), 1), _split_dev(grads["w_dkv"], 0),
                        _split_dev(jnp.stack(d_pool_w), 2),
                        _split_dev(jnp.concatenate(d_pool_scale, axis=0), 1))
    ffn_parts, misc_parts, rep_parts = _grad_exchange(gbuf.reshape(n_ffn * 3, N_DEV, fs, D), misc_g, rep_g)
    ffn_sum = _sum_parts(ffn_parts.reshape(N_DEV, n_ffn * 3 * fs, D), fs, "sum_ffn").reshape(n_ffn, 3, fs, D)
    misc_sum_parts = _unpack_misc(misc_parts, misc_shapes)
    rep_sum = _sum_parts(rep_parts, _row_tile(rep_rows, 128), "sum_rep").reshape(-1)

    def sum_small(p):
        shp = p.shape[1:]
        two_d = (int(np.prod(shp[:-1])), shp[-1])
        return _sum_parts(p.reshape((N_DEV,) + two_d), two_d[0], "sum_misc").reshape(shp)

    g_wo, g_wdq, g_wuq, g_wdkv, g_pw, g_ps = [sum_small(p) for p in misc_sum_parts]
    grads.update(w_o=g_wo, w_dq=g_wdq, w_uq=g_wuq.reshape(w_uq.shape), w_dkv=g_wdkv, pool_w=g_pw, pool_scale=g_ps)
    for kind, (npre, npost) in enumerate((("ffn_pre_wg", "ffn_post_wg"), ("ffn_pre_wu", "ffn_post_wu"),
                                          ("ffn_pre_wd", "ffn_post_wd"))):
        pre = ffn_sum[0::2, kind]
        post = ffn_sum[1::2, kind]
        if kind < 2:
            pre, post = jnp.swapaxes(pre, 1, 2), jnp.swapaxes(post, 1, 2)
        grads[npre], grads[npost] = pre, post
    off = 0
    for n in rep_names:
        size = int(np.prod(weights[n].shape))
        grads[n] = rep_sum[off:off + size].reshape(weights[n].shape)
        off += size
    loss = rep_sum[off]

    deltas, new_m, new_v = {}, {}, {}
    for n in order:
        deltas[n], new_m[n], new_v[n] = _adamw_nd(weights[n], grads[n], moments_m[n], moments_v[n])
    return (loss, grad_x, *[grads[n] for n in order], *[deltas[n] for n in order],
            *[new_m[n] for n in order], *[new_v[n] for n in order])
```

```python
import functools

import numpy as np
import jax
import jax.numpy as jnp
from jax import lax
from jax.experimental import pallas as pl
from jax.experimental.pallas import tpu as pltpu

F32, BF16 = jnp.float32, jnp.bfloat16
N_DEV = 8
RMS_EPS = 1e-6
N_HEADS = 16
D_NOPE, D_ROPE, D_V = 128, 64, 128
D_QK = D_NOPE + D_ROPE
D_HEAD_PAD = 256
ROPE_THETA = 10000.0
POOL_WINDOWS = (2, 4, 8, 16)
POOL_HALO = 16
ADAM_LR, ADAM_B1, ADAM_B2, ADAM_EPS, ADAM_WD, ADAM_STEP = 0.001, 0.9, 0.999, 1e-08, 0.01, 10
NEG_BIG = -1e30
V7X_VMEM_LIMIT = 56 * 1024 * 1024
TOKEN_TILE = 512
ATTN_TILE = 512
FFN_TILE = 256
MISC_ROWS = 864
REP_COLS = 1024


def _cparams(*sem):
    return pltpu.CompilerParams(dimension_semantics=sem, vmem_limit_bytes=V7X_VMEM_LIMIT)


def _dot(a, b):
    return lax.dot_general(a, b, (((1,), (0,)), ((), ())), preferred_element_type=F32)


def _dot_nt(a, b):
    return lax.dot_general(a, b, (((1,), (1,)), ((), ())), preferred_element_type=F32)


def _dot_tn(a, b):
    return lax.dot_general(a, b, (((0,), (0,)), ((), ())), preferred_element_type=F32)


def _rms_fwd(x, g):
    r = lax.rsqrt(jnp.mean(x * x, axis=-1, keepdims=True) + RMS_EPS)
    return (x * r) * g


def _rms_bwd(du, x, g):
    r = lax.rsqrt(jnp.mean(x * x, axis=-1, keepdims=True) + RMS_EPS)
    xh = x * r
    dg = jnp.sum(du * xh, axis=0, keepdims=True)
    dxh = du * g
    dx = r * (dxh - xh * jnp.mean(dxh * xh, axis=-1, keepdims=True))
    return dx, dg


def _sigmoid(x):
    return 1.0 / (1.0 + jnp.exp(-x))


def _split_bf16(x):
    hi = x.astype(BF16)
    lo = (x - hi.astype(F32)).astype(BF16)
    return hi, lo


def _full(shape):
    return pl.BlockSpec(shape, lambda *_: (0,) * len(shape))


def _row_tile(rows, cap):
    for t in range(min(cap, rows) // 8 * 8, 0, -8):
        if rows % t == 0:
            return t
    return rows


def _peers():
    x, y, c = lax.axis_index("x"), lax.axis_index("y"), lax.axis_index("c")
    out = []
    for k in range(1, N_DEV):
        px = 1 - x if (k >> 2) & 1 else x
        py = 1 - y if (k >> 1) & 1 else y
        pc = 1 - c if k & 1 else c
        out.append(((px, py, pc), 4 * px + 2 * py + pc))
    return 4 * x + 2 * y + c, out


def _exchange(arrays, src_of, dst_of, out_shapes, name):
    n = len(arrays)

    def body(*refs):
        ins, outs = refs[:n], refs[n:2 * n]
        send_sems, recv_sems, loc_sems = refs[2 * n:]
        me, peers = _peers()
        own = [pltpu.make_async_copy(src_of(j, ins[j], me), dst_of(j, outs[j], me), loc_sems.at[j])
               for j in range(n)]
        for cp in own:
            cp.start()
        sends = []
        for k, (peer, pidx) in enumerate(peers):
            for j in range(n):
                sends.append(pltpu.make_async_remote_copy(
                    src_ref=src_of(j, ins[j], pidx), dst_ref=dst_of(j, outs[j], me),
                    send_sem=send_sems.at[j, k], recv_sem=recv_sems.at[j, k],
                    device_id=peer, device_id_type=pl.DeviceIdType.MESH))
        for cp in sends:
            cp.start()
        for k, (peer, pidx) in enumerate(peers):
            for j in range(n):
                pltpu.make_async_remote_copy(
                    src_ref=src_of(j, ins[j], pidx), dst_ref=dst_of(j, outs[j], pidx),
                    send_sem=send_sems.at[j, k], recv_sem=recv_sems.at[j, k],
                    device_id=peer, device_id_type=pl.DeviceIdType.MESH).wait_recv()
        for cp in sends:
            cp.wait_send()
        for cp in own:
            cp.wait()

    any_spec = pl.BlockSpec(memory_space=pl.ANY)
    return pl.pallas_call(
        body, name=name,
        out_shape=[jax.ShapeDtypeStruct(s, a.dtype) for s, a in zip(out_shapes, arrays)],
        in_specs=[any_spec] * n, out_specs=[any_spec] * n,
        scratch_shapes=[pltpu.SemaphoreType.DMA((n, N_DEV - 1)), pltpu.SemaphoreType.DMA((n, N_DEV - 1)),
                        pltpu.SemaphoreType.DMA((n,))],
    )(*arrays)


def _all_gather(ffn_local, misc_local):
    w, r, d = ffn_local.shape

    def src_of(j, ref, idx):
        return ref

    def dst_of(j, ref, idx):
        return ref.at[:, idx] if j == 0 else ref.at[idx]

    return _exchange([ffn_local, misc_local], src_of, dst_of,
                     [(w, N_DEV, r, d), (N_DEV,) + misc_local.shape], "comm_all_gather")


def _grad_exchange(ffn_g, misc_g, rep_g):
    w, _, r, d = ffn_g.shape

    def src_of(j, ref, idx):
        return (ref.at[:, idx], ref.at[idx], ref)[j]

    def dst_of(j, ref, idx):
        return ref.at[idx]

    return _exchange([ffn_g, misc_g, rep_g], src_of, dst_of,
                     [(N_DEV, w, r, d), misc_g.shape, (N_DEV,) + rep_g.shape], "comm_grad_exchange")


def _ffn_fwd(h, g, wall, e):
    T, D = h.shape
    F = wall.shape[2]
    tm, tf = min(TOKEN_TILE, T), F // 2
    nf = F // tf

    def body(h_ref, g_ref, wg_ref, wu_ref, wd_ref, ho_ref, gate_ref, up_ref, u_sc, acc_sc):
        f = pl.program_id(1)

        @pl.when(f == 0)
        def _():
            u_sc[...] = _rms_fwd(h_ref[...], g_ref[...]).astype(BF16)
            acc_sc[...] = jnp.zeros_like(acc_sc)

        u = u_sc[...]
        gate = _dot_nt(u, wg_ref[...])
        up = _dot_nt(u, wu_ref[...])
        gate_ref[...] = gate.astype(BF16)
        up_ref[...] = up.astype(BF16)
        act = (gate * _sigmoid(gate) * up).astype(BF16)
        acc_sc[...] += _dot(act, wd_ref[...])

        @pl.when(f == nf - 1)
        def _():
            ho_ref[...] = h_ref[...] + 0.5 * acc_sc[...]

    def wspec(kind):
        return pl.BlockSpec((None, None, tf, D), lambda i, f: (e, kind, f, 0))

    return pl.pallas_call(
        body, name="ffn_fwd", grid=(T // tm, nf),
        in_specs=[pl.BlockSpec((tm, D), lambda i, f: (i, 0)), _full((1, D)), wspec(0), wspec(1), wspec(2)],
        out_specs=[pl.BlockSpec((tm, D), lambda i, f: (i, 0)),
                   pl.BlockSpec((tm, tf), lambda i, f: (i, f)), pl.BlockSpec((tm, tf), lambda i, f: (i, f))],
        out_shape=[jax.ShapeDtypeStruct((T, D), F32), jax.ShapeDtypeStruct((T, F), BF16),
                   jax.ShapeDtypeStruct((T, F), BF16)],
        scratch_shapes=[pltpu.VMEM((tm, D), BF16), pltpu.VMEM((tm, D), F32)],
        compiler_params=_cparams("parallel", "arbitrary"),
    )(h, g, wall, wall, wall)


def _ffn_bwd_dx(dho, h, g, gate, up, wall, e):
    T, D = h.shape
    F = wall.shape[2]
    tm, tf = min(TOKEN_TILE, T), F // 2
    nf = F // tf

    def body(dho_ref, h_ref, g_ref, gate_ref, up_ref, wg_ref, wu_ref, wd_ref,
             dhi_ref, dgate_ref, dup_ref, u_ref, dy_ref, dg_ref, dy_sc, acc_sc):
        i, f = pl.program_id(0), pl.program_id(1)

        @pl.when(f == 0)
        def _():
            dy = (0.5 * dho_ref[...]).astype(BF16)
            dy_sc[...] = dy
            dy_ref[...] = dy
            u_ref[...] = _rms_fwd(h_ref[...], g_ref[...]).astype(BF16)
            acc_sc[...] = jnp.zeros_like(acc_sc)

        dact = _dot_nt(dy_sc[...], wd_ref[...])
        gt = gate_ref[...].astype(F32)
        sig = _sigmoid(gt)
        dup = (dact * (gt * sig)).astype(BF16)
        dgate = (dact * up_ref[...].astype(F32) * (sig * (1.0 + gt * (1.0 - sig)))).astype(BF16)
        dup_ref[...] = dup
        dgate_ref[...] = dgate
        acc_sc[...] += _dot(dgate, wg_ref[...]) + _dot(dup, wu_ref[...])

        @pl.when(f == nf - 1)
        def _():
            dx, dg = _rms_bwd(acc_sc[...], h_ref[...], g_ref[...])
            dhi_ref[...] = dho_ref[...] + dx

            @pl.when(i == 0)
            def _():
                dg_ref[...] = jnp.zeros_like(dg_ref)

            dg_ref[...] += dg

    def wspec(kind):
        return pl.BlockSpec((None, None, tf, D), lambda i, f: (e, kind, f, 0))

    row = pl.BlockSpec((tm, D), lambda i, f: (i, 0))
    blk = pl.BlockSpec((tm, tf), lambda i, f: (i, f))
    return pl.pallas_call(
        body, name="ffn_bwd_dx", grid=(T // tm, nf),
        in_specs=[row, row, _full((1, D)), blk, blk, wspec(0), wspec(1), wspec(2)],
        out_specs=[row, blk, blk, row, row, _full((1, D))],
        out_shape=[jax.ShapeDtypeStruct((T, D), F32), jax.ShapeDtypeStruct((T, F), BF16),
                   jax.ShapeDtypeStruct((T, F), BF16), jax.ShapeDtypeStruct((T, D), BF16),
                   jax.ShapeDtypeStruct((T, D), BF16), jax.ShapeDtypeStruct((1, D), F32)],
        scratch_shapes=[pltpu.VMEM((tm, D), BF16), pltpu.VMEM((tm, D), F32)],
        compiler_params=_cparams("arbitrary", "arbitrary"),
    )(dho, h, g, gate, up, wall, wall, wall)


def _ffn_bwd_dw(dgate, dup, gate, up, u, dy, gbuf, e):
    T, F = gate.shape
    D = u.shape[1]
    tfw = F // 2
    tk = min(TOKEN_TILE, T)
    nk = T // tk

    def body(dgate_ref, dup_ref, gate_ref, up_ref, u_ref, dy_ref, gbuf_ref, out_ref, acc_sc):
        k = pl.program_id(1)

        @pl.when(k == 0)
        def _():
            acc_sc[...] = jnp.zeros_like(acc_sc)

        uu = u_ref[...]
        gt = gate_ref[...].astype(F32)
        act = (gt * _sigmoid(gt) * up_ref[...].astype(F32)).astype(BF16)
        acc_sc[0] += _dot_tn(dgate_ref[...], uu)
        acc_sc[1] += _dot_tn(dup_ref[...], uu)
        acc_sc[2] += _dot_tn(act, dy_ref[...])

        @pl.when(k == nk - 1)
        def _():
            out_ref[...] = acc_sc[...].astype(BF16)

    blk = pl.BlockSpec((tk, tfw), lambda j, k: (k, j))
    row = pl.BlockSpec((tk, D), lambda j, k: (k, 0))
    return pl.pallas_call(
        body, name="ffn_bwd_dw", grid=(F // tfw, nk),
        in_specs=[blk, blk, blk, blk, row, row, pl.BlockSpec(memory_space=pl.ANY)],
        out_specs=pl.BlockSpec((None, 3, tfw, D), lambda j, k: (e, 0, j, 0)),
        out_shape=jax.ShapeDtypeStruct(gbuf.shape, BF16),
        scratch_shapes=[pltpu.VMEM((3, tfw, D), F32)],
        input_output_aliases={6: 0},
        compiler_params=_cparams("parallel", "arbitrary"),
    )(dgate, dup, gate, up, u, dy, gbuf)


def _mm_rows(a, b, *, nt, out_dtype, norm_g=None, res=None, name):
    T, K = a.shape
    N = b.shape[0] if nt else b.shape[1]
    tm = min(TOKEN_TILE, T)
    has_g, has_r = norm_g is not None, res is not None

    def body(*refs):
        a_ref, b_ref = refs[0], refs[1]
        o_ref = refs[-1]
        x = a_ref[...]
        if has_g:
            x = _rms_fwd(x, refs[2][...])
        x = x.astype(BF16)
        acc = _dot_nt(x, b_ref[...]) if nt else _dot(x, b_ref[...])
        if has_r:
            acc = refs[2 + has_g][...] + acc
        o_ref[...] = acc.astype(out_dtype)

    ins, specs = [a, b], [pl.BlockSpec((tm, K), lambda i: (i, 0)), _full(b.shape)]
    if has_g:
        ins.append(norm_g)
        specs.append(_full((1, K)))
    if has_r:
        ins.append(res)
        specs.append(pl.BlockSpec((tm, N), lambda i: (i, 0)))
    return pl.pallas_call(
        body, name=name, grid=(T // tm,), in_specs=specs,
        out_specs=pl.BlockSpec((tm, N), lambda i: (i, 0)),
        out_shape=jax.ShapeDtypeStruct((T, N), out_dtype),
        compiler_params=_cparams("parallel"),
    )(*ins)


def _mm_tn(a, b, *, norm_g=None, name):
    T, M = a.shape
    N = b.shape[1]
    tk = min(TOKEN_TILE, T)
    has_g = norm_g is not None

    def body(*refs):
        a_ref, b_ref, o_ref = refs[0], refs[1], refs[-1]

        @pl.when(pl.program_id(0) == 0)
        def _():
            o_ref[...] = jnp.zeros_like(o_ref)

        x = a_ref[...]
        if has_g:
            x = _rms_fwd(x, refs[2][...])
        o_ref[...] += _dot_tn(x.astype(BF16), b_ref[...].astype(BF16))

    ins = [a, b]
    specs = [pl.BlockSpec((tk, M), lambda k: (k, 0)), pl.BlockSpec((tk, N), lambda k: (k, 0))]
    if has_g:
        ins.append(norm_g)
        specs.append(_full((1, M)))
    return pl.pallas_call(
        body, name=name, grid=(T // tk,), in_specs=specs, out_specs=_full((M, N)),
        out_shape=jax.ShapeDtypeStruct((M, N), F32),
        compiler_params=_cparams("arbitrary"),
    )(*ins)


def _proj_bwd(dz, w, h, g, dh, name):
    T, D = h.shape
    N = w.shape[1]
    tm = min(TOKEN_TILE, T)

    def body(dz_ref, w_ref, h_ref, g_ref, dh_ref, o_ref, dg_ref):
        du = _dot_nt(dz_ref[...].astype(BF16), w_ref[...])
        dx, dg = _rms_bwd(du, h_ref[...], g_ref[...])
        o_ref[...] = dh_ref[...] + dx

        @pl.when(pl.program_id(0) == 0)
        def _():
            dg_ref[...] = jnp.zeros_like(dg_ref)

        dg_ref[...] += dg

    row = pl.BlockSpec((tm, D), lambda i: (i, 0))
    return pl.pallas_call(
        body, name=name, grid=(T // tm,),
        in_specs=[pl.BlockSpec((tm, N), lambda i: (i, 0)), _full((D, N)), row, _full((1, D)), row],
        out_specs=[row, _full((1, D))],
        out_shape=[jax.ShapeDtypeStruct((T, D), F32), jax.ShapeDtypeStruct((1, D), F32)],
        compiler_params=_cparams("arbitrary"),
    )(dz, w, h, g, dh)


def _pool_bands(tm):
    r = np.arange(tm)[:, None]
    c = np.arange(tm)[None, :]
    j = np.arange(POOL_HALO)[None, :]
    main, halo, main_t, halo_t = [], [], [], []
    for w in POOL_WINDOWS:
        main.append(((r - c >= 0) & (r - c < w)) / w)
        halo.append((r + POOL_HALO - j < w) / w)
        main_t.append(((c - r >= 0) & (c - r < w)) / w)
        halo_t.append((tm + j - r < w) / w)
    return tuple(jnp.asarray(np.stack(m), BF16) for m in (main, halo, main_t, halo_t))


def _pool_count_scale(i, tm, w):
    t = i * tm + lax.broadcasted_iota(jnp.int32, (tm, 1), 0)
    return w / jnp.minimum(t + 1, w).astype(F32)


def _pool_fwd(h, g, wp, scale):
    T, D = h.shape
    G, dg = len(POOL_WINDOWS), D // len(POOL_WINDOWS)
    tm = min(TOKEN_TILE, T)
    hb = tm // POOL_HALO
    bm, bh, _, _ = _pool_bands(tm)

    def body(h_ref, hh_ref, g_ref, wp_ref, sc_ref, bm_ref, bh_ref, ho_ref, y_ref):
        i = pl.program_id(0)
        x = h_ref[...]
        u = _rms_fwd(x, g_ref[...])
        uh = _rms_fwd(hh_ref[...], g_ref[...]) * (i > 0).astype(F32)
        for gi, w in enumerate(POOL_WINDOWS):
            cols = slice(gi * dg, (gi + 1) * dg)
            ug = u[:, cols]
            hi, lo = _split_bf16(ug)
            hhi, hlo = _split_bf16(uh[:, cols])
            s = (_dot(bm_ref[gi], hi) + _dot(bm_ref[gi], lo)
                 + _dot(bh_ref[gi], hhi) + _dot(bh_ref[gi], hlo))
            y = (s * _pool_count_scale(i, tm, w) - ug).astype(BF16)
            y_ref[:, cols] = y
            ho_ref[:, cols] = x[:, cols] + _dot(y, wp_ref[gi]) * sc_ref[:, cols]

    row = pl.BlockSpec((tm, D), lambda i: (i, 0))
    return pl.pallas_call(
        body, name="pool_fwd", grid=(T // tm,),
        in_specs=[row, pl.BlockSpec((POOL_HALO, D), lambda i: (jnp.maximum(i * hb - 1, 0), 0)),
                  _full((1, D)), _full((G, dg, dg)), _full((1, D)),
                  _full((G, tm, tm)), _full((G, tm, POOL_HALO))],
        out_specs=[row, row],
        out_shape=[jax.ShapeDtypeStruct((T, D), F32), jax.ShapeDtypeStruct((T, D), BF16)],
        compiler_params=_cparams("parallel"),
    )(h, h, g, wp, scale, bm, bh)


def _pool_bwd(dh, h, g, y, wp, scale):
    T, D = h.shape
    G, dg = len(POOL_WINDOWS), D // len(POOL_WINDOWS)
    tm = min(TOKEN_TILE, T)
    hb = tm // POOL_HALO
    nt = T // tm
    _, _, bmt, bht = _pool_bands(tm)

    def body(dh_ref, dhn_ref, h_ref, g_ref, y_ref, wp_ref, sc_ref, bmt_ref, bht_ref,
             o_ref, dg_ref, dwp_ref, dsc_ref, du_sc):
        i = pl.program_id(0)

        @pl.when(i == 0)
        def _():
            dg_ref[...] = jnp.zeros_like(dg_ref)
            dwp_ref[...] = jnp.zeros_like(dwp_ref)
            dsc_ref[...] = jnp.zeros_like(dsc_ref)

        dho = dh_ref[...]
        dz = dho * sc_ref[...]
        dzn = dhn_ref[...] * sc_ref[...] * (i < nt - 1).astype(F32)
        for gi, w in enumerate(POOL_WINDOWS):
            cols = slice(gi * dg, (gi + 1) * dg)
            yg = y_ref[:, cols]
            dzg = dz[:, cols].astype(BF16)
            dsc_ref[:, cols] += jnp.sum(dho[:, cols] * _dot(yg, wp_ref[gi]), axis=0, keepdims=True)
            dwp_ref[gi] += _dot_tn(yg, dzg)
            dy = _dot_nt(dzg, wp_ref[gi])
            dyn = _dot_nt(dzn[:, cols].astype(BF16), wp_ref[gi])
            hi, lo = _split_bf16(dy * _pool_count_scale(i, tm, w))
            nhi, nlo = _split_bf16(dyn)
            du_sc[:, cols] = (_dot(bmt_ref[gi], hi) + _dot(bmt_ref[gi], lo)
                              + _dot(bht_ref[gi], nhi) + _dot(bht_ref[gi], nlo) - dy)
        dx, dgp = _rms_bwd(du_sc[...], h_ref[...], g_ref[...])
        o_ref[...] = dho + dx
        dg_ref[...] += dgp

    row = pl.BlockSpec((tm, D), lambda i: (i, 0))
    return pl.pallas_call(
        body, name="pool_bwd", grid=(nt,),
        in_specs=[row, pl.BlockSpec((POOL_HALO, D), lambda i: (jnp.minimum((i + 1) * hb, T // POOL_HALO - 1), 0)),
                  row, _full((1, D)), row, _full((G, dg, dg)), _full((1, D)),
                  _full((G, tm, tm)), _full((G, tm, POOL_HALO))],
        out_specs=[row, _full((1, D)), _full((G, dg, dg)), _full((1, D))],
        out_shape=[jax.ShapeDtypeStruct((T, D), F32), jax.ShapeDtypeStruct((1, D), F32),
                   jax.ShapeDtypeStruct((G, dg, dg), F32), jax.ShapeDtypeStruct((1, D), F32)],
        scratch_shapes=[pltpu.VMEM((tm, D), F32)],
        compiler_params=_cparams("arbitrary"),
    )(dh, dh, h, g, y, wp, scale, bmt, bht)


def _rope_tables(T):
    pos = jnp.arange(T, dtype=F32)
    inv_freq = ROPE_THETA ** (-jnp.arange(0, D_ROPE, 2, dtype=F32) / D_ROPE)
    ang = pos[:, None] * inv_freq[None, :]
    cos2 = jnp.tile(jnp.cos(ang), (1, 2))
    sin2 = jnp.tile(jnp.sin(ang), (1, 2))
    pad = jnp.zeros((T, D_HEAD_PAD - D_QK), F32)
    ca_q = jnp.concatenate([jnp.ones((T, D_NOPE), F32), cos2, pad], axis=1)
    ca_k = jnp.concatenate([jnp.zeros((T, D_NOPE), F32), cos2, pad], axis=1)
    sb = jnp.concatenate([jnp.zeros((T, D_NOPE), F32), sin2, pad], axis=1)
    return ca_q, ca_k, sb


def _rope_weight_pair(w):
    half = D_ROPE // 2
    z_pad = jnp.zeros(w.shape[:-1] + (D_HEAD_PAD - D_QK,), w.dtype)
    z_nope = jnp.zeros(w.shape[:-1] + (D_NOPE,), w.dtype)
    wa = jnp.concatenate([w, z_pad], axis=-1)
    wb = jnp.concatenate([z_nope, -w[..., D_NOPE + half:], w[..., D_NOPE:D_NOPE + half], z_pad], axis=-1)
    return wa, wb


def _rope_weight_pair_grad(dwa, dwb):
    half = D_ROPE // 2
    d1 = dwa[..., D_NOPE:D_NOPE + half] + dwb[..., D_NOPE + half:D_QK]
    d2 = dwa[..., D_NOPE + half:D_QK] - dwb[..., D_NOPE:D_NOPE + half]
    return jnp.concatenate([dwa[..., :D_NOPE], d1, d2], axis=-1)


def _q_proj(cq, qg, wa, wb, ca, sb):
    T, R = cq.shape
    tm = min(TOKEN_TILE, T)
    P = D_HEAD_PAD

    def body(cq_ref, qg_ref, wa_ref, wb_ref, ca_ref, sb_ref, q_ref):
        c = _rms_fwd(cq_ref[...], qg_ref[...]).astype(BF16)
        q_ref[...] = (_dot(c, wa_ref[...]) * ca_ref[...] + _dot(c, wb_ref[...]) * sb_ref[...]).astype(BF16)

    tok = pl.BlockSpec((tm, P), lambda i, hh: (i, 0))
    wsp = pl.BlockSpec((R, P), lambda i, hh: (0, hh))
    return pl.pallas_call(
        body, name="q_proj", grid=(T // tm, N_HEADS),
        in_specs=[pl.BlockSpec((tm, R), lambda i, hh: (i, 0)), _full((1, R)), wsp, wsp, tok, tok],
        out_specs=pl.BlockSpec((tm, P), lambda i, hh: (i, hh)),
        out_shape=jax.ShapeDtypeStruct((T, N_HEADS * P), BF16),
        compiler_params=_cparams("parallel", "arbitrary"),
    )(cq, qg, wa, wb, ca, sb)


def _q_proj_bwd(dq, cq, qg, wa, wb, ca, sb):
    T, R = cq.shape
    tm = min(TOKEN_TILE, T)
    P = D_HEAD_PAD

    def body(dq_ref, cq_ref, qg_ref, wa_ref, wb_ref, ca_ref, sb_ref,
             da_ref, db_ref, cqn_ref, dcq_ref, dqg_ref, acc_sc):
        i, hh = pl.program_id(0), pl.program_id(1)

        @pl.when(hh == 0)
        def _():
            acc_sc[...] = jnp.zeros_like(acc_sc)
            cqn_ref[...] = _rms_fwd(cq_ref[...], qg_ref[...]).astype(BF16)

        d = dq_ref[...].T
        da = (d * ca_ref[...]).astype(BF16)
        db = (d * sb_ref[...]).astype(BF16)
        da_ref[...] = da
        db_ref[...] = db
        acc_sc[...] += _dot_nt(da, wa_ref[...]) + _dot_nt(db, wb_ref[...])

        @pl.when(hh == N_HEADS - 1)
        def _():
            dx, dg = _rms_bwd(acc_sc[...], cq_ref[...], qg_ref[...])
            dcq_ref[...] = dx

            @pl.when(i == 0)
            def _():
                dqg_ref[...] = jnp.zeros_like(dqg_ref)

            dqg_ref[...] += dg

    tok = pl.BlockSpec((tm, P), lambda i, hh: (i, 0))
    hd = pl.BlockSpec((tm, P), lambda i, hh: (i, hh))
    wsp = pl.BlockSpec((R, P), lambda i, hh: (0, hh))
    rr = pl.BlockSpec((tm, R), lambda i, hh: (i, 0))
    return pl.pallas_call(
        body, name="q_proj_bwd", grid=(T // tm, N_HEADS),
        in_specs=[pl.BlockSpec((P, tm), lambda i, hh: (hh, i)), rr, _full((1, R)), wsp, wsp, tok, tok],
        out_specs=[hd, hd, rr, rr, _full((1, R))],
        out_shape=[jax.ShapeDtypeStruct((T, N_HEADS * P), BF16), jax.ShapeDtypeStruct((T, N_HEADS * P), BF16),
                   jax.ShapeDtypeStruct((T, R), BF16), jax.ShapeDtypeStruct((T, R), F32),
                   jax.ShapeDtypeStruct((1, R), F32)],
        scratch_shapes=[pltpu.VMEM((tm, R), F32)],
        compiler_params=_cparams("arbitrary", "arbitrary"),
    )(dq, cq, qg, wa, wb, ca, sb)


def _kv_proj(h, g_in, wka, wkb, g_c, wuk, wuv, ca, sb):
    T, D = h.shape
    tm = min(TOKEN_TILE, T)
    P, C = D_HEAD_PAD, D_NOPE

    def body(h_ref, gi_ref, wka_ref, wkb_ref, gc_ref, wuk_ref, wuv_ref, ca_ref, sb_ref, k_ref, v_ref, craw_ref):
        u = _rms_fwd(h_ref[...], gi_ref[...]).astype(BF16)
        kva = _dot(u, wka_ref[...])
        kvb = _dot(u, wkb_ref[...])
        craw = kva[:, :C]
        craw_ref[...] = craw
        c = _rms_fwd(craw, gc_ref[...]).astype(BF16)
        kr = kva * ca_ref[...] + kvb * sb_ref[...]
        kn = _dot(c, wuk_ref[...])
        for hh in range(N_HEADS):
            k_ref[:, hh * P:(hh + 1) * P] = (kn[:, hh * P:(hh + 1) * P] + kr).astype(BF16)
        v_ref[...] = _dot(c, wuv_ref[...]).astype(BF16)

    tok = pl.BlockSpec((tm, P), lambda i: (i, 0))
    return pl.pallas_call(
        body, name="kv_proj", grid=(T // tm,),
        in_specs=[pl.BlockSpec((tm, D), lambda i: (i, 0)), _full((1, D)), _full((D, P)), _full((D, P)),
                  _full((1, C)), _full(wuk.shape), _full(wuv.shape), tok, tok],
        out_specs=[pl.BlockSpec((tm, N_HEADS * P), lambda i: (i, 0)),
                   pl.BlockSpec((tm, N_HEADS * D_V), lambda i: (i, 0)), pl.BlockSpec((tm, C), lambda i: (i, 0))],
        out_shape=[jax.ShapeDtypeStruct((T, N_HEADS * P), BF16), jax.ShapeDtypeStruct((T, N_HEADS * D_V), BF16),
                   jax.ShapeDtypeStruct((T, C), F32)],
        compiler_params=_cparams("parallel"),
    )(h, g_in, wka, wkb, g_c, wuk, wuv, ca, sb)


def _kv_proj_bwd(dk, dv, dh, h, g_in, wka, wkb, craw, g_c, wuk, wuv, ca, sb):
    T, D = h.shape
    tm = min(TOKEN_TILE // 2, T)
    P, C = D_HEAD_PAD, D_NOPE

    def body(dk_ref, dv_ref, dh_ref, h_ref, gi_ref, wka_ref, wkb_ref, craw_ref, gc_ref, wuk_ref, wuv_ref,
             ca_ref, sb_ref, o_ref, dgi_ref, dwka_ref, dwkb_ref, dgc_ref, dwuk_ref, dwuv_ref):
        @pl.when(pl.program_id(0) == 0)
        def _():
            for r in (dgi_ref, dwka_ref, dwkb_ref, dgc_ref, dwuk_ref, dwuv_ref):
                r[...] = jnp.zeros_like(r)

        x = h_ref[...]
        u = _rms_fwd(x, gi_ref[...]).astype(BF16)
        craw = craw_ref[...]
        c = _rms_fwd(craw, gc_ref[...]).astype(BF16)
        dkf = dk_ref[...]
        dkb = dkf.astype(BF16)
        dvb = dv_ref[...].astype(BF16)
        dwuk_ref[...] += _dot_tn(c, dkb)
        dwuv_ref[...] += _dot_tn(c, dvb)
        dc = _dot_nt(dkb, wuk_ref[...]) + _dot_nt(dvb, wuv_ref[...])
        dkr = dkf[:, :P]
        for hh in range(1, N_HEADS):
            dkr = dkr + dkf[:, hh * P:(hh + 1) * P]
        dcraw, dgc = _rms_bwd(dc, craw, gc_ref[...])
        dgc_ref[...] += dgc
        dkva = jnp.concatenate([dcraw, (dkr * ca_ref[...])[:, C:]], axis=1).astype(BF16)
        dkvb = (dkr * sb_ref[...]).astype(BF16)
        dwka_ref[...] += _dot_tn(u, dkva)
        dwkb_ref[...] += _dot_tn(u, dkvb)
        du = _dot_nt(dkva, wka_ref[...]) + _dot_nt(dkvb, wkb_ref[...])
        dx, dgi = _rms_bwd(du, x, gi_ref[...])
        dgi_ref[...] += dgi
        o_ref[...] = dh_ref[...] + dx

    row = pl.BlockSpec((tm, D), lambda i: (i, 0))
    tok = pl.BlockSpec((tm, P), lambda i: (i, 0))
    return pl.pallas_call(
        body, name="kv_proj_bwd", grid=(T // tm,),
        in_specs=[pl.BlockSpec((tm, N_HEADS * P), lambda i: (i, 0)),
                  pl.BlockSpec((tm, N_HEADS * D_V), lambda i: (i, 0)), row, row, _full((1, D)),
                  _full((D, P)), _full((D, P)), pl.BlockSpec((tm, C), lambda i: (i, 0)), _full((1, C)),
                  _full(wuk.shape), _full(wuv.shape), tok, tok],
        out_specs=[row, _full((1, D)), _full((D, P)), _full((D, P)), _full((1, C)),
                   _full(wuk.shape), _full(wuv.shape)],
        out_shape=[jax.ShapeDtypeStruct((T, D), F32), jax.ShapeDtypeStruct((1, D), F32),
                   jax.ShapeDtypeStruct((D, P), F32), jax.ShapeDtypeStruct((D, P), F32),
                   jax.ShapeDtypeStruct((1, C), F32), jax.ShapeDtypeStruct(wuk.shape, F32),
                   jax.ShapeDtypeStruct(wuv.shape, F32)],
        compiler_params=_cparams("arbitrary"),
    )(dk, dv, dh, h, g_in, wka, wkb, craw, g_c, wuk, wuv, ca, sb)


_ATTN_SCALE = D_QK ** -0.5
_LOG2_E = 1.4426950408889634
_ATTN_SCALE_LOG2 = _ATTN_SCALE * _LOG2_E


def _causal_mask(t):
    return lax.broadcasted_iota(jnp.int32, (t, t), 1) <= lax.broadcasted_iota(jnp.int32, (t, t), 0)


def _causal_mask_t(t):
    return lax.broadcasted_iota(jnp.int32, (t, t), 0) <= lax.broadcasted_iota(jnp.int32, (t, t), 1)


def _flash_fwd(q, k, v):
    T = q.shape[0]
    t = min(ATTN_TILE, T)
    P = D_HEAD_PAD

    def body(q_ref, k_ref, v_ref, o_ref, lse_ref, m_sc, l_sc, acc_sc):
        qi = pl.program_id(1)
        qq = q_ref[...]
        m_sc[...] = jnp.full_like(m_sc, NEG_BIG)
        l_sc[...] = jnp.zeros_like(l_sc)
        acc_sc[...] = jnp.zeros_like(acc_sc)

        def rows_of(ki):
            return pl.ds(pl.multiple_of(ki * t, t), t)

        def scores(ki):
            return _dot_nt(k_ref[rows_of(ki), :], qq)

        def accumulate(ki, s_t):
            m_prev = m_sc[...]
            m_new = jnp.maximum(m_prev, jnp.max(s_t, axis=0, keepdims=True))
            p_t = jnp.exp2((s_t - m_new) * _ATTN_SCALE_LOG2)
            alpha = jnp.exp2((m_prev - m_new) * _ATTN_SCALE_LOG2)
            l_sc[...] = alpha * l_sc[...] + jnp.sum(p_t, axis=0, keepdims=True)
            acc_sc[...] = alpha * acc_sc[...] + _dot_tn(v_ref[rows_of(ki), :], p_t.astype(BF16))
            m_sc[...] = m_new

        def loop_body(ki, s_t):
            s_next = scores(ki + 1)
            accumulate(ki, s_t)
            return s_next

        s_last = lax.fori_loop(0, qi, loop_body, scores(0))
        accumulate(qi, jnp.where(_causal_mask_t(t), s_last, NEG_BIG))
        l = l_sc[...]
        o_ref[...] = (acc_sc[...] / l).T.astype(BF16)
        lse_ref[...] = m_sc[...] * _ATTN_SCALE + jnp.log(l)

    return pl.pallas_call(
        body, name="flash_fwd", grid=(N_HEADS, T // t),
        in_specs=[pl.BlockSpec((t, P), lambda hh, i: (i, hh)), pl.BlockSpec((T, P), lambda hh, i: (0, hh)),
                  pl.BlockSpec((T, D_V), lambda hh, i: (0, hh))],
        out_specs=[pl.BlockSpec((t, D_V), lambda hh, i: (i, hh)),
                   pl.BlockSpec((None, None, 1, t), lambda hh, i: (hh, i, 0, 0))],
        out_shape=[jax.ShapeDtypeStruct((T, N_HEADS * D_V), BF16),
                   jax.ShapeDtypeStruct((N_HEADS, T // t, 1, t), F32)],
        scratch_shapes=[pltpu.VMEM((1, t), F32), pltpu.VMEM((1, t), F32), pltpu.VMEM((D_V, t), F32)],
        compiler_params=_cparams("parallel", "arbitrary"),
    )(q, k, v)


def _attn_delta(o, do):
    T = o.shape[0]
    t = min(ATTN_TILE, T)

    def body(o_ref, do_ref, out_ref):
        hi, lo = _split_bf16(o_ref[...].astype(F32) * do_ref[...].astype(F32))
        ones = jnp.ones((8, D_V), BF16)
        out_ref[...] = (_dot_nt(ones, hi) + _dot_nt(ones, lo))[0:1]

    hd_v = pl.BlockSpec((t, D_V), lambda hh, i: (i, hh))
    return pl.pallas_call(
        body, name="attn_delta", grid=(N_HEADS, T // t), in_specs=[hd_v, hd_v],
        out_specs=pl.BlockSpec((None, None, 1, t), lambda hh, i: (hh, i, 0, 0)),
        out_shape=jax.ShapeDtypeStruct((N_HEADS, T // t, 1, t), F32),
        compiler_params=_cparams("parallel", "parallel"),
    )(o, do)


def _flash_bwd(q, k, v, do, lse_row, delta_row, dk_prev, dv_prev):
    T = q.shape[0]
    t = min(ATTN_TILE, T)
    nq = T // t
    P = D_HEAD_PAD

    def body(k_ref, v_ref, q_ref, do_ref, lse_ref, delta_ref, dkp_ref, dvp_ref,
             dqt_ref, dk_ref, dv_ref, dk_sc, dv_sc, dqt_sc):
        ki = pl.program_id(1)
        kk = k_ref[...]
        vv = v_ref[...]
        kk_t = kk.astype(F32).T.astype(BF16)
        dk_sc[...] = dkp_ref[...]
        dv_sc[...] = dvp_ref[...]

        @pl.when(ki == 0)
        def _():
            dqt_sc[...] = jnp.zeros_like(dqt_sc)

        def chunk(qi, diagonal):
            rows = pl.ds(pl.multiple_of(qi * t, t), t)
            qq = q_ref[rows, :]
            dd = do_ref[rows, :]
            p_t = jnp.exp2(_dot_nt(kk, qq) * _ATTN_SCALE_LOG2 - lse_ref[qi] * _LOG2_E)
            if diagonal:
                p_t = jnp.where(_causal_mask_t(t), p_t, 0.0)
            dv_sc[...] += _dot(p_t.astype(BF16), dd)
            dp_t = _dot_nt(vv, dd)
            ds_t = (p_t * (dp_t - delta_ref[qi]) * _ATTN_SCALE).astype(BF16)
            dk_sc[...] += _dot(ds_t, qq)
            dqt_sc[qi] += _dot(kk_t, ds_t)

        def loop_body(qi, carry):
            chunk(qi, False)
            return carry

        chunk(ki, True)
        lax.fori_loop(ki + 1, nq, loop_body, 0)
        dqt_ref[...] = dqt_sc[ki]
        dk_ref[...] = dk_sc[...]
        dv_ref[...] = dv_sc[...]

    kb = pl.BlockSpec((t, P), lambda hh, i: (i, hh))
    vb = pl.BlockSpec((t, D_V), lambda hh, i: (i, hh))
    stat = pl.BlockSpec((None, nq, 1, t), lambda hh, i: (hh, 0, 0, 0))
    return pl.pallas_call(
        body, name="flash_bwd", grid=(N_HEADS, nq),
        in_specs=[kb, vb, pl.BlockSpec((T, P), lambda hh, i: (0, hh)), pl.BlockSpec((T, D_V), lambda hh, i: (0, hh)),
                  stat, stat, kb, vb],
        out_specs=[pl.BlockSpec((P, t), lambda hh, i: (hh, i)), kb, vb],
        out_shape=[jax.ShapeDtypeStruct((N_HEADS * P, T), F32), jax.ShapeDtypeStruct((T, N_HEADS * P), F32),
                   jax.ShapeDtypeStruct((T, N_HEADS * D_V), F32)],
        scratch_shapes=[pltpu.VMEM((t, P), F32), pltpu.VMEM((t, D_V), F32), pltpu.VMEM((nq, P, t), F32)],
        compiler_params=_cparams("arbitrary", "arbitrary"),
    )(k, v, q, do, lse_row, delta_row, dk_prev, dv_prev)


def _loss_head(h, g, target):
    T, D = h.shape
    tm = min(TOKEN_TILE, T)

    def body(h_ref, g_ref, t_ref, dh_ref, loss_ref, dg_ref):
        @pl.when(pl.program_id(0) == 0)
        def _():
            loss_ref[...] = jnp.zeros_like(loss_ref)
            dg_ref[...] = jnp.zeros_like(dg_ref)

        x = h_ref[...]
        err = _rms_fwd(x, g_ref[...]) - t_ref[...]
        per_tok = jnp.mean(err * err, axis=-1, keepdims=True)
        loss_ref[...] += 0.5 * jnp.sum(per_tok, axis=0, keepdims=True)
        dx, dg = _rms_bwd(err * (1.0 / D), x, g_ref[...])
        dh_ref[...] = dx
        dg_ref[...] += dg

    row = pl.BlockSpec((tm, D), lambda i: (i, 0))
    return pl.pallas_call(
        body, name="loss_head", grid=(T // tm,),
        in_specs=[row, _full((1, D)), row], out_specs=[row, _full((1, 128)), _full((1, D))],
        out_shape=[jax.ShapeDtypeStruct((T, D), F32), jax.ShapeDtypeStruct((1, 128), F32),
                   jax.ShapeDtypeStruct((1, D), F32)],
        compiler_params=_cparams("arbitrary"),
    )(h, g, target)


def _sum_parts(parts, tr, name):
    _, R, C = parts.shape

    def body(p_ref, o_ref):
        acc = p_ref[0].astype(F32)
        for j in range(1, N_DEV):
            acc = acc + p_ref[j].astype(F32)
        o_ref[...] = acc

    return pl.pallas_call(
        body, name=name, grid=(R // tr,),
        in_specs=[pl.BlockSpec((N_DEV, tr, C), lambda i: (0, i, 0))],
        out_specs=pl.BlockSpec((tr, C), lambda i: (i, 0)),
        out_shape=jax.ShapeDtypeStruct((R, C), F32),
        compiler_params=_cparams("parallel"),
    )(parts)


def _adamw(w, g, m, v):
    R, C = w.shape
    tr = _row_tile(R, TOKEN_TILE)

    def body(w_ref, g_ref, m_ref, v_ref, d_ref, mo_ref, vo_ref):
        gg = g_ref[...]
        mn = ADAM_B1 * m_ref[...] + (1.0 - ADAM_B1) * gg
        vn = ADAM_B2 * v_ref[...] + (1.0 - ADAM_B2) * (gg * gg)
        m_hat = mn / (1.0 - ADAM_B1 ** ADAM_STEP)
        v_hat = vn / (1.0 - ADAM_B2 ** ADAM_STEP)
        d_ref[...] = -ADAM_LR * (m_hat / (jnp.sqrt(v_hat) + ADAM_EPS) + ADAM_WD * w_ref[...])
        mo_ref[...] = mn
        vo_ref[...] = vn

    blk = pl.BlockSpec((tr, C), lambda i: (i, 0))
    return pl.pallas_call(
        body, name="adamw", grid=(R // tr,), in_specs=[blk] * 4, out_specs=[blk] * 3,
        out_shape=[jax.ShapeDtypeStruct((R, C), F32)] * 3,
        compiler_params=_cparams("parallel"),
    )(w, g, m, v)


def _adamw_nd(w, g, m, v):
    shape = w.shape
    two_d = (1, shape[0]) if len(shape) == 1 else (int(np.prod(shape[:-1])), shape[-1])
    outs = _adamw(w.reshape(two_d), g.reshape(two_d), m.reshape(two_d), v.reshape(two_d))
    return tuple(o.reshape(shape) for o in outs)


def _f32_as_bf16_pairs(a):
    return lax.bitcast_convert_type(a, BF16).reshape(a.shape[:-1] + (a.shape[-1] * 2,))


def _bf16_pairs_as_f32(a):
    return lax.bitcast_convert_type(a.reshape(a.shape[:-1] + (a.shape[-1] // 2, 2)), F32)


def _pack_misc(w_o, w_dq, w_uq, w_dkv, pool_w, pool_scale):
    lead = w_o.shape[:-3]
    rows = [w_o, w_dq, w_uq, w_dkv, pool_w]
    flat = [r.astype(BF16).reshape(lead + (-1, REP_COLS)) for r in rows]
    ps = _f32_as_bf16_pairs(pool_scale.astype(F32)).reshape(lead + (1, -1))
    ps = jnp.concatenate([ps, jnp.zeros(lead + (1, REP_COLS - ps.shape[-1]), BF16)], axis=-1)
    used = sum(f.shape[-2] for f in flat) + 1
    pad = jnp.zeros(lead + (MISC_ROWS - used, REP_COLS), BF16)
    return jnp.concatenate(flat + [ps, pad], axis=-2)


def _unpack_misc(buf, shapes):
    out, r0 = [], 0
    for shp in shapes[:-1]:
        n = int(np.prod(shp)) // REP_COLS
        out.append(buf[:, r0:r0 + n].reshape((N_DEV,) + shp))
        r0 += n
    n_ps = int(np.prod(shapes[-1]))
    out.append(_bf16_pairs_as_f32(buf[:, r0, :2 * n_ps]).reshape((N_DEV,) + shapes[-1]))
    return out


def _cat_dev(a, axis):
    a = jnp.moveaxis(a, 0, axis)
    return a.reshape(a.shape[:axis] + (a.shape[axis] * a.shape[axis + 1],) + a.shape[axis + 2:])


def _split_dev(a, axis):
    a = a.reshape(a.shape[:axis] + (N_DEV, a.shape[axis] // N_DEV) + a.shape[axis + 1:])
    return jnp.moveaxis(a, axis, 0)


def kernel(x, ffn_pre_norm, ffn_pre_wg, ffn_pre_wu, ffn_pre_wd, mix_norm, ffn_post_norm, ffn_post_wg, ffn_post_wu, ffn_post_wd, pool_w, pool_scale, kv_in_norm, w_dkv, ckv_norm, w_uk, w_uv, q_lora_norm, w_dq, w_uq, w_o, final_norm, loss_target, m_ffn_pre_norm, m_ffn_pre_wg, m_ffn_pre_wu, m_ffn_pre_wd, m_mix_norm, m_ffn_post_norm, m_ffn_post_wg, m_ffn_post_wu, m_ffn_post_wd, m_pool_w, m_pool_scale, m_kv_in_norm, m_w_dkv, m_ckv_norm, m_w_uk, m_w_uv, m_q_lora_norm, m_w_dq, m_w_uq, m_w_o, m_final_norm, v_ffn_pre_norm, v_ffn_pre_wg, v_ffn_pre_wu, v_ffn_pre_wd, v_mix_norm, v_ffn_post_norm, v_ffn_post_wg, v_ffn_post_wu, v_ffn_post_wd, v_pool_w, v_pool_scale, v_kv_in_norm, v_w_dkv, v_ckv_norm, v_w_uk, v_w_uv, v_q_lora_norm, v_w_dq, v_w_uq, v_w_o, v_final_norm):
    weights = dict(ffn_pre_norm=ffn_pre_norm, ffn_pre_wg=ffn_pre_wg, ffn_pre_wu=ffn_pre_wu, ffn_pre_wd=ffn_pre_wd,
                   mix_norm=mix_norm, ffn_post_norm=ffn_post_norm, ffn_post_wg=ffn_post_wg,
                   ffn_post_wu=ffn_post_wu, ffn_post_wd=ffn_post_wd, pool_w=pool_w, pool_scale=pool_scale,
                   kv_in_norm=kv_in_norm, w_dkv=w_dkv, ckv_norm=ckv_norm, w_uk=w_uk, w_uv=w_uv,
                   q_lora_norm=q_lora_norm, w_dq=w_dq, w_uq=w_uq, w_o=w_o, final_norm=final_norm)
    moments_m = dict(ffn_pre_norm=m_ffn_pre_norm, ffn_pre_wg=m_ffn_pre_wg, ffn_pre_wu=m_ffn_pre_wu,
                     ffn_pre_wd=m_ffn_pre_wd, mix_norm=m_mix_norm, ffn_post_norm=m_ffn_post_norm,
                     ffn_post_wg=m_ffn_post_wg, ffn_post_wu=m_ffn_post_wu, ffn_post_wd=m_ffn_post_wd,
                     pool_w=m_pool_w, pool_scale=m_pool_scale, kv_in_norm=m_kv_in_norm, w_dkv=m_w_dkv,
                     ckv_norm=m_ckv_norm, w_uk=m_w_uk, w_uv=m_w_uv, q_lora_norm=m_q_lora_norm, w_dq=m_w_dq,
                     w_uq=m_w_uq, w_o=m_w_o, final_norm=m_final_norm)
    moments_v = dict(ffn_pre_norm=v_ffn_pre_norm, ffn_pre_wg=v_ffn_pre_wg, ffn_pre_wu=v_ffn_pre_wu,
                     ffn_pre_wd=v_ffn_pre_wd, mix_norm=v_mix_norm, ffn_post_norm=v_ffn_post_norm,
                     ffn_post_wg=v_ffn_post_wg, ffn_post_wu=v_ffn_post_wu, ffn_post_wd=v_ffn_post_wd,
                     pool_w=v_pool_w, pool_scale=v_pool_scale, kv_in_norm=v_kv_in_norm, w_dkv=v_w_dkv,
                     ckv_norm=v_ckv_norm, w_uk=v_w_uk, w_uv=v_w_uv, q_lora_norm=v_q_lora_norm, w_dq=v_w_dq,
                     w_uq=v_w_uq, w_o=v_w_o, final_norm=v_final_norm)
    order = list(weights)

    T, D = x.shape[1], x.shape[2]
    depth = ffn_pre_norm.shape[0]
    n_a = pool_w.shape[0]
    n_b = depth - n_a
    fs = ffn_pre_wd.shape[1]
    F = fs * N_DEV
    n_ffn = 2 * depth
    t_attn = min(ATTN_TILE, T)

    ffn_local = jnp.stack([
        jnp.stack([jnp.swapaxes(wg[l], 0, 1), jnp.swapaxes(wu[l], 0, 1), wd[l]])
        for l in range(depth)
        for wg, wu, wd in ((ffn_pre_wg, ffn_pre_wu, ffn_pre_wd), (ffn_post_wg, ffn_post_wu, ffn_post_wd))
    ]).astype(BF16)
    misc_local = _pack_misc(w_o, w_dq, w_uq.reshape(n_b, w_uq.shape[1], -1), w_dkv, pool_w, pool_scale)
    misc_shapes = (w_o.shape, w_dq.shape, (n_b, w_uq.shape[1], N_HEADS * D_QK), w_dkv.shape, pool_w.shape,
                   pool_scale.shape)
    ffn_all, misc_all = _all_gather(ffn_local.reshape(n_ffn * 3, fs, D), misc_local)
    wall = ffn_all.reshape(n_ffn, 3, F, D)
    o_blk, dq_blk, uq_blk, dkv_blk, pw_blk, ps_blk = _unpack_misc(misc_all, misc_shapes)
    w_o_f = _cat_dev(o_blk, 1)
    w_dq_f = _cat_dev(dq_blk, 1)
    w_uq_f = _cat_dev(uq_blk, 1).reshape(n_b, -1, N_HEADS, D_QK)
    w_dkv_f = _cat_dev(dkv_blk, 0)
    pool_w_f = _cat_dev(pw_blk, 2)
    pool_scale_f = _cat_dev(ps_blk, 1)
    rq = w_dq_f.shape[2]
    wqa, wqb = _rope_weight_pair(w_uq_f)
    wqa = wqa.reshape(n_b, rq, N_HEADS * D_HEAD_PAD)
    wqb = wqb.reshape(n_b, rq, N_HEADS * D_HEAD_PAD)
    wka, wkb = _rope_weight_pair(w_dkv_f)
    wuk = jnp.concatenate([w_uk, jnp.zeros_like(w_uk)], axis=-1).astype(BF16).reshape(D_NOPE, N_HEADS * D_HEAD_PAD)
    wuv = w_uv.astype(BF16).reshape(D_NOPE, N_HEADS * D_V)
    ca_q, ca_k, sb = _rope_tables(T)

    def vec(a):
        return a.reshape(1, -1)

    h = x.reshape(T, D)
    saved = []
    k_all = v_all = craw = h_kv = None
    for l in range(depth):
        s = {"h0": h}
        h, s["g1"], s["u1"] = _ffn_fwd(h, vec(ffn_pre_norm[l]), wall, 2 * l)
        s["h1"] = h
        if l < n_a:
            h, s["y"] = _pool_fwd(h, vec(mix_norm[l]), pool_w_f[l], vec(pool_scale_f[l]))
        else:
            j = l - n_a
            s["cq"] = _mm_rows(h, w_dq_f[j], nt=False, out_dtype=F32, norm_g=vec(mix_norm[l]), name="q_down")
            s["q"] = _q_proj(s["cq"], vec(q_lora_norm[j]), wqa[j], wqb[j], ca_q, sb)
            s["o"], s["lse"] = _flash_fwd(s["q"], k_all, v_all)
            h = _mm_rows(s["o"], w_o_f[j], nt=False, out_dtype=F32, res=h, name="attn_out")
        s["h2"] = h
        h, s["g2"], s["u2"] = _ffn_fwd(h, vec(ffn_post_norm[l]), wall, 2 * l + 1)
        if l == n_a - 1:
            h_kv = h
            k_all, v_all, craw = _kv_proj(h, vec(kv_in_norm), wka, wkb, vec(ckv_norm), wuk, wuv, ca_k, sb)
        saved.append(s)

    dh, loss_part, d_final = _loss_head(h, vec(final_norm), loss_target.reshape(T, D))

    gbuf = jnp.zeros((n_ffn, 3, F, D), BF16)
    grads = {}
    d_pre, d_post, d_mix = [None] * depth, [None] * depth, [None] * depth
    d_pool_w, d_pool_scale = [None] * n_a, [None] * n_a
    d_qln, d_wdq, d_wuq, d_wo = [None] * n_b, [None] * n_b, [None] * n_b, [None] * n_b
    dk_acc = jnp.zeros((T, N_HEADS * D_HEAD_PAD), F32)
    dv_acc = jnp.zeros((T, N_HEADS * D_V), F32)
    for l in reversed(range(depth)):
        s = saved[l]
        if l == n_a - 1:
            (dh, grads["kv_in_norm"], dwka, dwkb, grads["ckv_norm"], dwuk, dwuv) = _kv_proj_bwd(
                dk_acc, dv_acc, dh, h_kv, vec(kv_in_norm), wka, wkb, craw, vec(ckv_norm), wuk, wuv, ca_k, sb)
            grads["w_dkv"] = _rope_weight_pair_grad(dwka, dwkb)
            grads["w_uk"] = dwuk.reshape(D_NOPE, N_HEADS, D_HEAD_PAD)[..., :D_NOPE]
            grads["w_uv"] = dwuv.reshape(D_NOPE, N_HEADS, D_V)
        dh, dgt, dup, u_b, dy_b, d_post[l] = _ffn_bwd_dx(dh, s["h2"], vec(ffn_post_norm[l]), s["g2"], s["u2"],
                                                       wall, 2 * l + 1)
        gbuf = _ffn_bwd_dw(dgt, dup, s["g2"], s["u2"], u_b, dy_b, gbuf, 2 * l + 1)
        if l < n_a:
            dh, d_mix[l], d_pool_w[l], d_pool_scale[l] = _pool_bwd(
                dh, s["h1"], vec(mix_norm[l]), s["y"], pool_w_f[l], vec(pool_scale_f[l]))
        else:
            j = l - n_a
            d_wo[j] = _mm_tn(s["o"], dh, name="attn_out_dw")
            do = _mm_rows(dh, w_o_f[j], nt=True, out_dtype=BF16, name="attn_out_dx")
            delta_row = _attn_delta(s["o"], do)
            dq_t, dk_acc, dv_acc = _flash_bwd(s["q"], k_all, v_all, do, s["lse"], delta_row, dk_acc, dv_acc)
            da, db, cqn, dcq, d_qln[j] = _q_proj_bwd(dq_t, s["cq"], vec(q_lora_norm[j]), wqa[j], wqb[j], ca_q, sb)
            dwa = _mm_tn(cqn, da, name="q_up_dw")
            dwb = _mm_tn(cqn, db, name="q_up_dw")
            d_wuq[j] = _rope_weight_pair_grad(dwa.reshape(rq, N_HEADS, D_HEAD_PAD),
                                              dwb.reshape(rq, N_HEADS, D_HEAD_PAD))
            d_wdq[j] = _mm_tn(s["h1"], dcq, norm_g=vec(mix_norm[l]), name="q_down_dw")
            dh, d_mix[l] = _proj_bwd(dcq, w_dq_f[j], s["h1"], vec(mix_norm[l]), dh, "q_down_dx")
        dh, dgt, dup, u_b, dy_b, d_pre[l] = _ffn_bwd_dx(dh, s["h0"], vec(ffn_pre_norm[l]), s["g1"], s["u1"],
                                                      wall, 2 * l)
        gbuf = _ffn_bwd_dw(dgt, dup, s["g1"], s["u1"], u_b, dy_b, gbuf, 2 * l)
    grad_x = dh.reshape(x.shape)

    rep_names = ["ffn_pre_norm", "mix_norm", "ffn_post_norm", "kv_in_norm", "ckv_norm", "q_lora_norm",
                 "final_norm", "w_uk", "w_uv"]
    grads["ffn_pre_norm"] = jnp.concatenate(d_pre, axis=0)
    grads["mix_norm"] = jnp.concatenate(d_mix, axis=0)
    grads["ffn_post_norm"] = jnp.concatenate(d_post, axis=0)
    grads["q_lora_norm"] = jnp.concatenate(d_qln, axis=0)
    grads["final_norm"] = d_final
    rep_flat = jnp.concatenate([grads[n].reshape(-1) for n in rep_names] + [loss_part[0, :1]])
    n_rep = rep_flat.shape[0]
    rep_rows = -(-n_rep // (8 * REP_COLS)) * 8
    rep_g = jnp.concatenate([rep_flat, jnp.zeros((rep_rows * REP_COLS - n_rep,), F32)]).reshape(rep_rows, REP_COLS)
    misc_g = _pack_misc(_split_dev(jnp.stack(d_wo), 1), _split_dev(jnp.stack(d_wdq), 1),
                        _split_dev(jnp.stack(d_wuq).reshape(n_b, rq, -1), 1), _split_dev(grads["w_dkv"], 0),
                        _split_dev(jnp.stack(d_pool_w), 2),
                        _split_dev(jnp.concatenate(d_pool_scale, axis=0), 1))
    ffn_parts, misc_parts, rep_parts = _grad_exchange(gbuf.reshape(n_ffn * 3, N_DEV, fs, D), misc_g, rep_g)
    ffn_sum = _sum_parts(ffn_parts.reshape(N_DEV, n_ffn * 3 * fs, D), fs, "sum_ffn").reshape(n_ffn, 3, fs, D)
    misc_sum_parts = _unpack_misc(misc_parts, misc_shapes)
    rep_sum = _sum_parts(rep_parts, _row_tile(rep_rows, 128), "sum_rep").reshape(-1)

    def sum_small(p):
        shp = p.shape[1:]
        two_d = (int(np.prod(shp[:-1])), shp[-1])
        return _sum_parts(p.reshape((N_DEV,) + two_d), two_d[0], "sum_misc").reshape(shp)

    g_wo, g_wdq, g_wuq, g_wdkv, g_pw, g_ps = [sum_small(p) for p in misc_sum_parts]
    grads.update(w_o=g_wo, w_dq=g_wdq, w_uq=g_wuq.reshape(w_uq.shape), w_dkv=g_wdkv, pool_w=g_pw, pool_scale=g_ps)
    for kind, (npre, npost) in enumerate((("ffn_pre_wg", "ffn_post_wg"), ("ffn_pre_wu", "ffn_post_wu"),
                                          ("ffn_pre_wd", "ffn_post_wd"))):
        pre = ffn_sum[0::2, kind]
        post = ffn_sum[1::2, kind]
        if kind < 2:
            pre, post = jnp.swapaxes(pre, 1, 2), jnp.swapaxes(post, 1, 2)
        grads[npre], grads[npost] = pre, post
    off = 0
    for n in rep_names:
        size = int(np.prod(weights[n].shape))
        grads[n] = rep_sum[off:off + size].reshape(weights[n].shape)
        off += size
    loss = rep_sum[off]

    deltas, new_m, new_v = {}, {}, {}
    for n in order:
        deltas[n], new_m[n], new_v[n] = _adamw_nd(weights[n], grads[n], moments_m[n], moments_v[n])
    return (loss, grad_x, *[grads[n] for n in order], *[deltas[n] for n in order],
            *[new_m[n] for n in order], *[new_v[n] for n in order])
```

```python
import functools

import numpy as np
import jax
import jax.numpy as jnp
from jax import lax
from jax.experimental import pallas as pl
from jax.experimental.pallas import tpu as pltpu

F32, BF16 = jnp.float32, jnp.bfloat16
N_DEV = 8
RMS_EPS = 1e-6
N_HEADS = 16
D_NOPE, D_ROPE, D_V = 128, 64, 128
D_QK = D_NOPE + D_ROPE
D_HEAD_PAD = 256
HEAD_GROUP = 4
ROPE_THETA = 10000.0
POOL_WINDOWS = (2, 4, 8, 16)
POOL_HALO = 16
ADAM_LR, ADAM_B1, ADAM_B2, ADAM_EPS, ADAM_WD, ADAM_STEP = 0.001, 0.9, 0.999, 1e-08, 0.01, 10
NEG_BIG = -1e30
V7X_VMEM_LIMIT = 56 * 1024 * 1024
TOKEN_TILE = 512
ATTN_TILE = 512
FFN_TILE = 256
MISC_ROWS = 864
REP_COLS = 1024


def _cparams(*sem):
    return pltpu.CompilerParams(dimension_semantics=sem, vmem_limit_bytes=V7X_VMEM_LIMIT)


def _dot(a, b):
    return lax.dot_general(a, b, (((1,), (0,)), ((), ())), preferred_element_type=F32)


def _dot_nt(a, b):
    return lax.dot_general(a, b, (((1,), (1,)), ((), ())), preferred_element_type=F32)


def _dot_tn(a, b):
    return lax.dot_general(a, b, (((0,), (0,)), ((), ())), preferred_element_type=F32)


def _rms_fwd(x, g):
    r = lax.rsqrt(jnp.mean(x * x, axis=-1, keepdims=True) + RMS_EPS)
    return (x * r) * g


def _rms_bwd(du, x, g):
    r = lax.rsqrt(jnp.mean(x * x, axis=-1, keepdims=True) + RMS_EPS)
    xh = x * r
    dg = jnp.sum(du * xh, axis=0, keepdims=True)
    dxh = du * g
    dx = r * (dxh - xh * jnp.mean(dxh * xh, axis=-1, keepdims=True))
    return dx, dg


def _sigmoid(x):
    return 1.0 / (1.0 + jnp.exp(-x))


def _split_bf16(x):
    hi = x.astype(BF16)
    lo = (x - hi.astype(F32)).astype(BF16)
    return hi, lo


def _full(shape):
    return pl.BlockSpec(shape, lambda *_: (0,) * len(shape))


def _row_tile(rows, cap):
    for t in range(min(cap, rows) // 8 * 8, 0, -8):
        if rows % t == 0:
            return t
    return rows


def _peers():
    x, y, c = lax.axis_index("x"), lax.axis_index("y"), lax.axis_index("c")
    out = []
    for k in range(1, N_DEV):
        px = 1 - x if (k >> 2) & 1 else x
        py = 1 - y if (k >> 1) & 1 else y
        pc = 1 - c if k & 1 else c
        out.append(((px, py, pc), 4 * px + 2 * py + pc))
    return 4 * x + 2 * y + c, out


def _exchange(arrays, src_of, dst_of, out_shapes, name):
    n = len(arrays)

    def body(*refs):
        ins, outs = refs[:n], refs[n:2 * n]
        send_sems, recv_sems, loc_sems = refs[2 * n:]
        me, peers = _peers()
        own = [pltpu.make_async_copy(src_of(j, ins[j], me), dst_of(j, outs[j], me), loc_sems.at[j])
               for j in range(n)]
        for cp in own:
            cp.start()
        sends = []
        for k, (peer, pidx) in enumerate(peers):
            for j in range(n):
                sends.append(pltpu.make_async_remote_copy(
                    src_ref=src_of(j, ins[j], pidx), dst_ref=dst_of(j, outs[j], me),
                    send_sem=send_sems.at[j, k], recv_sem=recv_sems.at[j, k],
                    device_id=peer, device_id_type=pl.DeviceIdType.MESH))
        for cp in sends:
            cp.start()
        for k, (peer, pidx) in enumerate(peers):
            for j in range(n):
                pltpu.make_async_remote_copy(
                    src_ref=src_of(j, ins[j], pidx), dst_ref=dst_of(j, outs[j], pidx),
                    send_sem=send_sems.at[j, k], recv_sem=recv_sems.at[j, k],
                    device_id=peer, device_id_type=pl.DeviceIdType.MESH).wait_recv()
        for cp in sends:
            cp.wait_send()
        for cp in own:
            cp.wait()

    any_spec = pl.BlockSpec(memory_space=pl.ANY)
    return pl.pallas_call(
        body, name=name,
        out_shape=[jax.ShapeDtypeStruct(s, a.dtype) for s, a in zip(out_shapes, arrays)],
        in_specs=[any_spec] * n, out_specs=[any_spec] * n,
        scratch_shapes=[pltpu.SemaphoreType.DMA((n, N_DEV - 1)), pltpu.SemaphoreType.DMA((n, N_DEV - 1)),
                        pltpu.SemaphoreType.DMA((n,))],
    )(*arrays)


def _all_gather(ffn_local, misc_local):
    w, r, d = ffn_local.shape

    def src_of(j, ref, idx):
        return ref

    def dst_of(j, ref, idx):
        return ref.at[:, idx] if j == 0 else ref.at[idx]

    return _exchange([ffn_local, misc_local], src_of, dst_of,
                     [(w, N_DEV, r, d), (N_DEV,) + misc_local.shape], "comm_all_gather")


def _grad_exchange(ffn_g, misc_g, rep_g):
    w, _, r, d = ffn_g.shape

    def src_of(j, ref, idx):
        return (ref.at[:, idx], ref.at[idx], ref)[j]

    def dst_of(j, ref, idx):
        return ref.at[idx]

    return _exchange([ffn_g, misc_g, rep_g], src_of, dst_of,
                     [(N_DEV, w, r, d), misc_g.shape, (N_DEV,) + rep_g.shape], "comm_grad_exchange")


def _ffn_fwd(h, g, wall, e):
    T, D = h.shape
    F = wall.shape[2]
    tm, tf = min(TOKEN_TILE, T), F // 2
    nf = F // tf

    def body(h_ref, g_ref, wg_ref, wu_ref, wd_ref, ho_ref, gate_ref, up_ref, u_sc, acc_sc):
        f = pl.program_id(1)

        @pl.when(f == 0)
        def _():
            u_sc[...] = _rms_fwd(h_ref[...], g_ref[...]).astype(BF16)
            acc_sc[...] = jnp.zeros_like(acc_sc)

        u = u_sc[...]
        gate = _dot_nt(u, wg_ref[...])
        up = _dot_nt(u, wu_ref[...])
        gate_ref[...] = gate.astype(BF16)
        up_ref[...] = up.astype(BF16)
        act = (gate * _sigmoid(gate) * up).astype(BF16)
        acc_sc[...] += _dot(act, wd_ref[...])

        @pl.when(f == nf - 1)
        def _():
            ho_ref[...] = h_ref[...] + 0.5 * acc_sc[...]

    def wspec(kind):
        return pl.BlockSpec((None, None, tf, D), lambda i, f: (e, kind, f, 0))

    return pl.pallas_call(
        body, name="ffn_fwd", grid=(T // tm, nf),
        in_specs=[pl.BlockSpec((tm, D), lambda i, f: (i, 0)), _full((1, D)), wspec(0), wspec(1), wspec(2)],
        out_specs=[pl.BlockSpec((tm, D), lambda i, f: (i, 0)),
                   pl.BlockSpec((tm, tf), lambda i, f: (i, f)), pl.BlockSpec((tm, tf), lambda i, f: (i, f))],
        out_shape=[jax.ShapeDtypeStruct((T, D), F32), jax.ShapeDtypeStruct((T, F), BF16),
                   jax.ShapeDtypeStruct((T, F), BF16)],
        scratch_shapes=[pltpu.VMEM((tm, D), BF16), pltpu.VMEM((tm, D), F32)],
        compiler_params=_cparams("parallel", "arbitrary"),
    )(h, g, wall, wall, wall)


def _ffn_bwd_dx(dho, h, g, gate, up, wall, e):
    T, D = h.shape
    F = wall.shape[2]
    tm, tf = min(TOKEN_TILE, T), F // 2
    nf = F // tf

    def body(dho_ref, h_ref, g_ref, gate_ref, up_ref, wg_ref, wu_ref, wd_ref,
             dhi_ref, dgate_ref, dup_ref, u_ref, dy_ref, dg_ref, dy_sc, acc_sc):
        i, f = pl.program_id(0), pl.program_id(1)

        @pl.when(f == 0)
        def _():
            dy = (0.5 * dho_ref[...]).astype(BF16)
            dy_sc[...] = dy
            dy_ref[...] = dy
            u_ref[...] = _rms_fwd(h_ref[...], g_ref[...]).astype(BF16)
            acc_sc[...] = jnp.zeros_like(acc_sc)

        dact = _dot_nt(dy_sc[...], wd_ref[...])
        gt = gate_ref[...].astype(F32)
        sig = _sigmoid(gt)
        dup = (dact * (gt * sig)).astype(BF16)
        dgate = (dact * up_ref[...].astype(F32) * (sig * (1.0 + gt * (1.0 - sig)))).astype(BF16)
        dup_ref[...] = dup
        dgate_ref[...] = dgate
        acc_sc[...] += _dot(dgate, wg_ref[...]) + _dot(dup, wu_ref[...])

        @pl.when(f == nf - 1)
        def _():
            dx, dg = _rms_bwd(acc_sc[...], h_ref[...], g_ref[...])
            dhi_ref[...] = dho_ref[...] + dx

            @pl.when(i == 0)
            def _():
                dg_ref[...] = jnp.zeros_like(dg_ref)

            dg_ref[...] += dg

    def wspec(kind):
        return pl.BlockSpec((None, None, tf, D), lambda i, f: (e, kind, f, 0))

    row = pl.BlockSpec((tm, D), lambda i, f: (i, 0))
    blk = pl.BlockSpec((tm, tf), lambda i, f: (i, f))
    return pl.pallas_call(
        body, name="ffn_bwd_dx", grid=(T // tm, nf),
        in_specs=[row, row, _full((1, D)), blk, blk, wspec(0), wspec(1), wspec(2)],
        out_specs=[row, blk, blk, row, row, _full((1, D))],
        out_shape=[jax.ShapeDtypeStruct((T, D), F32), jax.ShapeDtypeStruct((T, F), BF16),
                   jax.ShapeDtypeStruct((T, F), BF16), jax.ShapeDtypeStruct((T, D), BF16),
                   jax.ShapeDtypeStruct((T, D), BF16), jax.ShapeDtypeStruct((1, D), F32)],
        scratch_shapes=[pltpu.VMEM((tm, D), BF16), pltpu.VMEM((tm, D), F32)],
        compiler_params=_cparams("arbitrary", "arbitrary"),
    )(dho, h, g, gate, up, wall, wall, wall)


def _ffn_bwd_dw(dgate, dup, gate, up, u, dy, gbuf, e):
    T, F = gate.shape
    D = u.shape[1]
    tfw = F // 2
    tk = min(TOKEN_TILE, T)
    nk = T // tk

    def body(dgate_ref, dup_ref, gate_ref, up_ref, u_ref, dy_ref, gbuf_ref, out_ref, acc_sc):
        k = pl.program_id(1)

        @pl.when(k == 0)
        def _():
            acc_sc[...] = jnp.zeros_like(acc_sc)

        uu = u_ref[...]
        gt = gate_ref[...].astype(F32)
        act = (gt * _sigmoid(gt) * up_ref[...].astype(F32)).astype(BF16)
        acc_sc[0] += _dot_tn(dgate_ref[...], uu)
        acc_sc[1] += _dot_tn(dup_ref[...], uu)
        acc_sc[2] += _dot_tn(act, dy_ref[...])

        @pl.when(k == nk - 1)
        def _():
            out_ref[...] = acc_sc[...].astype(BF16)

    blk = pl.BlockSpec((tk, tfw), lambda j, k: (k, j))
    row = pl.BlockSpec((tk, D), lambda j, k: (k, 0))
    return pl.pallas_call(
        body, name="ffn_bwd_dw", grid=(F // tfw, nk),
        in_specs=[blk, blk, blk, blk, row, row, pl.BlockSpec(memory_space=pl.ANY)],
        out_specs=pl.BlockSpec((None, 3, tfw, D), lambda j, k: (e, 0, j, 0)),
        out_shape=jax.ShapeDtypeStruct(gbuf.shape, BF16),
        scratch_shapes=[pltpu.VMEM((3, tfw, D), F32)],
        input_output_aliases={6: 0},
        compiler_params=_cparams("parallel", "arbitrary"),
    )(dgate, dup, gate, up, u, dy, gbuf)


def _mm_rows(a, b, *, nt, out_dtype, norm_g=None, res=None, name):
    T, K = a.shape
    N = b.shape[0] if nt else b.shape[1]
    tm = min(TOKEN_TILE, T)
    has_g, has_r = norm_g is not None, res is not None

    def body(*refs):
        a_ref, b_ref = refs[0], refs[1]
        o_ref = refs[-1]
        x = a_ref[...]
        if has_g:
            x = _rms_fwd(x, refs[2][...])
        x = x.astype(BF16)
        acc = _dot_nt(x, b_ref[...]) if nt else _dot(x, b_ref[...])
        if has_r:
            acc = refs[2 + has_g][...] + acc
        o_ref[...] = acc.astype(out_dtype)

    ins, specs = [a, b], [pl.BlockSpec((tm, K), lambda i: (i, 0)), _full(b.shape)]
    if has_g:
        ins.append(norm_g)
        specs.append(_full((1, K)))
    if has_r:
        ins.append(res)
        specs.append(pl.BlockSpec((tm, N), lambda i: (i, 0)))
    return pl.pallas_call(
        body, name=name, grid=(T // tm,), in_specs=specs,
        out_specs=pl.BlockSpec((tm, N), lambda i: (i, 0)),
        out_shape=jax.ShapeDtypeStruct((T, N), out_dtype),
        compiler_params=_cparams("parallel"),
    )(*ins)


def _mm_tn(a, b, *, norm_g=None, name):
    T, M = a.shape
    N = b.shape[1]
    tk = min(TOKEN_TILE, T)
    has_g = norm_g is not None

    def body(*refs):
        a_ref, b_ref, o_ref = refs[0], refs[1], refs[-1]

        @pl.when(pl.program_id(0) == 0)
        def _():
            o_ref[...] = jnp.zeros_like(o_ref)

        x = a_ref[...]
        if has_g:
            x = _rms_fwd(x, refs[2][...])
        o_ref[...] += _dot_tn(x.astype(BF16), b_ref[...].astype(BF16))

    ins = [a, b]
    specs = [pl.BlockSpec((tk, M), lambda k: (k, 0)), pl.BlockSpec((tk, N), lambda k: (k, 0))]
    if has_g:
        ins.append(norm_g)
        specs.append(_full((1, M)))
    return pl.pallas_call(
        body, name=name, grid=(T // tk,), in_specs=specs, out_specs=_full((M, N)),
        out_shape=jax.ShapeDtypeStruct((M, N), F32),
        compiler_params=_cparams("arbitrary"),
    )(*ins)


def _proj_bwd(dz, w, h, g, dh, name):
    T, D = h.shape
    N = w.shape[1]
    tm = min(TOKEN_TILE, T)

    def body(dz_ref, w_ref, h_ref, g_ref, dh_ref, o_ref, dg_ref):
        du = _dot_nt(dz_ref[...].astype(BF16), w_ref[...])
        dx, dg = _rms_bwd(du, h_ref[...], g_ref[...])
        o_ref[...] = dh_ref[...] + dx

        @pl.when(pl.program_id(0) == 0)
        def _():
            dg_ref[...] = jnp.zeros_like(dg_ref)

        dg_ref[...] += dg

    row = pl.BlockSpec((tm, D), lambda i: (i, 0))
    return pl.pallas_call(
        body, name=name, grid=(T // tm,),
        in_specs=[pl.BlockSpec((tm, N), lambda i: (i, 0)), _full((D, N)), row, _full((1, D)), row],
        out_specs=[row, _full((1, D))],
        out_shape=[jax.ShapeDtypeStruct((T, D), F32), jax.ShapeDtypeStruct((1, D), F32)],
        compiler_params=_cparams("arbitrary"),
    )(dz, w, h, g, dh)


def _pool_bands(tm):
    r = np.arange(tm)[:, None]
    c = np.arange(tm)[None, :]
    j = np.arange(POOL_HALO)[None, :]
    main, halo, main_t, halo_t = [], [], [], []
    for w in POOL_WINDOWS:
        main.append(((r - c >= 0) & (r - c < w)) / w)
        halo.append((r + POOL_HALO - j < w) / w)
        main_t.append(((c - r >= 0) & (c - r < w)) / w)
        halo_t.append((tm + j - r < w) / w)
    return tuple(jnp.asarray(np.stack(m), BF16) for m in (main, halo, main_t, halo_t))


def _pool_count_scale(i, tm, w):
    t = i * tm + lax.broadcasted_iota(jnp.int32, (tm, 1), 0)
    return w / jnp.minimum(t + 1, w).astype(F32)


def _pool_fwd(h, g, wp, scale):
    T, D = h.shape
    G, dg = len(POOL_WINDOWS), D // len(POOL_WINDOWS)
    tm = min(TOKEN_TILE, T)
    hb = tm // POOL_HALO
    bm, bh, _, _ = _pool_bands(tm)

    def body(h_ref, hh_ref, g_ref, wp_ref, sc_ref, bm_ref, bh_ref, ho_ref, y_ref):
        i = pl.program_id(0)
        x = h_ref[...]
        u = _rms_fwd(x, g_ref[...])
        uh = _rms_fwd(hh_ref[...], g_ref[...]) * (i > 0).astype(F32)
        for gi, w in enumerate(POOL_WINDOWS):
            cols = slice(gi * dg, (gi + 1) * dg)
            ug = u[:, cols]
            hi, lo = _split_bf16(ug)
            hhi, hlo = _split_bf16(uh[:, cols])
            s = (_dot(bm_ref[gi], hi) + _dot(bm_ref[gi], lo)
                 + _dot(bh_ref[gi], hhi) + _dot(bh_ref[gi], hlo))
            y = (s * _pool_count_scale(i, tm, w) - ug).astype(BF16)
            y_ref[:, cols] = y
            ho_ref[:, cols] = x[:, cols] + _dot(y, wp_ref[gi]) * sc_ref[:, cols]

    row = pl.BlockSpec((tm, D), lambda i: (i, 0))
    return pl.pallas_call(
        body, name="pool_fwd", grid=(T // tm,),
        in_specs=[row, pl.BlockSpec((POOL_HALO, D), lambda i: (jnp.maximum(i * hb - 1, 0), 0)),
                  _full((1, D)), _full((G, dg, dg)), _full((1, D)),
                  _full((G, tm, tm)), _full((G, tm, POOL_HALO))],
        out_specs=[row, row],
        out_shape=[jax.ShapeDtypeStruct((T, D), F32), jax.ShapeDtypeStruct((T, D), BF16)],
        compiler_params=_cparams("parallel"),
    )(h, h, g, wp, scale, bm, bh)


def _pool_bwd(dh, h, g, y, wp, scale):
    T, D = h.shape
    G, dg = len(POOL_WINDOWS), D // len(POOL_WINDOWS)
    tm = min(TOKEN_TILE, T)
    hb = tm // POOL_HALO
    nt = T // tm
    _, _, bmt, bht = _pool_bands(tm)

    def body(dh_ref, dhn_ref, h_ref, g_ref, y_ref, wp_ref, sc_ref, bmt_ref, bht_ref,
             o_ref, dg_ref, dwp_ref, dsc_ref, du_sc):
        i = pl.program_id(0)

        @pl.when(i == 0)
        def _():
            dg_ref[...] = jnp.zeros_like(dg_ref)
            dwp_ref[...] = jnp.zeros_like(dwp_ref)
            dsc_ref[...] = jnp.zeros_like(dsc_ref)

        dho = dh_ref[...]
        dz = dho * sc_ref[...]
        dzn = dhn_ref[...] * sc_ref[...] * (i < nt - 1).astype(F32)
        for gi, w in enumerate(POOL_WINDOWS):
            cols = slice(gi * dg, (gi + 1) * dg)
            yg = y_ref[:, cols]
            dzg = dz[:, cols].astype(BF16)
            dsc_ref[:, cols] += jnp.sum(dho[:, cols] * _dot(yg, wp_ref[gi]), axis=0, keepdims=True)
            dwp_ref[gi] += _dot_tn(yg, dzg)
            dy = _dot_nt(dzg, wp_ref[gi])
            dyn = _dot_nt(dzn[:, cols].astype(BF16), wp_ref[gi])
            hi, lo = _split_bf16(dy * _pool_count_scale(i, tm, w))
            nhi, nlo = _split_bf16(dyn)
            du_sc[:, cols] = (_dot(bmt_ref[gi], hi) + _dot(bmt_ref[gi], lo)
                              + _dot(bht_ref[gi], nhi) + _dot(bht_ref[gi], nlo) - dy)
        dx, dgp = _rms_bwd(du_sc[...], h_ref[...], g_ref[...])
        o_ref[...] = dho + dx
        dg_ref[...] += dgp

    row = pl.BlockSpec((tm, D), lambda i: (i, 0))
    return pl.pallas_call(
        body, name="pool_bwd", grid=(nt,),
        in_specs=[row, pl.BlockSpec((POOL_HALO, D), lambda i: (jnp.minimum((i + 1) * hb, T // POOL_HALO - 1), 0)),
                  row, _full((1, D)), row, _full((G, dg, dg)), _full((1, D)),
                  _full((G, tm, tm)), _full((G, tm, POOL_HALO))],
        out_specs=[row, _full((1, D)), _full((G, dg, dg)), _full((1, D))],
        out_shape=[jax.ShapeDtypeStruct((T, D), F32), jax.ShapeDtypeStruct((1, D), F32),
                   jax.ShapeDtypeStruct((G, dg, dg), F32), jax.ShapeDtypeStruct((1, D), F32)],
        scratch_shapes=[pltpu.VMEM((tm, D), F32)],
        compiler_params=_cparams("arbitrary"),
    )(dh, dh, h, g, y, wp, scale, bmt, bht)


def _rope_tables(T):
    pos = jnp.arange(T, dtype=F32)
    inv_freq = ROPE_THETA ** (-jnp.arange(0, D_ROPE, 2, dtype=F32) / D_ROPE)
    ang = pos[:, None] * inv_freq[None, :]
    cos2 = jnp.tile(jnp.cos(ang), (1, 2))
    sin2 = jnp.tile(jnp.sin(ang), (1, 2))
    pad = jnp.zeros((T, D_HEAD_PAD - D_QK), F32)
    ca_q = jnp.concatenate([jnp.ones((T, D_NOPE), F32), cos2, pad], axis=1)
    ca_k = jnp.concatenate([jnp.zeros((T, D_NOPE), F32), cos2, pad], axis=1)
    sb = jnp.concatenate([jnp.zeros((T, D_NOPE), F32), sin2, pad], axis=1)
    return ca_q, ca_k, sb


def _rope_weight_pair(w):
    half = D_ROPE // 2
    z_pad = jnp.zeros(w.shape[:-1] + (D_HEAD_PAD - D_QK,), w.dtype)
    z_nope = jnp.zeros(w.shape[:-1] + (D_NOPE,), w.dtype)
    wa = jnp.concatenate([w, z_pad], axis=-1)
    wb = jnp.concatenate([z_nope, -w[..., D_NOPE + half:], w[..., D_NOPE:D_NOPE + half], z_pad], axis=-1)
    return wa, wb


def _rope_weight_pair_grad(dwa, dwb):
    half = D_ROPE // 2
    d1 = dwa[..., D_NOPE:D_NOPE + half] + dwb[..., D_NOPE + half:D_QK]
    d2 = dwa[..., D_NOPE + half:D_QK] - dwb[..., D_NOPE:D_NOPE + half]
    return jnp.concatenate([dwa[..., :D_NOPE], d1, d2], axis=-1)


def _q_proj(cq, qg, wa, wb, ca, sb):
    T, R = cq.shape
    tm = min(TOKEN_TILE, T)
    P = D_HEAD_PAD
    GP = HEAD_GROUP * P

    def body(cq_ref, qg_ref, wa_ref, wb_ref, ca_ref, sb_ref, q_ref):
        c = _rms_fwd(cq_ref[...], qg_ref[...]).astype(BF16)
        ca = jnp.tile(ca_ref[...], (1, HEAD_GROUP))
        sb = jnp.tile(sb_ref[...], (1, HEAD_GROUP))
        q_ref[...] = (_dot(c, wa_ref[...]) * ca + _dot(c, wb_ref[...]) * sb).astype(BF16)

    tok = pl.BlockSpec((tm, P), lambda i, hh: (i, 0))
    wsp = pl.BlockSpec((R, GP), lambda i, hh: (0, hh))
    return pl.pallas_call(
        body, name="q_proj", grid=(T // tm, N_HEADS // HEAD_GROUP),
        in_specs=[pl.BlockSpec((tm, R), lambda i, hh: (i, 0)), _full((1, R)), wsp, wsp, tok, tok],
        out_specs=pl.BlockSpec((tm, GP), lambda i, hh: (i, hh)),
        out_shape=jax.ShapeDtypeStruct((T, N_HEADS * P), BF16),
        compiler_params=_cparams("parallel", "arbitrary"),
    )(cq, qg, wa, wb, ca, sb)


def _q_proj_bwd(dq, cq, qg, wa, wb, ca, sb):
    T, R = cq.shape
    tm = min(TOKEN_TILE, T)
    P = D_HEAD_PAD

    def body(dq_ref, cq_ref, qg_ref, wa_ref, wb_ref, ca_ref, sb_ref,
             da_ref, db_ref, cqn_ref, dcq_ref, dqg_ref, acc_sc):
        i, hh = pl.program_id(0), pl.program_id(1)

        @pl.when(hh == 0)
        def _():
            acc_sc[...] = jnp.zeros_like(acc_sc)
            cqn_ref[...] = _rms_fwd(cq_ref[...], qg_ref[...]).astype(BF16)

        d = dq_ref[...].T
        da = (d * jnp.tile(ca_ref[...], (1, HEAD_GROUP))).astype(BF16)
        db = (d * jnp.tile(sb_ref[...], (1, HEAD_GROUP))).astype(BF16)
        da_ref[...] = da
        db_ref[...] = db
        acc_sc[...] += _dot_nt(da, wa_ref[...]) + _dot_nt(db, wb_ref[...])

        @pl.when(hh == N_HEADS // HEAD_GROUP - 1)
        def _():
            dx, dg = _rms_bwd(acc_sc[...], cq_ref[...], qg_ref[...])
            dcq_ref[...] = dx

            @pl.when(i == 0)
            def _():
                dqg_ref[...] = jnp.zeros_like(dqg_ref)

            dqg_ref[...] += dg

    GP = HEAD_GROUP * P
    tok = pl.BlockSpec((tm, P), lambda i, hh: (i, 0))
    hd = pl.BlockSpec((tm, GP), lambda i, hh: (i, hh))
    wsp = pl.BlockSpec((R, GP), lambda i, hh: (0, hh))
    rr = pl.BlockSpec((tm, R), lambda i, hh: (i, 0))
    return pl.pallas_call(
        body, name="q_proj_bwd", grid=(T // tm, N_HEADS // HEAD_GROUP),
        in_specs=[pl.BlockSpec((GP, tm), lambda i, hh: (hh, i)), rr, _full((1, R)), wsp, wsp, tok, tok],
        out_specs=[hd, hd, rr, rr, _full((1, R))],
        out_shape=[jax.ShapeDtypeStruct((T, N_HEADS * P), BF16), jax.ShapeDtypeStruct((T, N_HEADS * P), BF16),
                   jax.ShapeDtypeStruct((T, R), BF16), jax.ShapeDtypeStruct((T, R), F32),
                   jax.ShapeDtypeStruct((1, R), F32)],
        scratch_shapes=[pltpu.VMEM((tm, R), F32)],
        compiler_params=_cparams("arbitrary", "arbitrary"),
    )(dq, cq, qg, wa, wb, ca, sb)


def _kv_proj(h, g_in, wka, wkb, g_c, wuk, wuv, ca, sb):
    T, D = h.shape
    tm = min(TOKEN_TILE, T)
    P, C = D_HEAD_PAD, D_NOPE

    def body(h_ref, gi_ref, wka_ref, wkb_ref, gc_ref, wuk_ref, wuv_ref, ca_ref, sb_ref, k_ref, v_ref, craw_ref):
        u = _rms_fwd(h_ref[...], gi_ref[...]).astype(BF16)
        kva = _dot(u, wka_ref[...])
        kvb = _dot(u, wkb_ref[...])
        craw = kva[:, :C]
        craw_ref[...] = craw
        c = _rms_fwd(craw, gc_ref[...]).astype(BF16)
        kr = kva * ca_ref[...] + kvb * sb_ref[...]
        kn = _dot(c, wuk_ref[...])
        for hh in range(N_HEADS):
            k_ref[:, hh * P:(hh + 1) * P] = (kn[:, hh * P:(hh + 1) * P] + kr).astype(BF16)
        v_ref[...] = _dot(c, wuv_ref[...]).astype(BF16)

    tok = pl.BlockSpec((tm, P), lambda i: (i, 0))
    return pl.pallas_call(
        body, name="kv_proj", grid=(T // tm,),
        in_specs=[pl.BlockSpec((tm, D), lambda i: (i, 0)), _full((1, D)), _full((D, P)), _full((D, P)),
                  _full((1, C)), _full(wuk.shape), _full(wuv.shape), tok, tok],
        out_specs=[pl.BlockSpec((tm, N_HEADS * P), lambda i: (i, 0)),
                   pl.BlockSpec((tm, N_HEADS * D_V), lambda i: (i, 0)), pl.BlockSpec((tm, C), lambda i: (i, 0))],
        out_shape=[jax.ShapeDtypeStruct((T, N_HEADS * P), BF16), jax.ShapeDtypeStruct((T, N_HEADS * D_V), BF16),
                   jax.ShapeDtypeStruct((T, C), F32)],
        compiler_params=_cparams("parallel"),
    )(h, g_in, wka, wkb, g_c, wuk, wuv, ca, sb)


def _kv_proj_bwd(dk, dv, dh, h, g_in, wka, wkb, craw, g_c, wuk, wuv, ca, sb):
    T, D = h.shape
    tm = min(TOKEN_TILE // 2, T)
    P, C = D_HEAD_PAD, D_NOPE

    def body(dk_ref, dv_ref, dh_ref, h_ref, gi_ref, wka_ref, wkb_ref, craw_ref, gc_ref, wuk_ref, wuv_ref,
             ca_ref, sb_ref, o_ref, dgi_ref, dwka_ref, dwkb_ref, dgc_ref, dwuk_ref, dwuv_ref):
        @pl.when(pl.program_id(0) == 0)
        def _():
            for r in (dgi_ref, dwka_ref, dwkb_ref, dgc_ref, dwuk_ref, dwuv_ref):
                r[...] = jnp.zeros_like(r)

        x = h_ref[...]
        u = _rms_fwd(x, gi_ref[...]).astype(BF16)
        craw = craw_ref[...]
        c = _rms_fwd(craw, gc_ref[...]).astype(BF16)
        dkf = dk_ref[...]
        dkb = dkf.astype(BF16)
        dvb = dv_ref[...].astype(BF16)
        dwuk_ref[...] += _dot_tn(c, dkb)
        dwuv_ref[...] += _dot_tn(c, dvb)
        dc = _dot_nt(dkb, wuk_ref[...]) + _dot_nt(dvb, wuv_ref[...])
        dkr = dkf[:, :P]
        for hh in range(1, N_HEADS):
            dkr = dkr + dkf[:, hh * P:(hh + 1) * P]
        dcraw, dgc = _rms_bwd(dc, craw, gc_ref[...])
        dgc_ref[...] += dgc
        dkva = jnp.concatenate([dcraw, (dkr * ca_ref[...])[:, C:]], axis=1).astype(BF16)
        dkvb = (dkr * sb_ref[...]).astype(BF16)
        dwka_ref[...] += _dot_tn(u, dkva)
        dwkb_ref[...] += _dot_tn(u, dkvb)
        du = _dot_nt(dkva, wka_ref[...]) + _dot_nt(dkvb, wkb_ref[...])
        dx, dgi = _rms_bwd(du, x, gi_ref[...])
        dgi_ref[...] += dgi
        o_ref[...] = dh_ref[...] + dx

    row = pl.BlockSpec((tm, D), lambda i: (i, 0))
    tok = pl.BlockSpec((tm, P), lambda i: (i, 0))
    return pl.pallas_call(
        body, name="kv_proj_bwd", grid=(T // tm,),
        in_specs=[pl.BlockSpec((tm, N_HEADS * P), lambda i: (i, 0)),
                  pl.BlockSpec((tm, N_HEADS * D_V), lambda i: (i, 0)), row, row, _full((1, D)),
                  _full((D, P)), _full((D, P)), pl.BlockSpec((tm, C), lambda i: (i, 0)), _full((1, C)),
                  _full(wuk.shape), _full(wuv.shape), tok, tok],
        out_specs=[row, _full((1, D)), _full((D, P)), _full((D, P)), _full((1, C)),
                   _full(wuk.shape), _full(wuv.shape)],
        out_shape=[jax.ShapeDtypeStruct((T, D), F32), jax.ShapeDtypeStruct((1, D), F32),
                   jax.ShapeDtypeStruct((D, P), F32), jax.ShapeDtypeStruct((D, P), F32),
                   jax.ShapeDtypeStruct((1, C), F32), jax.ShapeDtypeStruct(wuk.shape, F32),
                   jax.ShapeDtypeStruct(wuv.shape, F32)],
        compiler_params=_cparams("arbitrary"),
    )(dk, dv, dh, h, g_in, wka, wkb, craw, g_c, wuk, wuv, ca, sb)


_ATTN_SCALE = D_QK ** -0.5
_LOG2_E = 1.4426950408889634
_ATTN_SCALE_LOG2 = _ATTN_SCALE * _LOG2_E


def _causal_mask(t):
    return lax.broadcasted_iota(jnp.int32, (t, t), 1) <= lax.broadcasted_iota(jnp.int32, (t, t), 0)


def _causal_mask_t(t):
    return lax.broadcasted_iota(jnp.int32, (t, t), 0) <= lax.broadcasted_iota(jnp.int32, (t, t), 1)


def _flash_fwd(q, k, v):
    T = q.shape[0]
    t = min(ATTN_TILE, T)
    P = D_HEAD_PAD

    def body(q_ref, k_ref, v_ref, o_ref, lse_ref, m_sc, l_sc, acc_sc, s0_sc, s1_sc, p0_sc, p1_sc, a0_sc, a1_sc):
        n = pl.program_id(1) + 1
        s_sc, p_sc, a_sc = (s0_sc, s1_sc), (p0_sc, p1_sc), (a0_sc, a1_sc)
        m_sc[...] = jnp.full_like(m_sc, NEG_BIG)
        l_sc[...] = jnp.zeros_like(l_sc)
        acc_sc[...] = jnp.zeros_like(acc_sc)

        def rows_of(c):
            return pl.ds(pl.multiple_of(c * t, t), t)

        def scores(c, slot):
            s_sc[slot][...] = _dot_nt(k_ref[rows_of(c), :], q_ref[...])

        def softmax(slot, diagonal):
            s_t = s_sc[slot][...]
            if diagonal:
                s_t = jnp.where(_causal_mask_t(t), s_t, NEG_BIG)
            m_prev = m_sc[...]
            m_new = jnp.maximum(m_prev, jnp.max(s_t, axis=0, keepdims=True))
            p_t = jnp.exp2((s_t - m_new) * _ATTN_SCALE_LOG2)
            alpha = jnp.exp2((m_prev - m_new) * _ATTN_SCALE_LOG2)
            l_sc[...] = alpha * l_sc[...] + jnp.sum(p_t, axis=0, keepdims=True)
            m_sc[...] = m_new
            p_sc[slot][...] = p_t.astype(BF16)
            a_sc[slot][...] = alpha

        def values(c, slot):
            acc_sc[...] = a_sc[slot][...] * acc_sc[...] + _dot_tn(v_ref[rows_of(c), :], p_sc[slot][...])

        def stage(c, slot, first=False, last=False):
            if not first:
                values(c - 1, 1 - slot)
            if not last:
                scores(c + 1, 1 - slot)
            softmax(slot, diagonal=last)

        scores(0, 0)

        @pl.when(n == 1)
        def _():
            softmax(0, diagonal=True)
            values(0, 0)

        @pl.when(n >= 2)
        def _():
            stage(0, 0, first=True)

            def pair(j, carry):
                stage(1 + 2 * j, 1)
                stage(2 + 2 * j, 0)
                return carry

            lax.fori_loop(0, (n - 2) // 2, pair, 0)

            @pl.when((n - 2) % 2 == 1)
            def _():
                stage(n - 2, 1)

            for slot in range(2):
                @pl.when((n - 1) % 2 == slot)
                def _():
                    stage(n - 1, slot, last=True)
                    values(n - 1, slot)

        l = l_sc[...]
        o_ref[...] = (acc_sc[...] / l).T.astype(BF16)
        lse_ref[...] = m_sc[...] * _ATTN_SCALE + jnp.log(l)

    return pl.pallas_call(
        body, name="flash_fwd", grid=(N_HEADS, T // t),
        in_specs=[pl.BlockSpec((t, P), lambda hh, i: (i, hh)), pl.BlockSpec((T, P), lambda hh, i: (0, hh)),
                  pl.BlockSpec((T, D_V), lambda hh, i: (0, hh))],
        out_specs=[pl.BlockSpec((t, D_V), lambda hh, i: (i, hh)),
                   pl.BlockSpec((None, None, 1, t), lambda hh, i: (hh, i, 0, 0))],
        out_shape=[jax.ShapeDtypeStruct((T, N_HEADS * D_V), BF16),
                   jax.ShapeDtypeStruct((N_HEADS, T // t, 1, t), F32)],
        scratch_shapes=[pltpu.VMEM((1, t), F32), pltpu.VMEM((1, t), F32), pltpu.VMEM((D_V, t), F32),
                        pltpu.VMEM((t, t), F32), pltpu.VMEM((t, t), F32), pltpu.VMEM((t, t), BF16),
                        pltpu.VMEM((t, t), BF16), pltpu.VMEM((1, t), F32), pltpu.VMEM((1, t), F32)],
        compiler_params=_cparams("parallel", "arbitrary"),
    )(q, k, v)


def _attn_delta(o, do):
    T = o.shape[0]
    t = min(ATTN_TILE, T)

    def body(o_ref, do_ref, out_ref):
        ones = jnp.ones((8, D_V), BF16)
        for hh in range(N_HEADS):
            cols = slice(hh * D_V, (hh + 1) * D_V)
            hi, lo = _split_bf16(o_ref[:, cols].astype(F32) * do_ref[:, cols].astype(F32))
            out_ref[hh] = (_dot_nt(ones, hi) + _dot_nt(ones, lo))[0:1]

    tok = pl.BlockSpec((t, N_HEADS * D_V), lambda i: (i, 0))
    return pl.pallas_call(
        body, name="attn_delta", grid=(T // t,), in_specs=[tok, tok],
        out_specs=pl.BlockSpec((N_HEADS, None, 1, t), lambda i: (0, i, 0, 0)),
        out_shape=jax.ShapeDtypeStruct((N_HEADS, T // t, 1, t), F32),
        compiler_params=_cparams("parallel"),
    )(o, do)


def _flash_bwd(q, k, v, do, lse_row, delta_row, dk_prev, dv_prev):
    T = q.shape[0]
    t = min(ATTN_TILE, T)
    nq = T // t
    P = D_HEAD_PAD

    def body(k_ref, v_ref, q_ref, do_ref, lse_ref, delta_ref, dkp_ref, dvp_ref,
             dqt_ref, dk_ref, dv_ref, dk_sc, dv_sc, dqt_sc, kt_sc,
             s0_sc, s1_sc, dp0_sc, dp1_sc, p0_sc, p1_sc, ds0_sc, ds1_sc):
        ki = pl.program_id(1)
        n = nq - ki
        s_sc, dp_sc, p_sc, ds_sc = (s0_sc, s1_sc), (dp0_sc, dp1_sc), (p0_sc, p1_sc), (ds0_sc, ds1_sc)
        kt_sc[...] = k_ref[...].astype(F32).T.astype(BF16)
        dk_sc[...] = dkp_ref[...]
        dv_sc[...] = dvp_ref[...]

        @pl.when(ki == 0)
        def _():
            dqt_sc[...] = jnp.zeros_like(dqt_sc)

        def rows_of(c):
            return pl.ds(pl.multiple_of((ki + c) * t, t), t)

        def products(c, slot):
            s_sc[slot][...] = _dot_nt(k_ref[...], q_ref[rows_of(c), :])
            dp_sc[slot][...] = _dot_nt(v_ref[...], do_ref[rows_of(c), :])

        def elementwise(c, slot, diagonal):
            p_t = jnp.exp2(s_sc[slot][...] * _ATTN_SCALE_LOG2 - lse_ref[ki + c] * _LOG2_E)
            if diagonal:
                p_t = jnp.where(_causal_mask_t(t), p_t, 0.0)
            p_sc[slot][...] = p_t.astype(BF16)
            ds_sc[slot][...] = (p_t * (dp_sc[slot][...] - delta_ref[ki + c]) * _ATTN_SCALE).astype(BF16)

        def gradients(c, slot):
            dv_sc[...] += _dot(p_sc[slot][...], do_ref[rows_of(c), :])
            ds_t = ds_sc[slot][...]
            dk_sc[...] += _dot(ds_t, q_ref[rows_of(c), :])
            dqt_sc[ki + c] += _dot(kt_sc[...], ds_t)

        def stage(c, slot, first=False, last=False):
            if not first:
                gradients(c - 1, 1 - slot)
            if not last:
                products(c + 1, 1 - slot)
            elementwise(c, slot, diagonal=first)

        products(0, 0)

        @pl.when(n == 1)
        def _():
            elementwise(0, 0, diagonal=True)
            gradients(0, 0)

        @pl.when(n >= 2)
        def _():
            stage(0, 0, first=True)

            def pair(j, carry):
                stage(1 + 2 * j, 1)
                stage(2 + 2 * j, 0)
                return carry

            lax.fori_loop(0, (n - 2) // 2, pair, 0)

            @pl.when((n - 2) % 2 == 1)
            def _():
                stage(n - 2, 1)

            for slot in range(2):
                @pl.when((n - 1) % 2 == slot)
                def _():
                    stage(n - 1, slot, last=True)
                    gradients(n - 1, slot)

        dqt_ref[...] = dqt_sc[ki]
        dk_ref[...] = dk_sc[...]
        dv_ref[...] = dv_sc[...]

    kb = pl.BlockSpec((t, P), lambda hh, i: (i, hh))
    vb = pl.BlockSpec((t, D_V), lambda hh, i: (i, hh))
    stat = pl.BlockSpec((None, nq, 1, t), lambda hh, i: (hh, 0, 0, 0))
    return pl.pallas_call(
        body, name="flash_bwd", grid=(N_HEADS, nq),
        in_specs=[kb, vb,
                  pl.BlockSpec((T, P), lambda hh, i: (0, hh), pipeline_mode=pl.Buffered(1)),
                  pl.BlockSpec((T, D_V), lambda hh, i: (0, hh), pipeline_mode=pl.Buffered(1)),
                  stat, stat, kb, vb],
        out_specs=[pl.BlockSpec((P, t), lambda hh, i: (hh, i)), kb, vb],
        out_shape=[jax.ShapeDtypeStruct((N_HEADS * P, T), F32), jax.ShapeDtypeStruct((T, N_HEADS * P), F32),
                   jax.ShapeDtypeStruct((T, N_HEADS * D_V), F32)],
        scratch_shapes=[pltpu.VMEM((t, P), F32), pltpu.VMEM((t, D_V), F32), pltpu.VMEM((nq, P, t), F32),
                        pltpu.VMEM((P, t), BF16)] + [pltpu.VMEM((t, t), F32)] * 4 + [pltpu.VMEM((t, t), BF16)] * 4,
        compiler_params=_cparams("arbitrary", "arbitrary"),
    )(k, v, q, do, lse_row, delta_row, dk_prev, dv_prev)


def _loss_head(h, g, target):
    T, D = h.shape
    tm = min(TOKEN_TILE, T)

    def body(h_ref, g_ref, t_ref, dh_ref, loss_ref, dg_ref):
        @pl.when(pl.program_id(0) == 0)
        def _():
            loss_ref[...] = jnp.zeros_like(loss_ref)
            dg_ref[...] = jnp.zeros_like(dg_ref)

        x = h_ref[...]
        err = _rms_fwd(x, g_ref[...]) - t_ref[...]
        per_tok = jnp.mean(err * err, axis=-1, keepdims=True)
        loss_ref[...] += 0.5 * jnp.sum(per_tok, axis=0, keepdims=True)
        dx, dg = _rms_bwd(err * (1.0 / D), x, g_ref[...])
        dh_ref[...] = dx
        dg_ref[...] += dg

    row = pl.BlockSpec((tm, D), lambda i: (i, 0))
    return pl.pallas_call(
        body, name="loss_head", grid=(T // tm,),
        in_specs=[row, _full((1, D)), row], out_specs=[row, _full((1, 128)), _full((1, D))],
        out_shape=[jax.ShapeDtypeStruct((T, D), F32), jax.ShapeDtypeStruct((1, 128), F32),
                   jax.ShapeDtypeStruct((1, D), F32)],
        compiler_params=_cparams("arbitrary"),
    )(h, g, target)


def _sum_parts(parts, tr, name):
    _, R, C = parts.shape

    def body(p_ref, o_ref):
        acc = p_ref[0].astype(F32)
        for j in range(1, N_DEV):
            acc = acc + p_ref[j].astype(F32)
        o_ref[...] = acc

    return pl.pallas_call(
        body, name=name, grid=(R // tr,),
        in_specs=[pl.BlockSpec((N_DEV, tr, C), lambda i: (0, i, 0))],
        out_specs=pl.BlockSpec((tr, C), lambda i: (i, 0)),
        out_shape=jax.ShapeDtypeStruct((R, C), F32),
        compiler_params=_cparams("parallel"),
    )(parts)


def _adamw(w, g, m, v):
    R, C = w.shape
    tr = _row_tile(R, TOKEN_TILE)

    def body(w_ref, g_ref, m_ref, v_ref, d_ref, mo_ref, vo_ref):
        gg = g_ref[...]
        mn = ADAM_B1 * m_ref[...] + (1.0 - ADAM_B1) * gg
        vn = ADAM_B2 * v_ref[...] + (1.0 - ADAM_B2) * (gg * gg)
        m_hat = mn / (1.0 - ADAM_B1 ** ADAM_STEP)
        v_hat = vn / (1.0 - ADAM_B2 ** ADAM_STEP)
        d_ref[...] = -ADAM_LR * (m_hat / (jnp.sqrt(v_hat) + ADAM_EPS) + ADAM_WD * w_ref[...])
        mo_ref[...] = mn
        vo_ref[...] = vn

    blk = pl.BlockSpec((tr, C), lambda i: (i, 0))
    return pl.pallas_call(
        body, name="adamw", grid=(R // tr,), in_specs=[blk] * 4, out_specs=[blk] * 3,
        out_shape=[jax.ShapeDtypeStruct((R, C), F32)] * 3,
        compiler_params=_cparams("parallel"),
    )(w, g, m, v)


def _adamw_nd(w, g, m, v):
    shape = w.shape
    two_d = (1, shape[0]) if len(shape) == 1 else (int(np.prod(shape[:-1])), shape[-1])
    outs = _adamw(w.reshape(two_d), g.reshape(two_d), m.reshape(two_d), v.reshape(two_d))
    return tuple(o.reshape(shape) for o in outs)


def _f32_as_bf16_pairs(a):
    return lax.bitcast_convert_type(a, BF16).reshape(a.shape[:-1] + (a.shape[-1] * 2,))


def _bf16_pairs_as_f32(a):
    return lax.bitcast_convert_type(a.reshape(a.shape[:-1] + (a.shape[-1] // 2, 2)), F32)


def _pack_misc(w_o, w_dq, w_uq, w_dkv, pool_w, pool_scale):
    lead = w_o.shape[:-3]
    rows = [w_o, w_dq, w_uq, w_dkv, pool_w]
    flat = [r.astype(BF16).reshape(lead + (-1, REP_COLS)) for r in rows]
    ps = _f32_as_bf16_pairs(pool_scale.astype(F32)).reshape(lead + (1, -1))
    ps = jnp.concatenate([ps, jnp.zeros(lead + (1, REP_COLS - ps.shape[-1]), BF16)], axis=-1)
    used = sum(f.shape[-2] for f in flat) + 1
    pad = jnp.zeros(lead + (MISC_ROWS - used, REP_COLS), BF16)
    return jnp.concatenate(flat + [ps, pad], axis=-2)


def _unpack_misc(buf, shapes):
    out, r0 = [], 0
    for shp in shapes[:-1]:
        n = int(np.prod(shp)) // REP_COLS
        out.append(buf[:, r0:r0 + n].reshape((N_DEV,) + shp))
        r0 += n
    n_ps = int(np.prod(shapes[-1]))
    out.append(_bf16_pairs_as_f32(buf[:, r0, :2 * n_ps]).reshape((N_DEV,) + shapes[-1]))
    return out


def _cat_dev(a, axis):
    a = jnp.moveaxis(a, 0, axis)
    return a.reshape(a.shape[:axis] + (a.shape[axis] * a.shape[axis + 1],) + a.shape[axis + 2:])


def _split_dev(a, axis):
    a = a.reshape(a.shape[:axis] + (N_DEV, a.shape[axis] // N_DEV) + a.shape[axis + 1:])
    return jnp.moveaxis(a, axis, 0)


def kernel(x, ffn_pre_norm, ffn_pre_wg, ffn_pre_wu, ffn_pre_wd, mix_norm, ffn_post_norm, ffn_post_wg, ffn_post_wu, ffn_post_wd, pool_w, pool_scale, kv_in_norm, w_dkv, ckv_norm, w_uk, w_uv, q_lora_norm, w_dq, w_uq, w_o, final_norm, loss_target, m_ffn_pre_norm, m_ffn_pre_wg, m_ffn_pre_wu, m_ffn_pre_wd, m_mix_norm, m_ffn_post_norm, m_ffn_post_wg, m_ffn_post_wu, m_ffn_post_wd, m_pool_w, m_pool_scale, m_kv_in_norm, m_w_dkv, m_ckv_norm, m_w_uk, m_w_uv, m_q_lora_norm, m_w_dq, m_w_uq, m_w_o, m_final_norm, v_ffn_pre_norm, v_ffn_pre_wg, v_ffn_pre_wu, v_ffn_pre_wd, v_mix_norm, v_ffn_post_norm, v_ffn_post_wg, v_ffn_post_wu, v_ffn_post_wd, v_pool_w, v_pool_scale, v_kv_in_norm, v_w_dkv, v_ckv_norm, v_w_uk, v_w_uv, v_q_lora_norm, v_w_dq, v_w_uq, v_w_o, v_final_norm):
    weights = dict(ffn_pre_norm=ffn_pre_norm, ffn_pre_wg=ffn_pre_wg, ffn_pre_wu=ffn_pre_wu, ffn_pre_wd=ffn_pre_wd,
                   mix_norm=mix_norm, ffn_post_norm=ffn_post_norm, ffn_post_wg=ffn_post_wg,
                   ffn_post_wu=ffn_post_wu, ffn_post_wd=ffn_post_wd, pool_w=pool_w, pool_scale=pool_scale,
                   kv_in_norm=kv_in_norm, w_dkv=w_dkv, ckv_norm=ckv_norm, w_uk=w_uk, w_uv=w_uv,
                   q_lora_norm=q_lora_norm, w_dq=w_dq, w_uq=w_uq, w_o=w_o, final_norm=final_norm)
    moments_m = dict(ffn_pre_norm=m_ffn_pre_norm, ffn_pre_wg=m_ffn_pre_wg, ffn_pre_wu=m_ffn_pre_wu,
                     ffn_pre_wd=m_ffn_pre_wd, mix_norm=m_mix_norm, ffn_post_norm=m_ffn_post_norm,
                     ffn_post_wg=m_ffn_post_wg, ffn_post_wu=m_ffn_post_wu, ffn_post_wd=m_ffn_post_wd,
                     pool_w=m_pool_w, pool_scale=m_pool_scale, kv_in_norm=m_kv_in_norm, w_dkv=m_w_dkv,
                     ckv_norm=m_ckv_norm, w_uk=m_w_uk, w_uv=m_w_uv, q_lora_norm=m_q_lora_norm, w_dq=m_w_dq,
                     w_uq=m_w_uq, w_o=m_w_o, final_norm=m_final_norm)
    moments_v = dict(ffn_pre_norm=v_ffn_pre_norm, ffn_pre_wg=v_ffn_pre_wg, ffn_pre_wu=v_ffn_pre_wu,
                     ffn_pre_wd=v_ffn_pre_wd, mix_norm=v_mix_norm, ffn_post_norm=v_ffn_post_norm,
                     ffn_post_wg=v_ffn_post_wg, ffn_post_wu=v_ffn_post_wu, ffn_post_wd=v_ffn_post_wd,
                     pool_w=v_pool_w, pool_scale=v_pool_scale, kv_in_norm=v_kv_in_norm, w_dkv=v_w_dkv,
                     ckv_norm=v_ckv_norm, w_uk=v_w_uk, w_uv=v_w_uv, q_lora_norm=v_q_lora_norm, w_dq=v_w_dq,
                     w_uq=v_w_uq, w_o=v_w_o, final_norm=v_final_norm)
    order = list(weights)

    T, D = x.shape[1], x.shape[2]
    depth = ffn_pre_norm.shape[0]
    n_a = pool_w.shape[0]
    n_b = depth - n_a
    fs = ffn_pre_wd.shape[1]
    F = fs * N_DEV
    n_ffn = 2 * depth
    t_attn = min(ATTN_TILE, T)

    ffn_local = jnp.stack([
        jnp.stack([jnp.swapaxes(wg[l], 0, 1), jnp.swapaxes(wu[l], 0, 1), wd[l]])
        for l in range(depth)
        for wg, wu, wd in ((ffn_pre_wg, ffn_pre_wu, ffn_pre_wd), (ffn_post_wg, ffn_post_wu, ffn_post_wd))
    ]).astype(BF16)
    misc_local = _pack_misc(w_o, w_dq, w_uq.reshape(n_b, w_uq.shape[1], -1), w_dkv, pool_w, pool_scale)
    misc_shapes = (w_o.shape, w_dq.shape, (n_b, w_uq.shape[1], N_HEADS * D_QK), w_dkv.shape, pool_w.shape,
                   pool_scale.shape)
    ffn_all, misc_all = _all_gather(ffn_local.reshape(n_ffn * 3, fs, D), misc_local)
    wall = ffn_all.reshape(n_ffn, 3, F, D)
    o_blk, dq_blk, uq_blk, dkv_blk, pw_blk, ps_blk = _unpack_misc(misc_all, misc_shapes)
    w_o_f = _cat_dev(o_blk, 1)
    w_dq_f = _cat_dev(dq_blk, 1)
    w_uq_f = _cat_dev(uq_blk, 1).reshape(n_b, -1, N_HEADS, D_QK)
    w_dkv_f = _cat_dev(dkv_blk, 0)
    pool_w_f = _cat_dev(pw_blk, 2)
    pool_scale_f = _cat_dev(ps_blk, 1)
    rq = w_dq_f.shape[2]
    wqa, wqb = _rope_weight_pair(w_uq_f)
    wqa = wqa.reshape(n_b, rq, N_HEADS * D_HEAD_PAD)
    wqb = wqb.reshape(n_b, rq, N_HEADS * D_HEAD_PAD)
    wka, wkb = _rope_weight_pair(w_dkv_f)
    wuk = jnp.concatenate([w_uk, jnp.zeros_like(w_uk)], axis=-1).astype(BF16).reshape(D_NOPE, N_HEADS * D_HEAD_PAD)
    wuv = w_uv.astype(BF16).reshape(D_NOPE, N_HEADS * D_V)
    ca_q, ca_k, sb = _rope_tables(T)

    def vec(a):
        return a.reshape(1, -1)

    h = x.reshape(T, D)
    saved = []
    k_all = v_all = craw = h_kv = None
    for l in range(depth):
        s = {"h0": h}
        h, s["g1"], s["u1"] = _ffn_fwd(h, vec(ffn_pre_norm[l]), wall, 2 * l)
        s["h1"] = h
        if l < n_a:
            h, s["y"] = _pool_fwd(h, vec(mix_norm[l]), pool_w_f[l], vec(pool_scale_f[l]))
        else:
            j = l - n_a
            s["cq"] = _mm_rows(h, w_dq_f[j], nt=False, out_dtype=F32, norm_g=vec(mix_norm[l]), name="q_down")
            s["q"] = _q_proj(s["cq"], vec(q_lora_norm[j]), wqa[j], wqb[j], ca_q, sb)
            s["o"], s["lse"] = _flash_fwd(s["q"], k_all, v_all)
            h = _mm_rows(s["o"], w_o_f[j], nt=False, out_dtype=F32, res=h, name="attn_out")
        s["h2"] = h
        h, s["g2"], s["u2"] = _ffn_fwd(h, vec(ffn_post_norm[l]), wall, 2 * l + 1)
        if l == n_a - 1:
            h_kv = h
            k_all, v_all, craw = _kv_proj(h, vec(kv_in_norm), wka, wkb, vec(ckv_norm), wuk, wuv, ca_k, sb)
        saved.append(s)

    dh, loss_part, d_final = _loss_head(h, vec(final_norm), loss_target.reshape(T, D))

    gbuf = jnp.zeros((n_ffn, 3, F, D), BF16)
    grads = {}
    d_pre, d_post, d_mix = [None] * depth, [None] * depth, [None] * depth
    d_pool_w, d_pool_scale = [None] * n_a, [None] * n_a
    d_qln, d_wdq, d_wuq, d_wo = [None] * n_b, [None] * n_b, [None] * n_b, [None] * n_b
    dk_acc = jnp.zeros((T, N_HEADS * D_HEAD_PAD), F32)
    dv_acc = jnp.zeros((T, N_HEADS * D_V), F32)
    for l in reversed(range(depth)):
        s = saved[l]
        if l == n_a - 1:
            (dh, grads["kv_in_norm"], dwka, dwkb, grads["ckv_norm"], dwuk, dwuv) = _kv_proj_bwd(
                dk_acc, dv_acc, dh, h_kv, vec(kv_in_norm), wka, wkb, craw, vec(ckv_norm), wuk, wuv, ca_k, sb)
            grads["w_dkv"] = _rope_weight_pair_grad(dwka, dwkb)
            grads["w_uk"] = dwuk.reshape(D_NOPE, N_HEADS, D_HEAD_PAD)[..., :D_NOPE]
            grads["w_uv"] = dwuv.reshape(D_NOPE, N_HEADS, D_V)
        dh, dgt, dup, u_b, dy_b, d_post[l] = _ffn_bwd_dx(dh, s["h2"], vec(ffn_post_norm[l]), s["g2"], s["u2"],
                                                       wall, 2 * l + 1)
        gbuf = _ffn_bwd_dw(dgt, dup, s["g2"], s["u2"], u_b, dy_b, gbuf, 2 * l + 1)
        if l < n_a:
            dh, d_mix[l], d_pool_w[l], d_pool_scale[l] = _pool_bwd(
                dh, s["h1"], vec(mix_norm[l]), s["y"], pool_w_f[l], vec(pool_scale_f[l]))
        else:
            j = l - n_a
            d_wo[j] = _mm_tn(s["o"], dh, name="attn_out_dw")
            do = _mm_rows(dh, w_o_f[j], nt=True, out_dtype=BF16, name="attn_out_dx")
            delta_row = _attn_delta(s["o"], do)
            dq_t, dk_acc, dv_acc = _flash_bwd(s["q"], k_all, v_all, do, s["lse"], delta_row, dk_acc, dv_acc)
            da, db, cqn, dcq, d_qln[j] = _q_proj_bwd(dq_t, s["cq"], vec(q_lora_norm[j]), wqa[j], wqb[j], ca_q, sb)
            dwa = _mm_tn(cqn, da, name="q_up_dw")
            dwb = _mm_tn(cqn, db, name="q_up_dw")
            d_wuq[j] = _rope_weight_pair_grad(dwa.reshape(rq, N_HEADS, D_HEAD_PAD),
                                              dwb.reshape(rq, N_HEADS, D_HEAD_PAD))
            d_wdq[j] = _mm_tn(s["h1"], dcq, norm_g=vec(mix_norm[l]), name="q_down_dw")
            dh, d_mix[l] = _proj_bwd(dcq, w_dq_f[j], s["h1"], vec(mix_norm[l]), dh, "q_down_dx")
        dh, dgt, dup, u_b, dy_b, d_pre[l] = _ffn_bwd_dx(dh, s["h0"], vec(ffn_pre_norm[l]), s["g1"], s["u1"],
                                                      wall, 2 * l)
        gbuf = _ffn_bwd_dw(dgt, dup, s["g1"], s["u1"], u_b, dy_b, gbuf, 2 * l)
    grad_x = dh.reshape(x.shape)

    rep_names = ["ffn_pre_norm", "mix_norm", "ffn_post_norm", "kv_in_norm", "ckv_norm", "q_lora_norm",
                 "final_norm", "w_uk", "w_uv"]
    grads["ffn_pre_norm"] = jnp.concatenate(d_pre, axis=0)
    grads["mix_norm"] = jnp.concatenate(d_mix, axis=0)
    grads["ffn_post_norm"] = jnp.concatenate(d_post, axis=0)
    grads["q_lora_norm"] = jnp.concatenate(d_qln, axis=0)
    grads["final_norm"] = d_final
    rep_flat = jnp.concatenate([grads[n].reshape(-1) for n in rep_names] + [loss_part[0, :1]])
    n_rep = rep_flat.shape[0]
    rep_rows = -(-n_rep // (8 * REP_COLS)) * 8
    rep_g = jnp.concatenate([rep_flat, jnp.zeros((rep_rows * REP_COLS - n_rep,), F32)]).reshape(rep_rows, REP_COLS)
    misc_g = _pack_misc(_split_dev(jnp.stack(d_wo), 1), _split_dev(jnp.stack(d_wdq), 1),
                        _split_dev(jnp.stack(d_wuq).reshape(n_b, rq, -1), 1), _split_dev(grads["w_dkv"], 0),
                        _split_dev(jnp.stack(d_pool_w), 2),
                        _split_dev(jnp.concatenate(d_pool_scale, axis=0), 1))
    ffn_parts, misc_parts, rep_parts = _grad_exchange(gbuf.reshape(n_ffn * 3, N_DEV, fs, D), misc_g, rep_g)
    ffn_sum = _sum_parts(ffn_parts.reshape(N_DEV, n_ffn * 3 * fs, D), fs, "sum_ffn").reshape(n_ffn, 3, fs, D)
    misc_sum_parts = _unpack_misc(misc_parts, misc_shapes)
    rep_sum = _sum_parts(rep_parts, _row_tile(rep_rows, 128), "sum_rep").reshape(-1)

    def sum_small(p):
        shp = p.shape[1:]
        two_d = (int(np.prod(shp[:-1])), shp[-1])
        return _sum_parts(p.reshape((N_DEV,) + two_d), two_d[0], "sum_misc").reshape(shp)

    g_wo, g_wdq, g_wuq, g_wdkv, g_pw, g_ps = [sum_small(p) for p in misc_sum_parts]
    grads.update(w_o=g_wo, w_dq=g_wdq, w_uq=g_wuq.reshape(w_uq.shape), w_dkv=g_wdkv, pool_w=g_pw, pool_scale=g_ps)
    for kind, (npre, npost) in enumerate((("ffn_pre_wg", "ffn_post_wg"), ("ffn_pre_wu", "ffn_post_wu"),
                                          ("ffn_pre_wd", "ffn_post_wd"))):
        pre = ffn_sum[0::2, kind]
        post = ffn_sum[1::2, kind]
        if kind < 2:
            pre, post = jnp.swapaxes(pre, 1, 2), jnp.swapaxes(post, 1, 2)
        grads[npre], grads[npost] = pre, post
    off = 0
    for n in rep_names:
        size = int(np.prod(weights[n].shape))
        grads[n] = rep_sum[off:off + size].reshape(weights[n].shape)
        off += size
    loss = rep_sum[off]

    deltas, new_m, new_v = {}, {}, {}
    for n in order:
        deltas[n], new_m[n], new_v[n] = _adamw_nd(weights[n], grads[n], moments_m[n], moments_v[n])
    return (loss, grad_x, *[grads[n] for n in order], *[deltas[n] for n in order],
            *[new_m[n] for n in order], *[new_v[n] for n in order])
```

```python
import functools

import numpy as np
import jax
import jax.numpy as jnp
from jax import lax
from jax.experimental import pallas as pl
from jax.experimental.pallas import tpu as pltpu

F32, BF16 = jnp.float32, jnp.bfloat16
N_DEV = 8
RMS_EPS = 1e-6
N_HEADS = 16
D_NOPE, D_ROPE, D_V = 128, 64, 128
D_QK = D_NOPE + D_ROPE
D_HEAD_PAD = 256
HEAD_GROUP = 4
ROPE_THETA = 10000.0
POOL_WINDOWS = (2, 4, 8, 16)
POOL_HALO = 16
ADAM_LR, ADAM_B1, ADAM_B2, ADAM_EPS, ADAM_WD, ADAM_STEP = 0.001, 0.9, 0.999, 1e-08, 0.01, 10
NEG_BIG = -1e30
V7X_VMEM_LIMIT = 56 * 1024 * 1024
TOKEN_TILE = 512
ATTN_TILE = 512
FFN_TILE = 256
MISC_ROWS = 864
REP_COLS = 1024


def _cparams(*sem):
    return pltpu.CompilerParams(dimension_semantics=sem, vmem_limit_bytes=V7X_VMEM_LIMIT)


def _dot(a, b):
    return lax.dot_general(a, b, (((1,), (0,)), ((), ())), preferred_element_type=F32)


def _dot_nt(a, b):
    return lax.dot_general(a, b, (((1,), (1,)), ((), ())), preferred_element_type=F32)


def _dot_tn(a, b):
    return lax.dot_general(a, b, (((0,), (0,)), ((), ())), preferred_element_type=F32)


def _rms_fwd(x, g):
    r = lax.rsqrt(jnp.mean(x * x, axis=-1, keepdims=True) + RMS_EPS)
    return (x * r) * g


def _rms_bwd(du, x, g):
    r = lax.rsqrt(jnp.mean(x * x, axis=-1, keepdims=True) + RMS_EPS)
    xh = x * r
    dg = jnp.sum(du * xh, axis=0, keepdims=True)
    dxh = du * g
    dx = r * (dxh - xh * jnp.mean(dxh * xh, axis=-1, keepdims=True))
    return dx, dg


def _sigmoid(x):
    return 1.0 / (1.0 + jnp.exp(-x))


def _split_bf16(x):
    hi = x.astype(BF16)
    lo = (x - hi.astype(F32)).astype(BF16)
    return hi, lo


def _full(shape):
    return pl.BlockSpec(shape, lambda *_: (0,) * len(shape))


def _row_tile(rows, cap):
    for t in range(min(cap, rows) // 8 * 8, 0, -8):
        if rows % t == 0:
            return t
    return rows


def _peers():
    x, y, c = lax.axis_index("x"), lax.axis_index("y"), lax.axis_index("c")
    out = []
    for k in range(1, N_DEV):
        px = 1 - x if (k >> 2) & 1 else x
        py = 1 - y if (k >> 1) & 1 else y
        pc = 1 - c if k & 1 else c
        out.append(((px, py, pc), 4 * px + 2 * py + pc))
    return 4 * x + 2 * y + c, out


def _exchange(arrays, src_of, dst_of, out_shapes, name):
    n = len(arrays)

    def body(*refs):
        ins, outs, sems = refs[:n], refs[n:2 * n], refs[2 * n:]
        _exchange_start(ins, outs, sems, src_of, dst_of)
        _exchange_wait(ins, outs, sems, src_of, dst_of)

    any_spec = pl.BlockSpec(memory_space=pl.ANY)
    return pl.pallas_call(
        body, name=name,
        out_shape=[jax.ShapeDtypeStruct(s, a.dtype) for s, a in zip(out_shapes, arrays)],
        in_specs=[any_spec] * n, out_specs=[any_spec] * n,
        scratch_shapes=_exchange_sems(n),
    )(*arrays)


def _exchange_sems(n):
    return [pltpu.SemaphoreType.DMA((n, N_DEV - 1)), pltpu.SemaphoreType.DMA((n, N_DEV - 1)),
            pltpu.SemaphoreType.DMA((n,))]


def _exchange_copies(ins, outs, sems, src_of, dst_of, receiving):
    send_sems, recv_sems, loc_sems = sems
    me, peers = _peers()
    own = [pltpu.make_async_copy(src_of(j, ins[j], me), dst_of(j, outs[j], me), loc_sems.at[j])
           for j in range(len(ins))]
    remote = [pltpu.make_async_remote_copy(
        src_ref=src_of(j, ins[j], pidx), dst_ref=dst_of(j, outs[j], pidx if receiving else me),
        send_sem=send_sems.at[j, k], recv_sem=recv_sems.at[j, k],
        device_id=peer, device_id_type=pl.DeviceIdType.MESH)
        for k, (peer, pidx) in enumerate(peers) for j in range(len(ins))]
    return own, remote


def _exchange_start(ins, outs, sems, src_of, dst_of):
    own, sends = _exchange_copies(ins, outs, sems, src_of, dst_of, receiving=False)
    for cp in own + sends:
        cp.start()


def _exchange_wait(ins, outs, sems, src_of, dst_of):
    own, sends = _exchange_copies(ins, outs, sems, src_of, dst_of, receiving=False)
    _, arrivals = _exchange_copies(ins, outs, sems, src_of, dst_of, receiving=True)
    for cp in arrivals:
        cp.wait_recv()
    for cp in sends:
        cp.wait_send()
    for cp in own:
        cp.wait()


def _all_gather(ffn_local, misc_local):
    w, r, d = ffn_local.shape

    def src_of(j, ref, idx):
        return ref

    def dst_of(j, ref, idx):
        return ref.at[:, idx] if j == 0 else ref.at[idx]

    return _exchange([ffn_local, misc_local], src_of, dst_of,
                     [(w, N_DEV, r, d), (N_DEV,) + misc_local.shape], "comm_all_gather")


def _grad_exchange(ffn_g, misc_g, rep_g):
    w, _, r, d = ffn_g.shape

    def src_of(j, ref, idx):
        return (ref.at[:, idx], ref.at[idx], ref)[j]

    def dst_of(j, ref, idx):
        return ref.at[idx]

    return _exchange([ffn_g, misc_g, rep_g], src_of, dst_of,
                     [(N_DEV, w, r, d), misc_g.shape, (N_DEV,) + rep_g.shape], "comm_grad_exchange")


def _carried(carry):
    if carry is None:
        return [], [], [], []
    arr, out_shape = carry
    return ([arr], [pl.BlockSpec(memory_space=pl.ANY)], [jax.ShapeDtypeStruct(out_shape, arr.dtype)],
            _exchange_sems(1))


def _gather_fns():
    return (lambda j, ref, idx: ref), (lambda j, ref, idx: ref.at[:, idx])


def _scatter_fns():
    return (lambda j, ref, idx: ref.at[:, idx]), (lambda j, ref, idx: ref.at[idx])


def _ffn_fwd(h, g, w3, next_local=None):
    T, D = h.shape
    F = w3.shape[1]
    tm, tf = min(TOKEN_TILE, T), F // 2
    nf = F // tf
    nt = T // tm
    has_c = next_local is not None
    c_in, c_specs, c_out, c_sems = _carried(
        (next_local, next_local.shape[:1] + (N_DEV,) + next_local.shape[1:]) if has_c else None)

    def body(*refs):
        h_ref, g_ref, wg_ref, wu_ref, wd_ref = refs[:5]
        ho_ref, gate_ref, up_ref = refs[5 + has_c:8 + has_c]
        u_sc, acc_sc = refs[8 + 2 * has_c:10 + 2 * has_c]
        i, f = pl.program_id(0), pl.program_id(1)
        if has_c:
            comm = ([refs[5]], [refs[8 + has_c]], refs[10 + 2 * has_c:], *_gather_fns())

            @pl.when((i == 0) & (f == 0))
            def _():
                _exchange_start(*comm)

        @pl.when(f == 0)
        def _():
            u_sc[...] = _rms_fwd(h_ref[...], g_ref[...]).astype(BF16)
            acc_sc[...] = jnp.zeros_like(acc_sc)

        u = u_sc[...]
        gate = _dot_nt(u, wg_ref[...])
        up = _dot_nt(u, wu_ref[...])
        gate_ref[...] = gate.astype(BF16)
        up_ref[...] = up.astype(BF16)
        act = (gate * _sigmoid(gate) * up).astype(BF16)
        acc_sc[...] += _dot(act, wd_ref[...])

        @pl.when(f == nf - 1)
        def _():
            ho_ref[...] = h_ref[...] + 0.5 * acc_sc[...]

        if has_c:
            @pl.when((i == nt - 1) & (f == nf - 1))
            def _():
                _exchange_wait(*comm)

    def wspec(kind):
        return pl.BlockSpec((None, tf, D), lambda i, f: (kind, f, 0))

    return pl.pallas_call(
        body, name="ffn_fwd_gather" if has_c else "ffn_fwd", grid=(nt, nf),
        in_specs=[pl.BlockSpec((tm, D), lambda i, f: (i, 0)), _full((1, D)), wspec(0), wspec(1), wspec(2)] + c_specs,
        out_specs=[pl.BlockSpec((tm, D), lambda i, f: (i, 0)),
                   pl.BlockSpec((tm, tf), lambda i, f: (i, f)), pl.BlockSpec((tm, tf), lambda i, f: (i, f))] + c_specs,
        out_shape=[jax.ShapeDtypeStruct((T, D), F32), jax.ShapeDtypeStruct((T, F), BF16),
                   jax.ShapeDtypeStruct((T, F), BF16)] + c_out,
        scratch_shapes=[pltpu.VMEM((tm, D), BF16), pltpu.VMEM((tm, D), F32)] + c_sems,
        compiler_params=_cparams("arbitrary", "arbitrary"),
    )(h, g, w3, w3, w3, *c_in)


def _ffn_bwd_dx(dho, h, g, gate, up, w3, grad_slab=None):
    T, D = h.shape
    F = w3.shape[1]
    tm, tf = min(TOKEN_TILE, T), F // 2
    nf = F // tf
    nt = T // tm
    has_c = grad_slab is not None
    c_in, c_specs, c_out, c_sems = _carried(
        (grad_slab, (N_DEV, grad_slab.shape[0]) + grad_slab.shape[2:]) if has_c else None)

    def body(*refs):
        dho_ref, h_ref, g_ref, gate_ref, up_ref, wg_ref, wu_ref, wd_ref = refs[:8]
        dhi_ref, dgate_ref, dup_ref, u_ref, dy_ref, dg_ref = refs[8 + has_c:14 + has_c]
        dy_sc, acc_sc = refs[14 + 2 * has_c:16 + 2 * has_c]
        i, f = pl.program_id(0), pl.program_id(1)
        if has_c:
            comm = ([refs[8]], [refs[14 + has_c]], refs[16 + 2 * has_c:], *_scatter_fns())

            @pl.when((i == 0) & (f == 0))
            def _():
                _exchange_start(*comm)

        @pl.when(f == 0)
        def _():
            dy = (0.5 * dho_ref[...]).astype(BF16)
            dy_sc[...] = dy
            dy_ref[...] = dy
            u_ref[...] = _rms_fwd(h_ref[...], g_ref[...]).astype(BF16)
            acc_sc[...] = jnp.zeros_like(acc_sc)

        dact = _dot_nt(dy_sc[...], wd_ref[...])
        gt = gate_ref[...].astype(F32)
        sig = _sigmoid(gt)
        dup = (dact * (gt * sig)).astype(BF16)
        dgate = (dact * up_ref[...].astype(F32) * (sig * (1.0 + gt * (1.0 - sig)))).astype(BF16)
        dup_ref[...] = dup
        dgate_ref[...] = dgate
        acc_sc[...] += _dot(dgate, wg_ref[...]) + _dot(dup, wu_ref[...])

        @pl.when(f == nf - 1)
        def _():
            dx, dg = _rms_bwd(acc_sc[...], h_ref[...], g_ref[...])
            dhi_ref[...] = dho_ref[...] + dx

            @pl.when(i == 0)
            def _():
                dg_ref[...] = jnp.zeros_like(dg_ref)

            dg_ref[...] += dg

        if has_c:
            @pl.when((i == nt - 1) & (f == nf - 1))
            def _():
                _exchange_wait(*comm)

    def wspec(kind):
        return pl.BlockSpec((None, tf, D), lambda i, f: (kind, f, 0))

    row = pl.BlockSpec((tm, D), lambda i, f: (i, 0))
    blk = pl.BlockSpec((tm, tf), lambda i, f: (i, f))
    return pl.pallas_call(
        body, name="ffn_bwd_dx_scatter" if has_c else "ffn_bwd_dx", grid=(nt, nf),
        in_specs=[row, row, _full((1, D)), blk, blk, wspec(0), wspec(1), wspec(2)] + c_specs,
        out_specs=[row, blk, blk, row, row, _full((1, D))] + c_specs,
        out_shape=[jax.ShapeDtypeStruct((T, D), F32), jax.ShapeDtypeStruct((T, F), BF16),
                   jax.ShapeDtypeStruct((T, F), BF16), jax.ShapeDtypeStruct((T, D), BF16),
                   jax.ShapeDtypeStruct((T, D), BF16), jax.ShapeDtypeStruct((1, D), F32)] + c_out,
        scratch_shapes=[pltpu.VMEM((tm, D), BF16), pltpu.VMEM((tm, D), F32)] + c_sems,
        compiler_params=_cparams("arbitrary", "arbitrary"),
    )(dho, h, g, gate, up, w3, w3, w3, *c_in)


def _ffn_bwd_dw(dgate, dup, gate, up, u, dy):
    T, F = gate.shape
    D = u.shape[1]
    tfw = F // 2
    tk = min(TOKEN_TILE, T)
    nk = T // tk

    def body(dgate_ref, dup_ref, gate_ref, up_ref, u_ref, dy_ref, out_ref, acc_sc):
        k = pl.program_id(1)

        @pl.when(k == 0)
        def _():
            acc_sc[...] = jnp.zeros_like(acc_sc)

        uu = u_ref[...]
        gt = gate_ref[...].astype(F32)
        act = (gt * _sigmoid(gt) * up_ref[...].astype(F32)).astype(BF16)
        acc_sc[0] += _dot_tn(dgate_ref[...], uu)
        acc_sc[1] += _dot_tn(dup_ref[...], uu)
        acc_sc[2] += _dot_tn(act, dy_ref[...])

        @pl.when(k == nk - 1)
        def _():
            out_ref[...] = acc_sc[...].astype(BF16)

    blk = pl.BlockSpec((tk, tfw), lambda j, k: (k, j))
    row = pl.BlockSpec((tk, D), lambda j, k: (k, 0))
    return pl.pallas_call(
        body, name="ffn_bwd_dw", grid=(F // tfw, nk),
        in_specs=[blk, blk, blk, blk, row, row],
        out_specs=pl.BlockSpec((3, tfw, D), lambda j, k: (0, j, 0)),
        out_shape=jax.ShapeDtypeStruct((3, F, D), BF16),
        scratch_shapes=[pltpu.VMEM((3, tfw, D), F32)],
        compiler_params=_cparams("parallel", "arbitrary"),
    )(dgate, dup, gate, up, u, dy)


def _mm_rows(a, b, *, nt, out_dtype, norm_g=None, res=None, name):
    T, K = a.shape
    N = b.shape[0] if nt else b.shape[1]
    tm = min(TOKEN_TILE, T)
    has_g, has_r = norm_g is not None, res is not None

    def body(*refs):
        a_ref, b_ref = refs[0], refs[1]
        o_ref = refs[-1]
        x = a_ref[...]
        if has_g:
            x = _rms_fwd(x, refs[2][...])
        x = x.astype(BF16)
        acc = _dot_nt(x, b_ref[...]) if nt else _dot(x, b_ref[...])
        if has_r:
            acc = refs[2 + has_g][...] + acc
        o_ref[...] = acc.astype(out_dtype)

    ins, specs = [a, b], [pl.BlockSpec((tm, K), lambda i: (i, 0)), _full(b.shape)]
    if has_g:
        ins.append(norm_g)
        specs.append(_full((1, K)))
    if has_r:
        ins.append(res)
        specs.append(pl.BlockSpec((tm, N), lambda i: (i, 0)))
    return pl.pallas_call(
        body, name=name, grid=(T // tm,), in_specs=specs,
        out_specs=pl.BlockSpec((tm, N), lambda i: (i, 0)),
        out_shape=jax.ShapeDtypeStruct((T, N), out_dtype),
        compiler_params=_cparams("parallel"),
    )(*ins)


def _mm_tn(a, b, *, norm_g=None, name):
    T, M = a.shape
    N = b.shape[1]
    tk = min(TOKEN_TILE, T)
    has_g = norm_g is not None

    def body(*refs):
        a_ref, b_ref, o_ref = refs[0], refs[1], refs[-1]

        @pl.when(pl.program_id(0) == 0)
        def _():
            o_ref[...] = jnp.zeros_like(o_ref)

        x = a_ref[...]
        if has_g:
            x = _rms_fwd(x, refs[2][...])
        o_ref[...] += _dot_tn(x.astype(BF16), b_ref[...].astype(BF16))

    ins = [a, b]
    specs = [pl.BlockSpec((tk, M), lambda k: (k, 0)), pl.BlockSpec((tk, N), lambda k: (k, 0))]
    if has_g:
        ins.append(norm_g)
        specs.append(_full((1, M)))
    return pl.pallas_call(
        body, name=name, grid=(T // tk,), in_specs=specs, out_specs=_full((M, N)),
        out_shape=jax.ShapeDtypeStruct((M, N), F32),
        compiler_params=_cparams("arbitrary"),
    )(*ins)


def _proj_bwd(dz, w, h, g, dh, name):
    T, D = h.shape
    N = w.shape[1]
    tm = min(TOKEN_TILE, T)

    def body(dz_ref, w_ref, h_ref, g_ref, dh_ref, o_ref, dg_ref):
        du = _dot_nt(dz_ref[...].astype(BF16), w_ref[...])
        dx, dg = _rms_bwd(du, h_ref[...], g_ref[...])
        o_ref[...] = dh_ref[...] + dx

        @pl.when(pl.program_id(0) == 0)
        def _():
            dg_ref[...] = jnp.zeros_like(dg_ref)

        dg_ref[...] += dg

    row = pl.BlockSpec((tm, D), lambda i: (i, 0))
    return pl.pallas_call(
        body, name=name, grid=(T // tm,),
        in_specs=[pl.BlockSpec((tm, N), lambda i: (i, 0)), _full((D, N)), row, _full((1, D)), row],
        out_specs=[row, _full((1, D))],
        out_shape=[jax.ShapeDtypeStruct((T, D), F32), jax.ShapeDtypeStruct((1, D), F32)],
        compiler_params=_cparams("arbitrary"),
    )(dz, w, h, g, dh)


def _pool_bands(tm):
    r = np.arange(tm)[:, None]
    c = np.arange(tm)[None, :]
    j = np.arange(POOL_HALO)[None, :]
    main, halo, main_t, halo_t = [], [], [], []
    for w in POOL_WINDOWS:
        main.append(((r - c >= 0) & (r - c < w)) / w)
        halo.append((r + POOL_HALO - j < w) / w)
        main_t.append(((c - r >= 0) & (c - r < w)) / w)
        halo_t.append((tm + j - r < w) / w)
    return tuple(jnp.asarray(np.stack(m), BF16) for m in (main, halo, main_t, halo_t))


def _pool_count_scale(i, tm, w):
    t = i * tm + lax.broadcasted_iota(jnp.int32, (tm, 1), 0)
    return w / jnp.minimum(t + 1, w).astype(F32)


def _pool_fwd(h, g, wp, scale):
    T, D = h.shape
    G, dg = len(POOL_WINDOWS), D // len(POOL_WINDOWS)
    tm = min(TOKEN_TILE, T)
    hb = tm // POOL_HALO
    bm, bh, _, _ = _pool_bands(tm)

    def body(h_ref, hh_ref, g_ref, wp_ref, sc_ref, bm_ref, bh_ref, ho_ref, y_ref):
        i = pl.program_id(0)
        x = h_ref[...]
        u = _rms_fwd(x, g_ref[...])
        uh = _rms_fwd(hh_ref[...], g_ref[...]) * (i > 0).astype(F32)
        for gi, w in enumerate(POOL_WINDOWS):
            cols = slice(gi * dg, (gi + 1) * dg)
            ug = u[:, cols]
            hi, lo = _split_bf16(ug)
            hhi, hlo = _split_bf16(uh[:, cols])
            s = (_dot(bm_ref[gi], hi) + _dot(bm_ref[gi], lo)
                 + _dot(bh_ref[gi], hhi) + _dot(bh_ref[gi], hlo))
            y = (s * _pool_count_scale(i, tm, w) - ug).astype(BF16)
            y_ref[:, cols] = y
            ho_ref[:, cols] = x[:, cols] + _dot(y, wp_ref[gi]) * sc_ref[:, cols]

    row = pl.BlockSpec((tm, D), lambda i: (i, 0))
    return pl.pallas_call(
        body, name="pool_fwd", grid=(T // tm,),
        in_specs=[row, pl.BlockSpec((POOL_HALO, D), lambda i: (jnp.maximum(i * hb - 1, 0), 0)),
                  _full((1, D)), _full((G, dg, dg)), _full((1, D)),
                  _full((G, tm, tm)), _full((G, tm, POOL_HALO))],
        out_specs=[row, row],
        out_shape=[jax.ShapeDtypeStruct((T, D), F32), jax.ShapeDtypeStruct((T, D), BF16)],
        compiler_params=_cparams("parallel"),
    )(h, h, g, wp, scale, bm, bh)


def _pool_bwd(dh, h, g, y, wp, scale):
    T, D = h.shape
    G, dg = len(POOL_WINDOWS), D // len(POOL_WINDOWS)
    tm = min(TOKEN_TILE, T)
    hb = tm // POOL_HALO
    nt = T // tm
    _, _, bmt, bht = _pool_bands(tm)

    def body(dh_ref, dhn_ref, h_ref, g_ref, y_ref, wp_ref, sc_ref, bmt_ref, bht_ref,
             o_ref, dg_ref, dwp_ref, dsc_ref, du_sc):
        i = pl.program_id(0)

        @pl.when(i == 0)
        def _():
            dg_ref[...] = jnp.zeros_like(dg_ref)
            dwp_ref[...] = jnp.zeros_like(dwp_ref)
            dsc_ref[...] = jnp.zeros_like(dsc_ref)

        dho = dh_ref[...]
        dz = dho * sc_ref[...]
        dzn = dhn_ref[...] * sc_ref[...] * (i < nt - 1).astype(F32)
        for gi, w in enumerate(POOL_WINDOWS):
            cols = slice(gi * dg, (gi + 1) * dg)
            yg = y_ref[:, cols]
            dzg = dz[:, cols].astype(BF16)
            dsc_ref[:, cols] += jnp.sum(dho[:, cols] * _dot(yg, wp_ref[gi]), axis=0, keepdims=True)
            dwp_ref[gi] += _dot_tn(yg, dzg)
            dy = _dot_nt(dzg, wp_ref[gi])
            dyn = _dot_nt(dzn[:, cols].astype(BF16), wp_ref[gi])
            hi, lo = _split_bf16(dy * _pool_count_scale(i, tm, w))
            nhi, nlo = _split_bf16(dyn)
            du_sc[:, cols] = (_dot(bmt_ref[gi], hi) + _dot(bmt_ref[gi], lo)
                              + _dot(bht_ref[gi], nhi) + _dot(bht_ref[gi], nlo) - dy)
        dx, dgp = _rms_bwd(du_sc[...], h_ref[...], g_ref[...])
        o_ref[...] = dho + dx
        dg_ref[...] += dgp

    row = pl.BlockSpec((tm, D), lambda i: (i, 0))
    return pl.pallas_call(
        body, name="pool_bwd", grid=(nt,),
        in_specs=[row, pl.BlockSpec((POOL_HALO, D), lambda i: (jnp.minimum((i + 1) * hb, T // POOL_HALO - 1), 0)),
                  row, _full((1, D)), row, _full((G, dg, dg)), _full((1, D)),
                  _full((G, tm, tm)), _full((G, tm, POOL_HALO))],
        out_specs=[row, _full((1, D)), _full((G, dg, dg)), _full((1, D))],
        out_shape=[jax.ShapeDtypeStruct((T, D), F32), jax.ShapeDtypeStruct((1, D), F32),
                   jax.ShapeDtypeStruct((G, dg, dg), F32), jax.ShapeDtypeStruct((1, D), F32)],
        scratch_shapes=[pltpu.VMEM((tm, D), F32)],
        compiler_params=_cparams("arbitrary"),
    )(dh, dh, h, g, y, wp, scale, bmt, bht)


def _rope_tables(T):
    pos = jnp.arange(T, dtype=F32)
    inv_freq = ROPE_THETA ** (-jnp.arange(0, D_ROPE, 2, dtype=F32) / D_ROPE)
    ang = pos[:, None] * inv_freq[None, :]
    cos2 = jnp.tile(jnp.cos(ang), (1, 2))
    sin2 = jnp.tile(jnp.sin(ang), (1, 2))
    pad = jnp.zeros((T, D_HEAD_PAD - D_QK), F32)
    ca_q = jnp.concatenate([jnp.ones((T, D_NOPE), F32), cos2, pad], axis=1)
    ca_k = jnp.concatenate([jnp.zeros((T, D_NOPE), F32), cos2, pad], axis=1)
    sb = jnp.concatenate([jnp.zeros((T, D_NOPE), F32), sin2, pad], axis=1)
    return ca_q, ca_k, sb


def _rope_weight_pair(w):
    half = D_ROPE // 2
    z_pad = jnp.zeros(w.shape[:-1] + (D_HEAD_PAD - D_QK,), w.dtype)
    z_nope = jnp.zeros(w.shape[:-1] + (D_NOPE,), w.dtype)
    wa = jnp.concatenate([w, z_pad], axis=-1)
    wb = jnp.concatenate([z_nope, -w[..., D_NOPE + half:], w[..., D_NOPE:D_NOPE + half], z_pad], axis=-1)
    return wa, wb


def _rope_weight_pair_grad(dwa, dwb):
    half = D_ROPE // 2
    d1 = dwa[..., D_NOPE:D_NOPE + half] + dwb[..., D_NOPE + half:D_QK]
    d2 = dwa[..., D_NOPE + half:D_QK] - dwb[..., D_NOPE:D_NOPE + half]
    return jnp.concatenate([dwa[..., :D_NOPE], d1, d2], axis=-1)


def _q_proj(cq, qg, wa, wb, ca, sb):
    T, R = cq.shape
    tm = min(TOKEN_TILE, T)
    P = D_HEAD_PAD
    GP = HEAD_GROUP * P

    def body(cq_ref, qg_ref, wa_ref, wb_ref, ca_ref, sb_ref, q_ref):
        c = _rms_fwd(cq_ref[...], qg_ref[...]).astype(BF16)
        ca = jnp.tile(ca_ref[...], (1, HEAD_GROUP))
        sb = jnp.tile(sb_ref[...], (1, HEAD_GROUP))
        q_ref[...] = (_dot(c, wa_ref[...]) * ca + _dot(c, wb_ref[...]) * sb).astype(BF16)

    tok = pl.BlockSpec((tm, P), lambda i, hh: (i, 0))
    wsp = pl.BlockSpec((R, GP), lambda i, hh: (0, hh))
    return pl.pallas_call(
        body, name="q_proj", grid=(T // tm, N_HEADS // HEAD_GROUP),
        in_specs=[pl.BlockSpec((tm, R), lambda i, hh: (i, 0)), _full((1, R)), wsp, wsp, tok, tok],
        out_specs=pl.BlockSpec((tm, GP), lambda i, hh: (i, hh)),
        out_shape=jax.ShapeDtypeStruct((T, N_HEADS * P), BF16),
        compiler_params=_cparams("parallel", "arbitrary"),
    )(cq, qg, wa, wb, ca, sb)


def _q_proj_bwd(dq, cq, qg, wa, wb, ca, sb):
    T, R = cq.shape
    tm = min(TOKEN_TILE, T)
    P = D_HEAD_PAD

    def body(dq_ref, cq_ref, qg_ref, wa_ref, wb_ref, ca_ref, sb_ref,
             da_ref, db_ref, cqn_ref, dcq_ref, dqg_ref, acc_sc):
        i, hh = pl.program_id(0), pl.program_id(1)

        @pl.when(hh == 0)
        def _():
            acc_sc[...] = jnp.zeros_like(acc_sc)
            cqn_ref[...] = _rms_fwd(cq_ref[...], qg_ref[...]).astype(BF16)

        d = dq_ref[...].T
        da = (d * jnp.tile(ca_ref[...], (1, HEAD_GROUP))).astype(BF16)
        db = (d * jnp.tile(sb_ref[...], (1, HEAD_GROUP))).astype(BF16)
        da_ref[...] = da
        db_ref[...] = db
        acc_sc[...] += _dot_nt(da, wa_ref[...]) + _dot_nt(db, wb_ref[...])

        @pl.when(hh == N_HEADS // HEAD_GROUP - 1)
        def _():
            dx, dg = _rms_bwd(acc_sc[...], cq_ref[...], qg_ref[...])
            dcq_ref[...] = dx

            @pl.when(i == 0)
            def _():
                dqg_ref[...] = jnp.zeros_like(dqg_ref)

            dqg_ref[...] += dg

    GP = HEAD_GROUP * P
    tok = pl.BlockSpec((tm, P), lambda i, hh: (i, 0))
    hd = pl.BlockSpec((tm, GP), lambda i, hh: (i, hh))
    wsp = pl.BlockSpec((R, GP), lambda i, hh: (0, hh))
    rr = pl.BlockSpec((tm, R), lambda i, hh: (i, 0))
    return pl.pallas_call(
        body, name="q_proj_bwd", grid=(T // tm, N_HEADS // HEAD_GROUP),
        in_specs=[pl.BlockSpec((GP, tm), lambda i, hh: (hh, i)), rr, _full((1, R)), wsp, wsp, tok, tok],
        out_specs=[hd, hd, rr, rr, _full((1, R))],
        out_shape=[jax.ShapeDtypeStruct((T, N_HEADS * P), BF16), jax.ShapeDtypeStruct((T, N_HEADS * P), BF16),
                   jax.ShapeDtypeStruct((T, R), BF16), jax.ShapeDtypeStruct((T, R), F32),
                   jax.ShapeDtypeStruct((1, R), F32)],
        scratch_shapes=[pltpu.VMEM((tm, R), F32)],
        compiler_params=_cparams("arbitrary", "arbitrary"),
    )(dq, cq, qg, wa, wb, ca, sb)


def _kv_proj(h, g_in, wka, wkb, g_c, wuk, wuv, ca, sb):
    T, D = h.shape
    tm = min(TOKEN_TILE, T)
    P, C = D_HEAD_PAD, D_NOPE

    def body(h_ref, gi_ref, wka_ref, wkb_ref, gc_ref, wuk_ref, wuv_ref, ca_ref, sb_ref, k_ref, v_ref, craw_ref):
        u = _rms_fwd(h_ref[...], gi_ref[...]).astype(BF16)
        kva = _dot(u, wka_ref[...])
        kvb = _dot(u, wkb_ref[...])
        craw = kva[:, :C]
        craw_ref[...] = craw
        c = _rms_fwd(craw, gc_ref[...]).astype(BF16)
        kr = kva * ca_ref[...] + kvb * sb_ref[...]
        kn = _dot(c, wuk_ref[...])
        for hh in range(N_HEADS):
            k_ref[:, hh * P:(hh + 1) * P] = (kn[:, hh * P:(hh + 1) * P] + kr).astype(BF16)
        v_ref[...] = _dot(c, wuv_ref[...]).astype(BF16)

    tok = pl.BlockSpec((tm, P), lambda i: (i, 0))
    return pl.pallas_call(
        body, name="kv_proj", grid=(T // tm,),
        in_specs=[pl.BlockSpec((tm, D), lambda i: (i, 0)), _full((1, D)), _full((D, P)), _full((D, P)),
                  _full((1, C)), _full(wuk.shape), _full(wuv.shape), tok, tok],
        out_specs=[pl.BlockSpec((tm, N_HEADS * P), lambda i: (i, 0)),
                   pl.BlockSpec((tm, N_HEADS * D_V), lambda i: (i, 0)), pl.BlockSpec((tm, C), lambda i: (i, 0))],
        out_shape=[jax.ShapeDtypeStruct((T, N_HEADS * P), BF16), jax.ShapeDtypeStruct((T, N_HEADS * D_V), BF16),
                   jax.ShapeDtypeStruct((T, C), F32)],
        compiler_params=_cparams("parallel"),
    )(h, g_in, wka, wkb, g_c, wuk, wuv, ca, sb)


def _kv_proj_bwd(dk, dv, dh, h, g_in, wka, wkb, craw, g_c, wuk, wuv, ca, sb):
    T, D = h.shape
    tm = min(TOKEN_TILE // 2, T)
    P, C = D_HEAD_PAD, D_NOPE

    def body(dk_ref, dv_ref, dh_ref, h_ref, gi_ref, wka_ref, wkb_ref, craw_ref, gc_ref, wuk_ref, wuv_ref,
             ca_ref, sb_ref, o_ref, dgi_ref, dwka_ref, dwkb_ref, dgc_ref, dwuk_ref, dwuv_ref):
        @pl.when(pl.program_id(0) == 0)
        def _():
            for r in (dgi_ref, dwka_ref, dwkb_ref, dgc_ref, dwuk_ref, dwuv_ref):
                r[...] = jnp.zeros_like(r)

        x = h_ref[...]
        u = _rms_fwd(x, gi_ref[...]).astype(BF16)
        craw = craw_ref[...]
        c = _rms_fwd(craw, gc_ref[...]).astype(BF16)
        dkf = dk_ref[...]
        dkb = dkf.astype(BF16)
        dvb = dv_ref[...].astype(BF16)
        dwuk_ref[...] += _dot_tn(c, dkb)
        dwuv_ref[...] += _dot_tn(c, dvb)
        dc = _dot_nt(dkb, wuk_ref[...]) + _dot_nt(dvb, wuv_ref[...])
        dkr = dkf[:, :P]
        for hh in range(1, N_HEADS):
            dkr = dkr + dkf[:, hh * P:(hh + 1) * P]
        dcraw, dgc = _rms_bwd(dc, craw, gc_ref[...])
        dgc_ref[...] += dgc
        dkva = jnp.concatenate([dcraw, (dkr * ca_ref[...])[:, C:]], axis=1).astype(BF16)
        dkvb = (dkr * sb_ref[...]).astype(BF16)
        dwka_ref[...] += _dot_tn(u, dkva)
        dwkb_ref[...] += _dot_tn(u, dkvb)
        du = _dot_nt(dkva, wka_ref[...]) + _dot_nt(dkvb, wkb_ref[...])
        dx, dgi = _rms_bwd(du, x, gi_ref[...])
        dgi_ref[...] += dgi
        o_ref[...] = dh_ref[...] + dx

    row = pl.BlockSpec((tm, D), lambda i: (i, 0))
    tok = pl.BlockSpec((tm, P), lambda i: (i, 0))
    return pl.pallas_call(
        body, name="kv_proj_bwd", grid=(T // tm,),
        in_specs=[pl.BlockSpec((tm, N_HEADS * P), lambda i: (i, 0)),
                  pl.BlockSpec((tm, N_HEADS * D_V), lambda i: (i, 0)), row, row, _full((1, D)),
                  _full((D, P)), _full((D, P)), pl.BlockSpec((tm, C), lambda i: (i, 0)), _full((1, C)),
                  _full(wuk.shape), _full(wuv.shape), tok, tok],
        out_specs=[row, _full((1, D)), _full((D, P)), _full((D, P)), _full((1, C)),
                   _full(wuk.shape), _full(wuv.shape)],
        out_shape=[jax.ShapeDtypeStruct((T, D), F32), jax.ShapeDtypeStruct((1, D), F32),
                   jax.ShapeDtypeStruct((D, P), F32), jax.ShapeDtypeStruct((D, P), F32),
                   jax.ShapeDtypeStruct((1, C), F32), jax.ShapeDtypeStruct(wuk.shape, F32),
                   jax.ShapeDtypeStruct(wuv.shape, F32)],
        compiler_params=_cparams("arbitrary"),
    )(dk, dv, dh, h, g_in, wka, wkb, craw, g_c, wuk, wuv, ca, sb)


_ATTN_SCALE = D_QK ** -0.5
_LOG2_E = 1.4426950408889634
_ATTN_SCALE_LOG2 = _ATTN_SCALE * _LOG2_E


def _causal_mask(t):
    return lax.broadcasted_iota(jnp.int32, (t, t), 1) <= lax.broadcasted_iota(jnp.int32, (t, t), 0)


def _causal_mask_t(t):
    return lax.broadcasted_iota(jnp.int32, (t, t), 0) <= lax.broadcasted_iota(jnp.int32, (t, t), 1)


def _flash_fwd(q, k, v):
    T = q.shape[0]
    t = min(ATTN_TILE, T)
    P = D_HEAD_PAD

    def body(q_ref, k_ref, v_ref, o_ref, lse_ref, m_sc, l_sc, acc_sc, s0_sc, s1_sc, p0_sc, p1_sc, a0_sc, a1_sc):
        n = pl.program_id(1) + 1
        s_sc, p_sc, a_sc = (s0_sc, s1_sc), (p0_sc, p1_sc), (a0_sc, a1_sc)
        m_sc[...] = jnp.full_like(m_sc, NEG_BIG)
        l_sc[...] = jnp.zeros_like(l_sc)
        acc_sc[...] = jnp.zeros_like(acc_sc)

        def rows_of(c):
            return pl.ds(pl.multiple_of(c * t, t), t)

        def scores(c, slot):
            s_sc[slot][...] = _dot_nt(k_ref[rows_of(c), :], q_ref[...])

        def softmax(slot, diagonal):
            s_t = s_sc[slot][...]
            if diagonal:
                s_t = jnp.where(_causal_mask_t(t), s_t, NEG_BIG)
            m_prev = m_sc[...]
            m_new = jnp.maximum(m_prev, jnp.max(s_t, axis=0, keepdims=True))
            p_t = jnp.exp2((s_t - m_new) * _ATTN_SCALE_LOG2)
            alpha = jnp.exp2((m_prev - m_new) * _ATTN_SCALE_LOG2)
            l_sc[...] = alpha * l_sc[...] + jnp.sum(p_t, axis=0, keepdims=True)
            m_sc[...] = m_new
            p_sc[slot][...] = p_t.astype(BF16)
            a_sc[slot][...] = alpha

        def values(c, slot):
            acc_sc[...] = a_sc[slot][...] * acc_sc[...] + _dot_tn(v_ref[rows_of(c), :], p_sc[slot][...])

        def stage(c, slot, first=False, last=False):
            if not first:
                values(c - 1, 1 - slot)
            if not last:
                scores(c + 1, 1 - slot)
            softmax(slot, diagonal=last)

        scores(0, 0)

        @pl.when(n == 1)
        def _():
            softmax(0, diagonal=True)
            values(0, 0)

        @pl.when(n >= 2)
        def _():
            stage(0, 0, first=True)

            def pair(j, carry):
                stage(1 + 2 * j, 1)
                stage(2 + 2 * j, 0)
                return carry

            lax.fori_loop(0, (n - 2) // 2, pair, 0)

            @pl.when((n - 2) % 2 == 1)
            def _():
                stage(n - 2, 1)

            for slot in range(2):
                @pl.when((n - 1) % 2 == slot)
                def _():
                    stage(n - 1, slot, last=True)
                    values(n - 1, slot)

        l = l_sc[...]
        o_ref[...] = (acc_sc[...] / l).T.astype(BF16)
        lse_ref[...] = m_sc[...] * _ATTN_SCALE + jnp.log(l)

    return pl.pallas_call(
        body, name="flash_fwd", grid=(N_HEADS, T // t),
        in_specs=[pl.BlockSpec((t, P), lambda hh, i: (i, hh)), pl.BlockSpec((T, P), lambda hh, i: (0, hh)),
                  pl.BlockSpec((T, D_V), lambda hh, i: (0, hh))],
        out_specs=[pl.BlockSpec((t, D_V), lambda hh, i: (i, hh)),
                   pl.BlockSpec((None, None, 1, t), lambda hh, i: (hh, i, 0, 0))],
        out_shape=[jax.ShapeDtypeStruct((T, N_HEADS * D_V), BF16),
                   jax.ShapeDtypeStruct((N_HEADS, T // t, 1, t), F32)],
        scratch_shapes=[pltpu.VMEM((1, t), F32), pltpu.VMEM((1, t), F32), pltpu.VMEM((D_V, t), F32),
                        pltpu.VMEM((t, t), F32), pltpu.VMEM((t, t), F32), pltpu.VMEM((t, t), BF16),
                        pltpu.VMEM((t, t), BF16), pltpu.VMEM((1, t), F32), pltpu.VMEM((1, t), F32)],
        compiler_params=_cparams("parallel", "arbitrary"),
    )(q, k, v)


def _attn_delta(o, do):
    T = o.shape[0]
    t = min(ATTN_TILE, T)

    def body(o_ref, do_ref, out_ref):
        ones = jnp.ones((8, D_V), BF16)
        for hh in range(N_HEADS):
            cols = slice(hh * D_V, (hh + 1) * D_V)
            hi, lo = _split_bf16(o_ref[:, cols].astype(F32) * do_ref[:, cols].astype(F32))
            out_ref[hh] = (_dot_nt(ones, hi) + _dot_nt(ones, lo))[0:1]

    tok = pl.BlockSpec((t, N_HEADS * D_V), lambda i: (i, 0))
    return pl.pallas_call(
        body, name="attn_delta", grid=(T // t,), in_specs=[tok, tok],
        out_specs=pl.BlockSpec((N_HEADS, None, 1, t), lambda i: (0, i, 0, 0)),
        out_shape=jax.ShapeDtypeStruct((N_HEADS, T // t, 1, t), F32),
        compiler_params=_cparams("parallel"),
    )(o, do)


def _flash_bwd(q, k, v, do, lse_row, delta_row, dk_prev, dv_prev):
    T = q.shape[0]
    t = min(ATTN_TILE, T)
    nq = T // t
    P = D_HEAD_PAD

    def body(k_ref, v_ref, q_ref, do_ref, lse_ref, delta_ref, dkp_ref, dvp_ref,
             dqt_ref, dk_ref, dv_ref, dk_sc, dv_sc, dqt_sc, kt_sc,
             s0_sc, s1_sc, dp0_sc, dp1_sc, p0_sc, p1_sc, ds0_sc, ds1_sc):
        ki = pl.program_id(1)
        n = nq - ki
        s_sc, dp_sc, p_sc, ds_sc = (s0_sc, s1_sc), (dp0_sc, dp1_sc), (p0_sc, p1_sc), (ds0_sc, ds1_sc)
        kt_sc[...] = k_ref[...].astype(F32).T.astype(BF16)
        dk_sc[...] = dkp_ref[...]
        dv_sc[...] = dvp_ref[...]

        @pl.when(ki == 0)
        def _():
            dqt_sc[...] = jnp.zeros_like(dqt_sc)

        def rows_of(c):
            return pl.ds(pl.multiple_of((ki + c) * t, t), t)

        def products(c, slot):
            s_sc[slot][...] = _dot_nt(k_ref[...], q_ref[rows_of(c), :])
            dp_sc[slot][...] = _dot_nt(v_ref[...], do_ref[rows_of(c), :])

        def elementwise(c, slot, diagonal):
            p_t = jnp.exp2(s_sc[slot][...] * _ATTN_SCALE_LOG2 - lse_ref[ki + c] * _LOG2_E)
            if diagonal:
                p_t = jnp.where(_causal_mask_t(t), p_t, 0.0)
            p_sc[slot][...] = p_t.astype(BF16)
            ds_sc[slot][...] = (p_t * (dp_sc[slot][...] - delta_ref[ki + c]) * _ATTN_SCALE).astype(BF16)

        def gradients(c, slot):
            dv_sc[...] += _dot(p_sc[slot][...], do_ref[rows_of(c), :])
            ds_t = ds_sc[slot][...]
            dk_sc[...] += _dot(ds_t, q_ref[rows_of(c), :])
            dqt_sc[ki + c] += _dot(kt_sc[...], ds_t)

        def stage(c, slot, first=False, last=False):
            if not first:
                gradients(c - 1, 1 - slot)
            if not last:
                products(c + 1, 1 - slot)
            elementwise(c, slot, diagonal=first)

        products(0, 0)

        @pl.when(n == 1)
        def _():
            elementwise(0, 0, diagonal=True)
            gradients(0, 0)

        @pl.when(n >= 2)
        def _():
            stage(0, 0, first=True)

            def pair(j, carry):
                stage(1 + 2 * j, 1)
                stage(2 + 2 * j, 0)
                return carry

            lax.fori_loop(0, (n - 2) // 2, pair, 0)

            @pl.when((n - 2) % 2 == 1)
            def _():
                stage(n - 2, 1)

            for slot in range(2):
                @pl.when((n - 1) % 2 == slot)
                def _():
                    stage(n - 1, slot, last=True)
                    gradients(n - 1, slot)

        dqt_ref[...] = dqt_sc[ki]
        dk_ref[...] = dk_sc[...]
        dv_ref[...] = dv_sc[...]

    kb = pl.BlockSpec((t, P), lambda hh, i: (i, hh))
    vb = pl.BlockSpec((t, D_V), lambda hh, i: (i, hh))
    stat = pl.BlockSpec((None, nq, 1, t), lambda hh, i: (hh, 0, 0, 0))
    return pl.pallas_call(
        body, name="flash_bwd", grid=(N_HEADS, nq),
        in_specs=[kb, vb,
                  pl.BlockSpec((T, P), lambda hh, i: (0, hh), pipeline_mode=pl.Buffered(1)),
                  pl.BlockSpec((T, D_V), lambda hh, i: (0, hh), pipeline_mode=pl.Buffered(1)),
                  stat, stat, kb, vb],
        out_specs=[pl.BlockSpec((P, t), lambda hh, i: (hh, i)), kb, vb],
        out_shape=[jax.ShapeDtypeStruct((N_HEADS * P, T), F32), jax.ShapeDtypeStruct((T, N_HEADS * P), F32),
                   jax.ShapeDtypeStruct((T, N_HEADS * D_V), F32)],
        scratch_shapes=[pltpu.VMEM((t, P), F32), pltpu.VMEM((t, D_V), F32), pltpu.VMEM((nq, P, t), F32),
                        pltpu.VMEM((P, t), BF16)] + [pltpu.VMEM((t, t), F32)] * 4 + [pltpu.VMEM((t, t), BF16)] * 4,
        compiler_params=_cparams("arbitrary", "arbitrary"),
    )(k, v, q, do, lse_row, delta_row, dk_prev, dv_prev)


def _loss_head(h, g, target):
    T, D = h.shape
    tm = min(TOKEN_TILE, T)

    def body(h_ref, g_ref, t_ref, dh_ref, loss_ref, dg_ref):
        @pl.when(pl.program_id(0) == 0)
        def _():
            loss_ref[...] = jnp.zeros_like(loss_ref)
            dg_ref[...] = jnp.zeros_like(dg_ref)

        x = h_ref[...]
        err = _rms_fwd(x, g_ref[...]) - t_ref[...]
        per_tok = jnp.mean(err * err, axis=-1, keepdims=True)
        loss_ref[...] += 0.5 * jnp.sum(per_tok, axis=0, keepdims=True)
        dx, dg = _rms_bwd(err * (1.0 / D), x, g_ref[...])
        dh_ref[...] = dx
        dg_ref[...] += dg

    row = pl.BlockSpec((tm, D), lambda i: (i, 0))
    return pl.pallas_call(
        body, name="loss_head", grid=(T // tm,),
        in_specs=[row, _full((1, D)), row], out_specs=[row, _full((1, 128)), _full((1, D))],
        out_shape=[jax.ShapeDtypeStruct((T, D), F32), jax.ShapeDtypeStruct((1, 128), F32),
                   jax.ShapeDtypeStruct((1, D), F32)],
        compiler_params=_cparams("arbitrary"),
    )(h, g, target)


def _sum_parts(parts, tr, name):
    _, R, C = parts.shape

    def body(p_ref, o_ref):
        acc = p_ref[0].astype(F32)
        for j in range(1, N_DEV):
            acc = acc + p_ref[j].astype(F32)
        o_ref[...] = acc

    return pl.pallas_call(
        body, name=name, grid=(R // tr,),
        in_specs=[pl.BlockSpec((N_DEV, tr, C), lambda i: (0, i, 0))],
        out_specs=pl.BlockSpec((tr, C), lambda i: (i, 0)),
        out_shape=jax.ShapeDtypeStruct((R, C), F32),
        compiler_params=_cparams("parallel"),
    )(parts)


def _adamw(w, g, m, v):
    R, C = w.shape
    tr = _row_tile(R, TOKEN_TILE)

    def body(w_ref, g_ref, m_ref, v_ref, d_ref, mo_ref, vo_ref):
        gg = g_ref[...]
        mn = ADAM_B1 * m_ref[...] + (1.0 - ADAM_B1) * gg
        vn = ADAM_B2 * v_ref[...] + (1.0 - ADAM_B2) * (gg * gg)
        m_hat = mn / (1.0 - ADAM_B1 ** ADAM_STEP)
        v_hat = vn / (1.0 - ADAM_B2 ** ADAM_STEP)
        d_ref[...] = -ADAM_LR * (m_hat / (jnp.sqrt(v_hat) + ADAM_EPS) + ADAM_WD * w_ref[...])
        mo_ref[...] = mn
        vo_ref[...] = vn

    blk = pl.BlockSpec((tr, C), lambda i: (i, 0))
    return pl.pallas_call(
        body, name="adamw", grid=(R // tr,), in_specs=[blk] * 4, out_specs=[blk] * 3,
        out_shape=[jax.ShapeDtypeStruct((R, C), F32)] * 3,
        compiler_params=_cparams("parallel"),
    )(w, g, m, v)


def _adamw_nd(w, g, m, v):
    shape = w.shape
    two_d = (1, shape[0]) if len(shape) == 1 else (int(np.prod(shape[:-1])), shape[-1])
    outs = _adamw(w.reshape(two_d), g.reshape(two_d), m.reshape(two_d), v.reshape(two_d))
    return tuple(o.reshape(shape) for o in outs)


def _f32_as_bf16_pairs(a):
    return lax.bitcast_convert_type(a, BF16).reshape(a.shape[:-1] + (a.shape[-1] * 2,))


def _bf16_pairs_as_f32(a):
    return lax.bitcast_convert_type(a.reshape(a.shape[:-1] + (a.shape[-1] // 2, 2)), F32)


def _pack_misc(w_o, w_dq, w_uq, w_dkv, pool_w, pool_scale):
    lead = w_o.shape[:-3]
    rows = [w_o, w_dq, w_uq, w_dkv, pool_w]
    flat = [r.astype(BF16).reshape(lead + (-1, REP_COLS)) for r in rows]
    ps = _f32_as_bf16_pairs(pool_scale.astype(F32)).reshape(lead + (1, -1))
    ps = jnp.concatenate([ps, jnp.zeros(lead + (1, REP_COLS - ps.shape[-1]), BF16)], axis=-1)
    used = sum(f.shape[-2] for f in flat) + 1
    pad = jnp.zeros(lead + (MISC_ROWS - used, REP_COLS), BF16)
    return jnp.concatenate(flat + [ps, pad], axis=-2)


def _unpack_misc(buf, shapes):
    out, r0 = [], 0
    for shp in shapes[:-1]:
        n = int(np.prod(shp)) // REP_COLS
        out.append(buf[:, r0:r0 + n].reshape((N_DEV,) + shp))
        r0 += n
    n_ps = int(np.prod(shapes[-1]))
    out.append(_bf16_pairs_as_f32(buf[:, r0, :2 * n_ps]).reshape((N_DEV,) + shapes[-1]))
    return out


def _cat_dev(a, axis):
    a = jnp.moveaxis(a, 0, axis)
    return a.reshape(a.shape[:axis] + (a.shape[axis] * a.shape[axis + 1],) + a.shape[axis + 2:])


def _split_dev(a, axis):
    a = a.reshape(a.shape[:axis] + (N_DEV, a.shape[axis] // N_DEV) + a.shape[axis + 1:])
    return jnp.moveaxis(a, axis, 0)


def kernel(x, ffn_pre_norm, ffn_pre_wg, ffn_pre_wu, ffn_pre_wd, mix_norm, ffn_post_norm, ffn_post_wg, ffn_post_wu, ffn_post_wd, pool_w, pool_scale, kv_in_norm, w_dkv, ckv_norm, w_uk, w_uv, q_lora_norm, w_dq, w_uq, w_o, final_norm, loss_target, m_ffn_pre_norm, m_ffn_pre_wg, m_ffn_pre_wu, m_ffn_pre_wd, m_mix_norm, m_ffn_post_norm, m_ffn_post_wg, m_ffn_post_wu, m_ffn_post_wd, m_pool_w, m_pool_scale, m_kv_in_norm, m_w_dkv, m_ckv_norm, m_w_uk, m_w_uv, m_q_lora_norm, m_w_dq, m_w_uq, m_w_o, m_final_norm, v_ffn_pre_norm, v_ffn_pre_wg, v_ffn_pre_wu, v_ffn_pre_wd, v_mix_norm, v_ffn_post_norm, v_ffn_post_wg, v_ffn_post_wu, v_ffn_post_wd, v_pool_w, v_pool_scale, v_kv_in_norm, v_w_dkv, v_ckv_norm, v_w_uk, v_w_uv, v_q_lora_norm, v_w_dq, v_w_uq, v_w_o, v_final_norm):
    weights = dict(ffn_pre_norm=ffn_pre_norm, ffn_pre_wg=ffn_pre_wg, ffn_pre_wu=ffn_pre_wu, ffn_pre_wd=ffn_pre_wd,
                   mix_norm=mix_norm, ffn_post_norm=ffn_post_norm, ffn_post_wg=ffn_post_wg,
                   ffn_post_wu=ffn_post_wu, ffn_post_wd=ffn_post_wd, pool_w=pool_w, pool_scale=pool_scale,
                   kv_in_norm=kv_in_norm, w_dkv=w_dkv, ckv_norm=ckv_norm, w_uk=w_uk, w_uv=w_uv,
                   q_lora_norm=q_lora_norm, w_dq=w_dq, w_uq=w_uq, w_o=w_o, final_norm=final_norm)
    moments_m = dict(ffn_pre_norm=m_ffn_pre_norm, ffn_pre_wg=m_ffn_pre_wg, ffn_pre_wu=m_ffn_pre_wu,
                     ffn_pre_wd=m_ffn_pre_wd, mix_norm=m_mix_norm, ffn_post_norm=m_ffn_post_norm,
                     ffn_post_wg=m_ffn_post_wg, ffn_post_wu=m_ffn_post_wu, ffn_post_wd=m_ffn_post_wd,
                     pool_w=m_pool_w, pool_scale=m_pool_scale, kv_in_norm=m_kv_in_norm, w_dkv=m_w_dkv,
                     ckv_norm=m_ckv_norm, w_uk=m_w_uk, w_uv=m_w_uv, q_lora_norm=m_q_lora_norm, w_dq=m_w_dq,
                     w_uq=m_w_uq, w_o=m_w_o, final_norm=m_final_norm)
    moments_v = dict(ffn_pre_norm=v_ffn_pre_norm, ffn_pre_wg=v_ffn_pre_wg, ffn_pre_wu=v_ffn_pre_wu,
                     ffn_pre_wd=v_ffn_pre_wd, mix_norm=v_mix_norm, ffn_post_norm=v_ffn_post_norm,
                     ffn_post_wg=v_ffn_post_wg, ffn_post_wu=v_ffn_post_wu, ffn_post_wd=v_ffn_post_wd,
                     pool_w=v_pool_w, pool_scale=v_pool_scale, kv_in_norm=v_kv_in_norm, w_dkv=v_w_dkv,
                     ckv_norm=v_ckv_norm, w_uk=v_w_uk, w_uv=v_w_uv, q_lora_norm=v_q_lora_norm, w_dq=v_w_dq,
                     w_uq=v_w_uq, w_o=v_w_o, final_norm=v_final_norm)
    order = list(weights)

    T, D = x.shape[1], x.shape[2]
    depth = ffn_pre_norm.shape[0]
    n_a = pool_w.shape[0]
    n_b = depth - n_a
    fs = ffn_pre_wd.shape[1]
    F = fs * N_DEV
    n_ffn = 2 * depth
    t_attn = min(ATTN_TILE, T)

    ffn_local = [
        jnp.stack([jnp.swapaxes(wg[l], 0, 1), jnp.swapaxes(wu[l], 0, 1), wd[l]]).astype(BF16)
        for l in range(depth)
        for wg, wu, wd in ((ffn_pre_wg, ffn_pre_wu, ffn_pre_wd), (ffn_post_wg, ffn_post_wu, ffn_post_wd))
    ]
    misc_local = _pack_misc(w_o, w_dq, w_uq.reshape(n_b, w_uq.shape[1], -1), w_dkv, pool_w, pool_scale)
    misc_shapes = (w_o.shape, w_dq.shape, (n_b, w_uq.shape[1], N_HEADS * D_QK), w_dkv.shape, pool_w.shape,
                   pool_scale.shape)
    w0_all, misc_all = _all_gather(ffn_local[0], misc_local)
    walls = [w0_all.reshape(3, F, D)] + [None] * (n_ffn - 1)
    o_blk, dq_blk, uq_blk, dkv_blk, pw_blk, ps_blk = _unpack_misc(misc_all, misc_shapes)
    w_o_f = _cat_dev(o_blk, 1)
    w_dq_f = _cat_dev(dq_blk, 1)
    w_uq_f = _cat_dev(uq_blk, 1).reshape(n_b, -1, N_HEADS, D_QK)
    w_dkv_f = _cat_dev(dkv_blk, 0)
    pool_w_f = _cat_dev(pw_blk, 2)
    pool_scale_f = _cat_dev(ps_blk, 1)
    rq = w_dq_f.shape[2]
    wqa, wqb = _rope_weight_pair(w_uq_f)
    wqa = wqa.reshape(n_b, rq, N_HEADS * D_HEAD_PAD)
    wqb = wqb.reshape(n_b, rq, N_HEADS * D_HEAD_PAD)
    wka, wkb = _rope_weight_pair(w_dkv_f)
    wuk = jnp.concatenate([w_uk, jnp.zeros_like(w_uk)], axis=-1).astype(BF16).reshape(D_NOPE, N_HEADS * D_HEAD_PAD)
    wuv = w_uv.astype(BF16).reshape(D_NOPE, N_HEADS * D_V)
    ca_q, ca_k, sb = _rope_tables(T)

    def vec(a):
        return a.reshape(1, -1)

    def ffn_stage(e, h_in, norm):
        nxt = ffn_local[e + 1] if e + 1 < n_ffn else None
        outs = _ffn_fwd(h_in, norm, walls[e], nxt)
        if nxt is not None:
            walls[e + 1] = outs[3].reshape(3, F, D)
        return outs[:3]

    h = x.reshape(T, D)
    saved = []
    k_all = v_all = craw = h_kv = None
    for l in range(depth):
        s = {"h0": h}
        h, s["g1"], s["u1"] = ffn_stage(2 * l, h, vec(ffn_pre_norm[l]))
        s["h1"] = h
        if l < n_a:
            h, s["y"] = _pool_fwd(h, vec(mix_norm[l]), pool_w_f[l], vec(pool_scale_f[l]))
        else:
            j = l - n_a
            s["cq"] = _mm_rows(h, w_dq_f[j], nt=False, out_dtype=F32, norm_g=vec(mix_norm[l]), name="q_down")
            s["q"] = _q_proj(s["cq"], vec(q_lora_norm[j]), wqa[j], wqb[j], ca_q, sb)
            s["o"], s["lse"] = _flash_fwd(s["q"], k_all, v_all)
            h = _mm_rows(s["o"], w_o_f[j], nt=False, out_dtype=F32, res=h, name="attn_out")
        s["h2"] = h
        h, s["g2"], s["u2"] = ffn_stage(2 * l + 1, h, vec(ffn_post_norm[l]))
        if l == n_a - 1:
            h_kv = h
            k_all, v_all, craw = _kv_proj(h, vec(kv_in_norm), wka, wkb, vec(ckv_norm), wuk, wuv, ca_k, sb)
        saved.append(s)

    dh, loss_part, d_final = _loss_head(h, vec(final_norm), loss_target.reshape(T, D))

    slabs, ffn_parts = [None] * n_ffn, [None] * n_ffn

    def ffn_stage_bwd(e, dh_out, h_in, norm, gate, up):
        carry = slabs[e + 1].reshape(3, N_DEV, fs, D) if e + 1 < n_ffn else None
        outs = _ffn_bwd_dx(dh_out, h_in, norm, gate, up, walls[e], carry)
        if carry is not None:
            ffn_parts[e + 1] = outs[6]
        dh_in, dgt, dup, u_b, dy_b, dnorm = outs[:6]
        slabs[e] = _ffn_bwd_dw(dgt, dup, gate, up, u_b, dy_b)
        return dh_in, dnorm

    grads = {}
    d_pre, d_post, d_mix = [None] * depth, [None] * depth, [None] * depth
    d_pool_w, d_pool_scale = [None] * n_a, [None] * n_a
    d_qln, d_wdq, d_wuq, d_wo = [None] * n_b, [None] * n_b, [None] * n_b, [None] * n_b
    dk_acc = jnp.zeros((T, N_HEADS * D_HEAD_PAD), F32)
    dv_acc = jnp.zeros((T, N_HEADS * D_V), F32)
    for l in reversed(range(depth)):
        s = saved[l]
        if l == n_a - 1:
            (dh, grads["kv_in_norm"], dwka, dwkb, grads["ckv_norm"], dwuk, dwuv) = _kv_proj_bwd(
                dk_acc, dv_acc, dh, h_kv, vec(kv_in_norm), wka, wkb, craw, vec(ckv_norm), wuk, wuv, ca_k, sb)
            grads["w_dkv"] = _rope_weight_pair_grad(dwka, dwkb)
            grads["w_uk"] = dwuk.reshape(D_NOPE, N_HEADS, D_HEAD_PAD)[..., :D_NOPE]
            grads["w_uv"] = dwuv.reshape(D_NOPE, N_HEADS, D_V)
        dh, d_post[l] = ffn_stage_bwd(2 * l + 1, dh, s["h2"], vec(ffn_post_norm[l]), s["g2"], s["u2"])
        if l < n_a:
            dh, d_mix[l], d_pool_w[l], d_pool_scale[l] = _pool_bwd(
                dh, s["h1"], vec(mix_norm[l]), s["y"], pool_w_f[l], vec(pool_scale_f[l]))
        else:
            j = l - n_a
            d_wo[j] = _mm_tn(s["o"], dh, name="attn_out_dw")
            do = _mm_rows(dh, w_o_f[j], nt=True, out_dtype=BF16, name="attn_out_dx")
            delta_row = _attn_delta(s["o"], do)
            dq_t, dk_acc, dv_acc = _flash_bwd(s["q"], k_all, v_all, do, s["lse"], delta_row, dk_acc, dv_acc)
            da, db, cqn, dcq, d_qln[j] = _q_proj_bwd(dq_t, s["cq"], vec(q_lora_norm[j]), wqa[j], wqb[j], ca_q, sb)
            dwa = _mm_tn(cqn, da, name="q_up_dw")
            dwb = _mm_tn(cqn, db, name="q_up_dw")
            d_wuq[j] = _rope_weight_pair_grad(dwa.reshape(rq, N_HEADS, D_HEAD_PAD),
                                              dwb.reshape(rq, N_HEADS, D_HEAD_PAD))
            d_wdq[j] = _mm_tn(s["h1"], dcq, norm_g=vec(mix_norm[l]), name="q_down_dw")
            dh, d_mix[l] = _proj_bwd(dcq, w_dq_f[j], s["h1"], vec(mix_norm[l]), dh, "q_down_dx")
        dh, d_pre[l] = ffn_stage_bwd(2 * l, dh, s["h0"], vec(ffn_pre_norm[l]), s["g1"], s["u1"])
    grad_x = dh.reshape(x.shape)

    rep_names = ["ffn_pre_norm", "mix_norm", "ffn_post_norm", "kv_in_norm", "ckv_norm", "q_lora_norm",
                 "final_norm", "w_uk", "w_uv"]
    grads["ffn_pre_norm"] = jnp.concatenate(d_pre, axis=0)
    grads["mix_norm"] = jnp.concatenate(d_mix, axis=0)
    grads["ffn_post_norm"] = jnp.concatenate(d_post, axis=0)
    grads["q_lora_norm"] = jnp.concatenate(d_qln, axis=0)
    grads["final_norm"] = d_final
    rep_flat = jnp.concatenate([grads[n].reshape(-1) for n in rep_names] + [loss_part[0, :1]])
    n_rep = rep_flat.shape[0]
    rep_rows = -(-n_rep // (8 * REP_COLS)) * 8
    rep_g = jnp.concatenate([rep_flat, jnp.zeros((rep_rows * REP_COLS - n_rep,), F32)]).reshape(rep_rows, REP_COLS)
    misc_g = _pack_misc(_split_dev(jnp.stack(d_wo), 1), _split_dev(jnp.stack(d_wdq), 1),
                        _split_dev(jnp.stack(d_wuq).reshape(n_b, rq, -1), 1), _split_dev(grads["w_dkv"], 0),
                        _split_dev(jnp.stack(d_pool_w), 2),
                        _split_dev(jnp.concatenate(d_pool_scale, axis=0), 1))
    ffn_parts[0], misc_parts, rep_parts = _grad_exchange(slabs[0].reshape(3, N_DEV, fs, D), misc_g, rep_g)
    ffn_sum = jnp.stack([_sum_parts(p.reshape(N_DEV, 3 * fs, D), fs, "sum_ffn").reshape(3, fs, D)
                         for p in ffn_parts])
    misc_sum_parts = _unpack_misc(misc_parts, misc_shapes)
    rep_sum = _sum_parts(rep_parts, _row_tile(rep_rows, 128), "sum_rep").reshape(-1)

    def sum_small(p):
        shp = p.shape[1:]
        two_d = (int(np.prod(shp[:-1])), shp[-1])
        return _sum_parts(p.reshape((N_DEV,) + two_d), two_d[0], "sum_misc").reshape(shp)

    g_wo, g_wdq, g_wuq, g_wdkv, g_pw, g_ps = [sum_small(p) for p in misc_sum_parts]
    grads.update(w_o=g_wo, w_dq=g_wdq, w_uq=g_wuq.reshape(w_uq.shape), w_dkv=g_wdkv, pool_w=g_pw, pool_scale=g_ps)
    for kind, (npre, npost) in enumerate((("ffn_pre_wg", "ffn_post_wg"), ("ffn_pre_wu", "ffn_post_wu"),
                                          ("ffn_pre_wd", "ffn_post_wd"))):
        pre = ffn_sum[0::2, kind]
        post = ffn_sum[1::2, kind]
        if kind < 2:
            pre, post = jnp.swapaxes(pre, 1, 2), jnp.swapaxes(post, 1, 2)
        grads[npre], grads[npost] = pre, post
    off = 0
    for n in rep_names:
        size = int(np.prod(weights[n].shape))
        grads[n] = rep_sum[off:off + size].reshape(weights[n].shape)
        off += size
    loss = rep_sum[off]

    deltas, new_m, new_v = {}, {}, {}
    for n in order:
        deltas[n], new_m[n], new_v[n] = _adamw_nd(weights[n], grads[n], moments_m[n], moments_v[n])
    return (loss, grad_x, *[grads[n] for n in order], *[deltas[n] for n in order],
            *[new_m[n] for n in order], *[new_v[n] for n in order])
```

```python
import functools

import numpy as np
import jax
import jax.numpy as jnp
from jax import lax
from jax.experimental import pallas as pl
from jax.experimental.pallas import tpu as pltpu

F32, BF16 = jnp.float32, jnp.bfloat16
N_DEV = 8
RMS_EPS = 1e-6
N_HEADS = 16
D_NOPE, D_ROPE, D_V = 128, 64, 128
D_QK = D_NOPE + D_ROPE
D_HEAD_PAD = 256
HEAD_GROUP = 4
ROPE_THETA = 10000.0
POOL_WINDOWS = (2, 4, 8, 16)
POOL_HALO = 16
ADAM_LR, ADAM_B1, ADAM_B2, ADAM_EPS, ADAM_WD, ADAM_STEP = 0.001, 0.9, 0.999, 1e-08, 0.01, 10
NEG_BIG = -1e30
V7X_VMEM_LIMIT = 56 * 1024 * 1024
TOKEN_TILE = 512
ATTN_TILE = 512
FFN_TILE = 256
MISC_ROWS = 864
REP_COLS = 1024


def _cparams(*sem):
    return pltpu.CompilerParams(dimension_semantics=sem, vmem_limit_bytes=V7X_VMEM_LIMIT)


def _dot(a, b):
    return lax.dot_general(a, b, (((1,), (0,)), ((), ())), preferred_element_type=F32)


def _dot_nt(a, b):
    return lax.dot_general(a, b, (((1,), (1,)), ((), ())), preferred_element_type=F32)


def _dot_tn(a, b):
    return lax.dot_general(a, b, (((0,), (0,)), ((), ())), preferred_element_type=F32)


def _rms_fwd(x, g):
    r = lax.rsqrt(jnp.mean(x * x, axis=-1, keepdims=True) + RMS_EPS)
    return (x * r) * g


def _rms_bwd(du, x, g):
    r = lax.rsqrt(jnp.mean(x * x, axis=-1, keepdims=True) + RMS_EPS)
    xh = x * r
    dg = jnp.sum(du * xh, axis=0, keepdims=True)
    dxh = du * g
    dx = r * (dxh - xh * jnp.mean(dxh * xh, axis=-1, keepdims=True))
    return dx, dg


def _sigmoid(x):
    return 1.0 / (1.0 + jnp.exp(-x))


def _split_bf16(x):
    hi = x.astype(BF16)
    lo = (x - hi.astype(F32)).astype(BF16)
    return hi, lo


def _full(shape):
    return pl.BlockSpec(shape, lambda *_: (0,) * len(shape))


def _resident(shape):
    return pl.BlockSpec(shape, lambda *_: (0,) * len(shape), pipeline_mode=pl.Buffered(1))


def _row_tile(rows, cap):
    for t in range(min(cap, rows) // 8 * 8, 0, -8):
        if rows % t == 0:
            return t
    return rows


def _peers():
    x, y, c = lax.axis_index("x"), lax.axis_index("y"), lax.axis_index("c")
    out = []
    for k in range(1, N_DEV):
        px = 1 - x if (k >> 2) & 1 else x
        py = 1 - y if (k >> 1) & 1 else y
        pc = 1 - c if k & 1 else c
        out.append(((px, py, pc), 4 * px + 2 * py + pc))
    return 4 * x + 2 * y + c, out


def _exchange(arrays, src_of, dst_of, out_shapes, name):
    n = len(arrays)

    def body(*refs):
        ins, outs, sems = refs[:n], refs[n:2 * n], refs[2 * n:]
        _exchange_start(ins, outs, sems, src_of, dst_of)
        _exchange_wait(ins, outs, sems, src_of, dst_of)

    any_spec = pl.BlockSpec(memory_space=pl.ANY)
    return pl.pallas_call(
        body, name=name,
        out_shape=[jax.ShapeDtypeStruct(s, a.dtype) for s, a in zip(out_shapes, arrays)],
        in_specs=[any_spec] * n, out_specs=[any_spec] * n,
        scratch_shapes=_exchange_sems(n),
    )(*arrays)


def _exchange_sems(n):
    return [pltpu.SemaphoreType.DMA((n, N_DEV - 1)), pltpu.SemaphoreType.DMA((n, N_DEV - 1)),
            pltpu.SemaphoreType.DMA((n,))]


def _own_copies(ins, outs, sems, src_of, dst_of):
    me, _ = _peers()
    return [pltpu.make_async_copy(src_of(j, ins[j], me), dst_of(j, outs[j], me), sems[2].at[j])
            for j in range(len(ins))]


def _remote_copies(ins, outs, sems, src_of, dst_of, receiving):
    me, peers = _peers()
    return [pltpu.make_async_remote_copy(
        src_ref=src_of(j, ins[j], pidx), dst_ref=dst_of(j, outs[j], pidx if receiving else me),
        send_sem=sems[0].at[j, k], recv_sem=sems[1].at[j, k],
        device_id=peer, device_id_type=pl.DeviceIdType.MESH)
        for k, (peer, pidx) in enumerate(peers) for j in range(len(ins))]


def _exchange_start(ins, outs, sems, src_of, dst_of):
    for cp in _own_copies(ins, outs, sems, src_of, dst_of):
        cp.start()
    for cp in _remote_copies(ins, outs, sems, src_of, dst_of, receiving=False):
        cp.start()


def _exchange_wait(ins, outs, sems, src_of, dst_of):
    for cp in _remote_copies(ins, outs, sems, src_of, dst_of, receiving=True):
        cp.wait_recv()
    for cp in _remote_copies(ins, outs, sems, src_of, dst_of, receiving=False):
        cp.wait_send()
    for cp in _own_copies(ins, outs, sems, src_of, dst_of):
        cp.wait()


def _all_gather(ffn_local, misc_local):
    w, r, d = ffn_local.shape

    def src_of(j, ref, idx):
        return ref

    def dst_of(j, ref, idx):
        return ref.at[:, idx] if j == 0 else ref.at[idx]

    return _exchange([ffn_local, misc_local], src_of, dst_of,
                     [(w, N_DEV, r, d), (N_DEV,) + misc_local.shape], "comm_all_gather")


def _grad_exchange(ffn_g, misc_g, rep_g):
    w, _, r, d = ffn_g.shape

    def src_of(j, ref, idx):
        return (ref.at[:, idx], ref.at[idx], ref)[j]

    def dst_of(j, ref, idx):
        return ref.at[idx]

    return _exchange([ffn_g, misc_g, rep_g], src_of, dst_of,
                     [(N_DEV, w, r, d), misc_g.shape, (N_DEV,) + rep_g.shape], "comm_grad_exchange")


def _carried(carry):
    if carry is None:
        return [], [], [], []
    arr, out_shape = carry
    return ([arr], [pl.BlockSpec(memory_space=pl.ANY)], [jax.ShapeDtypeStruct(out_shape, arr.dtype)],
            _exchange_sems(1))


def _gather_fns():
    return (lambda j, ref, idx: ref), (lambda j, ref, idx: ref.at[:, idx])


def _scatter_fns():
    return (lambda j, ref, idx: ref.at[:, idx]), (lambda j, ref, idx: ref.at[idx])


def _ffn_fwd(h, g, w3, next_local=None):
    T, D = h.shape
    F = w3.shape[1]
    tm, tf = min(TOKEN_TILE, T), F // 2
    nf = F // tf
    nt = T // tm
    has_c = next_local is not None
    c_in, c_specs, c_out, c_sems = _carried(
        (next_local, next_local.shape[:1] + (N_DEV,) + next_local.shape[1:]) if has_c else None)

    def body(*refs):
        h_ref, g_ref, w_ref = refs[:3]
        ho_ref, gate_ref, up_ref = refs[3 + has_c:6 + has_c]
        i = pl.program_id(0)
        if has_c:
            comm = ([refs[3]], [refs[6 + has_c]], refs[6 + 2 * has_c:], *_gather_fns())

            @pl.when(i == 0)
            def _():
                _exchange_start(*comm)

        x = h_ref[...]
        u = _rms_fwd(x, g_ref[...]).astype(BF16)
        acc = None
        for f in range(nf):
            cols = slice(f * tf, (f + 1) * tf)
            gate = _dot_nt(u, w_ref[0, cols, :])
            up = _dot_nt(u, w_ref[1, cols, :])
            gate_ref[:, cols] = gate.astype(BF16)
            up_ref[:, cols] = up.astype(BF16)
            part = _dot((gate * _sigmoid(gate) * up).astype(BF16), w_ref[2, cols, :])
            acc = part if acc is None else acc + part
        ho_ref[...] = x + 0.5 * acc

        if has_c:
            @pl.when(i == nt - 1)
            def _():
                _exchange_wait(*comm)

    row = pl.BlockSpec((tm, D), lambda i: (i, 0))
    wide = pl.BlockSpec((tm, F), lambda i: (i, 0))
    return pl.pallas_call(
        body, name="ffn_fwd_gather" if has_c else "ffn_fwd", grid=(nt,),
        in_specs=[row, _full((1, D)), _resident(w3.shape)] + c_specs,
        out_specs=[row, wide, wide] + c_specs,
        out_shape=[jax.ShapeDtypeStruct((T, D), F32), jax.ShapeDtypeStruct((T, F), BF16),
                   jax.ShapeDtypeStruct((T, F), BF16)] + c_out,
        scratch_shapes=c_sems,
        compiler_params=_cparams("arbitrary"),
    )(h, g, w3, *c_in)


def _ffn_bwd_dx(dho, h, g, gate, up, w3, grad_slab=None):
    T, D = h.shape
    F = w3.shape[1]
    tm, tf = min(TOKEN_TILE // 2, T), F // 2
    nf = F // tf
    nt = T // tm
    has_c = grad_slab is not None
    c_in, c_specs, c_out, c_sems = _carried(
        (grad_slab, (N_DEV, grad_slab.shape[0]) + grad_slab.shape[2:]) if has_c else None)

    def body(*refs):
        dho_ref, h_ref, g_ref, gate_ref, up_ref, w_ref = refs[:6]
        dhi_ref, dgate_ref, dup_ref, u_ref, dy_ref, dg_ref = refs[6 + has_c:12 + has_c]
        i = pl.program_id(0)
        if has_c:
            comm = ([refs[6]], [refs[12 + has_c]], refs[12 + 2 * has_c:], *_scatter_fns())

            @pl.when(i == 0)
            def _():
                _exchange_start(*comm)

        dho = dho_ref[...]
        x = h_ref[...]
        dy = (0.5 * dho).astype(BF16)
        dy_ref[...] = dy
        u_ref[...] = _rms_fwd(x, g_ref[...]).astype(BF16)
        acc = None
        for f in range(nf):
            cols = slice(f * tf, (f + 1) * tf)
            dact = _dot_nt(dy, w_ref[2, cols, :])
            gt = gate_ref[:, cols].astype(F32)
            sig = _sigmoid(gt)
            dup = (dact * (gt * sig)).astype(BF16)
            dgate = (dact * up_ref[:, cols].astype(F32) * (sig * (1.0 + gt * (1.0 - sig)))).astype(BF16)
            dup_ref[:, cols] = dup
            dgate_ref[:, cols] = dgate
            part = _dot(dgate, w_ref[0, cols, :]) + _dot(dup, w_ref[1, cols, :])
            acc = part if acc is None else acc + part
        dx, dg = _rms_bwd(acc, x, g_ref[...])
        dhi_ref[...] = dho + dx

        @pl.when(i == 0)
        def _():
            dg_ref[...] = jnp.zeros_like(dg_ref)

        dg_ref[...] += dg

        if has_c:
            @pl.when(i == nt - 1)
            def _():
                _exchange_wait(*comm)

    row = pl.BlockSpec((tm, D), lambda i: (i, 0))
    wide = pl.BlockSpec((tm, F), lambda i: (i, 0))
    return pl.pallas_call(
        body, name="ffn_bwd_dx_scatter" if has_c else "ffn_bwd_dx", grid=(nt,),
        in_specs=[row, row, _full((1, D)), wide, wide, _resident(w3.shape)] + c_specs,
        out_specs=[row, wide, wide, row, row, _full((1, D))] + c_specs,
        out_shape=[jax.ShapeDtypeStruct((T, D), F32), jax.ShapeDtypeStruct((T, F), BF16),
                   jax.ShapeDtypeStruct((T, F), BF16), jax.ShapeDtypeStruct((T, D), BF16),
                   jax.ShapeDtypeStruct((T, D), BF16), jax.ShapeDtypeStruct((1, D), F32)] + c_out,
        scratch_shapes=c_sems,
        compiler_params=_cparams("arbitrary"),
    )(dho, h, g, gate, up, w3, *c_in)


def _ffn_bwd_dw(dgate, dup, gate, up, u, dy):
    T, F = gate.shape
    D = u.shape[1]
    tfw = F // 2
    tk = min(TOKEN_TILE, T)
    nk = T // tk

    def body(dgate_ref, dup_ref, gate_ref, up_ref, u_ref, dy_ref, out_ref, acc_sc):
        k = pl.program_id(1)

        @pl.when(k == 0)
        def _():
            acc_sc[...] = jnp.zeros_like(acc_sc)

        uu = u_ref[...]
        gt = gate_ref[...].astype(F32)
        act = (gt * _sigmoid(gt) * up_ref[...].astype(F32)).astype(BF16)
        acc_sc[0] += _dot_tn(dgate_ref[...], uu)
        acc_sc[1] += _dot_tn(dup_ref[...], uu)
        acc_sc[2] += _dot_tn(act, dy_ref[...])

        @pl.when(k == nk - 1)
        def _():
            out_ref[...] = acc_sc[...].astype(BF16)

    blk = pl.BlockSpec((tk, tfw), lambda j, k: (k, j))
    row = pl.BlockSpec((tk, D), lambda j, k: (k, 0))
    return pl.pallas_call(
        body, name="ffn_bwd_dw", grid=(F // tfw, nk),
        in_specs=[blk, blk, blk, blk, row, row],
        out_specs=pl.BlockSpec((3, tfw, D), lambda j, k: (0, j, 0)),
        out_shape=jax.ShapeDtypeStruct((3, F, D), BF16),
        scratch_shapes=[pltpu.VMEM((3, tfw, D), F32)],
        compiler_params=_cparams("parallel", "arbitrary"),
    )(dgate, dup, gate, up, u, dy)


def _mm_rows(a, b, *, nt, out_dtype, norm_g=None, res=None, name):
    T, K = a.shape
    N = b.shape[0] if nt else b.shape[1]
    tm = min(TOKEN_TILE, T)
    has_g, has_r = norm_g is not None, res is not None

    def body(*refs):
        a_ref, b_ref = refs[0], refs[1]
        o_ref = refs[-1]
        x = a_ref[...]
        if has_g:
            x = _rms_fwd(x, refs[2][...])
        x = x.astype(BF16)
        acc = _dot_nt(x, b_ref[...]) if nt else _dot(x, b_ref[...])
        if has_r:
            acc = refs[2 + has_g][...] + acc
        o_ref[...] = acc.astype(out_dtype)

    ins, specs = [a, b], [pl.BlockSpec((tm, K), lambda i: (i, 0)), _full(b.shape)]
    if has_g:
        ins.append(norm_g)
        specs.append(_full((1, K)))
    if has_r:
        ins.append(res)
        specs.append(pl.BlockSpec((tm, N), lambda i: (i, 0)))
    return pl.pallas_call(
        body, name=name, grid=(T // tm,), in_specs=specs,
        out_specs=pl.BlockSpec((tm, N), lambda i: (i, 0)),
        out_shape=jax.ShapeDtypeStruct((T, N), out_dtype),
        compiler_params=_cparams("parallel"),
    )(*ins)


def _mm_tn(a, b, *, norm_g=None, name):
    T, M = a.shape
    N = b.shape[1]
    tk = min(TOKEN_TILE, T)
    has_g = norm_g is not None

    def body(*refs):
        a_ref, b_ref, o_ref = refs[0], refs[1], refs[-1]

        @pl.when(pl.program_id(0) == 0)
        def _():
            o_ref[...] = jnp.zeros_like(o_ref)

        x = a_ref[...]
        if has_g:
            x = _rms_fwd(x, refs[2][...])
        o_ref[...] += _dot_tn(x.astype(BF16), b_ref[...].astype(BF16))

    ins = [a, b]
    specs = [pl.BlockSpec((tk, M), lambda k: (k, 0)), pl.BlockSpec((tk, N), lambda k: (k, 0))]
    if has_g:
        ins.append(norm_g)
        specs.append(_full((1, M)))
    return pl.pallas_call(
        body, name=name, grid=(T // tk,), in_specs=specs, out_specs=_full((M, N)),
        out_shape=jax.ShapeDtypeStruct((M, N), F32),
        compiler_params=_cparams("arbitrary"),
    )(*ins)


def _proj_bwd(dz, w, h, g, dh, name):
    T, D = h.shape
    N = w.shape[1]
    tm = min(TOKEN_TILE, T)

    def body(dz_ref, w_ref, h_ref, g_ref, dh_ref, o_ref, dg_ref):
        du = _dot_nt(dz_ref[...].astype(BF16), w_ref[...])
        dx, dg = _rms_bwd(du, h_ref[...], g_ref[...])
        o_ref[...] = dh_ref[...] + dx

        @pl.when(pl.program_id(0) == 0)
        def _():
            dg_ref[...] = jnp.zeros_like(dg_ref)

        dg_ref[...] += dg

    row = pl.BlockSpec((tm, D), lambda i: (i, 0))
    return pl.pallas_call(
        body, name=name, grid=(T // tm,),
        in_specs=[pl.BlockSpec((tm, N), lambda i: (i, 0)), _full((D, N)), row, _full((1, D)), row],
        out_specs=[row, _full((1, D))],
        out_shape=[jax.ShapeDtypeStruct((T, D), F32), jax.ShapeDtypeStruct((1, D), F32)],
        compiler_params=_cparams("arbitrary"),
    )(dz, w, h, g, dh)


def _pool_bands(tm):
    r = np.arange(tm)[:, None]
    c = np.arange(tm)[None, :]
    j = np.arange(POOL_HALO)[None, :]
    main, halo, main_t, halo_t = [], [], [], []
    for w in POOL_WINDOWS:
        main.append(((r - c >= 0) & (r - c < w)) / w)
        halo.append((r + POOL_HALO - j < w) / w)
        main_t.append(((c - r >= 0) & (c - r < w)) / w)
        halo_t.append((tm + j - r < w) / w)
    return tuple(jnp.asarray(np.stack(m), BF16) for m in (main, halo, main_t, halo_t))


def _pool_count_scale(i, tm, w):
    t = i * tm + lax.broadcasted_iota(jnp.int32, (tm, 1), 0)
    return w / jnp.minimum(t + 1, w).astype(F32)


def _pool_fwd(h, g, wp, scale):
    T, D = h.shape
    G, dg = len(POOL_WINDOWS), D // len(POOL_WINDOWS)
    tm = min(TOKEN_TILE, T)
    hb = tm // POOL_HALO
    bm, bh, _, _ = _pool_bands(tm)

    def body(h_ref, hh_ref, g_ref, wp_ref, sc_ref, bm_ref, bh_ref, ho_ref, y_ref):
        i = pl.program_id(0)
        x = h_ref[...]
        u = _rms_fwd(x, g_ref[...])
        uh = _rms_fwd(hh_ref[...], g_ref[...]) * (i > 0).astype(F32)
        for gi, w in enumerate(POOL_WINDOWS):
            cols = slice(gi * dg, (gi + 1) * dg)
            ug = u[:, cols]
            hi, lo = _split_bf16(ug)
            hhi, hlo = _split_bf16(uh[:, cols])
            s = (_dot(bm_ref[gi], hi) + _dot(bm_ref[gi], lo)
                 + _dot(bh_ref[gi], hhi) + _dot(bh_ref[gi], hlo))
            y = (s * _pool_count_scale(i, tm, w) - ug).astype(BF16)
            y_ref[:, cols] = y
            ho_ref[:, cols] = x[:, cols] + _dot(y, wp_ref[gi]) * sc_ref[:, cols]

    row = pl.BlockSpec((tm, D), lambda i: (i, 0))
    return pl.pallas_call(
        body, name="pool_fwd", grid=(T // tm,),
        in_specs=[row, pl.BlockSpec((POOL_HALO, D), lambda i: (jnp.maximum(i * hb - 1, 0), 0)),
                  _full((1, D)), _full((G, dg, dg)), _full((1, D)),
                  _full((G, tm, tm)), _full((G, tm, POOL_HALO))],
        out_specs=[row, row],
        out_shape=[jax.ShapeDtypeStruct((T, D), F32), jax.ShapeDtypeStruct((T, D), BF16)],
        compiler_params=_cparams("parallel"),
    )(h, h, g, wp, scale, bm, bh)


def _pool_bwd(dh, h, g, y, wp, scale):
    T, D = h.shape
    G, dg = len(POOL_WINDOWS), D // len(POOL_WINDOWS)
    tm = min(TOKEN_TILE, T)
    hb = tm // POOL_HALO
    nt = T // tm
    _, _, bmt, bht = _pool_bands(tm)

    def body(dh_ref, dhn_ref, h_ref, g_ref, y_ref, wp_ref, sc_ref, bmt_ref, bht_ref,
             o_ref, dg_ref, dwp_ref, dsc_ref, du_sc):
        i = pl.program_id(0)

        @pl.when(i == 0)
        def _():
            dg_ref[...] = jnp.zeros_like(dg_ref)
            dwp_ref[...] = jnp.zeros_like(dwp_ref)
            dsc_ref[...] = jnp.zeros_like(dsc_ref)

        dho = dh_ref[...]
        dz = dho * sc_ref[...]
        dzn = dhn_ref[...] * sc_ref[...] * (i < nt - 1).astype(F32)
        for gi, w in enumerate(POOL_WINDOWS):
            cols = slice(gi * dg, (gi + 1) * dg)
            yg = y_ref[:, cols]
            dzg = dz[:, cols].astype(BF16)
            dsc_ref[:, cols] += jnp.sum(dho[:, cols] * _dot(yg, wp_ref[gi]), axis=0, keepdims=True)
            dwp_ref[gi] += _dot_tn(yg, dzg)
            dy = _dot_nt(dzg, wp_ref[gi])
            dyn = _dot_nt(dzn[:, cols].astype(BF16), wp_ref[gi])
            hi, lo = _split_bf16(dy * _pool_count_scale(i, tm, w))
            nhi, nlo = _split_bf16(dyn)
            du_sc[:, cols] = (_dot(bmt_ref[gi], hi) + _dot(bmt_ref[gi], lo)
                              + _dot(bht_ref[gi], nhi) + _dot(bht_ref[gi], nlo) - dy)
        dx, dgp = _rms_bwd(du_sc[...], h_ref[...], g_ref[...])
        o_ref[...] = dho + dx
        dg_ref[...] += dgp

    row = pl.BlockSpec((tm, D), lambda i: (i, 0))
    return pl.pallas_call(
        body, name="pool_bwd", grid=(nt,),
        in_specs=[row, pl.BlockSpec((POOL_HALO, D), lambda i: (jnp.minimum((i + 1) * hb, T // POOL_HALO - 1), 0)),
                  row, _full((1, D)), row, _full((G, dg, dg)), _full((1, D)),
                  _full((G, tm, tm)), _full((G, tm, POOL_HALO))],
        out_specs=[row, _full((1, D)), _full((G, dg, dg)), _full((1, D))],
        out_shape=[jax.ShapeDtypeStruct((T, D), F32), jax.ShapeDtypeStruct((1, D), F32),
                   jax.ShapeDtypeStruct((G, dg, dg), F32), jax.ShapeDtypeStruct((1, D), F32)],
        scratch_shapes=[pltpu.VMEM((tm, D), F32)],
        compiler_params=_cparams("arbitrary"),
    )(dh, dh, h, g, y, wp, scale, bmt, bht)


def _rope_tables(T):
    pos = jnp.arange(T, dtype=F32)
    inv_freq = ROPE_THETA ** (-jnp.arange(0, D_ROPE, 2, dtype=F32) / D_ROPE)
    ang = pos[:, None] * inv_freq[None, :]
    cos2 = jnp.tile(jnp.cos(ang), (1, 2))
    sin2 = jnp.tile(jnp.sin(ang), (1, 2))
    pad = jnp.zeros((T, D_HEAD_PAD - D_QK), F32)
    ca_q = jnp.concatenate([jnp.ones((T, D_NOPE), F32), cos2, pad], axis=1)
    ca_k = jnp.concatenate([jnp.zeros((T, D_NOPE), F32), cos2, pad], axis=1)
    sb = jnp.concatenate([jnp.zeros((T, D_NOPE), F32), sin2, pad], axis=1)
    return ca_q, ca_k, sb


def _rope_weight_pair(w):
    half = D_ROPE // 2
    z_pad = jnp.zeros(w.shape[:-1] + (D_HEAD_PAD - D_QK,), w.dtype)
    z_nope = jnp.zeros(w.shape[:-1] + (D_NOPE,), w.dtype)
    wa = jnp.concatenate([w, z_pad], axis=-1)
    wb = jnp.concatenate([z_nope, -w[..., D_NOPE + half:], w[..., D_NOPE:D_NOPE + half], z_pad], axis=-1)
    return wa, wb


def _rope_weight_pair_grad(dwa, dwb):
    half = D_ROPE // 2
    d1 = dwa[..., D_NOPE:D_NOPE + half] + dwb[..., D_NOPE + half:D_QK]
    d2 = dwa[..., D_NOPE + half:D_QK] - dwb[..., D_NOPE:D_NOPE + half]
    return jnp.concatenate([dwa[..., :D_NOPE], d1, d2], axis=-1)


def _q_proj(cq, qg, wa, wb, ca, sb):
    T, R = cq.shape
    tm = min(TOKEN_TILE, T)
    P = D_HEAD_PAD
    GP = HEAD_GROUP * P

    def body(cq_ref, qg_ref, wa_ref, wb_ref, ca_ref, sb_ref, q_ref):
        c = _rms_fwd(cq_ref[...], qg_ref[...]).astype(BF16)
        ca = jnp.tile(ca_ref[...], (1, HEAD_GROUP))
        sb = jnp.tile(sb_ref[...], (1, HEAD_GROUP))
        q_ref[...] = (_dot(c, wa_ref[...]) * ca + _dot(c, wb_ref[...]) * sb).astype(BF16)

    tok = pl.BlockSpec((tm, P), lambda i, hh: (i, 0))
    wsp = pl.BlockSpec((R, GP), lambda i, hh: (0, hh))
    return pl.pallas_call(
        body, name="q_proj", grid=(T // tm, N_HEADS // HEAD_GROUP),
        in_specs=[pl.BlockSpec((tm, R), lambda i, hh: (i, 0)), _full((1, R)), wsp, wsp, tok, tok],
        out_specs=pl.BlockSpec((tm, GP), lambda i, hh: (i, hh)),
        out_shape=jax.ShapeDtypeStruct((T, N_HEADS * P), BF16),
        compiler_params=_cparams("parallel", "arbitrary"),
    )(cq, qg, wa, wb, ca, sb)


def _q_proj_bwd(dq, cq, qg, wa, wb, ca, sb):
    T, R = cq.shape
    tm = min(TOKEN_TILE, T)
    P = D_HEAD_PAD

    def body(dq_ref, cq_ref, qg_ref, wa_ref, wb_ref, ca_ref, sb_ref,
             da_ref, db_ref, cqn_ref, dcq_ref, dqg_ref, acc_sc):
        i, hh = pl.program_id(0), pl.program_id(1)

        @pl.when(hh == 0)
        def _():
            acc_sc[...] = jnp.zeros_like(acc_sc)
            cqn_ref[...] = _rms_fwd(cq_ref[...], qg_ref[...]).astype(BF16)

        d = dq_ref[...].T
        da = (d * jnp.tile(ca_ref[...], (1, HEAD_GROUP))).astype(BF16)
        db = (d * jnp.tile(sb_ref[...], (1, HEAD_GROUP))).astype(BF16)
        da_ref[...] = da
        db_ref[...] = db
        acc_sc[...] += _dot_nt(da, wa_ref[...]) + _dot_nt(db, wb_ref[...])

        @pl.when(hh == N_HEADS // HEAD_GROUP - 1)
        def _():
            dx, dg = _rms_bwd(acc_sc[...], cq_ref[...], qg_ref[...])
            dcq_ref[...] = dx

            @pl.when(i == 0)
            def _():
                dqg_ref[...] = jnp.zeros_like(dqg_ref)

            dqg_ref[...] += dg

    GP = HEAD_GROUP * P
    tok = pl.BlockSpec((tm, P), lambda i, hh: (i, 0))
    hd = pl.BlockSpec((tm, GP), lambda i, hh: (i, hh))
    wsp = pl.BlockSpec((R, GP), lambda i, hh: (0, hh))
    rr = pl.BlockSpec((tm, R), lambda i, hh: (i, 0))
    return pl.pallas_call(
        body, name="q_proj_bwd", grid=(T // tm, N_HEADS // HEAD_GROUP),
        in_specs=[pl.BlockSpec((GP, tm), lambda i, hh: (hh, i)), rr, _full((1, R)), wsp, wsp, tok, tok],
        out_specs=[hd, hd, rr, rr, _full((1, R))],
        out_shape=[jax.ShapeDtypeStruct((T, N_HEADS * P), BF16), jax.ShapeDtypeStruct((T, N_HEADS * P), BF16),
                   jax.ShapeDtypeStruct((T, R), BF16), jax.ShapeDtypeStruct((T, R), F32),
                   jax.ShapeDtypeStruct((1, R), F32)],
        scratch_shapes=[pltpu.VMEM((tm, R), F32)],
        compiler_params=_cparams("arbitrary", "arbitrary"),
    )(dq, cq, qg, wa, wb, ca, sb)


def _kv_proj(h, g_in, wka, wkb, g_c, wuk, wuv, ca, sb):
    T, D = h.shape
    tm = min(TOKEN_TILE, T)
    P, C = D_HEAD_PAD, D_NOPE

    def body(h_ref, gi_ref, wka_ref, wkb_ref, gc_ref, wuk_ref, wuv_ref, ca_ref, sb_ref, k_ref, v_ref, craw_ref):
        u = _rms_fwd(h_ref[...], gi_ref[...]).astype(BF16)
        kva = _dot(u, wka_ref[...])
        kvb = _dot(u, wkb_ref[...])
        craw = kva[:, :C]
        craw_ref[...] = craw
        c = _rms_fwd(craw, gc_ref[...]).astype(BF16)
        kr = kva * ca_ref[...] + kvb * sb_ref[...]
        kn = _dot(c, wuk_ref[...])
        for hh in range(N_HEADS):
            k_ref[:, hh * P:(hh + 1) * P] = (kn[:, hh * P:(hh + 1) * P] + kr).astype(BF16)
        v_ref[...] = _dot(c, wuv_ref[...]).astype(BF16)

    tok = pl.BlockSpec((tm, P), lambda i: (i, 0))
    return pl.pallas_call(
        body, name="kv_proj", grid=(T // tm,),
        in_specs=[pl.BlockSpec((tm, D), lambda i: (i, 0)), _full((1, D)), _full((D, P)), _full((D, P)),
                  _full((1, C)), _full(wuk.shape), _full(wuv.shape), tok, tok],
        out_specs=[pl.BlockSpec((tm, N_HEADS * P), lambda i: (i, 0)),
                   pl.BlockSpec((tm, N_HEADS * D_V), lambda i: (i, 0)), pl.BlockSpec((tm, C), lambda i: (i, 0))],
        out_shape=[jax.ShapeDtypeStruct((T, N_HEADS * P), BF16), jax.ShapeDtypeStruct((T, N_HEADS * D_V), BF16),
                   jax.ShapeDtypeStruct((T, C), F32)],
        compiler_params=_cparams("parallel"),
    )(h, g_in, wka, wkb, g_c, wuk, wuv, ca, sb)


def _kv_proj_bwd(dk, dv, dh, h, g_in, wka, wkb, craw, g_c, wuk, wuv, ca, sb):
    T, D = h.shape
    tm = min(TOKEN_TILE // 2, T)
    P, C = D_HEAD_PAD, D_NOPE

    def body(dk_ref, dv_ref, dh_ref, h_ref, gi_ref, wka_ref, wkb_ref, craw_ref, gc_ref, wuk_ref, wuv_ref,
             ca_ref, sb_ref, o_ref, dgi_ref, dwka_ref, dwkb_ref, dgc_ref, dwuk_ref, dwuv_ref):
        @pl.when(pl.program_id(0) == 0)
        def _():
            for r in (dgi_ref, dwka_ref, dwkb_ref, dgc_ref, dwuk_ref, dwuv_ref):
                r[...] = jnp.zeros_like(r)

        x = h_ref[...]
        u = _rms_fwd(x, gi_ref[...]).astype(BF16)
        craw = craw_ref[...]
        c = _rms_fwd(craw, gc_ref[...]).astype(BF16)
        dkf = dk_ref[...]
        dkb = dkf.astype(BF16)
        dvb = dv_ref[...].astype(BF16)
        dwuk_ref[...] += _dot_tn(c, dkb)
        dwuv_ref[...] += _dot_tn(c, dvb)
        dc = _dot_nt(dkb, wuk_ref[...]) + _dot_nt(dvb, wuv_ref[...])
        dkr = dkf[:, :P]
        for hh in range(1, N_HEADS):
            dkr = dkr + dkf[:, hh * P:(hh + 1) * P]
        dcraw, dgc = _rms_bwd(dc, craw, gc_ref[...])
        dgc_ref[...] += dgc
        dkva = jnp.concatenate([dcraw, (dkr * ca_ref[...])[:, C:]], axis=1).astype(BF16)
        dkvb = (dkr * sb_ref[...]).astype(BF16)
        dwka_ref[...] += _dot_tn(u, dkva)
        dwkb_ref[...] += _dot_tn(u, dkvb)
        du = _dot_nt(dkva, wka_ref[...]) + _dot_nt(dkvb, wkb_ref[...])
        dx, dgi = _rms_bwd(du, x, gi_ref[...])
        dgi_ref[...] += dgi
        o_ref[...] = dh_ref[...] + dx

    row = pl.BlockSpec((tm, D), lambda i: (i, 0))
    tok = pl.BlockSpec((tm, P), lambda i: (i, 0))
    return pl.pallas_call(
        body, name="kv_proj_bwd", grid=(T // tm,),
        in_specs=[pl.BlockSpec((tm, N_HEADS * P), lambda i: (i, 0)),
                  pl.BlockSpec((tm, N_HEADS * D_V), lambda i: (i, 0)), row, row, _full((1, D)),
                  _full((D, P)), _full((D, P)), pl.BlockSpec((tm, C), lambda i: (i, 0)), _full((1, C)),
                  _full(wuk.shape), _full(wuv.shape), tok, tok],
        out_specs=[row, _full((1, D)), _full((D, P)), _full((D, P)), _full((1, C)),
                   _full(wuk.shape), _full(wuv.shape)],
        out_shape=[jax.ShapeDtypeStruct((T, D), F32), jax.ShapeDtypeStruct((1, D), F32),
                   jax.ShapeDtypeStruct((D, P), F32), jax.ShapeDtypeStruct((D, P), F32),
                   jax.ShapeDtypeStruct((1, C), F32), jax.ShapeDtypeStruct(wuk.shape, F32),
                   jax.ShapeDtypeStruct(wuv.shape, F32)],
        compiler_params=_cparams("arbitrary"),
    )(dk, dv, dh, h, g_in, wka, wkb, craw, g_c, wuk, wuv, ca, sb)


_ATTN_SCALE = D_QK ** -0.5
_LOG2_E = 1.4426950408889634
_ATTN_SCALE_LOG2 = _ATTN_SCALE * _LOG2_E


def _causal_mask(t):
    return lax.broadcasted_iota(jnp.int32, (t, t), 1) <= lax.broadcasted_iota(jnp.int32, (t, t), 0)


def _causal_mask_t(t):
    return lax.broadcasted_iota(jnp.int32, (t, t), 0) <= lax.broadcasted_iota(jnp.int32, (t, t), 1)


def _flash_fwd(q, k, v):
    T = q.shape[0]
    t = min(ATTN_TILE, T)
    P = D_HEAD_PAD

    def body(q_ref, k_ref, v_ref, o_ref, lse_ref, m_sc, l_sc, acc_sc, s0_sc, s1_sc, p0_sc, p1_sc, a0_sc, a1_sc):
        n = pl.program_id(1) + 1
        s_sc, p_sc, a_sc = (s0_sc, s1_sc), (p0_sc, p1_sc), (a0_sc, a1_sc)
        m_sc[...] = jnp.full_like(m_sc, NEG_BIG)
        l_sc[...] = jnp.zeros_like(l_sc)
        acc_sc[...] = jnp.zeros_like(acc_sc)

        def rows_of(c):
            return pl.ds(pl.multiple_of(c * t, t), t)

        def scores(c, slot):
            s_sc[slot][...] = _dot_nt(k_ref[rows_of(c), :], q_ref[...])

        def softmax(slot, diagonal):
            s_t = s_sc[slot][...]
            if diagonal:
                s_t = jnp.where(_causal_mask_t(t), s_t, NEG_BIG)
            m_prev = m_sc[...]
            m_new = jnp.maximum(m_prev, jnp.max(s_t, axis=0, keepdims=True))
            p_t = jnp.exp2((s_t - m_new) * _ATTN_SCALE_LOG2)
            alpha = jnp.exp2((m_prev - m_new) * _ATTN_SCALE_LOG2)
            l_sc[...] = alpha * l_sc[...] + jnp.sum(p_t, axis=0, keepdims=True)
            m_sc[...] = m_new
            p_sc[slot][...] = p_t.astype(BF16)
            a_sc[slot][...] = alpha

        def values(c, slot):
            acc_sc[...] = a_sc[slot][...] * acc_sc[...] + _dot_tn(v_ref[rows_of(c), :], p_sc[slot][...])

        def stage(c, slot, first=False, last=False):
            if not first:
                values(c - 1, 1 - slot)
            if not last:
                scores(c + 1, 1 - slot)
            softmax(slot, diagonal=last)

        scores(0, 0)

        @pl.when(n == 1)
        def _():
            softmax(0, diagonal=True)
            values(0, 0)

        @pl.when(n >= 2)
        def _():
            stage(0, 0, first=True)

            def pair(j, carry):
                stage(1 + 2 * j, 1)
                stage(2 + 2 * j, 0)
                return carry

            lax.fori_loop(0, (n - 2) // 2, pair, 0)

            @pl.when((n - 2) % 2 == 1)
            def _():
                stage(n - 2, 1)

            for slot in range(2):
                @pl.when((n - 1) % 2 == slot)
                def _():
                    stage(n - 1, slot, last=True)
                    values(n - 1, slot)

        l = l_sc[...]
        o_ref[...] = (acc_sc[...] / l).T.astype(BF16)
        lse_ref[...] = m_sc[...] * _ATTN_SCALE + jnp.log(l)

    return pl.pallas_call(
        body, name="flash_fwd", grid=(N_HEADS, T // t),
        in_specs=[pl.BlockSpec((t, P), lambda hh, i: (i, hh)), pl.BlockSpec((T, P), lambda hh, i: (0, hh)),
                  pl.BlockSpec((T, D_V), lambda hh, i: (0, hh))],
        out_specs=[pl.BlockSpec((t, D_V), lambda hh, i: (i, hh)),
                   pl.BlockSpec((None, None, 1, t), lambda hh, i: (hh, i, 0, 0))],
        out_shape=[jax.ShapeDtypeStruct((T, N_HEADS * D_V), BF16),
                   jax.ShapeDtypeStruct((N_HEADS, T // t, 1, t), F32)],
        scratch_shapes=[pltpu.VMEM((1, t), F32), pltpu.VMEM((1, t), F32), pltpu.VMEM((D_V, t), F32),
                        pltpu.VMEM((t, t), F32), pltpu.VMEM((t, t), F32), pltpu.VMEM((t, t), BF16),
                        pltpu.VMEM((t, t), BF16), pltpu.VMEM((1, t), F32), pltpu.VMEM((1, t), F32)],
        compiler_params=_cparams("parallel", "arbitrary"),
    )(q, k, v)


def _attn_delta(o, do):
    T = o.shape[0]
    t = min(ATTN_TILE, T)

    def body(o_ref, do_ref, out_ref):
        ones = jnp.ones((8, D_V), BF16)
        for hh in range(N_HEADS):
            cols = slice(hh * D_V, (hh + 1) * D_V)
            hi, lo = _split_bf16(o_ref[:, cols].astype(F32) * do_ref[:, cols].astype(F32))
            out_ref[hh] = (_dot_nt(ones, hi) + _dot_nt(ones, lo))[0:1]

    tok = pl.BlockSpec((t, N_HEADS * D_V), lambda i: (i, 0))
    return pl.pallas_call(
        body, name="attn_delta", grid=(T // t,), in_specs=[tok, tok],
        out_specs=pl.BlockSpec((N_HEADS, None, 1, t), lambda i: (0, i, 0, 0)),
        out_shape=jax.ShapeDtypeStruct((N_HEADS, T // t, 1, t), F32),
        compiler_params=_cparams("parallel"),
    )(o, do)


def _flash_bwd(q, k, v, do, lse_row, delta_row, dk_prev, dv_prev):
    T = q.shape[0]
    t = min(ATTN_TILE, T)
    nq = T // t
    P = D_HEAD_PAD

    def body(k_ref, v_ref, q_ref, do_ref, lse_ref, delta_ref, dkp_ref, dvp_ref,
             dqt_ref, dk_ref, dv_ref, dk_sc, dv_sc, dqt_sc, kt_sc,
             s0_sc, s1_sc, dp0_sc, dp1_sc, p0_sc, p1_sc, ds0_sc, ds1_sc):
        ki = pl.program_id(1)
        n = nq - ki
        s_sc, dp_sc, p_sc, ds_sc = (s0_sc, s1_sc), (dp0_sc, dp1_sc), (p0_sc, p1_sc), (ds0_sc, ds1_sc)
        kt_sc[...] = k_ref[...].astype(F32).T.astype(BF16)
        dk_sc[...] = dkp_ref[...]
        dv_sc[...] = dvp_ref[...]

        @pl.when(ki == 0)
        def _():
            dqt_sc[...] = jnp.zeros_like(dqt_sc)

        def rows_of(c):
            return pl.ds(pl.multiple_of((ki + c) * t, t), t)

        def products(c, slot):
            s_sc[slot][...] = _dot_nt(k_ref[...], q_ref[rows_of(c), :])
            dp_sc[slot][...] = _dot_nt(v_ref[...], do_ref[rows_of(c), :])

        def elementwise(c, slot, diagonal):
            p_t = jnp.exp2(s_sc[slot][...] * _ATTN_SCALE_LOG2 - lse_ref[ki + c] * _LOG2_E)
            if diagonal:
                p_t = jnp.where(_causal_mask_t(t), p_t, 0.0)
            p_sc[slot][...] = p_t.astype(BF16)
            ds_sc[slot][...] = (p_t * (dp_sc[slot][...] - delta_ref[ki + c]) * _ATTN_SCALE).astype(BF16)

        def gradients(c, slot):
            dv_sc[...] += _dot(p_sc[slot][...], do_ref[rows_of(c), :])
            ds_t = ds_sc[slot][...]
            dk_sc[...] += _dot(ds_t, q_ref[rows_of(c), :])
            dqt_sc[ki + c] += _dot(kt_sc[...], ds_t)

        def stage(c, slot, first=False, last=False):
            if not first:
                gradients(c - 1, 1 - slot)
            if not last:
                products(c + 1, 1 - slot)
            elementwise(c, slot, diagonal=first)

        products(0, 0)

        @pl.when(n == 1)
        def _():
            elementwise(0, 0, diagonal=True)
            gradients(0, 0)

        @pl.when(n >= 2)
        def _():
            stage(0, 0, first=True)

            def pair(j, carry):
                stage(1 + 2 * j, 1)
                stage(2 + 2 * j, 0)
                return carry

            lax.fori_loop(0, (n - 2) // 2, pair, 0)

            @pl.when((n - 2) % 2 == 1)
            def _():
                stage(n - 2, 1)

            for slot in range(2):
                @pl.when((n - 1) % 2 == slot)
                def _():
                    stage(n - 1, slot, last=True)
                    gradients(n - 1, slot)

        dqt_ref[...] = dqt_sc[ki]
        dk_ref[...] = dk_sc[...]
        dv_ref[...] = dv_sc[...]

    kb = pl.BlockSpec((t, P), lambda hh, i: (i, hh))
    vb = pl.BlockSpec((t, D_V), lambda hh, i: (i, hh))
    stat = pl.BlockSpec((None, nq, 1, t), lambda hh, i: (hh, 0, 0, 0))
    return pl.pallas_call(
        body, name="flash_bwd", grid=(N_HEADS, nq),
        in_specs=[kb, vb,
                  pl.BlockSpec((T, P), lambda hh, i: (0, hh), pipeline_mode=pl.Buffered(1)),
                  pl.BlockSpec((T, D_V), lambda hh, i: (0, hh), pipeline_mode=pl.Buffered(1)),
                  stat, stat, kb, vb],
        out_specs=[pl.BlockSpec((P, t), lambda hh, i: (hh, i)), kb, vb],
        out_shape=[jax.ShapeDtypeStruct((N_HEADS * P, T), F32), jax.ShapeDtypeStruct((T, N_HEADS * P), F32),
                   jax.ShapeDtypeStruct((T, N_HEADS * D_V), F32)],
        scratch_shapes=[pltpu.VMEM((t, P), F32), pltpu.VMEM((t, D_V), F32), pltpu.VMEM((nq, P, t), F32),
                        pltpu.VMEM((P, t), BF16)] + [pltpu.VMEM((t, t), F32)] * 4 + [pltpu.VMEM((t, t), BF16)] * 4,
        compiler_params=_cparams("arbitrary", "arbitrary"),
    )(k, v, q, do, lse_row, delta_row, dk_prev, dv_prev)


def _loss_head(h, g, target):
    T, D = h.shape
    tm = min(TOKEN_TILE, T)

    def body(h_ref, g_ref, t_ref, dh_ref, loss_ref, dg_ref):
        @pl.when(pl.program_id(0) == 0)
        def _():
            loss_ref[...] = jnp.zeros_like(loss_ref)
            dg_ref[...] = jnp.zeros_like(dg_ref)

        x = h_ref[...]
        err = _rms_fwd(x, g_ref[...]) - t_ref[...]
        per_tok = jnp.mean(err * err, axis=-1, keepdims=True)
        loss_ref[...] += 0.5 * jnp.sum(per_tok, axis=0, keepdims=True)
        dx, dg = _rms_bwd(err * (1.0 / D), x, g_ref[...])
        dh_ref[...] = dx
        dg_ref[...] += dg

    row = pl.BlockSpec((tm, D), lambda i: (i, 0))
    return pl.pallas_call(
        body, name="loss_head", grid=(T // tm,),
        in_specs=[row, _full((1, D)), row], out_specs=[row, _full((1, 128)), _full((1, D))],
        out_shape=[jax.ShapeDtypeStruct((T, D), F32), jax.ShapeDtypeStruct((1, 128), F32),
                   jax.ShapeDtypeStruct((1, D), F32)],
        compiler_params=_cparams("arbitrary"),
    )(h, g, target)


def _sum_parts(parts, tr, name):
    _, R, C = parts.shape

    def body(p_ref, o_ref):
        acc = p_ref[0].astype(F32)
        for j in range(1, N_DEV):
            acc = acc + p_ref[j].astype(F32)
        o_ref[...] = acc

    return pl.pallas_call(
        body, name=name, grid=(R // tr,),
        in_specs=[pl.BlockSpec((N_DEV, tr, C), lambda i: (0, i, 0))],
        out_specs=pl.BlockSpec((tr, C), lambda i: (i, 0)),
        out_shape=jax.ShapeDtypeStruct((R, C), F32),
        compiler_params=_cparams("parallel"),
    )(parts)


def _adamw(w, g, m, v):
    R, C = w.shape
    tr = _row_tile(R, TOKEN_TILE)

    def body(w_ref, g_ref, m_ref, v_ref, d_ref, mo_ref, vo_ref):
        gg = g_ref[...]
        mn = ADAM_B1 * m_ref[...] + (1.0 - ADAM_B1) * gg
        vn = ADAM_B2 * v_ref[...] + (1.0 - ADAM_B2) * (gg * gg)
        m_hat = mn / (1.0 - ADAM_B1 ** ADAM_STEP)
        v_hat = vn / (1.0 - ADAM_B2 ** ADAM_STEP)
        d_ref[...] = -ADAM_LR * (m_hat / (jnp.sqrt(v_hat) + ADAM_EPS) + ADAM_WD * w_ref[...])
        mo_ref[...] = mn
        vo_ref[...] = vn

    blk = pl.BlockSpec((tr, C), lambda i: (i, 0))
    return pl.pallas_call(
        body, name="adamw", grid=(R // tr,), in_specs=[blk] * 4, out_specs=[blk] * 3,
        out_shape=[jax.ShapeDtypeStruct((R, C), F32)] * 3,
        compiler_params=_cparams("parallel"),
    )(w, g, m, v)


def _adamw_nd(w, g, m, v):
    shape = w.shape
    two_d = (1, shape[0]) if len(shape) == 1 else (int(np.prod(shape[:-1])), shape[-1])
    outs = _adamw(w.reshape(two_d), g.reshape(two_d), m.reshape(two_d), v.reshape(two_d))
    return tuple(o.reshape(shape) for o in outs)


def _f32_as_bf16_pairs(a):
    return lax.bitcast_convert_type(a, BF16).reshape(a.shape[:-1] + (a.shape[-1] * 2,))


def _bf16_pairs_as_f32(a):
    return lax.bitcast_convert_type(a.reshape(a.shape[:-1] + (a.shape[-1] // 2, 2)), F32)


def _pack_misc(w_o, w_dq, w_uq, w_dkv, pool_w, pool_scale):
    lead = w_o.shape[:-3]
    rows = [w_o, w_dq, w_uq, w_dkv, pool_w]
    flat = [r.astype(BF16).reshape(lead + (-1, REP_COLS)) for r in rows]
    ps = _f32_as_bf16_pairs(pool_scale.astype(F32)).reshape(lead + (1, -1))
    ps = jnp.concatenate([ps, jnp.zeros(lead + (1, REP_COLS - ps.shape[-1]), BF16)], axis=-1)
    used = sum(f.shape[-2] for f in flat) + 1
    pad = jnp.zeros(lead + (MISC_ROWS - used, REP_COLS), BF16)
    return jnp.concatenate(flat + [ps, pad], axis=-2)


def _unpack_misc(buf, shapes):
    out, r0 = [], 0
    for shp in shapes[:-1]:
        n = int(np.prod(shp)) // REP_COLS
        out.append(buf[:, r0:r0 + n].reshape((N_DEV,) + shp))
        r0 += n
    n_ps = int(np.prod(shapes[-1]))
    out.append(_bf16_pairs_as_f32(buf[:, r0, :2 * n_ps]).reshape((N_DEV,) + shapes[-1]))
    return out


def _cat_dev(a, axis):
    a = jnp.moveaxis(a, 0, axis)
    return a.reshape(a.shape[:axis] + (a.shape[axis] * a.shape[axis + 1],) + a.shape[axis + 2:])


def _split_dev(a, axis):
    a = a.reshape(a.shape[:axis] + (N_DEV, a.shape[axis] // N_DEV) + a.shape[axis + 1:])
    return jnp.moveaxis(a, axis, 0)


def kernel(x, ffn_pre_norm, ffn_pre_wg, ffn_pre_wu, ffn_pre_wd, mix_norm, ffn_post_norm, ffn_post_wg, ffn_post_wu, ffn_post_wd, pool_w, pool_scale, kv_in_norm, w_dkv, ckv_norm, w_uk, w_uv, q_lora_norm, w_dq, w_uq, w_o, final_norm, loss_target, m_ffn_pre_norm, m_ffn_pre_wg, m_ffn_pre_wu, m_ffn_pre_wd, m_mix_norm, m_ffn_post_norm, m_ffn_post_wg, m_ffn_post_wu, m_ffn_post_wd, m_pool_w, m_pool_scale, m_kv_in_norm, m_w_dkv, m_ckv_norm, m_w_uk, m_w_uv, m_q_lora_norm, m_w_dq, m_w_uq, m_w_o, m_final_norm, v_ffn_pre_norm, v_ffn_pre_wg, v_ffn_pre_wu, v_ffn_pre_wd, v_mix_norm, v_ffn_post_norm, v_ffn_post_wg, v_ffn_post_wu, v_ffn_post_wd, v_pool_w, v_pool_scale, v_kv_in_norm, v_w_dkv, v_ckv_norm, v_w_uk, v_w_uv, v_q_lora_norm, v_w_dq, v_w_uq, v_w_o, v_final_norm):
    weights = dict(ffn_pre_norm=ffn_pre_norm, ffn_pre_wg=ffn_pre_wg, ffn_pre_wu=ffn_pre_wu, ffn_pre_wd=ffn_pre_wd,
                   mix_norm=mix_norm, ffn_post_norm=ffn_post_norm, ffn_post_wg=ffn_post_wg,
                   ffn_post_wu=ffn_post_wu, ffn_post_wd=ffn_post_wd, pool_w=pool_w, pool_scale=pool_scale,
                   kv_in_norm=kv_in_norm, w_dkv=w_dkv, ckv_norm=ckv_norm, w_uk=w_uk, w_uv=w_uv,
                   q_lora_norm=q_lora_norm, w_dq=w_dq, w_uq=w_uq, w_o=w_o, final_norm=final_norm)
    moments_m = dict(ffn_pre_norm=m_ffn_pre_norm, ffn_pre_wg=m_ffn_pre_wg, ffn_pre_wu=m_ffn_pre_wu,
                     ffn_pre_wd=m_ffn_pre_wd, mix_norm=m_mix_norm, ffn_post_norm=m_ffn_post_norm,
                     ffn_post_wg=m_ffn_post_wg, ffn_post_wu=m_ffn_post_wu, ffn_post_wd=m_ffn_post_wd,
                     pool_w=m_pool_w, pool_scale=m_pool_scale, kv_in_norm=m_kv_in_norm, w_dkv=m_w_dkv,
                     ckv_norm=m_ckv_norm, w_uk=m_w_uk, w_uv=m_w_uv, q_lora_norm=m_q_lora_norm, w_dq=m_w_dq,
                     w_uq=m_w_uq, w_o=m_w_o, final_norm=m_final_norm)
    moments_v = dict(ffn_pre_norm=v_ffn_pre_norm, ffn_pre_wg=v_ffn_pre_wg, ffn_pre_wu=v_ffn_pre_wu,
                     ffn_pre_wd=v_ffn_pre_wd, mix_norm=v_mix_norm, ffn_post_norm=v_ffn_post_norm,
                     ffn_post_wg=v_ffn_post_wg, ffn_post_wu=v_ffn_post_wu, ffn_post_wd=v_ffn_post_wd,
                     pool_w=v_pool_w, pool_scale=v_pool_scale, kv_in_norm=v_kv_in_norm, w_dkv=v_w_dkv,
                     ckv_norm=v_ckv_norm, w_uk=v_w_uk, w_uv=v_w_uv, q_lora_norm=v_q_lora_norm, w_dq=v_w_dq,
                     w_uq=v_w_uq, w_o=v_w_o, final_norm=v_final_norm)
    order = list(weights)

    T, D = x.shape[1], x.shape[2]
    depth = ffn_pre_norm.shape[0]
    n_a = pool_w.shape[0]
    n_b = depth - n_a
    fs = ffn_pre_wd.shape[1]
    F = fs * N_DEV
    n_ffn = 2 * depth
    t_attn = min(ATTN_TILE, T)

    ffn_local = [
        jnp.stack([jnp.swapaxes(wg[l], 0, 1), jnp.swapaxes(wu[l], 0, 1), wd[l]]).astype(BF16)
        for l in range(depth)
        for wg, wu, wd in ((ffn_pre_wg, ffn_pre_wu, ffn_pre_wd), (ffn_post_wg, ffn_post_wu, ffn_post_wd))
    ]
    misc_local = _pack_misc(w_o, w_dq, w_uq.reshape(n_b, w_uq.shape[1], -1), w_dkv, pool_w, pool_scale)
    misc_shapes = (w_o.shape, w_dq.shape, (n_b, w_uq.shape[1], N_HEADS * D_QK), w_dkv.shape, pool_w.shape,
                   pool_scale.shape)
    w0_all, misc_all = _all_gather(ffn_local[0], misc_local)
    walls = [w0_all.reshape(3, F, D)] + [None] * (n_ffn - 1)
    o_blk, dq_blk, uq_blk, dkv_blk, pw_blk, ps_blk = _unpack_misc(misc_all, misc_shapes)
    w_o_f = _cat_dev(o_blk, 1)
    w_dq_f = _cat_dev(dq_blk, 1)
    w_uq_f = _cat_dev(uq_blk, 1).reshape(n_b, -1, N_HEADS, D_QK)
    w_dkv_f = _cat_dev(dkv_blk, 0)
    pool_w_f = _cat_dev(pw_blk, 2)
    pool_scale_f = _cat_dev(ps_blk, 1)
    rq = w_dq_f.shape[2]
    wqa, wqb = _rope_weight_pair(w_uq_f)
    wqa = wqa.reshape(n_b, rq, N_HEADS * D_HEAD_PAD)
    wqb = wqb.reshape(n_b, rq, N_HEADS * D_HEAD_PAD)
    wka, wkb = _rope_weight_pair(w_dkv_f)
    wuk = jnp.concatenate([w_uk, jnp.zeros_like(w_uk)], axis=-1).astype(BF16).reshape(D_NOPE, N_HEADS * D_HEAD_PAD)
    wuv = w_uv.astype(BF16).reshape(D_NOPE, N_HEADS * D_V)
    ca_q, ca_k, sb = _rope_tables(T)

    def vec(a):
        return a.reshape(1, -1)

    def ffn_stage(e, h_in, norm):
        nxt = ffn_local[e + 1] if e + 1 < n_ffn else None
        outs = _ffn_fwd(h_in, norm, walls[e], nxt)
        if nxt is not None:
            walls[e + 1] = outs[3].reshape(3, F, D)
        return outs[:3]

    h = x.reshape(T, D)
    saved = []
    k_all = v_all = craw = h_kv = None
    for l in range(depth):
        s = {"h0": h}
        h, s["g1"], s["u1"] = ffn_stage(2 * l, h, vec(ffn_pre_norm[l]))
        s["h1"] = h
        if l < n_a:
            h, s["y"] = _pool_fwd(h, vec(mix_norm[l]), pool_w_f[l], vec(pool_scale_f[l]))
        else:
            j = l - n_a
            s["cq"] = _mm_rows(h, w_dq_f[j], nt=False, out_dtype=F32, norm_g=vec(mix_norm[l]), name="q_down")
            s["q"] = _q_proj(s["cq"], vec(q_lora_norm[j]), wqa[j], wqb[j], ca_q, sb)
            s["o"], s["lse"] = _flash_fwd(s["q"], k_all, v_all)
            h = _mm_rows(s["o"], w_o_f[j], nt=False, out_dtype=F32, res=h, name="attn_out")
        s["h2"] = h
        h, s["g2"], s["u2"] = ffn_stage(2 * l + 1, h, vec(ffn_post_norm[l]))
        if l == n_a - 1:
            h_kv = h
            k_all, v_all, craw = _kv_proj(h, vec(kv_in_norm), wka, wkb, vec(ckv_norm), wuk, wuv, ca_k, sb)
        saved.append(s)

    dh, loss_part, d_final = _loss_head(h, vec(final_norm), loss_target.reshape(T, D))

    slabs, ffn_parts = [None] * n_ffn, [None] * n_ffn

    def ffn_stage_bwd(e, dh_out, h_in, norm, gate, up):
        carry = slabs[e + 1].reshape(3, N_DEV, fs, D) if e + 1 < n_ffn else None
        outs = _ffn_bwd_dx(dh_out, h_in, norm, gate, up, walls[e], carry)
        if carry is not None:
            ffn_parts[e + 1] = outs[6]
        dh_in, dgt, dup, u_b, dy_b, dnorm = outs[:6]
        slabs[e] = _ffn_bwd_dw(dgt, dup, gate, up, u_b, dy_b)
        return dh_in, dnorm

    grads = {}
    d_pre, d_post, d_mix = [None] * depth, [None] * depth, [None] * depth
    d_pool_w, d_pool_scale = [None] * n_a, [None] * n_a
    d_qln, d_wdq, d_wuq, d_wo = [None] * n_b, [None] * n_b, [None] * n_b, [None] * n_b
    dk_acc = jnp.zeros((T, N_HEADS * D_HEAD_PAD), F32)
    dv_acc = jnp.zeros((T, N_HEADS * D_V), F32)
    for l in reversed(range(depth)):
        s = saved[l]
        if l == n_a - 1:
            (dh, grads["kv_in_norm"], dwka, dwkb, grads["ckv_norm"], dwuk, dwuv) = _kv_proj_bwd(
                dk_acc, dv_acc, dh, h_kv, vec(kv_in_norm), wka, wkb, craw, vec(ckv_norm), wuk, wuv, ca_k, sb)
            grads["w_dkv"] = _rope_weight_pair_grad(dwka, dwkb)
            grads["w_uk"] = dwuk.reshape(D_NOPE, N_HEADS, D_HEAD_PAD)[..., :D_NOPE]
            grads["w_uv"] = dwuv.reshape(D_NOPE, N_HEADS, D_V)
        dh, d_post[l] = ffn_stage_bwd(2 * l + 1, dh, s["h2"], vec(ffn_post_norm[l]), s["g2"], s["u2"])
        if l < n_a:
            dh, d_mix[l], d_pool_w[l], d_pool_scale[l] = _pool_bwd(
                dh, s["h1"], vec(mix_norm[l]), s["y"], pool_w_f[l], vec(pool_scale_f[l]))
        else:
            j = l - n_a
            d_wo[j] = _mm_tn(s["o"], dh, name="attn_out_dw")
            do = _mm_rows(dh, w_o_f[j], nt=True, out_dtype=BF16, name="attn_out_dx")
            delta_row = _attn_delta(s["o"], do)
            dq_t, dk_acc, dv_acc = _flash_bwd(s["q"], k_all, v_all, do, s["lse"], delta_row, dk_acc, dv_acc)
            da, db, cqn, dcq, d_qln[j] = _q_proj_bwd(dq_t, s["cq"], vec(q_lora_norm[j]), wqa[j], wqb[j], ca_q, sb)
            dwa = _mm_tn(cqn, da, name="q_up_dw")
            dwb = _mm_tn(cqn, db, name="q_up_dw")
            d_wuq[j] = _rope_weight_pair_grad(dwa.reshape(rq, N_HEADS, D_HEAD_PAD),
                                              dwb.reshape(rq, N_HEADS, D_HEAD_PAD))
            d_wdq[j] = _mm_tn(s["h1"], dcq, norm_g=vec(mix_norm[l]), name="q_down_dw")
            dh, d_mix[l] = _proj_bwd(dcq, w_dq_f[j], s["h1"], vec(mix_norm[l]), dh, "q_down_dx")
        dh, d_pre[l] = ffn_stage_bwd(2 * l, dh, s["h0"], vec(ffn_pre_norm[l]), s["g1"], s["u1"])
    grad_x = dh.reshape(x.shape)

    rep_names = ["ffn_pre_norm", "mix_norm", "ffn_post_norm", "kv_in_norm", "ckv_norm", "q_lora_norm",
                 "final_norm", "w_uk", "w_uv"]
    grads["ffn_pre_norm"] = jnp.concatenate(d_pre, axis=0)
    grads["mix_norm"] = jnp.concatenate(d_mix, axis=0)
    grads["ffn_post_norm"] = jnp.concatenate(d_post, axis=0)
    grads["q_lora_norm"] = jnp.concatenate(d_qln, axis=0)
    grads["final_norm"] = d_final
    rep_flat = jnp.concatenate([grads[n].reshape(-1) for n in rep_names] + [loss_part[0, :1]])
    n_rep = rep_flat.shape[0]
    rep_rows = -(-n_rep // (8 * REP_COLS)) * 8
    rep_g = jnp.concatenate([rep_flat, jnp.zeros((rep_rows * REP_COLS - n_rep,), F32)]).reshape(rep_rows, REP_COLS)
    misc_g = _pack_misc(_split_dev(jnp.stack(d_wo), 1), _split_dev(jnp.stack(d_wdq), 1),
                        _split_dev(jnp.stack(d_wuq).reshape(n_b, rq, -1), 1), _split_dev(grads["w_dkv"], 0),
                        _split_dev(jnp.stack(d_pool_w), 2),
                        _split_dev(jnp.concatenate(d_pool_scale, axis=0), 1))
    ffn_parts[0], misc_parts, rep_parts = _grad_exchange(slabs[0].reshape(3, N_DEV, fs, D), misc_g, rep_g)
    ffn_sum = jnp.stack([_sum_parts(p.reshape(N_DEV, 3 * fs, D), fs, "sum_ffn").reshape(3, fs, D)
                         for p in ffn_parts])
    misc_sum_parts = _unpack_misc(misc_parts, misc_shapes)
    rep_sum = _sum_parts(rep_parts, _row_tile(rep_rows, 128), "sum_rep").reshape(-1)

    def sum_small(p):
        shp = p.shape[1:]
        two_d = (int(np.prod(shp[:-1])), shp[-1])
        return _sum_parts(p.reshape((N_DEV,) + two_d), two_d[0], "sum_misc").reshape(shp)

    g_wo, g_wdq, g_wuq, g_wdkv, g_pw, g_ps = [sum_small(p) for p in misc_sum_parts]
    grads.update(w_o=g_wo, w_dq=g_wdq, w_uq=g_wuq.reshape(w_uq.shape), w_dkv=g_wdkv, pool_w=g_pw, pool_scale=g_ps)
    for kind, (npre, npost) in enumerate((("ffn_pre_wg", "ffn_post_wg"), ("ffn_pre_wu", "ffn_post_wu"),
                                          ("ffn_pre_wd", "ffn_post_wd"))):
        pre = ffn_sum[0::2, kind]
        post = ffn_sum[1::2, kind]
        if kind < 2:
            pre, post = jnp.swapaxes(pre, 1, 2), jnp.swapaxes(post, 1, 2)
        grads[npre], grads[npost] = pre, post
    off = 0
    for n in rep_names:
        size = int(np.prod(weights[n].shape))
        grads[n] = rep_sum[off:off + size].reshape(weights[n].shape)
        off += size
    loss = rep_sum[off]

    deltas, new_m, new_v = {}, {}, {}
    for n in order:
        deltas[n], new_m[n], new_v[n] = _adamw_nd(weights[n], grads[n], moments_m[n], moments_v[n])
    return (loss, grad_x, *[grads[n] for n in order], *[deltas[n] for n in order],
            *[new_m[n] for n in order], *[new_v[n] for n in order])
```

```python
import functools

import numpy as np
import jax
import jax.numpy as jnp
from jax import lax
from jax.experimental import pallas as pl
from jax.experimental.pallas import tpu as pltpu

F32, BF16 = jnp.float32, jnp.bfloat16
N_DEV = 8
RMS_EPS = 1e-6
N_HEADS = 16
D_NOPE, D_ROPE, D_V = 128, 64, 128
D_QK = D_NOPE + D_ROPE
D_HEAD_PAD = 256
HEAD_GROUP = 4
ROPE_THETA = 10000.0
POOL_WINDOWS = (2, 4, 8, 16)
POOL_HALO = 16
ADAM_LR, ADAM_B1, ADAM_B2, ADAM_EPS, ADAM_WD, ADAM_STEP = 0.001, 0.9, 0.999, 1e-08, 0.01, 10
NEG_BIG = -1e30
V7X_VMEM_LIMIT = 56 * 1024 * 1024
TOKEN_TILE = 512
ATTN_TILE = 512
FFN_TILE = 256
MISC_ROWS = 864
REP_COLS = 1024


def _cparams(*sem):
    return pltpu.CompilerParams(dimension_semantics=sem, vmem_limit_bytes=V7X_VMEM_LIMIT)


def _dot(a, b):
    return lax.dot_general(a, b, (((1,), (0,)), ((), ())), preferred_element_type=F32)


def _dot_nt(a, b):
    return lax.dot_general(a, b, (((1,), (1,)), ((), ())), preferred_element_type=F32)


def _dot_tn(a, b):
    return lax.dot_general(a, b, (((0,), (0,)), ((), ())), preferred_element_type=F32)


def _rms_fwd(x, g):
    r = lax.rsqrt(jnp.mean(x * x, axis=-1, keepdims=True) + RMS_EPS)
    return (x * r) * g


def _rms_bwd(du, x, g):
    r = lax.rsqrt(jnp.mean(x * x, axis=-1, keepdims=True) + RMS_EPS)
    xh = x * r
    dg = jnp.sum(du * xh, axis=0, keepdims=True)
    dxh = du * g
    dx = r * (dxh - xh * jnp.mean(dxh * xh, axis=-1, keepdims=True))
    return dx, dg


def _sigmoid(x):
    return 1.0 / (1.0 + jnp.exp(-x))


def _split_bf16(x):
    hi = x.astype(BF16)
    lo = (x - hi.astype(F32)).astype(BF16)
    return hi, lo


def _full(shape):
    return pl.BlockSpec(shape, lambda *_: (0,) * len(shape))


def _resident(shape):
    return pl.BlockSpec(shape, lambda *_: (0,) * len(shape), pipeline_mode=pl.Buffered(1))


def _row_tile(rows, cap):
    for t in range(min(cap, rows) // 8 * 8, 0, -8):
        if rows % t == 0:
            return t
    return rows


def _peers():
    x, y, c = lax.axis_index("x"), lax.axis_index("y"), lax.axis_index("c")
    out = []
    for k in range(1, N_DEV):
        px = 1 - x if (k >> 2) & 1 else x
        py = 1 - y if (k >> 1) & 1 else y
        pc = 1 - c if k & 1 else c
        out.append(((px, py, pc), 4 * px + 2 * py + pc))
    return 4 * x + 2 * y + c, out


def _exchange(arrays, src_of, dst_of, out_shapes, name):
    n = len(arrays)

    def body(*refs):
        ins, outs, sems = refs[:n], refs[n:2 * n], refs[2 * n:]
        _exchange_start(ins, outs, sems, src_of, dst_of)
        _exchange_wait(ins, outs, sems, src_of, dst_of)

    any_spec = pl.BlockSpec(memory_space=pl.ANY)
    return pl.pallas_call(
        body, name=name,
        out_shape=[jax.ShapeDtypeStruct(s, a.dtype) for s, a in zip(out_shapes, arrays)],
        in_specs=[any_spec] * n, out_specs=[any_spec] * n,
        scratch_shapes=_exchange_sems(n),
    )(*arrays)


def _exchange_sems(n):
    return [pltpu.SemaphoreType.DMA((n, N_DEV - 1)), pltpu.SemaphoreType.DMA((n, N_DEV - 1)),
            pltpu.SemaphoreType.DMA((n,))]


def _own_copies(ins, outs, sems, src_of, dst_of):
    me, _ = _peers()
    return [pltpu.make_async_copy(src_of(j, ins[j], me), dst_of(j, outs[j], me), sems[2].at[j])
            for j in range(len(ins))]


def _remote_copies(ins, outs, sems, src_of, dst_of, receiving):
    me, peers = _peers()
    return [pltpu.make_async_remote_copy(
        src_ref=src_of(j, ins[j], pidx), dst_ref=dst_of(j, outs[j], pidx if receiving else me),
        send_sem=sems[0].at[j, k], recv_sem=sems[1].at[j, k],
        device_id=peer, device_id_type=pl.DeviceIdType.MESH)
        for k, (peer, pidx) in enumerate(peers) for j in range(len(ins))]


def _exchange_start(ins, outs, sems, src_of, dst_of):
    for cp in _own_copies(ins, outs, sems, src_of, dst_of):
        cp.start()
    for cp in _remote_copies(ins, outs, sems, src_of, dst_of, receiving=False):
        cp.start()


def _exchange_wait(ins, outs, sems, src_of, dst_of):
    for cp in _remote_copies(ins, outs, sems, src_of, dst_of, receiving=True):
        cp.wait_recv()
    for cp in _remote_copies(ins, outs, sems, src_of, dst_of, receiving=False):
        cp.wait_send()
    for cp in _own_copies(ins, outs, sems, src_of, dst_of):
        cp.wait()


def _all_gather(ffn_local, misc_local):
    w, r, d = ffn_local.shape

    def src_of(j, ref, idx):
        return ref

    def dst_of(j, ref, idx):
        return ref.at[:, idx] if j == 0 else ref.at[idx]

    return _exchange([ffn_local, misc_local], src_of, dst_of,
                     [(w, N_DEV, r, d), (N_DEV,) + misc_local.shape], "comm_all_gather")


def _grad_exchange(ffn_g, misc_g, rep_g):
    w, _, r, d = ffn_g.shape

    def src_of(j, ref, idx):
        return (ref.at[:, idx], ref.at[idx], ref)[j]

    def dst_of(j, ref, idx):
        return ref.at[idx]

    return _exchange([ffn_g, misc_g, rep_g], src_of, dst_of,
                     [(N_DEV, w, r, d), misc_g.shape, (N_DEV,) + rep_g.shape], "comm_grad_exchange")


def _carried(carry):
    if carry is None:
        return [], [], [], []
    arr, out_shape = carry
    return ([arr], [pl.BlockSpec(memory_space=pl.ANY)], [jax.ShapeDtypeStruct(out_shape, arr.dtype)],
            _exchange_sems(1))


def _gather_fns():
    return (lambda j, ref, idx: ref), (lambda j, ref, idx: ref.at[:, idx])


def _scatter_fns():
    return (lambda j, ref, idx: ref.at[:, idx]), (lambda j, ref, idx: ref.at[idx])


def _ffn_fwd(h, g, w3, next_local=None):
    T, D = h.shape
    F = w3.shape[1]
    tm, tf = min(TOKEN_TILE, T), F // 2
    nf = F // tf
    nt = T // tm
    has_c = next_local is not None
    c_in, c_specs, c_out, c_sems = _carried(
        (next_local, next_local.shape[:1] + (N_DEV,) + next_local.shape[1:]) if has_c else None)

    def body(*refs):
        h_ref, g_ref, w_ref = refs[:3]
        ho_ref, gate_ref, up_ref = refs[3 + has_c:6 + has_c]
        i = pl.program_id(0)
        if has_c:
            comm = ([refs[3]], [refs[6 + has_c]], refs[6 + 2 * has_c:], *_gather_fns())

            @pl.when(i == 0)
            def _():
                _exchange_start(*comm)

        x = h_ref[...]
        u = _rms_fwd(x, g_ref[...]).astype(BF16)
        acc = None
        for f in range(nf):
            cols = slice(f * tf, (f + 1) * tf)
            gate = _dot_nt(u, w_ref[0, cols, :])
            up = _dot_nt(u, w_ref[1, cols, :])
            gate_ref[:, cols] = gate.astype(BF16)
            up_ref[:, cols] = up.astype(BF16)
            part = _dot((gate * _sigmoid(gate) * up).astype(BF16), w_ref[2, cols, :])
            acc = part if acc is None else acc + part
        ho_ref[...] = x + 0.5 * acc

        if has_c:
            @pl.when(i == nt - 1)
            def _():
                _exchange_wait(*comm)

    row = pl.BlockSpec((tm, D), lambda i: (i, 0))
    wide = pl.BlockSpec((tm, F), lambda i: (i, 0))
    return pl.pallas_call(
        body, name="ffn_fwd_gather" if has_c else "ffn_fwd", grid=(nt,),
        in_specs=[row, _full((1, D)), _resident(w3.shape)] + c_specs,
        out_specs=[row, wide, wide] + c_specs,
        out_shape=[jax.ShapeDtypeStruct((T, D), F32), jax.ShapeDtypeStruct((T, F), BF16),
                   jax.ShapeDtypeStruct((T, F), BF16)] + c_out,
        scratch_shapes=c_sems,
        compiler_params=_cparams("arbitrary"),
    )(h, g, w3, *c_in)


def _ffn_bwd_dx(dho, h, g, gate, up, w3, grad_slab=None):
    T, D = h.shape
    F = w3.shape[1]
    tm, tf = min(TOKEN_TILE // 2, T), F // 2
    nf = F // tf
    nt = T // tm
    has_c = grad_slab is not None
    c_in, c_specs, c_out, c_sems = _carried(
        (grad_slab, (N_DEV, grad_slab.shape[0]) + grad_slab.shape[2:]) if has_c else None)

    def body(*refs):
        dho_ref, h_ref, g_ref, gate_ref, up_ref, w_ref = refs[:6]
        dhi_ref, dgate_ref, dup_ref, u_ref, dy_ref, dg_ref = refs[6 + has_c:12 + has_c]
        i = pl.program_id(0)
        if has_c:
            comm = ([refs[6]], [refs[12 + has_c]], refs[12 + 2 * has_c:], *_scatter_fns())

            @pl.when(i == 0)
            def _():
                _exchange_start(*comm)

        dho = dho_ref[...]
        x = h_ref[...]
        dy = (0.5 * dho).astype(BF16)
        dy_ref[...] = dy
        u_ref[...] = _rms_fwd(x, g_ref[...]).astype(BF16)
        acc = None
        for f in range(nf):
            cols = slice(f * tf, (f + 1) * tf)
            dact = _dot_nt(dy, w_ref[2, cols, :])
            gt = gate_ref[:, cols].astype(F32)
            sig = _sigmoid(gt)
            dup = (dact * (gt * sig)).astype(BF16)
            dgate = (dact * up_ref[:, cols].astype(F32) * (sig * (1.0 + gt * (1.0 - sig)))).astype(BF16)
            dup_ref[:, cols] = dup
            dgate_ref[:, cols] = dgate
            part = _dot(dgate, w_ref[0, cols, :]) + _dot(dup, w_ref[1, cols, :])
            acc = part if acc is None else acc + part
        dx, dg = _rms_bwd(acc, x, g_ref[...])
        dhi_ref[...] = dho + dx

        @pl.when(i == 0)
        def _():
            dg_ref[...] = jnp.zeros_like(dg_ref)

        dg_ref[...] += dg

        if has_c:
            @pl.when(i == nt - 1)
            def _():
                _exchange_wait(*comm)

    row = pl.BlockSpec((tm, D), lambda i: (i, 0))
    wide = pl.BlockSpec((tm, F), lambda i: (i, 0))
    return pl.pallas_call(
        body, name="ffn_bwd_dx_scatter" if has_c else "ffn_bwd_dx", grid=(nt,),
        in_specs=[row, row, _full((1, D)), wide, wide, _resident(w3.shape)] + c_specs,
        out_specs=[row, wide, wide, row, row, _full((1, D))] + c_specs,
        out_shape=[jax.ShapeDtypeStruct((T, D), F32), jax.ShapeDtypeStruct((T, F), BF16),
                   jax.ShapeDtypeStruct((T, F), BF16), jax.ShapeDtypeStruct((T, D), BF16),
                   jax.ShapeDtypeStruct((T, D), BF16), jax.ShapeDtypeStruct((1, D), F32)] + c_out,
        scratch_shapes=c_sems,
        compiler_params=_cparams("arbitrary"),
    )(dho, h, g, gate, up, w3, *c_in)


def _ffn_bwd_dw(dgate, dup, gate, up, u, dy):
    T, F = gate.shape
    D = u.shape[1]
    tfw = F // 2
    tk = min(TOKEN_TILE, T)
    nk = T // tk

    def body(dgate_ref, dup_ref, gate_ref, up_ref, u_ref, dy_ref, out_ref, acc_sc):
        k = pl.program_id(1)

        @pl.when(k == 0)
        def _():
            acc_sc[...] = jnp.zeros_like(acc_sc)

        uu = u_ref[...]
        gt = gate_ref[...].astype(F32)
        act = (gt * _sigmoid(gt) * up_ref[...].astype(F32)).astype(BF16)
        acc_sc[0] += _dot_tn(dgate_ref[...], uu)
        acc_sc[1] += _dot_tn(dup_ref[...], uu)
        acc_sc[2] += _dot_tn(act, dy_ref[...])

        @pl.when(k == nk - 1)
        def _():
            out_ref[...] = acc_sc[...].astype(BF16)

    blk = pl.BlockSpec((tk, tfw), lambda j, k: (k, j))
    row = pl.BlockSpec((tk, D), lambda j, k: (k, 0))
    return pl.pallas_call(
        body, name="ffn_bwd_dw", grid=(F // tfw, nk),
        in_specs=[blk, blk, blk, blk, row, row],
        out_specs=pl.BlockSpec((3, tfw, D), lambda j, k: (0, j, 0)),
        out_shape=jax.ShapeDtypeStruct((3, F, D), BF16),
        scratch_shapes=[pltpu.VMEM((3, tfw, D), F32)],
        compiler_params=_cparams("parallel", "arbitrary"),
    )(dgate, dup, gate, up, u, dy)


def _mm_rows(a, b, *, nt, out_dtype, norm_g=None, res=None, name):
    T, K = a.shape
    N = b.shape[0] if nt else b.shape[1]
    tm = min(TOKEN_TILE, T)
    has_g, has_r = norm_g is not None, res is not None

    def body(*refs):
        a_ref, b_ref = refs[0], refs[1]
        o_ref = refs[-1]
        x = a_ref[...]
        if has_g:
            x = _rms_fwd(x, refs[2][...])
        x = x.astype(BF16)
        acc = _dot_nt(x, b_ref[...]) if nt else _dot(x, b_ref[...])
        if has_r:
            acc = refs[2 + has_g][...] + acc
        o_ref[...] = acc.astype(out_dtype)

    ins, specs = [a, b], [pl.BlockSpec((tm, K), lambda i: (i, 0)), _full(b.shape)]
    if has_g:
        ins.append(norm_g)
        specs.append(_full((1, K)))
    if has_r:
        ins.append(res)
        specs.append(pl.BlockSpec((tm, N), lambda i: (i, 0)))
    return pl.pallas_call(
        body, name=name, grid=(T // tm,), in_specs=specs,
        out_specs=pl.BlockSpec((tm, N), lambda i: (i, 0)),
        out_shape=jax.ShapeDtypeStruct((T, N), out_dtype),
        compiler_params=_cparams("parallel"),
    )(*ins)


def _mm_tn(a, b, *, norm_g=None, name):
    T, M = a.shape
    N = b.shape[1]
    tk = min(TOKEN_TILE, T)
    has_g = norm_g is not None

    def body(*refs):
        a_ref, b_ref, o_ref = refs[0], refs[1], refs[-1]

        @pl.when(pl.program_id(0) == 0)
        def _():
            o_ref[...] = jnp.zeros_like(o_ref)

        x = a_ref[...]
        if has_g:
            x = _rms_fwd(x, refs[2][...])
        o_ref[...] += _dot_tn(x.astype(BF16), b_ref[...].astype(BF16))

    ins = [a, b]
    specs = [pl.BlockSpec((tk, M), lambda k: (k, 0)), pl.BlockSpec((tk, N), lambda k: (k, 0))]
    if has_g:
        ins.append(norm_g)
        specs.append(_full((1, M)))
    return pl.pallas_call(
        body, name=name, grid=(T // tk,), in_specs=specs, out_specs=_full((M, N)),
        out_shape=jax.ShapeDtypeStruct((M, N), F32),
        compiler_params=_cparams("arbitrary"),
    )(*ins)


def _proj_bwd(dz, w, h, g, dh, name):
    T, D = h.shape
    N = w.shape[1]
    tm = min(TOKEN_TILE, T)

    def body(dz_ref, w_ref, h_ref, g_ref, dh_ref, o_ref, dg_ref):
        du = _dot_nt(dz_ref[...].astype(BF16), w_ref[...])
        dx, dg = _rms_bwd(du, h_ref[...], g_ref[...])
        o_ref[...] = dh_ref[...] + dx

        @pl.when(pl.program_id(0) == 0)
        def _():
            dg_ref[...] = jnp.zeros_like(dg_ref)

        dg_ref[...] += dg

    row = pl.BlockSpec((tm, D), lambda i: (i, 0))
    return pl.pallas_call(
        body, name=name, grid=(T // tm,),
        in_specs=[pl.BlockSpec((tm, N), lambda i: (i, 0)), _full((D, N)), row, _full((1, D)), row],
        out_specs=[row, _full((1, D))],
        out_shape=[jax.ShapeDtypeStruct((T, D), F32), jax.ShapeDtypeStruct((1, D), F32)],
        compiler_params=_cparams("arbitrary"),
    )(dz, w, h, g, dh)


def _pool_bands(tm):
    r = np.arange(tm)[:, None]
    c = np.arange(tm)[None, :]
    j = np.arange(POOL_HALO)[None, :]
    main, halo, main_t, halo_t = [], [], [], []
    for w in POOL_WINDOWS:
        main.append(((r - c >= 0) & (r - c < w)) / w)
        halo.append((r + POOL_HALO - j < w) / w)
        main_t.append(((c - r >= 0) & (c - r < w)) / w)
        halo_t.append((tm + j - r < w) / w)
    return tuple(jnp.asarray(np.stack(m), BF16) for m in (main, halo, main_t, halo_t))


def _pool_count_scale(i, tm, w):
    t = i * tm + lax.broadcasted_iota(jnp.int32, (tm, 1), 0)
    return w / jnp.minimum(t + 1, w).astype(F32)


def _pool_fwd(h, g, wp, scale):
    T, D = h.shape
    G, dg = len(POOL_WINDOWS), D // len(POOL_WINDOWS)
    tm = min(TOKEN_TILE, T)
    hb = tm // POOL_HALO
    bm, bh, _, _ = _pool_bands(tm)

    def body(h_ref, hh_ref, g_ref, wp_ref, sc_ref, bm_ref, bh_ref, ho_ref, y_ref):
        i = pl.program_id(0)
        x = h_ref[...]
        u = _rms_fwd(x, g_ref[...])
        uh = _rms_fwd(hh_ref[...], g_ref[...]) * (i > 0).astype(F32)
        for gi, w in enumerate(POOL_WINDOWS):
            cols = slice(gi * dg, (gi + 1) * dg)
            ug = u[:, cols]
            hi, lo = _split_bf16(ug)
            hhi, hlo = _split_bf16(uh[:, cols])
            s = (_dot(bm_ref[gi], hi) + _dot(bm_ref[gi], lo)
                 + _dot(bh_ref[gi], hhi) + _dot(bh_ref[gi], hlo))
            y = (s * _pool_count_scale(i, tm, w) - ug).astype(BF16)
            y_ref[:, cols] = y
            ho_ref[:, cols] = x[:, cols] + _dot(y, wp_ref[gi]) * sc_ref[:, cols]

    row = pl.BlockSpec((tm, D), lambda i: (i, 0))
    return pl.pallas_call(
        body, name="pool_fwd", grid=(T // tm,),
        in_specs=[row, pl.BlockSpec((POOL_HALO, D), lambda i: (jnp.maximum(i * hb - 1, 0), 0)),
                  _full((1, D)), _full((G, dg, dg)), _full((1, D)),
                  _full((G, tm, tm)), _full((G, tm, POOL_HALO))],
        out_specs=[row, row],
        out_shape=[jax.ShapeDtypeStruct((T, D), F32), jax.ShapeDtypeStruct((T, D), BF16)],
        compiler_params=_cparams("parallel"),
    )(h, h, g, wp, scale, bm, bh)


def _pool_bwd(dh, h, g, y, wp, scale):
    T, D = h.shape
    G, dg = len(POOL_WINDOWS), D // len(POOL_WINDOWS)
    tm = min(TOKEN_TILE, T)
    hb = tm // POOL_HALO
    nt = T // tm
    _, _, bmt, bht = _pool_bands(tm)

    def body(dh_ref, dhn_ref, h_ref, g_ref, y_ref, wp_ref, sc_ref, bmt_ref, bht_ref,
             o_ref, dg_ref, dwp_ref, dsc_ref, du_sc):
        i = pl.program_id(0)

        @pl.when(i == 0)
        def _():
            dg_ref[...] = jnp.zeros_like(dg_ref)
            dwp_ref[...] = jnp.zeros_like(dwp_ref)
            dsc_ref[...] = jnp.zeros_like(dsc_ref)

        dho = dh_ref[...]
        dz = dho * sc_ref[...]
        dzn = dhn_ref[...] * sc_ref[...] * (i < nt - 1).astype(F32)
        for gi, w in enumerate(POOL_WINDOWS):
            cols = slice(gi * dg, (gi + 1) * dg)
            yg = y_ref[:, cols]
            dzg = dz[:, cols].astype(BF16)
            dsc_ref[:, cols] += jnp.sum(dho[:, cols] * _dot(yg, wp_ref[gi]), axis=0, keepdims=True)
            dwp_ref[gi] += _dot_tn(yg, dzg)
            dy = _dot_nt(dzg, wp_ref[gi])
            dyn = _dot_nt(dzn[:, cols].astype(BF16), wp_ref[gi])
            hi, lo = _split_bf16(dy * _pool_count_scale(i, tm, w))
            nhi, nlo = _split_bf16(dyn)
            du_sc[:, cols] = (_dot(bmt_ref[gi], hi) + _dot(bmt_ref[gi], lo)
                              + _dot(bht_ref[gi], nhi) + _dot(bht_ref[gi], nlo) - dy)
        dx, dgp = _rms_bwd(du_sc[...], h_ref[...], g_ref[...])
        o_ref[...] = dho + dx
        dg_ref[...] += dgp

    row = pl.BlockSpec((tm, D), lambda i: (i, 0))
    return pl.pallas_call(
        body, name="pool_bwd", grid=(nt,),
        in_specs=[row, pl.BlockSpec((POOL_HALO, D), lambda i: (jnp.minimum((i + 1) * hb, T // POOL_HALO - 1), 0)),
                  row, _full((1, D)), row, _full((G, dg, dg)), _full((1, D)),
                  _full((G, tm, tm)), _full((G, tm, POOL_HALO))],
        out_specs=[row, _full((1, D)), _full((G, dg, dg)), _full((1, D))],
        out_shape=[jax.ShapeDtypeStruct((T, D), F32), jax.ShapeDtypeStruct((1, D), F32),
                   jax.ShapeDtypeStruct((G, dg, dg), F32), jax.ShapeDtypeStruct((1, D), F32)],
        scratch_shapes=[pltpu.VMEM((tm, D), F32)],
        compiler_params=_cparams("arbitrary"),
    )(dh, dh, h, g, y, wp, scale, bmt, bht)


def _rope_tables(T):
    pos = jnp.arange(T, dtype=F32)
    inv_freq = ROPE_THETA ** (-jnp.arange(0, D_ROPE, 2, dtype=F32) / D_ROPE)
    ang = pos[:, None] * inv_freq[None, :]
    cos2 = jnp.tile(jnp.cos(ang), (1, 2))
    sin2 = jnp.tile(jnp.sin(ang), (1, 2))
    pad = jnp.zeros((T, D_HEAD_PAD - D_QK), F32)
    ca_q = jnp.concatenate([jnp.ones((T, D_NOPE), F32), cos2, pad], axis=1)
    ca_k = jnp.concatenate([jnp.zeros((T, D_NOPE), F32), cos2, pad], axis=1)
    sb = jnp.concatenate([jnp.zeros((T, D_NOPE), F32), sin2, pad], axis=1)
    return ca_q, ca_k, sb


def _rope_weight_pair(w):
    half = D_ROPE // 2
    z_pad = jnp.zeros(w.shape[:-1] + (D_HEAD_PAD - D_QK,), w.dtype)
    z_nope = jnp.zeros(w.shape[:-1] + (D_NOPE,), w.dtype)
    wa = jnp.concatenate([w, z_pad], axis=-1)
    wb = jnp.concatenate([z_nope, -w[..., D_NOPE + half:], w[..., D_NOPE:D_NOPE + half], z_pad], axis=-1)
    return wa, wb


def _rope_weight_pair_grad(dwa, dwb):
    half = D_ROPE // 2
    d1 = dwa[..., D_NOPE:D_NOPE + half] + dwb[..., D_NOPE + half:D_QK]
    d2 = dwa[..., D_NOPE + half:D_QK] - dwb[..., D_NOPE:D_NOPE + half]
    return jnp.concatenate([dwa[..., :D_NOPE], d1, d2], axis=-1)


def _q_proj(cq, qg, wa, wb, ca, sb):
    T, R = cq.shape
    tm = min(TOKEN_TILE, T)
    P = D_HEAD_PAD
    GP = HEAD_GROUP * P

    def body(cq_ref, qg_ref, wa_ref, wb_ref, ca_ref, sb_ref, q_ref):
        c = _rms_fwd(cq_ref[...], qg_ref[...]).astype(BF16)
        ca = jnp.tile(ca_ref[...], (1, HEAD_GROUP))
        sb = jnp.tile(sb_ref[...], (1, HEAD_GROUP))
        q_ref[...] = (_dot(c, wa_ref[...]) * ca + _dot(c, wb_ref[...]) * sb).astype(BF16)

    tok = pl.BlockSpec((tm, P), lambda i, hh: (i, 0))
    wsp = pl.BlockSpec((R, GP), lambda i, hh: (0, hh))
    return pl.pallas_call(
        body, name="q_proj", grid=(T // tm, N_HEADS // HEAD_GROUP),
        in_specs=[pl.BlockSpec((tm, R), lambda i, hh: (i, 0)), _full((1, R)), wsp, wsp, tok, tok],
        out_specs=pl.BlockSpec((tm, GP), lambda i, hh: (i, hh)),
        out_shape=jax.ShapeDtypeStruct((T, N_HEADS * P), BF16),
        compiler_params=_cparams("parallel", "arbitrary"),
    )(cq, qg, wa, wb, ca, sb)


def _q_proj_bwd(dq, cq, qg, wa, wb, ca, sb):
    T, R = cq.shape
    tm = min(TOKEN_TILE, T)
    P = D_HEAD_PAD

    def body(dq_ref, cq_ref, qg_ref, wa_ref, wb_ref, ca_ref, sb_ref,
             da_ref, db_ref, cqn_ref, dcq_ref, dqg_ref, acc_sc):
        i, hh = pl.program_id(0), pl.program_id(1)

        @pl.when(hh == 0)
        def _():
            acc_sc[...] = jnp.zeros_like(acc_sc)
            cqn_ref[...] = _rms_fwd(cq_ref[...], qg_ref[...]).astype(BF16)

        d = dq_ref[...].T
        da = (d * jnp.tile(ca_ref[...], (1, HEAD_GROUP))).astype(BF16)
        db = (d * jnp.tile(sb_ref[...], (1, HEAD_GROUP))).astype(BF16)
        da_ref[...] = da
        db_ref[...] = db
        acc_sc[...] += _dot_nt(da, wa_ref[...]) + _dot_nt(db, wb_ref[...])

        @pl.when(hh == N_HEADS // HEAD_GROUP - 1)
        def _():
            dx, dg = _rms_bwd(acc_sc[...], cq_ref[...], qg_ref[...])
            dcq_ref[...] = dx

            @pl.when(i == 0)
            def _():
                dqg_ref[...] = jnp.zeros_like(dqg_ref)

            dqg_ref[...] += dg

    GP = HEAD_GROUP * P
    tok = pl.BlockSpec((tm, P), lambda i, hh: (i, 0))
    hd = pl.BlockSpec((tm, GP), lambda i, hh: (i, hh))
    wsp = pl.BlockSpec((R, GP), lambda i, hh: (0, hh))
    rr = pl.BlockSpec((tm, R), lambda i, hh: (i, 0))
    return pl.pallas_call(
        body, name="q_proj_bwd", grid=(T // tm, N_HEADS // HEAD_GROUP),
        in_specs=[pl.BlockSpec((GP, tm), lambda i, hh: (hh, i)), rr, _full((1, R)), wsp, wsp, tok, tok],
        out_specs=[hd, hd, rr, rr, _full((1, R))],
        out_shape=[jax.ShapeDtypeStruct((T, N_HEADS * P), BF16), jax.ShapeDtypeStruct((T, N_HEADS * P), BF16),
                   jax.ShapeDtypeStruct((T, R), BF16), jax.ShapeDtypeStruct((T, R), F32),
                   jax.ShapeDtypeStruct((1, R), F32)],
        scratch_shapes=[pltpu.VMEM((tm, R), F32)],
        compiler_params=_cparams("arbitrary", "arbitrary"),
    )(dq, cq, qg, wa, wb, ca, sb)


def _kv_proj(h, g_in, wka, wkb, g_c, wuk, wuv, ca, sb):
    T, D = h.shape
    tm = min(TOKEN_TILE, T)
    P, C = D_HEAD_PAD, D_NOPE

    def body(h_ref, gi_ref, wka_ref, wkb_ref, gc_ref, wuk_ref, wuv_ref, ca_ref, sb_ref, k_ref, v_ref, craw_ref):
        u = _rms_fwd(h_ref[...], gi_ref[...]).astype(BF16)
        kva = _dot(u, wka_ref[...])
        kvb = _dot(u, wkb_ref[...])
        craw = kva[:, :C]
        craw_ref[...] = craw
        c = _rms_fwd(craw, gc_ref[...]).astype(BF16)
        kr = kva * ca_ref[...] + kvb * sb_ref[...]
        kn = _dot(c, wuk_ref[...])
        for hh in range(N_HEADS):
            k_ref[:, hh * P:(hh + 1) * P] = (kn[:, hh * P:(hh + 1) * P] + kr).astype(BF16)
        v_ref[...] = _dot(c, wuv_ref[...]).astype(BF16)

    tok = pl.BlockSpec((tm, P), lambda i: (i, 0))
    return pl.pallas_call(
        body, name="kv_proj", grid=(T // tm,),
        in_specs=[pl.BlockSpec((tm, D), lambda i: (i, 0)), _full((1, D)), _full((D, P)), _full((D, P)),
                  _full((1, C)), _full(wuk.shape), _full(wuv.shape), tok, tok],
        out_specs=[pl.BlockSpec((tm, N_HEADS * P), lambda i: (i, 0)),
                   pl.BlockSpec((tm, N_HEADS * D_V), lambda i: (i, 0)), pl.BlockSpec((tm, C), lambda i: (i, 0))],
        out_shape=[jax.ShapeDtypeStruct((T, N_HEADS * P), BF16), jax.ShapeDtypeStruct((T, N_HEADS * D_V), BF16),
                   jax.ShapeDtypeStruct((T, C), F32)],
        compiler_params=_cparams("parallel"),
    )(h, g_in, wka, wkb, g_c, wuk, wuv, ca, sb)


def _kv_proj_bwd(dk, dv, dh, h, g_in, wka, wkb, craw, g_c, wuk, wuv, ca, sb):
    T, D = h.shape
    tm = min(TOKEN_TILE // 2, T)
    P, C = D_HEAD_PAD, D_NOPE

    def body(dk_ref, dv_ref, dh_ref, h_ref, gi_ref, wka_ref, wkb_ref, craw_ref, gc_ref, wuk_ref, wuv_ref,
             ca_ref, sb_ref, o_ref, dgi_ref, dwka_ref, dwkb_ref, dgc_ref, dwuk_ref, dwuv_ref):
        @pl.when(pl.program_id(0) == 0)
        def _():
            for r in (dgi_ref, dwka_ref, dwkb_ref, dgc_ref, dwuk_ref, dwuv_ref):
                r[...] = jnp.zeros_like(r)

        x = h_ref[...]
        u = _rms_fwd(x, gi_ref[...]).astype(BF16)
        craw = craw_ref[...]
        c = _rms_fwd(craw, gc_ref[...]).astype(BF16)
        dkf = dk_ref[...]
        dkb = dkf.astype(BF16)
        dvb = dv_ref[...].astype(BF16)
        dwuk_ref[...] += _dot_tn(c, dkb)
        dwuv_ref[...] += _dot_tn(c, dvb)
        dc = _dot_nt(dkb, wuk_ref[...]) + _dot_nt(dvb, wuv_ref[...])
        dkr = dkf[:, :P]
        for hh in range(1, N_HEADS):
            dkr = dkr + dkf[:, hh * P:(hh + 1) * P]
        dcraw, dgc = _rms_bwd(dc, craw, gc_ref[...])
        dgc_ref[...] += dgc
        dkva = jnp.concatenate([dcraw, (dkr * ca_ref[...])[:, C:]], axis=1).astype(BF16)
        dkvb = (dkr * sb_ref[...]).astype(BF16)
        dwka_ref[...] += _dot_tn(u, dkva)
        dwkb_ref[...] += _dot_tn(u, dkvb)
        du = _dot_nt(dkva, wka_ref[...]) + _dot_nt(dkvb, wkb_ref[...])
        dx, dgi = _rms_bwd(du, x, gi_ref[...])
        dgi_ref[...] += dgi
        o_ref[...] = dh_ref[...] + dx

    row = pl.BlockSpec((tm, D), lambda i: (i, 0))
    tok = pl.BlockSpec((tm, P), lambda i: (i, 0))
    return pl.pallas_call(
        body, name="kv_proj_bwd", grid=(T // tm,),
        in_specs=[pl.BlockSpec((tm, N_HEADS * P), lambda i: (i, 0)),
                  pl.BlockSpec((tm, N_HEADS * D_V), lambda i: (i, 0)), row, row, _full((1, D)),
                  _full((D, P)), _full((D, P)), pl.BlockSpec((tm, C), lambda i: (i, 0)), _full((1, C)),
                  _full(wuk.shape), _full(wuv.shape), tok, tok],
        out_specs=[row, _full((1, D)), _full((D, P)), _full((D, P)), _full((1, C)),
                   _full(wuk.shape), _full(wuv.shape)],
        out_shape=[jax.ShapeDtypeStruct((T, D), F32), jax.ShapeDtypeStruct((1, D), F32),
                   jax.ShapeDtypeStruct((D, P), F32), jax.ShapeDtypeStruct((D, P), F32),
                   jax.ShapeDtypeStruct((1, C), F32), jax.ShapeDtypeStruct(wuk.shape, F32),
                   jax.ShapeDtypeStruct(wuv.shape, F32)],
        compiler_params=_cparams("arbitrary"),
    )(dk, dv, dh, h, g_in, wka, wkb, craw, g_c, wuk, wuv, ca, sb)


_ATTN_SCALE = D_QK ** -0.5
_LOG2_E = 1.4426950408889634
_LN_2 = 0.6931471805599453
_ATTN_SCALE_LOG2 = _ATTN_SCALE * _LOG2_E


def _causal_mask(t):
    return lax.broadcasted_iota(jnp.int32, (t, t), 1) <= lax.broadcasted_iota(jnp.int32, (t, t), 0)


def _causal_mask_t(t):
    return lax.broadcasted_iota(jnp.int32, (t, t), 0) <= lax.broadcasted_iota(jnp.int32, (t, t), 1)


def _flash_fwd(q, k, v):
    T = q.shape[0]
    t = min(ATTN_TILE, T // 2)
    tq = 2 * t
    P = D_HEAD_PAD

    def body(q_ref, k_ref, v_ref, o_ref, lse_ref, m_sc, l_sc, acc_sc, s0_sc, s1_sc, p0_sc, p1_sc, a0_sc, a1_sc):
        qi = pl.program_id(1)
        n = 2 * (qi + 1)
        s_sc, p_sc, a_sc = (s0_sc, s1_sc), (p0_sc, p1_sc), (a0_sc, a1_sc)
        m_sc[...] = jnp.full_like(m_sc, NEG_BIG)
        l_sc[...] = jnp.zeros_like(l_sc)
        acc_sc[...] = jnp.zeros_like(acc_sc)

        def rows_of(c):
            return pl.ds(pl.multiple_of(c * t, t), t)

        def scores(c, slot):
            s_sc[slot][...] = _dot_nt(k_ref[rows_of(c), :], q_ref[...])

        def softmax(slot, key_offset):
            s_t = s_sc[slot][...]
            if key_offset is not None:
                rows = lax.broadcasted_iota(jnp.int32, (t, tq), 0) + key_offset
                s_t = jnp.where(rows <= lax.broadcasted_iota(jnp.int32, (t, tq), 1), s_t, NEG_BIG)
            m_prev = m_sc[...]
            m_new = jnp.maximum(m_prev, jnp.max(s_t, axis=0, keepdims=True))
            p_t = jnp.exp2(s_t - m_new)
            alpha = jnp.exp2(m_prev - m_new)
            l_sc[...] = alpha * l_sc[...] + jnp.sum(p_t, axis=0, keepdims=True)
            m_sc[...] = m_new
            p_sc[slot][...] = p_t.astype(BF16)
            a_sc[slot][...] = alpha

        def values(c, slot):
            acc_sc[...] = a_sc[slot][...] * acc_sc[...] + _dot_tn(v_ref[rows_of(c), :], p_sc[slot][...])

        def stage(c, slot, first=False, last=False, key_offset=None):
            if not first:
                values(c - 1, 1 - slot)
            if not last:
                scores(c + 1, 1 - slot)
            softmax(slot, key_offset)

        def drain():
            stage(n - 2, 0, key_offset=0)
            stage(n - 1, 1, last=True, key_offset=t)
            values(n - 1, 1)

        scores(0, 0)

        @pl.when(qi == 0)
        def _():
            stage(0, 0, first=True, key_offset=0)
            stage(1, 1, last=True, key_offset=t)
            values(1, 1)

        @pl.when(qi > 0)
        def _():
            stage(0, 0, first=True)

            def pair(j, carry):
                stage(1 + 2 * j, 1)
                stage(2 + 2 * j, 0)
                return carry

            lax.fori_loop(0, qi - 1, pair, 0)
            stage(n - 3, 1)
            drain()

        l = l_sc[...]
        o_ref[...] = (acc_sc[...] / l).T.astype(BF16)
        lse_ref[...] = m_sc[...] + jnp.log(l) * _LOG2_E

    return pl.pallas_call(
        body, name="flash_fwd", grid=(N_HEADS, T // tq),
        in_specs=[pl.BlockSpec((tq, P), lambda hh, i: (i, hh)), pl.BlockSpec((T, P), lambda hh, i: (0, hh)),
                  pl.BlockSpec((T, D_V), lambda hh, i: (0, hh))],
        out_specs=[pl.BlockSpec((tq, D_V), lambda hh, i: (i, hh)),
                   pl.BlockSpec((None, None, 1, tq), lambda hh, i: (hh, i, 0, 0))],
        out_shape=[jax.ShapeDtypeStruct((T, N_HEADS * D_V), BF16),
                   jax.ShapeDtypeStruct((N_HEADS, T // tq, 1, tq), F32)],
        scratch_shapes=[pltpu.VMEM((1, tq), F32), pltpu.VMEM((1, tq), F32), pltpu.VMEM((D_V, tq), F32),
                        pltpu.VMEM((t, tq), F32), pltpu.VMEM((t, tq), F32), pltpu.VMEM((t, tq), BF16),
                        pltpu.VMEM((t, tq), BF16), pltpu.VMEM((1, tq), F32), pltpu.VMEM((1, tq), F32)],
        compiler_params=_cparams("parallel", "arbitrary"),
    )(q, k, v)


def _attn_delta(o, do):
    T = o.shape[0]
    t = min(ATTN_TILE, T)

    def body(o_ref, do_ref, out_ref):
        ones = jnp.ones((8, D_V), BF16)
        for hh in range(N_HEADS):
            cols = slice(hh * D_V, (hh + 1) * D_V)
            hi, lo = _split_bf16(o_ref[:, cols].astype(F32) * do_ref[:, cols].astype(F32))
            out_ref[hh] = (_dot_nt(ones, hi) + _dot_nt(ones, lo))[0:1]

    tok = pl.BlockSpec((t, N_HEADS * D_V), lambda i: (i, 0))
    return pl.pallas_call(
        body, name="attn_delta", grid=(T // t,), in_specs=[tok, tok],
        out_specs=pl.BlockSpec((N_HEADS, None, 1, t), lambda i: (0, i, 0, 0)),
        out_shape=jax.ShapeDtypeStruct((N_HEADS, T // t, 1, t), F32),
        compiler_params=_cparams("parallel"),
    )(o, do)


def _flash_bwd(q, k, v, do, lse_row, delta_row, dk_prev, dv_prev):
    T = q.shape[0]
    t = min(ATTN_TILE, T)
    nq = T // t
    P = D_HEAD_PAD

    def body(k_ref, v_ref, q_ref, do_ref, lse_ref, delta_ref, dkp_ref, dvp_ref,
             dqt_ref, dk_ref, dv_ref, dk_sc, dv_sc, dqt_sc, kt_sc,
             s0_sc, s1_sc, dp0_sc, dp1_sc, p0_sc, p1_sc, ds0_sc, ds1_sc):
        ki = pl.program_id(1)
        n = nq - ki
        s_sc, dp_sc, p_sc, ds_sc = (s0_sc, s1_sc), (dp0_sc, dp1_sc), (p0_sc, p1_sc), (ds0_sc, ds1_sc)
        kt_sc[...] = k_ref[...].astype(F32).T.astype(BF16)
        dk_sc[...] = jnp.zeros_like(dk_sc)
        dv_sc[...] = dvp_ref[...]

        @pl.when(ki == 0)
        def _():
            dqt_sc[...] = jnp.zeros_like(dqt_sc)

        def rows_of(c):
            return pl.ds(pl.multiple_of((ki + c) * t, t), t)

        def products(c, slot):
            s_sc[slot][...] = _dot_nt(k_ref[...], q_ref[rows_of(c), :])
            dp_sc[slot][...] = _dot_nt(v_ref[...], do_ref[rows_of(c), :])

        def elementwise(c, slot, diagonal):
            p_t = jnp.exp2(s_sc[slot][...] - lse_ref[ki + c])
            if diagonal:
                p_t = jnp.where(_causal_mask_t(t), p_t, 0.0)
            p_sc[slot][...] = p_t.astype(BF16)
            ds_sc[slot][...] = (p_t * (dp_sc[slot][...] - delta_ref[ki + c])).astype(BF16)

        def gradients(c, slot):
            dv_sc[...] += _dot(p_sc[slot][...], do_ref[rows_of(c), :])
            ds_t = ds_sc[slot][...]
            dk_sc[...] += _dot(ds_t, q_ref[rows_of(c), :])
            dqt_sc[ki + c] += _dot(kt_sc[...], ds_t)

        def stage(c, slot, first=False, last=False):
            if not first:
                gradients(c - 1, 1 - slot)
            if not last:
                products(c + 1, 1 - slot)
            elementwise(c, slot, diagonal=first)

        products(0, 0)

        @pl.when(n == 1)
        def _():
            elementwise(0, 0, diagonal=True)
            gradients(0, 0)

        @pl.when(n >= 2)
        def _():
            stage(0, 0, first=True)

            def pair(j, carry):
                stage(1 + 2 * j, 1)
                stage(2 + 2 * j, 0)
                return carry

            lax.fori_loop(0, (n - 2) // 2, pair, 0)

            @pl.when((n - 2) % 2 == 1)
            def _():
                stage(n - 2, 1)

            for slot in range(2):
                @pl.when((n - 1) % 2 == slot)
                def _():
                    stage(n - 1, slot, last=True)
                    gradients(n - 1, slot)

        dqt_ref[...] = dqt_sc[ki] * _ATTN_SCALE
        dk_ref[...] = dkp_ref[...] + dk_sc[...] * _LN_2
        dv_ref[...] = dv_sc[...]

    kb = pl.BlockSpec((t, P), lambda hh, i: (i, hh))
    vb = pl.BlockSpec((t, D_V), lambda hh, i: (i, hh))
    stat = pl.BlockSpec((None, nq, 1, t), lambda hh, i: (hh, 0, 0, 0))
    return pl.pallas_call(
        body, name="flash_bwd", grid=(N_HEADS, nq),
        in_specs=[kb, vb,
                  pl.BlockSpec((T, P), lambda hh, i: (0, hh), pipeline_mode=pl.Buffered(1)),
                  pl.BlockSpec((T, D_V), lambda hh, i: (0, hh), pipeline_mode=pl.Buffered(1)),
                  stat, stat, kb, vb],
        out_specs=[pl.BlockSpec((P, t), lambda hh, i: (hh, i)), kb, vb],
        out_shape=[jax.ShapeDtypeStruct((N_HEADS * P, T), F32), jax.ShapeDtypeStruct((T, N_HEADS * P), F32),
                   jax.ShapeDtypeStruct((T, N_HEADS * D_V), F32)],
        scratch_shapes=[pltpu.VMEM((t, P), F32), pltpu.VMEM((t, D_V), F32), pltpu.VMEM((nq, P, t), F32),
                        pltpu.VMEM((P, t), BF16)] + [pltpu.VMEM((t, t), F32)] * 4 + [pltpu.VMEM((t, t), BF16)] * 4,
        compiler_params=_cparams("arbitrary", "arbitrary"),
    )(k, v, q, do, lse_row, delta_row, dk_prev, dv_prev)


def _loss_head(h, g, target):
    T, D = h.shape
    tm = min(TOKEN_TILE, T)

    def body(h_ref, g_ref, t_ref, dh_ref, loss_ref, dg_ref):
        @pl.when(pl.program_id(0) == 0)
        def _():
            loss_ref[...] = jnp.zeros_like(loss_ref)
            dg_ref[...] = jnp.zeros_like(dg_ref)

        x = h_ref[...]
        err = _rms_fwd(x, g_ref[...]) - t_ref[...]
        per_tok = jnp.mean(err * err, axis=-1, keepdims=True)
        loss_ref[...] += 0.5 * jnp.sum(per_tok, axis=0, keepdims=True)
        dx, dg = _rms_bwd(err * (1.0 / D), x, g_ref[...])
        dh_ref[...] = dx
        dg_ref[...] += dg

    row = pl.BlockSpec((tm, D), lambda i: (i, 0))
    return pl.pallas_call(
        body, name="loss_head", grid=(T // tm,),
        in_specs=[row, _full((1, D)), row], out_specs=[row, _full((1, 128)), _full((1, D))],
        out_shape=[jax.ShapeDtypeStruct((T, D), F32), jax.ShapeDtypeStruct((1, 128), F32),
                   jax.ShapeDtypeStruct((1, D), F32)],
        compiler_params=_cparams("arbitrary"),
    )(h, g, target)


def _sum_parts(parts, tr, name):
    _, R, C = parts.shape

    def body(p_ref, o_ref):
        acc = p_ref[0].astype(F32)
        for j in range(1, N_DEV):
            acc = acc + p_ref[j].astype(F32)
        o_ref[...] = acc

    return pl.pallas_call(
        body, name=name, grid=(R // tr,),
        in_specs=[pl.BlockSpec((N_DEV, tr, C), lambda i: (0, i, 0))],
        out_specs=pl.BlockSpec((tr, C), lambda i: (i, 0)),
        out_shape=jax.ShapeDtypeStruct((R, C), F32),
        compiler_params=_cparams("parallel"),
    )(parts)


def _adamw(w, g, m, v):
    R, C = w.shape
    tr = _row_tile(R, TOKEN_TILE)

    def body(w_ref, g_ref, m_ref, v_ref, d_ref, mo_ref, vo_ref):
        gg = g_ref[...]
        mn = ADAM_B1 * m_ref[...] + (1.0 - ADAM_B1) * gg
        vn = ADAM_B2 * v_ref[...] + (1.0 - ADAM_B2) * (gg * gg)
        m_hat = mn / (1.0 - ADAM_B1 ** ADAM_STEP)
        v_hat = vn / (1.0 - ADAM_B2 ** ADAM_STEP)
        d_ref[...] = -ADAM_LR * (m_hat / (jnp.sqrt(v_hat) + ADAM_EPS) + ADAM_WD * w_ref[...])
        mo_ref[...] = mn
        vo_ref[...] = vn

    blk = pl.BlockSpec((tr, C), lambda i: (i, 0))
    return pl.pallas_call(
        body, name="adamw", grid=(R // tr,), in_specs=[blk] * 4, out_specs=[blk] * 3,
        out_shape=[jax.ShapeDtypeStruct((R, C), F32)] * 3,
        compiler_params=_cparams("parallel"),
    )(w, g, m, v)


def _adamw_nd(w, g, m, v):
    shape = w.shape
    two_d = (1, shape[0]) if len(shape) == 1 else (int(np.prod(shape[:-1])), shape[-1])
    outs = _adamw(w.reshape(two_d), g.reshape(two_d), m.reshape(two_d), v.reshape(two_d))
    return tuple(o.reshape(shape) for o in outs)


def _f32_as_bf16_pairs(a):
    return lax.bitcast_convert_type(a, BF16).reshape(a.shape[:-1] + (a.shape[-1] * 2,))


def _bf16_pairs_as_f32(a):
    return lax.bitcast_convert_type(a.reshape(a.shape[:-1] + (a.shape[-1] // 2, 2)), F32)


def _pack_misc(w_o, w_dq, w_uq, w_dkv, pool_w, pool_scale):
    lead = w_o.shape[:-3]
    rows = [w_o, w_dq, w_uq, w_dkv, pool_w]
    flat = [r.astype(BF16).reshape(lead + (-1, REP_COLS)) for r in rows]
    ps = _f32_as_bf16_pairs(pool_scale.astype(F32)).reshape(lead + (1, -1))
    ps = jnp.concatenate([ps, jnp.zeros(lead + (1, REP_COLS - ps.shape[-1]), BF16)], axis=-1)
    used = sum(f.shape[-2] for f in flat) + 1
    pad = jnp.zeros(lead + (MISC_ROWS - used, REP_COLS), BF16)
    return jnp.concatenate(flat + [ps, pad], axis=-2)


def _unpack_misc(buf, shapes):
    out, r0 = [], 0
    for shp in shapes[:-1]:
        n = int(np.prod(shp)) // REP_COLS
        out.append(buf[:, r0:r0 + n].reshape((N_DEV,) + shp))
        r0 += n
    n_ps = int(np.prod(shapes[-1]))
    out.append(_bf16_pairs_as_f32(buf[:, r0, :2 * n_ps]).reshape((N_DEV,) + shapes[-1]))
    return out


def _cat_dev(a, axis):
    a = jnp.moveaxis(a, 0, axis)
    return a.reshape(a.shape[:axis] + (a.shape[axis] * a.shape[axis + 1],) + a.shape[axis + 2:])


def _split_dev(a, axis):
    a = a.reshape(a.shape[:axis] + (N_DEV, a.shape[axis] // N_DEV) + a.shape[axis + 1:])
    return jnp.moveaxis(a, axis, 0)


def kernel(x, ffn_pre_norm, ffn_pre_wg, ffn_pre_wu, ffn_pre_wd, mix_norm, ffn_post_norm, ffn_post_wg, ffn_post_wu, ffn_post_wd, pool_w, pool_scale, kv_in_norm, w_dkv, ckv_norm, w_uk, w_uv, q_lora_norm, w_dq, w_uq, w_o, final_norm, loss_target, m_ffn_pre_norm, m_ffn_pre_wg, m_ffn_pre_wu, m_ffn_pre_wd, m_mix_norm, m_ffn_post_norm, m_ffn_post_wg, m_ffn_post_wu, m_ffn_post_wd, m_pool_w, m_pool_scale, m_kv_in_norm, m_w_dkv, m_ckv_norm, m_w_uk, m_w_uv, m_q_lora_norm, m_w_dq, m_w_uq, m_w_o, m_final_norm, v_ffn_pre_norm, v_ffn_pre_wg, v_ffn_pre_wu, v_ffn_pre_wd, v_mix_norm, v_ffn_post_norm, v_ffn_post_wg, v_ffn_post_wu, v_ffn_post_wd, v_pool_w, v_pool_scale, v_kv_in_norm, v_w_dkv, v_ckv_norm, v_w_uk, v_w_uv, v_q_lora_norm, v_w_dq, v_w_uq, v_w_o, v_final_norm):
    weights = dict(ffn_pre_norm=ffn_pre_norm, ffn_pre_wg=ffn_pre_wg, ffn_pre_wu=ffn_pre_wu, ffn_pre_wd=ffn_pre_wd,
                   mix_norm=mix_norm, ffn_post_norm=ffn_post_norm, ffn_post_wg=ffn_post_wg,
                   ffn_post_wu=ffn_post_wu, ffn_post_wd=ffn_post_wd, pool_w=pool_w, pool_scale=pool_scale,
                   kv_in_norm=kv_in_norm, w_dkv=w_dkv, ckv_norm=ckv_norm, w_uk=w_uk, w_uv=w_uv,
                   q_lora_norm=q_lora_norm, w_dq=w_dq, w_uq=w_uq, w_o=w_o, final_norm=final_norm)
    moments_m = dict(ffn_pre_norm=m_ffn_pre_norm, ffn_pre_wg=m_ffn_pre_wg, ffn_pre_wu=m_ffn_pre_wu,
                     ffn_pre_wd=m_ffn_pre_wd, mix_norm=m_mix_norm, ffn_post_norm=m_ffn_post_norm,
                     ffn_post_wg=m_ffn_post_wg, ffn_post_wu=m_ffn_post_wu, ffn_post_wd=m_ffn_post_wd,
                     pool_w=m_pool_w, pool_scale=m_pool_scale, kv_in_norm=m_kv_in_norm, w_dkv=m_w_dkv,
                     ckv_norm=m_ckv_norm, w_uk=m_w_uk, w_uv=m_w_uv, q_lora_norm=m_q_lora_norm, w_dq=m_w_dq,
                     w_uq=m_w_uq, w_o=m_w_o, final_norm=m_final_norm)
    moments_v = dict(ffn_pre_norm=v_ffn_pre_norm, ffn_pre_wg=v_ffn_pre_wg, ffn_pre_wu=v_ffn_pre_wu,
                     ffn_pre_wd=v_ffn_pre_wd, mix_norm=v_mix_norm, ffn_post_norm=v_ffn_post_norm,
                     ffn_post_wg=v_ffn_post_wg, ffn_post_wu=v_ffn_post_wu, ffn_post_wd=v_ffn_post_wd,
                     pool_w=v_pool_w, pool_scale=v_pool_scale, kv_in_norm=v_kv_in_norm, w_dkv=v_w_dkv,
                     ckv_norm=v_ckv_norm, w_uk=v_w_uk, w_uv=v_w_uv, q_lora_norm=v_q_lora_norm, w_dq=v_w_dq,
                     w_uq=v_w_uq, w_o=v_w_o, final_norm=v_final_norm)
    order = list(weights)

    T, D = x.shape[1], x.shape[2]
    depth = ffn_pre_norm.shape[0]
    n_a = pool_w.shape[0]
    n_b = depth - n_a
    fs = ffn_pre_wd.shape[1]
    F = fs * N_DEV
    n_ffn = 2 * depth
    t_attn = min(ATTN_TILE, T)

    ffn_local = [
        jnp.stack([jnp.swapaxes(wg[l], 0, 1), jnp.swapaxes(wu[l], 0, 1), wd[l]]).astype(BF16)
        for l in range(depth)
        for wg, wu, wd in ((ffn_pre_wg, ffn_pre_wu, ffn_pre_wd), (ffn_post_wg, ffn_post_wu, ffn_post_wd))
    ]
    misc_local = _pack_misc(w_o, w_dq, w_uq.reshape(n_b, w_uq.shape[1], -1), w_dkv, pool_w, pool_scale)
    misc_shapes = (w_o.shape, w_dq.shape, (n_b, w_uq.shape[1], N_HEADS * D_QK), w_dkv.shape, pool_w.shape,
                   pool_scale.shape)
    w0_all, misc_all = _all_gather(ffn_local[0], misc_local)
    walls = [w0_all.reshape(3, F, D)] + [None] * (n_ffn - 1)
    o_blk, dq_blk, uq_blk, dkv_blk, pw_blk, ps_blk = _unpack_misc(misc_all, misc_shapes)
    w_o_f = _cat_dev(o_blk, 1)
    w_dq_f = _cat_dev(dq_blk, 1)
    w_uq_f = _cat_dev(uq_blk, 1).reshape(n_b, -1, N_HEADS, D_QK)
    w_dkv_f = _cat_dev(dkv_blk, 0)
    pool_w_f = _cat_dev(pw_blk, 2)
    pool_scale_f = _cat_dev(ps_blk, 1)
    rq = w_dq_f.shape[2]
    wqa, wqb = _rope_weight_pair(w_uq_f)
    wqa = wqa.reshape(n_b, rq, N_HEADS * D_HEAD_PAD)
    wqb = wqb.reshape(n_b, rq, N_HEADS * D_HEAD_PAD)
    wka, wkb = _rope_weight_pair(w_dkv_f)
    wuk = jnp.concatenate([w_uk, jnp.zeros_like(w_uk)], axis=-1).astype(BF16).reshape(D_NOPE, N_HEADS * D_HEAD_PAD)
    wuv = w_uv.astype(BF16).reshape(D_NOPE, N_HEADS * D_V)
    ca_q, ca_k, sb = _rope_tables(T)
    ca_q_scaled, sb_scaled = ca_q * _ATTN_SCALE_LOG2, sb * _ATTN_SCALE_LOG2

    def vec(a):
        return a.reshape(1, -1)

    def ffn_stage(e, h_in, norm):
        nxt = ffn_local[e + 1] if e + 1 < n_ffn else None
        outs = _ffn_fwd(h_in, norm, walls[e], nxt)
        if nxt is not None:
            walls[e + 1] = outs[3].reshape(3, F, D)
        return outs[:3]

    h = x.reshape(T, D)
    saved = []
    k_all = v_all = craw = h_kv = None
    for l in range(depth):
        s = {"h0": h}
        h, s["g1"], s["u1"] = ffn_stage(2 * l, h, vec(ffn_pre_norm[l]))
        s["h1"] = h
        if l < n_a:
            h, s["y"] = _pool_fwd(h, vec(mix_norm[l]), pool_w_f[l], vec(pool_scale_f[l]))
        else:
            j = l - n_a
            s["cq"] = _mm_rows(h, w_dq_f[j], nt=False, out_dtype=F32, norm_g=vec(mix_norm[l]), name="q_down")
            s["q"] = _q_proj(s["cq"], vec(q_lora_norm[j]), wqa[j], wqb[j], ca_q_scaled, sb_scaled)
            s["o"], lse = _flash_fwd(s["q"], k_all, v_all)
            s["lse"] = lse.reshape(N_HEADS, T // t_attn, 1, t_attn)
            h = _mm_rows(s["o"], w_o_f[j], nt=False, out_dtype=F32, res=h, name="attn_out")
        s["h2"] = h
        h, s["g2"], s["u2"] = ffn_stage(2 * l + 1, h, vec(ffn_post_norm[l]))
        if l == n_a - 1:
            h_kv = h
            k_all, v_all, craw = _kv_proj(h, vec(kv_in_norm), wka, wkb, vec(ckv_norm), wuk, wuv, ca_k, sb)
        saved.append(s)

    dh, loss_part, d_final = _loss_head(h, vec(final_norm), loss_target.reshape(T, D))

    slabs, ffn_parts = [None] * n_ffn, [None] * n_ffn

    def ffn_stage_bwd(e, dh_out, h_in, norm, gate, up):
        carry = slabs[e + 1].reshape(3, N_DEV, fs, D) if e + 1 < n_ffn else None
        outs = _ffn_bwd_dx(dh_out, h_in, norm, gate, up, walls[e], carry)
        if carry is not None:
            ffn_parts[e + 1] = outs[6]
        dh_in, dgt, dup, u_b, dy_b, dnorm = outs[:6]
        slabs[e] = _ffn_bwd_dw(dgt, dup, gate, up, u_b, dy_b)
        return dh_in, dnorm

    grads = {}
    d_pre, d_post, d_mix = [None] * depth, [None] * depth, [None] * depth
    d_pool_w, d_pool_scale = [None] * n_a, [None] * n_a
    d_qln, d_wdq, d_wuq, d_wo = [None] * n_b, [None] * n_b, [None] * n_b, [None] * n_b
    dk_acc = jnp.zeros((T, N_HEADS * D_HEAD_PAD), F32)
    dv_acc = jnp.zeros((T, N_HEADS * D_V), F32)
    for l in reversed(range(depth)):
        s = saved[l]
        if l == n_a - 1:
            (dh, grads["kv_in_norm"], dwka, dwkb, grads["ckv_norm"], dwuk, dwuv) = _kv_proj_bwd(
                dk_acc, dv_acc, dh, h_kv, vec(kv_in_norm), wka, wkb, craw, vec(ckv_norm), wuk, wuv, ca_k, sb)
            grads["w_dkv"] = _rope_weight_pair_grad(dwka, dwkb)
            grads["w_uk"] = dwuk.reshape(D_NOPE, N_HEADS, D_HEAD_PAD)[..., :D_NOPE]
            grads["w_uv"] = dwuv.reshape(D_NOPE, N_HEADS, D_V)
        dh, d_post[l] = ffn_stage_bwd(2 * l + 1, dh, s["h2"], vec(ffn_post_norm[l]), s["g2"], s["u2"])
        if l < n_a:
            dh, d_mix[l], d_pool_w[l], d_pool_scale[l] = _pool_bwd(
                dh, s["h1"], vec(mix_norm[l]), s["y"], pool_w_f[l], vec(pool_scale_f[l]))
        else:
            j = l - n_a
            d_wo[j] = _mm_tn(s["o"], dh, name="attn_out_dw")
            do = _mm_rows(dh, w_o_f[j], nt=True, out_dtype=BF16, name="attn_out_dx")
            delta_row = _attn_delta(s["o"], do)
            dq_t, dk_acc, dv_acc = _flash_bwd(s["q"], k_all, v_all, do, s["lse"], delta_row, dk_acc, dv_acc)
            da, db, cqn, dcq, d_qln[j] = _q_proj_bwd(dq_t, s["cq"], vec(q_lora_norm[j]), wqa[j], wqb[j], ca_q, sb)
            dwa = _mm_tn(cqn, da, name="q_up_dw")
            dwb = _mm_tn(cqn, db, name="q_up_dw")
            d_wuq[j] = _rope_weight_pair_grad(dwa.reshape(rq, N_HEADS, D_HEAD_PAD),
                                              dwb.reshape(rq, N_HEADS, D_HEAD_PAD))
            d_wdq[j] = _mm_tn(s["h1"], dcq, norm_g=vec(mix_norm[l]), name="q_down_dw")
            dh, d_mix[l] = _proj_bwd(dcq, w_dq_f[j], s["h1"], vec(mix_norm[l]), dh, "q_down_dx")
        dh, d_pre[l] = ffn_stage_bwd(2 * l, dh, s["h0"], vec(ffn_pre_norm[l]), s["g1"], s["u1"])
    grad_x = dh.reshape(x.shape)

    rep_names = ["ffn_pre_norm", "mix_norm", "ffn_post_norm", "kv_in_norm", "ckv_norm", "q_lora_norm",
                 "final_norm", "w_uk", "w_uv"]
    grads["ffn_pre_norm"] = jnp.concatenate(d_pre, axis=0)
    grads["mix_norm"] = jnp.concatenate(d_mix, axis=0)
    grads["ffn_post_norm"] = jnp.concatenate(d_post, axis=0)
    grads["q_lora_norm"] = jnp.concatenate(d_qln, axis=0)
    grads["final_norm"] = d_final
    rep_flat = jnp.concatenate([grads[n].reshape(-1) for n in rep_names] + [loss_part[0, :1]])
    n_rep = rep_flat.shape[0]
    rep_rows = -(-n_rep // (8 * REP_COLS)) * 8
    rep_g = jnp.concatenate([rep_flat, jnp.zeros((rep_rows * REP_COLS - n_rep,), F32)]).reshape(rep_rows, REP_COLS)
    misc_g = _pack_misc(_split_dev(jnp.stack(d_wo), 1), _split_dev(jnp.stack(d_wdq), 1),
                        _split_dev(jnp.stack(d_wuq).reshape(n_b, rq, -1), 1), _split_dev(grads["w_dkv"], 0),
                        _split_dev(jnp.stack(d_pool_w), 2),
                        _split_dev(jnp.concatenate(d_pool_scale, axis=0), 1))
    ffn_parts[0], misc_parts, rep_parts = _grad_exchange(slabs[0].reshape(3, N_DEV, fs, D), misc_g, rep_g)
    ffn_sum = jnp.stack([_sum_parts(p.reshape(N_DEV, 3 * fs, D), fs, "sum_ffn").reshape(3, fs, D)
                         for p in ffn_parts])
    misc_sum_parts = _unpack_misc(misc_parts, misc_shapes)
    rep_sum = _sum_parts(rep_parts, _row_tile(rep_rows, 128), "sum_rep").reshape(-1)

    def sum_small(p):
        shp = p.shape[1:]
        two_d = (int(np.prod(shp[:-1])), shp[-1])
        return _sum_parts(p.reshape((N_DEV,) + two_d), two_d[0], "sum_misc").reshape(shp)

    g_wo, g_wdq, g_wuq, g_wdkv, g_pw, g_ps = [sum_small(p) for p in misc_sum_parts]
    grads.update(w_o=g_wo, w_dq=g_wdq, w_uq=g_wuq.reshape(w_uq.shape), w_dkv=g_wdkv, pool_w=g_pw, pool_scale=g_ps)
    for kind, (npre, npost) in enumerate((("ffn_pre_wg", "ffn_post_wg"), ("ffn_pre_wu", "ffn_post_wu"),
                                          ("ffn_pre_wd", "ffn_post_wd"))):
        pre = ffn_sum[0::2, kind]
        post = ffn_sum[1::2, kind]
        if kind < 2:
            pre, post = jnp.swapaxes(pre, 1, 2), jnp.swapaxes(post, 1, 2)
        grads[npre], grads[npost] = pre, post
    off = 0
    for n in rep_names:
        size = int(np.prod(weights[n].shape))
        grads[n] = rep_sum[off:off + size].reshape(weights[n].shape)
        off += size
    loss = rep_sum[off]

    deltas, new_m, new_v = {}, {}, {}
    for n in order:
        deltas[n], new_m[n], new_v[n] = _adamw_nd(weights[n], grads[n], moments_m[n], moments_v[n])
    return (loss, grad_x, *[grads[n] for n in order], *[deltas[n] for n in order],
            *[new_m[n] for n in order], *[new_v[n] for n in order])
```

```python
import functools

import numpy as np
import jax
import jax.numpy as jnp
from jax import lax
from jax.experimental import pallas as pl
from jax.experimental.pallas import tpu as pltpu

F32, BF16 = jnp.float32, jnp.bfloat16
N_DEV = 8
RMS_EPS = 1e-6
N_HEADS = 16
D_NOPE, D_ROPE, D_V = 128, 64, 128
D_QK = D_NOPE + D_ROPE
D_HEAD_PAD = 256
HEAD_GROUP = 4
ROPE_THETA = 10000.0
POOL_WINDOWS = (2, 4, 8, 16)
POOL_HALO = 16
ADAM_LR, ADAM_B1, ADAM_B2, ADAM_EPS, ADAM_WD, ADAM_STEP = 0.001, 0.9, 0.999, 1e-08, 0.01, 10
NEG_BIG = -1e30
V7X_VMEM_LIMIT = 56 * 1024 * 1024
TOKEN_TILE = 512
ATTN_TILE = 512
FFN_TILE = 256
MISC_ROWS = 864
REP_COLS = 1024


def _cparams(*sem):
    return pltpu.CompilerParams(dimension_semantics=sem, vmem_limit_bytes=V7X_VMEM_LIMIT)


def _dot(a, b):
    return lax.dot_general(a, b, (((1,), (0,)), ((), ())), preferred_element_type=F32)


def _dot_nt(a, b):
    return lax.dot_general(a, b, (((1,), (1,)), ((), ())), preferred_element_type=F32)


def _dot_tn(a, b):
    return lax.dot_general(a, b, (((0,), (0,)), ((), ())), preferred_element_type=F32)


def _rms_fwd(x, g):
    r = lax.rsqrt(jnp.mean(x * x, axis=-1, keepdims=True) + RMS_EPS)
    return (x * r) * g


def _rms_bwd(du, x, g):
    r = lax.rsqrt(jnp.mean(x * x, axis=-1, keepdims=True) + RMS_EPS)
    xh = x * r
    dg = jnp.sum(du * xh, axis=0, keepdims=True)
    dxh = du * g
    dx = r * (dxh - xh * jnp.mean(dxh * xh, axis=-1, keepdims=True))
    return dx, dg


def _sigmoid(x):
    return 1.0 / (1.0 + jnp.exp(-x))


def _split_bf16(x):
    hi = x.astype(BF16)
    lo = (x - hi.astype(F32)).astype(BF16)
    return hi, lo


def _full(shape):
    return pl.BlockSpec(shape, lambda *_: (0,) * len(shape))


def _resident(shape):
    return pl.BlockSpec(shape, lambda *_: (0,) * len(shape), pipeline_mode=pl.Buffered(1))


def _row_tile(rows, cap):
    for t in range(min(cap, rows) // 8 * 8, 0, -8):
        if rows % t == 0:
            return t
    return rows


def _peers():
    x, y, c = lax.axis_index("x"), lax.axis_index("y"), lax.axis_index("c")
    out = []
    for k in range(1, N_DEV):
        px = 1 - x if (k >> 2) & 1 else x
        py = 1 - y if (k >> 1) & 1 else y
        pc = 1 - c if k & 1 else c
        out.append(((px, py, pc), 4 * px + 2 * py + pc))
    return 4 * x + 2 * y + c, out


def _exchange(arrays, src_of, dst_of, out_shapes, name):
    n = len(arrays)

    def body(*refs):
        ins, outs, sems = refs[:n], refs[n:2 * n], refs[2 * n:]
        _exchange_start(ins, outs, sems, src_of, dst_of)
        _exchange_wait(ins, outs, sems, src_of, dst_of)

    any_spec = pl.BlockSpec(memory_space=pl.ANY)
    return pl.pallas_call(
        body, name=name,
        out_shape=[jax.ShapeDtypeStruct(s, a.dtype) for s, a in zip(out_shapes, arrays)],
        in_specs=[any_spec] * n, out_specs=[any_spec] * n,
        scratch_shapes=_exchange_sems(n),
    )(*arrays)


def _exchange_sems(n):
    return [pltpu.SemaphoreType.DMA((n, N_DEV - 1)), pltpu.SemaphoreType.DMA((n, N_DEV - 1)),
            pltpu.SemaphoreType.DMA((n,))]


def _own_copies(ins, outs, sems, src_of, dst_of):
    me, _ = _peers()
    return [pltpu.make_async_copy(src_of(j, ins[j], me), dst_of(j, outs[j], me), sems[2].at[j])
            for j in range(len(ins))]


def _remote_copies(ins, outs, sems, src_of, dst_of, receiving):
    me, peers = _peers()
    return [pltpu.make_async_remote_copy(
        src_ref=src_of(j, ins[j], pidx), dst_ref=dst_of(j, outs[j], pidx if receiving else me),
        send_sem=sems[0].at[j, k], recv_sem=sems[1].at[j, k],
        device_id=peer, device_id_type=pl.DeviceIdType.MESH)
        for k, (peer, pidx) in enumerate(peers) for j in range(len(ins))]


def _exchange_start(ins, outs, sems, src_of, dst_of):
    for cp in _own_copies(ins, outs, sems, src_of, dst_of):
        cp.start()
    for cp in _remote_copies(ins, outs, sems, src_of, dst_of, receiving=False):
        cp.start()


def _exchange_wait(ins, outs, sems, src_of, dst_of):
    for cp in _remote_copies(ins, outs, sems, src_of, dst_of, receiving=True):
        cp.wait_recv()
    for cp in _remote_copies(ins, outs, sems, src_of, dst_of, receiving=False):
        cp.wait_send()
    for cp in _own_copies(ins, outs, sems, src_of, dst_of):
        cp.wait()


def _all_gather(ffn_local, misc_local):
    w, r, d = ffn_local.shape

    def src_of(j, ref, idx):
        return ref

    def dst_of(j, ref, idx):
        return ref.at[:, idx] if j == 0 else ref.at[idx]

    return _exchange([ffn_local, misc_local], src_of, dst_of,
                     [(w, N_DEV, r, d), (N_DEV,) + misc_local.shape], "comm_all_gather")


def _grad_exchange(ffn_g, misc_g, rep_g):
    w, _, r, d = ffn_g.shape

    def src_of(j, ref, idx):
        return (ref.at[:, idx], ref.at[idx], ref)[j]

    def dst_of(j, ref, idx):
        return ref.at[idx]

    return _exchange([ffn_g, misc_g, rep_g], src_of, dst_of,
                     [(N_DEV, w, r, d), misc_g.shape, (N_DEV,) + rep_g.shape], "comm_grad_exchange")


def _carried(carry):
    if carry is None:
        return [], [], [], []
    arr, out_shape = carry
    return ([arr], [pl.BlockSpec(memory_space=pl.ANY)], [jax.ShapeDtypeStruct(out_shape, arr.dtype)],
            _exchange_sems(1))


def _gather_fns():
    return (lambda j, ref, idx: ref), (lambda j, ref, idx: ref.at[:, idx])


def _scatter_fns():
    return (lambda j, ref, idx: ref.at[:, idx]), (lambda j, ref, idx: ref.at[idx])


def _ffn_fwd(h, g, w3, next_local=None):
    T, D = h.shape
    F = w3.shape[1]
    tm, tf = min(TOKEN_TILE, T), F // 2
    nf = F // tf
    nt = T // tm
    has_c = next_local is not None
    c_in, c_specs, c_out, c_sems = _carried(
        (next_local, next_local.shape[:1] + (N_DEV,) + next_local.shape[1:]) if has_c else None)

    def body(*refs):
        h_ref, g_ref, w_ref = refs[:3]
        ho_ref, gate_ref, up_ref = refs[3 + has_c:6 + has_c]
        i = pl.program_id(0)
        if has_c:
            comm = ([refs[3]], [refs[6 + has_c]], refs[6 + 2 * has_c:], *_gather_fns())

            @pl.when(i == 0)
            def _():
                _exchange_start(*comm)

        x = h_ref[...]
        u = _rms_fwd(x, g_ref[...]).astype(BF16)
        acc = None
        for f in range(nf):
            cols = slice(f * tf, (f + 1) * tf)
            gate = _dot_nt(u, w_ref[0, cols, :])
            up = _dot_nt(u, w_ref[1, cols, :])
            gate_ref[:, cols] = gate.astype(BF16)
            up_ref[:, cols] = up.astype(BF16)
            part = _dot((gate * _sigmoid(gate) * up).astype(BF16), w_ref[2, cols, :])
            acc = part if acc is None else acc + part
        ho_ref[...] = x + 0.5 * acc

        if has_c:
            @pl.when(i == nt - 1)
            def _():
                _exchange_wait(*comm)

    row = pl.BlockSpec((tm, D), lambda i: (i, 0))
    wide = pl.BlockSpec((tm, F), lambda i: (i, 0))
    return pl.pallas_call(
        body, name="ffn_fwd_gather" if has_c else "ffn_fwd", grid=(nt,),
        in_specs=[row, _full((1, D)), _resident(w3.shape)] + c_specs,
        out_specs=[row, wide, wide] + c_specs,
        out_shape=[jax.ShapeDtypeStruct((T, D), F32), jax.ShapeDtypeStruct((T, F), BF16),
                   jax.ShapeDtypeStruct((T, F), BF16)] + c_out,
        scratch_shapes=c_sems,
        compiler_params=_cparams("arbitrary"),
    )(h, g, w3, *c_in)


def _ffn_bwd_dx(dho, h, g, gate, up, w3, grad_slab=None):
    T, D = h.shape
    F = w3.shape[1]
    tm, tf = min(TOKEN_TILE // 2, T), F // 2
    nf = F // tf
    nt = T // tm
    has_c = grad_slab is not None
    c_in, c_specs, c_out, c_sems = _carried(
        (grad_slab, (N_DEV, grad_slab.shape[0]) + grad_slab.shape[2:]) if has_c else None)

    def body(*refs):
        dho_ref, h_ref, g_ref, gate_ref, up_ref, w_ref = refs[:6]
        dhi_ref, dgate_ref, dup_ref, u_ref, dy_ref, dg_ref = refs[6 + has_c:12 + has_c]
        i = pl.program_id(0)
        if has_c:
            comm = ([refs[6]], [refs[12 + has_c]], refs[12 + 2 * has_c:], *_scatter_fns())

            @pl.when(i == 0)
            def _():
                _exchange_start(*comm)

        dho = dho_ref[...]
        x = h_ref[...]
        dy = (0.5 * dho).astype(BF16)
        dy_ref[...] = dy
        u_ref[...] = _rms_fwd(x, g_ref[...]).astype(BF16)
        acc = None
        for f in range(nf):
            cols = slice(f * tf, (f + 1) * tf)
            dact = _dot_nt(dy, w_ref[2, cols, :])
            gt = gate_ref[:, cols].astype(F32)
            sig = _sigmoid(gt)
            dup = (dact * (gt * sig)).astype(BF16)
            dgate = (dact * up_ref[:, cols].astype(F32) * (sig * (1.0 + gt * (1.0 - sig)))).astype(BF16)
            dup_ref[:, cols] = dup
            dgate_ref[:, cols] = dgate
            part = _dot(dgate, w_ref[0, cols, :]) + _dot(dup, w_ref[1, cols, :])
            acc = part if acc is None else acc + part
        dx, dg = _rms_bwd(acc, x, g_ref[...])
        dhi_ref[...] = dho + dx

        @pl.when(i == 0)
        def _():
            dg_ref[...] = jnp.zeros_like(dg_ref)

        dg_ref[...] += dg

        if has_c:
            @pl.when(i == nt - 1)
            def _():
                _exchange_wait(*comm)

    row = pl.BlockSpec((tm, D), lambda i: (i, 0))
    wide = pl.BlockSpec((tm, F), lambda i: (i, 0))
    return pl.pallas_call(
        body, name="ffn_bwd_dx_scatter" if has_c else "ffn_bwd_dx", grid=(nt,),
        in_specs=[row, row, _full((1, D)), wide, wide, _resident(w3.shape)] + c_specs,
        out_specs=[row, wide, wide, row, row, _full((1, D))] + c_specs,
        out_shape=[jax.ShapeDtypeStruct((T, D), F32), jax.ShapeDtypeStruct((T, F), BF16),
                   jax.ShapeDtypeStruct((T, F), BF16), jax.ShapeDtypeStruct((T, D), BF16),
                   jax.ShapeDtypeStruct((T, D), BF16), jax.ShapeDtypeStruct((1, D), F32)] + c_out,
        scratch_shapes=c_sems,
        compiler_params=_cparams("arbitrary"),
    )(dho, h, g, gate, up, w3, *c_in)


def _ffn_bwd_dw(dgate, dup, gate, up, u, dy):
    T, F = gate.shape
    D = u.shape[1]
    tfw = F // 2
    tk = min(TOKEN_TILE, T)
    nk = T // tk

    def body(dgate_ref, dup_ref, gate_ref, up_ref, u_ref, dy_ref, out_ref, acc_sc):
        k = pl.program_id(1)

        @pl.when(k == 0)
        def _():
            acc_sc[...] = jnp.zeros_like(acc_sc)

        uu = u_ref[...]
        gt = gate_ref[...].astype(F32)
        act = (gt * _sigmoid(gt) * up_ref[...].astype(F32)).astype(BF16)
        acc_sc[0] += _dot_tn(dgate_ref[...], uu)
        acc_sc[1] += _dot_tn(dup_ref[...], uu)
        acc_sc[2] += _dot_tn(act, dy_ref[...])

        @pl.when(k == nk - 1)
        def _():
            out_ref[...] = acc_sc[...].astype(BF16)

    blk = pl.BlockSpec((tk, tfw), lambda j, k: (k, j))
    row = pl.BlockSpec((tk, D), lambda j, k: (k, 0))
    return pl.pallas_call(
        body, name="ffn_bwd_dw", grid=(F // tfw, nk),
        in_specs=[blk, blk, blk, blk, row, row],
        out_specs=pl.BlockSpec((3, tfw, D), lambda j, k: (0, j, 0)),
        out_shape=jax.ShapeDtypeStruct((3, F, D), BF16),
        scratch_shapes=[pltpu.VMEM((3, tfw, D), F32)],
        compiler_params=_cparams("parallel", "arbitrary"),
    )(dgate, dup, gate, up, u, dy)


def _mm_rows(a, b, *, nt, out_dtype, norm_g=None, res=None, name):
    T, K = a.shape
    N = b.shape[0] if nt else b.shape[1]
    tm = min(TOKEN_TILE, T)
    has_g, has_r = norm_g is not None, res is not None

    def body(*refs):
        a_ref, b_ref = refs[0], refs[1]
        o_ref = refs[-1]
        x = a_ref[...]
        if has_g:
            x = _rms_fwd(x, refs[2][...])
        x = x.astype(BF16)
        acc = _dot_nt(x, b_ref[...]) if nt else _dot(x, b_ref[...])
        if has_r:
            acc = refs[2 + has_g][...] + acc
        o_ref[...] = acc.astype(out_dtype)

    ins, specs = [a, b], [pl.BlockSpec((tm, K), lambda i: (i, 0)), _full(b.shape)]
    if has_g:
        ins.append(norm_g)
        specs.append(_full((1, K)))
    if has_r:
        ins.append(res)
        specs.append(pl.BlockSpec((tm, N), lambda i: (i, 0)))
    return pl.pallas_call(
        body, name=name, grid=(T // tm,), in_specs=specs,
        out_specs=pl.BlockSpec((tm, N), lambda i: (i, 0)),
        out_shape=jax.ShapeDtypeStruct((T, N), out_dtype),
        compiler_params=_cparams("parallel"),
    )(*ins)


def _mm_tn(a, b, *, norm_g=None, name):
    T, M = a.shape
    N = b.shape[1]
    tk = min(TOKEN_TILE, T)
    has_g = norm_g is not None

    def body(*refs):
        a_ref, b_ref, o_ref = refs[0], refs[1], refs[-1]

        @pl.when(pl.program_id(0) == 0)
        def _():
            o_ref[...] = jnp.zeros_like(o_ref)

        x = a_ref[...]
        if has_g:
            x = _rms_fwd(x, refs[2][...])
        o_ref[...] += _dot_tn(x.astype(BF16), b_ref[...].astype(BF16))

    ins = [a, b]
    specs = [pl.BlockSpec((tk, M), lambda k: (k, 0)), pl.BlockSpec((tk, N), lambda k: (k, 0))]
    if has_g:
        ins.append(norm_g)
        specs.append(_full((1, M)))
    return pl.pallas_call(
        body, name=name, grid=(T // tk,), in_specs=specs, out_specs=_full((M, N)),
        out_shape=jax.ShapeDtypeStruct((M, N), F32),
        compiler_params=_cparams("arbitrary"),
    )(*ins)


def _proj_bwd(dz, w, h, g, dh, name):
    T, D = h.shape
    N = w.shape[1]
    tm = min(TOKEN_TILE, T)

    def body(dz_ref, w_ref, h_ref, g_ref, dh_ref, o_ref, dg_ref):
        du = _dot_nt(dz_ref[...].astype(BF16), w_ref[...])
        dx, dg = _rms_bwd(du, h_ref[...], g_ref[...])
        o_ref[...] = dh_ref[...] + dx

        @pl.when(pl.program_id(0) == 0)
        def _():
            dg_ref[...] = jnp.zeros_like(dg_ref)

        dg_ref[...] += dg

    row = pl.BlockSpec((tm, D), lambda i: (i, 0))
    return pl.pallas_call(
        body, name=name, grid=(T // tm,),
        in_specs=[pl.BlockSpec((tm, N), lambda i: (i, 0)), _full((D, N)), row, _full((1, D)), row],
        out_specs=[row, _full((1, D))],
        out_shape=[jax.ShapeDtypeStruct((T, D), F32), jax.ShapeDtypeStruct((1, D), F32)],
        compiler_params=_cparams("arbitrary"),
    )(dz, w, h, g, dh)


def _pool_bands(tm):
    r = np.arange(tm)[:, None]
    c = np.arange(tm)[None, :]
    j = np.arange(POOL_HALO)[None, :]
    main, halo, main_t, halo_t = [], [], [], []
    for w in POOL_WINDOWS:
        main.append(((r - c >= 0) & (r - c < w)) / w)
        halo.append((r + POOL_HALO - j < w) / w)
        main_t.append(((c - r >= 0) & (c - r < w)) / w)
        halo_t.append((tm + j - r < w) / w)
    return tuple(jnp.asarray(np.stack(m), BF16) for m in (main, halo, main_t, halo_t))


def _pool_count_scale(i, tm, w):
    t = i * tm + lax.broadcasted_iota(jnp.int32, (tm, 1), 0)
    return w / jnp.minimum(t + 1, w).astype(F32)


def _pool_fwd(h, g, wp, scale):
    T, D = h.shape
    G, dg = len(POOL_WINDOWS), D // len(POOL_WINDOWS)
    tm = min(TOKEN_TILE, T)
    hb = tm // POOL_HALO
    bm, bh, _, _ = _pool_bands(tm)

    def body(h_ref, hh_ref, g_ref, wp_ref, sc_ref, bm_ref, bh_ref, ho_ref, y_ref):
        i = pl.program_id(0)
        x = h_ref[...]
        u = _rms_fwd(x, g_ref[...])
        uh = _rms_fwd(hh_ref[...], g_ref[...]) * (i > 0).astype(F32)
        for gi, w in enumerate(POOL_WINDOWS):
            cols = slice(gi * dg, (gi + 1) * dg)
            ug = u[:, cols]
            hi, lo = _split_bf16(ug)
            hhi, hlo = _split_bf16(uh[:, cols])
            s = (_dot(bm_ref[gi], hi) + _dot(bm_ref[gi], lo)
                 + _dot(bh_ref[gi], hhi) + _dot(bh_ref[gi], hlo))
            y = (s * _pool_count_scale(i, tm, w) - ug).astype(BF16)
            y_ref[:, cols] = y
            ho_ref[:, cols] = x[:, cols] + _dot(y, wp_ref[gi]) * sc_ref[:, cols]

    row = pl.BlockSpec((tm, D), lambda i: (i, 0))
    return pl.pallas_call(
        body, name="pool_fwd", grid=(T // tm,),
        in_specs=[row, pl.BlockSpec((POOL_HALO, D), lambda i: (jnp.maximum(i * hb - 1, 0), 0)),
                  _full((1, D)), _full((G, dg, dg)), _full((1, D)),
                  _full((G, tm, tm)), _full((G, tm, POOL_HALO))],
        out_specs=[row, row],
        out_shape=[jax.ShapeDtypeStruct((T, D), F32), jax.ShapeDtypeStruct((T, D), BF16)],
        compiler_params=_cparams("parallel"),
    )(h, h, g, wp, scale, bm, bh)


def _pool_bwd(dh, h, g, y, wp, scale):
    T, D = h.shape
    G, dg = len(POOL_WINDOWS), D // len(POOL_WINDOWS)
    tm = min(TOKEN_TILE, T)
    hb = tm // POOL_HALO
    nt = T // tm
    _, _, bmt, bht = _pool_bands(tm)

    def body(dh_ref, dhn_ref, h_ref, g_ref, y_ref, wp_ref, sc_ref, bmt_ref, bht_ref,
             o_ref, dg_ref, dwp_ref, dsc_ref, du_sc):
        i = pl.program_id(0)

        @pl.when(i == 0)
        def _():
            dg_ref[...] = jnp.zeros_like(dg_ref)
            dwp_ref[...] = jnp.zeros_like(dwp_ref)
            dsc_ref[...] = jnp.zeros_like(dsc_ref)

        dho = dh_ref[...]
        dz = dho * sc_ref[...]
        dzn = dhn_ref[...] * sc_ref[...] * (i < nt - 1).astype(F32)
        for gi, w in enumerate(POOL_WINDOWS):
            cols = slice(gi * dg, (gi + 1) * dg)
            yg = y_ref[:, cols]
            dzg = dz[:, cols].astype(BF16)
            dsc_ref[:, cols] += jnp.sum(dho[:, cols] * _dot(yg, wp_ref[gi]), axis=0, keepdims=True)
            dwp_ref[gi] += _dot_tn(yg, dzg)
            dy = _dot_nt(dzg, wp_ref[gi])
            dyn = _dot_nt(dzn[:, cols].astype(BF16), wp_ref[gi])
            hi, lo = _split_bf16(dy * _pool_count_scale(i, tm, w))
            nhi, nlo = _split_bf16(dyn)
            du_sc[:, cols] = (_dot(bmt_ref[gi], hi) + _dot(bmt_ref[gi], lo)
                              + _dot(bht_ref[gi], nhi) + _dot(bht_ref[gi], nlo) - dy)
        dx, dgp = _rms_bwd(du_sc[...], h_ref[...], g_ref[...])
        o_ref[...] = dho + dx
        dg_ref[...] += dgp

    row = pl.BlockSpec((tm, D), lambda i: (i, 0))
    return pl.pallas_call(
        body, name="pool_bwd", grid=(nt,),
        in_specs=[row, pl.BlockSpec((POOL_HALO, D), lambda i: (jnp.minimum((i + 1) * hb, T // POOL_HALO - 1), 0)),
                  row, _full((1, D)), row, _full((G, dg, dg)), _full((1, D)),
                  _full((G, tm, tm)), _full((G, tm, POOL_HALO))],
        out_specs=[row, _full((1, D)), _full((G, dg, dg)), _full((1, D))],
        out_shape=[jax.ShapeDtypeStruct((T, D), F32), jax.ShapeDtypeStruct((1, D), F32),
                   jax.ShapeDtypeStruct((G, dg, dg), F32), jax.ShapeDtypeStruct((1, D), F32)],
        scratch_shapes=[pltpu.VMEM((tm, D), F32)],
        compiler_params=_cparams("arbitrary"),
    )(dh, dh, h, g, y, wp, scale, bmt, bht)


def _rope_tables(T):
    pos = jnp.arange(T, dtype=F32)
    inv_freq = ROPE_THETA ** (-jnp.arange(0, D_ROPE, 2, dtype=F32) / D_ROPE)
    ang = pos[:, None] * inv_freq[None, :]
    cos2 = jnp.tile(jnp.cos(ang), (1, 2))
    sin2 = jnp.tile(jnp.sin(ang), (1, 2))
    pad = jnp.zeros((T, D_HEAD_PAD - D_QK), F32)
    ca_q = jnp.concatenate([jnp.ones((T, D_NOPE), F32), cos2, pad], axis=1)
    ca_k = jnp.concatenate([jnp.zeros((T, D_NOPE), F32), cos2, pad], axis=1)
    sb = jnp.concatenate([jnp.zeros((T, D_NOPE), F32), sin2, pad], axis=1)
    return ca_q, ca_k, sb


def _rope_weight_pair(w):
    half = D_ROPE // 2
    z_pad = jnp.zeros(w.shape[:-1] + (D_HEAD_PAD - D_QK,), w.dtype)
    z_nope = jnp.zeros(w.shape[:-1] + (D_NOPE,), w.dtype)
    wa = jnp.concatenate([w, z_pad], axis=-1)
    wb = jnp.concatenate([z_nope, -w[..., D_NOPE + half:], w[..., D_NOPE:D_NOPE + half], z_pad], axis=-1)
    return wa, wb


def _rope_weight_pair_grad(dwa, dwb):
    half = D_ROPE // 2
    d1 = dwa[..., D_NOPE:D_NOPE + half] + dwb[..., D_NOPE + half:D_QK]
    d2 = dwa[..., D_NOPE + half:D_QK] - dwb[..., D_NOPE:D_NOPE + half]
    return jnp.concatenate([dwa[..., :D_NOPE], d1, d2], axis=-1)


def _q_proj(cq, qg, wa, wb, ca, sb):
    T, R = cq.shape
    tm = min(TOKEN_TILE, T)
    P = D_HEAD_PAD
    GP = HEAD_GROUP * P

    def body(cq_ref, qg_ref, wa_ref, wb_ref, ca_ref, sb_ref, q_ref):
        c = _rms_fwd(cq_ref[...], qg_ref[...]).astype(BF16)
        ca = jnp.tile(ca_ref[...], (1, HEAD_GROUP))
        sb = jnp.tile(sb_ref[...], (1, HEAD_GROUP))
        q_ref[...] = (_dot(c, wa_ref[...]) * ca + _dot(c, wb_ref[...]) * sb).astype(BF16)

    tok = pl.BlockSpec((tm, P), lambda i, hh: (i, 0))
    wsp = pl.BlockSpec((R, GP), lambda i, hh: (0, hh))
    return pl.pallas_call(
        body, name="q_proj", grid=(T // tm, N_HEADS // HEAD_GROUP),
        in_specs=[pl.BlockSpec((tm, R), lambda i, hh: (i, 0)), _full((1, R)), wsp, wsp, tok, tok],
        out_specs=pl.BlockSpec((tm, GP), lambda i, hh: (i, hh)),
        out_shape=jax.ShapeDtypeStruct((T, N_HEADS * P), BF16),
        compiler_params=_cparams("parallel", "arbitrary"),
    )(cq, qg, wa, wb, ca, sb)


def _q_proj_bwd(dq, cq, qg, wa, wb, ca, sb):
    T, R = cq.shape
    tm = min(TOKEN_TILE, T)
    P = D_HEAD_PAD

    def body(dq_ref, cq_ref, qg_ref, wa_ref, wb_ref, ca_ref, sb_ref,
             da_ref, db_ref, cqn_ref, dcq_ref, dqg_ref, acc_sc):
        i, hh = pl.program_id(0), pl.program_id(1)

        @pl.when(hh == 0)
        def _():
            acc_sc[...] = jnp.zeros_like(acc_sc)
            cqn_ref[...] = _rms_fwd(cq_ref[...], qg_ref[...]).astype(BF16)

        d = dq_ref[...].T
        da = (d * jnp.tile(ca_ref[...], (1, HEAD_GROUP))).astype(BF16)
        db = (d * jnp.tile(sb_ref[...], (1, HEAD_GROUP))).astype(BF16)
        da_ref[...] = da
        db_ref[...] = db
        acc_sc[...] += _dot_nt(da, wa_ref[...]) + _dot_nt(db, wb_ref[...])

        @pl.when(hh == N_HEADS // HEAD_GROUP - 1)
        def _():
            dx, dg = _rms_bwd(acc_sc[...], cq_ref[...], qg_ref[...])
            dcq_ref[...] = dx

            @pl.when(i == 0)
            def _():
                dqg_ref[...] = jnp.zeros_like(dqg_ref)

            dqg_ref[...] += dg

    GP = HEAD_GROUP * P
    tok = pl.BlockSpec((tm, P), lambda i, hh: (i, 0))
    hd = pl.BlockSpec((tm, GP), lambda i, hh: (i, hh))
    wsp = pl.BlockSpec((R, GP), lambda i, hh: (0, hh))
    rr = pl.BlockSpec((tm, R), lambda i, hh: (i, 0))
    return pl.pallas_call(
        body, name="q_proj_bwd", grid=(T // tm, N_HEADS // HEAD_GROUP),
        in_specs=[pl.BlockSpec((GP, tm), lambda i, hh: (hh, i)), rr, _full((1, R)), wsp, wsp, tok, tok],
        out_specs=[hd, hd, rr, rr, _full((1, R))],
        out_shape=[jax.ShapeDtypeStruct((T, N_HEADS * P), BF16), jax.ShapeDtypeStruct((T, N_HEADS * P), BF16),
                   jax.ShapeDtypeStruct((T, R), BF16), jax.ShapeDtypeStruct((T, R), F32),
                   jax.ShapeDtypeStruct((1, R), F32)],
        scratch_shapes=[pltpu.VMEM((tm, R), F32)],
        compiler_params=_cparams("arbitrary", "arbitrary"),
    )(dq, cq, qg, wa, wb, ca, sb)


def _kv_proj(h, g_in, wka, wkb, g_c, wuk, wuv, ca, sb):
    T, D = h.shape
    tm = min(TOKEN_TILE, T)
    P, C = D_HEAD_PAD, D_NOPE

    def body(h_ref, gi_ref, wka_ref, wkb_ref, gc_ref, wuk_ref, wuv_ref, ca_ref, sb_ref, k_ref, v_ref, craw_ref):
        u = _rms_fwd(h_ref[...], gi_ref[...]).astype(BF16)
        kva = _dot(u, wka_ref[...])
        kvb = _dot(u, wkb_ref[...])
        craw = kva[:, :C]
        craw_ref[...] = craw
        c = _rms_fwd(craw, gc_ref[...]).astype(BF16)
        kr = kva * ca_ref[...] + kvb * sb_ref[...]
        kn = _dot(c, wuk_ref[...])
        for hh in range(N_HEADS):
            k_ref[:, hh * P:(hh + 1) * P] = (kn[:, hh * P:(hh + 1) * P] + kr).astype(BF16)
        v_ref[...] = _dot(c, wuv_ref[...]).astype(BF16)

    tok = pl.BlockSpec((tm, P), lambda i: (i, 0))
    return pl.pallas_call(
        body, name="kv_proj", grid=(T // tm,),
        in_specs=[pl.BlockSpec((tm, D), lambda i: (i, 0)), _full((1, D)), _full((D, P)), _full((D, P)),
                  _full((1, C)), _full(wuk.shape), _full(wuv.shape), tok, tok],
        out_specs=[pl.BlockSpec((tm, N_HEADS * P), lambda i: (i, 0)),
                   pl.BlockSpec((tm, N_HEADS * D_V), lambda i: (i, 0)), pl.BlockSpec((tm, C), lambda i: (i, 0))],
        out_shape=[jax.ShapeDtypeStruct((T, N_HEADS * P), BF16), jax.ShapeDtypeStruct((T, N_HEADS * D_V), BF16),
                   jax.ShapeDtypeStruct((T, C), F32)],
        compiler_params=_cparams("parallel"),
    )(h, g_in, wka, wkb, g_c, wuk, wuv, ca, sb)


def _kv_proj_bwd(dks, dvs, dh, h, g_in, wka, wkb, craw, g_c, wuk, wuv, ca, sb):
    T, D = h.shape
    tm = min(TOKEN_TILE // 4, T)
    P, C = D_HEAD_PAD, D_NOPE
    nl = len(dks)

    def body(*refs):
        dk_refs, dv_refs = refs[:nl], refs[nl:2 * nl]
        (dh_ref, h_ref, gi_ref, wka_ref, wkb_ref, craw_ref, gc_ref, wuk_ref, wuv_ref,
         ca_ref, sb_ref, o_ref, dgi_ref, dwka_ref, dwkb_ref, dgc_ref, dwuk_ref, dwuv_ref) = refs[2 * nl:]

        @pl.when(pl.program_id(0) == 0)
        def _():
            for r in (dgi_ref, dwka_ref, dwkb_ref, dgc_ref, dwuk_ref, dwuv_ref):
                r[...] = jnp.zeros_like(r)

        x = h_ref[...]
        u = _rms_fwd(x, gi_ref[...]).astype(BF16)
        craw = craw_ref[...]
        c = _rms_fwd(craw, gc_ref[...]).astype(BF16)
        dkf = sum(r[...] for r in dk_refs[1:]) + dk_refs[0][...]
        dkb = dkf.astype(BF16)
        dvb = (sum(r[...] for r in dv_refs[1:]) + dv_refs[0][...]).astype(BF16)
        dwuk_ref[...] += _dot_tn(c, dkb)
        dwuv_ref[...] += _dot_tn(c, dvb)
        dc = _dot_nt(dkb, wuk_ref[...]) + _dot_nt(dvb, wuv_ref[...])
        dkr = dkf[:, :P]
        for hh in range(1, N_HEADS):
            dkr = dkr + dkf[:, hh * P:(hh + 1) * P]
        dcraw, dgc = _rms_bwd(dc, craw, gc_ref[...])
        dgc_ref[...] += dgc
        dkva = jnp.concatenate([dcraw, (dkr * ca_ref[...])[:, C:]], axis=1).astype(BF16)
        dkvb = (dkr * sb_ref[...]).astype(BF16)
        dwka_ref[...] += _dot_tn(u, dkva)
        dwkb_ref[...] += _dot_tn(u, dkvb)
        du = _dot_nt(dkva, wka_ref[...]) + _dot_nt(dkvb, wkb_ref[...])
        dx, dgi = _rms_bwd(du, x, gi_ref[...])
        dgi_ref[...] += dgi
        o_ref[...] = dh_ref[...] + dx

    row = pl.BlockSpec((tm, D), lambda i: (i, 0))
    tok = pl.BlockSpec((tm, P), lambda i: (i, 0))
    return pl.pallas_call(
        body, name="kv_proj_bwd", grid=(T // tm,),
        in_specs=[pl.BlockSpec((tm, N_HEADS * P), lambda i: (i, 0))] * nl
        + [pl.BlockSpec((tm, N_HEADS * D_V), lambda i: (i, 0))] * nl
        + [row, row, _full((1, D)), _full((D, P)), _full((D, P)), pl.BlockSpec((tm, C), lambda i: (i, 0)),
           _full((1, C)), _full(wuk.shape), _full(wuv.shape), tok, tok],
        out_specs=[row, _full((1, D)), _full((D, P)), _full((D, P)), _full((1, C)),
                   _full(wuk.shape), _full(wuv.shape)],
        out_shape=[jax.ShapeDtypeStruct((T, D), F32), jax.ShapeDtypeStruct((1, D), F32),
                   jax.ShapeDtypeStruct((D, P), F32), jax.ShapeDtypeStruct((D, P), F32),
                   jax.ShapeDtypeStruct((1, C), F32), jax.ShapeDtypeStruct(wuk.shape, F32),
                   jax.ShapeDtypeStruct(wuv.shape, F32)],
        compiler_params=_cparams("arbitrary"),
    )(*dks, *dvs, dh, h, g_in, wka, wkb, craw, g_c, wuk, wuv, ca, sb)


_ATTN_SCALE = D_QK ** -0.5
_LOG2_E = 1.4426950408889634
_LN_2 = 0.6931471805599453
_ATTN_SCALE_LOG2 = _ATTN_SCALE * _LOG2_E


def _flash_fwd(q, k, v):
    T = q.shape[0]
    t = min(ATTN_TILE, T // 2)
    tq = 2 * t
    P = D_HEAD_PAD

    def body(q_ref, k_ref, v_ref, o_ref, lse_ref, m_sc, l_sc, acc_sc, s0_sc, s1_sc, p0_sc, p1_sc, a0_sc, a1_sc):
        qi = pl.program_id(1)
        n = 2 * (qi + 1)
        s_sc, p_sc, a_sc = (s0_sc, s1_sc), (p0_sc, p1_sc), (a0_sc, a1_sc)
        m_sc[...] = jnp.full_like(m_sc, NEG_BIG)
        l_sc[...] = jnp.zeros_like(l_sc)
        acc_sc[...] = jnp.zeros_like(acc_sc)

        def rows_of(c):
            return pl.ds(pl.multiple_of(c * t, t), t)

        def scores(c, slot):
            s_sc[slot][...] = _dot_nt(k_ref[rows_of(c), :], q_ref[...])

        def softmax(slot, key_offset):
            s_t = s_sc[slot][...]
            if key_offset is not None:
                rows = lax.broadcasted_iota(jnp.int32, (t, tq), 0) + key_offset
                s_t = jnp.where(rows <= lax.broadcasted_iota(jnp.int32, (t, tq), 1), s_t, NEG_BIG)
            m_prev = m_sc[...]
            m_new = jnp.maximum(m_prev, jnp.max(s_t, axis=0, keepdims=True))
            p_t = jnp.exp2(s_t - m_new)
            alpha = jnp.exp2(m_prev - m_new)
            l_sc[...] = alpha * l_sc[...] + jnp.sum(p_t, axis=0, keepdims=True)
            m_sc[...] = m_new
            p_sc[slot][...] = p_t.astype(BF16)
            a_sc[slot][...] = alpha

        def values(c, slot):
            acc_sc[...] = a_sc[slot][...] * acc_sc[...] + _dot_tn(v_ref[rows_of(c), :], p_sc[slot][...])

        def stage(c, slot, first=False, last=False, key_offset=None):
            if not first:
                values(c - 1, 1 - slot)
            if not last:
                scores(c + 1, 1 - slot)
            softmax(slot, key_offset)

        def drain():
            stage(n - 2, 0, key_offset=0)
            stage(n - 1, 1, last=True, key_offset=t)
            values(n - 1, 1)

        scores(0, 0)

        @pl.when(qi == 0)
        def _():
            stage(0, 0, first=True, key_offset=0)
            stage(1, 1, last=True, key_offset=t)
            values(1, 1)

        @pl.when(qi > 0)
        def _():
            stage(0, 0, first=True)

            def pair(j, carry):
                stage(1 + 2 * j, 1)
                stage(2 + 2 * j, 0)
                return carry

            lax.fori_loop(0, qi - 1, pair, 0)
            stage(n - 3, 1)
            drain()

        l = l_sc[...]
        o_ref[...] = (acc_sc[...] / l).T.astype(BF16)
        lse_ref[...] = m_sc[...] + jnp.log(l) * _LOG2_E

    return pl.pallas_call(
        body, name="flash_fwd", grid=(N_HEADS, T // tq),
        in_specs=[pl.BlockSpec((tq, P), lambda hh, i: (i, hh)), pl.BlockSpec((T, P), lambda hh, i: (0, hh)),
                  pl.BlockSpec((T, D_V), lambda hh, i: (0, hh))],
        out_specs=[pl.BlockSpec((tq, D_V), lambda hh, i: (i, hh)),
                   pl.BlockSpec((None, None, 1, tq), lambda hh, i: (hh, i, 0, 0))],
        out_shape=[jax.ShapeDtypeStruct((T, N_HEADS * D_V), BF16),
                   jax.ShapeDtypeStruct((N_HEADS, T // tq, 1, tq), F32)],
        scratch_shapes=[pltpu.VMEM((1, tq), F32), pltpu.VMEM((1, tq), F32), pltpu.VMEM((D_V, tq), F32),
                        pltpu.VMEM((t, tq), F32), pltpu.VMEM((t, tq), F32), pltpu.VMEM((t, tq), BF16),
                        pltpu.VMEM((t, tq), BF16), pltpu.VMEM((1, tq), F32), pltpu.VMEM((1, tq), F32)],
        compiler_params=_cparams("parallel", "arbitrary"),
    )(q, k, v)


def _attn_delta(o, do):
    T = o.shape[0]
    t = min(ATTN_TILE, T)

    def body(o_ref, do_ref, out_ref):
        ones = jnp.ones((8, D_V), BF16)
        for hh in range(N_HEADS):
            cols = slice(hh * D_V, (hh + 1) * D_V)
            hi, lo = _split_bf16(o_ref[:, cols].astype(F32) * do_ref[:, cols].astype(F32))
            out_ref[hh] = (_dot_nt(ones, hi) + _dot_nt(ones, lo))[0:1]

    tok = pl.BlockSpec((t, N_HEADS * D_V), lambda i: (i, 0))
    return pl.pallas_call(
        body, name="attn_delta", grid=(T // t,), in_specs=[tok, tok],
        out_specs=pl.BlockSpec((N_HEADS, None, 1, t), lambda i: (0, i, 0, 0)),
        out_shape=jax.ShapeDtypeStruct((N_HEADS, T // t, 1, t), F32),
        compiler_params=_cparams("parallel"),
    )(o, do)


def _flash_bwd(q, k, v, do, lse_row, delta_row):
    T = q.shape[0]
    t = min(ATTN_TILE, T // 2)
    tk = 2 * t
    nq = T // t
    P = D_HEAD_PAD

    def body(k_ref, v_ref, q_ref, do_ref, lse_ref, delta_ref, dqt_ref, dk_ref, dv_ref, dqt_sc, kt_sc,
             s0_sc, s1_sc, dp0_sc, dp1_sc, p0_sc, p1_sc, ds0_sc, ds1_sc):
        ki = pl.program_id(1)
        q0 = 2 * ki
        n = nq - q0
        s_sc, dp_sc, p_sc, ds_sc = (s0_sc, s1_sc), (dp0_sc, dp1_sc), (p0_sc, p1_sc), (ds0_sc, ds1_sc)
        kt_sc[...] = k_ref[...].astype(F32).T.astype(BF16)
        dk_ref[...] = jnp.zeros_like(dk_ref)
        dv_ref[...] = jnp.zeros_like(dv_ref)

        @pl.when(ki == 0)
        def _():
            dqt_sc[...] = jnp.zeros_like(dqt_sc)

        def rows_of(c):
            return pl.ds(pl.multiple_of((q0 + c) * t, t), t)

        def products(c, slot):
            s_sc[slot][...] = _dot_nt(k_ref[...], q_ref[rows_of(c), :])
            dp_sc[slot][...] = _dot_nt(v_ref[...], do_ref[rows_of(c), :])

        def elementwise(c, slot, query_offset):
            p_t = jnp.exp2(s_sc[slot][...] - lse_ref[q0 + c])
            if query_offset is not None:
                cols = lax.broadcasted_iota(jnp.int32, (tk, t), 1) + query_offset
                p_t = jnp.where(lax.broadcasted_iota(jnp.int32, (tk, t), 0) <= cols, p_t, 0.0)
            p_sc[slot][...] = p_t.astype(BF16)
            ds_sc[slot][...] = (p_t * (dp_sc[slot][...] - delta_ref[q0 + c])).astype(BF16)

        def gradients(c, slot):
            dv_ref[...] += _dot(p_sc[slot][...], do_ref[rows_of(c), :])
            ds_t = ds_sc[slot][...]
            dk_ref[...] += _dot(ds_t, q_ref[rows_of(c), :])
            dqt_sc[q0 + c] += _dot(kt_sc[...], ds_t)

        def stage(c, slot, first=False, last=False, query_offset=None):
            if not first:
                gradients(c - 1, 1 - slot)
            if not last:
                products(c + 1, 1 - slot)
            elementwise(c, slot, query_offset)

        products(0, 0)

        @pl.when(n == 2)
        def _():
            stage(0, 0, first=True, query_offset=0)
            stage(1, 1, last=True, query_offset=t)
            gradients(1, 1)

        @pl.when(n > 2)
        def _():
            stage(0, 0, first=True, query_offset=0)
            stage(1, 1, query_offset=t)

            def pair(j, carry):
                stage(2 + 2 * j, 0)
                stage(3 + 2 * j, 1)
                return carry

            lax.fori_loop(0, (n - 4) // 2, pair, 0)
            stage(n - 2, 0)
            stage(n - 1, 1, last=True)
            gradients(n - 1, 1)

        dqt_ref[:, :t] = dqt_sc[q0] * _ATTN_SCALE
        dqt_ref[:, t:] = dqt_sc[q0 + 1] * _ATTN_SCALE
        dk_ref[...] = dk_ref[...] * _LN_2

    kb = pl.BlockSpec((tk, P), lambda hh, i: (i, hh))
    vb = pl.BlockSpec((tk, D_V), lambda hh, i: (i, hh))
    stat = pl.BlockSpec((None, nq, 1, t), lambda hh, i: (hh, 0, 0, 0))
    return pl.pallas_call(
        body, name="flash_bwd", grid=(N_HEADS, T // tk),
        in_specs=[kb, vb,
                  pl.BlockSpec((T, P), lambda hh, i: (0, hh), pipeline_mode=pl.Buffered(1)),
                  pl.BlockSpec((T, D_V), lambda hh, i: (0, hh), pipeline_mode=pl.Buffered(1)),
                  stat, stat],
        out_specs=[pl.BlockSpec((P, tk), lambda hh, i: (hh, i)), kb, vb],
        out_shape=[jax.ShapeDtypeStruct((N_HEADS * P, T), F32), jax.ShapeDtypeStruct((T, N_HEADS * P), F32),
                   jax.ShapeDtypeStruct((T, N_HEADS * D_V), F32)],
        scratch_shapes=[pltpu.VMEM((nq, P, t), F32), pltpu.VMEM((P, tk), BF16)]
        + [pltpu.VMEM((tk, t), F32)] * 4 + [pltpu.VMEM((tk, t), BF16)] * 4,
        compiler_params=_cparams("arbitrary", "arbitrary"),
    )(k, v, q, do, lse_row, delta_row)


def _loss_head(h, g, target):
    T, D = h.shape
    tm = min(TOKEN_TILE, T)

    def body(h_ref, g_ref, t_ref, dh_ref, loss_ref, dg_ref):
        @pl.when(pl.program_id(0) == 0)
        def _():
            loss_ref[...] = jnp.zeros_like(loss_ref)
            dg_ref[...] = jnp.zeros_like(dg_ref)

        x = h_ref[...]
        err = _rms_fwd(x, g_ref[...]) - t_ref[...]
        per_tok = jnp.mean(err * err, axis=-1, keepdims=True)
        loss_ref[...] += 0.5 * jnp.sum(per_tok, axis=0, keepdims=True)
        dx, dg = _rms_bwd(err * (1.0 / D), x, g_ref[...])
        dh_ref[...] = dx
        dg_ref[...] += dg

    row = pl.BlockSpec((tm, D), lambda i: (i, 0))
    return pl.pallas_call(
        body, name="loss_head", grid=(T // tm,),
        in_specs=[row, _full((1, D)), row], out_specs=[row, _full((1, 128)), _full((1, D))],
        out_shape=[jax.ShapeDtypeStruct((T, D), F32), jax.ShapeDtypeStruct((1, 128), F32),
                   jax.ShapeDtypeStruct((1, D), F32)],
        compiler_params=_cparams("arbitrary"),
    )(h, g, target)


def _sum_parts(parts, tr, name):
    _, R, C = parts.shape

    def body(p_ref, o_ref):
        acc = p_ref[0].astype(F32)
        for j in range(1, N_DEV):
            acc = acc + p_ref[j].astype(F32)
        o_ref[...] = acc

    return pl.pallas_call(
        body, name=name, grid=(R // tr,),
        in_specs=[pl.BlockSpec((N_DEV, tr, C), lambda i: (0, i, 0))],
        out_specs=pl.BlockSpec((tr, C), lambda i: (i, 0)),
        out_shape=jax.ShapeDtypeStruct((R, C), F32),
        compiler_params=_cparams("parallel"),
    )(parts)


def _adamw(w, g, m, v):
    R, C = w.shape
    tr = _row_tile(R, TOKEN_TILE)

    def body(w_ref, g_ref, m_ref, v_ref, d_ref, mo_ref, vo_ref):
        gg = g_ref[...]
        mn = ADAM_B1 * m_ref[...] + (1.0 - ADAM_B1) * gg
        vn = ADAM_B2 * v_ref[...] + (1.0 - ADAM_B2) * (gg * gg)
        m_hat = mn / (1.0 - ADAM_B1 ** ADAM_STEP)
        v_hat = vn / (1.0 - ADAM_B2 ** ADAM_STEP)
        d_ref[...] = -ADAM_LR * (m_hat / (jnp.sqrt(v_hat) + ADAM_EPS) + ADAM_WD * w_ref[...])
        mo_ref[...] = mn
        vo_ref[...] = vn

    blk = pl.BlockSpec((tr, C), lambda i: (i, 0))
    return pl.pallas_call(
        body, name="adamw", grid=(R // tr,), in_specs=[blk] * 4, out_specs=[blk] * 3,
        out_shape=[jax.ShapeDtypeStruct((R, C), F32)] * 3,
        compiler_params=_cparams("parallel"),
    )(w, g, m, v)


def _adamw_nd(w, g, m, v):
    shape = w.shape
    two_d = (1, shape[0]) if len(shape) == 1 else (int(np.prod(shape[:-1])), shape[-1])
    outs = _adamw(w.reshape(two_d), g.reshape(two_d), m.reshape(two_d), v.reshape(two_d))
    return tuple(o.reshape(shape) for o in outs)


def _f32_as_bf16_pairs(a):
    return lax.bitcast_convert_type(a, BF16).reshape(a.shape[:-1] + (a.shape[-1] * 2,))


def _bf16_pairs_as_f32(a):
    return lax.bitcast_convert_type(a.reshape(a.shape[:-1] + (a.shape[-1] // 2, 2)), F32)


def _pack_misc(w_o, w_dq, w_uq, w_dkv, pool_w, pool_scale):
    lead = w_o.shape[:-3]
    rows = [w_o, w_dq, w_uq, w_dkv, pool_w]
    flat = [r.astype(BF16).reshape(lead + (-1, REP_COLS)) for r in rows]
    ps = _f32_as_bf16_pairs(pool_scale.astype(F32)).reshape(lead + (1, -1))
    ps = jnp.concatenate([ps, jnp.zeros(lead + (1, REP_COLS - ps.shape[-1]), BF16)], axis=-1)
    used = sum(f.shape[-2] for f in flat) + 1
    pad = jnp.zeros(lead + (MISC_ROWS - used, REP_COLS), BF16)
    return jnp.concatenate(flat + [ps, pad], axis=-2)


def _unpack_misc(buf, shapes):
    out, r0 = [], 0
    for shp in shapes[:-1]:
        n = int(np.prod(shp)) // REP_COLS
        out.append(buf[:, r0:r0 + n].reshape((N_DEV,) + shp))
        r0 += n
    n_ps = int(np.prod(shapes[-1]))
    out.append(_bf16_pairs_as_f32(buf[:, r0, :2 * n_ps]).reshape((N_DEV,) + shapes[-1]))
    return out


def _cat_dev(a, axis):
    a = jnp.moveaxis(a, 0, axis)
    return a.reshape(a.shape[:axis] + (a.shape[axis] * a.shape[axis + 1],) + a.shape[axis + 2:])


def _split_dev(a, axis):
    a = a.reshape(a.shape[:axis] + (N_DEV, a.shape[axis] // N_DEV) + a.shape[axis + 1:])
    return jnp.moveaxis(a, axis, 0)


def kernel(x, ffn_pre_norm, ffn_pre_wg, ffn_pre_wu, ffn_pre_wd, mix_norm, ffn_post_norm, ffn_post_wg, ffn_post_wu, ffn_post_wd, pool_w, pool_scale, kv_in_norm, w_dkv, ckv_norm, w_uk, w_uv, q_lora_norm, w_dq, w_uq, w_o, final_norm, loss_target, m_ffn_pre_norm, m_ffn_pre_wg, m_ffn_pre_wu, m_ffn_pre_wd, m_mix_norm, m_ffn_post_norm, m_ffn_post_wg, m_ffn_post_wu, m_ffn_post_wd, m_pool_w, m_pool_scale, m_kv_in_norm, m_w_dkv, m_ckv_norm, m_w_uk, m_w_uv, m_q_lora_norm, m_w_dq, m_w_uq, m_w_o, m_final_norm, v_ffn_pre_norm, v_ffn_pre_wg, v_ffn_pre_wu, v_ffn_pre_wd, v_mix_norm, v_ffn_post_norm, v_ffn_post_wg, v_ffn_post_wu, v_ffn_post_wd, v_pool_w, v_pool_scale, v_kv_in_norm, v_w_dkv, v_ckv_norm, v_w_uk, v_w_uv, v_q_lora_norm, v_w_dq, v_w_uq, v_w_o, v_final_norm):
    weights = dict(ffn_pre_norm=ffn_pre_norm, ffn_pre_wg=ffn_pre_wg, ffn_pre_wu=ffn_pre_wu, ffn_pre_wd=ffn_pre_wd,
                   mix_norm=mix_norm, ffn_post_norm=ffn_post_norm, ffn_post_wg=ffn_post_wg,
                   ffn_post_wu=ffn_post_wu, ffn_post_wd=ffn_post_wd, pool_w=pool_w, pool_scale=pool_scale,
                   kv_in_norm=kv_in_norm, w_dkv=w_dkv, ckv_norm=ckv_norm, w_uk=w_uk, w_uv=w_uv,
                   q_lora_norm=q_lora_norm, w_dq=w_dq, w_uq=w_uq, w_o=w_o, final_norm=final_norm)
    moments_m = dict(ffn_pre_norm=m_ffn_pre_norm, ffn_pre_wg=m_ffn_pre_wg, ffn_pre_wu=m_ffn_pre_wu,
                     ffn_pre_wd=m_ffn_pre_wd, mix_norm=m_mix_norm, ffn_post_norm=m_ffn_post_norm,
                     ffn_post_wg=m_ffn_post_wg, ffn_post_wu=m_ffn_post_wu, ffn_post_wd=m_ffn_post_wd,
                     pool_w=m_pool_w, pool_scale=m_pool_scale, kv_in_norm=m_kv_in_norm, w_dkv=m_w_dkv,
                     ckv_norm=m_ckv_norm, w_uk=m_w_uk, w_uv=m_w_uv, q_lora_norm=m_q_lora_norm, w_dq=m_w_dq,
                     w_uq=m_w_uq, w_o=m_w_o, final_norm=m_final_norm)
    moments_v = dict(ffn_pre_norm=v_ffn_pre_norm, ffn_pre_wg=v_ffn_pre_wg, ffn_pre_wu=v_ffn_pre_wu,
                     ffn_pre_wd=v_ffn_pre_wd, mix_norm=v_mix_norm, ffn_post_norm=v_ffn_post_norm,
                     ffn_post_wg=v_ffn_post_wg, ffn_post_wu=v_ffn_post_wu, ffn_post_wd=v_ffn_post_wd,
                     pool_w=v_pool_w, pool_scale=v_pool_scale, kv_in_norm=v_kv_in_norm, w_dkv=v_w_dkv,
                     ckv_norm=v_ckv_norm, w_uk=v_w_uk, w_uv=v_w_uv, q_lora_norm=v_q_lora_norm, w_dq=v_w_dq,
                     w_uq=v_w_uq, w_o=v_w_o, final_norm=v_final_norm)
    order = list(weights)

    T, D = x.shape[1], x.shape[2]
    depth = ffn_pre_norm.shape[0]
    n_a = pool_w.shape[0]
    n_b = depth - n_a
    fs = ffn_pre_wd.shape[1]
    F = fs * N_DEV
    n_ffn = 2 * depth
    t_attn = min(ATTN_TILE, T)

    ffn_local = [
        jnp.stack([jnp.swapaxes(wg[l], 0, 1), jnp.swapaxes(wu[l], 0, 1), wd[l]]).astype(BF16)
        for l in range(depth)
        for wg, wu, wd in ((ffn_pre_wg, ffn_pre_wu, ffn_pre_wd), (ffn_post_wg, ffn_post_wu, ffn_post_wd))
    ]
    misc_local = _pack_misc(w_o, w_dq, w_uq.reshape(n_b, w_uq.shape[1], -1), w_dkv, pool_w, pool_scale)
    misc_shapes = (w_o.shape, w_dq.shape, (n_b, w_uq.shape[1], N_HEADS * D_QK), w_dkv.shape, pool_w.shape,
                   pool_scale.shape)
    w0_all, misc_all = _all_gather(ffn_local[0], misc_local)
    walls = [w0_all.reshape(3, F, D)] + [None] * (n_ffn - 1)
    o_blk, dq_blk, uq_blk, dkv_blk, pw_blk, ps_blk = _unpack_misc(misc_all, misc_shapes)
    w_o_f = _cat_dev(o_blk, 1)
    w_dq_f = _cat_dev(dq_blk, 1)
    w_uq_f = _cat_dev(uq_blk, 1).reshape(n_b, -1, N_HEADS, D_QK)
    w_dkv_f = _cat_dev(dkv_blk, 0)
    pool_w_f = _cat_dev(pw_blk, 2)
    pool_scale_f = _cat_dev(ps_blk, 1)
    rq = w_dq_f.shape[2]
    wqa, wqb = _rope_weight_pair(w_uq_f)
    wqa = wqa.reshape(n_b, rq, N_HEADS * D_HEAD_PAD)
    wqb = wqb.reshape(n_b, rq, N_HEADS * D_HEAD_PAD)
    wka, wkb = _rope_weight_pair(w_dkv_f)
    wuk = jnp.concatenate([w_uk, jnp.zeros_like(w_uk)], axis=-1).astype(BF16).reshape(D_NOPE, N_HEADS * D_HEAD_PAD)
    wuv = w_uv.astype(BF16).reshape(D_NOPE, N_HEADS * D_V)
    ca_q, ca_k, sb = _rope_tables(T)
    ca_q_scaled, sb_scaled = ca_q * _ATTN_SCALE_LOG2, sb * _ATTN_SCALE_LOG2

    def vec(a):
        return a.reshape(1, -1)

    def ffn_stage(e, h_in, norm):
        nxt = ffn_local[e + 1] if e + 1 < n_ffn else None
        outs = _ffn_fwd(h_in, norm, walls[e], nxt)
        if nxt is not None:
            walls[e + 1] = outs[3].reshape(3, F, D)
        return outs[:3]

    h = x.reshape(T, D)
    saved = []
    k_all = v_all = craw = h_kv = None
    for l in range(depth):
        s = {"h0": h}
        h, s["g1"], s["u1"] = ffn_stage(2 * l, h, vec(ffn_pre_norm[l]))
        s["h1"] = h
        if l < n_a:
            h, s["y"] = _pool_fwd(h, vec(mix_norm[l]), pool_w_f[l], vec(pool_scale_f[l]))
        else:
            j = l - n_a
            s["cq"] = _mm_rows(h, w_dq_f[j], nt=False, out_dtype=F32, norm_g=vec(mix_norm[l]), name="q_down")
            s["q"] = _q_proj(s["cq"], vec(q_lora_norm[j]), wqa[j], wqb[j], ca_q_scaled, sb_scaled)
            s["o"], lse = _flash_fwd(s["q"], k_all, v_all)
            s["lse"] = lse.reshape(N_HEADS, T // t_attn, 1, t_attn)
            h = _mm_rows(s["o"], w_o_f[j], nt=False, out_dtype=F32, res=h, name="attn_out")
        s["h2"] = h
        h, s["g2"], s["u2"] = ffn_stage(2 * l + 1, h, vec(ffn_post_norm[l]))
        if l == n_a - 1:
            h_kv = h
            k_all, v_all, craw = _kv_proj(h, vec(kv_in_norm), wka, wkb, vec(ckv_norm), wuk, wuv, ca_k, sb)
        saved.append(s)

    dh, loss_part, d_final = _loss_head(h, vec(final_norm), loss_target.reshape(T, D))

    slabs, ffn_parts = [None] * n_ffn, [None] * n_ffn

    def ffn_stage_bwd(e, dh_out, h_in, norm, gate, up):
        carry = slabs[e + 1].reshape(3, N_DEV, fs, D) if e + 1 < n_ffn else None
        outs = _ffn_bwd_dx(dh_out, h_in, norm, gate, up, walls[e], carry)
        if carry is not None:
            ffn_parts[e + 1] = outs[6]
        dh_in, dgt, dup, u_b, dy_b, dnorm = outs[:6]
        slabs[e] = _ffn_bwd_dw(dgt, dup, gate, up, u_b, dy_b)
        return dh_in, dnorm

    grads = {}
    d_pre, d_post, d_mix = [None] * depth, [None] * depth, [None] * depth
    d_pool_w, d_pool_scale = [None] * n_a, [None] * n_a
    d_qln, d_wdq, d_wuq, d_wo = [None] * n_b, [None] * n_b, [None] * n_b, [None] * n_b
    dks, dvs = [], []
    for l in reversed(range(depth)):
        s = saved[l]
        if l == n_a - 1:
            (dh, grads["kv_in_norm"], dwka, dwkb, grads["ckv_norm"], dwuk, dwuv) = _kv_proj_bwd(
                dks, dvs, dh, h_kv, vec(kv_in_norm), wka, wkb, craw, vec(ckv_norm), wuk, wuv, ca_k, sb)
            grads["w_dkv"] = _rope_weight_pair_grad(dwka, dwkb)
            grads["w_uk"] = dwuk.reshape(D_NOPE, N_HEADS, D_HEAD_PAD)[..., :D_NOPE]
            grads["w_uv"] = dwuv.reshape(D_NOPE, N_HEADS, D_V)
        dh, d_post[l] = ffn_stage_bwd(2 * l + 1, dh, s["h2"], vec(ffn_post_norm[l]), s["g2"], s["u2"])
        if l < n_a:
            dh, d_mix[l], d_pool_w[l], d_pool_scale[l] = _pool_bwd(
                dh, s["h1"], vec(mix_norm[l]), s["y"], pool_w_f[l], vec(pool_scale_f[l]))
        else:
            j = l - n_a
            d_wo[j] = _mm_tn(s["o"], dh, name="attn_out_dw")
            do = _mm_rows(dh, w_o_f[j], nt=True, out_dtype=BF16, name="attn_out_dx")
            delta_row = _attn_delta(s["o"], do)
            dq_t, dk_l, dv_l = _flash_bwd(s["q"], k_all, v_all, do, s["lse"], delta_row)
            dks.append(dk_l)
            dvs.append(dv_l)
            da, db, cqn, dcq, d_qln[j] = _q_proj_bwd(dq_t, s["cq"], vec(q_lora_norm[j]), wqa[j], wqb[j], ca_q, sb)
            dwa = _mm_tn(cqn, da, name="q_up_dw")
            dwb = _mm_tn(cqn, db, name="q_up_dw")
            d_wuq[j] = _rope_weight_pair_grad(dwa.reshape(rq, N_HEADS, D_HEAD_PAD),
                                              dwb.reshape(rq, N_HEADS, D_HEAD_PAD))
            d_wdq[j] = _mm_tn(s["h1"], dcq, norm_g=vec(mix_norm[l]), name="q_down_dw")
            dh, d_mix[l] = _proj_bwd(dcq, w_dq_f[j], s["h1"], vec(mix_norm[l]), dh, "q_down_dx")
        dh, d_pre[l] = ffn_stage_bwd(2 * l, dh, s["h0"], vec(ffn_pre_norm[l]), s["g1"], s["u1"])
    grad_x = dh.reshape(x.shape)

    rep_names = ["ffn_pre_norm", "mix_norm", "ffn_post_norm", "kv_in_norm", "ckv_norm", "q_lora_norm",
                 "final_norm", "w_uk", "w_uv"]
    grads["ffn_pre_norm"] = jnp.concatenate(d_pre, axis=0)
    grads["mix_norm"] = jnp.concatenate(d_mix, axis=0)
    grads["ffn_post_norm"] = jnp.concatenate(d_post, axis=0)
    grads["q_lora_norm"] = jnp.concatenate(d_qln, axis=0)
    grads["final_norm"] = d_final
    rep_flat = jnp.concatenate([grads[n].reshape(-1) for n in rep_names] + [loss_part[0, :1]])
    n_rep = rep_flat.shape[0]
    rep_rows = -(-n_rep // (8 * REP_COLS)) * 8
    rep_g = jnp.concatenate([rep_flat, jnp.zeros((rep_rows * REP_COLS - n_rep,), F32)]).reshape(rep_rows, REP_COLS)
    misc_g = _pack_misc(_split_dev(jnp.stack(d_wo), 1), _split_dev(jnp.stack(d_wdq), 1),
                        _split_dev(jnp.stack(d_wuq).reshape(n_b, rq, -1), 1), _split_dev(grads["w_dkv"], 0),
                        _split_dev(jnp.stack(d_pool_w), 2),
                        _split_dev(jnp.concatenate(d_pool_scale, axis=0), 1))
    ffn_parts[0], misc_parts, rep_parts = _grad_exchange(slabs[0].reshape(3, N_DEV, fs, D), misc_g, rep_g)
    ffn_sum = jnp.stack([_sum_parts(p.reshape(N_DEV, 3 * fs, D), fs, "sum_ffn").reshape(3, fs, D)
                         for p in ffn_parts])
    misc_sum_parts = _unpack_misc(misc_parts, misc_shapes)
    rep_sum = _sum_parts(rep_parts, _row_tile(rep_rows, 128), "sum_rep").reshape(-1)

    def sum_small(p):
        shp = p.shape[1:]
        two_d = (int(np.prod(shp[:-1])), shp[-1])
        return _sum_parts(p.reshape((N_DEV,) + two_d), two_d[0], "sum_misc").reshape(shp)

    g_wo, g_wdq, g_wuq, g_wdkv, g_pw, g_ps = [sum_small(p) for p in misc_sum_parts]
    grads.update(w_o=g_wo, w_dq=g_wdq, w_uq=g_wuq.reshape(w_uq.shape), w_dkv=g_wdkv, pool_w=g_pw, pool_scale=g_ps)
    for kind, (npre, npost) in enumerate((("ffn_pre_wg", "ffn_post_wg"), ("ffn_pre_wu", "ffn_post_wu"),
                                          ("ffn_pre_wd", "ffn_post_wd"))):
        pre = ffn_sum[0::2, kind]
        post = ffn_sum[1::2, kind]
        if kind < 2:
            pre, post = jnp.swapaxes(pre, 1, 2), jnp.swapaxes(post, 1, 2)
        grads[npre], grads[npost] = pre, post
    off = 0
    for n in rep_names:
        size = int(np.prod(weights[n].shape))
        grads[n] = rep_sum[off:off + size].reshape(weights[n].shape)
        off += size
    loss = rep_sum[off]

    deltas, new_m, new_v = {}, {}, {}
    for n in order:
        deltas[n], new_m[n], new_v[n] = _adamw_nd(weights[n], grads[n], moments_m[n], moments_v[n])
    return (loss, grad_x, *[grads[n] for n in order], *[deltas[n] for n in order],
            *[new_m[n] for n in order], *[new_v[n] for n in order])
```

```python
import functools

import numpy as np
import jax
import jax.numpy as jnp
from jax import lax
from jax.experimental import pallas as pl
from jax.experimental.pallas import tpu as pltpu

F32, BF16 = jnp.float32, jnp.bfloat16
N_DEV = 8
RMS_EPS = 1e-6
N_HEADS = 16
D_NOPE, D_ROPE, D_V = 128, 64, 128
D_QK = D_NOPE + D_ROPE
D_HEAD_PAD = 256
HEAD_GROUP = 4
ROPE_THETA = 10000.0
POOL_WINDOWS = (2, 4, 8, 16)
POOL_HALO = 16
ADAM_LR, ADAM_B1, ADAM_B2, ADAM_EPS, ADAM_WD, ADAM_STEP = 0.001, 0.9, 0.999, 1e-08, 0.01, 10
NEG_BIG = -1e30
V7X_VMEM_LIMIT = 56 * 1024 * 1024
TOKEN_TILE = 512
ATTN_TILE = 512
FFN_TILE = 256
MISC_ROWS = 864
REP_COLS = 1024


def _cparams(*sem):
    return pltpu.CompilerParams(dimension_semantics=sem, vmem_limit_bytes=V7X_VMEM_LIMIT)


def _dot(a, b):
    return lax.dot_general(a, b, (((1,), (0,)), ((), ())), preferred_element_type=F32)


def _dot_nt(a, b):
    return lax.dot_general(a, b, (((1,), (1,)), ((), ())), preferred_element_type=F32)


def _dot_tn(a, b):
    return lax.dot_general(a, b, (((0,), (0,)), ((), ())), preferred_element_type=F32)


def _rms_fwd(x, g):
    r = lax.rsqrt(jnp.mean(x * x, axis=-1, keepdims=True) + RMS_EPS)
    return (x * r) * g


def _rms_bwd(du, x, g):
    r = lax.rsqrt(jnp.mean(x * x, axis=-1, keepdims=True) + RMS_EPS)
    xh = x * r
    dg = jnp.sum(du * xh, axis=0, keepdims=True)
    dxh = du * g
    dx = r * (dxh - xh * jnp.mean(dxh * xh, axis=-1, keepdims=True))
    return dx, dg


def _sigmoid(x):
    return 1.0 / (1.0 + jnp.exp(-x))


def _split_bf16(x):
    hi = x.astype(BF16)
    lo = (x - hi.astype(F32)).astype(BF16)
    return hi, lo


def _full(shape):
    return pl.BlockSpec(shape, lambda *_: (0,) * len(shape))


def _resident(shape):
    return pl.BlockSpec(shape, lambda *_: (0,) * len(shape), pipeline_mode=pl.Buffered(1))


def _row_tile(rows, cap):
    for t in range(min(cap, rows) // 8 * 8, 0, -8):
        if rows % t == 0:
            return t
    return rows


def _peers():
    x, y, c = lax.axis_index("x"), lax.axis_index("y"), lax.axis_index("c")
    out = []
    for k in range(1, N_DEV):
        px = 1 - x if (k >> 2) & 1 else x
        py = 1 - y if (k >> 1) & 1 else y
        pc = 1 - c if k & 1 else c
        out.append(((px, py, pc), 4 * px + 2 * py + pc))
    return 4 * x + 2 * y + c, out


_ROUTES = {
    "gather_mid": (lambda ref, idx: ref, lambda ref, idx: ref.at[:, idx], lambda s: s[:1] + (N_DEV,) + s[1:]),
    "to_all": (lambda ref, idx: ref, lambda ref, idx: ref.at[idx], lambda s: (N_DEV,) + s),
    "scatter_mid": (lambda ref, idx: ref.at[:, idx], lambda ref, idx: ref.at[idx],
                    lambda s: (N_DEV, s[0]) + s[2:]),
    "scatter_lead": (lambda ref, idx: ref.at[idx], lambda ref, idx: ref.at[idx], lambda s: s),
}


def _route_fns(items):
    kinds = [kind for _, kind in items]
    return (lambda j, ref, idx: _ROUTES[kinds[j]][0](ref, idx)), (lambda j, ref, idx: _ROUTES[kinds[j]][1](ref, idx))


def _route_out_shapes(items):
    return [jax.ShapeDtypeStruct(_ROUTES[kind][2](arr.shape), arr.dtype) for arr, kind in items]


def _exchange(items, name):
    n = len(items)
    fns = _route_fns(items)

    def body(*refs):
        ins, outs, sems = refs[:n], refs[n:2 * n], refs[2 * n:]
        _exchange_start(ins, outs, sems, *fns)
        _exchange_wait(ins, outs, sems, *fns)

    any_spec = pl.BlockSpec(memory_space=pl.ANY)
    return pl.pallas_call(
        body, name=name, out_shape=_route_out_shapes(items),
        in_specs=[any_spec] * n, out_specs=[any_spec] * n,
        scratch_shapes=_exchange_sems(n),
    )(*[arr for arr, _ in items])


def _exchange_sems(n):
    return [pltpu.SemaphoreType.DMA((n, N_DEV - 1)), pltpu.SemaphoreType.DMA((n, N_DEV - 1)),
            pltpu.SemaphoreType.DMA((n,))]


def _own_copies(ins, outs, sems, src_of, dst_of):
    me, _ = _peers()
    return [pltpu.make_async_copy(src_of(j, ins[j], me), dst_of(j, outs[j], me), sems[2].at[j])
            for j in range(len(ins))]


def _remote_copies(ins, outs, sems, src_of, dst_of, receiving):
    me, peers = _peers()
    return [pltpu.make_async_remote_copy(
        src_ref=src_of(j, ins[j], pidx), dst_ref=dst_of(j, outs[j], pidx if receiving else me),
        send_sem=sems[0].at[j, k], recv_sem=sems[1].at[j, k],
        device_id=peer, device_id_type=pl.DeviceIdType.MESH)
        for k, (peer, pidx) in enumerate(peers) for j in range(len(ins))]


def _exchange_start(ins, outs, sems, src_of, dst_of):
    for cp in _own_copies(ins, outs, sems, src_of, dst_of):
        cp.start()
    for cp in _remote_copies(ins, outs, sems, src_of, dst_of, receiving=False):
        cp.start()


def _exchange_wait(ins, outs, sems, src_of, dst_of):
    for cp in _remote_copies(ins, outs, sems, src_of, dst_of, receiving=True):
        cp.wait_recv()
    for cp in _remote_copies(ins, outs, sems, src_of, dst_of, receiving=False):
        cp.wait_send()
    for cp in _own_copies(ins, outs, sems, src_of, dst_of):
        cp.wait()


def _carried(items):
    if not items:
        return [], [], [], []
    return ([arr for arr, _ in items], [pl.BlockSpec(memory_space=pl.ANY)] * len(items), _route_out_shapes(items),
            _exchange_sems(len(items)))


def _ffn_fwd(h, g, w3, carry=()):
    T, D = h.shape
    F = w3.shape[1]
    tm, tf = min(TOKEN_TILE, T), F // 2
    nf = F // tf
    nt = T // tm
    nc = len(carry)
    has_c = nc > 0
    c_in, c_specs, c_out, c_sems = _carried(carry)

    def body(*refs):
        h_ref, g_ref, w_ref = refs[:3]
        ho_ref, gate_ref, up_ref = refs[3 + nc:6 + nc]
        i = pl.program_id(0)
        if has_c:
            comm = (refs[3:3 + nc], refs[6 + nc:6 + 2 * nc], refs[6 + 2 * nc:], *_route_fns(carry))

            @pl.when(i == 0)
            def _():
                _exchange_start(*comm)

        x = h_ref[...]
        u = _rms_fwd(x, g_ref[...]).astype(BF16)
        acc = None
        for f in range(nf):
            cols = slice(f * tf, (f + 1) * tf)
            gate = _dot_nt(u, w_ref[0, cols, :])
            up = _dot_nt(u, w_ref[1, cols, :])
            gate_ref[:, cols] = gate.astype(BF16)
            up_ref[:, cols] = up.astype(BF16)
            part = _dot((gate * _sigmoid(gate) * up).astype(BF16), w_ref[2, cols, :])
            acc = part if acc is None else acc + part
        ho_ref[...] = x + 0.5 * acc

        if has_c:
            @pl.when(i == nt - 1)
            def _():
                _exchange_wait(*comm)

    row = pl.BlockSpec((tm, D), lambda i: (i, 0))
    wide = pl.BlockSpec((tm, F), lambda i: (i, 0))
    return pl.pallas_call(
        body, name="ffn_fwd_gather" if has_c else "ffn_fwd", grid=(nt,),
        in_specs=[row, _full((1, D)), _resident(w3.shape)] + c_specs,
        out_specs=[row, wide, wide] + c_specs,
        out_shape=[jax.ShapeDtypeStruct((T, D), F32), jax.ShapeDtypeStruct((T, F), BF16),
                   jax.ShapeDtypeStruct((T, F), BF16)] + c_out,
        scratch_shapes=c_sems,
        compiler_params=_cparams("arbitrary"),
    )(h, g, w3, *c_in)


def _ffn_bwd_dx(dho, h, g, gate, up, w3, carry=()):
    T, D = h.shape
    F = w3.shape[1]
    tm, tf = min(TOKEN_TILE // 2, T), F // 2
    nf = F // tf
    nt = T // tm
    nc = len(carry)
    has_c = nc > 0
    c_in, c_specs, c_out, c_sems = _carried(carry)

    def body(*refs):
        dho_ref, h_ref, g_ref, gate_ref, up_ref, w_ref = refs[:6]
        dhi_ref, dgate_ref, dup_ref, u_ref, dy_ref, dg_ref = refs[6 + nc:12 + nc]
        i = pl.program_id(0)
        if has_c:
            comm = (refs[6:6 + nc], refs[12 + nc:12 + 2 * nc], refs[12 + 2 * nc:], *_route_fns(carry))

            @pl.when(i == 0)
            def _():
                _exchange_start(*comm)

        dho = dho_ref[...]
        x = h_ref[...]
        dy = (0.5 * dho).astype(BF16)
        dy_ref[...] = dy
        u_ref[...] = _rms_fwd(x, g_ref[...]).astype(BF16)
        acc = None
        for f in range(nf):
            cols = slice(f * tf, (f + 1) * tf)
            dact = _dot_nt(dy, w_ref[2, cols, :])
            gt = gate_ref[:, cols].astype(F32)
            sig = _sigmoid(gt)
            dup = (dact * (gt * sig)).astype(BF16)
            dgate = (dact * up_ref[:, cols].astype(F32) * (sig * (1.0 + gt * (1.0 - sig)))).astype(BF16)
            dup_ref[:, cols] = dup
            dgate_ref[:, cols] = dgate
            part = _dot(dgate, w_ref[0, cols, :]) + _dot(dup, w_ref[1, cols, :])
            acc = part if acc is None else acc + part
        dx, dg = _rms_bwd(acc, x, g_ref[...])
        dhi_ref[...] = dho + dx

        @pl.when(i == 0)
        def _():
            dg_ref[...] = jnp.zeros_like(dg_ref)

        dg_ref[...] += dg

        if has_c:
            @pl.when(i == nt - 1)
            def _():
                _exchange_wait(*comm)

    row = pl.BlockSpec((tm, D), lambda i: (i, 0))
    wide = pl.BlockSpec((tm, F), lambda i: (i, 0))
    return pl.pallas_call(
        body, name="ffn_bwd_dx_scatter" if has_c else "ffn_bwd_dx", grid=(nt,),
        in_specs=[row, row, _full((1, D)), wide, wide, _resident(w3.shape)] + c_specs,
        out_specs=[row, wide, wide, row, row, _full((1, D))] + c_specs,
        out_shape=[jax.ShapeDtypeStruct((T, D), F32), jax.ShapeDtypeStruct((T, F), BF16),
                   jax.ShapeDtypeStruct((T, F), BF16), jax.ShapeDtypeStruct((T, D), BF16),
                   jax.ShapeDtypeStruct((T, D), BF16), jax.ShapeDtypeStruct((1, D), F32)] + c_out,
        scratch_shapes=c_sems,
        compiler_params=_cparams("arbitrary"),
    )(dho, h, g, gate, up, w3, *c_in)


def _ffn_bwd_dw(dgate, dup, gate, up, u, dy):
    T, F = gate.shape
    D = u.shape[1]
    tfw = F // 2
    tk = min(TOKEN_TILE, T)
    nk = T // tk

    def body(dgate_ref, dup_ref, gate_ref, up_ref, u_ref, dy_ref, out_ref, acc_sc):
        k = pl.program_id(1)

        @pl.when(k == 0)
        def _():
            acc_sc[...] = jnp.zeros_like(acc_sc)

        uu = u_ref[...]
        acc_sc[0] += _dot_tn(dgate_ref[...], uu)
        acc_sc[1] += _dot_tn(dup_ref[...], uu)
        gt = gate_ref[...].astype(F32)
        act = (gt * _sigmoid(gt) * up_ref[...].astype(F32)).astype(BF16)
        acc_sc[2] += _dot_tn(act, dy_ref[...])

        @pl.when(k == nk - 1)
        def _():
            out_ref[...] = acc_sc[...].astype(BF16)

    blk = pl.BlockSpec((tk, tfw), lambda j, k: (k, j))
    row = pl.BlockSpec((tk, D), lambda j, k: (k, 0))
    return pl.pallas_call(
        body, name="ffn_bwd_dw", grid=(F // tfw, nk),
        in_specs=[blk, blk, blk, blk, row, row],
        out_specs=pl.BlockSpec((3, tfw, D), lambda j, k: (0, j, 0)),
        out_shape=jax.ShapeDtypeStruct((3, F, D), BF16),
        scratch_shapes=[pltpu.VMEM((3, tfw, D), F32)],
        compiler_params=_cparams("parallel", "arbitrary"),
    )(dgate, dup, gate, up, u, dy)


def _mm_rows(a, b, *, nt, out_dtype, norm_g=None, res=None, name):
    T, K = a.shape
    N = b.shape[0] if nt else b.shape[1]
    tm = min(TOKEN_TILE, T)
    has_g, has_r = norm_g is not None, res is not None

    def body(*refs):
        a_ref, b_ref = refs[0], refs[1]
        o_ref = refs[-1]
        x = a_ref[...]
        if has_g:
            x = _rms_fwd(x, refs[2][...])
        x = x.astype(BF16)
        acc = _dot_nt(x, b_ref[...]) if nt else _dot(x, b_ref[...])
        if has_r:
            acc = refs[2 + has_g][...] + acc
        o_ref[...] = acc.astype(out_dtype)

    ins, specs = [a, b], [pl.BlockSpec((tm, K), lambda i: (i, 0)), _full(b.shape)]
    if has_g:
        ins.append(norm_g)
        specs.append(_full((1, K)))
    if has_r:
        ins.append(res)
        specs.append(pl.BlockSpec((tm, N), lambda i: (i, 0)))
    return pl.pallas_call(
        body, name=name, grid=(T // tm,), in_specs=specs,
        out_specs=pl.BlockSpec((tm, N), lambda i: (i, 0)),
        out_shape=jax.ShapeDtypeStruct((T, N), out_dtype),
        compiler_params=_cparams("parallel"),
    )(*ins)


def _mm_tn(a, b, *, norm_g=None, name):
    T, M = a.shape
    N = b.shape[1]
    tk = min(TOKEN_TILE, T)
    has_g = norm_g is not None

    def body(*refs):
        a_ref, b_ref, o_ref = refs[0], refs[1], refs[-1]

        @pl.when(pl.program_id(0) == 0)
        def _():
            o_ref[...] = jnp.zeros_like(o_ref)

        x = a_ref[...]
        if has_g:
            x = _rms_fwd(x, refs[2][...])
        o_ref[...] += _dot_tn(x.astype(BF16), b_ref[...].astype(BF16))

    ins = [a, b]
    specs = [pl.BlockSpec((tk, M), lambda k: (k, 0)), pl.BlockSpec((tk, N), lambda k: (k, 0))]
    if has_g:
        ins.append(norm_g)
        specs.append(_full((1, M)))
    return pl.pallas_call(
        body, name=name, grid=(T // tk,), in_specs=specs, out_specs=_full((M, N)),
        out_shape=jax.ShapeDtypeStruct((M, N), F32),
        compiler_params=_cparams("arbitrary"),
    )(*ins)


def _proj_bwd(dz, w, h, g, dh, name):
    T, D = h.shape
    N = w.shape[1]
    tm = min(TOKEN_TILE, T)

    def body(dz_ref, w_ref, h_ref, g_ref, dh_ref, o_ref, dg_ref):
        du = _dot_nt(dz_ref[...].astype(BF16), w_ref[...])
        dx, dg = _rms_bwd(du, h_ref[...], g_ref[...])
        o_ref[...] = dh_ref[...] + dx

        @pl.when(pl.program_id(0) == 0)
        def _():
            dg_ref[...] = jnp.zeros_like(dg_ref)

        dg_ref[...] += dg

    row = pl.BlockSpec((tm, D), lambda i: (i, 0))
    return pl.pallas_call(
        body, name=name, grid=(T // tm,),
        in_specs=[pl.BlockSpec((tm, N), lambda i: (i, 0)), _full((D, N)), row, _full((1, D)), row],
        out_specs=[row, _full((1, D))],
        out_shape=[jax.ShapeDtypeStruct((T, D), F32), jax.ShapeDtypeStruct((1, D), F32)],
        compiler_params=_cparams("arbitrary"),
    )(dz, w, h, g, dh)


def _pool_bands(tm):
    r = np.arange(tm)[:, None]
    c = np.arange(tm)[None, :]
    j = np.arange(POOL_HALO)[None, :]
    main, halo, main_t, halo_t = [], [], [], []
    for w in POOL_WINDOWS:
        main.append(((r - c >= 0) & (r - c < w)) / w)
        halo.append((r + POOL_HALO - j < w) / w)
        main_t.append(((c - r >= 0) & (c - r < w)) / w)
        halo_t.append((tm + j - r < w) / w)
    return tuple(jnp.asarray(np.stack(m), BF16) for m in (main, halo, main_t, halo_t))


def _pool_count_scale(i, tm, w):
    t = i * tm + lax.broadcasted_iota(jnp.int32, (tm, 1), 0)
    return w / jnp.minimum(t + 1, w).astype(F32)


def _pool_fwd(h, g, wp, scale):
    T, D = h.shape
    G, dg = len(POOL_WINDOWS), D // len(POOL_WINDOWS)
    tm = min(TOKEN_TILE, T)
    hb = tm // POOL_HALO
    bm, bh, _, _ = _pool_bands(tm)

    def body(h_ref, hh_ref, g_ref, wp_ref, sc_ref, bm_ref, bh_ref, ho_ref, y_ref):
        i = pl.program_id(0)
        x = h_ref[...]
        u = _rms_fwd(x, g_ref[...])
        uh = _rms_fwd(hh_ref[...], g_ref[...]) * (i > 0).astype(F32)
        for gi, w in enumerate(POOL_WINDOWS):
            cols = slice(gi * dg, (gi + 1) * dg)
            ug = u[:, cols]
            hi, lo = _split_bf16(ug)
            hhi, hlo = _split_bf16(uh[:, cols])
            s = (_dot(bm_ref[gi], hi) + _dot(bm_ref[gi], lo)
                 + _dot(bh_ref[gi], hhi) + _dot(bh_ref[gi], hlo))
            y = (s * _pool_count_scale(i, tm, w) - ug).astype(BF16)
            y_ref[:, cols] = y
            ho_ref[:, cols] = x[:, cols] + _dot(y, wp_ref[gi]) * sc_ref[:, cols]

    row = pl.BlockSpec((tm, D), lambda i: (i, 0))
    return pl.pallas_call(
        body, name="pool_fwd", grid=(T // tm,),
        in_specs=[row, pl.BlockSpec((POOL_HALO, D), lambda i: (jnp.maximum(i * hb - 1, 0), 0)),
                  _full((1, D)), _full((G, dg, dg)), _full((1, D)),
                  _full((G, tm, tm)), _full((G, tm, POOL_HALO))],
        out_specs=[row, row],
        out_shape=[jax.ShapeDtypeStruct((T, D), F32), jax.ShapeDtypeStruct((T, D), BF16)],
        compiler_params=_cparams("parallel"),
    )(h, h, g, wp, scale, bm, bh)


def _pool_bwd(dh, h, g, y, wp, scale):
    T, D = h.shape
    G, dg = len(POOL_WINDOWS), D // len(POOL_WINDOWS)
    tm = min(TOKEN_TILE, T)
    hb = tm // POOL_HALO
    nt = T // tm
    _, _, bmt, bht = _pool_bands(tm)

    def body(dh_ref, dhn_ref, h_ref, g_ref, y_ref, wp_ref, sc_ref, bmt_ref, bht_ref,
             o_ref, dg_ref, dwp_ref, dsc_ref, du_sc):
        i = pl.program_id(0)

        @pl.when(i == 0)
        def _():
            dg_ref[...] = jnp.zeros_like(dg_ref)
            dwp_ref[...] = jnp.zeros_like(dwp_ref)
            dsc_ref[...] = jnp.zeros_like(dsc_ref)

        dho = dh_ref[...]
        dz = dho * sc_ref[...]
        dzn = dhn_ref[...] * sc_ref[...] * (i < nt - 1).astype(F32)
        for gi, w in enumerate(POOL_WINDOWS):
            cols = slice(gi * dg, (gi + 1) * dg)
            yg = y_ref[:, cols]
            dzg = dz[:, cols].astype(BF16)
            dsc_ref[:, cols] += jnp.sum(dho[:, cols] * _dot(yg, wp_ref[gi]), axis=0, keepdims=True)
            dwp_ref[gi] += _dot_tn(yg, dzg)
            dy = _dot_nt(dzg, wp_ref[gi])
            dyn = _dot_nt(dzn[:, cols].astype(BF16), wp_ref[gi])
            hi, lo = _split_bf16(dy * _pool_count_scale(i, tm, w))
            nhi, nlo = _split_bf16(dyn)
            du_sc[:, cols] = (_dot(bmt_ref[gi], hi) + _dot(bmt_ref[gi], lo)
                              + _dot(bht_ref[gi], nhi) + _dot(bht_ref[gi], nlo) - dy)
        dx, dgp = _rms_bwd(du_sc[...], h_ref[...], g_ref[...])
        o_ref[...] = dho + dx
        dg_ref[...] += dgp

    row = pl.BlockSpec((tm, D), lambda i: (i, 0))
    return pl.pallas_call(
        body, name="pool_bwd", grid=(nt,),
        in_specs=[row, pl.BlockSpec((POOL_HALO, D), lambda i: (jnp.minimum((i + 1) * hb, T // POOL_HALO - 1), 0)),
                  row, _full((1, D)), row, _full((G, dg, dg)), _full((1, D)),
                  _full((G, tm, tm)), _full((G, tm, POOL_HALO))],
        out_specs=[row, _full((1, D)), _full((G, dg, dg)), _full((1, D))],
        out_shape=[jax.ShapeDtypeStruct((T, D), F32), jax.ShapeDtypeStruct((1, D), F32),
                   jax.ShapeDtypeStruct((G, dg, dg), F32), jax.ShapeDtypeStruct((1, D), F32)],
        scratch_shapes=[pltpu.VMEM((tm, D), F32)],
        compiler_params=_cparams("arbitrary"),
    )(dh, dh, h, g, y, wp, scale, bmt, bht)


def _rope_tables(T):
    pos = jnp.arange(T, dtype=F32)
    inv_freq = ROPE_THETA ** (-jnp.arange(0, D_ROPE, 2, dtype=F32) / D_ROPE)
    ang = pos[:, None] * inv_freq[None, :]
    cos2 = jnp.tile(jnp.cos(ang), (1, 2))
    sin2 = jnp.tile(jnp.sin(ang), (1, 2))
    pad = jnp.zeros((T, D_HEAD_PAD - D_QK), F32)
    ca_q = jnp.concatenate([jnp.ones((T, D_NOPE), F32), cos2, pad], axis=1)
    ca_k = jnp.concatenate([jnp.zeros((T, D_NOPE), F32), cos2, pad], axis=1)
    sb = jnp.concatenate([jnp.zeros((T, D_NOPE), F32), sin2, pad], axis=1)
    return ca_q, ca_k, sb


def _rope_weight_pair(w):
    half = D_ROPE // 2
    z_pad = jnp.zeros(w.shape[:-1] + (D_HEAD_PAD - D_QK,), w.dtype)
    z_nope = jnp.zeros(w.shape[:-1] + (D_NOPE,), w.dtype)
    wa = jnp.concatenate([w, z_pad], axis=-1)
    wb = jnp.concatenate([z_nope, -w[..., D_NOPE + half:], w[..., D_NOPE:D_NOPE + half], z_pad], axis=-1)
    return wa, wb


def _rope_weight_pair_grad(dwa, dwb):
    half = D_ROPE // 2
    d1 = dwa[..., D_NOPE:D_NOPE + half] + dwb[..., D_NOPE + half:D_QK]
    d2 = dwa[..., D_NOPE + half:D_QK] - dwb[..., D_NOPE:D_NOPE + half]
    return jnp.concatenate([dwa[..., :D_NOPE], d1, d2], axis=-1)


def _q_proj(cq, qg, wa, wb, ca, sb):
    T, R = cq.shape
    tm = min(TOKEN_TILE, T)
    P = D_HEAD_PAD
    GP = HEAD_GROUP * P

    def body(cq_ref, qg_ref, wa_ref, wb_ref, ca_ref, sb_ref, q_ref):
        c = _rms_fwd(cq_ref[...], qg_ref[...]).astype(BF16)
        ca = jnp.tile(ca_ref[...], (1, HEAD_GROUP))
        sb = jnp.tile(sb_ref[...], (1, HEAD_GROUP))
        q_ref[...] = (_dot(c, wa_ref[...]) * ca + _dot(c, wb_ref[...]) * sb).astype(BF16)

    tok = pl.BlockSpec((tm, P), lambda i, hh: (i, 0))
    wsp = pl.BlockSpec((R, GP), lambda i, hh: (0, hh))
    return pl.pallas_call(
        body, name="q_proj", grid=(T // tm, N_HEADS // HEAD_GROUP),
        in_specs=[pl.BlockSpec((tm, R), lambda i, hh: (i, 0)), _full((1, R)), wsp, wsp, tok, tok],
        out_specs=pl.BlockSpec((tm, GP), lambda i, hh: (i, hh)),
        out_shape=jax.ShapeDtypeStruct((T, N_HEADS * P), BF16),
        compiler_params=_cparams("parallel", "arbitrary"),
    )(cq, qg, wa, wb, ca, sb)


def _q_proj_bwd(dq, cq, qg, wa, wb, ca, sb):
    T, R = cq.shape
    tm = min(TOKEN_TILE, T)
    P = D_HEAD_PAD

    def body(dq_ref, cq_ref, qg_ref, wa_ref, wb_ref, ca_ref, sb_ref,
             da_ref, db_ref, cqn_ref, dcq_ref, dqg_ref, acc_sc):
        i, hh = pl.program_id(0), pl.program_id(1)

        @pl.when(hh == 0)
        def _():
            acc_sc[...] = jnp.zeros_like(acc_sc)
            cqn_ref[...] = _rms_fwd(cq_ref[...], qg_ref[...]).astype(BF16)

        d = dq_ref[...].T
        da = (d * jnp.tile(ca_ref[...], (1, HEAD_GROUP))).astype(BF16)
        db = (d * jnp.tile(sb_ref[...], (1, HEAD_GROUP))).astype(BF16)
        da_ref[...] = da
        db_ref[...] = db
        acc_sc[...] += _dot_nt(da, wa_ref[...]) + _dot_nt(db, wb_ref[...])

        @pl.when(hh == N_HEADS // HEAD_GROUP - 1)
        def _():
            dx, dg = _rms_bwd(acc_sc[...], cq_ref[...], qg_ref[...])
            dcq_ref[...] = dx

            @pl.when(i == 0)
            def _():
                dqg_ref[...] = jnp.zeros_like(dqg_ref)

            dqg_ref[...] += dg

    GP = HEAD_GROUP * P
    tok = pl.BlockSpec((tm, P), lambda i, hh: (i, 0))
    hd = pl.BlockSpec((tm, GP), lambda i, hh: (i, hh))
    wsp = pl.BlockSpec((R, GP), lambda i, hh: (0, hh))
    rr = pl.BlockSpec((tm, R), lambda i, hh: (i, 0))
    return pl.pallas_call(
        body, name="q_proj_bwd", grid=(T // tm, N_HEADS // HEAD_GROUP),
        in_specs=[pl.BlockSpec((GP, tm), lambda i, hh: (hh, i)), rr, _full((1, R)), wsp, wsp, tok, tok],
        out_specs=[hd, hd, rr, rr, _full((1, R))],
        out_shape=[jax.ShapeDtypeStruct((T, N_HEADS * P), BF16), jax.ShapeDtypeStruct((T, N_HEADS * P), BF16),
                   jax.ShapeDtypeStruct((T, R), BF16), jax.ShapeDtypeStruct((T, R), F32),
                   jax.ShapeDtypeStruct((1, R), F32)],
        scratch_shapes=[pltpu.VMEM((tm, R), F32)],
        compiler_params=_cparams("arbitrary", "arbitrary"),
    )(dq, cq, qg, wa, wb, ca, sb)


def _kv_proj(h, g_in, wka, wkb, g_c, wuk, wuv, ca, sb):
    T, D = h.shape
    tm = min(TOKEN_TILE, T)
    P, C = D_HEAD_PAD, D_NOPE

    def body(h_ref, gi_ref, wka_ref, wkb_ref, gc_ref, wuk_ref, wuv_ref, ca_ref, sb_ref, k_ref, v_ref, craw_ref):
        u = _rms_fwd(h_ref[...], gi_ref[...]).astype(BF16)
        kva = _dot(u, wka_ref[...])
        kvb = _dot(u, wkb_ref[...])
        craw = kva[:, :C]
        craw_ref[...] = craw
        c = _rms_fwd(craw, gc_ref[...]).astype(BF16)
        kr = kva * ca_ref[...] + kvb * sb_ref[...]
        kn = _dot(c, wuk_ref[...])
        for hh in range(N_HEADS):
            k_ref[:, hh * P:(hh + 1) * P] = (kn[:, hh * P:(hh + 1) * P] + kr).astype(BF16)
        v_ref[...] = _dot(c, wuv_ref[...]).astype(BF16)

    tok = pl.BlockSpec((tm, P), lambda i: (i, 0))
    return pl.pallas_call(
        body, name="kv_proj", grid=(T // tm,),
        in_specs=[pl.BlockSpec((tm, D), lambda i: (i, 0)), _full((1, D)), _full((D, P)), _full((D, P)),
                  _full((1, C)), _full(wuk.shape), _full(wuv.shape), tok, tok],
        out_specs=[pl.BlockSpec((tm, N_HEADS * P), lambda i: (i, 0)),
                   pl.BlockSpec((tm, N_HEADS * D_V), lambda i: (i, 0)), pl.BlockSpec((tm, C), lambda i: (i, 0))],
        out_shape=[jax.ShapeDtypeStruct((T, N_HEADS * P), BF16), jax.ShapeDtypeStruct((T, N_HEADS * D_V), BF16),
                   jax.ShapeDtypeStruct((T, C), F32)],
        compiler_params=_cparams("parallel"),
    )(h, g_in, wka, wkb, g_c, wuk, wuv, ca, sb)


def _kv_proj_bwd(dks, dvs, dh, h, g_in, wka, wkb, craw, g_c, wuk, wuv, ca, sb):
    T, D = h.shape
    tm = min(TOKEN_TILE // 2, T)
    P, C = D_HEAD_PAD, D_NOPE
    nl = len(dks)

    def body(*refs):
        dk_refs, dv_refs = refs[:nl], refs[nl:2 * nl]
        (dh_ref, h_ref, gi_ref, wka_ref, wkb_ref, craw_ref, gc_ref, wuk_ref, wuv_ref,
         ca_ref, sb_ref, o_ref, dgi_ref, dwka_ref, dwkb_ref, dgc_ref, dwuk_ref, dwuv_ref) = refs[2 * nl:]

        @pl.when(pl.program_id(0) == 0)
        def _():
            for r in (dgi_ref, dwka_ref, dwkb_ref, dgc_ref, dwuk_ref, dwuv_ref):
                r[...] = jnp.zeros_like(r)

        x = h_ref[...]
        u = _rms_fwd(x, gi_ref[...]).astype(BF16)
        craw = craw_ref[...]
        c = _rms_fwd(craw, gc_ref[...]).astype(BF16)
        dkf = sum(r[...] for r in dk_refs[1:]) + dk_refs[0][...]
        dkb = dkf.astype(BF16)
        dvb = (sum(r[...] for r in dv_refs[1:]) + dv_refs[0][...]).astype(BF16)
        dwuk_ref[...] += _dot_tn(c, dkb)
        dwuv_ref[...] += _dot_tn(c, dvb)
        dc = _dot_nt(dkb, wuk_ref[...]) + _dot_nt(dvb, wuv_ref[...])
        dkr = dkf[:, :P]
        for hh in range(1, N_HEADS):
            dkr = dkr + dkf[:, hh * P:(hh + 1) * P]
        dcraw, dgc = _rms_bwd(dc, craw, gc_ref[...])
        dgc_ref[...] += dgc
        dkva = jnp.concatenate([dcraw, (dkr * ca_ref[...])[:, C:]], axis=1).astype(BF16)
        dkvb = (dkr * sb_ref[...]).astype(BF16)
        dwka_ref[...] += _dot_tn(u, dkva)
        dwkb_ref[...] += _dot_tn(u, dkvb)
        du = _dot_nt(dkva, wka_ref[...]) + _dot_nt(dkvb, wkb_ref[...])
        dx, dgi = _rms_bwd(du, x, gi_ref[...])
        dgi_ref[...] += dgi
        o_ref[...] = dh_ref[...] + dx

    row = pl.BlockSpec((tm, D), lambda i: (i, 0))
    tok = pl.BlockSpec((tm, P), lambda i: (i, 0))
    return pl.pallas_call(
        body, name="kv_proj_bwd", grid=(T // tm,),
        in_specs=[pl.BlockSpec((tm, N_HEADS * P), lambda i: (i, 0))] * nl
        + [pl.BlockSpec((tm, N_HEADS * D_V), lambda i: (i, 0))] * nl
        + [row, row, _full((1, D)), _full((D, P)), _full((D, P)), pl.BlockSpec((tm, C), lambda i: (i, 0)),
           _full((1, C)), _full(wuk.shape), _full(wuv.shape), tok, tok],
        out_specs=[row, _full((1, D)), _full((D, P)), _full((D, P)), _full((1, C)),
                   _full(wuk.shape), _full(wuv.shape)],
        out_shape=[jax.ShapeDtypeStruct((T, D), F32), jax.ShapeDtypeStruct((1, D), F32),
                   jax.ShapeDtypeStruct((D, P), F32), jax.ShapeDtypeStruct((D, P), F32),
                   jax.ShapeDtypeStruct((1, C), F32), jax.ShapeDtypeStruct(wuk.shape, F32),
                   jax.ShapeDtypeStruct(wuv.shape, F32)],
        compiler_params=_cparams("arbitrary"),
    )(*dks, *dvs, dh, h, g_in, wka, wkb, craw, g_c, wuk, wuv, ca, sb)


_ATTN_SCALE = D_QK ** -0.5
_LOG2_E = 1.4426950408889634
_LN_2 = 0.6931471805599453
_ATTN_SCALE_LOG2 = _ATTN_SCALE * _LOG2_E


def _flash_fwd(q, k, v):
    T = q.shape[0]
    t = min(ATTN_TILE, T // 2)
    tq = 2 * t
    P = D_HEAD_PAD

    def body(q_ref, k_ref, v_ref, o_ref, lse_ref, m_sc, l_sc, acc_sc, s0_sc, s1_sc, p0_sc, p1_sc, a0_sc, a1_sc):
        qi = pl.program_id(1)
        n = 2 * (qi + 1)
        s_sc, p_sc, a_sc = (s0_sc, s1_sc), (p0_sc, p1_sc), (a0_sc, a1_sc)
        m_sc[...] = jnp.full_like(m_sc, NEG_BIG)
        l_sc[...] = jnp.zeros_like(l_sc)
        acc_sc[...] = jnp.zeros_like(acc_sc)

        def rows_of(c):
            return pl.ds(pl.multiple_of(c * t, t), t)

        def scores(c, slot):
            s_sc[slot][...] = _dot_nt(k_ref[rows_of(c), :], q_ref[...])

        def softmax(slot, key_offset):
            s_t = s_sc[slot][...]
            if key_offset is not None:
                rows = lax.broadcasted_iota(jnp.int32, (t, tq), 0) + key_offset
                s_t = jnp.where(rows <= lax.broadcasted_iota(jnp.int32, (t, tq), 1), s_t, NEG_BIG)
            m_prev = m_sc[...]
            m_new = jnp.maximum(m_prev, jnp.max(s_t, axis=0, keepdims=True))
            p_t = jnp.exp2(s_t - m_new)
            alpha = jnp.exp2(m_prev - m_new)
            l_sc[...] = alpha * l_sc[...] + jnp.sum(p_t, axis=0, keepdims=True)
            m_sc[...] = m_new
            p_sc[slot][...] = p_t.astype(BF16)
            a_sc[slot][...] = alpha

        def values(c, slot):
            acc_sc[...] = a_sc[slot][...] * acc_sc[...] + _dot_tn(v_ref[rows_of(c), :], p_sc[slot][...])

        def stage(c, slot, first=False, last=False, key_offset=None):
            if not first:
                values(c - 1, 1 - slot)
            if not last:
                scores(c + 1, 1 - slot)
            softmax(slot, key_offset)

        def drain():
            stage(n - 2, 0, key_offset=0)
            stage(n - 1, 1, last=True, key_offset=t)
            values(n - 1, 1)

        scores(0, 0)

        @pl.when(qi == 0)
        def _():
            stage(0, 0, first=True, key_offset=0)
            stage(1, 1, last=True, key_offset=t)
            values(1, 1)

        @pl.when(qi > 0)
        def _():
            stage(0, 0, first=True)

            def pair(j, carry):
                stage(1 + 2 * j, 1)
                stage(2 + 2 * j, 0)
                return carry

            lax.fori_loop(0, qi - 1, pair, 0)
            stage(n - 3, 1)
            drain()

        l = l_sc[...]
        o_ref[...] = (acc_sc[...] / l).T.astype(BF16)
        lse_ref[...] = m_sc[...] + jnp.log(l) * _LOG2_E

    return pl.pallas_call(
        body, name="flash_fwd", grid=(N_HEADS, T // tq),
        in_specs=[pl.BlockSpec((tq, P), lambda hh, i: (i, hh)), pl.BlockSpec((T, P), lambda hh, i: (0, hh)),
                  pl.BlockSpec((T, D_V), lambda hh, i: (0, hh))],
        out_specs=[pl.BlockSpec((tq, D_V), lambda hh, i: (i, hh)),
                   pl.BlockSpec((None, None, 1, tq), lambda hh, i: (hh, i, 0, 0))],
        out_shape=[jax.ShapeDtypeStruct((T, N_HEADS * D_V), BF16),
                   jax.ShapeDtypeStruct((N_HEADS, T // tq, 1, tq), F32)],
        scratch_shapes=[pltpu.VMEM((1, tq), F32), pltpu.VMEM((1, tq), F32), pltpu.VMEM((D_V, tq), F32),
                        pltpu.VMEM((t, tq), F32), pltpu.VMEM((t, tq), F32), pltpu.VMEM((t, tq), BF16),
                        pltpu.VMEM((t, tq), BF16), pltpu.VMEM((1, tq), F32), pltpu.VMEM((1, tq), F32)],
        compiler_params=_cparams("parallel", "arbitrary"),
    )(q, k, v)


def _attn_delta(o, do):
    T = o.shape[0]
    t = min(ATTN_TILE, T)

    def body(o_ref, do_ref, out_ref):
        ones = jnp.ones((8, D_V), BF16)
        for hh in range(N_HEADS):
            cols = slice(hh * D_V, (hh + 1) * D_V)
            hi, lo = _split_bf16(o_ref[:, cols].astype(F32) * do_ref[:, cols].astype(F32))
            out_ref[hh] = (_dot_nt(ones, hi) + _dot_nt(ones, lo))[0:1]

    tok = pl.BlockSpec((t, N_HEADS * D_V), lambda i: (i, 0))
    return pl.pallas_call(
        body, name="attn_delta", grid=(T // t,), in_specs=[tok, tok],
        out_specs=pl.BlockSpec((N_HEADS, None, 1, t), lambda i: (0, i, 0, 0)),
        out_shape=jax.ShapeDtypeStruct((N_HEADS, T // t, 1, t), F32),
        compiler_params=_cparams("parallel"),
    )(o, do)


def _flash_bwd(q, k, v, do, lse_row, delta_row):
    T = q.shape[0]
    t = min(ATTN_TILE, T // 2)
    tk = 2 * t
    nq = T // t
    P = D_HEAD_PAD

    def body(k_ref, v_ref, q_ref, do_ref, lse_ref, delta_ref, dqt_ref, dk_ref, dv_ref, dqt_sc, kt_sc,
             s0_sc, s1_sc, dp0_sc, dp1_sc, p0_sc, p1_sc, ds0_sc, ds1_sc):
        ki = pl.program_id(1)
        q0 = 2 * ki
        n = nq - q0
        s_sc, dp_sc, p_sc, ds_sc = (s0_sc, s1_sc), (dp0_sc, dp1_sc), (p0_sc, p1_sc), (ds0_sc, ds1_sc)
        kt_sc[...] = k_ref[...].astype(F32).T.astype(BF16)
        dk_ref[...] = jnp.zeros_like(dk_ref)
        dv_ref[...] = jnp.zeros_like(dv_ref)

        @pl.when(ki == 0)
        def _():
            dqt_sc[...] = jnp.zeros_like(dqt_sc)

        def rows_of(c):
            return pl.ds(pl.multiple_of((q0 + c) * t, t), t)

        def products(c, slot):
            s_sc[slot][...] = _dot_nt(k_ref[...], q_ref[rows_of(c), :])
            dp_sc[slot][...] = _dot_nt(v_ref[...], do_ref[rows_of(c), :])

        def elementwise(c, slot, query_offset):
            p_t = jnp.exp2(s_sc[slot][...] - lse_ref[q0 + c])
            if query_offset is not None:
                cols = lax.broadcasted_iota(jnp.int32, (tk, t), 1) + query_offset
                p_t = jnp.where(lax.broadcasted_iota(jnp.int32, (tk, t), 0) <= cols, p_t, 0.0)
            p_sc[slot][...] = p_t.astype(BF16)
            ds_sc[slot][...] = (p_t * (dp_sc[slot][...] - delta_ref[q0 + c])).astype(BF16)

        def gradients(c, slot):
            dv_ref[...] += _dot(p_sc[slot][...], do_ref[rows_of(c), :])
            ds_t = ds_sc[slot][...]
            dk_ref[...] += _dot(ds_t, q_ref[rows_of(c), :])
            dqt_sc[q0 + c] += _dot(kt_sc[...], ds_t)

        def stage(c, slot, first=False, last=False, query_offset=None):
            if not first:
                gradients(c - 1, 1 - slot)
            if not last:
                products(c + 1, 1 - slot)
            elementwise(c, slot, query_offset)

        products(0, 0)

        @pl.when(n == 2)
        def _():
            stage(0, 0, first=True, query_offset=0)
            stage(1, 1, last=True, query_offset=t)
            gradients(1, 1)

        @pl.when(n > 2)
        def _():
            stage(0, 0, first=True, query_offset=0)
            stage(1, 1, query_offset=t)

            def pair(j, carry):
                stage(2 + 2 * j, 0)
                stage(3 + 2 * j, 1)
                return carry

            lax.fori_loop(0, (n - 4) // 2, pair, 0)
            stage(n - 2, 0)
            stage(n - 1, 1, last=True)
            gradients(n - 1, 1)

        dqt_ref[:, :t] = dqt_sc[q0] * _ATTN_SCALE
        dqt_ref[:, t:] = dqt_sc[q0 + 1] * _ATTN_SCALE
        dk_ref[...] = dk_ref[...] * _LN_2

    kb = pl.BlockSpec((tk, P), lambda hh, i: (i, hh))
    vb = pl.BlockSpec((tk, D_V), lambda hh, i: (i, hh))
    stat = pl.BlockSpec((None, nq, 1, t), lambda hh, i: (hh, 0, 0, 0))
    return pl.pallas_call(
        body, name="flash_bwd", grid=(N_HEADS, T // tk),
        in_specs=[kb, vb,
                  pl.BlockSpec((T, P), lambda hh, i: (0, hh), pipeline_mode=pl.Buffered(1)),
                  pl.BlockSpec((T, D_V), lambda hh, i: (0, hh), pipeline_mode=pl.Buffered(1)),
                  stat, stat],
        out_specs=[pl.BlockSpec((P, tk), lambda hh, i: (hh, i)), kb, vb],
        out_shape=[jax.ShapeDtypeStruct((N_HEADS * P, T), F32), jax.ShapeDtypeStruct((T, N_HEADS * P), F32),
                   jax.ShapeDtypeStruct((T, N_HEADS * D_V), F32)],
        scratch_shapes=[pltpu.VMEM((nq, P, t), F32), pltpu.VMEM((P, tk), BF16)]
        + [pltpu.VMEM((tk, t), F32)] * 4 + [pltpu.VMEM((tk, t), BF16)] * 4,
        compiler_params=_cparams("arbitrary", "arbitrary"),
    )(k, v, q, do, lse_row, delta_row)


def _loss_head(h, g, target):
    T, D = h.shape
    tm = min(TOKEN_TILE, T)

    def body(h_ref, g_ref, t_ref, dh_ref, loss_ref, dg_ref):
        @pl.when(pl.program_id(0) == 0)
        def _():
            loss_ref[...] = jnp.zeros_like(loss_ref)
            dg_ref[...] = jnp.zeros_like(dg_ref)

        x = h_ref[...]
        err = _rms_fwd(x, g_ref[...]) - t_ref[...]
        per_tok = jnp.mean(err * err, axis=-1, keepdims=True)
        loss_ref[...] += 0.5 * jnp.sum(per_tok, axis=0, keepdims=True)
        dx, dg = _rms_bwd(err * (1.0 / D), x, g_ref[...])
        dh_ref[...] = dx
        dg_ref[...] += dg

    row = pl.BlockSpec((tm, D), lambda i: (i, 0))
    return pl.pallas_call(
        body, name="loss_head", grid=(T // tm,),
        in_specs=[row, _full((1, D)), row], out_specs=[row, _full((1, 128)), _full((1, D))],
        out_shape=[jax.ShapeDtypeStruct((T, D), F32), jax.ShapeDtypeStruct((1, 128), F32),
                   jax.ShapeDtypeStruct((1, D), F32)],
        compiler_params=_cparams("arbitrary"),
    )(h, g, target)


def _sum_parts(parts, tr, name):
    _, R, C = parts.shape

    def body(p_ref, o_ref):
        acc = p_ref[0].astype(F32)
        for j in range(1, N_DEV):
            acc = acc + p_ref[j].astype(F32)
        o_ref[...] = acc

    return pl.pallas_call(
        body, name=name, grid=(R // tr,),
        in_specs=[pl.BlockSpec((N_DEV, tr, C), lambda i: (0, i, 0))],
        out_specs=pl.BlockSpec((tr, C), lambda i: (i, 0)),
        out_shape=jax.ShapeDtypeStruct((R, C), F32),
        compiler_params=_cparams("parallel"),
    )(parts)


def _adamw(w, g, m, v):
    R, C = w.shape
    tr = _row_tile(R, TOKEN_TILE)

    def body(w_ref, g_ref, m_ref, v_ref, d_ref, mo_ref, vo_ref):
        gg = g_ref[...]
        mn = ADAM_B1 * m_ref[...] + (1.0 - ADAM_B1) * gg
        vn = ADAM_B2 * v_ref[...] + (1.0 - ADAM_B2) * (gg * gg)
        m_hat = mn / (1.0 - ADAM_B1 ** ADAM_STEP)
        v_hat = vn / (1.0 - ADAM_B2 ** ADAM_STEP)
        d_ref[...] = -ADAM_LR * (m_hat / (jnp.sqrt(v_hat) + ADAM_EPS) + ADAM_WD * w_ref[...])
        mo_ref[...] = mn
        vo_ref[...] = vn

    blk = pl.BlockSpec((tr, C), lambda i: (i, 0))
    return pl.pallas_call(
        body, name="adamw", grid=(R // tr,), in_specs=[blk] * 4, out_specs=[blk] * 3,
        out_shape=[jax.ShapeDtypeStruct((R, C), F32)] * 3,
        compiler_params=_cparams("parallel"),
    )(w, g, m, v)


def _adamw_nd(w, g, m, v):
    shape = w.shape
    two_d = (1, shape[0]) if len(shape) == 1 else (int(np.prod(shape[:-1])), shape[-1])
    outs = _adamw(w.reshape(two_d), g.reshape(two_d), m.reshape(two_d), v.reshape(two_d))
    return tuple(o.reshape(shape) for o in outs)


def _f32_as_bf16_pairs(a):
    return lax.bitcast_convert_type(a, BF16).reshape(a.shape[:-1] + (a.shape[-1] * 2,))


def _bf16_pairs_as_f32(a):
    return lax.bitcast_convert_type(a.reshape(a.shape[:-1] + (a.shape[-1] // 2, 2)), F32)


def _pack_misc(w_o, w_dq, w_uq, w_dkv, pool_w, pool_scale):
    lead = w_o.shape[:-3]
    rows = [w_o, w_dq, w_uq, w_dkv, pool_w]
    flat = [r.astype(BF16).reshape(lead + (-1, REP_COLS)) for r in rows]
    ps = _f32_as_bf16_pairs(pool_scale.astype(F32)).reshape(lead + (1, -1))
    ps = jnp.concatenate([ps, jnp.zeros(lead + (1, REP_COLS - ps.shape[-1]), BF16)], axis=-1)
    used = sum(f.shape[-2] for f in flat) + 1
    pad = jnp.zeros(lead + (MISC_ROWS - used, REP_COLS), BF16)
    return jnp.concatenate(flat + [ps, pad], axis=-2)


def _unpack_misc(buf, shapes):
    out, r0 = [], 0
    for shp in shapes[:-1]:
        n = int(np.prod(shp)) // REP_COLS
        out.append(buf[:, r0:r0 + n].reshape((N_DEV,) + shp))
        r0 += n
    n_ps = int(np.prod(shapes[-1]))
    out.append(_bf16_pairs_as_f32(buf[:, r0, :2 * n_ps]).reshape((N_DEV,) + shapes[-1]))
    return out


def _cat_dev(a, axis):
    a = jnp.moveaxis(a, 0, axis)
    return a.reshape(a.shape[:axis] + (a.shape[axis] * a.shape[axis + 1],) + a.shape[axis + 2:])


def _split_dev(a, axis):
    a = a.reshape(a.shape[:axis] + (N_DEV, a.shape[axis] // N_DEV) + a.shape[axis + 1:])
    return jnp.moveaxis(a, axis, 0)


def kernel(x, ffn_pre_norm, ffn_pre_wg, ffn_pre_wu, ffn_pre_wd, mix_norm, ffn_post_norm, ffn_post_wg, ffn_post_wu, ffn_post_wd, pool_w, pool_scale, kv_in_norm, w_dkv, ckv_norm, w_uk, w_uv, q_lora_norm, w_dq, w_uq, w_o, final_norm, loss_target, m_ffn_pre_norm, m_ffn_pre_wg, m_ffn_pre_wu, m_ffn_pre_wd, m_mix_norm, m_ffn_post_norm, m_ffn_post_wg, m_ffn_post_wu, m_ffn_post_wd, m_pool_w, m_pool_scale, m_kv_in_norm, m_w_dkv, m_ckv_norm, m_w_uk, m_w_uv, m_q_lora_norm, m_w_dq, m_w_uq, m_w_o, m_final_norm, v_ffn_pre_norm, v_ffn_pre_wg, v_ffn_pre_wu, v_ffn_pre_wd, v_mix_norm, v_ffn_post_norm, v_ffn_post_wg, v_ffn_post_wu, v_ffn_post_wd, v_pool_w, v_pool_scale, v_kv_in_norm, v_w_dkv, v_ckv_norm, v_w_uk, v_w_uv, v_q_lora_norm, v_w_dq, v_w_uq, v_w_o, v_final_norm):
    weights = dict(ffn_pre_norm=ffn_pre_norm, ffn_pre_wg=ffn_pre_wg, ffn_pre_wu=ffn_pre_wu, ffn_pre_wd=ffn_pre_wd,
                   mix_norm=mix_norm, ffn_post_norm=ffn_post_norm, ffn_post_wg=ffn_post_wg,
                   ffn_post_wu=ffn_post_wu, ffn_post_wd=ffn_post_wd, pool_w=pool_w, pool_scale=pool_scale,
                   kv_in_norm=kv_in_norm, w_dkv=w_dkv, ckv_norm=ckv_norm, w_uk=w_uk, w_uv=w_uv,
                   q_lora_norm=q_lora_norm, w_dq=w_dq, w_uq=w_uq, w_o=w_o, final_norm=final_norm)
    moments_m = dict(ffn_pre_norm=m_ffn_pre_norm, ffn_pre_wg=m_ffn_pre_wg, ffn_pre_wu=m_ffn_pre_wu,
                     ffn_pre_wd=m_ffn_pre_wd, mix_norm=m_mix_norm, ffn_post_norm=m_ffn_post_norm,
                     ffn_post_wg=m_ffn_post_wg, ffn_post_wu=m_ffn_post_wu, ffn_post_wd=m_ffn_post_wd,
                     pool_w=m_pool_w, pool_scale=m_pool_scale, kv_in_norm=m_kv_in_norm, w_dkv=m_w_dkv,
                     ckv_norm=m_ckv_norm, w_uk=m_w_uk, w_uv=m_w_uv, q_lora_norm=m_q_lora_norm, w_dq=m_w_dq,
                     w_uq=m_w_uq, w_o=m_w_o, final_norm=m_final_norm)
    moments_v = dict(ffn_pre_norm=v_ffn_pre_norm, ffn_pre_wg=v_ffn_pre_wg, ffn_pre_wu=v_ffn_pre_wu,
                     ffn_pre_wd=v_ffn_pre_wd, mix_norm=v_mix_norm, ffn_post_norm=v_ffn_post_norm,
                     ffn_post_wg=v_ffn_post_wg, ffn_post_wu=v_ffn_post_wu, ffn_post_wd=v_ffn_post_wd,
                     pool_w=v_pool_w, pool_scale=v_pool_scale, kv_in_norm=v_kv_in_norm, w_dkv=v_w_dkv,
                     ckv_norm=v_ckv_norm, w_uk=v_w_uk, w_uv=v_w_uv, q_lora_norm=v_q_lora_norm, w_dq=v_w_dq,
                     w_uq=v_w_uq, w_o=v_w_o, final_norm=v_final_norm)
    order = list(weights)

    T, D = x.shape[1], x.shape[2]
    depth = ffn_pre_norm.shape[0]
    n_a = pool_w.shape[0]
    n_b = depth - n_a
    fs = ffn_pre_wd.shape[1]
    F = fs * N_DEV
    n_ffn = 2 * depth
    t_attn = min(ATTN_TILE, T)

    ffn_local = [
        jnp.stack([jnp.swapaxes(wg[l], 0, 1), jnp.swapaxes(wu[l], 0, 1), wd[l]]).astype(BF16)
        for l in range(depth)
        for wg, wu, wd in ((ffn_pre_wg, ffn_pre_wu, ffn_pre_wd), (ffn_post_wg, ffn_post_wu, ffn_post_wd))
    ]
    misc_local = _pack_misc(w_o, w_dq, w_uq.reshape(n_b, w_uq.shape[1], -1), w_dkv, pool_w, pool_scale)
    misc_shapes = (w_o.shape, w_dq.shape, (n_b, w_uq.shape[1], N_HEADS * D_QK), w_dkv.shape, pool_w.shape,
                   pool_scale.shape)
    (w0_all,) = _exchange([(ffn_local[0], "gather_mid")], "comm_all_gather")
    walls = [w0_all.reshape(3, F, D)] + [None] * (n_ffn - 1)

    def vec(a):
        return a.reshape(1, -1)

    def ffn_stage(e, h_in, norm):
        carry = [(ffn_local[e + 1], "gather_mid")] if e + 1 < n_ffn else []
        if e == 0:
            carry.append((misc_local, "to_all"))
        outs = _ffn_fwd(h_in, norm, walls[e], carry)
        if carry:
            walls[e + 1] = outs[3].reshape(3, F, D)
        return outs

    h = x.reshape(T, D)
    stage0 = ffn_stage(0, h, vec(ffn_pre_norm[0]))
    misc_all = stage0[4]
    o_blk, dq_blk, uq_blk, dkv_blk, pw_blk, ps_blk = _unpack_misc(misc_all, misc_shapes)
    w_o_f = _cat_dev(o_blk, 1)
    w_dq_f = _cat_dev(dq_blk, 1)
    w_uq_f = _cat_dev(uq_blk, 1).reshape(n_b, -1, N_HEADS, D_QK)
    w_dkv_f = _cat_dev(dkv_blk, 0)
    pool_w_f = _cat_dev(pw_blk, 2)
    pool_scale_f = _cat_dev(ps_blk, 1)
    rq = w_dq_f.shape[2]
    wqa, wqb = _rope_weight_pair(w_uq_f)
    wqa = wqa.reshape(n_b, rq, N_HEADS * D_HEAD_PAD)
    wqb = wqb.reshape(n_b, rq, N_HEADS * D_HEAD_PAD)
    wka, wkb = _rope_weight_pair(w_dkv_f)
    wuk = jnp.concatenate([w_uk, jnp.zeros_like(w_uk)], axis=-1).astype(BF16).reshape(D_NOPE, N_HEADS * D_HEAD_PAD)
    wuv = w_uv.astype(BF16).reshape(D_NOPE, N_HEADS * D_V)
    ca_q, ca_k, sb = _rope_tables(T)
    ca_q_scaled, sb_scaled = ca_q * _ATTN_SCALE_LOG2, sb * _ATTN_SCALE_LOG2

    saved = []
    k_all = v_all = craw = h_kv = None
    for l in range(depth):
        s = {"h0": h}
        h, s["g1"], s["u1"] = (stage0 if l == 0 else ffn_stage(2 * l, h, vec(ffn_pre_norm[l])))[:3]
        s["h1"] = h
        if l < n_a:
            h, s["y"] = _pool_fwd(h, vec(mix_norm[l]), pool_w_f[l], vec(pool_scale_f[l]))
        else:
            j = l - n_a
            s["cq"] = _mm_rows(h, w_dq_f[j], nt=False, out_dtype=F32, norm_g=vec(mix_norm[l]), name="q_down")
            s["q"] = _q_proj(s["cq"], vec(q_lora_norm[j]), wqa[j], wqb[j], ca_q_scaled, sb_scaled)
            s["o"], lse = _flash_fwd(s["q"], k_all, v_all)
            s["lse"] = lse.reshape(N_HEADS, T // t_attn, 1, t_attn)
            h = _mm_rows(s["o"], w_o_f[j], nt=False, out_dtype=F32, res=h, name="attn_out")
        s["h2"] = h
        h, s["g2"], s["u2"] = ffn_stage(2 * l + 1, h, vec(ffn_post_norm[l]))[:3]
        if l == n_a - 1:
            h_kv = h
            k_all, v_all, craw = _kv_proj(h, vec(kv_in_norm), wka, wkb, vec(ckv_norm), wuk, wuv, ca_k, sb)
        saved.append(s)

    dh, loss_part, d_final = _loss_head(h, vec(final_norm), loss_target.reshape(T, D))

    slabs, ffn_parts = [None] * n_ffn, [None] * n_ffn

    misc_parts = []

    def packed_misc_grads():
        return _pack_misc(_split_dev(jnp.stack(d_wo), 1), _split_dev(jnp.stack(d_wdq), 1),
                          _split_dev(jnp.stack(d_wuq).reshape(n_b, rq, -1), 1), _split_dev(grads["w_dkv"], 0),
                          _split_dev(jnp.stack(d_pool_w), 2),
                          _split_dev(jnp.concatenate(d_pool_scale, axis=0), 1))

    def ffn_stage_bwd(e, dh_out, h_in, norm, gate, up):
        carry = [(slabs[e + 1].reshape(3, N_DEV, fs, D), "scatter_mid")] if e + 1 < n_ffn else []
        if e == 0:
            carry.append((packed_misc_grads(), "scatter_lead"))
        outs = _ffn_bwd_dx(dh_out, h_in, norm, gate, up, walls[e], carry)
        if carry:
            ffn_parts[e + 1] = outs[6]
        if e == 0:
            misc_parts.append(outs[7])
        dh_in, dgt, dup, u_b, dy_b, dnorm = outs[:6]
        slabs[e] = _ffn_bwd_dw(dgt, dup, gate, up, u_b, dy_b)
        return dh_in, dnorm

    grads = {}
    d_pre, d_post, d_mix = [None] * depth, [None] * depth, [None] * depth
    d_pool_w, d_pool_scale = [None] * n_a, [None] * n_a
    d_qln, d_wdq, d_wuq, d_wo = [None] * n_b, [None] * n_b, [None] * n_b, [None] * n_b
    dks, dvs = [], []
    for l in reversed(range(depth)):
        s = saved[l]
        if l == n_a - 1:
            (dh, grads["kv_in_norm"], dwka, dwkb, grads["ckv_norm"], dwuk, dwuv) = _kv_proj_bwd(
                dks, dvs, dh, h_kv, vec(kv_in_norm), wka, wkb, craw, vec(ckv_norm), wuk, wuv, ca_k, sb)
            grads["w_dkv"] = _rope_weight_pair_grad(dwka, dwkb)
            grads["w_uk"] = dwuk.reshape(D_NOPE, N_HEADS, D_HEAD_PAD)[..., :D_NOPE]
            grads["w_uv"] = dwuv.reshape(D_NOPE, N_HEADS, D_V)
        dh, d_post[l] = ffn_stage_bwd(2 * l + 1, dh, s["h2"], vec(ffn_post_norm[l]), s["g2"], s["u2"])
        if l < n_a:
            dh, d_mix[l], d_pool_w[l], d_pool_scale[l] = _pool_bwd(
                dh, s["h1"], vec(mix_norm[l]), s["y"], pool_w_f[l], vec(pool_scale_f[l]))
        else:
            j = l - n_a
            d_wo[j] = _mm_tn(s["o"], dh, name="attn_out_dw")
            do = _mm_rows(dh, w_o_f[j], nt=True, out_dtype=BF16, name="attn_out_dx")
            delta_row = _attn_delta(s["o"], do)
            dq_t, dk_l, dv_l = _flash_bwd(s["q"], k_all, v_all, do, s["lse"], delta_row)
            dks.append(dk_l)
            dvs.append(dv_l)
            da, db, cqn, dcq, d_qln[j] = _q_proj_bwd(dq_t, s["cq"], vec(q_lora_norm[j]), wqa[j], wqb[j], ca_q, sb)
            dwa = _mm_tn(cqn, da, name="q_up_dw")
            dwb = _mm_tn(cqn, db, name="q_up_dw")
            d_wuq[j] = _rope_weight_pair_grad(dwa.reshape(rq, N_HEADS, D_HEAD_PAD),
                                              dwb.reshape(rq, N_HEADS, D_HEAD_PAD))
            d_wdq[j] = _mm_tn(s["h1"], dcq, norm_g=vec(mix_norm[l]), name="q_down_dw")
            dh, d_mix[l] = _proj_bwd(dcq, w_dq_f[j], s["h1"], vec(mix_norm[l]), dh, "q_down_dx")
        dh, d_pre[l] = ffn_stage_bwd(2 * l, dh, s["h0"], vec(ffn_pre_norm[l]), s["g1"], s["u1"])
    grad_x = dh.reshape(x.shape)

    rep_names = ["ffn_pre_norm", "mix_norm", "ffn_post_norm", "kv_in_norm", "ckv_norm", "q_lora_norm",
                 "final_norm", "w_uk", "w_uv"]
    grads["ffn_pre_norm"] = jnp.concatenate(d_pre, axis=0)
    grads["mix_norm"] = jnp.concatenate(d_mix, axis=0)
    grads["ffn_post_norm"] = jnp.concatenate(d_post, axis=0)
    grads["q_lora_norm"] = jnp.concatenate(d_qln, axis=0)
    grads["final_norm"] = d_final
    rep_flat = jnp.concatenate([grads[n].reshape(-1) for n in rep_names] + [loss_part[0, :1]])
    n_rep = rep_flat.shape[0]
    rep_rows = -(-n_rep // (8 * REP_COLS)) * 8
    rep_g = jnp.concatenate([rep_flat, jnp.zeros((rep_rows * REP_COLS - n_rep,), F32)]).reshape(rep_rows, REP_COLS)
    ffn_parts[0], rep_parts = _exchange([(slabs[0].reshape(3, N_DEV, fs, D), "scatter_mid"), (rep_g, "to_all")],
                                        "comm_grad_exchange")
    ffn_sum = jnp.stack([_sum_parts(p.reshape(N_DEV, 3 * fs, D), fs, "sum_ffn").reshape(3, fs, D)
                         for p in ffn_parts])
    misc_sum_parts = _unpack_misc(misc_parts[0], misc_shapes)
    rep_sum = _sum_parts(rep_parts, _row_tile(rep_rows, 128), "sum_rep").reshape(-1)

    def sum_small(p):
        shp = p.shape[1:]
        two_d = (int(np.prod(shp[:-1])), shp[-1])
        return _sum_parts(p.reshape((N_DEV,) + two_d), two_d[0], "sum_misc").reshape(shp)

    g_wo, g_wdq, g_wuq, g_wdkv, g_pw, g_ps = [sum_small(p) for p in misc_sum_parts]
    grads.update(w_o=g_wo, w_dq=g_wdq, w_uq=g_wuq.reshape(w_uq.shape), w_dkv=g_wdkv, pool_w=g_pw, pool_scale=g_ps)
    for kind, (npre, npost) in enumerate((("ffn_pre_wg", "ffn_post_wg"), ("ffn_pre_wu", "ffn_post_wu"),
                                          ("ffn_pre_wd", "ffn_post_wd"))):
        pre = ffn_sum[0::2, kind]
        post = ffn_sum[1::2, kind]
        if kind < 2:
            pre, post = jnp.swapaxes(pre, 1, 2), jnp.swapaxes(post, 1, 2)
        grads[npre], grads[npost] = pre, post
    off = 0
    for n in rep_names:
        size = int(np.prod(weights[n].shape))
        grads[n] = rep_sum[off:off + size].reshape(weights[n].shape)
        off += size
    loss = rep_sum[off]

    deltas, new_m, new_v = {}, {}, {}
    for n in order:
        deltas[n], new_m[n], new_v[n] = _adamw_nd(weights[n], grads[n], moments_m[n], moments_v[n])
    return (loss, grad_x, *[grads[n] for n in order], *[deltas[n] for n in order],
            *[new_m[n] for n in order], *[new_v[n] for n in order])
```

```python
import functools

import numpy as np
import jax
import jax.numpy as jnp
from jax import lax
from jax.experimental import pallas as pl
from jax.experimental.pallas import tpu as pltpu

F32, BF16 = jnp.float32, jnp.bfloat16
N_DEV = 8
RMS_EPS = 1e-6
N_HEADS = 16
D_NOPE, D_ROPE, D_V = 128, 64, 128
D_QK = D_NOPE + D_ROPE
D_HEAD_PAD = 256
HEAD_GROUP = 4
ROPE_THETA = 10000.0
POOL_WINDOWS = (2, 4, 8, 16)
POOL_HALO = 16
ADAM_LR, ADAM_B1, ADAM_B2, ADAM_EPS, ADAM_WD, ADAM_STEP = 0.001, 0.9, 0.999, 1e-08, 0.01, 10
NEG_BIG = -1e30
V7X_VMEM_LIMIT = 56 * 1024 * 1024
TOKEN_TILE = 512
ATTN_TILE = 512
FFN_TILE = 256
MISC_ROWS = 864
REP_COLS = 1024


def _cparams(*sem):
    return pltpu.CompilerParams(dimension_semantics=sem, vmem_limit_bytes=V7X_VMEM_LIMIT)


def _dot(a, b):
    return lax.dot_general(a, b, (((1,), (0,)), ((), ())), preferred_element_type=F32)


def _dot_nt(a, b):
    return lax.dot_general(a, b, (((1,), (1,)), ((), ())), preferred_element_type=F32)


def _dot_tn(a, b):
    return lax.dot_general(a, b, (((0,), (0,)), ((), ())), preferred_element_type=F32)


def _rms_fwd(x, g):
    r = lax.rsqrt(jnp.mean(x * x, axis=-1, keepdims=True) + RMS_EPS)
    return (x * r) * g


def _rms_bwd(du, x, g):
    r = lax.rsqrt(jnp.mean(x * x, axis=-1, keepdims=True) + RMS_EPS)
    xh = x * r
    dg = jnp.sum(du * xh, axis=0, keepdims=True)
    dxh = du * g
    dx = r * (dxh - xh * jnp.mean(dxh * xh, axis=-1, keepdims=True))
    return dx, dg


def _sigmoid(x):
    return 1.0 / (1.0 + jnp.exp(-x))


def _split_bf16(x):
    hi = x.astype(BF16)
    lo = (x - hi.astype(F32)).astype(BF16)
    return hi, lo


def _full(shape):
    return pl.BlockSpec(shape, lambda *_: (0,) * len(shape))


def _resident(shape):
    return pl.BlockSpec(shape, lambda *_: (0,) * len(shape), pipeline_mode=pl.Buffered(1))


def _row_tile(rows, cap):
    for t in range(min(cap, rows) // 8 * 8, 0, -8):
        if rows % t == 0:
            return t
    return rows


def _peers():
    x, y, c = lax.axis_index("x"), lax.axis_index("y"), lax.axis_index("c")
    out = []
    for k in range(1, N_DEV):
        px = 1 - x if (k >> 2) & 1 else x
        py = 1 - y if (k >> 1) & 1 else y
        pc = 1 - c if k & 1 else c
        out.append(((px, py, pc), 4 * px + 2 * py + pc))
    return 4 * x + 2 * y + c, out


_ROUTES = {
    "gather_mid": (lambda ref, idx: ref, lambda ref, idx: ref.at[:, idx], lambda s: s[:1] + (N_DEV,) + s[1:]),
    "to_all": (lambda ref, idx: ref, lambda ref, idx: ref.at[idx], lambda s: (N_DEV,) + s),
    "scatter_mid": (lambda ref, idx: ref.at[:, idx], lambda ref, idx: ref.at[idx],
                    lambda s: (N_DEV, s[0]) + s[2:]),
    "scatter_lead": (lambda ref, idx: ref.at[idx], lambda ref, idx: ref.at[idx], lambda s: s),
}


def _route_fns(items):
    kinds = [kind for _, kind in items]
    return (lambda j, ref, idx: _ROUTES[kinds[j]][0](ref, idx)), (lambda j, ref, idx: _ROUTES[kinds[j]][1](ref, idx))


def _route_out_shapes(items):
    return [jax.ShapeDtypeStruct(_ROUTES[kind][2](arr.shape), arr.dtype) for arr, kind in items]


def _exchange(items, name):
    n = len(items)
    fns = _route_fns(items)

    def body(*refs):
        ins, outs, sems = refs[:n], refs[n:2 * n], refs[2 * n:]
        _exchange_start(ins, outs, sems, *fns)
        _exchange_wait(ins, outs, sems, *fns)

    any_spec = pl.BlockSpec(memory_space=pl.ANY)
    return pl.pallas_call(
        body, name=name, out_shape=_route_out_shapes(items),
        in_specs=[any_spec] * n, out_specs=[any_spec] * n,
        scratch_shapes=_exchange_sems(n),
    )(*[arr for arr, _ in items])


def _exchange_sems(n):
    return [pltpu.SemaphoreType.DMA((n, N_DEV - 1)), pltpu.SemaphoreType.DMA((n, N_DEV - 1)),
            pltpu.SemaphoreType.DMA((n,))]


def _own_copies(ins, outs, sems, src_of, dst_of):
    me, _ = _peers()
    return [pltpu.make_async_copy(src_of(j, ins[j], me), dst_of(j, outs[j], me), sems[2].at[j])
            for j in range(len(ins))]


def _remote_copies(ins, outs, sems, src_of, dst_of, receiving):
    me, peers = _peers()
    return [pltpu.make_async_remote_copy(
        src_ref=src_of(j, ins[j], pidx), dst_ref=dst_of(j, outs[j], pidx if receiving else me),
        send_sem=sems[0].at[j, k], recv_sem=sems[1].at[j, k],
        device_id=peer, device_id_type=pl.DeviceIdType.MESH)
        for k, (peer, pidx) in enumerate(peers) for j in range(len(ins))]


def _exchange_start(ins, outs, sems, src_of, dst_of):
    for cp in _own_copies(ins, outs, sems, src_of, dst_of):
        cp.start()
    for cp in _remote_copies(ins, outs, sems, src_of, dst_of, receiving=False):
        cp.start()


def _exchange_wait(ins, outs, sems, src_of, dst_of):
    for cp in _remote_copies(ins, outs, sems, src_of, dst_of, receiving=True):
        cp.wait_recv()
    for cp in _remote_copies(ins, outs, sems, src_of, dst_of, receiving=False):
        cp.wait_send()
    for cp in _own_copies(ins, outs, sems, src_of, dst_of):
        cp.wait()


def _carried(items):
    if not items:
        return [], [], [], []
    return ([arr for arr, _ in items], [pl.BlockSpec(memory_space=pl.ANY)] * len(items), _route_out_shapes(items),
            _exchange_sems(len(items)))


def _ffn_fwd(h, g, w3, carry=()):
    T, D = h.shape
    F = w3.shape[1]
    tm, tf = min(TOKEN_TILE, T), F // 2
    nf = F // tf
    nt = T // tm
    nc = len(carry)
    has_c = nc > 0
    c_in, c_specs, c_out, c_sems = _carried(carry)

    def body(*refs):
        h_ref, g_ref, w_ref = refs[:3]
        ho_ref, gate_ref, up_ref = refs[3 + nc:6 + nc]
        i = pl.program_id(0)
        if has_c:
            comm = (refs[3:3 + nc], refs[6 + nc:6 + 2 * nc], refs[6 + 2 * nc:], *_route_fns(carry))

            @pl.when(i == 0)
            def _():
                _exchange_start(*comm)

        x = h_ref[...]
        u = _rms_fwd(x, g_ref[...]).astype(BF16)
        acc = None
        for f in range(nf):
            cols = slice(f * tf, (f + 1) * tf)
            gate = _dot_nt(u, w_ref[0, cols, :])
            up = _dot_nt(u, w_ref[1, cols, :])
            gate_ref[:, cols] = gate.astype(BF16)
            up_ref[:, cols] = up.astype(BF16)
            part = _dot((gate * _sigmoid(gate) * up).astype(BF16), w_ref[2, cols, :])
            acc = part if acc is None else acc + part
        ho_ref[...] = x + 0.5 * acc

        if has_c:
            @pl.when(i == nt - 1)
            def _():
                _exchange_wait(*comm)

    row = pl.BlockSpec((tm, D), lambda i: (i, 0))
    wide = pl.BlockSpec((tm, F), lambda i: (i, 0))
    return pl.pallas_call(
        body, name="ffn_fwd_gather" if has_c else "ffn_fwd", grid=(nt,),
        in_specs=[row, _full((1, D)), _resident(w3.shape)] + c_specs,
        out_specs=[row, wide, wide] + c_specs,
        out_shape=[jax.ShapeDtypeStruct((T, D), F32), jax.ShapeDtypeStruct((T, F), BF16),
                   jax.ShapeDtypeStruct((T, F), BF16)] + c_out,
        scratch_shapes=c_sems,
        compiler_params=_cparams("arbitrary"),
    )(h, g, w3, *c_in)


def _ffn_bwd_dx(dho, h, g, gate, up, w3, carry=()):
    T, D = h.shape
    F = w3.shape[1]
    tm, tf = min(TOKEN_TILE // 2, T), F // 2
    nf = F // tf
    nt = T // tm
    nc = len(carry)
    has_c = nc > 0
    c_in, c_specs, c_out, c_sems = _carried(carry)

    def body(*refs):
        dho_ref, h_ref, g_ref, gate_ref, up_ref, w_ref = refs[:6]
        dhi_ref, dgate_ref, dup_ref, u_ref, dy_ref, dg_ref = refs[6 + nc:12 + nc]
        i = pl.program_id(0)
        if has_c:
            comm = (refs[6:6 + nc], refs[12 + nc:12 + 2 * nc], refs[12 + 2 * nc:], *_route_fns(carry))

            @pl.when(i == 0)
            def _():
                _exchange_start(*comm)

        dho = dho_ref[...]
        x = h_ref[...]
        dy = (0.5 * dho).astype(BF16)
        dy_ref[...] = dy
        u_ref[...] = _rms_fwd(x, g_ref[...]).astype(BF16)
        acc = None
        for f in range(nf):
            cols = slice(f * tf, (f + 1) * tf)
            dact = _dot_nt(dy, w_ref[2, cols, :])
            gt = gate_ref[:, cols].astype(F32)
            sig = _sigmoid(gt)
            dup = (dact * (gt * sig)).astype(BF16)
            dgate = (dact * up_ref[:, cols].astype(F32) * (sig * (1.0 + gt * (1.0 - sig)))).astype(BF16)
            dup_ref[:, cols] = dup
            dgate_ref[:, cols] = dgate
            part = _dot(dgate, w_ref[0, cols, :]) + _dot(dup, w_ref[1, cols, :])
            acc = part if acc is None else acc + part
        dx, dg = _rms_bwd(acc, x, g_ref[...])
        dhi_ref[...] = dho + dx

        @pl.when(i == 0)
        def _():
            dg_ref[...] = jnp.zeros_like(dg_ref)

        dg_ref[...] += dg

        if has_c:
            @pl.when(i == nt - 1)
            def _():
                _exchange_wait(*comm)

    row = pl.BlockSpec((tm, D), lambda i: (i, 0))
    wide = pl.BlockSpec((tm, F), lambda i: (i, 0))
    return pl.pallas_call(
        body, name="ffn_bwd_dx_scatter" if has_c else "ffn_bwd_dx", grid=(nt,),
        in_specs=[row, row, _full((1, D)), wide, wide, _resident(w3.shape)] + c_specs,
        out_specs=[row, wide, wide, row, row, _full((1, D))] + c_specs,
        out_shape=[jax.ShapeDtypeStruct((T, D), F32), jax.ShapeDtypeStruct((T, F), BF16),
                   jax.ShapeDtypeStruct((T, F), BF16), jax.ShapeDtypeStruct((T, D), BF16),
                   jax.ShapeDtypeStruct((T, D), BF16), jax.ShapeDtypeStruct((1, D), F32)] + c_out,
        scratch_shapes=c_sems,
        compiler_params=_cparams("arbitrary"),
    )(dho, h, g, gate, up, w3, *c_in)


def _ffn_bwd_dw(dgate, dup, gate, up, u, dy):
    T, F = gate.shape
    D = u.shape[1]
    tfw = F // 2
    tk = min(TOKEN_TILE, T)
    nk = T // tk

    def body(dgate_ref, dup_ref, gate_ref, up_ref, u_ref, dy_ref, out_ref, acc_sc):
        k = pl.program_id(1)

        @pl.when(k == 0)
        def _():
            acc_sc[...] = jnp.zeros_like(acc_sc)

        uu = u_ref[...]
        acc_sc[0] += _dot_tn(dgate_ref[...], uu)
        acc_sc[1] += _dot_tn(dup_ref[...], uu)
        gt = gate_ref[...].astype(F32)
        act = (gt * _sigmoid(gt) * up_ref[...].astype(F32)).astype(BF16)
        acc_sc[2] += _dot_tn(act, dy_ref[...])

        @pl.when(k == nk - 1)
        def _():
            out_ref[...] = acc_sc[...].astype(BF16)

    blk = pl.BlockSpec((tk, tfw), lambda j, k: (k, j))
    row = pl.BlockSpec((tk, D), lambda j, k: (k, 0))
    return pl.pallas_call(
        body, name="ffn_bwd_dw", grid=(F // tfw, nk),
        in_specs=[blk, blk, blk, blk, row, row],
        out_specs=pl.BlockSpec((3, tfw, D), lambda j, k: (0, j, 0)),
        out_shape=jax.ShapeDtypeStruct((3, F, D), BF16),
        scratch_shapes=[pltpu.VMEM((3, tfw, D), F32)],
        compiler_params=_cparams("parallel", "arbitrary"),
    )(dgate, dup, gate, up, u, dy)


def _mm_rows(a, b, *, nt, out_dtype, norm_g=None, res=None, name):
    T, K = a.shape
    N = b.shape[0] if nt else b.shape[1]
    tm = min(TOKEN_TILE, T)
    has_g, has_r = norm_g is not None, res is not None

    def body(*refs):
        a_ref, b_ref = refs[0], refs[1]
        o_ref = refs[-1]
        x = a_ref[...]
        if has_g:
            x = _rms_fwd(x, refs[2][...])
        x = x.astype(BF16)
        acc = _dot_nt(x, b_ref[...]) if nt else _dot(x, b_ref[...])
        if has_r:
            acc = refs[2 + has_g][...] + acc
        o_ref[...] = acc.astype(out_dtype)

    ins, specs = [a, b], [pl.BlockSpec((tm, K), lambda i: (i, 0)), _full(b.shape)]
    if has_g:
        ins.append(norm_g)
        specs.append(_full((1, K)))
    if has_r:
        ins.append(res)
        specs.append(pl.BlockSpec((tm, N), lambda i: (i, 0)))
    return pl.pallas_call(
        body, name=name, grid=(T // tm,), in_specs=specs,
        out_specs=pl.BlockSpec((tm, N), lambda i: (i, 0)),
        out_shape=jax.ShapeDtypeStruct((T, N), out_dtype),
        compiler_params=_cparams("parallel"),
    )(*ins)


def _mm_tn(a, b, *, norm_g=None, name):
    T, M = a.shape
    N = b.shape[1]
    tk = min(TOKEN_TILE, T)
    has_g = norm_g is not None

    def body(*refs):
        a_ref, b_ref, o_ref = refs[0], refs[1], refs[-1]

        @pl.when(pl.program_id(0) == 0)
        def _():
            o_ref[...] = jnp.zeros_like(o_ref)

        x = a_ref[...]
        if has_g:
            x = _rms_fwd(x, refs[2][...])
        o_ref[...] += _dot_tn(x.astype(BF16), b_ref[...].astype(BF16))

    ins = [a, b]
    specs = [pl.BlockSpec((tk, M), lambda k: (k, 0)), pl.BlockSpec((tk, N), lambda k: (k, 0))]
    if has_g:
        ins.append(norm_g)
        specs.append(_full((1, M)))
    return pl.pallas_call(
        body, name=name, grid=(T // tk,), in_specs=specs, out_specs=_full((M, N)),
        out_shape=jax.ShapeDtypeStruct((M, N), F32),
        compiler_params=_cparams("arbitrary"),
    )(*ins)


def _proj_bwd(dz, w, h, g, dh, name):
    T, D = h.shape
    N = w.shape[1]
    tm = min(TOKEN_TILE, T)

    def body(dz_ref, w_ref, h_ref, g_ref, dh_ref, o_ref, dg_ref):
        du = _dot_nt(dz_ref[...].astype(BF16), w_ref[...])
        dx, dg = _rms_bwd(du, h_ref[...], g_ref[...])
        o_ref[...] = dh_ref[...] + dx

        @pl.when(pl.program_id(0) == 0)
        def _():
            dg_ref[...] = jnp.zeros_like(dg_ref)

        dg_ref[...] += dg

    row = pl.BlockSpec((tm, D), lambda i: (i, 0))
    return pl.pallas_call(
        body, name=name, grid=(T // tm,),
        in_specs=[pl.BlockSpec((tm, N), lambda i: (i, 0)), _full((D, N)), row, _full((1, D)), row],
        out_specs=[row, _full((1, D))],
        out_shape=[jax.ShapeDtypeStruct((T, D), F32), jax.ShapeDtypeStruct((1, D), F32)],
        compiler_params=_cparams("arbitrary"),
    )(dz, w, h, g, dh)


def _pool_bands(tm):
    r = np.arange(tm)[:, None]
    c = np.arange(tm)[None, :]
    j = np.arange(POOL_HALO)[None, :]
    main, halo, main_t, halo_t = [], [], [], []
    for w in POOL_WINDOWS:
        main.append(((r - c >= 0) & (r - c < w)) / w)
        halo.append((r + POOL_HALO - j < w) / w)
        main_t.append(((c - r >= 0) & (c - r < w)) / w)
        halo_t.append((tm + j - r < w) / w)
    return tuple(jnp.asarray(np.stack(m), BF16) for m in (main, halo, main_t, halo_t))


def _pool_count_scale(i, tm, w):
    t = i * tm + lax.broadcasted_iota(jnp.int32, (tm, 1), 0)
    return w / jnp.minimum(t + 1, w).astype(F32)


def _pool_fwd(h, g, wp, scale):
    T, D = h.shape
    G, dg = len(POOL_WINDOWS), D // len(POOL_WINDOWS)
    tm = min(TOKEN_TILE, T)
    hb = tm // POOL_HALO
    bm, bh, _, _ = _pool_bands(tm)

    def body(h_ref, hh_ref, g_ref, wp_ref, sc_ref, bm_ref, bh_ref, ho_ref, y_ref):
        i = pl.program_id(0)
        x = h_ref[...]
        u = _rms_fwd(x, g_ref[...])
        uh = _rms_fwd(hh_ref[...], g_ref[...]) * (i > 0).astype(F32)
        for gi, w in enumerate(POOL_WINDOWS):
            cols = slice(gi * dg, (gi + 1) * dg)
            ug = u[:, cols]
            hi, lo = _split_bf16(ug)
            hhi, hlo = _split_bf16(uh[:, cols])
            s = (_dot(bm_ref[gi], hi) + _dot(bm_ref[gi], lo)
                 + _dot(bh_ref[gi], hhi) + _dot(bh_ref[gi], hlo))
            y = (s * _pool_count_scale(i, tm, w) - ug).astype(BF16)
            y_ref[:, cols] = y
            ho_ref[:, cols] = x[:, cols] + _dot(y, wp_ref[gi]) * sc_ref[:, cols]

    row = pl.BlockSpec((tm, D), lambda i: (i, 0))
    return pl.pallas_call(
        body, name="pool_fwd", grid=(T // tm,),
        in_specs=[row, pl.BlockSpec((POOL_HALO, D), lambda i: (jnp.maximum(i * hb - 1, 0), 0)),
                  _full((1, D)), _full((G, dg, dg)), _full((1, D)),
                  _full((G, tm, tm)), _full((G, tm, POOL_HALO))],
        out_specs=[row, row],
        out_shape=[jax.ShapeDtypeStruct((T, D), F32), jax.ShapeDtypeStruct((T, D), BF16)],
        compiler_params=_cparams("parallel"),
    )(h, h, g, wp, scale, bm, bh)


def _pool_bwd(dh, h, g, y, wp, scale):
    T, D = h.shape
    G, dg = len(POOL_WINDOWS), D // len(POOL_WINDOWS)
    tm = min(TOKEN_TILE, T)
    hb = tm // POOL_HALO
    nt = T // tm
    _, _, bmt, bht = _pool_bands(tm)

    def body(dh_ref, dhn_ref, h_ref, g_ref, y_ref, wp_ref, sc_ref, bmt_ref, bht_ref,
             o_ref, dg_ref, dwp_ref, dsc_ref, du_sc):
        i = pl.program_id(0)

        @pl.when(i == 0)
        def _():
            dg_ref[...] = jnp.zeros_like(dg_ref)
            dwp_ref[...] = jnp.zeros_like(dwp_ref)
            dsc_ref[...] = jnp.zeros_like(dsc_ref)

        dho = dh_ref[...]
        dz = dho * sc_ref[...]
        dzn = dhn_ref[...] * sc_ref[...] * (i < nt - 1).astype(F32)
        for gi, w in enumerate(POOL_WINDOWS):
            cols = slice(gi * dg, (gi + 1) * dg)
            yg = y_ref[:, cols]
            dzg = dz[:, cols].astype(BF16)
            dsc_ref[:, cols] += jnp.sum(dho[:, cols] * _dot(yg, wp_ref[gi]), axis=0, keepdims=True)
            dwp_ref[gi] += _dot_tn(yg, dzg)
            dy = _dot_nt(dzg, wp_ref[gi])
            dyn = _dot_nt(dzn[:, cols].astype(BF16), wp_ref[gi])
            hi, lo = _split_bf16(dy * _pool_count_scale(i, tm, w))
            nhi, nlo = _split_bf16(dyn)
            du_sc[:, cols] = (_dot(bmt_ref[gi], hi) + _dot(bmt_ref[gi], lo)
                              + _dot(bht_ref[gi], nhi) + _dot(bht_ref[gi], nlo) - dy)
        dx, dgp = _rms_bwd(du_sc[...], h_ref[...], g_ref[...])
        o_ref[...] = dho + dx
        dg_ref[...] += dgp

    row = pl.BlockSpec((tm, D), lambda i: (i, 0))
    return pl.pallas_call(
        body, name="pool_bwd", grid=(nt,),
        in_specs=[row, pl.BlockSpec((POOL_HALO, D), lambda i: (jnp.minimum((i + 1) * hb, T // POOL_HALO - 1), 0)),
                  row, _full((1, D)), row, _full((G, dg, dg)), _full((1, D)),
                  _full((G, tm, tm)), _full((G, tm, POOL_HALO))],
        out_specs=[row, _full((1, D)), _full((G, dg, dg)), _full((1, D))],
        out_shape=[jax.ShapeDtypeStruct((T, D), F32), jax.ShapeDtypeStruct((1, D), F32),
                   jax.ShapeDtypeStruct((G, dg, dg), F32), jax.ShapeDtypeStruct((1, D), F32)],
        scratch_shapes=[pltpu.VMEM((tm, D), F32)],
        compiler_params=_cparams("arbitrary"),
    )(dh, dh, h, g, y, wp, scale, bmt, bht)


def _rope_tables(T):
    pos = jnp.arange(T, dtype=F32)
    inv_freq = ROPE_THETA ** (-jnp.arange(0, D_ROPE, 2, dtype=F32) / D_ROPE)
    ang = pos[:, None] * inv_freq[None, :]
    cos2 = jnp.tile(jnp.cos(ang), (1, 2))
    sin2 = jnp.tile(jnp.sin(ang), (1, 2))
    pad = jnp.zeros((T, D_HEAD_PAD - D_QK), F32)
    ca_q = jnp.concatenate([jnp.ones((T, D_NOPE), F32), cos2, pad], axis=1)
    ca_k = jnp.concatenate([jnp.zeros((T, D_NOPE), F32), cos2, pad], axis=1)
    sb = jnp.concatenate([jnp.zeros((T, D_NOPE), F32), sin2, pad], axis=1)
    return ca_q, ca_k, sb


def _rope_weight_pair(w):
    half = D_ROPE // 2
    z_pad = jnp.zeros(w.shape[:-1] + (D_HEAD_PAD - D_QK,), w.dtype)
    z_nope = jnp.zeros(w.shape[:-1] + (D_NOPE,), w.dtype)
    wa = jnp.concatenate([w, z_pad], axis=-1)
    wb = jnp.concatenate([z_nope, -w[..., D_NOPE + half:], w[..., D_NOPE:D_NOPE + half], z_pad], axis=-1)
    return wa, wb


def _rope_weight_pair_grad(dwa, dwb):
    half = D_ROPE // 2
    d1 = dwa[..., D_NOPE:D_NOPE + half] + dwb[..., D_NOPE + half:D_QK]
    d2 = dwa[..., D_NOPE + half:D_QK] - dwb[..., D_NOPE:D_NOPE + half]
    return jnp.concatenate([dwa[..., :D_NOPE], d1, d2], axis=-1)


def _q_proj(cq, qg, wa, wb, ca, sb):
    T, R = cq.shape
    tm = min(TOKEN_TILE, T)
    P = D_HEAD_PAD
    GP = HEAD_GROUP * P

    def body(cq_ref, qg_ref, wa_ref, wb_ref, ca_ref, sb_ref, q_ref):
        c = _rms_fwd(cq_ref[...], qg_ref[...]).astype(BF16)
        ca = jnp.tile(ca_ref[...], (1, HEAD_GROUP))
        sb = jnp.tile(sb_ref[...], (1, HEAD_GROUP))
        q_ref[...] = (_dot(c, wa_ref[...]) * ca + _dot(c, wb_ref[...]) * sb).astype(BF16)

    tok = pl.BlockSpec((tm, P), lambda i, hh: (i, 0))
    wsp = pl.BlockSpec((R, GP), lambda i, hh: (0, hh))
    return pl.pallas_call(
        body, name="q_proj", grid=(T // tm, N_HEADS // HEAD_GROUP),
        in_specs=[pl.BlockSpec((tm, R), lambda i, hh: (i, 0)), _full((1, R)), wsp, wsp, tok, tok],
        out_specs=pl.BlockSpec((tm, GP), lambda i, hh: (i, hh)),
        out_shape=jax.ShapeDtypeStruct((T, N_HEADS * P), BF16),
        compiler_params=_cparams("parallel", "arbitrary"),
    )(cq, qg, wa, wb, ca, sb)


def _q_proj_bwd(dq, cq, qg, wa, wb, ca, sb):
    T, R = cq.shape
    tm = min(TOKEN_TILE, T)
    P = D_HEAD_PAD

    def body(dq_ref, cq_ref, qg_ref, wa_ref, wb_ref, ca_ref, sb_ref,
             da_ref, db_ref, cqn_ref, dcq_ref, dqg_ref, acc_sc):
        i, hh = pl.program_id(0), pl.program_id(1)

        @pl.when(hh == 0)
        def _():
            acc_sc[...] = jnp.zeros_like(acc_sc)
            cqn_ref[...] = _rms_fwd(cq_ref[...], qg_ref[...]).astype(BF16)

        d = dq_ref[...].T
        da = (d * jnp.tile(ca_ref[...], (1, HEAD_GROUP))).astype(BF16)
        db = (d * jnp.tile(sb_ref[...], (1, HEAD_GROUP))).astype(BF16)
        da_ref[...] = da
        db_ref[...] = db
        acc_sc[...] += _dot_nt(da, wa_ref[...]) + _dot_nt(db, wb_ref[...])

        @pl.when(hh == N_HEADS // HEAD_GROUP - 1)
        def _():
            dx, dg = _rms_bwd(acc_sc[...], cq_ref[...], qg_ref[...])
            dcq_ref[...] = dx

            @pl.when(i == 0)
            def _():
                dqg_ref[...] = jnp.zeros_like(dqg_ref)

            dqg_ref[...] += dg

    GP = HEAD_GROUP * P
    tok = pl.BlockSpec((tm, P), lambda i, hh: (i, 0))
    hd = pl.BlockSpec((tm, GP), lambda i, hh: (i, hh))
    wsp = pl.BlockSpec((R, GP), lambda i, hh: (0, hh))
    rr = pl.BlockSpec((tm, R), lambda i, hh: (i, 0))
    return pl.pallas_call(
        body, name="q_proj_bwd", grid=(T // tm, N_HEADS // HEAD_GROUP),
        in_specs=[pl.BlockSpec((GP, tm), lambda i, hh: (hh, i)), rr, _full((1, R)), wsp, wsp, tok, tok],
        out_specs=[hd, hd, rr, rr, _full((1, R))],
        out_shape=[jax.ShapeDtypeStruct((T, N_HEADS * P), BF16), jax.ShapeDtypeStruct((T, N_HEADS * P), BF16),
                   jax.ShapeDtypeStruct((T, R), BF16), jax.ShapeDtypeStruct((T, R), F32),
                   jax.ShapeDtypeStruct((1, R), F32)],
        scratch_shapes=[pltpu.VMEM((tm, R), F32)],
        compiler_params=_cparams("arbitrary", "arbitrary"),
    )(dq, cq, qg, wa, wb, ca, sb)


def _kv_proj(h, g_in, wka, wkb, g_c, wuk, wuv, ca, sb):
    T, D = h.shape
    tm = min(TOKEN_TILE, T)
    P, C = D_HEAD_PAD, D_NOPE

    def body(h_ref, gi_ref, wka_ref, wkb_ref, gc_ref, wuk_ref, wuv_ref, ca_ref, sb_ref, k_ref, v_ref, craw_ref):
        u = _rms_fwd(h_ref[...], gi_ref[...]).astype(BF16)
        kva = _dot(u, wka_ref[...])
        kvb = _dot(u, wkb_ref[...])
        craw = kva[:, :C]
        craw_ref[...] = craw
        c = _rms_fwd(craw, gc_ref[...]).astype(BF16)
        kr = kva * ca_ref[...] + kvb * sb_ref[...]
        kn = _dot(c, wuk_ref[...])
        for hh in range(N_HEADS):
            k_ref[:, hh * P:(hh + 1) * P] = (kn[:, hh * P:(hh + 1) * P] + kr).astype(BF16)
        v_ref[...] = _dot(c, wuv_ref[...]).astype(BF16)

    tok = pl.BlockSpec((tm, P), lambda i: (i, 0))
    return pl.pallas_call(
        body, name="kv_proj", grid=(T // tm,),
        in_specs=[pl.BlockSpec((tm, D), lambda i: (i, 0)), _full((1, D)), _full((D, P)), _full((D, P)),
                  _full((1, C)), _full(wuk.shape), _full(wuv.shape), tok, tok],
        out_specs=[pl.BlockSpec((tm, N_HEADS * P), lambda i: (i, 0)),
                   pl.BlockSpec((tm, N_HEADS * D_V), lambda i: (i, 0)), pl.BlockSpec((tm, C), lambda i: (i, 0))],
        out_shape=[jax.ShapeDtypeStruct((T, N_HEADS * P), BF16), jax.ShapeDtypeStruct((T, N_HEADS * D_V), BF16),
                   jax.ShapeDtypeStruct((T, C), F32)],
        compiler_params=_cparams("parallel"),
    )(h, g_in, wka, wkb, g_c, wuk, wuv, ca, sb)


def _kv_proj_bwd(dks, dvs, dh, h, g_in, wka, wkb, craw, g_c, wuk, wuv, ca, sb):
    T, D = h.shape
    tm = min(TOKEN_TILE // 2, T)
    P, C = D_HEAD_PAD, D_NOPE
    nl = len(dks)

    def body(*refs):
        dk_refs, dv_refs = refs[:nl], refs[nl:2 * nl]
        (dh_ref, h_ref, gi_ref, wka_ref, wkb_ref, craw_ref, gc_ref, wuk_ref, wuv_ref,
         ca_ref, sb_ref, o_ref, dgi_ref, dwka_ref, dwkb_ref, dgc_ref, dwuk_ref, dwuv_ref) = refs[2 * nl:]

        @pl.when(pl.program_id(0) == 0)
        def _():
            for r in (dgi_ref, dwka_ref, dwkb_ref, dgc_ref, dwuk_ref, dwuv_ref):
                r[...] = jnp.zeros_like(r)

        x = h_ref[...]
        u = _rms_fwd(x, gi_ref[...]).astype(BF16)
        craw = craw_ref[...]
        c = _rms_fwd(craw, gc_ref[...]).astype(BF16)
        dkf = sum(r[...] for r in dk_refs[1:]) + dk_refs[0][...]
        dkb = dkf.astype(BF16)
        dvb = (sum(r[...] for r in dv_refs[1:]) + dv_refs[0][...]).astype(BF16)
        dwuk_ref[...] += _dot_tn(c, dkb)
        dwuv_ref[...] += _dot_tn(c, dvb)
        dc = _dot_nt(dkb, wuk_ref[...]) + _dot_nt(dvb, wuv_ref[...])
        dkr = dkf[:, :P]
        for hh in range(1, N_HEADS):
            dkr = dkr + dkf[:, hh * P:(hh + 1) * P]
        dcraw, dgc = _rms_bwd(dc, craw, gc_ref[...])
        dgc_ref[...] += dgc
        dkva = jnp.concatenate([dcraw, (dkr * ca_ref[...])[:, C:]], axis=1).astype(BF16)
        dkvb = (dkr * sb_ref[...]).astype(BF16)
        dwka_ref[...] += _dot_tn(u, dkva)
        dwkb_ref[...] += _dot_tn(u, dkvb)
        du = _dot_nt(dkva, wka_ref[...]) + _dot_nt(dkvb, wkb_ref[...])
        dx, dgi = _rms_bwd(du, x, gi_ref[...])
        dgi_ref[...] += dgi
        o_ref[...] = dh_ref[...] + dx

    row = pl.BlockSpec((tm, D), lambda i: (i, 0))
    tok = pl.BlockSpec((tm, P), lambda i: (i, 0))
    return pl.pallas_call(
        body, name="kv_proj_bwd", grid=(T // tm,),
        in_specs=[pl.BlockSpec((tm, N_HEADS * P), lambda i: (i, 0))] * nl
        + [pl.BlockSpec((tm, N_HEADS * D_V), lambda i: (i, 0))] * nl
        + [row, row, _full((1, D)), _full((D, P)), _full((D, P)), pl.BlockSpec((tm, C), lambda i: (i, 0)),
           _full((1, C)), _full(wuk.shape), _full(wuv.shape), tok, tok],
        out_specs=[row, _full((1, D)), _full((D, P)), _full((D, P)), _full((1, C)),
                   _full(wuk.shape), _full(wuv.shape)],
        out_shape=[jax.ShapeDtypeStruct((T, D), F32), jax.ShapeDtypeStruct((1, D), F32),
                   jax.ShapeDtypeStruct((D, P), F32), jax.ShapeDtypeStruct((D, P), F32),
                   jax.ShapeDtypeStruct((1, C), F32), jax.ShapeDtypeStruct(wuk.shape, F32),
                   jax.ShapeDtypeStruct(wuv.shape, F32)],
        compiler_params=_cparams("arbitrary"),
    )(*dks, *dvs, dh, h, g_in, wka, wkb, craw, g_c, wuk, wuv, ca, sb)


_ATTN_SCALE = D_QK ** -0.5
_LOG2_E = 1.4426950408889634
_LN_2 = 0.6931471805599453
_ATTN_SCALE_LOG2 = _ATTN_SCALE * _LOG2_E


def _flash_fwd(q, k, v):
    T = q.shape[0]
    t = min(ATTN_TILE, T // 2)
    tq = 2 * t
    P = D_HEAD_PAD

    def body(q_ref, k_ref, v_ref, o_ref, lse_ref, m_sc, l_sc, acc_sc, s0_sc, s1_sc, p0_sc, p1_sc, a0_sc, a1_sc):
        qi = pl.program_id(1)
        n = 2 * (qi + 1)
        s_sc, p_sc, a_sc = (s0_sc, s1_sc), (p0_sc, p1_sc), (a0_sc, a1_sc)
        m_sc[...] = jnp.full_like(m_sc, NEG_BIG)
        l_sc[...] = jnp.zeros_like(l_sc)
        acc_sc[...] = jnp.zeros_like(acc_sc)

        def rows_of(c):
            return pl.ds(pl.multiple_of(c * t, t), t)

        def scores(c, slot):
            s_sc[slot][...] = _dot_nt(k_ref[rows_of(c), :], q_ref[...])

        def softmax(slot, key_offset):
            s_t = s_sc[slot][...]
            if key_offset is not None:
                rows = lax.broadcasted_iota(jnp.int32, (t, tq), 0) + key_offset
                s_t = jnp.where(rows <= lax.broadcasted_iota(jnp.int32, (t, tq), 1), s_t, NEG_BIG)
            m_prev = m_sc[...]
            m_new = jnp.maximum(m_prev, jnp.max(s_t, axis=0, keepdims=True))
            p_t = jnp.exp2(s_t - m_new)
            alpha = jnp.exp2(m_prev - m_new)
            l_sc[...] = alpha * l_sc[...] + jnp.sum(p_t, axis=0, keepdims=True)
            m_sc[...] = m_new
            p_sc[slot][...] = p_t.astype(BF16)
            a_sc[slot][...] = alpha

        def values(c, slot):
            acc_sc[...] = a_sc[slot][...] * acc_sc[...] + _dot_tn(v_ref[rows_of(c), :], p_sc[slot][...])

        def stage(c, slot, first=False, last=False, key_offset=None):
            if not first:
                values(c - 1, 1 - slot)
            if not last:
                scores(c + 1, 1 - slot)
            softmax(slot, key_offset)

        def drain():
            stage(n - 2, 0, key_offset=0)
            stage(n - 1, 1, last=True, key_offset=t)
            values(n - 1, 1)

        scores(0, 0)

        @pl.when(qi == 0)
        def _():
            stage(0, 0, first=True, key_offset=0)
            stage(1, 1, last=True, key_offset=t)
            values(1, 1)

        @pl.when(qi > 0)
        def _():
            stage(0, 0, first=True)

            def pair(j, carry):
                stage(1 + 2 * j, 1)
                stage(2 + 2 * j, 0)
                return carry

            lax.fori_loop(0, qi - 1, pair, 0)
            stage(n - 3, 1)
            drain()

        l = l_sc[...]
        o_ref[...] = (acc_sc[...] / l).T.astype(BF16)
        lse_ref[...] = m_sc[...] + jnp.log(l) * _LOG2_E

    return pl.pallas_call(
        body, name="flash_fwd", grid=(N_HEADS, T // tq),
        in_specs=[pl.BlockSpec((tq, P), lambda hh, i: (i, hh)), pl.BlockSpec((T, P), lambda hh, i: (0, hh)),
                  pl.BlockSpec((T, D_V), lambda hh, i: (0, hh))],
        out_specs=[pl.BlockSpec((tq, D_V), lambda hh, i: (i, hh)),
                   pl.BlockSpec((None, None, 1, tq), lambda hh, i: (hh, i, 0, 0))],
        out_shape=[jax.ShapeDtypeStruct((T, N_HEADS * D_V), BF16),
                   jax.ShapeDtypeStruct((N_HEADS, T // tq, 1, tq), F32)],
        scratch_shapes=[pltpu.VMEM((1, tq), F32), pltpu.VMEM((1, tq), F32), pltpu.VMEM((D_V, tq), F32),
                        pltpu.VMEM((t, tq), F32), pltpu.VMEM((t, tq), F32), pltpu.VMEM((t, tq), BF16),
                        pltpu.VMEM((t, tq), BF16), pltpu.VMEM((1, tq), F32), pltpu.VMEM((1, tq), F32)],
        compiler_params=_cparams("parallel", "arbitrary"),
    )(q, k, v)


def _attn_delta(o, do):
    T = o.shape[0]
    t = min(ATTN_TILE, T)

    def body(o_ref, do_ref, out_ref):
        ones = jnp.ones((8, D_V), BF16)
        for hh in range(N_HEADS):
            cols = slice(hh * D_V, (hh + 1) * D_V)
            hi, lo = _split_bf16(o_ref[:, cols].astype(F32) * do_ref[:, cols].astype(F32))
            out_ref[hh] = (_dot_nt(ones, hi) + _dot_nt(ones, lo))[0:1]

    tok = pl.BlockSpec((t, N_HEADS * D_V), lambda i: (i, 0))
    return pl.pallas_call(
        body, name="attn_delta", grid=(T // t,), in_specs=[tok, tok],
        out_specs=pl.BlockSpec((N_HEADS, None, 1, t), lambda i: (0, i, 0, 0)),
        out_shape=jax.ShapeDtypeStruct((N_HEADS, T // t, 1, t), F32),
        compiler_params=_cparams("parallel"),
    )(o, do)


def _flash_bwd(q, k, v, do, lse_row, delta_row):
    T = q.shape[0]
    t = min(ATTN_TILE, T // 2)
    tk = 2 * t
    nq = T // t
    P = D_HEAD_PAD

    def body(k_ref, v_ref, q_ref, do_ref, lse_ref, delta_ref, dqt_ref, dk_ref, dv_ref, dqt_sc, kt_sc,
             s0_sc, s1_sc, dp0_sc, dp1_sc, p0_sc, p1_sc, ds0_sc, ds1_sc):
        ki = pl.program_id(1)
        q0 = 2 * ki
        n = nq - q0
        s_sc, dp_sc, p_sc, ds_sc = (s0_sc, s1_sc), (dp0_sc, dp1_sc), (p0_sc, p1_sc), (ds0_sc, ds1_sc)
        kt_sc[...] = k_ref[...].astype(F32).T.astype(BF16)
        dk_ref[...] = jnp.zeros_like(dk_ref)
        dv_ref[...] = jnp.zeros_like(dv_ref)

        @pl.when(ki == 0)
        def _():
            dqt_sc[...] = jnp.zeros_like(dqt_sc)

        def rows_of(c):
            return pl.ds(pl.multiple_of((q0 + c) * t, t), t)

        def products(c, slot):
            s_sc[slot][...] = _dot_nt(k_ref[...], q_ref[rows_of(c), :])
            dp_sc[slot][...] = _dot_nt(v_ref[...], do_ref[rows_of(c), :])

        def elementwise(c, slot, query_offset):
            p_t = jnp.exp2(s_sc[slot][...] - lse_ref[q0 + c])
            if query_offset is not None:
                cols = lax.broadcasted_iota(jnp.int32, (tk, t), 1) + query_offset
                p_t = jnp.where(lax.broadcasted_iota(jnp.int32, (tk, t), 0) <= cols, p_t, 0.0)
            p_sc[slot][...] = p_t.astype(BF16)
            ds_sc[slot][...] = (p_t * (dp_sc[slot][...] - delta_ref[q0 + c])).astype(BF16)

        def gradients(c, slot):
            dv_ref[...] += _dot(p_sc[slot][...], do_ref[rows_of(c), :])
            ds_t = ds_sc[slot][...]
            dk_ref[...] += _dot(ds_t, q_ref[rows_of(c), :])
            dqt_sc[q0 + c] += _dot(kt_sc[...], ds_t)

        def stage(c, slot, first=False, last=False, query_offset=None):
            if not first:
                gradients(c - 1, 1 - slot)
            if not last:
                products(c + 1, 1 - slot)
            elementwise(c, slot, query_offset)

        products(0, 0)

        @pl.when(n == 2)
        def _():
            stage(0, 0, first=True, query_offset=0)
            stage(1, 1, last=True, query_offset=t)
            gradients(1, 1)

        @pl.when(n > 2)
        def _():
            stage(0, 0, first=True, query_offset=0)
            stage(1, 1, query_offset=t)

            def pair(j, carry):
                stage(2 + 2 * j, 0)
                stage(3 + 2 * j, 1)
                return carry

            lax.fori_loop(0, (n - 4) // 2, pair, 0)
            stage(n - 2, 0)
            stage(n - 1, 1, last=True)
            gradients(n - 1, 1)

        dqt_ref[:, :t] = dqt_sc[q0] * _ATTN_SCALE
        dqt_ref[:, t:] = dqt_sc[q0 + 1] * _ATTN_SCALE
        dk_ref[...] = dk_ref[...] * _LN_2

    kb = pl.BlockSpec((tk, P), lambda hh, i: (i, hh))
    vb = pl.BlockSpec((tk, D_V), lambda hh, i: (i, hh))
    stat = pl.BlockSpec((None, nq, 1, t), lambda hh, i: (hh, 0, 0, 0))
    return pl.pallas_call(
        body, name="flash_bwd", grid=(N_HEADS, T // tk),
        in_specs=[kb, vb,
                  pl.BlockSpec((T, P), lambda hh, i: (0, hh), pipeline_mode=pl.Buffered(1)),
                  pl.BlockSpec((T, D_V), lambda hh, i: (0, hh), pipeline_mode=pl.Buffered(1)),
                  stat, stat],
        out_specs=[pl.BlockSpec((P, tk), lambda hh, i: (hh, i)), kb, vb],
        out_shape=[jax.ShapeDtypeStruct((N_HEADS * P, T), F32), jax.ShapeDtypeStruct((T, N_HEADS * P), F32),
                   jax.ShapeDtypeStruct((T, N_HEADS * D_V), F32)],
        scratch_shapes=[pltpu.VMEM((nq, P, t), F32), pltpu.VMEM((P, tk), BF16)]
        + [pltpu.VMEM((tk, t), F32)] * 4 + [pltpu.VMEM((tk, t), BF16)] * 4,
        compiler_params=_cparams("arbitrary", "arbitrary"),
    )(k, v, q, do, lse_row, delta_row)


def _loss_head(h, g, target):
    T, D = h.shape
    tm = min(TOKEN_TILE, T)

    def body(h_ref, g_ref, t_ref, dh_ref, loss_ref, dg_ref):
        @pl.when(pl.program_id(0) == 0)
        def _():
            loss_ref[...] = jnp.zeros_like(loss_ref)
            dg_ref[...] = jnp.zeros_like(dg_ref)

        x = h_ref[...]
        err = _rms_fwd(x, g_ref[...]) - t_ref[...]
        per_tok = jnp.mean(err * err, axis=-1, keepdims=True)
        loss_ref[...] += 0.5 * jnp.sum(per_tok, axis=0, keepdims=True)
        dx, dg = _rms_bwd(err * (1.0 / D), x, g_ref[...])
        dh_ref[...] = dx
        dg_ref[...] += dg

    row = pl.BlockSpec((tm, D), lambda i: (i, 0))
    return pl.pallas_call(
        body, name="loss_head", grid=(T // tm,),
        in_specs=[row, _full((1, D)), row], out_specs=[row, _full((1, 128)), _full((1, D))],
        out_shape=[jax.ShapeDtypeStruct((T, D), F32), jax.ShapeDtypeStruct((1, 128), F32),
                   jax.ShapeDtypeStruct((1, D), F32)],
        compiler_params=_cparams("arbitrary"),
    )(h, g, target)


def _sum_parts(parts, tr, name):
    _, R, C = parts.shape

    def body(p_ref, o_ref):
        acc = p_ref[0].astype(F32)
        for j in range(1, N_DEV):
            acc = acc + p_ref[j].astype(F32)
        o_ref[...] = acc

    return pl.pallas_call(
        body, name=name, grid=(R // tr,),
        in_specs=[pl.BlockSpec((N_DEV, tr, C), lambda i: (0, i, 0))],
        out_specs=pl.BlockSpec((tr, C), lambda i: (i, 0)),
        out_shape=jax.ShapeDtypeStruct((R, C), F32),
        compiler_params=_cparams("parallel"),
    )(parts)


def _adamw(w, g, m, v):
    R, C = w.shape
    tr = _row_tile(R, TOKEN_TILE)

    def body(w_ref, g_ref, m_ref, v_ref, d_ref, mo_ref, vo_ref):
        gg = g_ref[...]
        mn = ADAM_B1 * m_ref[...] + (1.0 - ADAM_B1) * gg
        vn = ADAM_B2 * v_ref[...] + (1.0 - ADAM_B2) * (gg * gg)
        m_hat = mn / (1.0 - ADAM_B1 ** ADAM_STEP)
        v_hat = vn / (1.0 - ADAM_B2 ** ADAM_STEP)
        d_ref[...] = -ADAM_LR * (m_hat / (jnp.sqrt(v_hat) + ADAM_EPS) + ADAM_WD * w_ref[...])
        mo_ref[...] = mn
        vo_ref[...] = vn

    blk = pl.BlockSpec((tr, C), lambda i: (i, 0))
    return pl.pallas_call(
        body, name="adamw", grid=(R // tr,), in_specs=[blk] * 4, out_specs=[blk] * 3,
        out_shape=[jax.ShapeDtypeStruct((R, C), F32)] * 3,
        compiler_params=_cparams("parallel"),
    )(w, g, m, v)


def _adamw_nd(w, g, m, v):
    shape = w.shape
    two_d = (1, shape[0]) if len(shape) == 1 else (int(np.prod(shape[:-1])), shape[-1])
    outs = _adamw(w.reshape(two_d), g.reshape(two_d), m.reshape(two_d), v.reshape(two_d))
    return tuple(o.reshape(shape) for o in outs)


def _f32_as_bf16_pairs(a):
    return lax.bitcast_convert_type(a, BF16).reshape(a.shape[:-1] + (a.shape[-1] * 2,))


def _bf16_pairs_as_f32(a):
    return lax.bitcast_convert_type(a.reshape(a.shape[:-1] + (a.shape[-1] // 2, 2)), F32)


def _pack_misc(w_o, w_dq, w_uq, w_dkv, pool_w, pool_scale):
    lead = w_o.shape[:-3]
    rows = [w_o, w_dq, w_uq, w_dkv, pool_w]
    flat = [r.astype(BF16).reshape(lead + (-1, REP_COLS)) for r in rows]
    ps = _f32_as_bf16_pairs(pool_scale.astype(F32)).reshape(lead + (1, -1))
    ps = jnp.concatenate([ps, jnp.zeros(lead + (1, REP_COLS - ps.shape[-1]), BF16)], axis=-1)
    used = sum(f.shape[-2] for f in flat) + 1
    pad = jnp.zeros(lead + (MISC_ROWS - used, REP_COLS), BF16)
    return jnp.concatenate(flat + [ps, pad], axis=-2)


def _unpack_misc(buf, shapes):
    out, r0 = [], 0
    for shp in shapes[:-1]:
        n = int(np.prod(shp)) // REP_COLS
        out.append(buf[:, r0:r0 + n].reshape((N_DEV,) + shp))
        r0 += n
    n_ps = int(np.prod(shapes[-1]))
    out.append(_bf16_pairs_as_f32(buf[:, r0, :2 * n_ps]).reshape((N_DEV,) + shapes[-1]))
    return out


def _cat_dev(a, axis):
    a = jnp.moveaxis(a, 0, axis)
    return a.reshape(a.shape[:axis] + (a.shape[axis] * a.shape[axis + 1],) + a.shape[axis + 2:])


def _split_dev(a, axis):
    a = a.reshape(a.shape[:axis] + (N_DEV, a.shape[axis] // N_DEV) + a.shape[axis + 1:])
    return jnp.moveaxis(a, axis, 0)


def kernel(x, ffn_pre_norm, ffn_pre_wg, ffn_pre_wu, ffn_pre_wd, mix_norm, ffn_post_norm, ffn_post_wg, ffn_post_wu, ffn_post_wd, pool_w, pool_scale, kv_in_norm, w_dkv, ckv_norm, w_uk, w_uv, q_lora_norm, w_dq, w_uq, w_o, final_norm, loss_target, m_ffn_pre_norm, m_ffn_pre_wg, m_ffn_pre_wu, m_ffn_pre_wd, m_mix_norm, m_ffn_post_norm, m_ffn_post_wg, m_ffn_post_wu, m_ffn_post_wd, m_pool_w, m_pool_scale, m_kv_in_norm, m_w_dkv, m_ckv_norm, m_w_uk, m_w_uv, m_q_lora_norm, m_w_dq, m_w_uq, m_w_o, m_final_norm, v_ffn_pre_norm, v_ffn_pre_wg, v_ffn_pre_wu, v_ffn_pre_wd, v_mix_norm, v_ffn_post_norm, v_ffn_post_wg, v_ffn_post_wu, v_ffn_post_wd, v_pool_w, v_pool_scale, v_kv_in_norm, v_w_dkv, v_ckv_norm, v_w_uk, v_w_uv, v_q_lora_norm, v_w_dq, v_w_uq, v_w_o, v_final_norm):
    weights = dict(ffn_pre_norm=ffn_pre_norm, ffn_pre_wg=ffn_pre_wg, ffn_pre_wu=ffn_pre_wu, ffn_pre_wd=ffn_pre_wd,
                   mix_norm=mix_norm, ffn_post_norm=ffn_post_norm, ffn_post_wg=ffn_post_wg,
                   ffn_post_wu=ffn_post_wu, ffn_post_wd=ffn_post_wd, pool_w=pool_w, pool_scale=pool_scale,
                   kv_in_norm=kv_in_norm, w_dkv=w_dkv, ckv_norm=ckv_norm, w_uk=w_uk, w_uv=w_uv,
                   q_lora_norm=q_lora_norm, w_dq=w_dq, w_uq=w_uq, w_o=w_o, final_norm=final_norm)
    moments_m = dict(ffn_pre_norm=m_ffn_pre_norm, ffn_pre_wg=m_ffn_pre_wg, ffn_pre_wu=m_ffn_pre_wu,
                     ffn_pre_wd=m_ffn_pre_wd, mix_norm=m_mix_norm, ffn_post_norm=m_ffn_post_norm,
                     ffn_post_wg=m_ffn_post_wg, ffn_post_wu=m_ffn_post_wu, ffn_post_wd=m_ffn_post_wd,
                     pool_w=m_pool_w, pool_scale=m_pool_scale, kv_in_norm=m_kv_in_norm, w_dkv=m_w_dkv,
                     ckv_norm=m_ckv_norm, w_uk=m_w_uk, w_uv=m_w_uv, q_lora_norm=m_q_lora_norm, w_dq=m_w_dq,
                     w_uq=m_w_uq, w_o=m_w_o, final_norm=m_final_norm)
    moments_v = dict(ffn_pre_norm=v_ffn_pre_norm, ffn_pre_wg=v_ffn_pre_wg, ffn_pre_wu=v_ffn_pre_wu,
                     ffn_pre_wd=v_ffn_pre_wd, mix_norm=v_mix_norm, ffn_post_norm=v_ffn_post_norm,
                     ffn_post_wg=v_ffn_post_wg, ffn_post_wu=v_ffn_post_wu, ffn_post_wd=v_ffn_post_wd,
                     pool_w=v_pool_w, pool_scale=v_pool_scale, kv_in_norm=v_kv_in_norm, w_dkv=v_w_dkv,
                     ckv_norm=v_ckv_norm, w_uk=v_w_uk, w_uv=v_w_uv, q_lora_norm=v_q_lora_norm, w_dq=v_w_dq,
                     w_uq=v_w_uq, w_o=v_w_o, final_norm=v_final_norm)
    order = list(weights)

    T, D = x.shape[1], x.shape[2]
    depth = ffn_pre_norm.shape[0]
    n_a = pool_w.shape[0]
    n_b = depth - n_a
    fs = ffn_pre_wd.shape[1]
    F = fs * N_DEV
    n_ffn = 2 * depth
    t_attn = min(ATTN_TILE, T)

    ffn_local = [
        jnp.stack([jnp.swapaxes(wg[l], 0, 1), jnp.swapaxes(wu[l], 0, 1), wd[l]]).astype(BF16)
        for l in range(depth)
        for wg, wu, wd in ((ffn_pre_wg, ffn_pre_wu, ffn_pre_wd), (ffn_post_wg, ffn_post_wu, ffn_post_wd))
    ]
    misc_local = _pack_misc(w_o, w_dq, w_uq.reshape(n_b, w_uq.shape[1], -1), w_dkv, pool_w, pool_scale)
    misc_shapes = (w_o.shape, w_dq.shape, (n_b, w_uq.shape[1], N_HEADS * D_QK), w_dkv.shape, pool_w.shape,
                   pool_scale.shape)
    (w0_all,) = _exchange([(ffn_local[0], "gather_mid")], "comm_all_gather")
    walls = [w0_all.reshape(3, F, D)] + [None] * (n_ffn - 1)

    def vec(a):
        return a.reshape(1, -1)

    def ffn_stage(e, h_in, norm):
        carry = [(ffn_local[e + 1], "gather_mid")] if e + 1 < n_ffn else []
        if e == 0:
            carry.append((misc_local, "to_all"))
        outs = _ffn_fwd(h_in, norm, walls[e], carry)
        if carry:
            walls[e + 1] = outs[3].reshape(3, F, D)
        return outs

    h = x.reshape(T, D)
    stage0 = ffn_stage(0, h, vec(ffn_pre_norm[0]))
    misc_all = stage0[4]
    o_blk, dq_blk, uq_blk, dkv_blk, pw_blk, ps_blk = _unpack_misc(misc_all, misc_shapes)
    w_o_f = _cat_dev(o_blk, 1)
    w_dq_f = _cat_dev(dq_blk, 1)
    w_uq_f = _cat_dev(uq_blk, 1).reshape(n_b, -1, N_HEADS, D_QK)
    w_dkv_f = _cat_dev(dkv_blk, 0)
    pool_w_f = _cat_dev(pw_blk, 2)
    pool_scale_f = _cat_dev(ps_blk, 1)
    rq = w_dq_f.shape[2]
    wqa, wqb = _rope_weight_pair(w_uq_f)
    wqa = wqa.reshape(n_b, rq, N_HEADS * D_HEAD_PAD)
    wqb = wqb.reshape(n_b, rq, N_HEADS * D_HEAD_PAD)
    wka, wkb = _rope_weight_pair(w_dkv_f)
    wuk = jnp.concatenate([w_uk, jnp.zeros_like(w_uk)], axis=-1).astype(BF16).reshape(D_NOPE, N_HEADS * D_HEAD_PAD)
    wuv = w_uv.astype(BF16).reshape(D_NOPE, N_HEADS * D_V)
    ca_q, ca_k, sb = _rope_tables(T)
    ca_q_scaled, sb_scaled = ca_q * _ATTN_SCALE_LOG2, sb * _ATTN_SCALE_LOG2

    saved = []
    k_all = v_all = craw = h_kv = None
    for l in range(depth):
        s = {"h0": h}
        h, s["g1"], s["u1"] = (stage0 if l == 0 else ffn_stage(2 * l, h, vec(ffn_pre_norm[l])))[:3]
        s["h1"] = h
        if l < n_a:
            h, s["y"] = _pool_fwd(h, vec(mix_norm[l]), pool_w_f[l], vec(pool_scale_f[l]))
        else:
            j = l - n_a
            s["cq"] = _mm_rows(h, w_dq_f[j], nt=False, out_dtype=F32, norm_g=vec(mix_norm[l]), name="q_down")
            s["q"] = _q_proj(s["cq"], vec(q_lora_norm[j]), wqa[j], wqb[j], ca_q_scaled, sb_scaled)
            s["o"], lse = _flash_fwd(s["q"], k_all, v_all)
            s["lse"] = lse.reshape(N_HEADS, T // t_attn, 1, t_attn)
            h = _mm_rows(s["o"], w_o_f[j], nt=False, out_dtype=F32, res=h, name="attn_out")
        s["h2"] = h
        h, s["g2"], s["u2"] = ffn_stage(2 * l + 1, h, vec(ffn_post_norm[l]))[:3]
        if l == n_a - 1:
            h_kv = h
            k_all, v_all, craw = _kv_proj(h, vec(kv_in_norm), wka, wkb, vec(ckv_norm), wuk, wuv, ca_k, sb)
        saved.append(s)

    dh, loss_part, d_final = _loss_head(h, vec(final_norm), loss_target.reshape(T, D))

    slabs, ffn_parts = [None] * n_ffn, [None] * n_ffn

    misc_parts = []
    big_rep_names = ["w_uk", "w_uv"]

    def packed_misc_grads():
        return _pack_misc(_split_dev(jnp.stack(d_wo), 1), _split_dev(jnp.stack(d_wdq), 1),
                          _split_dev(jnp.stack(d_wuq).reshape(n_b, rq, -1), 1), _split_dev(grads["w_dkv"], 0),
                          _split_dev(jnp.stack(d_pool_w), 2),
                          _split_dev(jnp.concatenate(d_pool_scale, axis=0), 1))

    def ffn_stage_bwd(e, dh_out, h_in, norm, gate, up):
        carry = [(slabs[e + 1].reshape(3, N_DEV, fs, D), "scatter_mid")] if e + 1 < n_ffn else []
        if e == 0:
            carry.append((packed_misc_grads(), "scatter_lead"))
            carry.append((jnp.concatenate([grads[n].reshape(-1, REP_COLS) for n in big_rep_names]), "to_all"))
        outs = _ffn_bwd_dx(dh_out, h_in, norm, gate, up, walls[e], carry)
        if carry:
            ffn_parts[e + 1] = outs[6]
        if e == 0:
            misc_parts.extend(outs[7:9])
        dh_in, dgt, dup, u_b, dy_b, dnorm = outs[:6]
        slabs[e] = _ffn_bwd_dw(dgt, dup, gate, up, u_b, dy_b)
        return dh_in, dnorm

    grads = {}
    d_pre, d_post, d_mix = [None] * depth, [None] * depth, [None] * depth
    d_pool_w, d_pool_scale = [None] * n_a, [None] * n_a
    d_qln, d_wdq, d_wuq, d_wo = [None] * n_b, [None] * n_b, [None] * n_b, [None] * n_b
    dks, dvs = [], []
    for l in reversed(range(depth)):
        s = saved[l]
        if l == n_a - 1:
            (dh, grads["kv_in_norm"], dwka, dwkb, grads["ckv_norm"], dwuk, dwuv) = _kv_proj_bwd(
                dks, dvs, dh, h_kv, vec(kv_in_norm), wka, wkb, craw, vec(ckv_norm), wuk, wuv, ca_k, sb)
            grads["w_dkv"] = _rope_weight_pair_grad(dwka, dwkb)
            grads["w_uk"] = dwuk.reshape(D_NOPE, N_HEADS, D_HEAD_PAD)[..., :D_NOPE]
            grads["w_uv"] = dwuv.reshape(D_NOPE, N_HEADS, D_V)
        dh, d_post[l] = ffn_stage_bwd(2 * l + 1, dh, s["h2"], vec(ffn_post_norm[l]), s["g2"], s["u2"])
        if l < n_a:
            dh, d_mix[l], d_pool_w[l], d_pool_scale[l] = _pool_bwd(
                dh, s["h1"], vec(mix_norm[l]), s["y"], pool_w_f[l], vec(pool_scale_f[l]))
        else:
            j = l - n_a
            d_wo[j] = _mm_tn(s["o"], dh, name="attn_out_dw")
            do = _mm_rows(dh, w_o_f[j], nt=True, out_dtype=BF16, name="attn_out_dx")
            delta_row = _attn_delta(s["o"], do)
            dq_t, dk_l, dv_l = _flash_bwd(s["q"], k_all, v_all, do, s["lse"], delta_row)
            dks.append(dk_l)
            dvs.append(dv_l)
            da, db, cqn, dcq, d_qln[j] = _q_proj_bwd(dq_t, s["cq"], vec(q_lora_norm[j]), wqa[j], wqb[j], ca_q, sb)
            dwa = _mm_tn(cqn, da, name="q_up_dw")
            dwb = _mm_tn(cqn, db, name="q_up_dw")
            d_wuq[j] = _rope_weight_pair_grad(dwa.reshape(rq, N_HEADS, D_HEAD_PAD),
                                              dwb.reshape(rq, N_HEADS, D_HEAD_PAD))
            d_wdq[j] = _mm_tn(s["h1"], dcq, norm_g=vec(mix_norm[l]), name="q_down_dw")
            dh, d_mix[l] = _proj_bwd(dcq, w_dq_f[j], s["h1"], vec(mix_norm[l]), dh, "q_down_dx")
        dh, d_pre[l] = ffn_stage_bwd(2 * l, dh, s["h0"], vec(ffn_pre_norm[l]), s["g1"], s["u1"])
    grad_x = dh.reshape(x.shape)

    rep_names = ["ffn_pre_norm", "mix_norm", "ffn_post_norm", "kv_in_norm", "ckv_norm", "q_lora_norm", "final_norm"]
    grads["ffn_pre_norm"] = jnp.concatenate(d_pre, axis=0)
    grads["mix_norm"] = jnp.concatenate(d_mix, axis=0)
    grads["ffn_post_norm"] = jnp.concatenate(d_post, axis=0)
    grads["q_lora_norm"] = jnp.concatenate(d_qln, axis=0)
    grads["final_norm"] = d_final
    rep_flat = jnp.concatenate([grads[n].reshape(-1) for n in rep_names] + [loss_part[0, :1]])
    n_rep = rep_flat.shape[0]
    rep_rows = -(-n_rep // (8 * REP_COLS)) * 8
    rep_g = jnp.concatenate([rep_flat, jnp.zeros((rep_rows * REP_COLS - n_rep,), F32)]).reshape(rep_rows, REP_COLS)
    ffn_parts[0], rep_parts = _exchange([(slabs[0].reshape(3, N_DEV, fs, D), "scatter_mid"), (rep_g, "to_all")],
                                        "comm_grad_exchange")
    ffn_sum = jnp.stack([_sum_parts(p.reshape(N_DEV, 3 * fs, D), fs, "sum_ffn").reshape(3, fs, D)
                         for p in ffn_parts])
    misc_sum_parts = _unpack_misc(misc_parts[0], misc_shapes)
    rep_sum = _sum_parts(rep_parts, _row_tile(rep_rows, 128), "sum_rep").reshape(-1)

    def sum_small(p):
        shp = p.shape[1:]
        two_d = (int(np.prod(shp[:-1])), shp[-1])
        return _sum_parts(p.reshape((N_DEV,) + two_d), two_d[0], "sum_misc").reshape(shp)

    g_wo, g_wdq, g_wuq, g_wdkv, g_pw, g_ps = [sum_small(p) for p in misc_sum_parts]
    grads.update(w_o=g_wo, w_dq=g_wdq, w_uq=g_wuq.reshape(w_uq.shape), w_dkv=g_wdkv, pool_w=g_pw, pool_scale=g_ps)
    for kind, (npre, npost) in enumerate((("ffn_pre_wg", "ffn_post_wg"), ("ffn_pre_wu", "ffn_post_wu"),
                                          ("ffn_pre_wd", "ffn_post_wd"))):
        pre = ffn_sum[0::2, kind]
        post = ffn_sum[1::2, kind]
        if kind < 2:
            pre, post = jnp.swapaxes(pre, 1, 2), jnp.swapaxes(post, 1, 2)
        grads[npre], grads[npost] = pre, post
    off = 0
    for n in rep_names:
        size = int(np.prod(weights[n].shape))
        grads[n] = rep_sum[off:off + size].reshape(weights[n].shape)
        off += size
    loss = rep_sum[off]
    big_rep_sum = _sum_parts(misc_parts[1], _row_tile(misc_parts[1].shape[1], 64), "sum_rep_big")
    off = 0
    for n in big_rep_names:
        rows = int(np.prod(weights[n].shape)) // REP_COLS
        grads[n] = big_rep_sum[off:off + rows].reshape(weights[n].shape)
        off += rows

    deltas, new_m, new_v = {}, {}, {}
    for n in order:
        deltas[n], new_m[n], new_v[n] = _adamw_nd(weights[n], grads[n], moments_m[n], moments_v[n])
    return (loss, grad_x, *[grads[n] for n in order], *[deltas[n] for n in order],
            *[new_m[n] for n in order], *[new_v[n] for n in order])
```

```python
import functools

import numpy as np
import jax
import jax.numpy as jnp
from jax import lax
from jax.experimental import pallas as pl
from jax.experimental.pallas import tpu as pltpu

F32, BF16 = jnp.float32, jnp.bfloat16
N_DEV = 8
RMS_EPS = 1e-6
N_HEADS = 16
D_NOPE, D_ROPE, D_V = 128, 64, 128
D_QK = D_NOPE + D_ROPE
D_HEAD_PAD = 256
HEAD_GROUP = 4
ROPE_THETA = 10000.0
POOL_WINDOWS = (2, 4, 8, 16)
POOL_HALO = 16
ADAM_LR, ADAM_B1, ADAM_B2, ADAM_EPS, ADAM_WD, ADAM_STEP = 0.001, 0.9, 0.999, 1e-08, 0.01, 10
NEG_BIG = -1e30
V7X_VMEM_LIMIT = 56 * 1024 * 1024
TOKEN_TILE = 512
ATTN_TILE = 512
FFN_TILE = 256
MISC_ROWS = 864
REP_COLS = 1024


def _cparams(*sem):
    return pltpu.CompilerParams(dimension_semantics=sem, vmem_limit_bytes=V7X_VMEM_LIMIT)


def _dot(a, b):
    return lax.dot_general(a, b, (((1,), (0,)), ((), ())), preferred_element_type=F32)


def _dot_nt(a, b):
    return lax.dot_general(a, b, (((1,), (1,)), ((), ())), preferred_element_type=F32)


def _dot_tn(a, b):
    return lax.dot_general(a, b, (((0,), (0,)), ((), ())), preferred_element_type=F32)


def _rms_fwd(x, g):
    r = lax.rsqrt(jnp.mean(x * x, axis=-1, keepdims=True) + RMS_EPS)
    return (x * r) * g


def _rms_bwd(du, x, g):
    r = lax.rsqrt(jnp.mean(x * x, axis=-1, keepdims=True) + RMS_EPS)
    xh = x * r
    dg = jnp.sum(du * xh, axis=0, keepdims=True)
    dxh = du * g
    dx = r * (dxh - xh * jnp.mean(dxh * xh, axis=-1, keepdims=True))
    return dx, dg


def _sigmoid(x):
    return 1.0 / (1.0 + jnp.exp(-x))


def _split_bf16(x):
    hi = x.astype(BF16)
    lo = (x - hi.astype(F32)).astype(BF16)
    return hi, lo


def _full(shape):
    return pl.BlockSpec(shape, lambda *_: (0,) * len(shape))


def _resident(shape):
    return pl.BlockSpec(shape, lambda *_: (0,) * len(shape), pipeline_mode=pl.Buffered(1))


def _row_tile(rows, cap):
    for t in range(min(cap, rows) // 8 * 8, 0, -8):
        if rows % t == 0:
            return t
    return rows


def _peers():
    x, y, c = lax.axis_index("x"), lax.axis_index("y"), lax.axis_index("c")
    out = []
    for k in range(1, N_DEV):
        px = 1 - x if (k >> 2) & 1 else x
        py = 1 - y if (k >> 1) & 1 else y
        pc = 1 - c if k & 1 else c
        out.append(((px, py, pc), 4 * px + 2 * py + pc))
    return 4 * x + 2 * y + c, out


_ROUTES = {
    "gather_mid": (lambda ref, idx: ref, lambda ref, idx: ref.at[:, idx], lambda s: s[:1] + (N_DEV,) + s[1:]),
    "to_all": (lambda ref, idx: ref, lambda ref, idx: ref.at[idx], lambda s: (N_DEV,) + s),
    "scatter_mid": (lambda ref, idx: ref.at[:, idx], lambda ref, idx: ref.at[idx],
                    lambda s: (N_DEV, s[0]) + s[2:]),
    "scatter_lead": (lambda ref, idx: ref.at[idx], lambda ref, idx: ref.at[idx], lambda s: s),
}


def _route_fns(items):
    kinds = [kind for _, kind in items]
    return (lambda j, ref, idx: _ROUTES[kinds[j]][0](ref, idx)), (lambda j, ref, idx: _ROUTES[kinds[j]][1](ref, idx))


def _route_out_shapes(items):
    return [jax.ShapeDtypeStruct(_ROUTES[kind][2](arr.shape), arr.dtype) for arr, kind in items]


def _exchange(items, name):
    n = len(items)
    fns = _route_fns(items)

    def body(*refs):
        ins, outs, sems = refs[:n], refs[n:2 * n], refs[2 * n:]
        _exchange_start(ins, outs, sems, *fns)
        _exchange_wait(ins, outs, sems, *fns)

    any_spec = pl.BlockSpec(memory_space=pl.ANY)
    return pl.pallas_call(
        body, name=name, out_shape=_route_out_shapes(items),
        in_specs=[any_spec] * n, out_specs=[any_spec] * n,
        scratch_shapes=_exchange_sems(n),
    )(*[arr for arr, _ in items])


def _exchange_sems(n):
    return [pltpu.SemaphoreType.DMA((n, N_DEV - 1)), pltpu.SemaphoreType.DMA((n, N_DEV - 1)),
            pltpu.SemaphoreType.DMA((n,))]


def _own_copies(ins, outs, sems, src_of, dst_of):
    me, _ = _peers()
    return [pltpu.make_async_copy(src_of(j, ins[j], me), dst_of(j, outs[j], me), sems[2].at[j])
            for j in range(len(ins))]


def _remote_copies(ins, outs, sems, src_of, dst_of, receiving):
    me, peers = _peers()
    return [pltpu.make_async_remote_copy(
        src_ref=src_of(j, ins[j], pidx), dst_ref=dst_of(j, outs[j], pidx if receiving else me),
        send_sem=sems[0].at[j, k], recv_sem=sems[1].at[j, k],
        device_id=peer, device_id_type=pl.DeviceIdType.MESH)
        for k, (peer, pidx) in enumerate(peers) for j in range(len(ins))]


def _exchange_start(ins, outs, sems, src_of, dst_of):
    for cp in _own_copies(ins, outs, sems, src_of, dst_of):
        cp.start()
    for cp in _remote_copies(ins, outs, sems, src_of, dst_of, receiving=False):
        cp.start()


def _exchange_wait(ins, outs, sems, src_of, dst_of):
    for cp in _remote_copies(ins, outs, sems, src_of, dst_of, receiving=True):
        cp.wait_recv()
    for cp in _remote_copies(ins, outs, sems, src_of, dst_of, receiving=False):
        cp.wait_send()
    for cp in _own_copies(ins, outs, sems, src_of, dst_of):
        cp.wait()


def _carried(items):
    if not items:
        return [], [], [], []
    return ([arr for arr, _ in items], [pl.BlockSpec(memory_space=pl.ANY)] * len(items), _route_out_shapes(items),
            _exchange_sems(len(items)))


def _ffn_fwd(h, g, w3, carry=()):
    T, D = h.shape
    F = w3.shape[1]
    tm, tf = min(TOKEN_TILE, T), F // 2
    nf = F // tf
    nt = T // tm
    nc = len(carry)
    has_c = nc > 0
    c_in, c_specs, c_out, c_sems = _carried(carry)

    def body(*refs):
        h_ref, g_ref, w_ref = refs[:3]
        ho_ref, gate_ref, up_ref = refs[3 + nc:6 + nc]
        i = pl.program_id(0)
        if has_c:
            comm = (refs[3:3 + nc], refs[6 + nc:6 + 2 * nc], refs[6 + 2 * nc:], *_route_fns(carry))

            @pl.when(i == 0)
            def _():
                _exchange_start(*comm)

        x = h_ref[...]
        u = _rms_fwd(x, g_ref[...]).astype(BF16)
        acc = None
        for f in range(nf):
            cols = slice(f * tf, (f + 1) * tf)
            gate = _dot_nt(u, w_ref[0, cols, :])
            up = _dot_nt(u, w_ref[1, cols, :])
            gate_ref[:, cols] = gate.astype(BF16)
            up_ref[:, cols] = up.astype(BF16)
            part = _dot((gate * _sigmoid(gate) * up).astype(BF16), w_ref[2, cols, :])
            acc = part if acc is None else acc + part
        ho_ref[...] = x + 0.5 * acc

        if has_c:
            @pl.when(i == nt - 1)
            def _():
                _exchange_wait(*comm)

    row = pl.BlockSpec((tm, D), lambda i: (i, 0))
    wide = pl.BlockSpec((tm, F), lambda i: (i, 0))
    return pl.pallas_call(
        body, name="ffn_fwd_gather" if has_c else "ffn_fwd", grid=(nt,),
        in_specs=[row, _full((1, D)), _resident(w3.shape)] + c_specs,
        out_specs=[row, wide, wide] + c_specs,
        out_shape=[jax.ShapeDtypeStruct((T, D), F32), jax.ShapeDtypeStruct((T, F), BF16),
                   jax.ShapeDtypeStruct((T, F), BF16)] + c_out,
        scratch_shapes=c_sems,
        compiler_params=_cparams("arbitrary"),
    )(h, g, w3, *c_in)


def _ffn_bwd_dx(dho, h, g, gate, up, w3, carry=()):
    T, D = h.shape
    F = w3.shape[1]
    tm, tf = min(TOKEN_TILE // 2, T), F // 2
    nf = F // tf
    nt = T // tm
    nc = len(carry)
    has_c = nc > 0
    c_in, c_specs, c_out, c_sems = _carried(carry)

    def body(*refs):
        dho_ref, h_ref, g_ref, gate_ref, up_ref, w_ref = refs[:6]
        dhi_ref, dgate_ref, dup_ref, u_ref, dy_ref, dg_ref = refs[6 + nc:12 + nc]
        i = pl.program_id(0)
        if has_c:
            comm = (refs[6:6 + nc], refs[12 + nc:12 + 2 * nc], refs[12 + 2 * nc:], *_route_fns(carry))

            @pl.when(i == 0)
            def _():
                _exchange_start(*comm)

        dho = dho_ref[...]
        x = h_ref[...]
        dy_f = 0.5 * dho
        dy = dy_f.astype(BF16)
        dy_ref[...] = dy_f.T.astype(BF16)
        u_ref[...] = _rms_fwd(x, g_ref[...]).T.astype(BF16)
        acc = None
        for f in range(nf):
            cols = slice(f * tf, (f + 1) * tf)
            dact = _dot_nt(dy, w_ref[2, cols, :])
            gt = gate_ref[:, cols].astype(F32)
            sig = _sigmoid(gt)
            dup = (dact * (gt * sig)).astype(BF16)
            dgate = (dact * up_ref[:, cols].astype(F32) * (sig * (1.0 + gt * (1.0 - sig)))).astype(BF16)
            dup_ref[:, cols] = dup
            dgate_ref[:, cols] = dgate
            part = _dot(dgate, w_ref[0, cols, :]) + _dot(dup, w_ref[1, cols, :])
            acc = part if acc is None else acc + part
        dx, dg = _rms_bwd(acc, x, g_ref[...])
        dhi_ref[...] = dho + dx

        @pl.when(i == 0)
        def _():
            dg_ref[...] = jnp.zeros_like(dg_ref)

        dg_ref[...] += dg

        if has_c:
            @pl.when(i == nt - 1)
            def _():
                _exchange_wait(*comm)

    row = pl.BlockSpec((tm, D), lambda i: (i, 0))
    col = pl.BlockSpec((D, tm), lambda i: (0, i))
    wide = pl.BlockSpec((tm, F), lambda i: (i, 0))
    return pl.pallas_call(
        body, name="ffn_bwd_dx_scatter" if has_c else "ffn_bwd_dx", grid=(nt,),
        in_specs=[row, row, _full((1, D)), wide, wide, _resident(w3.shape)] + c_specs,
        out_specs=[row, wide, wide, col, col, _full((1, D))] + c_specs,
        out_shape=[jax.ShapeDtypeStruct((T, D), F32), jax.ShapeDtypeStruct((T, F), BF16),
                   jax.ShapeDtypeStruct((T, F), BF16), jax.ShapeDtypeStruct((D, T), BF16),
                   jax.ShapeDtypeStruct((D, T), BF16), jax.ShapeDtypeStruct((1, D), F32)] + c_out,
        scratch_shapes=c_sems,
        compiler_params=_cparams("arbitrary"),
    )(dho, h, g, gate, up, w3, *c_in)


def _ffn_bwd_dw(dgate, dup, gate, up, u_t, dy_t):
    T, F = gate.shape
    D = u_t.shape[0]
    tfw = F // 2
    tk = min(TOKEN_TILE, T)
    nk = T // tk

    def body(dgate_ref, dup_ref, gate_ref, up_ref, ut_ref, dyt_ref, out_ref, acc_sc):
        k = pl.program_id(1)

        @pl.when(k == 0)
        def _():
            acc_sc[...] = jnp.zeros_like(acc_sc)

        uu = ut_ref[...]
        acc_sc[0] += _dot(uu, dgate_ref[...])
        acc_sc[1] += _dot(uu, dup_ref[...])
        gt = gate_ref[...].astype(F32)
        act = (gt * _sigmoid(gt) * up_ref[...].astype(F32)).astype(BF16)
        acc_sc[2] += _dot(dyt_ref[...], act)

        @pl.when(k == nk - 1)
        def _():
            for kind in range(3):
                out_ref[kind] = acc_sc[kind].T.astype(BF16)

    blk = pl.BlockSpec((tk, tfw), lambda j, k: (k, j))
    col = pl.BlockSpec((D, tk), lambda j, k: (0, k))
    return pl.pallas_call(
        body, name="ffn_bwd_dw", grid=(F // tfw, nk),
        in_specs=[blk, blk, blk, blk, col, col],
        out_specs=pl.BlockSpec((3, tfw, D), lambda j, k: (0, j, 0)),
        out_shape=jax.ShapeDtypeStruct((3, F, D), BF16),
        scratch_shapes=[pltpu.VMEM((3, D, tfw), F32)],
        compiler_params=_cparams("parallel", "arbitrary"),
    )(dgate, dup, gate, up, u_t, dy_t)


def _mm_rows(a, b, *, nt, out_dtype, norm_g=None, res=None, name):
    T, K = a.shape
    N = b.shape[0] if nt else b.shape[1]
    tm = min(TOKEN_TILE, T)
    has_g, has_r = norm_g is not None, res is not None

    def body(*refs):
        a_ref, b_ref = refs[0], refs[1]
        o_ref = refs[-1]
        x = a_ref[...]
        if has_g:
            x = _rms_fwd(x, refs[2][...])
        x = x.astype(BF16)
        acc = _dot_nt(x, b_ref[...]) if nt else _dot(x, b_ref[...])
        if has_r:
            acc = refs[2 + has_g][...] + acc
        o_ref[...] = acc.astype(out_dtype)

    ins, specs = [a, b], [pl.BlockSpec((tm, K), lambda i: (i, 0)), _full(b.shape)]
    if has_g:
        ins.append(norm_g)
        specs.append(_full((1, K)))
    if has_r:
        ins.append(res)
        specs.append(pl.BlockSpec((tm, N), lambda i: (i, 0)))
    return pl.pallas_call(
        body, name=name, grid=(T // tm,), in_specs=specs,
        out_specs=pl.BlockSpec((tm, N), lambda i: (i, 0)),
        out_shape=jax.ShapeDtypeStruct((T, N), out_dtype),
        compiler_params=_cparams("parallel"),
    )(*ins)


def _mm_tn(a, b, *, norm_g=None, name):
    T, M = a.shape
    N = b.shape[1]
    tk = min(TOKEN_TILE, T)
    has_g = norm_g is not None

    def body(*refs):
        a_ref, b_ref, o_ref = refs[0], refs[1], refs[-1]

        @pl.when(pl.program_id(0) == 0)
        def _():
            o_ref[...] = jnp.zeros_like(o_ref)

        x = a_ref[...]
        if has_g:
            x = _rms_fwd(x, refs[2][...])
        o_ref[...] += _dot_tn(x.astype(BF16), b_ref[...].astype(BF16))

    ins = [a, b]
    specs = [pl.BlockSpec((tk, M), lambda k: (k, 0)), pl.BlockSpec((tk, N), lambda k: (k, 0))]
    if has_g:
        ins.append(norm_g)
        specs.append(_full((1, M)))
    return pl.pallas_call(
        body, name=name, grid=(T // tk,), in_specs=specs, out_specs=_full((M, N)),
        out_shape=jax.ShapeDtypeStruct((M, N), F32),
        compiler_params=_cparams("arbitrary"),
    )(*ins)


def _proj_bwd(dz, w, h, g, dh, name):
    T, D = h.shape
    N = w.shape[1]
    tm = min(TOKEN_TILE, T)

    def body(dz_ref, w_ref, h_ref, g_ref, dh_ref, o_ref, dg_ref):
        du = _dot_nt(dz_ref[...].astype(BF16), w_ref[...])
        dx, dg = _rms_bwd(du, h_ref[...], g_ref[...])
        o_ref[...] = dh_ref[...] + dx

        @pl.when(pl.program_id(0) == 0)
        def _():
            dg_ref[...] = jnp.zeros_like(dg_ref)

        dg_ref[...] += dg

    row = pl.BlockSpec((tm, D), lambda i: (i, 0))
    return pl.pallas_call(
        body, name=name, grid=(T // tm,),
        in_specs=[pl.BlockSpec((tm, N), lambda i: (i, 0)), _full((D, N)), row, _full((1, D)), row],
        out_specs=[row, _full((1, D))],
        out_shape=[jax.ShapeDtypeStruct((T, D), F32), jax.ShapeDtypeStruct((1, D), F32)],
        compiler_params=_cparams("arbitrary"),
    )(dz, w, h, g, dh)


def _pool_bands(tm):
    r = np.arange(tm)[:, None]
    c = np.arange(tm)[None, :]
    j = np.arange(POOL_HALO)[None, :]
    main, halo, main_t, halo_t = [], [], [], []
    for w in POOL_WINDOWS:
        main.append(((r - c >= 0) & (r - c < w)) / w)
        halo.append((r + POOL_HALO - j < w) / w)
        main_t.append(((c - r >= 0) & (c - r < w)) / w)
        halo_t.append((tm + j - r < w) / w)
    return tuple(jnp.asarray(np.stack(m), BF16) for m in (main, halo, main_t, halo_t))


def _pool_count_scale(i, tm, w):
    t = i * tm + lax.broadcasted_iota(jnp.int32, (tm, 1), 0)
    return w / jnp.minimum(t + 1, w).astype(F32)


def _pool_fwd(h, g, wp, scale):
    T, D = h.shape
    G, dg = len(POOL_WINDOWS), D // len(POOL_WINDOWS)
    tm = min(TOKEN_TILE, T)
    hb = tm // POOL_HALO
    bm, bh, _, _ = _pool_bands(tm)

    def body(h_ref, hh_ref, g_ref, wp_ref, sc_ref, bm_ref, bh_ref, ho_ref, y_ref):
        i = pl.program_id(0)
        x = h_ref[...]
        u = _rms_fwd(x, g_ref[...])
        uh = _rms_fwd(hh_ref[...], g_ref[...]) * (i > 0).astype(F32)
        for gi, w in enumerate(POOL_WINDOWS):
            cols = slice(gi * dg, (gi + 1) * dg)
            ug = u[:, cols]
            hi, lo = _split_bf16(ug)
            hhi, hlo = _split_bf16(uh[:, cols])
            s = (_dot(bm_ref[gi], hi) + _dot(bm_ref[gi], lo)
                 + _dot(bh_ref[gi], hhi) + _dot(bh_ref[gi], hlo))
            y = (s * _pool_count_scale(i, tm, w) - ug).astype(BF16)
            y_ref[:, cols] = y
            ho_ref[:, cols] = x[:, cols] + _dot(y, wp_ref[gi]) * sc_ref[:, cols]

    row = pl.BlockSpec((tm, D), lambda i: (i, 0))
    return pl.pallas_call(
        body, name="pool_fwd", grid=(T // tm,),
        in_specs=[row, pl.BlockSpec((POOL_HALO, D), lambda i: (jnp.maximum(i * hb - 1, 0), 0)),
                  _full((1, D)), _full((G, dg, dg)), _full((1, D)),
                  _full((G, tm, tm)), _full((G, tm, POOL_HALO))],
        out_specs=[row, row],
        out_shape=[jax.ShapeDtypeStruct((T, D), F32), jax.ShapeDtypeStruct((T, D), BF16)],
        compiler_params=_cparams("parallel"),
    )(h, h, g, wp, scale, bm, bh)


def _pool_bwd(dh, h, g, y, wp, scale):
    T, D = h.shape
    G, dg = len(POOL_WINDOWS), D // len(POOL_WINDOWS)
    tm = min(TOKEN_TILE, T)
    hb = tm // POOL_HALO
    nt = T // tm
    _, _, bmt, bht = _pool_bands(tm)

    def body(dh_ref, dhn_ref, h_ref, g_ref, y_ref, wp_ref, sc_ref, bmt_ref, bht_ref,
             o_ref, dg_ref, dwp_ref, dsc_ref, du_sc):
        i = pl.program_id(0)

        @pl.when(i == 0)
        def _():
            dg_ref[...] = jnp.zeros_like(dg_ref)
            dwp_ref[...] = jnp.zeros_like(dwp_ref)
            dsc_ref[...] = jnp.zeros_like(dsc_ref)

        dho = dh_ref[...]
        dz = dho * sc_ref[...]
        dzn = dhn_ref[...] * sc_ref[...] * (i < nt - 1).astype(F32)
        for gi, w in enumerate(POOL_WINDOWS):
            cols = slice(gi * dg, (gi + 1) * dg)
            yg = y_ref[:, cols]
            dzg = dz[:, cols].astype(BF16)
            dsc_ref[:, cols] += jnp.sum(dho[:, cols] * _dot(yg, wp_ref[gi]), axis=0, keepdims=True)
            dwp_ref[gi] += _dot_tn(yg, dzg)
            dy = _dot_nt(dzg, wp_ref[gi])
            dyn = _dot_nt(dzn[:, cols].astype(BF16), wp_ref[gi])
            hi, lo = _split_bf16(dy * _pool_count_scale(i, tm, w))
            nhi, nlo = _split_bf16(dyn)
            du_sc[:, cols] = (_dot(bmt_ref[gi], hi) + _dot(bmt_ref[gi], lo)
                              + _dot(bht_ref[gi], nhi) + _dot(bht_ref[gi], nlo) - dy)
        dx, dgp = _rms_bwd(du_sc[...], h_ref[...], g_ref[...])
        o_ref[...] = dho + dx
        dg_ref[...] += dgp

    row = pl.BlockSpec((tm, D), lambda i: (i, 0))
    return pl.pallas_call(
        body, name="pool_bwd", grid=(nt,),
        in_specs=[row, pl.BlockSpec((POOL_HALO, D), lambda i: (jnp.minimum((i + 1) * hb, T // POOL_HALO - 1), 0)),
                  row, _full((1, D)), row, _full((G, dg, dg)), _full((1, D)),
                  _full((G, tm, tm)), _full((G, tm, POOL_HALO))],
        out_specs=[row, _full((1, D)), _full((G, dg, dg)), _full((1, D))],
        out_shape=[jax.ShapeDtypeStruct((T, D), F32), jax.ShapeDtypeStruct((1, D), F32),
                   jax.ShapeDtypeStruct((G, dg, dg), F32), jax.ShapeDtypeStruct((1, D), F32)],
        scratch_shapes=[pltpu.VMEM((tm, D), F32)],
        compiler_params=_cparams("arbitrary"),
    )(dh, dh, h, g, y, wp, scale, bmt, bht)


def _rope_tables(T):
    pos = jnp.arange(T, dtype=F32)
    inv_freq = ROPE_THETA ** (-jnp.arange(0, D_ROPE, 2, dtype=F32) / D_ROPE)
    ang = pos[:, None] * inv_freq[None, :]
    cos2 = jnp.tile(jnp.cos(ang), (1, 2))
    sin2 = jnp.tile(jnp.sin(ang), (1, 2))
    pad = jnp.zeros((T, D_HEAD_PAD - D_QK), F32)
    ca_q = jnp.concatenate([jnp.ones((T, D_NOPE), F32), cos2, pad], axis=1)
    ca_k = jnp.concatenate([jnp.zeros((T, D_NOPE), F32), cos2, pad], axis=1)
    sb = jnp.concatenate([jnp.zeros((T, D_NOPE), F32), sin2, pad], axis=1)
    return ca_q, ca_k, sb


def _rope_weight_pair(w):
    half = D_ROPE // 2
    z_pad = jnp.zeros(w.shape[:-1] + (D_HEAD_PAD - D_QK,), w.dtype)
    z_nope = jnp.zeros(w.shape[:-1] + (D_NOPE,), w.dtype)
    wa = jnp.concatenate([w, z_pad], axis=-1)
    wb = jnp.concatenate([z_nope, -w[..., D_NOPE + half:], w[..., D_NOPE:D_NOPE + half], z_pad], axis=-1)
    return wa, wb


def _rope_weight_pair_grad(dwa, dwb):
    half = D_ROPE // 2
    d1 = dwa[..., D_NOPE:D_NOPE + half] + dwb[..., D_NOPE + half:D_QK]
    d2 = dwa[..., D_NOPE + half:D_QK] - dwb[..., D_NOPE:D_NOPE + half]
    return jnp.concatenate([dwa[..., :D_NOPE], d1, d2], axis=-1)


def _q_proj(cq, qg, wa, wb, ca, sb):
    T, R = cq.shape
    tm = min(TOKEN_TILE, T)
    P = D_HEAD_PAD
    GP = HEAD_GROUP * P

    def body(cq_ref, qg_ref, wa_ref, wb_ref, ca_ref, sb_ref, q_ref):
        c = _rms_fwd(cq_ref[...], qg_ref[...]).astype(BF16)
        ca = jnp.tile(ca_ref[...], (1, HEAD_GROUP))
        sb = jnp.tile(sb_ref[...], (1, HEAD_GROUP))
        q_ref[...] = (_dot(c, wa_ref[...]) * ca + _dot(c, wb_ref[...]) * sb).astype(BF16)

    tok = pl.BlockSpec((tm, P), lambda i, hh: (i, 0))
    wsp = pl.BlockSpec((R, GP), lambda i, hh: (0, hh))
    return pl.pallas_call(
        body, name="q_proj", grid=(T // tm, N_HEADS // HEAD_GROUP),
        in_specs=[pl.BlockSpec((tm, R), lambda i, hh: (i, 0)), _full((1, R)), wsp, wsp, tok, tok],
        out_specs=pl.BlockSpec((tm, GP), lambda i, hh: (i, hh)),
        out_shape=jax.ShapeDtypeStruct((T, N_HEADS * P), BF16),
        compiler_params=_cparams("parallel", "arbitrary"),
    )(cq, qg, wa, wb, ca, sb)


def _q_proj_bwd(dq, cq, qg, wa, wb, ca, sb):
    T, R = cq.shape
    tm = min(TOKEN_TILE, T)
    P = D_HEAD_PAD

    def body(dq_ref, cq_ref, qg_ref, wa_ref, wb_ref, ca_ref, sb_ref,
             da_ref, db_ref, cqn_ref, dcq_ref, dqg_ref, acc_sc):
        i, hh = pl.program_id(0), pl.program_id(1)

        @pl.when(hh == 0)
        def _():
            acc_sc[...] = jnp.zeros_like(acc_sc)
            cqn_ref[...] = _rms_fwd(cq_ref[...], qg_ref[...]).astype(BF16)

        d = dq_ref[...].T
        da = (d * jnp.tile(ca_ref[...], (1, HEAD_GROUP))).astype(BF16)
        db = (d * jnp.tile(sb_ref[...], (1, HEAD_GROUP))).astype(BF16)
        da_ref[...] = da
        db_ref[...] = db
        acc_sc[...] += _dot_nt(da, wa_ref[...]) + _dot_nt(db, wb_ref[...])

        @pl.when(hh == N_HEADS // HEAD_GROUP - 1)
        def _():
            dx, dg = _rms_bwd(acc_sc[...], cq_ref[...], qg_ref[...])
            dcq_ref[...] = dx

            @pl.when(i == 0)
            def _():
                dqg_ref[...] = jnp.zeros_like(dqg_ref)

            dqg_ref[...] += dg

    GP = HEAD_GROUP * P
    tok = pl.BlockSpec((tm, P), lambda i, hh: (i, 0))
    hd = pl.BlockSpec((tm, GP), lambda i, hh: (i, hh))
    wsp = pl.BlockSpec((R, GP), lambda i, hh: (0, hh))
    rr = pl.BlockSpec((tm, R), lambda i, hh: (i, 0))
    return pl.pallas_call(
        body, name="q_proj_bwd", grid=(T // tm, N_HEADS // HEAD_GROUP),
        in_specs=[pl.BlockSpec((GP, tm), lambda i, hh: (hh, i)), rr, _full((1, R)), wsp, wsp, tok, tok],
        out_specs=[hd, hd, rr, rr, _full((1, R))],
        out_shape=[jax.ShapeDtypeStruct((T, N_HEADS * P), BF16), jax.ShapeDtypeStruct((T, N_HEADS * P), BF16),
                   jax.ShapeDtypeStruct((T, R), BF16), jax.ShapeDtypeStruct((T, R), F32),
                   jax.ShapeDtypeStruct((1, R), F32)],
        scratch_shapes=[pltpu.VMEM((tm, R), F32)],
        compiler_params=_cparams("arbitrary", "arbitrary"),
    )(dq, cq, qg, wa, wb, ca, sb)


def _kv_proj(h, g_in, wka, wkb, g_c, wuk, wuv, ca, sb):
    T, D = h.shape
    tm = min(TOKEN_TILE, T)
    P, C = D_HEAD_PAD, D_NOPE

    def body(h_ref, gi_ref, wka_ref, wkb_ref, gc_ref, wuk_ref, wuv_ref, ca_ref, sb_ref, k_ref, v_ref, craw_ref):
        u = _rms_fwd(h_ref[...], gi_ref[...]).astype(BF16)
        kva = _dot(u, wka_ref[...])
        kvb = _dot(u, wkb_ref[...])
        craw = kva[:, :C]
        craw_ref[...] = craw
        c = _rms_fwd(craw, gc_ref[...]).astype(BF16)
        kr = kva * ca_ref[...] + kvb * sb_ref[...]
        kn = _dot(c, wuk_ref[...])
        for hh in range(N_HEADS):
            k_ref[:, hh * P:(hh + 1) * P] = (kn[:, hh * P:(hh + 1) * P] + kr).astype(BF16)
        v_ref[...] = _dot(c, wuv_ref[...]).astype(BF16)

    tok = pl.BlockSpec((tm, P), lambda i: (i, 0))
    return pl.pallas_call(
        body, name="kv_proj", grid=(T // tm,),
        in_specs=[pl.BlockSpec((tm, D), lambda i: (i, 0)), _full((1, D)), _full((D, P)), _full((D, P)),
                  _full((1, C)), _full(wuk.shape), _full(wuv.shape), tok, tok],
        out_specs=[pl.BlockSpec((tm, N_HEADS * P), lambda i: (i, 0)),
                   pl.BlockSpec((tm, N_HEADS * D_V), lambda i: (i, 0)), pl.BlockSpec((tm, C), lambda i: (i, 0))],
        out_shape=[jax.ShapeDtypeStruct((T, N_HEADS * P), BF16), jax.ShapeDtypeStruct((T, N_HEADS * D_V), BF16),
                   jax.ShapeDtypeStruct((T, C), F32)],
        compiler_params=_cparams("parallel"),
    )(h, g_in, wka, wkb, g_c, wuk, wuv, ca, sb)


def _kv_proj_bwd(dks, dvs, dh, h, g_in, wka, wkb, craw, g_c, wuk, wuv, ca, sb):
    T, D = h.shape
    tm = min(TOKEN_TILE // 2, T)
    P, C = D_HEAD_PAD, D_NOPE
    nl = len(dks)

    def body(*refs):
        dk_refs, dv_refs = refs[:nl], refs[nl:2 * nl]
        (dh_ref, h_ref, gi_ref, wka_ref, wkb_ref, craw_ref, gc_ref, wuk_ref, wuv_ref,
         ca_ref, sb_ref, o_ref, dgi_ref, dwka_ref, dwkb_ref, dgc_ref, dwuk_ref, dwuv_ref) = refs[2 * nl:]

        @pl.when(pl.program_id(0) == 0)
        def _():
            for r in (dgi_ref, dwka_ref, dwkb_ref, dgc_ref, dwuk_ref, dwuv_ref):
                r[...] = jnp.zeros_like(r)

        x = h_ref[...]
        u = _rms_fwd(x, gi_ref[...]).astype(BF16)
        craw = craw_ref[...]
        c = _rms_fwd(craw, gc_ref[...]).astype(BF16)
        dkf = sum(r[...] for r in dk_refs[1:]) + dk_refs[0][...]
        dkb = dkf.astype(BF16)
        dvb = (sum(r[...] for r in dv_refs[1:]) + dv_refs[0][...]).astype(BF16)
        dwuk_ref[...] += _dot_tn(c, dkb)
        dwuv_ref[...] += _dot_tn(c, dvb)
        dc = _dot_nt(dkb, wuk_ref[...]) + _dot_nt(dvb, wuv_ref[...])
        dkr = dkf[:, :P]
        for hh in range(1, N_HEADS):
            dkr = dkr + dkf[:, hh * P:(hh + 1) * P]
        dcraw, dgc = _rms_bwd(dc, craw, gc_ref[...])
        dgc_ref[...] += dgc
        dkva = jnp.concatenate([dcraw, (dkr * ca_ref[...])[:, C:]], axis=1).astype(BF16)
        dkvb = (dkr * sb_ref[...]).astype(BF16)
        dwka_ref[...] += _dot_tn(u, dkva)
        dwkb_ref[...] += _dot_tn(u, dkvb)
        du = _dot_nt(dkva, wka_ref[...]) + _dot_nt(dkvb, wkb_ref[...])
        dx, dgi = _rms_bwd(du, x, gi_ref[...])
        dgi_ref[...] += dgi
        o_ref[...] = dh_ref[...] + dx

    row = pl.BlockSpec((tm, D), lambda i: (i, 0))
    tok = pl.BlockSpec((tm, P), lambda i: (i, 0))
    return pl.pallas_call(
        body, name="kv_proj_bwd", grid=(T // tm,),
        in_specs=[pl.BlockSpec((tm, N_HEADS * P), lambda i: (i, 0))] * nl
        + [pl.BlockSpec((tm, N_HEADS * D_V), lambda i: (i, 0))] * nl
        + [row, row, _full((1, D)), _full((D, P)), _full((D, P)), pl.BlockSpec((tm, C), lambda i: (i, 0)),
           _full((1, C)), _full(wuk.shape), _full(wuv.shape), tok, tok],
        out_specs=[row, _full((1, D)), _full((D, P)), _full((D, P)), _full((1, C)),
                   _full(wuk.shape), _full(wuv.shape)],
        out_shape=[jax.ShapeDtypeStruct((T, D), F32), jax.ShapeDtypeStruct((1, D), F32),
                   jax.ShapeDtypeStruct((D, P), F32), jax.ShapeDtypeStruct((D, P), F32),
                   jax.ShapeDtypeStruct((1, C), F32), jax.ShapeDtypeStruct(wuk.shape, F32),
                   jax.ShapeDtypeStruct(wuv.shape, F32)],
        compiler_params=_cparams("arbitrary"),
    )(*dks, *dvs, dh, h, g_in, wka, wkb, craw, g_c, wuk, wuv, ca, sb)


_ATTN_SCALE = D_QK ** -0.5
_LOG2_E = 1.4426950408889634
_LN_2 = 0.6931471805599453
_ATTN_SCALE_LOG2 = _ATTN_SCALE * _LOG2_E


def _flash_fwd(q, k, v):
    T = q.shape[0]
    t = min(ATTN_TILE, T // 2)
    tq = 2 * t
    P = D_HEAD_PAD

    def body(q_ref, k_ref, v_ref, o_ref, lse_ref, m_sc, l_sc, acc_sc, s0_sc, s1_sc, p0_sc, p1_sc, a0_sc, a1_sc):
        qi = pl.program_id(1)
        n = 2 * (qi + 1)
        s_sc, p_sc, a_sc = (s0_sc, s1_sc), (p0_sc, p1_sc), (a0_sc, a1_sc)
        m_sc[...] = jnp.full_like(m_sc, NEG_BIG)
        l_sc[...] = jnp.zeros_like(l_sc)
        acc_sc[...] = jnp.zeros_like(acc_sc)

        def rows_of(c):
            return pl.ds(pl.multiple_of(c * t, t), t)

        def scores(c, slot):
            s_sc[slot][...] = _dot_nt(k_ref[rows_of(c), :], q_ref[...])

        def softmax(slot, key_offset):
            s_t = s_sc[slot][...]
            if key_offset is not None:
                rows = lax.broadcasted_iota(jnp.int32, (t, tq), 0) + key_offset
                s_t = jnp.where(rows <= lax.broadcasted_iota(jnp.int32, (t, tq), 1), s_t, NEG_BIG)
            m_prev = m_sc[...]
            m_new = jnp.maximum(m_prev, jnp.max(s_t, axis=0, keepdims=True))
            p_t = jnp.exp2(s_t - m_new)
            alpha = jnp.exp2(m_prev - m_new)
            l_sc[...] = alpha * l_sc[...] + jnp.sum(p_t, axis=0, keepdims=True)
            m_sc[...] = m_new
            p_sc[slot][...] = p_t.astype(BF16)
            a_sc[slot][...] = alpha

        def values(c, slot):
            acc_sc[...] = a_sc[slot][...] * acc_sc[...] + _dot_tn(v_ref[rows_of(c), :], p_sc[slot][...])

        def stage(c, slot, first=False, last=False, key_offset=None):
            if not first:
                values(c - 1, 1 - slot)
            if not last:
                scores(c + 1, 1 - slot)
            softmax(slot, key_offset)

        def drain():
            stage(n - 2, 0, key_offset=0)
            stage(n - 1, 1, last=True, key_offset=t)
            values(n - 1, 1)

        scores(0, 0)

        @pl.when(qi == 0)
        def _():
            stage(0, 0, first=True, key_offset=0)
            stage(1, 1, last=True, key_offset=t)
            values(1, 1)

        @pl.when(qi > 0)
        def _():
            stage(0, 0, first=True)

            def pair(j, carry):
                stage(1 + 2 * j, 1)
                stage(2 + 2 * j, 0)
                return carry

            lax.fori_loop(0, qi - 1, pair, 0)
            stage(n - 3, 1)
            drain()

        l = l_sc[...]
        o_ref[...] = (acc_sc[...] / l).T.astype(BF16)
        lse_ref[...] = m_sc[...] + jnp.log(l) * _LOG2_E

    return pl.pallas_call(
        body, name="flash_fwd", grid=(N_HEADS, T // tq),
        in_specs=[pl.BlockSpec((tq, P), lambda hh, i: (i, hh)), pl.BlockSpec((T, P), lambda hh, i: (0, hh)),
                  pl.BlockSpec((T, D_V), lambda hh, i: (0, hh))],
        out_specs=[pl.BlockSpec((tq, D_V), lambda hh, i: (i, hh)),
                   pl.BlockSpec((None, None, 1, tq), lambda hh, i: (hh, i, 0, 0))],
        out_shape=[jax.ShapeDtypeStruct((T, N_HEADS * D_V), BF16),
                   jax.ShapeDtypeStruct((N_HEADS, T // tq, 1, tq), F32)],
        scratch_shapes=[pltpu.VMEM((1, tq), F32), pltpu.VMEM((1, tq), F32), pltpu.VMEM((D_V, tq), F32),
                        pltpu.VMEM((t, tq), F32), pltpu.VMEM((t, tq), F32), pltpu.VMEM((t, tq), BF16),
                        pltpu.VMEM((t, tq), BF16), pltpu.VMEM((1, tq), F32), pltpu.VMEM((1, tq), F32)],
        compiler_params=_cparams("parallel", "arbitrary"),
    )(q, k, v)


def _attn_delta(o, do):
    T = o.shape[0]
    t = min(ATTN_TILE, T)

    def body(o_ref, do_ref, out_ref):
        ones = jnp.ones((8, D_V), BF16)
        for hh in range(N_HEADS):
            cols = slice(hh * D_V, (hh + 1) * D_V)
            hi, lo = _split_bf16(o_ref[:, cols].astype(F32) * do_ref[:, cols].astype(F32))
            out_ref[hh] = (_dot_nt(ones, hi) + _dot_nt(ones, lo))[0:1]

    tok = pl.BlockSpec((t, N_HEADS * D_V), lambda i: (i, 0))
    return pl.pallas_call(
        body, name="attn_delta", grid=(T // t,), in_specs=[tok, tok],
        out_specs=pl.BlockSpec((N_HEADS, None, 1, t), lambda i: (0, i, 0, 0)),
        out_shape=jax.ShapeDtypeStruct((N_HEADS, T // t, 1, t), F32),
        compiler_params=_cparams("parallel"),
    )(o, do)


def _flash_bwd(q, k, v, do, lse_row, delta_row):
    T = q.shape[0]
    t = min(ATTN_TILE, T // 2)
    tk = 2 * t
    nq = T // t
    P = D_HEAD_PAD

    def body(k_ref, v_ref, q_ref, do_ref, lse_ref, delta_ref, dqt_ref, dk_ref, dv_ref, dqt_sc, kt_sc,
             s0_sc, s1_sc, dp0_sc, dp1_sc, p0_sc, p1_sc, ds0_sc, ds1_sc):
        ki = pl.program_id(1)
        q0 = 2 * ki
        n = nq - q0
        s_sc, dp_sc, p_sc, ds_sc = (s0_sc, s1_sc), (dp0_sc, dp1_sc), (p0_sc, p1_sc), (ds0_sc, ds1_sc)
        kt_sc[...] = k_ref[...].astype(F32).T.astype(BF16)
        dk_ref[...] = jnp.zeros_like(dk_ref)
        dv_ref[...] = jnp.zeros_like(dv_ref)

        @pl.when(ki == 0)
        def _():
            dqt_sc[...] = jnp.zeros_like(dqt_sc)

        def rows_of(c):
            return pl.ds(pl.multiple_of((q0 + c) * t, t), t)

        def products(c, slot):
            s_sc[slot][...] = _dot_nt(k_ref[...], q_ref[rows_of(c), :])
            dp_sc[slot][...] = _dot_nt(v_ref[...], do_ref[rows_of(c), :])

        def elementwise(c, slot, query_offset):
            p_t = jnp.exp2(s_sc[slot][...] - lse_ref[q0 + c])
            if query_offset is not None:
                cols = lax.broadcasted_iota(jnp.int32, (tk, t), 1) + query_offset
                p_t = jnp.where(lax.broadcasted_iota(jnp.int32, (tk, t), 0) <= cols, p_t, 0.0)
            p_sc[slot][...] = p_t.astype(BF16)
            ds_sc[slot][...] = (p_t * (dp_sc[slot][...] - delta_ref[q0 + c])).astype(BF16)

        def gradients(c, slot):
            dv_ref[...] += _dot(p_sc[slot][...], do_ref[rows_of(c), :])
            ds_t = ds_sc[slot][...]
            dk_ref[...] += _dot(ds_t, q_ref[rows_of(c), :])
            dqt_sc[q0 + c] += _dot(kt_sc[...], ds_t)

        def stage(c, slot, first=False, last=False, query_offset=None):
            if not first:
                gradients(c - 1, 1 - slot)
            if not last:
                products(c + 1, 1 - slot)
            elementwise(c, slot, query_offset)

        products(0, 0)

        @pl.when(n == 2)
        def _():
            stage(0, 0, first=True, query_offset=0)
            stage(1, 1, last=True, query_offset=t)
            gradients(1, 1)

        @pl.when(n > 2)
        def _():
            stage(0, 0, first=True, query_offset=0)
            stage(1, 1, query_offset=t)

            def pair(j, carry):
                stage(2 + 2 * j, 0)
                stage(3 + 2 * j, 1)
                return carry

            lax.fori_loop(0, (n - 4) // 2, pair, 0)
            stage(n - 2, 0)
            stage(n - 1, 1, last=True)
            gradients(n - 1, 1)

        dqt_ref[:, :t] = dqt_sc[q0] * _ATTN_SCALE
        dqt_ref[:, t:] = dqt_sc[q0 + 1] * _ATTN_SCALE
        dk_ref[...] = dk_ref[...] * _LN_2

    kb = pl.BlockSpec((tk, P), lambda hh, i: (i, hh))
    vb = pl.BlockSpec((tk, D_V), lambda hh, i: (i, hh))
    stat = pl.BlockSpec((None, nq, 1, t), lambda hh, i: (hh, 0, 0, 0))
    return pl.pallas_call(
        body, name="flash_bwd", grid=(N_HEADS, T // tk),
        in_specs=[kb, vb,
                  pl.BlockSpec((T, P), lambda hh, i: (0, hh), pipeline_mode=pl.Buffered(1)),
                  pl.BlockSpec((T, D_V), lambda hh, i: (0, hh), pipeline_mode=pl.Buffered(1)),
                  stat, stat],
        out_specs=[pl.BlockSpec((P, tk), lambda hh, i: (hh, i)), kb, vb],
        out_shape=[jax.ShapeDtypeStruct((N_HEADS * P, T), F32), jax.ShapeDtypeStruct((T, N_HEADS * P), F32),
                   jax.ShapeDtypeStruct((T, N_HEADS * D_V), F32)],
        scratch_shapes=[pltpu.VMEM((nq, P, t), F32), pltpu.VMEM((P, tk), BF16)]
        + [pltpu.VMEM((tk, t), F32)] * 4 + [pltpu.VMEM((tk, t), BF16)] * 4,
        compiler_params=_cparams("arbitrary", "arbitrary"),
    )(k, v, q, do, lse_row, delta_row)


def _loss_head(h, g, target):
    T, D = h.shape
    tm = min(TOKEN_TILE, T)

    def body(h_ref, g_ref, t_ref, dh_ref, loss_ref, dg_ref):
        @pl.when(pl.program_id(0) == 0)
        def _():
            loss_ref[...] = jnp.zeros_like(loss_ref)
            dg_ref[...] = jnp.zeros_like(dg_ref)

        x = h_ref[...]
        err = _rms_fwd(x, g_ref[...]) - t_ref[...]
        per_tok = jnp.mean(err * err, axis=-1, keepdims=True)
        loss_ref[...] += 0.5 * jnp.sum(per_tok, axis=0, keepdims=True)
        dx, dg = _rms_bwd(err * (1.0 / D), x, g_ref[...])
        dh_ref[...] = dx
        dg_ref[...] += dg

    row = pl.BlockSpec((tm, D), lambda i: (i, 0))
    return pl.pallas_call(
        body, name="loss_head", grid=(T // tm,),
        in_specs=[row, _full((1, D)), row], out_specs=[row, _full((1, 128)), _full((1, D))],
        out_shape=[jax.ShapeDtypeStruct((T, D), F32), jax.ShapeDtypeStruct((1, 128), F32),
                   jax.ShapeDtypeStruct((1, D), F32)],
        compiler_params=_cparams("arbitrary"),
    )(h, g, target)


def _sum_parts(parts, tr, name):
    _, R, C = parts.shape

    def body(p_ref, o_ref):
        acc = p_ref[0].astype(F32)
        for j in range(1, N_DEV):
            acc = acc + p_ref[j].astype(F32)
        o_ref[...] = acc

    return pl.pallas_call(
        body, name=name, grid=(R // tr,),
        in_specs=[pl.BlockSpec((N_DEV, tr, C), lambda i: (0, i, 0))],
        out_specs=pl.BlockSpec((tr, C), lambda i: (i, 0)),
        out_shape=jax.ShapeDtypeStruct((R, C), F32),
        compiler_params=_cparams("parallel"),
    )(parts)


def _adamw(w, g, m, v):
    R, C = w.shape
    tr = _row_tile(R, TOKEN_TILE)

    def body(w_ref, g_ref, m_ref, v_ref, d_ref, mo_ref, vo_ref):
        gg = g_ref[...]
        mn = ADAM_B1 * m_ref[...] + (1.0 - ADAM_B1) * gg
        vn = ADAM_B2 * v_ref[...] + (1.0 - ADAM_B2) * (gg * gg)
        m_hat = mn / (1.0 - ADAM_B1 ** ADAM_STEP)
        v_hat = vn / (1.0 - ADAM_B2 ** ADAM_STEP)
        d_ref[...] = -ADAM_LR * (m_hat / (jnp.sqrt(v_hat) + ADAM_EPS) + ADAM_WD * w_ref[...])
        mo_ref[...] = mn
        vo_ref[...] = vn

    blk = pl.BlockSpec((tr, C), lambda i: (i, 0))
    return pl.pallas_call(
        body, name="adamw", grid=(R // tr,), in_specs=[blk] * 4, out_specs=[blk] * 3,
        out_shape=[jax.ShapeDtypeStruct((R, C), F32)] * 3,
        compiler_params=_cparams("parallel"),
    )(w, g, m, v)


def _adamw_nd(w, g, m, v):
    shape = w.shape
    two_d = (1, shape[0]) if len(shape) == 1 else (int(np.prod(shape[:-1])), shape[-1])
    outs = _adamw(w.reshape(two_d), g.reshape(two_d), m.reshape(two_d), v.reshape(two_d))
    return tuple(o.reshape(shape) for o in outs)


def _f32_as_bf16_pairs(a):
    return lax.bitcast_convert_type(a, BF16).reshape(a.shape[:-1] + (a.shape[-1] * 2,))


def _bf16_pairs_as_f32(a):
    return lax.bitcast_convert_type(a.reshape(a.shape[:-1] + (a.shape[-1] // 2, 2)), F32)


def _pack_misc(w_o, w_dq, w_uq, w_dkv, pool_w, pool_scale):
    lead = w_o.shape[:-3]
    rows = [w_o, w_dq, w_uq, w_dkv, pool_w]
    flat = [r.astype(BF16).reshape(lead + (-1, REP_COLS)) for r in rows]
    ps = _f32_as_bf16_pairs(pool_scale.astype(F32)).reshape(lead + (1, -1))
    ps = jnp.concatenate([ps, jnp.zeros(lead + (1, REP_COLS - ps.shape[-1]), BF16)], axis=-1)
    used = sum(f.shape[-2] for f in flat) + 1
    pad = jnp.zeros(lead + (MISC_ROWS - used, REP_COLS), BF16)
    return jnp.concatenate(flat + [ps, pad], axis=-2)


def _unpack_misc(buf, shapes):
    out, r0 = [], 0
    for shp in shapes[:-1]:
        n = int(np.prod(shp)) // REP_COLS
        out.append(buf[:, r0:r0 + n].reshape((N_DEV,) + shp))
        r0 += n
    n_ps = int(np.prod(shapes[-1]))
    out.append(_bf16_pairs_as_f32(buf[:, r0, :2 * n_ps]).reshape((N_DEV,) + shapes[-1]))
    return out


def _cat_dev(a, axis):
    a = jnp.moveaxis(a, 0, axis)
    return a.reshape(a.shape[:axis] + (a.shape[axis] * a.shape[axis + 1],) + a.shape[axis + 2:])


def _split_dev(a, axis):
    a = a.reshape(a.shape[:axis] + (N_DEV, a.shape[axis] // N_DEV) + a.shape[axis + 1:])
    return jnp.moveaxis(a, axis, 0)


def kernel(x, ffn_pre_norm, ffn_pre_wg, ffn_pre_wu, ffn_pre_wd, mix_norm, ffn_post_norm, ffn_post_wg, ffn_post_wu, ffn_post_wd, pool_w, pool_scale, kv_in_norm, w_dkv, ckv_norm, w_uk, w_uv, q_lora_norm, w_dq, w_uq, w_o, final_norm, loss_target, m_ffn_pre_norm, m_ffn_pre_wg, m_ffn_pre_wu, m_ffn_pre_wd, m_mix_norm, m_ffn_post_norm, m_ffn_post_wg, m_ffn_post_wu, m_ffn_post_wd, m_pool_w, m_pool_scale, m_kv_in_norm, m_w_dkv, m_ckv_norm, m_w_uk, m_w_uv, m_q_lora_norm, m_w_dq, m_w_uq, m_w_o, m_final_norm, v_ffn_pre_norm, v_ffn_pre_wg, v_ffn_pre_wu, v_ffn_pre_wd, v_mix_norm, v_ffn_post_norm, v_ffn_post_wg, v_ffn_post_wu, v_ffn_post_wd, v_pool_w, v_pool_scale, v_kv_in_norm, v_w_dkv, v_ckv_norm, v_w_uk, v_w_uv, v_q_lora_norm, v_w_dq, v_w_uq, v_w_o, v_final_norm):
    weights = dict(ffn_pre_norm=ffn_pre_norm, ffn_pre_wg=ffn_pre_wg, ffn_pre_wu=ffn_pre_wu, ffn_pre_wd=ffn_pre_wd,
                   mix_norm=mix_norm, ffn_post_norm=ffn_post_norm, ffn_post_wg=ffn_post_wg,
                   ffn_post_wu=ffn_post_wu, ffn_post_wd=ffn_post_wd, pool_w=pool_w, pool_scale=pool_scale,
                   kv_in_norm=kv_in_norm, w_dkv=w_dkv, ckv_norm=ckv_norm, w_uk=w_uk, w_uv=w_uv,
                   q_lora_norm=q_lora_norm, w_dq=w_dq, w_uq=w_uq, w_o=w_o, final_norm=final_norm)
    moments_m = dict(ffn_pre_norm=m_ffn_pre_norm, ffn_pre_wg=m_ffn_pre_wg, ffn_pre_wu=m_ffn_pre_wu,
                     ffn_pre_wd=m_ffn_pre_wd, mix_norm=m_mix_norm, ffn_post_norm=m_ffn_post_norm,
                     ffn_post_wg=m_ffn_post_wg, ffn_post_wu=m_ffn_post_wu, ffn_post_wd=m_ffn_post_wd,
                     pool_w=m_pool_w, pool_scale=m_pool_scale, kv_in_norm=m_kv_in_norm, w_dkv=m_w_dkv,
                     ckv_norm=m_ckv_norm, w_uk=m_w_uk, w_uv=m_w_uv, q_lora_norm=m_q_lora_norm, w_dq=m_w_dq,
                     w_uq=m_w_uq, w_o=m_w_o, final_norm=m_final_norm)
    moments_v = dict(ffn_pre_norm=v_ffn_pre_norm, ffn_pre_wg=v_ffn_pre_wg, ffn_pre_wu=v_ffn_pre_wu,
                     ffn_pre_wd=v_ffn_pre_wd, mix_norm=v_mix_norm, ffn_post_norm=v_ffn_post_norm,
                     ffn_post_wg=v_ffn_post_wg, ffn_post_wu=v_ffn_post_wu, ffn_post_wd=v_ffn_post_wd,
                     pool_w=v_pool_w, pool_scale=v_pool_scale, kv_in_norm=v_kv_in_norm, w_dkv=v_w_dkv,
                     ckv_norm=v_ckv_norm, w_uk=v_w_uk, w_uv=v_w_uv, q_lora_norm=v_q_lora_norm, w_dq=v_w_dq,
                     w_uq=v_w_uq, w_o=v_w_o, final_norm=v_final_norm)
    order = list(weights)

    T, D = x.shape[1], x.shape[2]
    depth = ffn_pre_norm.shape[0]
    n_a = pool_w.shape[0]
    n_b = depth - n_a
    fs = ffn_pre_wd.shape[1]
    F = fs * N_DEV
    n_ffn = 2 * depth
    t_attn = min(ATTN_TILE, T)

    ffn_local = [
        jnp.stack([jnp.swapaxes(wg[l], 0, 1), jnp.swapaxes(wu[l], 0, 1), wd[l]]).astype(BF16)
        for l in range(depth)
        for wg, wu, wd in ((ffn_pre_wg, ffn_pre_wu, ffn_pre_wd), (ffn_post_wg, ffn_post_wu, ffn_post_wd))
    ]
    misc_local = _pack_misc(w_o, w_dq, w_uq.reshape(n_b, w_uq.shape[1], -1), w_dkv, pool_w, pool_scale)
    misc_shapes = (w_o.shape, w_dq.shape, (n_b, w_uq.shape[1], N_HEADS * D_QK), w_dkv.shape, pool_w.shape,
                   pool_scale.shape)
    (w0_all,) = _exchange([(ffn_local[0], "gather_mid")], "comm_all_gather")
    walls = [w0_all.reshape(3, F, D)] + [None] * (n_ffn - 1)

    def vec(a):
        return a.reshape(1, -1)

    def ffn_stage(e, h_in, norm):
        carry = [(ffn_local[e + 1], "gather_mid")] if e + 1 < n_ffn else []
        if e == 0:
            carry.append((misc_local, "to_all"))
        outs = _ffn_fwd(h_in, norm, walls[e], carry)
        if carry:
            walls[e + 1] = outs[3].reshape(3, F, D)
        return outs

    h = x.reshape(T, D)
    stage0 = ffn_stage(0, h, vec(ffn_pre_norm[0]))
    misc_all = stage0[4]
    o_blk, dq_blk, uq_blk, dkv_blk, pw_blk, ps_blk = _unpack_misc(misc_all, misc_shapes)
    w_o_f = _cat_dev(o_blk, 1)
    w_dq_f = _cat_dev(dq_blk, 1)
    w_uq_f = _cat_dev(uq_blk, 1).reshape(n_b, -1, N_HEADS, D_QK)
    w_dkv_f = _cat_dev(dkv_blk, 0)
    pool_w_f = _cat_dev(pw_blk, 2)
    pool_scale_f = _cat_dev(ps_blk, 1)
    rq = w_dq_f.shape[2]
    wqa, wqb = _rope_weight_pair(w_uq_f)
    wqa = wqa.reshape(n_b, rq, N_HEADS * D_HEAD_PAD)
    wqb = wqb.reshape(n_b, rq, N_HEADS * D_HEAD_PAD)
    wka, wkb = _rope_weight_pair(w_dkv_f)
    wuk = jnp.concatenate([w_uk, jnp.zeros_like(w_uk)], axis=-1).astype(BF16).reshape(D_NOPE, N_HEADS * D_HEAD_PAD)
    wuv = w_uv.astype(BF16).reshape(D_NOPE, N_HEADS * D_V)
    ca_q, ca_k, sb = _rope_tables(T)
    ca_q_scaled, sb_scaled = ca_q * _ATTN_SCALE_LOG2, sb * _ATTN_SCALE_LOG2

    saved = []
    k_all = v_all = craw = h_kv = None
    for l in range(depth):
        s = {"h0": h}
        h, s["g1"], s["u1"] = (stage0 if l == 0 else ffn_stage(2 * l, h, vec(ffn_pre_norm[l])))[:3]
        s["h1"] = h
        if l < n_a:
            h, s["y"] = _pool_fwd(h, vec(mix_norm[l]), pool_w_f[l], vec(pool_scale_f[l]))
        else:
            j = l - n_a
            s["cq"] = _mm_rows(h, w_dq_f[j], nt=False, out_dtype=F32, norm_g=vec(mix_norm[l]), name="q_down")
            s["q"] = _q_proj(s["cq"], vec(q_lora_norm[j]), wqa[j], wqb[j], ca_q_scaled, sb_scaled)
            s["o"], lse = _flash_fwd(s["q"], k_all, v_all)
            s["lse"] = lse.reshape(N_HEADS, T // t_attn, 1, t_attn)
            h = _mm_rows(s["o"], w_o_f[j], nt=False, out_dtype=F32, res=h, name="attn_out")
        s["h2"] = h
        h, s["g2"], s["u2"] = ffn_stage(2 * l + 1, h, vec(ffn_post_norm[l]))[:3]
        if l == n_a - 1:
            h_kv = h
            k_all, v_all, craw = _kv_proj(h, vec(kv_in_norm), wka, wkb, vec(ckv_norm), wuk, wuv, ca_k, sb)
        saved.append(s)

    dh, loss_part, d_final = _loss_head(h, vec(final_norm), loss_target.reshape(T, D))

    slabs, ffn_parts = [None] * n_ffn, [None] * n_ffn

    misc_parts = []
    big_rep_names = ["w_uk", "w_uv"]

    def packed_misc_grads():
        return _pack_misc(_split_dev(jnp.stack(d_wo), 1), _split_dev(jnp.stack(d_wdq), 1),
                          _split_dev(jnp.stack(d_wuq).reshape(n_b, rq, -1), 1), _split_dev(grads["w_dkv"], 0),
                          _split_dev(jnp.stack(d_pool_w), 2),
                          _split_dev(jnp.concatenate(d_pool_scale, axis=0), 1))

    def ffn_stage_bwd(e, dh_out, h_in, norm, gate, up):
        carry = [(slabs[e + 1].reshape(3, N_DEV, fs, D), "scatter_mid")] if e + 1 < n_ffn else []
        if e == 0:
            carry.append((packed_misc_grads(), "scatter_lead"))
        if e == 2 * n_a - 1:
            carry.append((jnp.concatenate([grads[n].reshape(-1, REP_COLS) for n in big_rep_names]), "to_all"))
        outs = _ffn_bwd_dx(dh_out, h_in, norm, gate, up, walls[e], carry)
        if e + 1 < n_ffn:
            ffn_parts[e + 1] = outs[6]
        if e == 0:
            misc_parts.insert(0, outs[7])
        if e == 2 * n_a - 1:
            misc_parts.append(outs[6 + len(carry) - 1])
        dh_in, dgt, dup, u_t, dy_t, dnorm = outs[:6]
        slabs[e] = _ffn_bwd_dw(dgt, dup, gate, up, u_t, dy_t)
        return dh_in, dnorm

    grads = {}
    d_pre, d_post, d_mix = [None] * depth, [None] * depth, [None] * depth
    d_pool_w, d_pool_scale = [None] * n_a, [None] * n_a
    d_qln, d_wdq, d_wuq, d_wo = [None] * n_b, [None] * n_b, [None] * n_b, [None] * n_b
    dks, dvs = [], []
    for l in reversed(range(depth)):
        s = saved[l]
        if l == n_a - 1:
            (dh, grads["kv_in_norm"], dwka, dwkb, grads["ckv_norm"], dwuk, dwuv) = _kv_proj_bwd(
                dks, dvs, dh, h_kv, vec(kv_in_norm), wka, wkb, craw, vec(ckv_norm), wuk, wuv, ca_k, sb)
            grads["w_dkv"] = _rope_weight_pair_grad(dwka, dwkb)
            grads["w_uk"] = dwuk.reshape(D_NOPE, N_HEADS, D_HEAD_PAD)[..., :D_NOPE]
            grads["w_uv"] = dwuv.reshape(D_NOPE, N_HEADS, D_V)
        dh, d_post[l] = ffn_stage_bwd(2 * l + 1, dh, s["h2"], vec(ffn_post_norm[l]), s["g2"], s["u2"])
        if l < n_a:
            dh, d_mix[l], d_pool_w[l], d_pool_scale[l] = _pool_bwd(
                dh, s["h1"], vec(mix_norm[l]), s["y"], pool_w_f[l], vec(pool_scale_f[l]))
        else:
            j = l - n_a
            d_wo[j] = _mm_tn(s["o"], dh, name="attn_out_dw")
            do = _mm_rows(dh, w_o_f[j], nt=True, out_dtype=BF16, name="attn_out_dx")
            delta_row = _attn_delta(s["o"], do)
            dq_t, dk_l, dv_l = _flash_bwd(s["q"], k_all, v_all, do, s["lse"], delta_row)
            dks.append(dk_l)
            dvs.append(dv_l)
            da, db, cqn, dcq, d_qln[j] = _q_proj_bwd(dq_t, s["cq"], vec(q_lora_norm[j]), wqa[j], wqb[j], ca_q, sb)
            dwa = _mm_tn(cqn, da, name="q_up_dw")
            dwb = _mm_tn(cqn, db, name="q_up_dw")
            d_wuq[j] = _rope_weight_pair_grad(dwa.reshape(rq, N_HEADS, D_HEAD_PAD),
                                              dwb.reshape(rq, N_HEADS, D_HEAD_PAD))
            d_wdq[j] = _mm_tn(s["h1"], dcq, norm_g=vec(mix_norm[l]), name="q_down_dw")
            dh, d_mix[l] = _proj_bwd(dcq, w_dq_f[j], s["h1"], vec(mix_norm[l]), dh, "q_down_dx")
        dh, d_pre[l] = ffn_stage_bwd(2 * l, dh, s["h0"], vec(ffn_pre_norm[l]), s["g1"], s["u1"])
    grad_x = dh.reshape(x.shape)

    rep_names = ["ffn_pre_norm", "mix_norm", "ffn_post_norm", "kv_in_norm", "ckv_norm", "q_lora_norm", "final_norm"]
    grads["ffn_pre_norm"] = jnp.concatenate(d_pre, axis=0)
    grads["mix_norm"] = jnp.concatenate(d_mix, axis=0)
    grads["ffn_post_norm"] = jnp.concatenate(d_post, axis=0)
    grads["q_lora_norm"] = jnp.concatenate(d_qln, axis=0)
    grads["final_norm"] = d_final
    rep_flat = jnp.concatenate([grads[n].reshape(-1) for n in rep_names] + [loss_part[0, :1]])
    n_rep = rep_flat.shape[0]
    rep_rows = -(-n_rep // (8 * REP_COLS)) * 8
    rep_g = jnp.concatenate([rep_flat, jnp.zeros((rep_rows * REP_COLS - n_rep,), F32)]).reshape(rep_rows, REP_COLS)
    ffn_parts[0], rep_parts = _exchange([(slabs[0].reshape(3, N_DEV, fs, D), "scatter_mid"), (rep_g, "to_all")],
                                        "comm_grad_exchange")
    ffn_sum = jnp.stack([_sum_parts(p.reshape(N_DEV, 3 * fs, D), fs, "sum_ffn").reshape(3, fs, D)
                         for p in ffn_parts])
    misc_sum_parts = _unpack_misc(misc_parts[0], misc_shapes)
    rep_sum = _sum_parts(rep_parts, _row_tile(rep_rows, 128), "sum_rep").reshape(-1)

    def sum_small(p):
        shp = p.shape[1:]
        two_d = (int(np.prod(shp[:-1])), shp[-1])
        return _sum_parts(p.reshape((N_DEV,) + two_d), two_d[0], "sum_misc").reshape(shp)

    g_wo, g_wdq, g_wuq, g_wdkv, g_pw, g_ps = [sum_small(p) for p in misc_sum_parts]
    grads.update(w_o=g_wo, w_dq=g_wdq, w_uq=g_wuq.reshape(w_uq.shape), w_dkv=g_wdkv, pool_w=g_pw, pool_scale=g_ps)
    for kind, (npre, npost) in enumerate((("ffn_pre_wg", "ffn_post_wg"), ("ffn_pre_wu", "ffn_post_wu"),
                                          ("ffn_pre_wd", "ffn_post_wd"))):
        pre = ffn_sum[0::2, kind]
        post = ffn_sum[1::2, kind]
        if kind < 2:
            pre, post = jnp.swapaxes(pre, 1, 2), jnp.swapaxes(post, 1, 2)
        grads[npre], grads[npost] = pre, post
    off = 0
    for n in rep_names:
        size = int(np.prod(weights[n].shape))
        grads[n] = rep_sum[off:off + size].reshape(weights[n].shape)
        off += size
    loss = rep_sum[off]
    big_rep_sum = _sum_parts(misc_parts[1], _row_tile(misc_parts[1].shape[1], 64), "sum_rep_big")
    off = 0
    for n in big_rep_names:
        rows = int(np.prod(weights[n].shape)) // REP_COLS
        grads[n] = big_rep_sum[off:off + rows].reshape(weights[n].shape)
        off += rows

    deltas, new_m, new_v = {}, {}, {}
    for n in order:
        deltas[n], new_m[n], new_v[n] = _adamw_nd(weights[n], grads[n], moments_m[n], moments_v[n])
    return (loss, grad_x, *[grads[n] for n in order], *[deltas[n] for n in order],
            *[new_m[n] for n in order], *[new_v[n] for n in order])
```

```python
import functools

import numpy as np
import jax
import jax.numpy as jnp
from jax import lax
from jax.experimental import pallas as pl
from jax.experimental.pallas import tpu as pltpu

F32, BF16 = jnp.float32, jnp.bfloat16
N_DEV = 8
RMS_EPS = 1e-6
N_HEADS = 16
D_NOPE, D_ROPE, D_V = 128, 64, 128
D_QK = D_NOPE + D_ROPE
D_HEAD_PAD = 256
HEAD_GROUP = 4
ROPE_THETA = 10000.0
POOL_WINDOWS = (2, 4, 8, 16)
POOL_HALO = 16
ADAM_LR, ADAM_B1, ADAM_B2, ADAM_EPS, ADAM_WD, ADAM_STEP = 0.001, 0.9, 0.999, 1e-08, 0.01, 10
NEG_BIG = -1e30
V7X_VMEM_LIMIT = 56 * 1024 * 1024
V7X_MXU_DIM = 256
TOKEN_TILE = 512
ATTN_TILE = 512
FFN_TILE = 256
MISC_ROWS = 864
REP_COLS = 1024


def _cparams(*sem):
    return pltpu.CompilerParams(dimension_semantics=sem, vmem_limit_bytes=V7X_VMEM_LIMIT)


def _dot(a, b):
    return lax.dot_general(a, b, (((1,), (0,)), ((), ())), preferred_element_type=F32)


def _dot_nt(a, b):
    return lax.dot_general(a, b, (((1,), (1,)), ((), ())), preferred_element_type=F32)


def _dot_tn(a, b):
    return lax.dot_general(a, b, (((0,), (0,)), ((), ())), preferred_element_type=F32)


def _rms_fwd(x, g):
    r = lax.rsqrt(jnp.mean(x * x, axis=-1, keepdims=True) + RMS_EPS)
    return (x * r) * g


def _rms_bwd(du, x, g):
    r = lax.rsqrt(jnp.mean(x * x, axis=-1, keepdims=True) + RMS_EPS)
    xh = x * r
    dg = jnp.sum(du * xh, axis=0, keepdims=True)
    dxh = du * g
    dx = r * (dxh - xh * jnp.mean(dxh * xh, axis=-1, keepdims=True))
    return dx, dg


def _sigmoid(x):
    return 1.0 / (1.0 + jnp.exp(-x))


def _split_bf16(x):
    hi = x.astype(BF16)
    lo = (x - hi.astype(F32)).astype(BF16)
    return hi, lo


def _full(shape):
    return pl.BlockSpec(shape, lambda *_: (0,) * len(shape))


def _resident(shape):
    return pl.BlockSpec(shape, lambda *_: (0,) * len(shape), pipeline_mode=pl.Buffered(1))


def _row_tile(rows, cap):
    for t in range(min(cap, rows) // 8 * 8, 0, -8):
        if rows % t == 0:
            return t
    return rows


def _peers():
    x, y, c = lax.axis_index("x"), lax.axis_index("y"), lax.axis_index("c")
    out = []
    for k in range(1, N_DEV):
        px = 1 - x if (k >> 2) & 1 else x
        py = 1 - y if (k >> 1) & 1 else y
        pc = 1 - c if k & 1 else c
        out.append(((px, py, pc), 4 * px + 2 * py + pc))
    return 4 * x + 2 * y + c, out


_ROUTES = {
    "gather_mid": (lambda ref, idx: ref, lambda ref, idx: ref.at[:, idx], lambda s: s[:1] + (N_DEV,) + s[1:]),
    "to_all": (lambda ref, idx: ref, lambda ref, idx: ref.at[idx], lambda s: (N_DEV,) + s),
    "scatter_mid": (lambda ref, idx: ref.at[:, idx], lambda ref, idx: ref.at[idx],
                    lambda s: (N_DEV, s[0]) + s[2:]),
    "scatter_lead": (lambda ref, idx: ref.at[idx], lambda ref, idx: ref.at[idx], lambda s: s),
}


def _route_fns(items):
    kinds = [kind for _, kind in items]
    return (lambda j, ref, idx: _ROUTES[kinds[j]][0](ref, idx)), (lambda j, ref, idx: _ROUTES[kinds[j]][1](ref, idx))


def _route_out_shapes(items):
    return [jax.ShapeDtypeStruct(_ROUTES[kind][2](arr.shape), arr.dtype) for arr, kind in items]


def _exchange(items, name):
    n = len(items)
    fns = _route_fns(items)

    def body(*refs):
        ins, outs, sems = refs[:n], refs[n:2 * n], refs[2 * n:]
        _exchange_start(ins, outs, sems, *fns)
        _exchange_wait(ins, outs, sems, *fns)

    any_spec = pl.BlockSpec(memory_space=pl.ANY)
    return pl.pallas_call(
        body, name=name, out_shape=_route_out_shapes(items),
        in_specs=[any_spec] * n, out_specs=[any_spec] * n,
        scratch_shapes=_exchange_sems(n),
    )(*[arr for arr, _ in items])


def _exchange_sems(n):
    return [pltpu.SemaphoreType.DMA((n, N_DEV - 1)), pltpu.SemaphoreType.DMA((n, N_DEV - 1)),
            pltpu.SemaphoreType.DMA((n,))]


def _own_copies(ins, outs, sems, src_of, dst_of):
    me, _ = _peers()
    return [pltpu.make_async_copy(src_of(j, ins[j], me), dst_of(j, outs[j], me), sems[2].at[j])
            for j in range(len(ins))]


def _remote_copies(ins, outs, sems, src_of, dst_of, receiving):
    me, peers = _peers()
    return [pltpu.make_async_remote_copy(
        src_ref=src_of(j, ins[j], pidx), dst_ref=dst_of(j, outs[j], pidx if receiving else me),
        send_sem=sems[0].at[j, k], recv_sem=sems[1].at[j, k],
        device_id=peer, device_id_type=pl.DeviceIdType.MESH)
        for k, (peer, pidx) in enumerate(peers) for j in range(len(ins))]


def _exchange_start(ins, outs, sems, src_of, dst_of):
    for cp in _own_copies(ins, outs, sems, src_of, dst_of):
        cp.start()
    for cp in _remote_copies(ins, outs, sems, src_of, dst_of, receiving=False):
        cp.start()


def _exchange_wait(ins, outs, sems, src_of, dst_of):
    for cp in _remote_copies(ins, outs, sems, src_of, dst_of, receiving=True):
        cp.wait_recv()
    for cp in _remote_copies(ins, outs, sems, src_of, dst_of, receiving=False):
        cp.wait_send()
    for cp in _own_copies(ins, outs, sems, src_of, dst_of):
        cp.wait()


def _f_parts(F):
    first = -(-(F // V7X_MXU_DIM) // 2) * V7X_MXU_DIM
    return (slice(0, first), slice(first, F))


def _carried(items):
    if not items:
        return [], [], [], []
    return ([arr for arr, _ in items], [pl.BlockSpec(memory_space=pl.ANY)] * len(items), _route_out_shapes(items),
            _exchange_sems(len(items)))


def _ffn_fwd(h, g, w3, carry=()):
    T, D = h.shape
    F = w3.shape[1]
    tm = min(TOKEN_TILE, T)
    nt = T // tm
    nc = len(carry)
    has_c = nc > 0
    c_in, c_specs, c_out, c_sems = _carried(carry)

    def body(*refs):
        h_ref, g_ref, w_ref = refs[:3]
        ho_ref, gate_ref, up_ref = refs[3 + nc:6 + nc]
        i = pl.program_id(0)
        if has_c:
            comm = (refs[3:3 + nc], refs[6 + nc:6 + 2 * nc], refs[6 + 2 * nc:], *_route_fns(carry))

            @pl.when(i == 0)
            def _():
                _exchange_start(*comm)

        x = h_ref[...]
        u = _rms_fwd(x, g_ref[...]).astype(BF16)
        acc = None
        for cols in _f_parts(F):
            gate = _dot_nt(u, w_ref[0, cols, :])
            up = _dot_nt(u, w_ref[1, cols, :])
            gate_ref[:, cols] = gate.astype(BF16)
            up_ref[:, cols] = up.astype(BF16)
            part = _dot((gate * _sigmoid(gate) * up).astype(BF16), w_ref[2, cols, :])
            acc = part if acc is None else acc + part
        ho_ref[...] = x + 0.5 * acc

        if has_c:
            @pl.when(i == nt - 1)
            def _():
                _exchange_wait(*comm)

    row = pl.BlockSpec((tm, D), lambda i: (i, 0))
    wide = pl.BlockSpec((tm, F), lambda i: (i, 0))
    return pl.pallas_call(
        body, name="ffn_fwd_gather" if has_c else "ffn_fwd", grid=(nt,),
        in_specs=[row, _full((1, D)), _resident(w3.shape)] + c_specs,
        out_specs=[row, wide, wide] + c_specs,
        out_shape=[jax.ShapeDtypeStruct((T, D), F32), jax.ShapeDtypeStruct((T, F), BF16),
                   jax.ShapeDtypeStruct((T, F), BF16)] + c_out,
        scratch_shapes=c_sems,
        compiler_params=_cparams("arbitrary"),
    )(h, g, w3, *c_in)


def _ffn_bwd_dx(dho, h, g, gate, up, w3, carry=()):
    T, D = h.shape
    F = w3.shape[1]
    tm = min(TOKEN_TILE // 2, T)
    nt = T // tm
    nc = len(carry)
    has_c = nc > 0
    c_in, c_specs, c_out, c_sems = _carried(carry)

    def body(*refs):
        dho_ref, h_ref, g_ref, gate_ref, up_ref, w_ref = refs[:6]
        dhi_ref, dgate_ref, dup_ref, u_ref, dy_ref, dg_ref = refs[6 + nc:12 + nc]
        i = pl.program_id(0)
        if has_c:
            comm = (refs[6:6 + nc], refs[12 + nc:12 + 2 * nc], refs[12 + 2 * nc:], *_route_fns(carry))

            @pl.when(i == 0)
            def _():
                _exchange_start(*comm)

        dho = dho_ref[...]
        x = h_ref[...]
        dy_f = 0.5 * dho
        dy = dy_f.astype(BF16)
        dy_ref[...] = dy_f.T.astype(BF16)
        u_ref[...] = _rms_fwd(x, g_ref[...]).T.astype(BF16)
        acc = None
        for cols in _f_parts(F):
            dact = _dot_nt(dy, w_ref[2, cols, :])
            gt = gate_ref[:, cols].astype(F32)
            sig = _sigmoid(gt)
            dup = (dact * (gt * sig)).astype(BF16)
            dgate = (dact * up_ref[:, cols].astype(F32) * (sig * (1.0 + gt * (1.0 - sig)))).astype(BF16)
            dup_ref[:, cols] = dup
            dgate_ref[:, cols] = dgate
            part = _dot(dgate, w_ref[0, cols, :]) + _dot(dup, w_ref[1, cols, :])
            acc = part if acc is None else acc + part
        dx, dg = _rms_bwd(acc, x, g_ref[...])
        dhi_ref[...] = dho + dx

        @pl.when(i == 0)
        def _():
            dg_ref[...] = jnp.zeros_like(dg_ref)

        dg_ref[...] += dg

        if has_c:
            @pl.when(i == nt - 1)
            def _():
                _exchange_wait(*comm)

    row = pl.BlockSpec((tm, D), lambda i: (i, 0))
    col = pl.BlockSpec((D, tm), lambda i: (0, i))
    wide = pl.BlockSpec((tm, F), lambda i: (i, 0))
    return pl.pallas_call(
        body, name="ffn_bwd_dx_scatter" if has_c else "ffn_bwd_dx", grid=(nt,),
        in_specs=[row, row, _full((1, D)), wide, wide, _resident(w3.shape)] + c_specs,
        out_specs=[row, wide, wide, col, col, _full((1, D))] + c_specs,
        out_shape=[jax.ShapeDtypeStruct((T, D), F32), jax.ShapeDtypeStruct((T, F), BF16),
                   jax.ShapeDtypeStruct((T, F), BF16), jax.ShapeDtypeStruct((D, T), BF16),
                   jax.ShapeDtypeStruct((D, T), BF16), jax.ShapeDtypeStruct((1, D), F32)] + c_out,
        scratch_shapes=c_sems,
        compiler_params=_cparams("arbitrary"),
    )(dho, h, g, gate, up, w3, *c_in)


def _ffn_bwd_dw(dgate, dup, gate, up, u_t, dy_t):
    T, F = gate.shape
    D = u_t.shape[0]
    tfw = F // 2
    tk = min(TOKEN_TILE, T)
    nk = T // tk

    def body(dgate_ref, dup_ref, gate_ref, up_ref, ut_ref, dyt_ref, out_ref, acc_sc):
        k = pl.program_id(1)

        @pl.when(k == 0)
        def _():
            acc_sc[...] = jnp.zeros_like(acc_sc)

        uu = ut_ref[...]
        acc_sc[0] += _dot(uu, dgate_ref[...])
        acc_sc[1] += _dot(uu, dup_ref[...])
        gt = gate_ref[...].astype(F32)
        act = (gt * _sigmoid(gt) * up_ref[...].astype(F32)).astype(BF16)
        acc_sc[2] += _dot(dyt_ref[...], act)

        @pl.when(k == nk - 1)
        def _():
            for kind in range(3):
                out_ref[kind] = acc_sc[kind].T.astype(BF16)

    blk = pl.BlockSpec((tk, tfw), lambda j, k: (k, j))
    col = pl.BlockSpec((D, tk), lambda j, k: (0, k))
    return pl.pallas_call(
        body, name="ffn_bwd_dw", grid=(F // tfw, nk),
        in_specs=[blk, blk, blk, blk, col, col],
        out_specs=pl.BlockSpec((3, tfw, D), lambda j, k: (0, j, 0)),
        out_shape=jax.ShapeDtypeStruct((3, F, D), BF16),
        scratch_shapes=[pltpu.VMEM((3, D, tfw), F32)],
        compiler_params=_cparams("parallel", "arbitrary"),
    )(dgate, dup, gate, up, u_t, dy_t)


def _mm_rows(a, b, *, nt, out_dtype, norm_g=None, res=None, name):
    T, K = a.shape
    N = b.shape[0] if nt else b.shape[1]
    tm = min(TOKEN_TILE, T)
    has_g, has_r = norm_g is not None, res is not None

    def body(*refs):
        a_ref, b_ref = refs[0], refs[1]
        o_ref = refs[-1]
        x = a_ref[...]
        if has_g:
            x = _rms_fwd(x, refs[2][...])
        x = x.astype(BF16)
        acc = _dot_nt(x, b_ref[...]) if nt else _dot(x, b_ref[...])
        if has_r:
            acc = refs[2 + has_g][...] + acc
        o_ref[...] = acc.astype(out_dtype)

    ins, specs = [a, b], [pl.BlockSpec((tm, K), lambda i: (i, 0)), _full(b.shape)]
    if has_g:
        ins.append(norm_g)
        specs.append(_full((1, K)))
    if has_r:
        ins.append(res)
        specs.append(pl.BlockSpec((tm, N), lambda i: (i, 0)))
    return pl.pallas_call(
        body, name=name, grid=(T // tm,), in_specs=specs,
        out_specs=pl.BlockSpec((tm, N), lambda i: (i, 0)),
        out_shape=jax.ShapeDtypeStruct((T, N), out_dtype),
        compiler_params=_cparams("parallel"),
    )(*ins)


def _mm_tn(a, b, *, norm_g=None, name):
    T, M = a.shape
    N = b.shape[1]
    tk = min(TOKEN_TILE, T)
    has_g = norm_g is not None

    def body(*refs):
        a_ref, b_ref, o_ref = refs[0], refs[1], refs[-1]

        @pl.when(pl.program_id(0) == 0)
        def _():
            o_ref[...] = jnp.zeros_like(o_ref)

        x = a_ref[...]
        if has_g:
            x = _rms_fwd(x, refs[2][...])
        o_ref[...] += _dot_tn(x.astype(BF16), b_ref[...].astype(BF16))

    ins = [a, b]
    specs = [pl.BlockSpec((tk, M), lambda k: (k, 0)), pl.BlockSpec((tk, N), lambda k: (k, 0))]
    if has_g:
        ins.append(norm_g)
        specs.append(_full((1, M)))
    return pl.pallas_call(
        body, name=name, grid=(T // tk,), in_specs=specs, out_specs=_full((M, N)),
        out_shape=jax.ShapeDtypeStruct((M, N), F32),
        compiler_params=_cparams("arbitrary"),
    )(*ins)


def _proj_bwd(dz, w, h, g, dh, name):
    T, D = h.shape
    N = w.shape[1]
    tm = min(TOKEN_TILE, T)

    def body(dz_ref, w_ref, h_ref, g_ref, dh_ref, o_ref, dg_ref):
        du = _dot_nt(dz_ref[...].astype(BF16), w_ref[...])
        dx, dg = _rms_bwd(du, h_ref[...], g_ref[...])
        o_ref[...] = dh_ref[...] + dx

        @pl.when(pl.program_id(0) == 0)
        def _():
            dg_ref[...] = jnp.zeros_like(dg_ref)

        dg_ref[...] += dg

    row = pl.BlockSpec((tm, D), lambda i: (i, 0))
    return pl.pallas_call(
        body, name=name, grid=(T // tm,),
        in_specs=[pl.BlockSpec((tm, N), lambda i: (i, 0)), _full((D, N)), row, _full((1, D)), row],
        out_specs=[row, _full((1, D))],
        out_shape=[jax.ShapeDtypeStruct((T, D), F32), jax.ShapeDtypeStruct((1, D), F32)],
        compiler_params=_cparams("arbitrary"),
    )(dz, w, h, g, dh)


def _pool_bands(tm):
    r = np.arange(tm)[:, None]
    c = np.arange(tm)[None, :]
    j = np.arange(POOL_HALO)[None, :]
    main, halo, main_t, halo_t = [], [], [], []
    for w in POOL_WINDOWS:
        main.append(((r - c >= 0) & (r - c < w)) / w)
        halo.append((r + POOL_HALO - j < w) / w)
        main_t.append(((c - r >= 0) & (c - r < w)) / w)
        halo_t.append((tm + j - r < w) / w)
    return tuple(jnp.asarray(np.stack(m), BF16) for m in (main, halo, main_t, halo_t))


def _pool_count_scale(i, tm, w):
    t = i * tm + lax.broadcasted_iota(jnp.int32, (tm, 1), 0)
    return w / jnp.minimum(t + 1, w).astype(F32)


def _pool_fwd(h, g, wp, scale):
    T, D = h.shape
    G, dg = len(POOL_WINDOWS), D // len(POOL_WINDOWS)
    tm = min(TOKEN_TILE, T)
    hb = tm // POOL_HALO
    bm, bh, _, _ = _pool_bands(tm)

    def body(h_ref, hh_ref, g_ref, wp_ref, sc_ref, bm_ref, bh_ref, ho_ref, y_ref):
        i = pl.program_id(0)
        x = h_ref[...]
        u = _rms_fwd(x, g_ref[...])
        uh = _rms_fwd(hh_ref[...], g_ref[...]) * (i > 0).astype(F32)
        for gi, w in enumerate(POOL_WINDOWS):
            cols = slice(gi * dg, (gi + 1) * dg)
            ug = u[:, cols]
            hi, lo = _split_bf16(ug)
            hhi, hlo = _split_bf16(uh[:, cols])
            s = (_dot(bm_ref[gi], hi) + _dot(bm_ref[gi], lo)
                 + _dot(bh_ref[gi], hhi) + _dot(bh_ref[gi], hlo))
            y = (s * _pool_count_scale(i, tm, w) - ug).astype(BF16)
            y_ref[:, cols] = y
            ho_ref[:, cols] = x[:, cols] + _dot(y, wp_ref[gi]) * sc_ref[:, cols]

    row = pl.BlockSpec((tm, D), lambda i: (i, 0))
    return pl.pallas_call(
        body, name="pool_fwd", grid=(T // tm,),
        in_specs=[row, pl.BlockSpec((POOL_HALO, D), lambda i: (jnp.maximum(i * hb - 1, 0), 0)),
                  _full((1, D)), _full((G, dg, dg)), _full((1, D)),
                  _full((G, tm, tm)), _full((G, tm, POOL_HALO))],
        out_specs=[row, row],
        out_shape=[jax.ShapeDtypeStruct((T, D), F32), jax.ShapeDtypeStruct((T, D), BF16)],
        compiler_params=_cparams("parallel"),
    )(h, h, g, wp, scale, bm, bh)


def _pool_bwd(dh, h, g, y, wp, scale):
    T, D = h.shape
    G, dg = len(POOL_WINDOWS), D // len(POOL_WINDOWS)
    tm = min(TOKEN_TILE, T)
    hb = tm // POOL_HALO
    nt = T // tm
    _, _, bmt, bht = _pool_bands(tm)

    def body(dh_ref, dhn_ref, h_ref, g_ref, y_ref, wp_ref, sc_ref, bmt_ref, bht_ref,
             o_ref, dg_ref, dwp_ref, dsc_ref, du_sc):
        i = pl.program_id(0)

        @pl.when(i == 0)
        def _():
            dg_ref[...] = jnp.zeros_like(dg_ref)
            dwp_ref[...] = jnp.zeros_like(dwp_ref)
            dsc_ref[...] = jnp.zeros_like(dsc_ref)

        dho = dh_ref[...]
        dz = dho * sc_ref[...]
        dzn = dhn_ref[...] * sc_ref[...] * (i < nt - 1).astype(F32)
        for gi, w in enumerate(POOL_WINDOWS):
            cols = slice(gi * dg, (gi + 1) * dg)
            yg = y_ref[:, cols]
            dzg = dz[:, cols].astype(BF16)
            dsc_ref[:, cols] += jnp.sum(dho[:, cols] * _dot(yg, wp_ref[gi]), axis=0, keepdims=True)
            dwp_ref[gi] += _dot_tn(yg, dzg)
            dy = _dot_nt(dzg, wp_ref[gi])
            dyn = _dot_nt(dzn[:, cols].astype(BF16), wp_ref[gi])
            hi, lo = _split_bf16(dy * _pool_count_scale(i, tm, w))
            nhi, nlo = _split_bf16(dyn)
            du_sc[:, cols] = (_dot(bmt_ref[gi], hi) + _dot(bmt_ref[gi], lo)
                              + _dot(bht_ref[gi], nhi) + _dot(bht_ref[gi], nlo) - dy)
        dx, dgp = _rms_bwd(du_sc[...], h_ref[...], g_ref[...])
        o_ref[...] = dho + dx
        dg_ref[...] += dgp

    row = pl.BlockSpec((tm, D), lambda i: (i, 0))
    return pl.pallas_call(
        body, name="pool_bwd", grid=(nt,),
        in_specs=[row, pl.BlockSpec((POOL_HALO, D), lambda i: (jnp.minimum((i + 1) * hb, T // POOL_HALO - 1), 0)),
                  row, _full((1, D)), row, _full((G, dg, dg)), _full((1, D)),
                  _full((G, tm, tm)), _full((G, tm, POOL_HALO))],
        out_specs=[row, _full((1, D)), _full((G, dg, dg)), _full((1, D))],
        out_shape=[jax.ShapeDtypeStruct((T, D), F32), jax.ShapeDtypeStruct((1, D), F32),
                   jax.ShapeDtypeStruct((G, dg, dg), F32), jax.ShapeDtypeStruct((1, D), F32)],
        scratch_shapes=[pltpu.VMEM((tm, D), F32)],
        compiler_params=_cparams("arbitrary"),
    )(dh, dh, h, g, y, wp, scale, bmt, bht)


def _rope_tables(T):
    pos = jnp.arange(T, dtype=F32)
    inv_freq = ROPE_THETA ** (-jnp.arange(0, D_ROPE, 2, dtype=F32) / D_ROPE)
    ang = pos[:, None] * inv_freq[None, :]
    cos2 = jnp.tile(jnp.cos(ang), (1, 2))
    sin2 = jnp.tile(jnp.sin(ang), (1, 2))
    pad = jnp.zeros((T, D_HEAD_PAD - D_QK), F32)
    ca_q = jnp.concatenate([jnp.ones((T, D_NOPE), F32), cos2, pad], axis=1)
    ca_k = jnp.concatenate([jnp.zeros((T, D_NOPE), F32), cos2, pad], axis=1)
    sb = jnp.concatenate([jnp.zeros((T, D_NOPE), F32), sin2, pad], axis=1)
    return ca_q, ca_k, sb


def _rope_weight_pair(w):
    half = D_ROPE // 2
    z_pad = jnp.zeros(w.shape[:-1] + (D_HEAD_PAD - D_QK,), w.dtype)
    z_nope = jnp.zeros(w.shape[:-1] + (D_NOPE,), w.dtype)
    wa = jnp.concatenate([w, z_pad], axis=-1)
    wb = jnp.concatenate([z_nope, -w[..., D_NOPE + half:], w[..., D_NOPE:D_NOPE + half], z_pad], axis=-1)
    return wa, wb


def _rope_weight_pair_grad(dwa, dwb):
    half = D_ROPE // 2
    d1 = dwa[..., D_NOPE:D_NOPE + half] + dwb[..., D_NOPE + half:D_QK]
    d2 = dwa[..., D_NOPE + half:D_QK] - dwb[..., D_NOPE:D_NOPE + half]
    return jnp.concatenate([dwa[..., :D_NOPE], d1, d2], axis=-1)


def _q_proj(cq, qg, wa, wb, ca, sb):
    T, R = cq.shape
    tm = min(TOKEN_TILE, T)
    P = D_HEAD_PAD
    GP = HEAD_GROUP * P

    def body(cq_ref, qg_ref, wa_ref, wb_ref, ca_ref, sb_ref, q_ref):
        c = _rms_fwd(cq_ref[...], qg_ref[...]).astype(BF16)
        ca = jnp.tile(ca_ref[...], (1, HEAD_GROUP))
        sb = jnp.tile(sb_ref[...], (1, HEAD_GROUP))
        q_ref[...] = (_dot(c, wa_ref[...]) * ca + _dot(c, wb_ref[...]) * sb).astype(BF16)

    tok = pl.BlockSpec((tm, P), lambda i, hh: (i, 0))
    wsp = pl.BlockSpec((R, GP), lambda i, hh: (0, hh))
    return pl.pallas_call(
        body, name="q_proj", grid=(T // tm, N_HEADS // HEAD_GROUP),
        in_specs=[pl.BlockSpec((tm, R), lambda i, hh: (i, 0)), _full((1, R)), wsp, wsp, tok, tok],
        out_specs=pl.BlockSpec((tm, GP), lambda i, hh: (i, hh)),
        out_shape=jax.ShapeDtypeStruct((T, N_HEADS * P), BF16),
        compiler_params=_cparams("parallel", "arbitrary"),
    )(cq, qg, wa, wb, ca, sb)


def _q_proj_bwd(dq, cq, qg, wa, wb, ca, sb):
    T, R = cq.shape
    tm = min(TOKEN_TILE, T)
    P = D_HEAD_PAD

    def body(dq_ref, cq_ref, qg_ref, wa_ref, wb_ref, ca_ref, sb_ref,
             da_ref, db_ref, cqn_ref, dcq_ref, dqg_ref, acc_sc):
        i, hh = pl.program_id(0), pl.program_id(1)

        @pl.when(hh == 0)
        def _():
            acc_sc[...] = jnp.zeros_like(acc_sc)
            cqn_ref[...] = _rms_fwd(cq_ref[...], qg_ref[...]).astype(BF16)

        d = dq_ref[...].T
        da = (d * jnp.tile(ca_ref[...], (1, HEAD_GROUP))).astype(BF16)
        db = (d * jnp.tile(sb_ref[...], (1, HEAD_GROUP))).astype(BF16)
        da_ref[...] = da
        db_ref[...] = db
        acc_sc[...] += _dot_nt(da, wa_ref[...]) + _dot_nt(db, wb_ref[...])

        @pl.when(hh == N_HEADS // HEAD_GROUP - 1)
        def _():
            dx, dg = _rms_bwd(acc_sc[...], cq_ref[...], qg_ref[...])
            dcq_ref[...] = dx

            @pl.when(i == 0)
            def _():
                dqg_ref[...] = jnp.zeros_like(dqg_ref)

            dqg_ref[...] += dg

    GP = HEAD_GROUP * P
    tok = pl.BlockSpec((tm, P), lambda i, hh: (i, 0))
    hd = pl.BlockSpec((tm, GP), lambda i, hh: (i, hh))
    wsp = pl.BlockSpec((R, GP), lambda i, hh: (0, hh))
    rr = pl.BlockSpec((tm, R), lambda i, hh: (i, 0))
    return pl.pallas_call(
        body, name="q_proj_bwd", grid=(T // tm, N_HEADS // HEAD_GROUP),
        in_specs=[pl.BlockSpec((GP, tm), lambda i, hh: (hh, i)), rr, _full((1, R)), wsp, wsp, tok, tok],
        out_specs=[hd, hd, rr, rr, _full((1, R))],
        out_shape=[jax.ShapeDtypeStruct((T, N_HEADS * P), BF16), jax.ShapeDtypeStruct((T, N_HEADS * P), BF16),
                   jax.ShapeDtypeStruct((T, R), BF16), jax.ShapeDtypeStruct((T, R), F32),
                   jax.ShapeDtypeStruct((1, R), F32)],
        scratch_shapes=[pltpu.VMEM((tm, R), F32)],
        compiler_params=_cparams("arbitrary", "arbitrary"),
    )(dq, cq, qg, wa, wb, ca, sb)


def _kv_proj(h, g_in, wka, wkb, g_c, wuk, wuv, ca, sb):
    T, D = h.shape
    tm = min(TOKEN_TILE, T)
    P, C = D_HEAD_PAD, D_NOPE

    def body(h_ref, gi_ref, wka_ref, wkb_ref, gc_ref, wuk_ref, wuv_ref, ca_ref, sb_ref, k_ref, v_ref, craw_ref):
        u = _rms_fwd(h_ref[...], gi_ref[...]).astype(BF16)
        kva = _dot(u, wka_ref[...])
        kvb = _dot(u, wkb_ref[...])
        craw = kva[:, :C]
        craw_ref[...] = craw
        c = _rms_fwd(craw, gc_ref[...]).astype(BF16)
        kr = kva * ca_ref[...] + kvb * sb_ref[...]
        kn = _dot(c, wuk_ref[...])
        for hh in range(N_HEADS):
            k_ref[:, hh * P:(hh + 1) * P] = (kn[:, hh * P:(hh + 1) * P] + kr).astype(BF16)
        v_ref[...] = _dot(c, wuv_ref[...]).astype(BF16)

    tok = pl.BlockSpec((tm, P), lambda i: (i, 0))
    return pl.pallas_call(
        body, name="kv_proj", grid=(T // tm,),
        in_specs=[pl.BlockSpec((tm, D), lambda i: (i, 0)), _full((1, D)), _full((D, P)), _full((D, P)),
                  _full((1, C)), _full(wuk.shape), _full(wuv.shape), tok, tok],
        out_specs=[pl.BlockSpec((tm, N_HEADS * P), lambda i: (i, 0)),
                   pl.BlockSpec((tm, N_HEADS * D_V), lambda i: (i, 0)), pl.BlockSpec((tm, C), lambda i: (i, 0))],
        out_shape=[jax.ShapeDtypeStruct((T, N_HEADS * P), BF16), jax.ShapeDtypeStruct((T, N_HEADS * D_V), BF16),
                   jax.ShapeDtypeStruct((T, C), F32)],
        compiler_params=_cparams("parallel"),
    )(h, g_in, wka, wkb, g_c, wuk, wuv, ca, sb)


def _kv_proj_bwd(dks, dvs, dh, h, g_in, wka, wkb, craw, g_c, wuk, wuv, ca, sb):
    T, D = h.shape
    tm = min(TOKEN_TILE // 2, T)
    P, C = D_HEAD_PAD, D_NOPE
    nl = len(dks)

    def body(*refs):
        dk_refs, dv_refs = refs[:nl], refs[nl:2 * nl]
        (dh_ref, h_ref, gi_ref, wka_ref, wkb_ref, craw_ref, gc_ref, wuk_ref, wuv_ref,
         ca_ref, sb_ref, o_ref, dgi_ref, dwka_ref, dwkb_ref, dgc_ref, dwuk_ref, dwuv_ref) = refs[2 * nl:]

        @pl.when(pl.program_id(0) == 0)
        def _():
            for r in (dgi_ref, dwka_ref, dwkb_ref, dgc_ref, dwuk_ref, dwuv_ref):
                r[...] = jnp.zeros_like(r)

        x = h_ref[...]
        u = _rms_fwd(x, gi_ref[...]).astype(BF16)
        craw = craw_ref[...]
        c = _rms_fwd(craw, gc_ref[...]).astype(BF16)
        dkf = sum(r[...] for r in dk_refs[1:]) + dk_refs[0][...]
        dkb = dkf.astype(BF16)
        dvb = (sum(r[...] for r in dv_refs[1:]) + dv_refs[0][...]).astype(BF16)
        dwuk_ref[...] += _dot_tn(c, dkb)
        dwuv_ref[...] += _dot_tn(c, dvb)
        dc = _dot_nt(dkb, wuk_ref[...]) + _dot_nt(dvb, wuv_ref[...])
        dkr = dkf[:, :P]
        for hh in range(1, N_HEADS):
            dkr = dkr + dkf[:, hh * P:(hh + 1) * P]
        dcraw, dgc = _rms_bwd(dc, craw, gc_ref[...])
        dgc_ref[...] += dgc
        dkva = jnp.concatenate([dcraw, (dkr * ca_ref[...])[:, C:]], axis=1).astype(BF16)
        dkvb = (dkr * sb_ref[...]).astype(BF16)
        dwka_ref[...] += _dot_tn(u, dkva)
        dwkb_ref[...] += _dot_tn(u, dkvb)
        du = _dot_nt(dkva, wka_ref[...]) + _dot_nt(dkvb, wkb_ref[...])
        dx, dgi = _rms_bwd(du, x, gi_ref[...])
        dgi_ref[...] += dgi
        o_ref[...] = dh_ref[...] + dx

    row = pl.BlockSpec((tm, D), lambda i: (i, 0))
    tok = pl.BlockSpec((tm, P), lambda i: (i, 0))
    return pl.pallas_call(
        body, name="kv_proj_bwd", grid=(T // tm,),
        in_specs=[pl.BlockSpec((tm, N_HEADS * P), lambda i: (i, 0))] * nl
        + [pl.BlockSpec((tm, N_HEADS * D_V), lambda i: (i, 0))] * nl
        + [row, row, _full((1, D)), _full((D, P)), _full((D, P)), pl.BlockSpec((tm, C), lambda i: (i, 0)),
           _full((1, C)), _full(wuk.shape), _full(wuv.shape), tok, tok],
        out_specs=[row, _full((1, D)), _full((D, P)), _full((D, P)), _full((1, C)),
                   _full(wuk.shape), _full(wuv.shape)],
        out_shape=[jax.ShapeDtypeStruct((T, D), F32), jax.ShapeDtypeStruct((1, D), F32),
                   jax.ShapeDtypeStruct((D, P), F32), jax.ShapeDtypeStruct((D, P), F32),
                   jax.ShapeDtypeStruct((1, C), F32), jax.ShapeDtypeStruct(wuk.shape, F32),
                   jax.ShapeDtypeStruct(wuv.shape, F32)],
        compiler_params=_cparams("arbitrary"),
    )(*dks, *dvs, dh, h, g_in, wka, wkb, craw, g_c, wuk, wuv, ca, sb)


_ATTN_SCALE = D_QK ** -0.5
_LOG2_E = 1.4426950408889634
_LN_2 = 0.6931471805599453
_ATTN_SCALE_LOG2 = _ATTN_SCALE * _LOG2_E


def _flash_fwd(q, k, v):
    T = q.shape[0]
    t = min(ATTN_TILE, T // 2)
    tq = 2 * t
    P = D_HEAD_PAD

    def body(q_ref, k_ref, v_ref, o_ref, lse_ref, m_sc, l_sc, acc_sc, s0_sc, s1_sc, p0_sc, p1_sc, a0_sc, a1_sc):
        qi = pl.program_id(1)
        n = 2 * (qi + 1)
        s_sc, p_sc, a_sc = (s0_sc, s1_sc), (p0_sc, p1_sc), (a0_sc, a1_sc)
        m_sc[...] = jnp.full_like(m_sc, NEG_BIG)
        l_sc[...] = jnp.zeros_like(l_sc)
        acc_sc[...] = jnp.zeros_like(acc_sc)

        def rows_of(c):
            return pl.ds(pl.multiple_of(c * t, t), t)

        def scores(c, slot):
            s_sc[slot][...] = _dot_nt(k_ref[rows_of(c), :], q_ref[...])

        def softmax(slot, key_offset):
            s_t = s_sc[slot][...]
            if key_offset is not None:
                rows = lax.broadcasted_iota(jnp.int32, (t, tq), 0) + key_offset
                s_t = jnp.where(rows <= lax.broadcasted_iota(jnp.int32, (t, tq), 1), s_t, NEG_BIG)
            m_prev = m_sc[...]
            m_new = jnp.maximum(m_prev, jnp.max(s_t, axis=0, keepdims=True))
            p_t = jnp.exp2(s_t - m_new)
            alpha = jnp.exp2(m_prev - m_new)
            l_sc[...] = alpha * l_sc[...] + jnp.sum(p_t, axis=0, keepdims=True)
            m_sc[...] = m_new
            p_sc[slot][...] = p_t.astype(BF16)
            a_sc[slot][...] = alpha

        def values(c, slot):
            acc_sc[...] = a_sc[slot][...] * acc_sc[...] + _dot_tn(v_ref[rows_of(c), :], p_sc[slot][...])

        def stage(c, slot, first=False, last=False, key_offset=None):
            if not first:
                values(c - 1, 1 - slot)
            if not last:
                scores(c + 1, 1 - slot)
            softmax(slot, key_offset)

        def drain():
            stage(n - 2, 0, key_offset=0)
            stage(n - 1, 1, last=True, key_offset=t)
            values(n - 1, 1)

        scores(0, 0)

        @pl.when(qi == 0)
        def _():
            stage(0, 0, first=True, key_offset=0)
            stage(1, 1, last=True, key_offset=t)
            values(1, 1)

        @pl.when(qi > 0)
        def _():
            stage(0, 0, first=True)

            def pair(j, carry):
                stage(1 + 2 * j, 1)
                stage(2 + 2 * j, 0)
                return carry

            lax.fori_loop(0, qi - 1, pair, 0)
            stage(n - 3, 1)
            drain()

        l = l_sc[...]
        o_ref[...] = (acc_sc[...] / l).T.astype(BF16)
        lse_ref[...] = m_sc[...] + jnp.log(l) * _LOG2_E

    return pl.pallas_call(
        body, name="flash_fwd", grid=(N_HEADS, T // tq),
        in_specs=[pl.BlockSpec((tq, P), lambda hh, i: (i, hh)), pl.BlockSpec((T, P), lambda hh, i: (0, hh)),
                  pl.BlockSpec((T, D_V), lambda hh, i: (0, hh))],
        out_specs=[pl.BlockSpec((tq, D_V), lambda hh, i: (i, hh)),
                   pl.BlockSpec((None, None, 1, tq), lambda hh, i: (hh, i, 0, 0))],
        out_shape=[jax.ShapeDtypeStruct((T, N_HEADS * D_V), BF16),
                   jax.ShapeDtypeStruct((N_HEADS, T // tq, 1, tq), F32)],
        scratch_shapes=[pltpu.VMEM((1, tq), F32), pltpu.VMEM((1, tq), F32), pltpu.VMEM((D_V, tq), F32),
                        pltpu.VMEM((t, tq), F32), pltpu.VMEM((t, tq), F32), pltpu.VMEM((t, tq), BF16),
                        pltpu.VMEM((t, tq), BF16), pltpu.VMEM((1, tq), F32), pltpu.VMEM((1, tq), F32)],
        compiler_params=_cparams("parallel", "arbitrary"),
    )(q, k, v)


def _attn_delta(o, do):
    T = o.shape[0]
    t = min(ATTN_TILE, T)

    def body(o_ref, do_ref, out_ref):
        ones = jnp.ones((8, D_V), BF16)
        for hh in range(N_HEADS):
            cols = slice(hh * D_V, (hh + 1) * D_V)
            hi, lo = _split_bf16(o_ref[:, cols].astype(F32) * do_ref[:, cols].astype(F32))
            out_ref[hh] = (_dot_nt(ones, hi) + _dot_nt(ones, lo))[0:1]

    tok = pl.BlockSpec((t, N_HEADS * D_V), lambda i: (i, 0))
    return pl.pallas_call(
        body, name="attn_delta", grid=(T // t,), in_specs=[tok, tok],
        out_specs=pl.BlockSpec((N_HEADS, None, 1, t), lambda i: (0, i, 0, 0)),
        out_shape=jax.ShapeDtypeStruct((N_HEADS, T // t, 1, t), F32),
        compiler_params=_cparams("parallel"),
    )(o, do)


def _flash_bwd(q, k, v, do, lse_row, delta_row):
    T = q.shape[0]
    t = min(ATTN_TILE, T // 2)
    tk = 2 * t
    nq = T // t
    P = D_HEAD_PAD

    def body(k_ref, v_ref, q_ref, do_ref, lse_ref, delta_ref, dqt_ref, dk_ref, dv_ref, dqt_sc, kt_sc,
             s0_sc, s1_sc, dp0_sc, dp1_sc, p0_sc, p1_sc, ds0_sc, ds1_sc):
        ki = pl.program_id(1)
        q0 = 2 * ki
        n = nq - q0
        s_sc, dp_sc, p_sc, ds_sc = (s0_sc, s1_sc), (dp0_sc, dp1_sc), (p0_sc, p1_sc), (ds0_sc, ds1_sc)
        kt_sc[...] = k_ref[...].astype(F32).T.astype(BF16)
        dk_ref[...] = jnp.zeros_like(dk_ref)
        dv_ref[...] = jnp.zeros_like(dv_ref)

        @pl.when(ki == 0)
        def _():
            dqt_sc[...] = jnp.zeros_like(dqt_sc)

        def rows_of(c):
            return pl.ds(pl.multiple_of((q0 + c) * t, t), t)

        def products(c, slot):
            s_sc[slot][...] = _dot_nt(k_ref[...], q_ref[rows_of(c), :])
            dp_sc[slot][...] = _dot_nt(v_ref[...], do_ref[rows_of(c), :])

        def elementwise(c, slot, query_offset):
            p_t = jnp.exp2(s_sc[slot][...] - lse_ref[q0 + c])
            if query_offset is not None:
                cols = lax.broadcasted_iota(jnp.int32, (tk, t), 1) + query_offset
                p_t = jnp.where(lax.broadcasted_iota(jnp.int32, (tk, t), 0) <= cols, p_t, 0.0)
            p_sc[slot][...] = p_t.astype(BF16)
            ds_sc[slot][...] = (p_t * (dp_sc[slot][...] - delta_ref[q0 + c])).astype(BF16)

        def gradients(c, slot):
            dv_ref[...] += _dot(p_sc[slot][...], do_ref[rows_of(c), :])
            ds_t = ds_sc[slot][...]
            dk_ref[...] += _dot(ds_t, q_ref[rows_of(c), :])
            dqt_sc[q0 + c] += _dot(kt_sc[...], ds_t)

        def stage(c, slot, first=False, last=False, query_offset=None):
            if not first:
                gradients(c - 1, 1 - slot)
            if not last:
                products(c + 1, 1 - slot)
            elementwise(c, slot, query_offset)

        products(0, 0)

        @pl.when(n == 2)
        def _():
            stage(0, 0, first=True, query_offset=0)
            stage(1, 1, last=True, query_offset=t)
            gradients(1, 1)

        @pl.when(n > 2)
        def _():
            stage(0, 0, first=True, query_offset=0)
            stage(1, 1, query_offset=t)

            def pair(j, carry):
                stage(2 + 2 * j, 0)
                stage(3 + 2 * j, 1)
                return carry

            lax.fori_loop(0, (n - 4) // 2, pair, 0)
            stage(n - 2, 0)
            stage(n - 1, 1, last=True)
            gradients(n - 1, 1)

        dqt_ref[:, :t] = dqt_sc[q0] * _ATTN_SCALE
        dqt_ref[:, t:] = dqt_sc[q0 + 1] * _ATTN_SCALE
        dk_ref[...] = dk_ref[...] * _LN_2

    kb = pl.BlockSpec((tk, P), lambda hh, i: (i, hh))
    vb = pl.BlockSpec((tk, D_V), lambda hh, i: (i, hh))
    stat = pl.BlockSpec((None, nq, 1, t), lambda hh, i: (hh, 0, 0, 0))
    return pl.pallas_call(
        body, name="flash_bwd", grid=(N_HEADS, T // tk),
        in_specs=[kb, vb,
                  pl.BlockSpec((T, P), lambda hh, i: (0, hh), pipeline_mode=pl.Buffered(1)),
                  pl.BlockSpec((T, D_V), lambda hh, i: (0, hh), pipeline_mode=pl.Buffered(1)),
                  stat, stat],
        out_specs=[pl.BlockSpec((P, tk), lambda hh, i: (hh, i)), kb, vb],
        out_shape=[jax.ShapeDtypeStruct((N_HEADS * P, T), F32), jax.ShapeDtypeStruct((T, N_HEADS * P), F32),
                   jax.ShapeDtypeStruct((T, N_HEADS * D_V), F32)],
        scratch_shapes=[pltpu.VMEM((nq, P, t), F32), pltpu.VMEM((P, tk), BF16)]
        + [pltpu.VMEM((tk, t), F32)] * 4 + [pltpu.VMEM((tk, t), BF16)] * 4,
        compiler_params=_cparams("arbitrary", "arbitrary"),
    )(k, v, q, do, lse_row, delta_row)


def _loss_head(h, g, target):
    T, D = h.shape
    tm = min(TOKEN_TILE, T)

    def body(h_ref, g_ref, t_ref, dh_ref, loss_ref, dg_ref):
        @pl.when(pl.program_id(0) == 0)
        def _():
            loss_ref[...] = jnp.zeros_like(loss_ref)
            dg_ref[...] = jnp.zeros_like(dg_ref)

        x = h_ref[...]
        err = _rms_fwd(x, g_ref[...]) - t_ref[...]
        per_tok = jnp.mean(err * err, axis=-1, keepdims=True)
        loss_ref[...] += 0.5 * jnp.sum(per_tok, axis=0, keepdims=True)
        dx, dg = _rms_bwd(err * (1.0 / D), x, g_ref[...])
        dh_ref[...] = dx
        dg_ref[...] += dg

    row = pl.BlockSpec((tm, D), lambda i: (i, 0))
    return pl.pallas_call(
        body, name="loss_head", grid=(T // tm,),
        in_specs=[row, _full((1, D)), row], out_specs=[row, _full((1, 128)), _full((1, D))],
        out_shape=[jax.ShapeDtypeStruct((T, D), F32), jax.ShapeDtypeStruct((1, 128), F32),
                   jax.ShapeDtypeStruct((1, D), F32)],
        compiler_params=_cparams("arbitrary"),
    )(h, g, target)


def _sum_parts(parts, tr, name):
    _, R, C = parts.shape

    def body(p_ref, o_ref):
        acc = p_ref[0].astype(F32)
        for j in range(1, N_DEV):
            acc = acc + p_ref[j].astype(F32)
        o_ref[...] = acc

    return pl.pallas_call(
        body, name=name, grid=(R // tr,),
        in_specs=[pl.BlockSpec((N_DEV, tr, C), lambda i: (0, i, 0))],
        out_specs=pl.BlockSpec((tr, C), lambda i: (i, 0)),
        out_shape=jax.ShapeDtypeStruct((R, C), F32),
        compiler_params=_cparams("parallel"),
    )(parts)


def _adamw(w, g, m, v):
    R, C = w.shape
    tr = _row_tile(R, TOKEN_TILE)

    def body(w_ref, g_ref, m_ref, v_ref, d_ref, mo_ref, vo_ref):
        gg = g_ref[...]
        mn = ADAM_B1 * m_ref[...] + (1.0 - ADAM_B1) * gg
        vn = ADAM_B2 * v_ref[...] + (1.0 - ADAM_B2) * (gg * gg)
        m_hat = mn / (1.0 - ADAM_B1 ** ADAM_STEP)
        v_hat = vn / (1.0 - ADAM_B2 ** ADAM_STEP)
        d_ref[...] = -ADAM_LR * (m_hat / (jnp.sqrt(v_hat) + ADAM_EPS) + ADAM_WD * w_ref[...])
        mo_ref[...] = mn
        vo_ref[...] = vn

    blk = pl.BlockSpec((tr, C), lambda i: (i, 0))
    return pl.pallas_call(
        body, name="adamw", grid=(R // tr,), in_specs=[blk] * 4, out_specs=[blk] * 3,
        out_shape=[jax.ShapeDtypeStruct((R, C), F32)] * 3,
        compiler_params=_cparams("parallel"),
    )(w, g, m, v)


def _adamw_nd(w, g, m, v):
    shape = w.shape
    two_d = (1, shape[0]) if len(shape) == 1 else (int(np.prod(shape[:-1])), shape[-1])
    outs = _adamw(w.reshape(two_d), g.reshape(two_d), m.reshape(two_d), v.reshape(two_d))
    return tuple(o.reshape(shape) for o in outs)


def _f32_as_bf16_pairs(a):
    return lax.bitcast_convert_type(a, BF16).reshape(a.shape[:-1] + (a.shape[-1] * 2,))


def _bf16_pairs_as_f32(a):
    return lax.bitcast_convert_type(a.reshape(a.shape[:-1] + (a.shape[-1] // 2, 2)), F32)


def _pack_misc(w_o, w_dq, w_uq, w_dkv, pool_w, pool_scale):
    lead = w_o.shape[:-3]
    rows = [w_o, w_dq, w_uq, w_dkv, pool_w]
    flat = [r.astype(BF16).reshape(lead + (-1, REP_COLS)) for r in rows]
    ps = _f32_as_bf16_pairs(pool_scale.astype(F32)).reshape(lead + (1, -1))
    ps = jnp.concatenate([ps, jnp.zeros(lead + (1, REP_COLS - ps.shape[-1]), BF16)], axis=-1)
    used = sum(f.shape[-2] for f in flat) + 1
    pad = jnp.zeros(lead + (MISC_ROWS - used, REP_COLS), BF16)
    return jnp.concatenate(flat + [ps, pad], axis=-2)


def _unpack_misc(buf, shapes):
    out, r0 = [], 0
    for shp in shapes[:-1]:
        n = int(np.prod(shp)) // REP_COLS
        out.append(buf[:, r0:r0 + n].reshape((N_DEV,) + shp))
        r0 += n
    n_ps = int(np.prod(shapes[-1]))
    out.append(_bf16_pairs_as_f32(buf[:, r0, :2 * n_ps]).reshape((N_DEV,) + shapes[-1]))
    return out


def _cat_dev(a, axis):
    a = jnp.moveaxis(a, 0, axis)
    return a.reshape(a.shape[:axis] + (a.shape[axis] * a.shape[axis + 1],) + a.shape[axis + 2:])


def _split_dev(a, axis):
    a = a.reshape(a.shape[:axis] + (N_DEV, a.shape[axis] // N_DEV) + a.shape[axis + 1:])
    return jnp.moveaxis(a, axis, 0)


def kernel(x, ffn_pre_norm, ffn_pre_wg, ffn_pre_wu, ffn_pre_wd, mix_norm, ffn_post_norm, ffn_post_wg, ffn_post_wu, ffn_post_wd, pool_w, pool_scale, kv_in_norm, w_dkv, ckv_norm, w_uk, w_uv, q_lora_norm, w_dq, w_uq, w_o, final_norm, loss_target, m_ffn_pre_norm, m_ffn_pre_wg, m_ffn_pre_wu, m_ffn_pre_wd, m_mix_norm, m_ffn_post_norm, m_ffn_post_wg, m_ffn_post_wu, m_ffn_post_wd, m_pool_w, m_pool_scale, m_kv_in_norm, m_w_dkv, m_ckv_norm, m_w_uk, m_w_uv, m_q_lora_norm, m_w_dq, m_w_uq, m_w_o, m_final_norm, v_ffn_pre_norm, v_ffn_pre_wg, v_ffn_pre_wu, v_ffn_pre_wd, v_mix_norm, v_ffn_post_norm, v_ffn_post_wg, v_ffn_post_wu, v_ffn_post_wd, v_pool_w, v_pool_scale, v_kv_in_norm, v_w_dkv, v_ckv_norm, v_w_uk, v_w_uv, v_q_lora_norm, v_w_dq, v_w_uq, v_w_o, v_final_norm):
    weights = dict(ffn_pre_norm=ffn_pre_norm, ffn_pre_wg=ffn_pre_wg, ffn_pre_wu=ffn_pre_wu, ffn_pre_wd=ffn_pre_wd,
                   mix_norm=mix_norm, ffn_post_norm=ffn_post_norm, ffn_post_wg=ffn_post_wg,
                   ffn_post_wu=ffn_post_wu, ffn_post_wd=ffn_post_wd, pool_w=pool_w, pool_scale=pool_scale,
                   kv_in_norm=kv_in_norm, w_dkv=w_dkv, ckv_norm=ckv_norm, w_uk=w_uk, w_uv=w_uv,
                   q_lora_norm=q_lora_norm, w_dq=w_dq, w_uq=w_uq, w_o=w_o, final_norm=final_norm)
    moments_m = dict(ffn_pre_norm=m_ffn_pre_norm, ffn_pre_wg=m_ffn_pre_wg, ffn_pre_wu=m_ffn_pre_wu,
                     ffn_pre_wd=m_ffn_pre_wd, mix_norm=m_mix_norm, ffn_post_norm=m_ffn_post_norm,
                     ffn_post_wg=m_ffn_post_wg, ffn_post_wu=m_ffn_post_wu, ffn_post_wd=m_ffn_post_wd,
                     pool_w=m_pool_w, pool_scale=m_pool_scale, kv_in_norm=m_kv_in_norm, w_dkv=m_w_dkv,
                     ckv_norm=m_ckv_norm, w_uk=m_w_uk, w_uv=m_w_uv, q_lora_norm=m_q_lora_norm, w_dq=m_w_dq,
                     w_uq=m_w_uq, w_o=m_w_o, final_norm=m_final_norm)
    moments_v = dict(ffn_pre_norm=v_ffn_pre_norm, ffn_pre_wg=v_ffn_pre_wg, ffn_pre_wu=v_ffn_pre_wu,
                     ffn_pre_wd=v_ffn_pre_wd, mix_norm=v_mix_norm, ffn_post_norm=v_ffn_post_norm,
                     ffn_post_wg=v_ffn_post_wg, ffn_post_wu=v_ffn_post_wu, ffn_post_wd=v_ffn_post_wd,
                     pool_w=v_pool_w, pool_scale=v_pool_scale, kv_in_norm=v_kv_in_norm, w_dkv=v_w_dkv,
                     ckv_norm=v_ckv_norm, w_uk=v_w_uk, w_uv=v_w_uv, q_lora_norm=v_q_lora_norm, w_dq=v_w_dq,
                     w_uq=v_w_uq, w_o=v_w_o, final_norm=v_final_norm)
    order = list(weights)

    T, D = x.shape[1], x.shape[2]
    depth = ffn_pre_norm.shape[0]
    n_a = pool_w.shape[0]
    n_b = depth - n_a
    fs = ffn_pre_wd.shape[1]
    F = fs * N_DEV
    n_ffn = 2 * depth
    t_attn = min(ATTN_TILE, T)

    ffn_local = [
        jnp.stack([jnp.swapaxes(wg[l], 0, 1), jnp.swapaxes(wu[l], 0, 1), wd[l]]).astype(BF16)
        for l in range(depth)
        for wg, wu, wd in ((ffn_pre_wg, ffn_pre_wu, ffn_pre_wd), (ffn_post_wg, ffn_post_wu, ffn_post_wd))
    ]
    misc_local = _pack_misc(w_o, w_dq, w_uq.reshape(n_b, w_uq.shape[1], -1), w_dkv, pool_w, pool_scale)
    misc_shapes = (w_o.shape, w_dq.shape, (n_b, w_uq.shape[1], N_HEADS * D_QK), w_dkv.shape, pool_w.shape,
                   pool_scale.shape)
    (w0_all,) = _exchange([(ffn_local[0], "gather_mid")], "comm_all_gather")
    walls = [w0_all.reshape(3, F, D)] + [None] * (n_ffn - 1)

    def vec(a):
        return a.reshape(1, -1)

    def ffn_stage(e, h_in, norm):
        carry = [(ffn_local[e + 1], "gather_mid")] if e + 1 < n_ffn else []
        if e == 0:
            carry.append((misc_local, "to_all"))
        outs = _ffn_fwd(h_in, norm, walls[e], carry)
        if carry:
            walls[e + 1] = outs[3].reshape(3, F, D)
        return outs

    h = x.reshape(T, D)
    stage0 = ffn_stage(0, h, vec(ffn_pre_norm[0]))
    misc_all = stage0[4]
    o_blk, dq_blk, uq_blk, dkv_blk, pw_blk, ps_blk = _unpack_misc(misc_all, misc_shapes)
    w_o_f = _cat_dev(o_blk, 1)
    w_dq_f = _cat_dev(dq_blk, 1)
    w_uq_f = _cat_dev(uq_blk, 1).reshape(n_b, -1, N_HEADS, D_QK)
    w_dkv_f = _cat_dev(dkv_blk, 0)
    pool_w_f = _cat_dev(pw_blk, 2)
    pool_scale_f = _cat_dev(ps_blk, 1)
    rq = w_dq_f.shape[2]
    wqa, wqb = _rope_weight_pair(w_uq_f)
    wqa = wqa.reshape(n_b, rq, N_HEADS * D_HEAD_PAD)
    wqb = wqb.reshape(n_b, rq, N_HEADS * D_HEAD_PAD)
    wka, wkb = _rope_weight_pair(w_dkv_f)
    wuk = jnp.concatenate([w_uk, jnp.zeros_like(w_uk)], axis=-1).astype(BF16).reshape(D_NOPE, N_HEADS * D_HEAD_PAD)
    wuv = w_uv.astype(BF16).reshape(D_NOPE, N_HEADS * D_V)
    ca_q, ca_k, sb = _rope_tables(T)
    ca_q_scaled, sb_scaled = ca_q * _ATTN_SCALE_LOG2, sb * _ATTN_SCALE_LOG2

    saved = []
    k_all = v_all = craw = h_kv = None
    for l in range(depth):
        s = {"h0": h}
        h, s["g1"], s["u1"] = (stage0 if l == 0 else ffn_stage(2 * l, h, vec(ffn_pre_norm[l])))[:3]
        s["h1"] = h
        if l < n_a:
            h, s["y"] = _pool_fwd(h, vec(mix_norm[l]), pool_w_f[l], vec(pool_scale_f[l]))
        else:
            j = l - n_a
            s["cq"] = _mm_rows(h, w_dq_f[j], nt=False, out_dtype=F32, norm_g=vec(mix_norm[l]), name="q_down")
            s["q"] = _q_proj(s["cq"], vec(q_lora_norm[j]), wqa[j], wqb[j], ca_q_scaled, sb_scaled)
            s["o"], lse = _flash_fwd(s["q"], k_all, v_all)
            s["lse"] = lse.reshape(N_HEADS, T // t_attn, 1, t_attn)
            h = _mm_rows(s["o"], w_o_f[j], nt=False, out_dtype=F32, res=h, name="attn_out")
        s["h2"] = h
        h, s["g2"], s["u2"] = ffn_stage(2 * l + 1, h, vec(ffn_post_norm[l]))[:3]
        if l == n_a - 1:
            h_kv = h
            k_all, v_all, craw = _kv_proj(h, vec(kv_in_norm), wka, wkb, vec(ckv_norm), wuk, wuv, ca_k, sb)
        saved.append(s)

    dh, loss_part, d_final = _loss_head(h, vec(final_norm), loss_target.reshape(T, D))

    slabs, ffn_parts = [None] * n_ffn, [None] * n_ffn

    misc_parts = []
    big_rep_names = ["w_uk", "w_uv"]

    def packed_misc_grads():
        return _pack_misc(_split_dev(jnp.stack(d_wo), 1), _split_dev(jnp.stack(d_wdq), 1),
                          _split_dev(jnp.stack(d_wuq).reshape(n_b, rq, -1), 1), _split_dev(grads["w_dkv"], 0),
                          _split_dev(jnp.stack(d_pool_w), 2),
                          _split_dev(jnp.concatenate(d_pool_scale, axis=0), 1))

    def ffn_stage_bwd(e, dh_out, h_in, norm, gate, up):
        carry = [(slabs[e + 1].reshape(3, N_DEV, fs, D), "scatter_mid")] if e + 1 < n_ffn else []
        if e == 0:
            carry.append((packed_misc_grads(), "scatter_lead"))
        if e == 2 * n_a - 1:
            carry.append((jnp.concatenate([grads[n].reshape(-1, REP_COLS) for n in big_rep_names]), "to_all"))
        outs = _ffn_bwd_dx(dh_out, h_in, norm, gate, up, walls[e], carry)
        if e + 1 < n_ffn:
            ffn_parts[e + 1] = outs[6]
        if e == 0:
            misc_parts.insert(0, outs[7])
        if e == 2 * n_a - 1:
            misc_parts.append(outs[6 + len(carry) - 1])
        dh_in, dgt, dup, u_t, dy_t, dnorm = outs[:6]
        slabs[e] = _ffn_bwd_dw(dgt, dup, gate, up, u_t, dy_t)
        return dh_in, dnorm

    grads = {}
    d_pre, d_post, d_mix = [None] * depth, [None] * depth, [None] * depth
    d_pool_w, d_pool_scale = [None] * n_a, [None] * n_a
    d_qln, d_wdq, d_wuq, d_wo = [None] * n_b, [None] * n_b, [None] * n_b, [None] * n_b
    dks, dvs = [], []
    for l in reversed(range(depth)):
        s = saved[l]
        if l == n_a - 1:
            (dh, grads["kv_in_norm"], dwka, dwkb, grads["ckv_norm"], dwuk, dwuv) = _kv_proj_bwd(
                dks, dvs, dh, h_kv, vec(kv_in_norm), wka, wkb, craw, vec(ckv_norm), wuk, wuv, ca_k, sb)
            grads["w_dkv"] = _rope_weight_pair_grad(dwka, dwkb)
            grads["w_uk"] = dwuk.reshape(D_NOPE, N_HEADS, D_HEAD_PAD)[..., :D_NOPE]
            grads["w_uv"] = dwuv.reshape(D_NOPE, N_HEADS, D_V)
        dh, d_post[l] = ffn_stage_bwd(2 * l + 1, dh, s["h2"], vec(ffn_post_norm[l]), s["g2"], s["u2"])
        if l < n_a:
            dh, d_mix[l], d_pool_w[l], d_pool_scale[l] = _pool_bwd(
                dh, s["h1"], vec(mix_norm[l]), s["y"], pool_w_f[l], vec(pool_scale_f[l]))
        else:
            j = l - n_a
            d_wo[j] = _mm_tn(s["o"], dh, name="attn_out_dw")
            do = _mm_rows(dh, w_o_f[j], nt=True, out_dtype=BF16, name="attn_out_dx")
            delta_row = _attn_delta(s["o"], do)
            dq_t, dk_l, dv_l = _flash_bwd(s["q"], k_all, v_all, do, s["lse"], delta_row)
            dks.append(dk_l)
            dvs.append(dv_l)
            da, db, cqn, dcq, d_qln[j] = _q_proj_bwd(dq_t, s["cq"], vec(q_lora_norm[j]), wqa[j], wqb[j], ca_q, sb)
            dwa = _mm_tn(cqn, da, name="q_up_dw")
            dwb = _mm_tn(cqn, db, name="q_up_dw")
            d_wuq[j] = _rope_weight_pair_grad(dwa.reshape(rq, N_HEADS, D_HEAD_PAD),
                                              dwb.reshape(rq, N_HEADS, D_HEAD_PAD))
            d_wdq[j] = _mm_tn(s["h1"], dcq, norm_g=vec(mix_norm[l]), name="q_down_dw")
            dh, d_mix[l] = _proj_bwd(dcq, w_dq_f[j], s["h1"], vec(mix_norm[l]), dh, "q_down_dx")
        dh, d_pre[l] = ffn_stage_bwd(2 * l, dh, s["h0"], vec(ffn_pre_norm[l]), s["g1"], s["u1"])
    grad_x = dh.reshape(x.shape)

    rep_names = ["ffn_pre_norm", "mix_norm", "ffn_post_norm", "kv_in_norm", "ckv_norm", "q_lora_norm", "final_norm"]
    grads["ffn_pre_norm"] = jnp.concatenate(d_pre, axis=0)
    grads["mix_norm"] = jnp.concatenate(d_mix, axis=0)
    grads["ffn_post_norm"] = jnp.concatenate(d_post, axis=0)
    grads["q_lora_norm"] = jnp.concatenate(d_qln, axis=0)
    grads["final_norm"] = d_final
    rep_flat = jnp.concatenate([grads[n].reshape(-1) for n in rep_names] + [loss_part[0, :1]])
    n_rep = rep_flat.shape[0]
    rep_rows = -(-n_rep // (8 * REP_COLS)) * 8
    rep_g = jnp.concatenate([rep_flat, jnp.zeros((rep_rows * REP_COLS - n_rep,), F32)]).reshape(rep_rows, REP_COLS)
    ffn_parts[0], rep_parts = _exchange([(slabs[0].reshape(3, N_DEV, fs, D), "scatter_mid"), (rep_g, "to_all")],
                                        "comm_grad_exchange")
    ffn_sum = jnp.stack([_sum_parts(p.reshape(N_DEV, 3 * fs, D), fs, "sum_ffn").reshape(3, fs, D)
                         for p in ffn_parts])
    misc_sum_parts = _unpack_misc(misc_parts[0], misc_shapes)
    rep_sum = _sum_parts(rep_parts, _row_tile(rep_rows, 128), "sum_rep").reshape(-1)

    def sum_small(p):
        shp = p.shape[1:]
        two_d = (int(np.prod(shp[:-1])), shp[-1])
        return _sum_parts(p.reshape((N_DEV,) + two_d), two_d[0], "sum_misc").reshape(shp)

    g_wo, g_wdq, g_wuq, g_wdkv, g_pw, g_ps = [sum_small(p) for p in misc_sum_parts]
    grads.update(w_o=g_wo, w_dq=g_wdq, w_uq=g_wuq.reshape(w_uq.shape), w_dkv=g_wdkv, pool_w=g_pw, pool_scale=g_ps)
    for kind, (npre, npost) in enumerate((("ffn_pre_wg", "ffn_post_wg"), ("ffn_pre_wu", "ffn_post_wu"),
                                          ("ffn_pre_wd", "ffn_post_wd"))):
        pre = ffn_sum[0::2, kind]
        post = ffn_sum[1::2, kind]
        if kind < 2:
            pre, post = jnp.swapaxes(pre, 1, 2), jnp.swapaxes(post, 1, 2)
        grads[npre], grads[npost] = pre, post
    off = 0
    for n in rep_names:
        size = int(np.prod(weights[n].shape))
        grads[n] = rep_sum[off:off + size].reshape(weights[n].shape)
        off += size
    loss = rep_sum[off]
    big_rep_sum = _sum_parts(misc_parts[1], _row_tile(misc_parts[1].shape[1], 64), "sum_rep_big")
    off = 0
    for n in big_rep_names:
        rows = int(np.prod(weights[n].shape)) // REP_COLS
        grads[n] = big_rep_sum[off:off + rows].reshape(weights[n].shape)
        off += rows

    deltas, new_m, new_v = {}, {}, {}
    for n in order:
        deltas[n], new_m[n], new_v[n] = _adamw_nd(weights[n], grads[n], moments_m[n], moments_v[n])
    return (loss, grad_x, *[grads[n] for n in order], *[deltas[n] for n in order],
            *[new_m[n] for n in order], *[new_v[n] for n in order])
```

```python
import functools

import numpy as np
import jax
import jax.numpy as jnp
from jax import lax
from jax.experimental import pallas as pl
from jax.experimental.pallas import tpu as pltpu

F32, BF16 = jnp.float32, jnp.bfloat16
N_DEV = 8
RMS_EPS = 1e-6
N_HEADS = 16
D_NOPE, D_ROPE, D_V = 128, 64, 128
D_QK = D_NOPE + D_ROPE
D_HEAD_PAD = 256
HEAD_GROUP = 4
ROPE_THETA = 10000.0
POOL_WINDOWS = (2, 4, 8, 16)
POOL_HALO = 16
ADAM_LR, ADAM_B1, ADAM_B2, ADAM_EPS, ADAM_WD, ADAM_STEP = 0.001, 0.9, 0.999, 1e-08, 0.01, 10
NEG_BIG = -1e30
V7X_VMEM_LIMIT = 56 * 1024 * 1024
V7X_MXU_DIM = 256
TOKEN_TILE = 512
ATTN_TILE = 512
FFN_TILE = 256
MISC_ROWS = 864
REP_COLS = 1024


def _cparams(*sem):
    return pltpu.CompilerParams(dimension_semantics=sem, vmem_limit_bytes=V7X_VMEM_LIMIT)


def _dot(a, b):
    return lax.dot_general(a, b, (((1,), (0,)), ((), ())), preferred_element_type=F32)


def _dot_nt(a, b):
    return lax.dot_general(a, b, (((1,), (1,)), ((), ())), preferred_element_type=F32)


def _dot_tn(a, b):
    return lax.dot_general(a, b, (((0,), (0,)), ((), ())), preferred_element_type=F32)


def _rms_fwd(x, g):
    r = lax.rsqrt(jnp.mean(x * x, axis=-1, keepdims=True) + RMS_EPS)
    return (x * r) * g


def _rms_bwd(du, x, g):
    r = lax.rsqrt(jnp.mean(x * x, axis=-1, keepdims=True) + RMS_EPS)
    xh = x * r
    dg = jnp.sum(du * xh, axis=0, keepdims=True)
    dxh = du * g
    dx = r * (dxh - xh * jnp.mean(dxh * xh, axis=-1, keepdims=True))
    return dx, dg


def _sigmoid(x):
    return 1.0 / (1.0 + jnp.exp(-x))


def _split_bf16(x):
    hi = x.astype(BF16)
    lo = (x - hi.astype(F32)).astype(BF16)
    return hi, lo


def _full(shape):
    return pl.BlockSpec(shape, lambda *_: (0,) * len(shape))


def _resident(shape):
    return pl.BlockSpec(shape, lambda *_: (0,) * len(shape), pipeline_mode=pl.Buffered(1))


def _row_tile(rows, cap):
    for t in range(min(cap, rows) // 8 * 8, 0, -8):
        if rows % t == 0:
            return t
    return rows


def _peers():
    x, y, c = lax.axis_index("x"), lax.axis_index("y"), lax.axis_index("c")
    out = []
    for k in range(1, N_DEV):
        px = 1 - x if (k >> 2) & 1 else x
        py = 1 - y if (k >> 1) & 1 else y
        pc = 1 - c if k & 1 else c
        out.append(((px, py, pc), 4 * px + 2 * py + pc))
    return 4 * x + 2 * y + c, out


_ROUTES = {
    "gather_mid": (lambda ref, idx: ref, lambda ref, idx: ref.at[:, idx], lambda s: s[:1] + (N_DEV,) + s[1:]),
    "to_all": (lambda ref, idx: ref, lambda ref, idx: ref.at[idx], lambda s: (N_DEV,) + s),
    "scatter_mid": (lambda ref, idx: ref.at[:, idx], lambda ref, idx: ref.at[idx],
                    lambda s: (N_DEV, s[0]) + s[2:]),
    "scatter_lead": (lambda ref, idx: ref.at[idx], lambda ref, idx: ref.at[idx], lambda s: s),
}


def _route_fns(items):
    kinds = [kind for _, kind in items]
    return (lambda j, ref, idx: _ROUTES[kinds[j]][0](ref, idx)), (lambda j, ref, idx: _ROUTES[kinds[j]][1](ref, idx))


def _route_out_shapes(items):
    return [jax.ShapeDtypeStruct(_ROUTES[kind][2](arr.shape), arr.dtype) for arr, kind in items]


def _exchange(items, name):
    n = len(items)
    fns = _route_fns(items)

    def body(*refs):
        ins, outs, sems = refs[:n], refs[n:2 * n], refs[2 * n:]
        _exchange_start(ins, outs, sems, *fns)
        _exchange_wait(ins, outs, sems, *fns)

    any_spec = pl.BlockSpec(memory_space=pl.ANY)
    return pl.pallas_call(
        body, name=name, out_shape=_route_out_shapes(items),
        in_specs=[any_spec] * n, out_specs=[any_spec] * n,
        scratch_shapes=_exchange_sems(n),
    )(*[arr for arr, _ in items])


def _exchange_sems(n):
    return [pltpu.SemaphoreType.DMA((n, N_DEV - 1)), pltpu.SemaphoreType.DMA((n, N_DEV - 1)),
            pltpu.SemaphoreType.DMA((n,))]


def _own_copies(ins, outs, sems, src_of, dst_of):
    me, _ = _peers()
    return [pltpu.make_async_copy(src_of(j, ins[j], me), dst_of(j, outs[j], me), sems[2].at[j])
            for j in range(len(ins))]


def _remote_copies(ins, outs, sems, src_of, dst_of, receiving):
    me, peers = _peers()
    return [pltpu.make_async_remote_copy(
        src_ref=src_of(j, ins[j], pidx), dst_ref=dst_of(j, outs[j], pidx if receiving else me),
        send_sem=sems[0].at[j, k], recv_sem=sems[1].at[j, k],
        device_id=peer, device_id_type=pl.DeviceIdType.MESH)
        for k, (peer, pidx) in enumerate(peers) for j in range(len(ins))]


def _exchange_start(ins, outs, sems, src_of, dst_of):
    for cp in _own_copies(ins, outs, sems, src_of, dst_of):
        cp.start()
    for cp in _remote_copies(ins, outs, sems, src_of, dst_of, receiving=False):
        cp.start()


def _exchange_wait(ins, outs, sems, src_of, dst_of):
    for cp in _remote_copies(ins, outs, sems, src_of, dst_of, receiving=True):
        cp.wait_recv()
    for cp in _remote_copies(ins, outs, sems, src_of, dst_of, receiving=False):
        cp.wait_send()
    for cp in _own_copies(ins, outs, sems, src_of, dst_of):
        cp.wait()


def _f_parts(F):
    first = -(-(F // V7X_MXU_DIM) // 2) * V7X_MXU_DIM
    return (slice(0, first), slice(first, F))


def _carried(items):
    if not items:
        return [], [], [], []
    return ([arr for arr, _ in items], [pl.BlockSpec(memory_space=pl.ANY)] * len(items), _route_out_shapes(items),
            _exchange_sems(len(items)))


def _ffn_fwd(h, g, w3, carry=()):
    T, D = h.shape
    F = w3.shape[1]
    tm = min(TOKEN_TILE, T)
    nt = T // tm
    nc = len(carry)
    has_c = nc > 0
    c_in, c_specs, c_out, c_sems = _carried(carry)

    def body(*refs):
        h_ref, g_ref, w_ref = refs[:3]
        ho_ref, gate_ref, up_ref = refs[3 + nc:6 + nc]
        i = pl.program_id(0)
        if has_c:
            comm = (refs[3:3 + nc], refs[6 + nc:6 + 2 * nc], refs[6 + 2 * nc:], *_route_fns(carry))

            @pl.when(i == 0)
            def _():
                _exchange_start(*comm)

        x = h_ref[...]
        u = _rms_fwd(x, g_ref[...]).astype(BF16)
        acc = None
        for cols in _f_parts(F):
            gate = _dot_nt(u, w_ref[0, cols, :])
            up = _dot_nt(u, w_ref[1, cols, :])
            gate_ref[:, cols] = gate.astype(BF16)
            up_ref[:, cols] = up.astype(BF16)
            part = _dot((gate * _sigmoid(gate) * up).astype(BF16), w_ref[2, cols, :])
            acc = part if acc is None else acc + part
        ho_ref[...] = x + 0.5 * acc

        if has_c:
            @pl.when(i == nt - 1)
            def _():
                _exchange_wait(*comm)

    row = pl.BlockSpec((tm, D), lambda i: (i, 0))
    wide = pl.BlockSpec((tm, F), lambda i: (i, 0))
    return pl.pallas_call(
        body, name="ffn_fwd_gather" if has_c else "ffn_fwd", grid=(nt,),
        in_specs=[row, _full((1, D)), _resident(w3.shape)] + c_specs,
        out_specs=[row, wide, wide] + c_specs,
        out_shape=[jax.ShapeDtypeStruct((T, D), F32), jax.ShapeDtypeStruct((T, F), BF16),
                   jax.ShapeDtypeStruct((T, F), BF16)] + c_out,
        scratch_shapes=c_sems,
        compiler_params=_cparams("arbitrary"),
    )(h, g, w3, *c_in)


def _ffn_bwd_dx(dho, h, g, gate, up, w3, carry=()):
    T, D = h.shape
    F = w3.shape[1]
    tm = min(TOKEN_TILE // 2, T)
    nt = T // tm
    nc = len(carry)
    has_c = nc > 0
    c_in, c_specs, c_out, c_sems = _carried(carry)

    def body(*refs):
        dho_ref, h_ref, g_ref, gate_ref, up_ref, w_ref = refs[:6]
        dhi_ref, dgate_ref, dup_ref, u_ref, dy_ref, dg_ref = refs[6 + nc:12 + nc]
        i = pl.program_id(0)
        if has_c:
            comm = (refs[6:6 + nc], refs[12 + nc:12 + 2 * nc], refs[12 + 2 * nc:], *_route_fns(carry))

            @pl.when(i == 0)
            def _():
                _exchange_start(*comm)

        dho = dho_ref[...]
        x = h_ref[...]
        dy_f = 0.5 * dho
        dy = dy_f.astype(BF16)
        dy_ref[...] = dy_f.T.astype(BF16)
        u_ref[...] = _rms_fwd(x, g_ref[...]).T.astype(BF16)
        acc = None
        for cols in _f_parts(F):
            dact = _dot_nt(dy, w_ref[2, cols, :])
            gt = gate_ref[:, cols].astype(F32)
            sig = _sigmoid(gt)
            dup = (dact * (gt * sig)).astype(BF16)
            dgate = (dact * up_ref[:, cols].astype(F32) * (sig * (1.0 + gt * (1.0 - sig)))).astype(BF16)
            dup_ref[:, cols] = dup
            dgate_ref[:, cols] = dgate
            part = _dot(dgate, w_ref[0, cols, :]) + _dot(dup, w_ref[1, cols, :])
            acc = part if acc is None else acc + part
        dx, dg = _rms_bwd(acc, x, g_ref[...])
        dhi_ref[...] = dho + dx

        @pl.when(i == 0)
        def _():
            dg_ref[...] = jnp.zeros_like(dg_ref)

        dg_ref[...] += dg

        if has_c:
            @pl.when(i == nt - 1)
            def _():
                _exchange_wait(*comm)

    row = pl.BlockSpec((tm, D), lambda i: (i, 0))
    col = pl.BlockSpec((D, tm), lambda i: (0, i))
    wide = pl.BlockSpec((tm, F), lambda i: (i, 0))
    return pl.pallas_call(
        body, name="ffn_bwd_dx_scatter" if has_c else "ffn_bwd_dx", grid=(nt,),
        in_specs=[row, row, _full((1, D)), wide, wide, _resident(w3.shape)] + c_specs,
        out_specs=[row, wide, wide, col, col, _full((1, D))] + c_specs,
        out_shape=[jax.ShapeDtypeStruct((T, D), F32), jax.ShapeDtypeStruct((T, F), BF16),
                   jax.ShapeDtypeStruct((T, F), BF16), jax.ShapeDtypeStruct((D, T), BF16),
                   jax.ShapeDtypeStruct((D, T), BF16), jax.ShapeDtypeStruct((1, D), F32)] + c_out,
        scratch_shapes=c_sems,
        compiler_params=_cparams("arbitrary"),
    )(dho, h, g, gate, up, w3, *c_in)


def _ffn_bwd_dw(dgate, dup, gate, up, u_t, dy_t):
    T, F = gate.shape
    D = u_t.shape[0]
    tfw = F // 2
    tk = min(TOKEN_TILE, T)
    nk = T // tk

    def body(dgate_ref, dup_ref, gate_ref, up_ref, ut_ref, dyt_ref, out_ref, acc_sc):
        k = pl.program_id(1)

        @pl.when(k == 0)
        def _():
            acc_sc[...] = jnp.zeros_like(acc_sc)

        uu = ut_ref[...]
        acc_sc[0] += _dot(uu, dgate_ref[...])
        acc_sc[1] += _dot(uu, dup_ref[...])
        gt = gate_ref[...].astype(F32)
        act = (gt * _sigmoid(gt) * up_ref[...].astype(F32)).astype(BF16)
        acc_sc[2] += _dot(dyt_ref[...], act)

        @pl.when(k == nk - 1)
        def _():
            for kind in range(3):
                out_ref[kind] = acc_sc[kind].T.astype(BF16)

    blk = pl.BlockSpec((tk, tfw), lambda j, k: (k, j))
    col = pl.BlockSpec((D, tk), lambda j, k: (0, k))
    return pl.pallas_call(
        body, name="ffn_bwd_dw", grid=(F // tfw, nk),
        in_specs=[blk, blk, blk, blk, col, col],
        out_specs=pl.BlockSpec((3, tfw, D), lambda j, k: (0, j, 0)),
        out_shape=jax.ShapeDtypeStruct((3, F, D), BF16),
        scratch_shapes=[pltpu.VMEM((3, D, tfw), F32)],
        compiler_params=_cparams("parallel", "arbitrary"),
    )(dgate, dup, gate, up, u_t, dy_t)


def _mm_rows(a, b, *, nt, out_dtype, norm_g=None, res=None, name):
    T, K = a.shape
    N = b.shape[0] if nt else b.shape[1]
    tm = min(TOKEN_TILE, T)
    has_g, has_r = norm_g is not None, res is not None

    def body(*refs):
        a_ref, b_ref = refs[0], refs[1]
        o_ref = refs[-1]
        x = a_ref[...]
        if has_g:
            x = _rms_fwd(x, refs[2][...])
        x = x.astype(BF16)
        acc = _dot_nt(x, b_ref[...]) if nt else _dot(x, b_ref[...])
        if has_r:
            acc = refs[2 + has_g][...] + acc
        o_ref[...] = acc.astype(out_dtype)

    ins, specs = [a, b], [pl.BlockSpec((tm, K), lambda i: (i, 0)), _full(b.shape)]
    if has_g:
        ins.append(norm_g)
        specs.append(_full((1, K)))
    if has_r:
        ins.append(res)
        specs.append(pl.BlockSpec((tm, N), lambda i: (i, 0)))
    return pl.pallas_call(
        body, name=name, grid=(T // tm,), in_specs=specs,
        out_specs=pl.BlockSpec((tm, N), lambda i: (i, 0)),
        out_shape=jax.ShapeDtypeStruct((T, N), out_dtype),
        compiler_params=_cparams("parallel"),
    )(*ins)


def _mm_tn(a, b, *, norm_g=None, name):
    T, M = a.shape
    N = b.shape[1]
    tk = min(TOKEN_TILE, T)
    has_g = norm_g is not None

    def body(*refs):
        a_ref, b_ref, o_ref = refs[0], refs[1], refs[-1]

        @pl.when(pl.program_id(0) == 0)
        def _():
            o_ref[...] = jnp.zeros_like(o_ref)

        x = a_ref[...]
        if has_g:
            x = _rms_fwd(x, refs[2][...])
        o_ref[...] += _dot_tn(x.astype(BF16), b_ref[...].astype(BF16))

    ins = [a, b]
    specs = [pl.BlockSpec((tk, M), lambda k: (k, 0)), pl.BlockSpec((tk, N), lambda k: (k, 0))]
    if has_g:
        ins.append(norm_g)
        specs.append(_full((1, M)))
    return pl.pallas_call(
        body, name=name, grid=(T // tk,), in_specs=specs, out_specs=_full((M, N)),
        out_shape=jax.ShapeDtypeStruct((M, N), F32),
        compiler_params=_cparams("arbitrary"),
    )(*ins)


def _proj_bwd(dz, w, h, g, dh, name):
    T, D = h.shape
    N = w.shape[1]
    tm = min(TOKEN_TILE, T)

    def body(dz_ref, w_ref, h_ref, g_ref, dh_ref, o_ref, dg_ref):
        du = _dot_nt(dz_ref[...].astype(BF16), w_ref[...])
        dx, dg = _rms_bwd(du, h_ref[...], g_ref[...])
        o_ref[...] = dh_ref[...] + dx

        @pl.when(pl.program_id(0) == 0)
        def _():
            dg_ref[...] = jnp.zeros_like(dg_ref)

        dg_ref[...] += dg

    row = pl.BlockSpec((tm, D), lambda i: (i, 0))
    return pl.pallas_call(
        body, name=name, grid=(T // tm,),
        in_specs=[pl.BlockSpec((tm, N), lambda i: (i, 0)), _full((D, N)), row, _full((1, D)), row],
        out_specs=[row, _full((1, D))],
        out_shape=[jax.ShapeDtypeStruct((T, D), F32), jax.ShapeDtypeStruct((1, D), F32)],
        compiler_params=_cparams("arbitrary"),
    )(dz, w, h, g, dh)


def _pool_bands(tm):
    r = np.arange(tm)[:, None]
    c = np.arange(tm)[None, :]
    j = np.arange(POOL_HALO)[None, :]
    main, halo, main_t, halo_t = [], [], [], []
    for w in POOL_WINDOWS:
        main.append(((r - c >= 0) & (r - c < w)) / w)
        halo.append((r + POOL_HALO - j < w) / w)
        main_t.append(((c - r >= 0) & (c - r < w)) / w)
        halo_t.append((tm + j - r < w) / w)
    return tuple(jnp.asarray(np.stack(m), BF16) for m in (main, halo, main_t, halo_t))


def _pool_count_scale(i, tm, w):
    t = i * tm + lax.broadcasted_iota(jnp.int32, (tm, 1), 0)
    return w / jnp.minimum(t + 1, w).astype(F32)


def _pool_fwd(h, g, wp, scale):
    T, D = h.shape
    G, dg = len(POOL_WINDOWS), D // len(POOL_WINDOWS)
    tm = min(TOKEN_TILE, T)
    hb = tm // POOL_HALO
    bm, bh, _, _ = _pool_bands(tm)

    def body(h_ref, hh_ref, g_ref, wp_ref, sc_ref, bm_ref, bh_ref, ho_ref, y_ref):
        i = pl.program_id(0)
        x = h_ref[...]
        u = _rms_fwd(x, g_ref[...])
        uh = _rms_fwd(hh_ref[...], g_ref[...]) * (i > 0).astype(F32)
        for gi, w in enumerate(POOL_WINDOWS):
            cols = slice(gi * dg, (gi + 1) * dg)
            ug = u[:, cols]
            hi, lo = _split_bf16(ug)
            hhi, hlo = _split_bf16(uh[:, cols])
            s = (_dot(bm_ref[gi], hi) + _dot(bm_ref[gi], lo)
                 + _dot(bh_ref[gi], hhi) + _dot(bh_ref[gi], hlo))
            y = (s * _pool_count_scale(i, tm, w) - ug).astype(BF16)
            y_ref[:, cols] = y
            ho_ref[:, cols] = x[:, cols] + _dot(y, wp_ref[gi]) * sc_ref[:, cols]

    row = pl.BlockSpec((tm, D), lambda i: (i, 0))
    return pl.pallas_call(
        body, name="pool_fwd", grid=(T // tm,),
        in_specs=[row, pl.BlockSpec((POOL_HALO, D), lambda i: (jnp.maximum(i * hb - 1, 0), 0)),
                  _full((1, D)), _full((G, dg, dg)), _full((1, D)),
                  _full((G, tm, tm)), _full((G, tm, POOL_HALO))],
        out_specs=[row, row],
        out_shape=[jax.ShapeDtypeStruct((T, D), F32), jax.ShapeDtypeStruct((T, D), BF16)],
        compiler_params=_cparams("parallel"),
    )(h, h, g, wp, scale, bm, bh)


def _pool_bwd(dh, h, g, y, wp, scale):
    T, D = h.shape
    G, dg = len(POOL_WINDOWS), D // len(POOL_WINDOWS)
    tm = min(TOKEN_TILE, T)
    hb = tm // POOL_HALO
    nt = T // tm
    _, _, bmt, bht = _pool_bands(tm)

    def body(dh_ref, dhn_ref, h_ref, g_ref, y_ref, wp_ref, sc_ref, bmt_ref, bht_ref,
             o_ref, dg_ref, dwp_ref, dsc_ref, du_sc):
        i = pl.program_id(0)

        @pl.when(i == 0)
        def _():
            dg_ref[...] = jnp.zeros_like(dg_ref)
            dwp_ref[...] = jnp.zeros_like(dwp_ref)
            dsc_ref[...] = jnp.zeros_like(dsc_ref)

        dho = dh_ref[...]
        dz = dho * sc_ref[...]
        dzn = dhn_ref[...] * sc_ref[...] * (i < nt - 1).astype(F32)
        for gi, w in enumerate(POOL_WINDOWS):
            cols = slice(gi * dg, (gi + 1) * dg)
            yg = y_ref[:, cols]
            dzg = dz[:, cols].astype(BF16)
            dsc_ref[:, cols] += jnp.sum(dho[:, cols] * _dot(yg, wp_ref[gi]), axis=0, keepdims=True)
            dwp_ref[gi] += _dot_tn(yg, dzg)
            dy = _dot_nt(dzg, wp_ref[gi])
            dyn = _dot_nt(dzn[:, cols].astype(BF16), wp_ref[gi])
            hi, lo = _split_bf16(dy * _pool_count_scale(i, tm, w))
            nhi, nlo = _split_bf16(dyn)
            du_sc[:, cols] = (_dot(bmt_ref[gi], hi) + _dot(bmt_ref[gi], lo)
                              + _dot(bht_ref[gi], nhi) + _dot(bht_ref[gi], nlo) - dy)
        dx, dgp = _rms_bwd(du_sc[...], h_ref[...], g_ref[...])
        o_ref[...] = dho + dx
        dg_ref[...] += dgp

    row = pl.BlockSpec((tm, D), lambda i: (i, 0))
    return pl.pallas_call(
        body, name="pool_bwd", grid=(nt,),
        in_specs=[row, pl.BlockSpec((POOL_HALO, D), lambda i: (jnp.minimum((i + 1) * hb, T // POOL_HALO - 1), 0)),
                  row, _full((1, D)), row, _full((G, dg, dg)), _full((1, D)),
                  _full((G, tm, tm)), _full((G, tm, POOL_HALO))],
        out_specs=[row, _full((1, D)), _full((G, dg, dg)), _full((1, D))],
        out_shape=[jax.ShapeDtypeStruct((T, D), F32), jax.ShapeDtypeStruct((1, D), F32),
                   jax.ShapeDtypeStruct((G, dg, dg), F32), jax.ShapeDtypeStruct((1, D), F32)],
        scratch_shapes=[pltpu.VMEM((tm, D), F32)],
        compiler_params=_cparams("arbitrary"),
    )(dh, dh, h, g, y, wp, scale, bmt, bht)


def _rope_tables(T):
    pos = jnp.arange(T, dtype=F32)
    inv_freq = ROPE_THETA ** (-jnp.arange(0, D_ROPE, 2, dtype=F32) / D_ROPE)
    ang = pos[:, None] * inv_freq[None, :]
    cos2 = jnp.tile(jnp.cos(ang), (1, 2))
    sin2 = jnp.tile(jnp.sin(ang), (1, 2))
    pad = jnp.zeros((T, D_HEAD_PAD - D_QK), F32)
    ca_q = jnp.concatenate([jnp.ones((T, D_NOPE), F32), cos2, pad], axis=1)
    ca_k = jnp.concatenate([jnp.zeros((T, D_NOPE), F32), cos2, pad], axis=1)
    sb = jnp.concatenate([jnp.zeros((T, D_NOPE), F32), sin2, pad], axis=1)
    return ca_q, ca_k, sb


def _rope_weight_pair(w):
    half = D_ROPE // 2
    z_pad = jnp.zeros(w.shape[:-1] + (D_HEAD_PAD - D_QK,), w.dtype)
    z_nope = jnp.zeros(w.shape[:-1] + (D_NOPE,), w.dtype)
    wa = jnp.concatenate([w, z_pad], axis=-1)
    wb = jnp.concatenate([z_nope, -w[..., D_NOPE + half:], w[..., D_NOPE:D_NOPE + half], z_pad], axis=-1)
    return wa, wb


def _rope_weight_pair_grad(dwa, dwb):
    half = D_ROPE // 2
    d1 = dwa[..., D_NOPE:D_NOPE + half] + dwb[..., D_NOPE + half:D_QK]
    d2 = dwa[..., D_NOPE + half:D_QK] - dwb[..., D_NOPE:D_NOPE + half]
    return jnp.concatenate([dwa[..., :D_NOPE], d1, d2], axis=-1)


def _q_proj(cq, qg, wa, wb, ca, sb):
    T, R = cq.shape
    tm = min(TOKEN_TILE, T)
    P = D_HEAD_PAD
    GP = HEAD_GROUP * P

    def body(cq_ref, qg_ref, wa_ref, wb_ref, ca_ref, sb_ref, q_ref):
        c = _rms_fwd(cq_ref[...], qg_ref[...]).astype(BF16)
        ca = jnp.tile(ca_ref[...], (1, HEAD_GROUP))
        sb = jnp.tile(sb_ref[...], (1, HEAD_GROUP))
        q_ref[...] = (_dot(c, wa_ref[...]) * ca + _dot(c, wb_ref[...]) * sb).astype(BF16)

    tok = pl.BlockSpec((tm, P), lambda i, hh: (i, 0))
    wsp = pl.BlockSpec((R, GP), lambda i, hh: (0, hh))
    return pl.pallas_call(
        body, name="q_proj", grid=(T // tm, N_HEADS // HEAD_GROUP),
        in_specs=[pl.BlockSpec((tm, R), lambda i, hh: (i, 0)), _full((1, R)), wsp, wsp, tok, tok],
        out_specs=pl.BlockSpec((tm, GP), lambda i, hh: (i, hh)),
        out_shape=jax.ShapeDtypeStruct((T, N_HEADS * P), BF16),
        compiler_params=_cparams("parallel", "arbitrary"),
    )(cq, qg, wa, wb, ca, sb)


def _q_proj_bwd(dq, cq, qg, wa, wb, ca, sb):
    T, R = cq.shape
    tm = min(TOKEN_TILE // 2, T)
    P = D_HEAD_PAD
    GP = HEAD_GROUP * P

    def body(dq_ref, cq_ref, qg_ref, wa_ref, wb_ref, ca_ref, sb_ref, dcq_ref, dqg_ref, dwa_ref, dwb_ref):
        @pl.when(pl.program_id(0) == 0)
        def _():
            for r in (dqg_ref, dwa_ref, dwb_ref):
                r[...] = jnp.zeros_like(r)

        cq_f = cq_ref[...]
        cqn = _rms_fwd(cq_f, qg_ref[...]).astype(BF16)
        ca = jnp.tile(ca_ref[...], (1, HEAD_GROUP))
        sb = jnp.tile(sb_ref[...], (1, HEAD_GROUP))
        acc = None
        for grp in range(N_HEADS // HEAD_GROUP):
            cols = slice(grp * GP, (grp + 1) * GP)
            d = dq_ref[cols, :].T
            da = (d * ca).astype(BF16)
            db = (d * sb).astype(BF16)
            part = _dot_nt(da, wa_ref[:, cols]) + _dot_nt(db, wb_ref[:, cols])
            acc = part if acc is None else acc + part
            dwa_ref[:, cols] += _dot_tn(cqn, da)
            dwb_ref[:, cols] += _dot_tn(cqn, db)
        dx, dg = _rms_bwd(acc, cq_f, qg_ref[...])
        dcq_ref[...] = dx
        dqg_ref[...] += dg

    tok = pl.BlockSpec((tm, P), lambda i: (i, 0))
    rr = pl.BlockSpec((tm, R), lambda i: (i, 0))
    wfull = _full((R, N_HEADS * P))
    return pl.pallas_call(
        body, name="q_proj_bwd", grid=(T // tm,),
        in_specs=[pl.BlockSpec((N_HEADS * P, tm), lambda i: (0, i)), rr, _full((1, R)), wfull, wfull, tok, tok],
        out_specs=[rr, _full((1, R)), wfull, wfull],
        out_shape=[jax.ShapeDtypeStruct((T, R), F32), jax.ShapeDtypeStruct((1, R), F32),
                   jax.ShapeDtypeStruct((R, N_HEADS * P), F32), jax.ShapeDtypeStruct((R, N_HEADS * P), F32)],
        compiler_params=_cparams("arbitrary"),
    )(dq, cq, qg, wa, wb, ca, sb)


def _kv_proj(h, g_in, wka, wkb, g_c, wuk, wuv, ca, sb):
    T, D = h.shape
    tm = min(TOKEN_TILE, T)
    P, C = D_HEAD_PAD, D_NOPE

    def body(h_ref, gi_ref, wka_ref, wkb_ref, gc_ref, wuk_ref, wuv_ref, ca_ref, sb_ref, k_ref, v_ref, craw_ref):
        u = _rms_fwd(h_ref[...], gi_ref[...]).astype(BF16)
        kva = _dot(u, wka_ref[...])
        kvb = _dot(u, wkb_ref[...])
        craw = kva[:, :C]
        craw_ref[...] = craw
        c = _rms_fwd(craw, gc_ref[...]).astype(BF16)
        kr = kva * ca_ref[...] + kvb * sb_ref[...]
        kn = _dot(c, wuk_ref[...])
        for hh in range(N_HEADS):
            k_ref[:, hh * P:(hh + 1) * P] = (kn[:, hh * P:(hh + 1) * P] + kr).astype(BF16)
        v_ref[...] = _dot(c, wuv_ref[...]).astype(BF16)

    tok = pl.BlockSpec((tm, P), lambda i: (i, 0))
    return pl.pallas_call(
        body, name="kv_proj", grid=(T // tm,),
        in_specs=[pl.BlockSpec((tm, D), lambda i: (i, 0)), _full((1, D)), _full((D, P)), _full((D, P)),
                  _full((1, C)), _full(wuk.shape), _full(wuv.shape), tok, tok],
        out_specs=[pl.BlockSpec((tm, N_HEADS * P), lambda i: (i, 0)),
                   pl.BlockSpec((tm, N_HEADS * D_V), lambda i: (i, 0)), pl.BlockSpec((tm, C), lambda i: (i, 0))],
        out_shape=[jax.ShapeDtypeStruct((T, N_HEADS * P), BF16), jax.ShapeDtypeStruct((T, N_HEADS * D_V), BF16),
                   jax.ShapeDtypeStruct((T, C), F32)],
        compiler_params=_cparams("parallel"),
    )(h, g_in, wka, wkb, g_c, wuk, wuv, ca, sb)


def _kv_proj_bwd(dks, dvs, dh, h, g_in, wka, wkb, craw, g_c, wuk, wuv, ca, sb):
    T, D = h.shape
    tm = min(TOKEN_TILE // 2, T)
    P, C = D_HEAD_PAD, D_NOPE
    nl = len(dks)

    def body(*refs):
        dk_refs, dv_refs = refs[:nl], refs[nl:2 * nl]
        (dh_ref, h_ref, gi_ref, wka_ref, wkb_ref, craw_ref, gc_ref, wuk_ref, wuv_ref,
         ca_ref, sb_ref, o_ref, dgi_ref, dwka_ref, dwkb_ref, dgc_ref, dwuk_ref, dwuv_ref) = refs[2 * nl:]

        @pl.when(pl.program_id(0) == 0)
        def _():
            for r in (dgi_ref, dwka_ref, dwkb_ref, dgc_ref, dwuk_ref, dwuv_ref):
                r[...] = jnp.zeros_like(r)

        x = h_ref[...]
        u = _rms_fwd(x, gi_ref[...]).astype(BF16)
        craw = craw_ref[...]
        c = _rms_fwd(craw, gc_ref[...]).astype(BF16)
        dkf = sum(r[...] for r in dk_refs[1:]) + dk_refs[0][...]
        dkb = dkf.astype(BF16)
        dvb = (sum(r[...] for r in dv_refs[1:]) + dv_refs[0][...]).astype(BF16)
        dwuk_ref[...] += _dot_tn(c, dkb)
        dwuv_ref[...] += _dot_tn(c, dvb)
        dc = _dot_nt(dkb, wuk_ref[...]) + _dot_nt(dvb, wuv_ref[...])
        dkr = dkf[:, :P]
        for hh in range(1, N_HEADS):
            dkr = dkr + dkf[:, hh * P:(hh + 1) * P]
        dcraw, dgc = _rms_bwd(dc, craw, gc_ref[...])
        dgc_ref[...] += dgc
        dkva = jnp.concatenate([dcraw, (dkr * ca_ref[...])[:, C:]], axis=1).astype(BF16)
        dkvb = (dkr * sb_ref[...]).astype(BF16)
        dwka_ref[...] += _dot_tn(u, dkva)
        dwkb_ref[...] += _dot_tn(u, dkvb)
        du = _dot_nt(dkva, wka_ref[...]) + _dot_nt(dkvb, wkb_ref[...])
        dx, dgi = _rms_bwd(du, x, gi_ref[...])
        dgi_ref[...] += dgi
        o_ref[...] = dh_ref[...] + dx

    row = pl.BlockSpec((tm, D), lambda i: (i, 0))
    tok = pl.BlockSpec((tm, P), lambda i: (i, 0))
    return pl.pallas_call(
        body, name="kv_proj_bwd", grid=(T // tm,),
        in_specs=[pl.BlockSpec((tm, N_HEADS * P), lambda i: (i, 0))] * nl
        + [pl.BlockSpec((tm, N_HEADS * D_V), lambda i: (i, 0))] * nl
        + [row, row, _full((1, D)), _full((D, P)), _full((D, P)), pl.BlockSpec((tm, C), lambda i: (i, 0)),
           _full((1, C)), _full(wuk.shape), _full(wuv.shape), tok, tok],
        out_specs=[row, _full((1, D)), _full((D, P)), _full((D, P)), _full((1, C)),
                   _full(wuk.shape), _full(wuv.shape)],
        out_shape=[jax.ShapeDtypeStruct((T, D), F32), jax.ShapeDtypeStruct((1, D), F32),
                   jax.ShapeDtypeStruct((D, P), F32), jax.ShapeDtypeStruct((D, P), F32),
                   jax.ShapeDtypeStruct((1, C), F32), jax.ShapeDtypeStruct(wuk.shape, F32),
                   jax.ShapeDtypeStruct(wuv.shape, F32)],
        compiler_params=_cparams("arbitrary"),
    )(*dks, *dvs, dh, h, g_in, wka, wkb, craw, g_c, wuk, wuv, ca, sb)


_ATTN_SCALE = D_QK ** -0.5
_LOG2_E = 1.4426950408889634
_LN_2 = 0.6931471805599453
_ATTN_SCALE_LOG2 = _ATTN_SCALE * _LOG2_E


def _flash_fwd(q, k, v):
    T = q.shape[0]
    t = min(ATTN_TILE, T // 2)
    tq = 2 * t
    P = D_HEAD_PAD

    def body(q_ref, k_ref, v_ref, o_ref, lse_ref, m_sc, l_sc, acc_sc, s0_sc, s1_sc, p0_sc, p1_sc, a0_sc, a1_sc):
        qi = pl.program_id(1)
        n = 2 * (qi + 1)
        s_sc, p_sc, a_sc = (s0_sc, s1_sc), (p0_sc, p1_sc), (a0_sc, a1_sc)
        m_sc[...] = jnp.full_like(m_sc, NEG_BIG)
        l_sc[...] = jnp.zeros_like(l_sc)
        acc_sc[...] = jnp.zeros_like(acc_sc)

        def rows_of(c):
            return pl.ds(pl.multiple_of(c * t, t), t)

        def scores(c, slot):
            s_sc[slot][...] = _dot_nt(k_ref[rows_of(c), :], q_ref[...])

        def softmax(slot, key_offset):
            s_t = s_sc[slot][...]
            if key_offset is not None:
                rows = lax.broadcasted_iota(jnp.int32, (t, tq), 0) + key_offset
                s_t = jnp.where(rows <= lax.broadcasted_iota(jnp.int32, (t, tq), 1), s_t, NEG_BIG)
            m_prev = m_sc[...]
            m_new = jnp.maximum(m_prev, jnp.max(s_t, axis=0, keepdims=True))
            p_t = jnp.exp2(s_t - m_new)
            alpha = jnp.exp2(m_prev - m_new)
            l_sc[...] = alpha * l_sc[...] + jnp.sum(p_t, axis=0, keepdims=True)
            m_sc[...] = m_new
            p_sc[slot][...] = p_t.astype(BF16)
            a_sc[slot][...] = alpha

        def values(c, slot):
            acc_sc[...] = a_sc[slot][...] * acc_sc[...] + _dot_tn(v_ref[rows_of(c), :], p_sc[slot][...])

        def stage(c, slot, first=False, last=False, key_offset=None):
            if not first:
                values(c - 1, 1 - slot)
            if not last:
                scores(c + 1, 1 - slot)
            softmax(slot, key_offset)

        def drain():
            stage(n - 2, 0, key_offset=0)
            stage(n - 1, 1, last=True, key_offset=t)
            values(n - 1, 1)

        scores(0, 0)

        @pl.when(qi == 0)
        def _():
            stage(0, 0, first=True, key_offset=0)
            stage(1, 1, last=True, key_offset=t)
            values(1, 1)

        @pl.when(qi > 0)
        def _():
            stage(0, 0, first=True)

            def pair(j, carry):
                stage(1 + 2 * j, 1)
                stage(2 + 2 * j, 0)
                return carry

            lax.fori_loop(0, qi - 1, pair, 0)
            stage(n - 3, 1)
            drain()

        l = l_sc[...]
        o_ref[...] = (acc_sc[...] / l).T.astype(BF16)
        lse_ref[...] = m_sc[...] + jnp.log(l) * _LOG2_E

    return pl.pallas_call(
        body, name="flash_fwd", grid=(N_HEADS, T // tq),
        in_specs=[pl.BlockSpec((tq, P), lambda hh, i: (i, hh)), pl.BlockSpec((T, P), lambda hh, i: (0, hh)),
                  pl.BlockSpec((T, D_V), lambda hh, i: (0, hh))],
        out_specs=[pl.BlockSpec((tq, D_V), lambda hh, i: (i, hh)),
                   pl.BlockSpec((None, None, 1, tq), lambda hh, i: (hh, i, 0, 0))],
        out_shape=[jax.ShapeDtypeStruct((T, N_HEADS * D_V), BF16),
                   jax.ShapeDtypeStruct((N_HEADS, T // tq, 1, tq), F32)],
        scratch_shapes=[pltpu.VMEM((1, tq), F32), pltpu.VMEM((1, tq), F32), pltpu.VMEM((D_V, tq), F32),
                        pltpu.VMEM((t, tq), F32), pltpu.VMEM((t, tq), F32), pltpu.VMEM((t, tq), BF16),
                        pltpu.VMEM((t, tq), BF16), pltpu.VMEM((1, tq), F32), pltpu.VMEM((1, tq), F32)],
        compiler_params=_cparams("parallel", "arbitrary"),
    )(q, k, v)


def _attn_delta(o, do):
    T = o.shape[0]
    t = min(ATTN_TILE, T)

    def body(o_ref, do_ref, out_ref):
        ones = jnp.ones((8, D_V), BF16)
        for hh in range(N_HEADS):
            cols = slice(hh * D_V, (hh + 1) * D_V)
            hi, lo = _split_bf16(o_ref[:, cols].astype(F32) * do_ref[:, cols].astype(F32))
            out_ref[hh] = (_dot_nt(ones, hi) + _dot_nt(ones, lo))[0:1]

    tok = pl.BlockSpec((t, N_HEADS * D_V), lambda i: (i, 0))
    return pl.pallas_call(
        body, name="attn_delta", grid=(T // t,), in_specs=[tok, tok],
        out_specs=pl.BlockSpec((N_HEADS, None, 1, t), lambda i: (0, i, 0, 0)),
        out_shape=jax.ShapeDtypeStruct((N_HEADS, T // t, 1, t), F32),
        compiler_params=_cparams("parallel"),
    )(o, do)


def _flash_bwd(q, k, v, do, lse_row, delta_row):
    T = q.shape[0]
    t = min(ATTN_TILE, T // 2)
    tk = 2 * t
    nq = T // t
    P = D_HEAD_PAD

    def body(k_ref, v_ref, q_ref, do_ref, lse_ref, delta_ref, dqt_ref, dk_ref, dv_ref, dqt_sc, kt_sc,
             s0_sc, s1_sc, dp0_sc, dp1_sc, p0_sc, p1_sc, ds0_sc, ds1_sc):
        ki = pl.program_id(1)
        q0 = 2 * ki
        n = nq - q0
        s_sc, dp_sc, p_sc, ds_sc = (s0_sc, s1_sc), (dp0_sc, dp1_sc), (p0_sc, p1_sc), (ds0_sc, ds1_sc)
        kt_sc[...] = k_ref[...].astype(F32).T.astype(BF16)
        dk_ref[...] = jnp.zeros_like(dk_ref)
        dv_ref[...] = jnp.zeros_like(dv_ref)

        @pl.when(ki == 0)
        def _():
            dqt_sc[...] = jnp.zeros_like(dqt_sc)

        def rows_of(c):
            return pl.ds(pl.multiple_of((q0 + c) * t, t), t)

        def products(c, slot):
            s_sc[slot][...] = _dot_nt(k_ref[...], q_ref[rows_of(c), :])
            dp_sc[slot][...] = _dot_nt(v_ref[...], do_ref[rows_of(c), :])

        def elementwise(c, slot, query_offset):
            p_t = jnp.exp2(s_sc[slot][...] - lse_ref[q0 + c])
            if query_offset is not None:
                cols = lax.broadcasted_iota(jnp.int32, (tk, t), 1) + query_offset
                p_t = jnp.where(lax.broadcasted_iota(jnp.int32, (tk, t), 0) <= cols, p_t, 0.0)
            p_sc[slot][...] = p_t.astype(BF16)
            ds_sc[slot][...] = (p_t * (dp_sc[slot][...] - delta_ref[q0 + c])).astype(BF16)

        def gradients(c, slot):
            dv_ref[...] += _dot(p_sc[slot][...], do_ref[rows_of(c), :])
            ds_t = ds_sc[slot][...]
            dk_ref[...] += _dot(ds_t, q_ref[rows_of(c), :])
            dqt_sc[q0 + c] += _dot(kt_sc[...], ds_t)

        def stage(c, slot, first=False, last=False, query_offset=None):
            if not first:
                gradients(c - 1, 1 - slot)
            if not last:
                products(c + 1, 1 - slot)
            elementwise(c, slot, query_offset)

        products(0, 0)

        @pl.when(n == 2)
        def _():
            stage(0, 0, first=True, query_offset=0)
            stage(1, 1, last=True, query_offset=t)
            gradients(1, 1)

        @pl.when(n > 2)
        def _():
            stage(0, 0, first=True, query_offset=0)
            stage(1, 1, query_offset=t)

            def pair(j, carry):
                stage(2 + 2 * j, 0)
                stage(3 + 2 * j, 1)
                return carry

            lax.fori_loop(0, (n - 4) // 2, pair, 0)
            stage(n - 2, 0)
            stage(n - 1, 1, last=True)
            gradients(n - 1, 1)

        dqt_ref[:, :t] = dqt_sc[q0] * _ATTN_SCALE
        dqt_ref[:, t:] = dqt_sc[q0 + 1] * _ATTN_SCALE
        dk_ref[...] = dk_ref[...] * _LN_2

    kb = pl.BlockSpec((tk, P), lambda hh, i: (i, hh))
    vb = pl.BlockSpec((tk, D_V), lambda hh, i: (i, hh))
    stat = pl.BlockSpec((None, nq, 1, t), lambda hh, i: (hh, 0, 0, 0))
    return pl.pallas_call(
        body, name="flash_bwd", grid=(N_HEADS, T // tk),
        in_specs=[kb, vb,
                  pl.BlockSpec((T, P), lambda hh, i: (0, hh), pipeline_mode=pl.Buffered(1)),
                  pl.BlockSpec((T, D_V), lambda hh, i: (0, hh), pipeline_mode=pl.Buffered(1)),
                  stat, stat],
        out_specs=[pl.BlockSpec((P, tk), lambda hh, i: (hh, i)), kb, vb],
        out_shape=[jax.ShapeDtypeStruct((N_HEADS * P, T), F32), jax.ShapeDtypeStruct((T, N_HEADS * P), F32),
                   jax.ShapeDtypeStruct((T, N_HEADS * D_V), F32)],
        scratch_shapes=[pltpu.VMEM((nq, P, t), F32), pltpu.VMEM((P, tk), BF16)]
        + [pltpu.VMEM((tk, t), F32)] * 4 + [pltpu.VMEM((tk, t), BF16)] * 4,
        compiler_params=_cparams("arbitrary", "arbitrary"),
    )(k, v, q, do, lse_row, delta_row)


def _loss_head(h, g, target):
    T, D = h.shape
    tm = min(TOKEN_TILE, T)

    def body(h_ref, g_ref, t_ref, dh_ref, loss_ref, dg_ref):
        @pl.when(pl.program_id(0) == 0)
        def _():
            loss_ref[...] = jnp.zeros_like(loss_ref)
            dg_ref[...] = jnp.zeros_like(dg_ref)

        x = h_ref[...]
        err = _rms_fwd(x, g_ref[...]) - t_ref[...]
        per_tok = jnp.mean(err * err, axis=-1, keepdims=True)
        loss_ref[...] += 0.5 * jnp.sum(per_tok, axis=0, keepdims=True)
        dx, dg = _rms_bwd(err * (1.0 / D), x, g_ref[...])
        dh_ref[...] = dx
        dg_ref[...] += dg

    row = pl.BlockSpec((tm, D), lambda i: (i, 0))
    return pl.pallas_call(
        body, name="loss_head", grid=(T // tm,),
        in_specs=[row, _full((1, D)), row], out_specs=[row, _full((1, 128)), _full((1, D))],
        out_shape=[jax.ShapeDtypeStruct((T, D), F32), jax.ShapeDtypeStruct((1, 128), F32),
                   jax.ShapeDtypeStruct((1, D), F32)],
        compiler_params=_cparams("arbitrary"),
    )(h, g, target)


def _sum_parts(parts, tr, name):
    _, R, C = parts.shape

    def body(p_ref, o_ref):
        acc = p_ref[0].astype(F32)
        for j in range(1, N_DEV):
            acc = acc + p_ref[j].astype(F32)
        o_ref[...] = acc

    return pl.pallas_call(
        body, name=name, grid=(R // tr,),
        in_specs=[pl.BlockSpec((N_DEV, tr, C), lambda i: (0, i, 0))],
        out_specs=pl.BlockSpec((tr, C), lambda i: (i, 0)),
        out_shape=jax.ShapeDtypeStruct((R, C), F32),
        compiler_params=_cparams("parallel"),
    )(parts)


def _adamw(w, g, m, v):
    R, C = w.shape
    tr = _row_tile(R, TOKEN_TILE)

    def body(w_ref, g_ref, m_ref, v_ref, d_ref, mo_ref, vo_ref):
        gg = g_ref[...]
        mn = ADAM_B1 * m_ref[...] + (1.0 - ADAM_B1) * gg
        vn = ADAM_B2 * v_ref[...] + (1.0 - ADAM_B2) * (gg * gg)
        m_hat = mn / (1.0 - ADAM_B1 ** ADAM_STEP)
        v_hat = vn / (1.0 - ADAM_B2 ** ADAM_STEP)
        d_ref[...] = -ADAM_LR * (m_hat / (jnp.sqrt(v_hat) + ADAM_EPS) + ADAM_WD * w_ref[...])
        mo_ref[...] = mn
        vo_ref[...] = vn

    blk = pl.BlockSpec((tr, C), lambda i: (i, 0))
    return pl.pallas_call(
        body, name="adamw", grid=(R // tr,), in_specs=[blk] * 4, out_specs=[blk] * 3,
        out_shape=[jax.ShapeDtypeStruct((R, C), F32)] * 3,
        compiler_params=_cparams("parallel"),
    )(w, g, m, v)


def _adamw_nd(w, g, m, v):
    shape = w.shape
    two_d = (1, shape[0]) if len(shape) == 1 else (int(np.prod(shape[:-1])), shape[-1])
    outs = _adamw(w.reshape(two_d), g.reshape(two_d), m.reshape(two_d), v.reshape(two_d))
    return tuple(o.reshape(shape) for o in outs)


def _f32_as_bf16_pairs(a):
    return lax.bitcast_convert_type(a, BF16).reshape(a.shape[:-1] + (a.shape[-1] * 2,))


def _bf16_pairs_as_f32(a):
    return lax.bitcast_convert_type(a.reshape(a.shape[:-1] + (a.shape[-1] // 2, 2)), F32)


def _pack_misc(w_o, w_dq, w_uq, w_dkv, pool_w, pool_scale):
    lead = w_o.shape[:-3]
    rows = [w_o, w_dq, w_uq, w_dkv, pool_w]
    flat = [r.astype(BF16).reshape(lead + (-1, REP_COLS)) for r in rows]
    ps = _f32_as_bf16_pairs(pool_scale.astype(F32)).reshape(lead + (1, -1))
    ps = jnp.concatenate([ps, jnp.zeros(lead + (1, REP_COLS - ps.shape[-1]), BF16)], axis=-1)
    used = sum(f.shape[-2] for f in flat) + 1
    pad = jnp.zeros(lead + (MISC_ROWS - used, REP_COLS), BF16)
    return jnp.concatenate(flat + [ps, pad], axis=-2)


def _unpack_misc(buf, shapes):
    out, r0 = [], 0
    for shp in shapes[:-1]:
        n = int(np.prod(shp)) // REP_COLS
        out.append(buf[:, r0:r0 + n].reshape((N_DEV,) + shp))
        r0 += n
    n_ps = int(np.prod(shapes[-1]))
    out.append(_bf16_pairs_as_f32(buf[:, r0, :2 * n_ps]).reshape((N_DEV,) + shapes[-1]))
    return out


def _cat_dev(a, axis):
    a = jnp.moveaxis(a, 0, axis)
    return a.reshape(a.shape[:axis] + (a.shape[axis] * a.shape[axis + 1],) + a.shape[axis + 2:])


def _split_dev(a, axis):
    a = a.reshape(a.shape[:axis] + (N_DEV, a.shape[axis] // N_DEV) + a.shape[axis + 1:])
    return jnp.moveaxis(a, axis, 0)


def kernel(x, ffn_pre_norm, ffn_pre_wg, ffn_pre_wu, ffn_pre_wd, mix_norm, ffn_post_norm, ffn_post_wg, ffn_post_wu, ffn_post_wd, pool_w, pool_scale, kv_in_norm, w_dkv, ckv_norm, w_uk, w_uv, q_lora_norm, w_dq, w_uq, w_o, final_norm, loss_target, m_ffn_pre_norm, m_ffn_pre_wg, m_ffn_pre_wu, m_ffn_pre_wd, m_mix_norm, m_ffn_post_norm, m_ffn_post_wg, m_ffn_post_wu, m_ffn_post_wd, m_pool_w, m_pool_scale, m_kv_in_norm, m_w_dkv, m_ckv_norm, m_w_uk, m_w_uv, m_q_lora_norm, m_w_dq, m_w_uq, m_w_o, m_final_norm, v_ffn_pre_norm, v_ffn_pre_wg, v_ffn_pre_wu, v_ffn_pre_wd, v_mix_norm, v_ffn_post_norm, v_ffn_post_wg, v_ffn_post_wu, v_ffn_post_wd, v_pool_w, v_pool_scale, v_kv_in_norm, v_w_dkv, v_ckv_norm, v_w_uk, v_w_uv, v_q_lora_norm, v_w_dq, v_w_uq, v_w_o, v_final_norm):
    weights = dict(ffn_pre_norm=ffn_pre_norm, ffn_pre_wg=ffn_pre_wg, ffn_pre_wu=ffn_pre_wu, ffn_pre_wd=ffn_pre_wd,
                   mix_norm=mix_norm, ffn_post_norm=ffn_post_norm, ffn_post_wg=ffn_post_wg,
                   ffn_post_wu=ffn_post_wu, ffn_post_wd=ffn_post_wd, pool_w=pool_w, pool_scale=pool_scale,
                   kv_in_norm=kv_in_norm, w_dkv=w_dkv, ckv_norm=ckv_norm, w_uk=w_uk, w_uv=w_uv,
                   q_lora_norm=q_lora_norm, w_dq=w_dq, w_uq=w_uq, w_o=w_o, final_norm=final_norm)
    moments_m = dict(ffn_pre_norm=m_ffn_pre_norm, ffn_pre_wg=m_ffn_pre_wg, ffn_pre_wu=m_ffn_pre_wu,
                     ffn_pre_wd=m_ffn_pre_wd, mix_norm=m_mix_norm, ffn_post_norm=m_ffn_post_norm,
                     ffn_post_wg=m_ffn_post_wg, ffn_post_wu=m_ffn_post_wu, ffn_post_wd=m_ffn_post_wd,
                     pool_w=m_pool_w, pool_scale=m_pool_scale, kv_in_norm=m_kv_in_norm, w_dkv=m_w_dkv,
                     ckv_norm=m_ckv_norm, w_uk=m_w_uk, w_uv=m_w_uv, q_lora_norm=m_q_lora_norm, w_dq=m_w_dq,
                     w_uq=m_w_uq, w_o=m_w_o, final_norm=m_final_norm)
    moments_v = dict(ffn_pre_norm=v_ffn_pre_norm, ffn_pre_wg=v_ffn_pre_wg, ffn_pre_wu=v_ffn_pre_wu,
                     ffn_pre_wd=v_ffn_pre_wd, mix_norm=v_mix_norm, ffn_post_norm=v_ffn_post_norm,
                     ffn_post_wg=v_ffn_post_wg, ffn_post_wu=v_ffn_post_wu, ffn_post_wd=v_ffn_post_wd,
                     pool_w=v_pool_w, pool_scale=v_pool_scale, kv_in_norm=v_kv_in_norm, w_dkv=v_w_dkv,
                     ckv_norm=v_ckv_norm, w_uk=v_w_uk, w_uv=v_w_uv, q_lora_norm=v_q_lora_norm, w_dq=v_w_dq,
                     w_uq=v_w_uq, w_o=v_w_o, final_norm=v_final_norm)
    order = list(weights)

    T, D = x.shape[1], x.shape[2]
    depth = ffn_pre_norm.shape[0]
    n_a = pool_w.shape[0]
    n_b = depth - n_a
    fs = ffn_pre_wd.shape[1]
    F = fs * N_DEV
    n_ffn = 2 * depth
    t_attn = min(ATTN_TILE, T)

    ffn_local = [
        jnp.stack([jnp.swapaxes(wg[l], 0, 1), jnp.swapaxes(wu[l], 0, 1), wd[l]]).astype(BF16)
        for l in range(depth)
        for wg, wu, wd in ((ffn_pre_wg, ffn_pre_wu, ffn_pre_wd), (ffn_post_wg, ffn_post_wu, ffn_post_wd))
    ]
    misc_local = _pack_misc(w_o, w_dq, w_uq.reshape(n_b, w_uq.shape[1], -1), w_dkv, pool_w, pool_scale)
    misc_shapes = (w_o.shape, w_dq.shape, (n_b, w_uq.shape[1], N_HEADS * D_QK), w_dkv.shape, pool_w.shape,
                   pool_scale.shape)
    (w0_all,) = _exchange([(ffn_local[0], "gather_mid")], "comm_all_gather")
    walls = [w0_all.reshape(3, F, D)] + [None] * (n_ffn - 1)

    def vec(a):
        return a.reshape(1, -1)

    def ffn_stage(e, h_in, norm):
        carry = [(ffn_local[e + 1], "gather_mid")] if e + 1 < n_ffn else []
        if e == 0:
            carry.append((misc_local, "to_all"))
        outs = _ffn_fwd(h_in, norm, walls[e], carry)
        if carry:
            walls[e + 1] = outs[3].reshape(3, F, D)
        return outs

    h = x.reshape(T, D)
    stage0 = ffn_stage(0, h, vec(ffn_pre_norm[0]))
    misc_all = stage0[4]
    o_blk, dq_blk, uq_blk, dkv_blk, pw_blk, ps_blk = _unpack_misc(misc_all, misc_shapes)
    w_o_f = _cat_dev(o_blk, 1)
    w_dq_f = _cat_dev(dq_blk, 1)
    w_uq_f = _cat_dev(uq_blk, 1).reshape(n_b, -1, N_HEADS, D_QK)
    w_dkv_f = _cat_dev(dkv_blk, 0)
    pool_w_f = _cat_dev(pw_blk, 2)
    pool_scale_f = _cat_dev(ps_blk, 1)
    rq = w_dq_f.shape[2]
    wqa, wqb = _rope_weight_pair(w_uq_f)
    wqa = wqa.reshape(n_b, rq, N_HEADS * D_HEAD_PAD)
    wqb = wqb.reshape(n_b, rq, N_HEADS * D_HEAD_PAD)
    wka, wkb = _rope_weight_pair(w_dkv_f)
    wuk = jnp.concatenate([w_uk, jnp.zeros_like(w_uk)], axis=-1).astype(BF16).reshape(D_NOPE, N_HEADS * D_HEAD_PAD)
    wuv = w_uv.astype(BF16).reshape(D_NOPE, N_HEADS * D_V)
    ca_q, ca_k, sb = _rope_tables(T)
    ca_q_scaled, sb_scaled = ca_q * _ATTN_SCALE_LOG2, sb * _ATTN_SCALE_LOG2

    saved = []
    k_all = v_all = craw = h_kv = None
    for l in range(depth):
        s = {"h0": h}
        h, s["g1"], s["u1"] = (stage0 if l == 0 else ffn_stage(2 * l, h, vec(ffn_pre_norm[l])))[:3]
        s["h1"] = h
        if l < n_a:
            h, s["y"] = _pool_fwd(h, vec(mix_norm[l]), pool_w_f[l], vec(pool_scale_f[l]))
        else:
            j = l - n_a
            s["cq"] = _mm_rows(h, w_dq_f[j], nt=False, out_dtype=F32, norm_g=vec(mix_norm[l]), name="q_down")
            s["q"] = _q_proj(s["cq"], vec(q_lora_norm[j]), wqa[j], wqb[j], ca_q_scaled, sb_scaled)
            s["o"], lse = _flash_fwd(s["q"], k_all, v_all)
            s["lse"] = lse.reshape(N_HEADS, T // t_attn, 1, t_attn)
            h = _mm_rows(s["o"], w_o_f[j], nt=False, out_dtype=F32, res=h, name="attn_out")
        s["h2"] = h
        h, s["g2"], s["u2"] = ffn_stage(2 * l + 1, h, vec(ffn_post_norm[l]))[:3]
        if l == n_a - 1:
            h_kv = h
            k_all, v_all, craw = _kv_proj(h, vec(kv_in_norm), wka, wkb, vec(ckv_norm), wuk, wuv, ca_k, sb)
        saved.append(s)

    dh, loss_part, d_final = _loss_head(h, vec(final_norm), loss_target.reshape(T, D))

    slabs, ffn_parts = [None] * n_ffn, [None] * n_ffn

    misc_parts = []
    big_rep_names = ["w_uk", "w_uv"]

    def packed_misc_grads():
        return _pack_misc(_split_dev(jnp.stack(d_wo), 1), _split_dev(jnp.stack(d_wdq), 1),
                          _split_dev(jnp.stack(d_wuq).reshape(n_b, rq, -1), 1), _split_dev(grads["w_dkv"], 0),
                          _split_dev(jnp.stack(d_pool_w), 2),
                          _split_dev(jnp.concatenate(d_pool_scale, axis=0), 1))

    def ffn_stage_bwd(e, dh_out, h_in, norm, gate, up):
        carry = [(slabs[e + 1].reshape(3, N_DEV, fs, D), "scatter_mid")] if e + 1 < n_ffn else []
        if e == 0:
            carry.append((packed_misc_grads(), "scatter_lead"))
        if e == 2 * n_a - 1:
            carry.append((jnp.concatenate([grads[n].reshape(-1, REP_COLS) for n in big_rep_names]), "to_all"))
        outs = _ffn_bwd_dx(dh_out, h_in, norm, gate, up, walls[e], carry)
        if e + 1 < n_ffn:
            ffn_parts[e + 1] = outs[6]
        if e == 0:
            misc_parts.insert(0, outs[7])
        if e == 2 * n_a - 1:
            misc_parts.append(outs[6 + len(carry) - 1])
        dh_in, dgt, dup, u_t, dy_t, dnorm = outs[:6]
        slabs[e] = _ffn_bwd_dw(dgt, dup, gate, up, u_t, dy_t)
        return dh_in, dnorm

    grads = {}
    d_pre, d_post, d_mix = [None] * depth, [None] * depth, [None] * depth
    d_pool_w, d_pool_scale = [None] * n_a, [None] * n_a
    d_qln, d_wdq, d_wuq, d_wo = [None] * n_b, [None] * n_b, [None] * n_b, [None] * n_b
    dks, dvs = [], []
    for l in reversed(range(depth)):
        s = saved[l]
        if l == n_a - 1:
            (dh, grads["kv_in_norm"], dwka, dwkb, grads["ckv_norm"], dwuk, dwuv) = _kv_proj_bwd(
                dks, dvs, dh, h_kv, vec(kv_in_norm), wka, wkb, craw, vec(ckv_norm), wuk, wuv, ca_k, sb)
            grads["w_dkv"] = _rope_weight_pair_grad(dwka, dwkb)
            grads["w_uk"] = dwuk.reshape(D_NOPE, N_HEADS, D_HEAD_PAD)[..., :D_NOPE]
            grads["w_uv"] = dwuv.reshape(D_NOPE, N_HEADS, D_V)
        dh, d_post[l] = ffn_stage_bwd(2 * l + 1, dh, s["h2"], vec(ffn_post_norm[l]), s["g2"], s["u2"])
        if l < n_a:
            dh, d_mix[l], d_pool_w[l], d_pool_scale[l] = _pool_bwd(
                dh, s["h1"], vec(mix_norm[l]), s["y"], pool_w_f[l], vec(pool_scale_f[l]))
        else:
            j = l - n_a
            d_wo[j] = _mm_tn(s["o"], dh, name="attn_out_dw")
            do = _mm_rows(dh, w_o_f[j], nt=True, out_dtype=BF16, name="attn_out_dx")
            delta_row = _attn_delta(s["o"], do)
            dq_t, dk_l, dv_l = _flash_bwd(s["q"], k_all, v_all, do, s["lse"], delta_row)
            dks.append(dk_l)
            dvs.append(dv_l)
            dcq, d_qln[j], dwa, dwb = _q_proj_bwd(dq_t, s["cq"], vec(q_lora_norm[j]), wqa[j], wqb[j], ca_q, sb)
            d_wuq[j] = _rope_weight_pair_grad(dwa.reshape(rq, N_HEADS, D_HEAD_PAD),
                                              dwb.reshape(rq, N_HEADS, D_HEAD_PAD))
            d_wdq[j] = _mm_tn(s["h1"], dcq, norm_g=vec(mix_norm[l]), name="q_down_dw")
            dh, d_mix[l] = _proj_bwd(dcq, w_dq_f[j], s["h1"], vec(mix_norm[l]), dh, "q_down_dx")
        dh, d_pre[l] = ffn_stage_bwd(2 * l, dh, s["h0"], vec(ffn_pre_norm[l]), s["g1"], s["u1"])
    grad_x = dh.reshape(x.shape)

    rep_names = ["ffn_pre_norm", "mix_norm", "ffn_post_norm", "kv_in_norm", "ckv_norm", "q_lora_norm", "final_norm"]
    grads["ffn_pre_norm"] = jnp.concatenate(d_pre, axis=0)
    grads["mix_norm"] = jnp.concatenate(d_mix, axis=0)
    grads["ffn_post_norm"] = jnp.concatenate(d_post, axis=0)
    grads["q_lora_norm"] = jnp.concatenate(d_qln, axis=0)
    grads["final_norm"] = d_final
    rep_flat = jnp.concatenate([grads[n].reshape(-1) for n in rep_names] + [loss_part[0, :1]])
    n_rep = rep_flat.shape[0]
    rep_rows = -(-n_rep // (8 * REP_COLS)) * 8
    rep_g = jnp.concatenate([rep_flat, jnp.zeros((rep_rows * REP_COLS - n_rep,), F32)]).reshape(rep_rows, REP_COLS)
    ffn_parts[0], rep_parts = _exchange([(slabs[0].reshape(3, N_DEV, fs, D), "scatter_mid"), (rep_g, "to_all")],
                                        "comm_grad_exchange")
    ffn_sum = jnp.stack([_sum_parts(p.reshape(N_DEV, 3 * fs, D), fs, "sum_ffn").reshape(3, fs, D)
                         for p in ffn_parts])
    misc_sum_parts = _unpack_misc(misc_parts[0], misc_shapes)
    rep_sum = _sum_parts(rep_parts, _row_tile(rep_rows, 128), "sum_rep").reshape(-1)

    def sum_small(p):
        shp = p.shape[1:]
        two_d = (int(np.prod(shp[:-1])), shp[-1])
        return _sum_parts(p.reshape((N_DEV,) + two_d), two_d[0], "sum_misc").reshape(shp)

    g_wo, g_wdq, g_wuq, g_wdkv, g_pw, g_ps = [sum_small(p) for p in misc_sum_parts]
    grads.update(w_o=g_wo, w_dq=g_wdq, w_uq=g_wuq.reshape(w_uq.shape), w_dkv=g_wdkv, pool_w=g_pw, pool_scale=g_ps)
    for kind, (npre, npost) in enumerate((("ffn_pre_wg", "ffn_post_wg"), ("ffn_pre_wu", "ffn_post_wu"),
                                          ("ffn_pre_wd", "ffn_post_wd"))):
        pre = ffn_sum[0::2, kind]
        post = ffn_sum[1::2, kind]
        if kind < 2:
            pre, post = jnp.swapaxes(pre, 1, 2), jnp.swapaxes(post, 1, 2)
        grads[npre], grads[npost] = pre, post
    off = 0
    for n in rep_names:
        size = int(np.prod(weights[n].shape))
        grads[n] = rep_sum[off:off + size].reshape(weights[n].shape)
        off += size
    loss = rep_sum[off]
    big_rep_sum = _sum_parts(misc_parts[1], _row_tile(misc_parts[1].shape[1], 64), "sum_rep_big")
    off = 0
    for n in big_rep_names:
        rows = int(np.prod(weights[n].shape)) // REP_COLS
        grads[n] = big_rep_sum[off:off + rows].reshape(weights[n].shape)
        off += rows

    deltas, new_m, new_v = {}, {}, {}
    for n in order:
        deltas[n], new_m[n], new_v[n] = _adamw_nd(weights[n], grads[n], moments_m[n], moments_v[n])
    return (loss, grad_x, *[grads[n] for n in order], *[deltas[n] for n in order],
            *[new_m[n] for n in order], *[new_v[n] for n in order])
```

```python
import numpy as np
import jax
import jax.numpy as jnp
from jax import lax
from jax.experimental import pallas as pl
from jax.experimental.pallas import tpu as pltpu

F32, BF16 = jnp.float32, jnp.bfloat16
N_DEV = 8
RMS_EPS = 1e-6
N_HEADS = 16
D_NOPE, D_ROPE, D_V = 128, 64, 128
D_QK = D_NOPE + D_ROPE
D_HEAD_PAD = 256
HEAD_GROUP = 4
ROPE_THETA = 10000.0
POOL_WINDOWS = (2, 4, 8, 16)
POOL_HALO = 16
ADAM_LR, ADAM_B1, ADAM_B2, ADAM_EPS, ADAM_WD, ADAM_STEP = 0.001, 0.9, 0.999, 1e-08, 0.01, 10
NEG_BIG = -1e30
V7X_VMEM_LIMIT = 56 * 1024 * 1024
V7X_MXU_DIM = 256
TOKEN_TILE = 512
ATTN_TILE = 512
MISC_ROWS = 864
REP_COLS = 1024


def _cparams(*sem):
    return pltpu.CompilerParams(dimension_semantics=sem, vmem_limit_bytes=V7X_VMEM_LIMIT)


def _dot(a, b):
    return lax.dot_general(a, b, (((1,), (0,)), ((), ())), preferred_element_type=F32)


def _dot_nt(a, b):
    return lax.dot_general(a, b, (((1,), (1,)), ((), ())), preferred_element_type=F32)


def _dot_tn(a, b):
    return lax.dot_general(a, b, (((0,), (0,)), ((), ())), preferred_element_type=F32)


def _rms_fwd(x, g):
    r = lax.rsqrt(jnp.mean(x * x, axis=-1, keepdims=True) + RMS_EPS)
    return (x * r) * g


def _rms_bwd(du, x, g):
    r = lax.rsqrt(jnp.mean(x * x, axis=-1, keepdims=True) + RMS_EPS)
    xh = x * r
    dg = jnp.sum(du * xh, axis=0, keepdims=True)
    dxh = du * g
    dx = r * (dxh - xh * jnp.mean(dxh * xh, axis=-1, keepdims=True))
    return dx, dg


def _sigmoid(x):
    return 1.0 / (1.0 + jnp.exp(-x))


def _split_bf16(x):
    hi = x.astype(BF16)
    lo = (x - hi.astype(F32)).astype(BF16)
    return hi, lo


def _full(shape):
    return pl.BlockSpec(shape, lambda *_: (0,) * len(shape))


def _resident(shape):
    return pl.BlockSpec(shape, lambda *_: (0,) * len(shape), pipeline_mode=pl.Buffered(1))


def _row_tile(rows, cap):
    for t in range(min(cap, rows) // 8 * 8, 0, -8):
        if rows % t == 0:
            return t
    return rows


def _peers():
    x, y, c = lax.axis_index("x"), lax.axis_index("y"), lax.axis_index("c")
    out = []
    for k in range(1, N_DEV):
        px = 1 - x if (k >> 2) & 1 else x
        py = 1 - y if (k >> 1) & 1 else y
        pc = 1 - c if k & 1 else c
        out.append(((px, py, pc), 4 * px + 2 * py + pc))
    return 4 * x + 2 * y + c, out


_ROUTES = {
    "gather_mid": (lambda ref, idx: ref, lambda ref, idx: ref.at[:, idx], lambda s: s[:1] + (N_DEV,) + s[1:]),
    "to_all": (lambda ref, idx: ref, lambda ref, idx: ref.at[idx], lambda s: (N_DEV,) + s),
    "scatter_mid": (lambda ref, idx: ref.at[:, idx], lambda ref, idx: ref.at[idx],
                    lambda s: (N_DEV, s[0]) + s[2:]),
    "scatter_lead": (lambda ref, idx: ref.at[idx], lambda ref, idx: ref.at[idx], lambda s: s),
}


def _route_fns(items):
    kinds = [kind for _, kind in items]
    return (lambda j, ref, idx: _ROUTES[kinds[j]][0](ref, idx)), (lambda j, ref, idx: _ROUTES[kinds[j]][1](ref, idx))


def _route_out_shapes(items):
    return [jax.ShapeDtypeStruct(_ROUTES[kind][2](arr.shape), arr.dtype) for arr, kind in items]


def _exchange(items, name):
    n = len(items)
    fns = _route_fns(items)

    def body(*refs):
        ins, outs, sems = refs[:n], refs[n:2 * n], refs[2 * n:]
        _exchange_start(ins, outs, sems, *fns)
        _exchange_wait(ins, outs, sems, *fns)

    any_spec = pl.BlockSpec(memory_space=pl.ANY)
    return pl.pallas_call(
        body, name=name, out_shape=_route_out_shapes(items),
        in_specs=[any_spec] * n, out_specs=[any_spec] * n,
        scratch_shapes=_exchange_sems(n),
    )(*[arr for arr, _ in items])


def _exchange_sems(n):
    return [pltpu.SemaphoreType.DMA((n, N_DEV - 1)), pltpu.SemaphoreType.DMA((n, N_DEV - 1)),
            pltpu.SemaphoreType.DMA((n,))]


def _own_copies(ins, outs, sems, src_of, dst_of):
    me, _ = _peers()
    return [pltpu.make_async_copy(src_of(j, ins[j], me), dst_of(j, outs[j], me), sems[2].at[j])
            for j in range(len(ins))]


def _remote_copies(ins, outs, sems, src_of, dst_of, receiving):
    me, peers = _peers()
    return [pltpu.make_async_remote_copy(
        src_ref=src_of(j, ins[j], pidx), dst_ref=dst_of(j, outs[j], pidx if receiving else me),
        send_sem=sems[0].at[j, k], recv_sem=sems[1].at[j, k],
        device_id=peer, device_id_type=pl.DeviceIdType.MESH)
        for k, (peer, pidx) in enumerate(peers) for j in range(len(ins))]


def _exchange_start(ins, outs, sems, src_of, dst_of):
    for cp in _own_copies(ins, outs, sems, src_of, dst_of):
        cp.start()
    for cp in _remote_copies(ins, outs, sems, src_of, dst_of, receiving=False):
        cp.start()


def _exchange_wait(ins, outs, sems, src_of, dst_of):
    for cp in _remote_copies(ins, outs, sems, src_of, dst_of, receiving=True):
        cp.wait_recv()
    for cp in _remote_copies(ins, outs, sems, src_of, dst_of, receiving=False):
        cp.wait_send()
    for cp in _own_copies(ins, outs, sems, src_of, dst_of):
        cp.wait()


def _f_parts(F):
    first = -(-(F // V7X_MXU_DIM) // 2) * V7X_MXU_DIM
    return (slice(0, first), slice(first, F))


def _carried(items):
    if not items:
        return [], [], [], []
    return ([arr for arr, _ in items], [pl.BlockSpec(memory_space=pl.ANY)] * len(items), _route_out_shapes(items),
            _exchange_sems(len(items)))


def _ffn_fwd(h, g, w3, carry=()):
    T, D = h.shape
    F = w3.shape[1]
    tm = min(TOKEN_TILE, T)
    nt = T // tm
    nc = len(carry)
    has_c = nc > 0
    c_in, c_specs, c_out, c_sems = _carried(carry)

    def body(*refs):
        h_ref, g_ref, w_ref = refs[:3]
        ho_ref, gate_ref, up_ref = refs[3 + nc:6 + nc]
        i = pl.program_id(0)
        if has_c:
            comm = (refs[3:3 + nc], refs[6 + nc:6 + 2 * nc], refs[6 + 2 * nc:], *_route_fns(carry))

            @pl.when(i == 0)
            def _():
                _exchange_start(*comm)

        x = h_ref[...]
        u = _rms_fwd(x, g_ref[...]).astype(BF16)
        acc = None
        for cols in _f_parts(F):
            gate = _dot_nt(u, w_ref[0, cols, :])
            up = _dot_nt(u, w_ref[1, cols, :])
            gate_ref[:, cols] = gate.astype(BF16)
            up_ref[:, cols] = up.astype(BF16)
            part = _dot((gate * _sigmoid(gate) * up).astype(BF16), w_ref[2, cols, :])
            acc = part if acc is None else acc + part
        ho_ref[...] = x + 0.5 * acc

        if has_c:
            @pl.when(i == nt - 1)
            def _():
                _exchange_wait(*comm)

    row = pl.BlockSpec((tm, D), lambda i: (i, 0))
    wide = pl.BlockSpec((tm, F), lambda i: (i, 0))
    return pl.pallas_call(
        body, name="ffn_fwd_gather" if has_c else "ffn_fwd", grid=(nt,),
        in_specs=[row, _full((1, D)), _resident(w3.shape)] + c_specs,
        out_specs=[row, wide, wide] + c_specs,
        out_shape=[jax.ShapeDtypeStruct((T, D), F32), jax.ShapeDtypeStruct((T, F), BF16),
                   jax.ShapeDtypeStruct((T, F), BF16)] + c_out,
        scratch_shapes=c_sems,
        compiler_params=_cparams("arbitrary"),
    )(h, g, w3, *c_in)


def _ffn_bwd_dx(dho, h, g, gate, up, w3, carry=()):
    T, D = h.shape
    F = w3.shape[1]
    tm = min(TOKEN_TILE // 2, T)
    nt = T // tm
    nc = len(carry)
    has_c = nc > 0
    c_in, c_specs, c_out, c_sems = _carried(carry)

    def body(*refs):
        dho_ref, h_ref, g_ref, gate_ref, up_ref, w_ref = refs[:6]
        dhi_ref, dgate_ref, dup_ref, u_ref, dy_ref, dg_ref = refs[6 + nc:12 + nc]
        i = pl.program_id(0)
        if has_c:
            comm = (refs[6:6 + nc], refs[12 + nc:12 + 2 * nc], refs[12 + 2 * nc:], *_route_fns(carry))

            @pl.when(i == 0)
            def _():
                _exchange_start(*comm)

        dho = dho_ref[...]
        x = h_ref[...]
        dy_f = 0.5 * dho
        dy = dy_f.astype(BF16)
        dy_ref[...] = dy_f.T.astype(BF16)
        u_ref[...] = _rms_fwd(x, g_ref[...]).T.astype(BF16)
        acc = None
        for cols in _f_parts(F):
            dact = _dot_nt(dy, w_ref[2, cols, :])
            gt = gate_ref[:, cols].astype(F32)
            sig = _sigmoid(gt)
            dup = (dact * (gt * sig)).astype(BF16)
            dgate = (dact * up_ref[:, cols].astype(F32) * (sig * (1.0 + gt * (1.0 - sig)))).astype(BF16)
            dup_ref[:, cols] = dup
            dgate_ref[:, cols] = dgate
            part = _dot(dgate, w_ref[0, cols, :]) + _dot(dup, w_ref[1, cols, :])
            acc = part if acc is None else acc + part
        dx, dg = _rms_bwd(acc, x, g_ref[...])
        dhi_ref[...] = dho + dx

        @pl.when(i == 0)
        def _():
            dg_ref[...] = jnp.zeros_like(dg_ref)

        dg_ref[...] += dg

        if has_c:
            @pl.when(i == nt - 1)
            def _():
                _exchange_wait(*comm)

    row = pl.BlockSpec((tm, D), lambda i: (i, 0))
    col = pl.BlockSpec((D, tm), lambda i: (0, i))
    wide = pl.BlockSpec((tm, F), lambda i: (i, 0))
    return pl.pallas_call(
        body, name="ffn_bwd_dx_scatter" if has_c else "ffn_bwd_dx", grid=(nt,),
        in_specs=[row, row, _full((1, D)), wide, wide, _resident(w3.shape)] + c_specs,
        out_specs=[row, wide, wide, col, col, _full((1, D))] + c_specs,
        out_shape=[jax.ShapeDtypeStruct((T, D), F32), jax.ShapeDtypeStruct((T, F), BF16),
                   jax.ShapeDtypeStruct((T, F), BF16), jax.ShapeDtypeStruct((D, T), BF16),
                   jax.ShapeDtypeStruct((D, T), BF16), jax.ShapeDtypeStruct((1, D), F32)] + c_out,
        scratch_shapes=c_sems,
        compiler_params=_cparams("arbitrary"),
    )(dho, h, g, gate, up, w3, *c_in)


def _ffn_bwd_dw(dgate, dup, gate, up, u_t, dy_t):
    T, F = gate.shape
    D = u_t.shape[0]
    tfw = F // 2
    tk = min(TOKEN_TILE, T)
    nk = T // tk

    def body(dgate_ref, dup_ref, gate_ref, up_ref, ut_ref, dyt_ref, out_ref, acc_sc):
        k = pl.program_id(1)

        @pl.when(k == 0)
        def _():
            acc_sc[...] = jnp.zeros_like(acc_sc)

        uu = ut_ref[...]
        acc_sc[0] += _dot(uu, dgate_ref[...])
        acc_sc[1] += _dot(uu, dup_ref[...])
        gt = gate_ref[...].astype(F32)
        act = (gt * _sigmoid(gt) * up_ref[...].astype(F32)).astype(BF16)
        acc_sc[2] += _dot(dyt_ref[...], act)

        @pl.when(k == nk - 1)
        def _():
            for kind in range(3):
                out_ref[kind] = acc_sc[kind].T.astype(BF16)

    blk = pl.BlockSpec((tk, tfw), lambda j, k: (k, j))
    col = pl.BlockSpec((D, tk), lambda j, k: (0, k))
    return pl.pallas_call(
        body, name="ffn_bwd_dw", grid=(F // tfw, nk),
        in_specs=[blk, blk, blk, blk, col, col],
        out_specs=pl.BlockSpec((3, tfw, D), lambda j, k: (0, j, 0)),
        out_shape=jax.ShapeDtypeStruct((3, F, D), BF16),
        scratch_shapes=[pltpu.VMEM((3, D, tfw), F32)],
        compiler_params=_cparams("parallel", "arbitrary"),
    )(dgate, dup, gate, up, u_t, dy_t)


def _mm_rows(a, b, *, nt, out_dtype, norm_g=None, res=None, name):
    T, K = a.shape
    N = b.shape[0] if nt else b.shape[1]
    tm = min(TOKEN_TILE, T)
    has_g, has_r = norm_g is not None, res is not None

    def body(*refs):
        a_ref, b_ref = refs[0], refs[1]
        o_ref = refs[-1]
        x = a_ref[...]
        if has_g:
            x = _rms_fwd(x, refs[2][...])
        x = x.astype(BF16)
        acc = _dot_nt(x, b_ref[...]) if nt else _dot(x, b_ref[...])
        if has_r:
            acc = refs[2 + has_g][...] + acc
        o_ref[...] = acc.astype(out_dtype)

    ins, specs = [a, b], [pl.BlockSpec((tm, K), lambda i: (i, 0)), _full(b.shape)]
    if has_g:
        ins.append(norm_g)
        specs.append(_full((1, K)))
    if has_r:
        ins.append(res)
        specs.append(pl.BlockSpec((tm, N), lambda i: (i, 0)))
    return pl.pallas_call(
        body, name=name, grid=(T // tm,), in_specs=specs,
        out_specs=pl.BlockSpec((tm, N), lambda i: (i, 0)),
        out_shape=jax.ShapeDtypeStruct((T, N), out_dtype),
        compiler_params=_cparams("parallel"),
    )(*ins)


def _mm_tn(a, b, *, norm_g=None, name):
    T, M = a.shape
    N = b.shape[1]
    tk = min(TOKEN_TILE, T)
    has_g = norm_g is not None

    def body(*refs):
        a_ref, b_ref, o_ref = refs[0], refs[1], refs[-1]

        @pl.when(pl.program_id(0) == 0)
        def _():
            o_ref[...] = jnp.zeros_like(o_ref)

        x = a_ref[...]
        if has_g:
            x = _rms_fwd(x, refs[2][...])
        o_ref[...] += _dot_tn(x.astype(BF16), b_ref[...].astype(BF16))

    ins = [a, b]
    specs = [pl.BlockSpec((tk, M), lambda k: (k, 0)), pl.BlockSpec((tk, N), lambda k: (k, 0))]
    if has_g:
        ins.append(norm_g)
        specs.append(_full((1, M)))
    return pl.pallas_call(
        body, name=name, grid=(T // tk,), in_specs=specs, out_specs=_full((M, N)),
        out_shape=jax.ShapeDtypeStruct((M, N), F32),
        compiler_params=_cparams("arbitrary"),
    )(*ins)


def _proj_bwd(dz, w, h, g, dh, name):
    T, D = h.shape
    N = w.shape[1]
    tm = min(TOKEN_TILE, T)

    def body(dz_ref, w_ref, h_ref, g_ref, dh_ref, o_ref, dg_ref):
        du = _dot_nt(dz_ref[...].astype(BF16), w_ref[...])
        dx, dg = _rms_bwd(du, h_ref[...], g_ref[...])
        o_ref[...] = dh_ref[...] + dx

        @pl.when(pl.program_id(0) == 0)
        def _():
            dg_ref[...] = jnp.zeros_like(dg_ref)

        dg_ref[...] += dg

    row = pl.BlockSpec((tm, D), lambda i: (i, 0))
    return pl.pallas_call(
        body, name=name, grid=(T // tm,),
        in_specs=[pl.BlockSpec((tm, N), lambda i: (i, 0)), _full((D, N)), row, _full((1, D)), row],
        out_specs=[row, _full((1, D))],
        out_shape=[jax.ShapeDtypeStruct((T, D), F32), jax.ShapeDtypeStruct((1, D), F32)],
        compiler_params=_cparams("arbitrary"),
    )(dz, w, h, g, dh)


def _pool_bands(tm):
    r = np.arange(tm)[:, None]
    c = np.arange(tm)[None, :]
    j = np.arange(POOL_HALO)[None, :]
    main, halo, main_t, halo_t = [], [], [], []
    for w in POOL_WINDOWS:
        main.append(((r - c >= 0) & (r - c < w)) / w)
        halo.append((r + POOL_HALO - j < w) / w)
        main_t.append(((c - r >= 0) & (c - r < w)) / w)
        halo_t.append((tm + j - r < w) / w)
    return tuple(jnp.asarray(np.stack(m), BF16) for m in (main, halo, main_t, halo_t))


def _pool_count_scale(i, tm, w):
    t = i * tm + lax.broadcasted_iota(jnp.int32, (tm, 1), 0)
    return w / jnp.minimum(t + 1, w).astype(F32)


def _pool_fwd(h, g, wp, scale):
    T, D = h.shape
    G, dg = len(POOL_WINDOWS), D // len(POOL_WINDOWS)
    tm = min(TOKEN_TILE, T)
    hb = tm // POOL_HALO
    bm, bh, _, _ = _pool_bands(tm)

    def body(h_ref, hh_ref, g_ref, wp_ref, sc_ref, bm_ref, bh_ref, ho_ref, y_ref):
        i = pl.program_id(0)
        x = h_ref[...]
        u = _rms_fwd(x, g_ref[...])
        uh = _rms_fwd(hh_ref[...], g_ref[...]) * (i > 0).astype(F32)
        for gi, w in enumerate(POOL_WINDOWS):
            cols = slice(gi * dg, (gi + 1) * dg)
            ug = u[:, cols]
            hi, lo = _split_bf16(ug)
            hhi, hlo = _split_bf16(uh[:, cols])
            s = (_dot(bm_ref[gi], hi) + _dot(bm_ref[gi], lo)
                 + _dot(bh_ref[gi], hhi) + _dot(bh_ref[gi], hlo))
            y = (s * _pool_count_scale(i, tm, w) - ug).astype(BF16)
            y_ref[:, cols] = y
            ho_ref[:, cols] = x[:, cols] + _dot(y, wp_ref[gi]) * sc_ref[:, cols]

    row = pl.BlockSpec((tm, D), lambda i: (i, 0))
    return pl.pallas_call(
        body, name="pool_fwd", grid=(T // tm,),
        in_specs=[row, pl.BlockSpec((POOL_HALO, D), lambda i: (jnp.maximum(i * hb - 1, 0), 0)),
                  _full((1, D)), _full((G, dg, dg)), _full((1, D)),
                  _full((G, tm, tm)), _full((G, tm, POOL_HALO))],
        out_specs=[row, row],
        out_shape=[jax.ShapeDtypeStruct((T, D), F32), jax.ShapeDtypeStruct((T, D), BF16)],
        compiler_params=_cparams("parallel"),
    )(h, h, g, wp, scale, bm, bh)


def _pool_bwd(dh, h, g, y, wp, scale):
    T, D = h.shape
    G, dg = len(POOL_WINDOWS), D // len(POOL_WINDOWS)
    tm = min(TOKEN_TILE, T)
    hb = tm // POOL_HALO
    nt = T // tm
    _, _, bmt, bht = _pool_bands(tm)

    def body(dh_ref, dhn_ref, h_ref, g_ref, y_ref, wp_ref, sc_ref, bmt_ref, bht_ref,
             o_ref, dg_ref, dwp_ref, dsc_ref, du_sc):
        i = pl.program_id(0)

        @pl.when(i == 0)
        def _():
            dg_ref[...] = jnp.zeros_like(dg_ref)
            dwp_ref[...] = jnp.zeros_like(dwp_ref)
            dsc_ref[...] = jnp.zeros_like(dsc_ref)

        dho = dh_ref[...]
        dz = dho * sc_ref[...]
        dzn = dhn_ref[...] * sc_ref[...] * (i < nt - 1).astype(F32)
        for gi, w in enumerate(POOL_WINDOWS):
            cols = slice(gi * dg, (gi + 1) * dg)
            yg = y_ref[:, cols]
            dzg = dz[:, cols].astype(BF16)
            dsc_ref[:, cols] += jnp.sum(dho[:, cols] * _dot(yg, wp_ref[gi]), axis=0, keepdims=True)
            dwp_ref[gi] += _dot_tn(yg, dzg)
            dy = _dot_nt(dzg, wp_ref[gi])
            dyn = _dot_nt(dzn[:, cols].astype(BF16), wp_ref[gi])
            hi, lo = _split_bf16(dy * _pool_count_scale(i, tm, w))
            nhi, nlo = _split_bf16(dyn)
            du_sc[:, cols] = (_dot(bmt_ref[gi], hi) + _dot(bmt_ref[gi], lo)
                              + _dot(bht_ref[gi], nhi) + _dot(bht_ref[gi], nlo) - dy)
        dx, dgp = _rms_bwd(du_sc[...], h_ref[...], g_ref[...])
        o_ref[...] = dho + dx
        dg_ref[...] += dgp

    row = pl.BlockSpec((tm, D), lambda i: (i, 0))
    return pl.pallas_call(
        body, name="pool_bwd", grid=(nt,),
        in_specs=[row, pl.BlockSpec((POOL_HALO, D), lambda i: (jnp.minimum((i + 1) * hb, T // POOL_HALO - 1), 0)),
                  row, _full((1, D)), row, _full((G, dg, dg)), _full((1, D)),
                  _full((G, tm, tm)), _full((G, tm, POOL_HALO))],
        out_specs=[row, _full((1, D)), _full((G, dg, dg)), _full((1, D))],
        out_shape=[jax.ShapeDtypeStruct((T, D), F32), jax.ShapeDtypeStruct((1, D), F32),
                   jax.ShapeDtypeStruct((G, dg, dg), F32), jax.ShapeDtypeStruct((1, D), F32)],
        scratch_shapes=[pltpu.VMEM((tm, D), F32)],
        compiler_params=_cparams("arbitrary"),
    )(dh, dh, h, g, y, wp, scale, bmt, bht)


def _rope_tables(T):
    pos = jnp.arange(T, dtype=F32)
    inv_freq = ROPE_THETA ** (-jnp.arange(0, D_ROPE, 2, dtype=F32) / D_ROPE)
    ang = pos[:, None] * inv_freq[None, :]
    cos2 = jnp.tile(jnp.cos(ang), (1, 2))
    sin2 = jnp.tile(jnp.sin(ang), (1, 2))
    pad = jnp.zeros((T, D_HEAD_PAD - D_QK), F32)
    ca_q = jnp.concatenate([jnp.ones((T, D_NOPE), F32), cos2, pad], axis=1)
    ca_k = jnp.concatenate([jnp.zeros((T, D_NOPE), F32), cos2, pad], axis=1)
    sb = jnp.concatenate([jnp.zeros((T, D_NOPE), F32), sin2, pad], axis=1)
    return ca_q, ca_k, sb


def _rope_weight_pair(w):
    half = D_ROPE // 2
    z_pad = jnp.zeros(w.shape[:-1] + (D_HEAD_PAD - D_QK,), w.dtype)
    z_nope = jnp.zeros(w.shape[:-1] + (D_NOPE,), w.dtype)
    wa = jnp.concatenate([w, z_pad], axis=-1)
    wb = jnp.concatenate([z_nope, -w[..., D_NOPE + half:], w[..., D_NOPE:D_NOPE + half], z_pad], axis=-1)
    return wa, wb


def _rope_weight_pair_grad(dwa, dwb):
    half = D_ROPE // 2
    d1 = dwa[..., D_NOPE:D_NOPE + half] + dwb[..., D_NOPE + half:D_QK]
    d2 = dwa[..., D_NOPE + half:D_QK] - dwb[..., D_NOPE:D_NOPE + half]
    return jnp.concatenate([dwa[..., :D_NOPE], d1, d2], axis=-1)


def _q_proj(cq, qg, wa, wb, ca, sb):
    T, R = cq.shape
    tm = min(TOKEN_TILE, T)
    P = D_HEAD_PAD
    GP = HEAD_GROUP * P

    def body(cq_ref, qg_ref, wa_ref, wb_ref, ca_ref, sb_ref, q_ref):
        c = _rms_fwd(cq_ref[...], qg_ref[...]).astype(BF16)
        ca = jnp.tile(ca_ref[...], (1, HEAD_GROUP))
        sb = jnp.tile(sb_ref[...], (1, HEAD_GROUP))
        q_ref[...] = (_dot(c, wa_ref[...]) * ca + _dot(c, wb_ref[...]) * sb).astype(BF16)

    tok = pl.BlockSpec((tm, P), lambda i, hh: (i, 0))
    wsp = pl.BlockSpec((R, GP), lambda i, hh: (0, hh))
    return pl.pallas_call(
        body, name="q_proj", grid=(T // tm, N_HEADS // HEAD_GROUP),
        in_specs=[pl.BlockSpec((tm, R), lambda i, hh: (i, 0)), _full((1, R)), wsp, wsp, tok, tok],
        out_specs=pl.BlockSpec((tm, GP), lambda i, hh: (i, hh)),
        out_shape=jax.ShapeDtypeStruct((T, N_HEADS * P), BF16),
        compiler_params=_cparams("parallel", "arbitrary"),
    )(cq, qg, wa, wb, ca, sb)


def _q_proj_bwd(dq, cq, qg, wa, wb, ca, sb):
    T, R = cq.shape
    tm = min(TOKEN_TILE, T)
    P = D_HEAD_PAD
    GP = HEAD_GROUP * P

    def body(dq_ref, cq_ref, qg_ref, wa_ref, wb_ref, ca_ref, sb_ref, dcq_ref, dqg_ref, dwa_ref, dwb_ref):
        @pl.when(pl.program_id(0) == 0)
        def _():
            for r in (dqg_ref, dwa_ref, dwb_ref):
                r[...] = jnp.zeros_like(r)

        cq_f = cq_ref[...]
        cqn = _rms_fwd(cq_f, qg_ref[...]).astype(BF16)
        ca = jnp.tile(ca_ref[...], (1, HEAD_GROUP))
        sb = jnp.tile(sb_ref[...], (1, HEAD_GROUP))
        acc = None
        for grp in range(N_HEADS // HEAD_GROUP):
            cols = slice(grp * GP, (grp + 1) * GP)
            d = dq_ref[cols, :].T
            da = (d * ca).astype(BF16)
            db = (d * sb).astype(BF16)
            part = _dot_nt(da, wa_ref[:, cols]) + _dot_nt(db, wb_ref[:, cols])
            acc = part if acc is None else acc + part
            dwa_ref[:, cols] += _dot_tn(cqn, da)
            dwb_ref[:, cols] += _dot_tn(cqn, db)
        dx, dg = _rms_bwd(acc, cq_f, qg_ref[...])
        dcq_ref[...] = dx
        dqg_ref[...] += dg

    tok = pl.BlockSpec((tm, P), lambda i: (i, 0))
    rr = pl.BlockSpec((tm, R), lambda i: (i, 0))
    wfull = _full((R, N_HEADS * P))
    return pl.pallas_call(
        body, name="q_proj_bwd", grid=(T // tm,),
        in_specs=[pl.BlockSpec((N_HEADS * P, tm), lambda i: (0, i)), rr, _full((1, R)), wfull, wfull, tok, tok],
        out_specs=[rr, _full((1, R)), wfull, wfull],
        out_shape=[jax.ShapeDtypeStruct((T, R), F32), jax.ShapeDtypeStruct((1, R), F32),
                   jax.ShapeDtypeStruct((R, N_HEADS * P), F32), jax.ShapeDtypeStruct((R, N_HEADS * P), F32)],
        compiler_params=_cparams("arbitrary"),
    )(dq, cq, qg, wa, wb, ca, sb)


def _kv_proj(h, g_in, wka, wkb, g_c, wuk, wuv, ca, sb):
    T, D = h.shape
    tm = min(TOKEN_TILE, T)
    P, C = D_HEAD_PAD, D_NOPE

    def body(h_ref, gi_ref, wka_ref, wkb_ref, gc_ref, wuk_ref, wuv_ref, ca_ref, sb_ref, k_ref, v_ref, craw_ref):
        u = _rms_fwd(h_ref[...], gi_ref[...]).astype(BF16)
        kva = _dot(u, wka_ref[...])
        kvb = _dot(u, wkb_ref[...])
        craw = kva[:, :C]
        craw_ref[...] = craw
        c = _rms_fwd(craw, gc_ref[...]).astype(BF16)
        kr = kva * ca_ref[...] + kvb * sb_ref[...]
        kn = _dot(c, wuk_ref[...])
        for hh in range(N_HEADS):
            k_ref[:, hh * P:(hh + 1) * P] = (kn[:, hh * P:(hh + 1) * P] + kr).astype(BF16)
        v_ref[...] = _dot(c, wuv_ref[...]).astype(BF16)

    tok = pl.BlockSpec((tm, P), lambda i: (i, 0))
    return pl.pallas_call(
        body, name="kv_proj", grid=(T // tm,),
        in_specs=[pl.BlockSpec((tm, D), lambda i: (i, 0)), _full((1, D)), _full((D, P)), _full((D, P)),
                  _full((1, C)), _full(wuk.shape), _full(wuv.shape), tok, tok],
        out_specs=[pl.BlockSpec((tm, N_HEADS * P), lambda i: (i, 0)),
                   pl.BlockSpec((tm, N_HEADS * D_V), lambda i: (i, 0)), pl.BlockSpec((tm, C), lambda i: (i, 0))],
        out_shape=[jax.ShapeDtypeStruct((T, N_HEADS * P), BF16), jax.ShapeDtypeStruct((T, N_HEADS * D_V), BF16),
                   jax.ShapeDtypeStruct((T, C), F32)],
        compiler_params=_cparams("parallel"),
    )(h, g_in, wka, wkb, g_c, wuk, wuv, ca, sb)


def _kv_proj_bwd(dks, dvs, dh, h, g_in, wka, wkb, craw, g_c, wuk, wuv, ca, sb):
    T, D = h.shape
    tm = min(TOKEN_TILE // 2, T)
    P, C = D_HEAD_PAD, D_NOPE
    nl = len(dks)

    def body(*refs):
        dk_refs, dv_refs = refs[:nl], refs[nl:2 * nl]
        (dh_ref, h_ref, gi_ref, wka_ref, wkb_ref, craw_ref, gc_ref, wuk_ref, wuv_ref,
         ca_ref, sb_ref, o_ref, dgi_ref, dwka_ref, dwkb_ref, dgc_ref, dwuk_ref, dwuv_ref) = refs[2 * nl:]

        @pl.when(pl.program_id(0) == 0)
        def _():
            for r in (dgi_ref, dwka_ref, dwkb_ref, dgc_ref, dwuk_ref, dwuv_ref):
                r[...] = jnp.zeros_like(r)

        x = h_ref[...]
        u = _rms_fwd(x, gi_ref[...]).astype(BF16)
        craw = craw_ref[...]
        c = _rms_fwd(craw, gc_ref[...]).astype(BF16)
        dkf = sum(r[...] for r in dk_refs[1:]) + dk_refs[0][...]
        dkb = dkf.astype(BF16)
        dvb = (sum(r[...] for r in dv_refs[1:]) + dv_refs[0][...]).astype(BF16)
        dwuk_ref[...] += _dot_tn(c, dkb)
        dwuv_ref[...] += _dot_tn(c, dvb)
        dc = _dot_nt(dkb, wuk_ref[...]) + _dot_nt(dvb, wuv_ref[...])
        dkr = dkf[:, :P]
        for hh in range(1, N_HEADS):
            dkr = dkr + dkf[:, hh * P:(hh + 1) * P]
        dcraw, dgc = _rms_bwd(dc, craw, gc_ref[...])
        dgc_ref[...] += dgc
        dkva = jnp.concatenate([dcraw, (dkr * ca_ref[...])[:, C:]], axis=1).astype(BF16)
        dkvb = (dkr * sb_ref[...]).astype(BF16)
        dwka_ref[...] += _dot_tn(u, dkva)
        dwkb_ref[...] += _dot_tn(u, dkvb)
        du = _dot_nt(dkva, wka_ref[...]) + _dot_nt(dkvb, wkb_ref[...])
        dx, dgi = _rms_bwd(du, x, gi_ref[...])
        dgi_ref[...] += dgi
        o_ref[...] = dh_ref[...] + dx

    row = pl.BlockSpec((tm, D), lambda i: (i, 0))
    tok = pl.BlockSpec((tm, P), lambda i: (i, 0))
    return pl.pallas_call(
        body, name="kv_proj_bwd", grid=(T // tm,),
        in_specs=[pl.BlockSpec((tm, N_HEADS * P), lambda i: (i, 0))] * nl
        + [pl.BlockSpec((tm, N_HEADS * D_V), lambda i: (i, 0))] * nl
        + [row, row, _full((1, D)), _full((D, P)), _full((D, P)), pl.BlockSpec((tm, C), lambda i: (i, 0)),
           _full((1, C)), _full(wuk.shape), _full(wuv.shape), tok, tok],
        out_specs=[row, _full((1, D)), _full((D, P)), _full((D, P)), _full((1, C)),
                   _full(wuk.shape), _full(wuv.shape)],
        out_shape=[jax.ShapeDtypeStruct((T, D), F32), jax.ShapeDtypeStruct((1, D), F32),
                   jax.ShapeDtypeStruct((D, P), F32), jax.ShapeDtypeStruct((D, P), F32),
                   jax.ShapeDtypeStruct((1, C), F32), jax.ShapeDtypeStruct(wuk.shape, F32),
                   jax.ShapeDtypeStruct(wuv.shape, F32)],
        compiler_params=_cparams("arbitrary"),
    )(*dks, *dvs, dh, h, g_in, wka, wkb, craw, g_c, wuk, wuv, ca, sb)


_ATTN_SCALE = D_QK ** -0.5
_LOG2_E = 1.4426950408889634
_LN_2 = 0.6931471805599453
_ATTN_SCALE_LOG2 = _ATTN_SCALE * _LOG2_E


def _flash_fwd(q, k, v):
    T = q.shape[0]
    t = min(ATTN_TILE, T // 2)
    tq = 2 * t
    P = D_HEAD_PAD

    def body(q_ref, k_ref, v_ref, o_ref, lse_ref, m_sc, l_sc, acc_sc, s0_sc, s1_sc, p0_sc, p1_sc, a0_sc, a1_sc):
        qi = pl.program_id(1)
        n = 2 * (qi + 1)
        s_sc, p_sc, a_sc = (s0_sc, s1_sc), (p0_sc, p1_sc), (a0_sc, a1_sc)
        m_sc[...] = jnp.full_like(m_sc, NEG_BIG)
        l_sc[...] = jnp.zeros_like(l_sc)
        acc_sc[...] = jnp.zeros_like(acc_sc)

        def rows_of(c):
            return pl.ds(pl.multiple_of(c * t, t), t)

        def scores(c, slot):
            s_sc[slot][...] = _dot_nt(k_ref[rows_of(c), :], q_ref[...])

        def softmax(slot, key_offset):
            s_t = s_sc[slot][...]
            if key_offset is not None:
                rows = lax.broadcasted_iota(jnp.int32, (t, tq), 0) + key_offset
                s_t = jnp.where(rows <= lax.broadcasted_iota(jnp.int32, (t, tq), 1), s_t, NEG_BIG)
            m_prev = m_sc[...]
            m_new = jnp.maximum(m_prev, jnp.max(s_t, axis=0, keepdims=True))
            p_t = jnp.exp2(s_t - m_new)
            alpha = jnp.exp2(m_prev - m_new)
            l_sc[...] = alpha * l_sc[...] + jnp.sum(p_t, axis=0, keepdims=True)
            m_sc[...] = m_new
            p_sc[slot][...] = p_t.astype(BF16)
            a_sc[slot][...] = alpha

        def values(c, slot):
            acc_sc[...] = a_sc[slot][...] * acc_sc[...] + _dot_tn(v_ref[rows_of(c), :], p_sc[slot][...])

        def stage(c, slot, first=False, last=False, key_offset=None):
            if not first:
                values(c - 1, 1 - slot)
            if not last:
                scores(c + 1, 1 - slot)
            softmax(slot, key_offset)

        def drain():
            stage(n - 2, 0, key_offset=0)
            stage(n - 1, 1, last=True, key_offset=t)
            values(n - 1, 1)

        scores(0, 0)

        @pl.when(qi == 0)
        def _():
            stage(0, 0, first=True, key_offset=0)
            stage(1, 1, last=True, key_offset=t)
            values(1, 1)

        @pl.when(qi > 0)
        def _():
            stage(0, 0, first=True)

            def pair(j, carry):
                stage(1 + 2 * j, 1)
                stage(2 + 2 * j, 0)
                return carry

            lax.fori_loop(0, qi - 1, pair, 0)
            stage(n - 3, 1)
            drain()

        l = l_sc[...]
        o_ref[...] = (acc_sc[...] / l).T.astype(BF16)
        lse_ref[...] = m_sc[...] + jnp.log(l) * _LOG2_E

    return pl.pallas_call(
        body, name="flash_fwd", grid=(N_HEADS, T // tq),
        in_specs=[pl.BlockSpec((tq, P), lambda hh, i: (i, hh)), pl.BlockSpec((T, P), lambda hh, i: (0, hh)),
                  pl.BlockSpec((T, D_V), lambda hh, i: (0, hh))],
        out_specs=[pl.BlockSpec((tq, D_V), lambda hh, i: (i, hh)),
                   pl.BlockSpec((None, None, 1, tq), lambda hh, i: (hh, i, 0, 0))],
        out_shape=[jax.ShapeDtypeStruct((T, N_HEADS * D_V), BF16),
                   jax.ShapeDtypeStruct((N_HEADS, T // tq, 1, tq), F32)],
        scratch_shapes=[pltpu.VMEM((1, tq), F32), pltpu.VMEM((1, tq), F32), pltpu.VMEM((D_V, tq), F32),
                        pltpu.VMEM((t, tq), F32), pltpu.VMEM((t, tq), F32), pltpu.VMEM((t, tq), BF16),
                        pltpu.VMEM((t, tq), BF16), pltpu.VMEM((1, tq), F32), pltpu.VMEM((1, tq), F32)],
        compiler_params=_cparams("parallel", "arbitrary"),
    )(q, k, v)


def _attn_delta(o, do):
    T = o.shape[0]
    t = min(ATTN_TILE, T)

    def body(o_ref, do_ref, out_ref):
        ones = jnp.ones((8, D_V), BF16)
        for hh in range(N_HEADS):
            cols = slice(hh * D_V, (hh + 1) * D_V)
            hi, lo = _split_bf16(o_ref[:, cols].astype(F32) * do_ref[:, cols].astype(F32))
            out_ref[hh] = (_dot_nt(ones, hi) + _dot_nt(ones, lo))[0:1]

    tok = pl.BlockSpec((t, N_HEADS * D_V), lambda i: (i, 0))
    return pl.pallas_call(
        body, name="attn_delta", grid=(T // t,), in_specs=[tok, tok],
        out_specs=pl.BlockSpec((N_HEADS, None, 1, t), lambda i: (0, i, 0, 0)),
        out_shape=jax.ShapeDtypeStruct((N_HEADS, T // t, 1, t), F32),
        compiler_params=_cparams("parallel"),
    )(o, do)


def _flash_bwd(q, k, v, do, lse_row, delta_row):
    T = q.shape[0]
    t = min(ATTN_TILE, T // 2)
    tk = 2 * t
    nq = T // t
    P = D_HEAD_PAD

    def body(k_ref, v_ref, q_ref, do_ref, lse_ref, delta_ref, dqt_ref, dk_ref, dv_ref, dqt_sc, kt_sc,
             s0_sc, s1_sc, dp0_sc, dp1_sc, p0_sc, p1_sc, ds0_sc, ds1_sc):
        ki = pl.program_id(1)
        q0 = 2 * ki
        n = nq - q0
        s_sc, dp_sc, p_sc, ds_sc = (s0_sc, s1_sc), (dp0_sc, dp1_sc), (p0_sc, p1_sc), (ds0_sc, ds1_sc)
        kt_sc[...] = k_ref[...].astype(F32).T.astype(BF16)
        dk_ref[...] = jnp.zeros_like(dk_ref)
        dv_ref[...] = jnp.zeros_like(dv_ref)

        @pl.when(ki == 0)
        def _():
            dqt_sc[...] = jnp.zeros_like(dqt_sc)

        def rows_of(c):
            return pl.ds(pl.multiple_of((q0 + c) * t, t), t)

        def products(c, slot):
            s_sc[slot][...] = _dot_nt(k_ref[...], q_ref[rows_of(c), :])
            dp_sc[slot][...] = _dot_nt(v_ref[...], do_ref[rows_of(c), :])

        def elementwise(c, slot, query_offset):
            p_t = jnp.exp2(s_sc[slot][...] - lse_ref[q0 + c])
            if query_offset is not None:
                cols = lax.broadcasted_iota(jnp.int32, (tk, t), 1) + query_offset
                p_t = jnp.where(lax.broadcasted_iota(jnp.int32, (tk, t), 0) <= cols, p_t, 0.0)
            p_sc[slot][...] = p_t.astype(BF16)
            ds_sc[slot][...] = (p_t * (dp_sc[slot][...] - delta_ref[q0 + c])).astype(BF16)

        def gradients(c, slot):
            dv_ref[...] += _dot(p_sc[slot][...], do_ref[rows_of(c), :])
            ds_t = ds_sc[slot][...]
            dk_ref[...] += _dot(ds_t, q_ref[rows_of(c), :])
            dqt_sc[q0 + c] += _dot(kt_sc[...], ds_t)

        def stage(c, slot, first=False, last=False, query_offset=None):
            if not first:
                gradients(c - 1, 1 - slot)
            if not last:
                products(c + 1, 1 - slot)
            elementwise(c, slot, query_offset)

        products(0, 0)

        @pl.when(n == 2)
        def _():
            stage(0, 0, first=True, query_offset=0)
            stage(1, 1, last=True, query_offset=t)
            gradients(1, 1)

        @pl.when(n > 2)
        def _():
            stage(0, 0, first=True, query_offset=0)
            stage(1, 1, query_offset=t)

            def pair(j, carry):
                stage(2 + 2 * j, 0)
                stage(3 + 2 * j, 1)
                return carry

            lax.fori_loop(0, (n - 4) // 2, pair, 0)
            stage(n - 2, 0)
            stage(n - 1, 1, last=True)
            gradients(n - 1, 1)

        dqt_ref[:, :t] = dqt_sc[q0] * _ATTN_SCALE
        dqt_ref[:, t:] = dqt_sc[q0 + 1] * _ATTN_SCALE
        dk_ref[...] = dk_ref[...] * _LN_2

    kb = pl.BlockSpec((tk, P), lambda hh, i: (i, hh))
    vb = pl.BlockSpec((tk, D_V), lambda hh, i: (i, hh))
    stat = pl.BlockSpec((None, nq, 1, t), lambda hh, i: (hh, 0, 0, 0))
    return pl.pallas_call(
        body, name="flash_bwd", grid=(N_HEADS, T // tk),
        in_specs=[kb, vb,
                  pl.BlockSpec((T, P), lambda hh, i: (0, hh), pipeline_mode=pl.Buffered(1)),
                  pl.BlockSpec((T, D_V), lambda hh, i: (0, hh), pipeline_mode=pl.Buffered(1)),
                  stat, stat],
        out_specs=[pl.BlockSpec((P, tk), lambda hh, i: (hh, i)), kb, vb],
        out_shape=[jax.ShapeDtypeStruct((N_HEADS * P, T), F32), jax.ShapeDtypeStruct((T, N_HEADS * P), F32),
                   jax.ShapeDtypeStruct((T, N_HEADS * D_V), F32)],
        scratch_shapes=[pltpu.VMEM((nq, P, t), F32), pltpu.VMEM((P, tk), BF16)]
        + [pltpu.VMEM((tk, t), F32)] * 4 + [pltpu.VMEM((tk, t), BF16)] * 4,
        compiler_params=_cparams("arbitrary", "arbitrary"),
    )(k, v, q, do, lse_row, delta_row)


def _loss_head(h, g, target):
    T, D = h.shape
    tm = min(TOKEN_TILE, T)

    def body(h_ref, g_ref, t_ref, dh_ref, loss_ref, dg_ref):
        @pl.when(pl.program_id(0) == 0)
        def _():
            loss_ref[...] = jnp.zeros_like(loss_ref)
            dg_ref[...] = jnp.zeros_like(dg_ref)

        x = h_ref[...]
        err = _rms_fwd(x, g_ref[...]) - t_ref[...]
        per_tok = jnp.mean(err * err, axis=-1, keepdims=True)
        loss_ref[...] += 0.5 * jnp.sum(per_tok, axis=0, keepdims=True)
        dx, dg = _rms_bwd(err * (1.0 / D), x, g_ref[...])
        dh_ref[...] = dx
        dg_ref[...] += dg

    row = pl.BlockSpec((tm, D), lambda i: (i, 0))
    return pl.pallas_call(
        body, name="loss_head", grid=(T // tm,),
        in_specs=[row, _full((1, D)), row], out_specs=[row, _full((1, 128)), _full((1, D))],
        out_shape=[jax.ShapeDtypeStruct((T, D), F32), jax.ShapeDtypeStruct((1, 128), F32),
                   jax.ShapeDtypeStruct((1, D), F32)],
        compiler_params=_cparams("arbitrary"),
    )(h, g, target)


def _sum_parts(parts, tr, name):
    _, R, C = parts.shape

    def body(p_ref, o_ref):
        acc = p_ref[0].astype(F32)
        for j in range(1, N_DEV):
            acc = acc + p_ref[j].astype(F32)
        o_ref[...] = acc

    return pl.pallas_call(
        body, name=name, grid=(R // tr,),
        in_specs=[pl.BlockSpec((N_DEV, tr, C), lambda i: (0, i, 0))],
        out_specs=pl.BlockSpec((tr, C), lambda i: (i, 0)),
        out_shape=jax.ShapeDtypeStruct((R, C), F32),
        compiler_params=_cparams("parallel"),
    )(parts)


def _adamw(w, g, m, v):
    R, C = w.shape
    tr = _row_tile(R, TOKEN_TILE)

    def body(w_ref, g_ref, m_ref, v_ref, d_ref, mo_ref, vo_ref):
        gg = g_ref[...]
        mn = ADAM_B1 * m_ref[...] + (1.0 - ADAM_B1) * gg
        vn = ADAM_B2 * v_ref[...] + (1.0 - ADAM_B2) * (gg * gg)
        m_hat = mn / (1.0 - ADAM_B1 ** ADAM_STEP)
        v_hat = vn / (1.0 - ADAM_B2 ** ADAM_STEP)
        d_ref[...] = -ADAM_LR * (m_hat / (jnp.sqrt(v_hat) + ADAM_EPS) + ADAM_WD * w_ref[...])
        mo_ref[...] = mn
        vo_ref[...] = vn

    blk = pl.BlockSpec((tr, C), lambda i: (i, 0))
    return pl.pallas_call(
        body, name="adamw", grid=(R // tr,), in_specs=[blk] * 4, out_specs=[blk] * 3,
        out_shape=[jax.ShapeDtypeStruct((R, C), F32)] * 3,
        compiler_params=_cparams("parallel"),
    )(w, g, m, v)


def _adamw_nd(w, g, m, v):
    shape = w.shape
    two_d = (1, shape[0]) if len(shape) == 1 else (int(np.prod(shape[:-1])), shape[-1])
    outs = _adamw(w.reshape(two_d), g.reshape(two_d), m.reshape(two_d), v.reshape(two_d))
    return tuple(o.reshape(shape) for o in outs)


def _f32_as_bf16_pairs(a):
    return lax.bitcast_convert_type(a, BF16).reshape(a.shape[:-1] + (a.shape[-1] * 2,))


def _bf16_pairs_as_f32(a):
    return lax.bitcast_convert_type(a.reshape(a.shape[:-1] + (a.shape[-1] // 2, 2)), F32)


def _pack_misc(w_o, w_dq, w_uq, w_dkv, pool_w, pool_scale):
    lead = w_o.shape[:-3]
    rows = [w_o, w_dq, w_uq, w_dkv, pool_w]
    flat = [r.astype(BF16).reshape(lead + (-1, REP_COLS)) for r in rows]
    ps = _f32_as_bf16_pairs(pool_scale.astype(F32)).reshape(lead + (1, -1))
    ps = jnp.concatenate([ps, jnp.zeros(lead + (1, REP_COLS - ps.shape[-1]), BF16)], axis=-1)
    used = sum(f.shape[-2] for f in flat) + 1
    pad = jnp.zeros(lead + (MISC_ROWS - used, REP_COLS), BF16)
    return jnp.concatenate(flat + [ps, pad], axis=-2)


def _unpack_misc(buf, shapes):
    out, r0 = [], 0
    for shp in shapes[:-1]:
        n = int(np.prod(shp)) // REP_COLS
        out.append(buf[:, r0:r0 + n].reshape((N_DEV,) + shp))
        r0 += n
    n_ps = int(np.prod(shapes[-1]))
    out.append(_bf16_pairs_as_f32(buf[:, r0, :2 * n_ps]).reshape((N_DEV,) + shapes[-1]))
    return out


def _cat_dev(a, axis):
    a = jnp.moveaxis(a, 0, axis)
    return a.reshape(a.shape[:axis] + (a.shape[axis] * a.shape[axis + 1],) + a.shape[axis + 2:])


def _split_dev(a, axis):
    a = a.reshape(a.shape[:axis] + (N_DEV, a.shape[axis] // N_DEV) + a.shape[axis + 1:])
    return jnp.moveaxis(a, axis, 0)


def kernel(x, ffn_pre_norm, ffn_pre_wg, ffn_pre_wu, ffn_pre_wd, mix_norm, ffn_post_norm, ffn_post_wg, ffn_post_wu, ffn_post_wd, pool_w, pool_scale, kv_in_norm, w_dkv, ckv_norm, w_uk, w_uv, q_lora_norm, w_dq, w_uq, w_o, final_norm, loss_target, m_ffn_pre_norm, m_ffn_pre_wg, m_ffn_pre_wu, m_ffn_pre_wd, m_mix_norm, m_ffn_post_norm, m_ffn_post_wg, m_ffn_post_wu, m_ffn_post_wd, m_pool_w, m_pool_scale, m_kv_in_norm, m_w_dkv, m_ckv_norm, m_w_uk, m_w_uv, m_q_lora_norm, m_w_dq, m_w_uq, m_w_o, m_final_norm, v_ffn_pre_norm, v_ffn_pre_wg, v_ffn_pre_wu, v_ffn_pre_wd, v_mix_norm, v_ffn_post_norm, v_ffn_post_wg, v_ffn_post_wu, v_ffn_post_wd, v_pool_w, v_pool_scale, v_kv_in_norm, v_w_dkv, v_ckv_norm, v_w_uk, v_w_uv, v_q_lora_norm, v_w_dq, v_w_uq, v_w_o, v_final_norm):
    weights = dict(ffn_pre_norm=ffn_pre_norm, ffn_pre_wg=ffn_pre_wg, ffn_pre_wu=ffn_pre_wu, ffn_pre_wd=ffn_pre_wd,
                   mix_norm=mix_norm, ffn_post_norm=ffn_post_norm, ffn_post_wg=ffn_post_wg,
                   ffn_post_wu=ffn_post_wu, ffn_post_wd=ffn_post_wd, pool_w=pool_w, pool_scale=pool_scale,
                   kv_in_norm=kv_in_norm, w_dkv=w_dkv, ckv_norm=ckv_norm, w_uk=w_uk, w_uv=w_uv,
                   q_lora_norm=q_lora_norm, w_dq=w_dq, w_uq=w_uq, w_o=w_o, final_norm=final_norm)
    moments_m = dict(ffn_pre_norm=m_ffn_pre_norm, ffn_pre_wg=m_ffn_pre_wg, ffn_pre_wu=m_ffn_pre_wu,
                     ffn_pre_wd=m_ffn_pre_wd, mix_norm=m_mix_norm, ffn_post_norm=m_ffn_post_norm,
                     ffn_post_wg=m_ffn_post_wg, ffn_post_wu=m_ffn_post_wu, ffn_post_wd=m_ffn_post_wd,
                     pool_w=m_pool_w, pool_scale=m_pool_scale, kv_in_norm=m_kv_in_norm, w_dkv=m_w_dkv,
                     ckv_norm=m_ckv_norm, w_uk=m_w_uk, w_uv=m_w_uv, q_lora_norm=m_q_lora_norm, w_dq=m_w_dq,
                     w_uq=m_w_uq, w_o=m_w_o, final_norm=m_final_norm)
    moments_v = dict(ffn_pre_norm=v_ffn_pre_norm, ffn_pre_wg=v_ffn_pre_wg, ffn_pre_wu=v_ffn_pre_wu,
                     ffn_pre_wd=v_ffn_pre_wd, mix_norm=v_mix_norm, ffn_post_norm=v_ffn_post_norm,
                     ffn_post_wg=v_ffn_post_wg, ffn_post_wu=v_ffn_post_wu, ffn_post_wd=v_ffn_post_wd,
                     pool_w=v_pool_w, pool_scale=v_pool_scale, kv_in_norm=v_kv_in_norm, w_dkv=v_w_dkv,
                     ckv_norm=v_ckv_norm, w_uk=v_w_uk, w_uv=v_w_uv, q_lora_norm=v_q_lora_norm, w_dq=v_w_dq,
                     w_uq=v_w_uq, w_o=v_w_o, final_norm=v_final_norm)
    order = list(weights)

    T, D = x.shape[1], x.shape[2]
    depth = ffn_pre_norm.shape[0]
    n_a = pool_w.shape[0]
    n_b = depth - n_a
    fs = ffn_pre_wd.shape[1]
    F = fs * N_DEV
    n_ffn = 2 * depth
    t_attn = min(ATTN_TILE, T)

    ffn_local = [
        jnp.stack([jnp.swapaxes(wg[l], 0, 1), jnp.swapaxes(wu[l], 0, 1), wd[l]]).astype(BF16)
        for l in range(depth)
        for wg, wu, wd in ((ffn_pre_wg, ffn_pre_wu, ffn_pre_wd), (ffn_post_wg, ffn_post_wu, ffn_post_wd))
    ]
    misc_local = _pack_misc(w_o, w_dq, w_uq.reshape(n_b, w_uq.shape[1], -1), w_dkv, pool_w, pool_scale)
    misc_shapes = (w_o.shape, w_dq.shape, (n_b, w_uq.shape[1], N_HEADS * D_QK), w_dkv.shape, pool_w.shape,
                   pool_scale.shape)
    (w0_all,) = _exchange([(ffn_local[0], "gather_mid")], "comm_all_gather")
    walls = [w0_all.reshape(3, F, D)] + [None] * (n_ffn - 1)

    def vec(a):
        return a.reshape(1, -1)

    def ffn_stage(e, h_in, norm):
        carry = [(ffn_local[e + 1], "gather_mid")] if e + 1 < n_ffn else []
        if e == 0:
            carry.append((misc_local, "to_all"))
        outs = _ffn_fwd(h_in, norm, walls[e], carry)
        if carry:
            walls[e + 1] = outs[3].reshape(3, F, D)
        return outs

    h = x.reshape(T, D)
    stage0 = ffn_stage(0, h, vec(ffn_pre_norm[0]))
    misc_all = stage0[4]
    o_blk, dq_blk, uq_blk, dkv_blk, pw_blk, ps_blk = _unpack_misc(misc_all, misc_shapes)
    w_o_f = _cat_dev(o_blk, 1)
    w_dq_f = _cat_dev(dq_blk, 1)
    w_uq_f = _cat_dev(uq_blk, 1).reshape(n_b, -1, N_HEADS, D_QK)
    w_dkv_f = _cat_dev(dkv_blk, 0)
    pool_w_f = _cat_dev(pw_blk, 2)
    pool_scale_f = _cat_dev(ps_blk, 1)
    rq = w_dq_f.shape[2]
    wqa, wqb = _rope_weight_pair(w_uq_f)
    wqa = wqa.reshape(n_b, rq, N_HEADS * D_HEAD_PAD)
    wqb = wqb.reshape(n_b, rq, N_HEADS * D_HEAD_PAD)
    wka, wkb = _rope_weight_pair(w_dkv_f)
    wuk = jnp.concatenate([w_uk, jnp.zeros_like(w_uk)], axis=-1).astype(BF16).reshape(D_NOPE, N_HEADS * D_HEAD_PAD)
    wuv = w_uv.astype(BF16).reshape(D_NOPE, N_HEADS * D_V)
    ca_q, ca_k, sb = _rope_tables(T)
    ca_q_scaled, sb_scaled = ca_q * _ATTN_SCALE_LOG2, sb * _ATTN_SCALE_LOG2

    saved = []
    k_all = v_all = craw = h_kv = None
    for l in range(depth):
        s = {"h0": h}
        h, s["g1"], s["u1"] = (stage0 if l == 0 else ffn_stage(2 * l, h, vec(ffn_pre_norm[l])))[:3]
        s["h1"] = h
        if l < n_a:
            h, s["y"] = _pool_fwd(h, vec(mix_norm[l]), pool_w_f[l], vec(pool_scale_f[l]))
        else:
            j = l - n_a
            s["cq"] = _mm_rows(h, w_dq_f[j], nt=False, out_dtype=F32, norm_g=vec(mix_norm[l]), name="q_down")
            s["q"] = _q_proj(s["cq"], vec(q_lora_norm[j]), wqa[j], wqb[j], ca_q_scaled, sb_scaled)
            s["o"], lse = _flash_fwd(s["q"], k_all, v_all)
            s["lse"] = lse.reshape(N_HEADS, T // t_attn, 1, t_attn)
            h = _mm_rows(s["o"], w_o_f[j], nt=False, out_dtype=F32, res=h, name="attn_out")
        s["h2"] = h
        h, s["g2"], s["u2"] = ffn_stage(2 * l + 1, h, vec(ffn_post_norm[l]))[:3]
        if l == n_a - 1:
            h_kv = h
            k_all, v_all, craw = _kv_proj(h, vec(kv_in_norm), wka, wkb, vec(ckv_norm), wuk, wuv, ca_k, sb)
        saved.append(s)

    dh, loss_part, d_final = _loss_head(h, vec(final_norm), loss_target.reshape(T, D))

    slabs, ffn_parts = [None] * n_ffn, [None] * n_ffn

    misc_parts = []
    big_rep_names = ["w_uk", "w_uv"]

    def packed_misc_grads():
        return _pack_misc(_split_dev(jnp.stack(d_wo), 1), _split_dev(jnp.stack(d_wdq), 1),
                          _split_dev(jnp.stack(d_wuq).reshape(n_b, rq, -1), 1), _split_dev(grads["w_dkv"], 0),
                          _split_dev(jnp.stack(d_pool_w), 2),
                          _split_dev(jnp.concatenate(d_pool_scale, axis=0), 1))

    def ffn_stage_bwd(e, dh_out, h_in, norm, gate, up):
        carry = [(slabs[e + 1].reshape(3, N_DEV, fs, D), "scatter_mid")] if e + 1 < n_ffn else []
        if e == 0:
            carry.append((packed_misc_grads(), "scatter_lead"))
        if e == 2 * n_a - 1:
            carry.append((jnp.concatenate([grads[n].reshape(-1, REP_COLS) for n in big_rep_names]), "to_all"))
        outs = _ffn_bwd_dx(dh_out, h_in, norm, gate, up, walls[e], carry)
        if e + 1 < n_ffn:
            ffn_parts[e + 1] = outs[6]
        if e == 0:
            misc_parts.insert(0, outs[7])
        if e == 2 * n_a - 1:
            misc_parts.append(outs[6 + len(carry) - 1])
        dh_in, dgt, dup, u_t, dy_t, dnorm = outs[:6]
        slabs[e] = _ffn_bwd_dw(dgt, dup, gate, up, u_t, dy_t)
        return dh_in, dnorm

    grads = {}
    d_pre, d_post, d_mix = [None] * depth, [None] * depth, [None] * depth
    d_pool_w, d_pool_scale = [None] * n_a, [None] * n_a
    d_qln, d_wdq, d_wuq, d_wo = [None] * n_b, [None] * n_b, [None] * n_b, [None] * n_b
    dks, dvs = [], []
    for l in reversed(range(depth)):
        s = saved[l]
        if l == n_a - 1:
            (dh, grads["kv_in_norm"], dwka, dwkb, grads["ckv_norm"], dwuk, dwuv) = _kv_proj_bwd(
                dks, dvs, dh, h_kv, vec(kv_in_norm), wka, wkb, craw, vec(ckv_norm), wuk, wuv, ca_k, sb)
            grads["w_dkv"] = _rope_weight_pair_grad(dwka, dwkb)
            grads["w_uk"] = dwuk.reshape(D_NOPE, N_HEADS, D_HEAD_PAD)[..., :D_NOPE]
            grads["w_uv"] = dwuv.reshape(D_NOPE, N_HEADS, D_V)
        dh, d_post[l] = ffn_stage_bwd(2 * l + 1, dh, s["h2"], vec(ffn_post_norm[l]), s["g2"], s["u2"])
        if l < n_a:
            dh, d_mix[l], d_pool_w[l], d_pool_scale[l] = _pool_bwd(
                dh, s["h1"], vec(mix_norm[l]), s["y"], pool_w_f[l], vec(pool_scale_f[l]))
        else:
            j = l - n_a
            d_wo[j] = _mm_tn(s["o"], dh, name="attn_out_dw")
            do = _mm_rows(dh, w_o_f[j], nt=True, out_dtype=BF16, name="attn_out_dx")
            delta_row = _attn_delta(s["o"], do)
            dq_t, dk_l, dv_l = _flash_bwd(s["q"], k_all, v_all, do, s["lse"], delta_row)
            dks.append(dk_l)
            dvs.append(dv_l)
            dcq, d_qln[j], dwa, dwb = _q_proj_bwd(dq_t, s["cq"], vec(q_lora_norm[j]), wqa[j], wqb[j], ca_q, sb)
            d_wuq[j] = _rope_weight_pair_grad(dwa.reshape(rq, N_HEADS, D_HEAD_PAD),
                                              dwb.reshape(rq, N_HEADS, D_HEAD_PAD))
            d_wdq[j] = _mm_tn(s["h1"], dcq, norm_g=vec(mix_norm[l]), name="q_down_dw")
            dh, d_mix[l] = _proj_bwd(dcq, w_dq_f[j], s["h1"], vec(mix_norm[l]), dh, "q_down_dx")
        dh, d_pre[l] = ffn_stage_bwd(2 * l, dh, s["h0"], vec(ffn_pre_norm[l]), s["g1"], s["u1"])
    grad_x = dh.reshape(x.shape)

    rep_names = ["ffn_pre_norm", "mix_norm", "ffn_post_norm", "kv_in_norm", "ckv_norm", "q_lora_norm", "final_norm"]
    grads["ffn_pre_norm"] = jnp.concatenate(d_pre, axis=0)
    grads["mix_norm"] = jnp.concatenate(d_mix, axis=0)
    grads["ffn_post_norm"] = jnp.concatenate(d_post, axis=0)
    grads["q_lora_norm"] = jnp.concatenate(d_qln, axis=0)
    grads["final_norm"] = d_final
    rep_flat = jnp.concatenate([grads[n].reshape(-1) for n in rep_names] + [loss_part[0, :1]])
    n_rep = rep_flat.shape[0]
    rep_rows = -(-n_rep // (8 * REP_COLS)) * 8
    rep_g = jnp.concatenate([rep_flat, jnp.zeros((rep_rows * REP_COLS - n_rep,), F32)]).reshape(rep_rows, REP_COLS)
    ffn_parts[0], rep_parts = _exchange([(slabs[0].reshape(3, N_DEV, fs, D), "scatter_mid"), (rep_g, "to_all")],
                                        "comm_grad_exchange")
    ffn_sum = jnp.stack([_sum_parts(p.reshape(N_DEV, 3 * fs, D), fs, "sum_ffn").reshape(3, fs, D)
                         for p in ffn_parts])
    misc_sum_parts = _unpack_misc(misc_parts[0], misc_shapes)
    rep_sum = _sum_parts(rep_parts, _row_tile(rep_rows, 128), "sum_rep").reshape(-1)

    def sum_small(p):
        shp = p.shape[1:]
        two_d = (int(np.prod(shp[:-1])), shp[-1])
        return _sum_parts(p.reshape((N_DEV,) + two_d), two_d[0], "sum_misc").reshape(shp)

    g_wo, g_wdq, g_wuq, g_wdkv, g_pw, g_ps = [sum_small(p) for p in misc_sum_parts]
    grads.update(w_o=g_wo, w_dq=g_wdq, w_uq=g_wuq.reshape(w_uq.shape), w_dkv=g_wdkv, pool_w=g_pw, pool_scale=g_ps)
    for kind, (npre, npost) in enumerate((("ffn_pre_wg", "ffn_post_wg"), ("ffn_pre_wu", "ffn_post_wu"),
                                          ("ffn_pre_wd", "ffn_post_wd"))):
        pre = ffn_sum[0::2, kind]
        post = ffn_sum[1::2, kind]
        if kind < 2:
            pre, post = jnp.swapaxes(pre, 1, 2), jnp.swapaxes(post, 1, 2)
        grads[npre], grads[npost] = pre, post
    off = 0
    for n in rep_names:
        size = int(np.prod(weights[n].shape))
        grads[n] = rep_sum[off:off + size].reshape(weights[n].shape)
        off += size
    loss = rep_sum[off]
    big_rep_sum = _sum_parts(misc_parts[1], _row_tile(misc_parts[1].shape[1], 64), "sum_rep_big")
    off = 0
    for n in big_rep_names:
        rows = int(np.prod(weights[n].shape)) // REP_COLS
        grads[n] = big_rep_sum[off:off + rows].reshape(weights[n].shape)
        off += rows

    deltas, new_m, new_v = {}, {}, {}
    for n in order:
        deltas[n], new_m[n], new_v[n] = _adamw_nd(weights[n], grads[n], moments_m[n], moments_v[n])
    return (loss, grad_x, *[grads[n] for n in order], *[deltas[n] for n in order],
            *[new_m[n] for n in order], *[new_v[n] for n in order])
```

```python
import numpy as np
import jax
import jax.numpy as jnp
from jax import lax
from jax.experimental import pallas as pl
from jax.experimental.pallas import tpu as pltpu

F32, BF16 = jnp.float32, jnp.bfloat16
N_DEV = 8
RMS_EPS = 1e-6
N_HEADS = 16
D_NOPE, D_ROPE, D_V = 128, 64, 128
D_QK = D_NOPE + D_ROPE
D_HEAD_PAD = 256
HEAD_GROUP = 4
ROPE_THETA = 10000.0
POOL_WINDOWS = (2, 4, 8, 16)
POOL_HALO = 16
ADAM_LR, ADAM_B1, ADAM_B2, ADAM_EPS, ADAM_WD, ADAM_STEP = 0.001, 0.9, 0.999, 1e-08, 0.01, 10
NEG_BIG = -1e30
V7X_VMEM_LIMIT = 56 * 1024 * 1024
V7X_MXU_DIM = 256
TOKEN_TILE = 512
ATTN_TILE = 512
MISC_ROWS = 864
REP_COLS = 1024


def _cparams(*sem):
    return pltpu.CompilerParams(dimension_semantics=sem, vmem_limit_bytes=V7X_VMEM_LIMIT)


def _dot(a, b):
    return lax.dot_general(a, b, (((1,), (0,)), ((), ())), preferred_element_type=F32)


def _dot_nt(a, b):
    return lax.dot_general(a, b, (((1,), (1,)), ((), ())), preferred_element_type=F32)


def _dot_tn(a, b):
    return lax.dot_general(a, b, (((0,), (0,)), ((), ())), preferred_element_type=F32)


def _rms_fwd(x, g):
    r = lax.rsqrt(jnp.mean(x * x, axis=-1, keepdims=True) + RMS_EPS)
    return (x * r) * g


def _rms_bwd(du, x, g):
    r = lax.rsqrt(jnp.mean(x * x, axis=-1, keepdims=True) + RMS_EPS)
    xh = x * r
    dg = jnp.sum(du * xh, axis=0, keepdims=True)
    dxh = du * g
    dx = r * (dxh - xh * jnp.mean(dxh * xh, axis=-1, keepdims=True))
    return dx, dg


def _sigmoid(x):
    return 1.0 / (1.0 + jnp.exp(-x))


def _split_bf16(x):
    hi = x.astype(BF16)
    lo = (x - hi.astype(F32)).astype(BF16)
    return hi, lo


def _full(shape):
    return pl.BlockSpec(shape, lambda *_: (0,) * len(shape))


def _resident(shape):
    return pl.BlockSpec(shape, lambda *_: (0,) * len(shape), pipeline_mode=pl.Buffered(1))


def _row_tile(rows, cap):
    for t in range(min(cap, rows) // 8 * 8, 0, -8):
        if rows % t == 0:
            return t
    return rows


def _peers():
    x, y, c = lax.axis_index("x"), lax.axis_index("y"), lax.axis_index("c")
    out = []
    for k in range(1, N_DEV):
        px = 1 - x if (k >> 2) & 1 else x
        py = 1 - y if (k >> 1) & 1 else y
        pc = 1 - c if k & 1 else c
        out.append(((px, py, pc), 4 * px + 2 * py + pc))
    return 4 * x + 2 * y + c, out


_ROUTES = {
    "gather_mid": (lambda ref, idx: ref, lambda ref, idx: ref.at[:, idx], lambda s: s[:1] + (N_DEV,) + s[1:]),
    "to_all": (lambda ref, idx: ref, lambda ref, idx: ref.at[idx], lambda s: (N_DEV,) + s),
    "scatter_mid": (lambda ref, idx: ref.at[:, idx], lambda ref, idx: ref.at[idx],
                    lambda s: (N_DEV, s[0]) + s[2:]),
    "scatter_lead": (lambda ref, idx: ref.at[idx], lambda ref, idx: ref.at[idx], lambda s: s),
}


def _route_fns(items):
    kinds = [kind for _, kind in items]
    return (lambda j, ref, idx: _ROUTES[kinds[j]][0](ref, idx)), (lambda j, ref, idx: _ROUTES[kinds[j]][1](ref, idx))


def _route_out_shapes(items):
    return [jax.ShapeDtypeStruct(_ROUTES[kind][2](arr.shape), arr.dtype) for arr, kind in items]


def _exchange(items, name):
    n = len(items)
    fns = _route_fns(items)

    def body(*refs):
        ins, outs, sems = refs[:n], refs[n:2 * n], refs[2 * n:]
        _exchange_start(ins, outs, sems, *fns)
        _exchange_wait(ins, outs, sems, *fns)

    any_spec = pl.BlockSpec(memory_space=pl.ANY)
    return pl.pallas_call(
        body, name=name, out_shape=_route_out_shapes(items),
        in_specs=[any_spec] * n, out_specs=[any_spec] * n,
        scratch_shapes=_exchange_sems(n),
    )(*[arr for arr, _ in items])


def _exchange_sems(n):
    return [pltpu.SemaphoreType.DMA((n, N_DEV - 1)), pltpu.SemaphoreType.DMA((n, N_DEV - 1)),
            pltpu.SemaphoreType.DMA((n,))]


def _own_copies(ins, outs, sems, src_of, dst_of):
    me, _ = _peers()
    return [pltpu.make_async_copy(src_of(j, ins[j], me), dst_of(j, outs[j], me), sems[2].at[j])
            for j in range(len(ins))]


def _remote_copies(ins, outs, sems, src_of, dst_of, receiving):
    me, peers = _peers()
    return [pltpu.make_async_remote_copy(
        src_ref=src_of(j, ins[j], pidx), dst_ref=dst_of(j, outs[j], pidx if receiving else me),
        send_sem=sems[0].at[j, k], recv_sem=sems[1].at[j, k],
        device_id=peer, device_id_type=pl.DeviceIdType.MESH)
        for k, (peer, pidx) in enumerate(peers) for j in range(len(ins))]


def _exchange_start(ins, outs, sems, src_of, dst_of):
    for cp in _own_copies(ins, outs, sems, src_of, dst_of):
        cp.start()
    for cp in _remote_copies(ins, outs, sems, src_of, dst_of, receiving=False):
        cp.start()


def _exchange_wait(ins, outs, sems, src_of, dst_of):
    for cp in _remote_copies(ins, outs, sems, src_of, dst_of, receiving=True):
        cp.wait_recv()
    for cp in _remote_copies(ins, outs, sems, src_of, dst_of, receiving=False):
        cp.wait_send()
    for cp in _own_copies(ins, outs, sems, src_of, dst_of):
        cp.wait()


def _f_parts(F):
    first = -(-(F // V7X_MXU_DIM) // 2) * V7X_MXU_DIM
    return (slice(0, first), slice(first, F))


def _carried(items):
    if not items:
        return [], [], [], []
    return ([arr for arr, _ in items], [pl.BlockSpec(memory_space=pl.ANY)] * len(items), _route_out_shapes(items),
            _exchange_sems(len(items)))


def _ffn_fwd(h, g, w3, carry=()):
    T, D = h.shape
    F = w3.shape[1]
    tm = min(TOKEN_TILE, T)
    nt = T // tm
    nc = len(carry)
    has_c = nc > 0
    c_in, c_specs, c_out, c_sems = _carried(carry)

    def body(*refs):
        h_ref, g_ref, w_ref = refs[:3]
        ho_ref, gate_ref, up_ref = refs[3 + nc:6 + nc]
        i = pl.program_id(0)
        if has_c:
            comm = (refs[3:3 + nc], refs[6 + nc:6 + 2 * nc], refs[6 + 2 * nc:], *_route_fns(carry))

            @pl.when(i == 0)
            def _():
                _exchange_start(*comm)

        x = h_ref[...]
        u = _rms_fwd(x, g_ref[...]).astype(BF16)
        acc = None
        for cols in _f_parts(F):
            gate = _dot_nt(u, w_ref[0, cols, :])
            up = _dot_nt(u, w_ref[1, cols, :])
            gate_ref[:, cols] = gate.astype(BF16)
            up_ref[:, cols] = up.astype(BF16)
            part = _dot((gate * _sigmoid(gate) * up).astype(BF16), w_ref[2, cols, :])
            acc = part if acc is None else acc + part
        ho_ref[...] = x + 0.5 * acc

        if has_c:
            @pl.when(i == nt - 1)
            def _():
                _exchange_wait(*comm)

    row = pl.BlockSpec((tm, D), lambda i: (i, 0))
    wide = pl.BlockSpec((tm, F), lambda i: (i, 0))
    return pl.pallas_call(
        body, name="ffn_fwd_gather" if has_c else "ffn_fwd", grid=(nt,),
        in_specs=[row, _full((1, D)), _resident(w3.shape)] + c_specs,
        out_specs=[row, wide, wide] + c_specs,
        out_shape=[jax.ShapeDtypeStruct((T, D), F32), jax.ShapeDtypeStruct((T, F), BF16),
                   jax.ShapeDtypeStruct((T, F), BF16)] + c_out,
        scratch_shapes=c_sems,
        compiler_params=_cparams("arbitrary"),
    )(h, g, w3, *c_in)


def _ffn_bwd_dx(dho, h, g, gate, up, w3, carry=()):
    T, D = h.shape
    F = w3.shape[1]
    tm = min(TOKEN_TILE // 2, T)
    nt = T // tm
    nc = len(carry)
    has_c = nc > 0
    c_in, c_specs, c_out, c_sems = _carried(carry)

    def body(*refs):
        dho_ref, h_ref, g_ref, gate_ref, up_ref, w_ref = refs[:6]
        dhi_ref, dgate_ref, dup_ref, u_ref, dy_ref, dg_ref = refs[6 + nc:12 + nc]
        i = pl.program_id(0)
        if has_c:
            comm = (refs[6:6 + nc], refs[12 + nc:12 + 2 * nc], refs[12 + 2 * nc:], *_route_fns(carry))

            @pl.when(i == 0)
            def _():
                _exchange_start(*comm)

        dho = dho_ref[...]
        x = h_ref[...]
        dy_f = 0.5 * dho
        dy = dy_f.astype(BF16)
        dy_ref[...] = dy_f.T.astype(BF16)
        u_ref[...] = _rms_fwd(x, g_ref[...]).T.astype(BF16)
        acc = None
        for cols in _f_parts(F):
            dact = _dot_nt(dy, w_ref[2, cols, :])
            gt = gate_ref[:, cols].astype(F32)
            sig = _sigmoid(gt)
            dup = (dact * (gt * sig)).astype(BF16)
            dgate = (dact * up_ref[:, cols].astype(F32) * (sig * (1.0 + gt * (1.0 - sig)))).astype(BF16)
            dup_ref[:, cols] = dup
            dgate_ref[:, cols] = dgate
            part = _dot(dgate, w_ref[0, cols, :]) + _dot(dup, w_ref[1, cols, :])
            acc = part if acc is None else acc + part
        dx, dg = _rms_bwd(acc, x, g_ref[...])
        dhi_ref[...] = dho + dx

        @pl.when(i == 0)
        def _():
            dg_ref[...] = jnp.zeros_like(dg_ref)

        dg_ref[...] += dg

        if has_c:
            @pl.when(i == nt - 1)
            def _():
                _exchange_wait(*comm)

    row = pl.BlockSpec((tm, D), lambda i: (i, 0))
    col = pl.BlockSpec((D, tm), lambda i: (0, i))
    wide = pl.BlockSpec((tm, F), lambda i: (i, 0))
    return pl.pallas_call(
        body, name="ffn_bwd_dx_scatter" if has_c else "ffn_bwd_dx", grid=(nt,),
        in_specs=[row, row, _full((1, D)), wide, wide, _resident(w3.shape)] + c_specs,
        out_specs=[row, wide, wide, col, col, _full((1, D))] + c_specs,
        out_shape=[jax.ShapeDtypeStruct((T, D), F32), jax.ShapeDtypeStruct((T, F), BF16),
                   jax.ShapeDtypeStruct((T, F), BF16), jax.ShapeDtypeStruct((D, T), BF16),
                   jax.ShapeDtypeStruct((D, T), BF16), jax.ShapeDtypeStruct((1, D), F32)] + c_out,
        scratch_shapes=c_sems,
        compiler_params=_cparams("arbitrary"),
    )(dho, h, g, gate, up, w3, *c_in)


def _ffn_bwd_dw(dgate, dup, gate, up, u_t, dy_t):
    T, F = gate.shape
    D = u_t.shape[0]
    tfw = F // 2
    tk = min(TOKEN_TILE, T)
    nk = T // tk

    def body(dgate_ref, dup_ref, gate_ref, up_ref, ut_ref, dyt_ref, out_ref, acc_sc):
        k = pl.program_id(1)

        @pl.when(k == 0)
        def _():
            acc_sc[...] = jnp.zeros_like(acc_sc)

        uu = ut_ref[...]
        acc_sc[0] += _dot(uu, dgate_ref[...])
        acc_sc[1] += _dot(uu, dup_ref[...])
        gt = gate_ref[...].astype(F32)
        act = (gt * _sigmoid(gt) * up_ref[...].astype(F32)).astype(BF16)
        acc_sc[2] += _dot(dyt_ref[...], act)

        @pl.when(k == nk - 1)
        def _():
            for kind in range(3):
                out_ref[kind] = acc_sc[kind].T.astype(BF16)

    blk = pl.BlockSpec((tk, tfw), lambda j, k: (k, j))
    col = pl.BlockSpec((D, tk), lambda j, k: (0, k))
    return pl.pallas_call(
        body, name="ffn_bwd_dw", grid=(F // tfw, nk),
        in_specs=[blk, blk, blk, blk, col, col],
        out_specs=pl.BlockSpec((3, tfw, D), lambda j, k: (0, j, 0)),
        out_shape=jax.ShapeDtypeStruct((3, F, D), BF16),
        scratch_shapes=[pltpu.VMEM((3, D, tfw), F32)],
        compiler_params=_cparams("parallel", "arbitrary"),
    )(dgate, dup, gate, up, u_t, dy_t)


def _mm_rows(a, b, *, nt, out_dtype, norm_g=None, res=None, name):
    T, K = a.shape
    N = b.shape[0] if nt else b.shape[1]
    tm = min(TOKEN_TILE, T)
    has_g, has_r = norm_g is not None, res is not None

    def body(*refs):
        a_ref, b_ref = refs[0], refs[1]
        o_ref = refs[-1]
        x = a_ref[...]
        if has_g:
            x = _rms_fwd(x, refs[2][...])
        x = x.astype(BF16)
        acc = _dot_nt(x, b_ref[...]) if nt else _dot(x, b_ref[...])
        if has_r:
            acc = refs[2 + has_g][...] + acc
        o_ref[...] = acc.astype(out_dtype)

    ins, specs = [a, b], [pl.BlockSpec((tm, K), lambda i: (i, 0)), _full(b.shape)]
    if has_g:
        ins.append(norm_g)
        specs.append(_full((1, K)))
    if has_r:
        ins.append(res)
        specs.append(pl.BlockSpec((tm, N), lambda i: (i, 0)))
    return pl.pallas_call(
        body, name=name, grid=(T // tm,), in_specs=specs,
        out_specs=pl.BlockSpec((tm, N), lambda i: (i, 0)),
        out_shape=jax.ShapeDtypeStruct((T, N), out_dtype),
        compiler_params=_cparams("parallel"),
    )(*ins)


def _mm_tn(a, b, *, norm_g=None, name):
    T, M = a.shape
    N = b.shape[1]
    tk = min(TOKEN_TILE, T)
    has_g = norm_g is not None

    def body(*refs):
        a_ref, b_ref, o_ref = refs[0], refs[1], refs[-1]

        @pl.when(pl.program_id(0) == 0)
        def _():
            o_ref[...] = jnp.zeros_like(o_ref)

        x = a_ref[...]
        if has_g:
            x = _rms_fwd(x, refs[2][...])
        o_ref[...] += _dot_tn(x.astype(BF16), b_ref[...].astype(BF16))

    ins = [a, b]
    specs = [pl.BlockSpec((tk, M), lambda k: (k, 0)), pl.BlockSpec((tk, N), lambda k: (k, 0))]
    if has_g:
        ins.append(norm_g)
        specs.append(_full((1, M)))
    return pl.pallas_call(
        body, name=name, grid=(T // tk,), in_specs=specs, out_specs=_full((M, N)),
        out_shape=jax.ShapeDtypeStruct((M, N), F32),
        compiler_params=_cparams("arbitrary"),
    )(*ins)


def _proj_bwd(dz, w, h, g, dh, name):
    T, D = h.shape
    N = w.shape[1]
    tm = min(TOKEN_TILE, T)

    def body(dz_ref, w_ref, h_ref, g_ref, dh_ref, o_ref, dg_ref):
        du = _dot_nt(dz_ref[...].astype(BF16), w_ref[...])
        dx, dg = _rms_bwd(du, h_ref[...], g_ref[...])
        o_ref[...] = dh_ref[...] + dx

        @pl.when(pl.program_id(0) == 0)
        def _():
            dg_ref[...] = jnp.zeros_like(dg_ref)

        dg_ref[...] += dg

    row = pl.BlockSpec((tm, D), lambda i: (i, 0))
    return pl.pallas_call(
        body, name=name, grid=(T // tm,),
        in_specs=[pl.BlockSpec((tm, N), lambda i: (i, 0)), _full((D, N)), row, _full((1, D)), row],
        out_specs=[row, _full((1, D))],
        out_shape=[jax.ShapeDtypeStruct((T, D), F32), jax.ShapeDtypeStruct((1, D), F32)],
        compiler_params=_cparams("arbitrary"),
    )(dz, w, h, g, dh)


def _pool_bands(tm):
    r = np.arange(tm)[:, None]
    c = np.arange(tm)[None, :]
    j = np.arange(POOL_HALO)[None, :]
    main, halo, main_t, halo_t = [], [], [], []
    for w in POOL_WINDOWS:
        main.append(((r - c >= 0) & (r - c < w)) / w)
        halo.append((r + POOL_HALO - j < w) / w)
        main_t.append(((c - r >= 0) & (c - r < w)) / w)
        halo_t.append((tm + j - r < w) / w)
    return tuple(jnp.asarray(np.stack(m), BF16) for m in (main, halo, main_t, halo_t))


def _pool_count_scale(i, tm, w):
    t = i * tm + lax.broadcasted_iota(jnp.int32, (tm, 1), 0)
    return w / jnp.minimum(t + 1, w).astype(F32)


def _pool_fwd(h, g, wp, scale):
    T, D = h.shape
    G, dg = len(POOL_WINDOWS), D // len(POOL_WINDOWS)
    tm = min(TOKEN_TILE // 2, T)
    hb = tm // POOL_HALO
    bm, bh, _, _ = _pool_bands(tm)

    def body(h_ref, hh_ref, g_ref, wp_ref, sc_ref, bm_ref, bh_ref, ho_ref, y_ref):
        i = pl.program_id(0)
        x = h_ref[...]
        u = _rms_fwd(x, g_ref[...])
        uh = _rms_fwd(hh_ref[...], g_ref[...]) * (i > 0).astype(F32)
        for gi, w in enumerate(POOL_WINDOWS):
            cols = slice(gi * dg, (gi + 1) * dg)
            ug = u[:, cols]
            hi, lo = _split_bf16(ug)
            hhi, hlo = _split_bf16(uh[:, cols])
            s = (_dot(bm_ref[gi], hi) + _dot(bm_ref[gi], lo)
                 + _dot(bh_ref[gi], hhi) + _dot(bh_ref[gi], hlo))
            y = (s * _pool_count_scale(i, tm, w) - ug).astype(BF16)
            y_ref[:, cols] = y
            ho_ref[:, cols] = x[:, cols] + _dot(y, wp_ref[gi]) * sc_ref[:, cols]

    row = pl.BlockSpec((tm, D), lambda i: (i, 0))
    return pl.pallas_call(
        body, name="pool_fwd", grid=(T // tm,),
        in_specs=[row, pl.BlockSpec((POOL_HALO, D), lambda i: (jnp.maximum(i * hb - 1, 0), 0)),
                  _full((1, D)), _full((G, dg, dg)), _full((1, D)),
                  _full((G, tm, tm)), _full((G, tm, POOL_HALO))],
        out_specs=[row, row],
        out_shape=[jax.ShapeDtypeStruct((T, D), F32), jax.ShapeDtypeStruct((T, D), BF16)],
        compiler_params=_cparams("parallel"),
    )(h, h, g, wp, scale, bm, bh)


def _pool_bwd(dh, h, g, y, wp, scale):
    T, D = h.shape
    G, dg = len(POOL_WINDOWS), D // len(POOL_WINDOWS)
    tm = min(TOKEN_TILE // 2, T)
    hb = tm // POOL_HALO
    nt = T // tm
    _, _, bmt, bht = _pool_bands(tm)

    def body(dh_ref, dhn_ref, h_ref, g_ref, y_ref, wp_ref, sc_ref, bmt_ref, bht_ref,
             o_ref, dg_ref, dwp_ref, dsc_ref, du_sc):
        i = pl.program_id(0)

        @pl.when(i == 0)
        def _():
            dg_ref[...] = jnp.zeros_like(dg_ref)
            dwp_ref[...] = jnp.zeros_like(dwp_ref)
            dsc_ref[...] = jnp.zeros_like(dsc_ref)

        dho = dh_ref[...]
        dz = dho * sc_ref[...]
        dzn = dhn_ref[...] * sc_ref[...] * (i < nt - 1).astype(F32)
        for gi, w in enumerate(POOL_WINDOWS):
            cols = slice(gi * dg, (gi + 1) * dg)
            yg = y_ref[:, cols]
            dzg = dz[:, cols].astype(BF16)
            dsc_ref[:, cols] += jnp.sum(dho[:, cols] * _dot(yg, wp_ref[gi]), axis=0, keepdims=True)
            dwp_ref[gi] += _dot_tn(yg, dzg)
            dy = _dot_nt(dzg, wp_ref[gi])
            dyn = _dot_nt(dzn[:, cols].astype(BF16), wp_ref[gi])
            hi, lo = _split_bf16(dy * _pool_count_scale(i, tm, w))
            nhi, nlo = _split_bf16(dyn)
            du_sc[:, cols] = (_dot(bmt_ref[gi], hi) + _dot(bmt_ref[gi], lo)
                              + _dot(bht_ref[gi], nhi) + _dot(bht_ref[gi], nlo) - dy)
        dx, dgp = _rms_bwd(du_sc[...], h_ref[...], g_ref[...])
        o_ref[...] = dho + dx
        dg_ref[...] += dgp

    row = pl.BlockSpec((tm, D), lambda i: (i, 0))
    return pl.pallas_call(
        body, name="pool_bwd", grid=(nt,),
        in_specs=[row, pl.BlockSpec((POOL_HALO, D), lambda i: (jnp.minimum((i + 1) * hb, T // POOL_HALO - 1), 0)),
                  row, _full((1, D)), row, _full((G, dg, dg)), _full((1, D)),
                  _full((G, tm, tm)), _full((G, tm, POOL_HALO))],
        out_specs=[row, _full((1, D)), _full((G, dg, dg)), _full((1, D))],
        out_shape=[jax.ShapeDtypeStruct((T, D), F32), jax.ShapeDtypeStruct((1, D), F32),
                   jax.ShapeDtypeStruct((G, dg, dg), F32), jax.ShapeDtypeStruct((1, D), F32)],
        scratch_shapes=[pltpu.VMEM((tm, D), F32)],
        compiler_params=_cparams("arbitrary"),
    )(dh, dh, h, g, y, wp, scale, bmt, bht)


def _rope_tables(T):
    pos = jnp.arange(T, dtype=F32)
    inv_freq = ROPE_THETA ** (-jnp.arange(0, D_ROPE, 2, dtype=F32) / D_ROPE)
    ang = pos[:, None] * inv_freq[None, :]
    cos2 = jnp.tile(jnp.cos(ang), (1, 2))
    sin2 = jnp.tile(jnp.sin(ang), (1, 2))
    pad = jnp.zeros((T, D_HEAD_PAD - D_QK), F32)
    ca_q = jnp.concatenate([jnp.ones((T, D_NOPE), F32), cos2, pad], axis=1)
    ca_k = jnp.concatenate([jnp.zeros((T, D_NOPE), F32), cos2, pad], axis=1)
    sb = jnp.concatenate([jnp.zeros((T, D_NOPE), F32), sin2, pad], axis=1)
    return ca_q, ca_k, sb


def _rope_weight_pair(w):
    half = D_ROPE // 2
    z_pad = jnp.zeros(w.shape[:-1] + (D_HEAD_PAD - D_QK,), w.dtype)
    z_nope = jnp.zeros(w.shape[:-1] + (D_NOPE,), w.dtype)
    wa = jnp.concatenate([w, z_pad], axis=-1)
    wb = jnp.concatenate([z_nope, -w[..., D_NOPE + half:], w[..., D_NOPE:D_NOPE + half], z_pad], axis=-1)
    return wa, wb


def _rope_weight_pair_grad(dwa, dwb):
    half = D_ROPE // 2
    d1 = dwa[..., D_NOPE:D_NOPE + half] + dwb[..., D_NOPE + half:D_QK]
    d2 = dwa[..., D_NOPE + half:D_QK] - dwb[..., D_NOPE:D_NOPE + half]
    return jnp.concatenate([dwa[..., :D_NOPE], d1, d2], axis=-1)


def _q_proj(cq, qg, wa, wb, ca, sb):
    T, R = cq.shape
    tm = min(TOKEN_TILE, T)
    P = D_HEAD_PAD
    GP = HEAD_GROUP * P

    def body(cq_ref, qg_ref, wa_ref, wb_ref, ca_ref, sb_ref, q_ref):
        c = _rms_fwd(cq_ref[...], qg_ref[...]).astype(BF16)
        ca = jnp.tile(ca_ref[...], (1, HEAD_GROUP))
        sb = jnp.tile(sb_ref[...], (1, HEAD_GROUP))
        q_ref[...] = (_dot(c, wa_ref[...]) * ca + _dot(c, wb_ref[...]) * sb).astype(BF16)

    tok = pl.BlockSpec((tm, P), lambda i, hh: (i, 0))
    wsp = pl.BlockSpec((R, GP), lambda i, hh: (0, hh))
    return pl.pallas_call(
        body, name="q_proj", grid=(T // tm, N_HEADS // HEAD_GROUP),
        in_specs=[pl.BlockSpec((tm, R), lambda i, hh: (i, 0)), _full((1, R)), wsp, wsp, tok, tok],
        out_specs=pl.BlockSpec((tm, GP), lambda i, hh: (i, hh)),
        out_shape=jax.ShapeDtypeStruct((T, N_HEADS * P), BF16),
        compiler_params=_cparams("parallel", "arbitrary"),
    )(cq, qg, wa, wb, ca, sb)


def _q_proj_bwd(dq, cq, qg, wa, wb, ca, sb):
    T, R = cq.shape
    tm = min(TOKEN_TILE, T)
    P = D_HEAD_PAD
    GP = HEAD_GROUP * P

    def body(dq_ref, cq_ref, qg_ref, wa_ref, wb_ref, ca_ref, sb_ref, dcq_ref, dqg_ref, dwa_ref, dwb_ref):
        @pl.when(pl.program_id(0) == 0)
        def _():
            for r in (dqg_ref, dwa_ref, dwb_ref):
                r[...] = jnp.zeros_like(r)

        cq_f = cq_ref[...]
        cqn = _rms_fwd(cq_f, qg_ref[...]).astype(BF16)
        ca = jnp.tile(ca_ref[...], (1, HEAD_GROUP))
        sb = jnp.tile(sb_ref[...], (1, HEAD_GROUP))
        acc = None
        for grp in range(N_HEADS // HEAD_GROUP):
            cols = slice(grp * GP, (grp + 1) * GP)
            d = dq_ref[cols, :].T
            da = (d * ca).astype(BF16)
            db = (d * sb).astype(BF16)
            part = _dot_nt(da, wa_ref[:, cols]) + _dot_nt(db, wb_ref[:, cols])
            acc = part if acc is None else acc + part
            dwa_ref[:, cols] += _dot_tn(cqn, da)
            dwb_ref[:, cols] += _dot_tn(cqn, db)
        dx, dg = _rms_bwd(acc, cq_f, qg_ref[...])
        dcq_ref[...] = dx
        dqg_ref[...] += dg

    tok = pl.BlockSpec((tm, P), lambda i: (i, 0))
    rr = pl.BlockSpec((tm, R), lambda i: (i, 0))
    wfull = _full((R, N_HEADS * P))
    return pl.pallas_call(
        body, name="q_proj_bwd", grid=(T // tm,),
        in_specs=[pl.BlockSpec((N_HEADS * P, tm), lambda i: (0, i)), rr, _full((1, R)), wfull, wfull, tok, tok],
        out_specs=[rr, _full((1, R)), wfull, wfull],
        out_shape=[jax.ShapeDtypeStruct((T, R), F32), jax.ShapeDtypeStruct((1, R), F32),
                   jax.ShapeDtypeStruct((R, N_HEADS * P), F32), jax.ShapeDtypeStruct((R, N_HEADS * P), F32)],
        compiler_params=_cparams("arbitrary"),
    )(dq, cq, qg, wa, wb, ca, sb)


def _kv_proj(h, g_in, wka, wkb, g_c, wuk, wuv, ca, sb):
    T, D = h.shape
    tm = min(TOKEN_TILE, T)
    P, C = D_HEAD_PAD, D_NOPE

    def body(h_ref, gi_ref, wka_ref, wkb_ref, gc_ref, wuk_ref, wuv_ref, ca_ref, sb_ref, k_ref, v_ref, craw_ref):
        u = _rms_fwd(h_ref[...], gi_ref[...]).astype(BF16)
        kva = _dot(u, wka_ref[...])
        kvb = _dot(u, wkb_ref[...])
        craw = kva[:, :C]
        craw_ref[...] = craw
        c = _rms_fwd(craw, gc_ref[...]).astype(BF16)
        kr = kva * ca_ref[...] + kvb * sb_ref[...]
        kn = _dot(c, wuk_ref[...])
        for hh in range(N_HEADS):
            k_ref[:, hh * P:(hh + 1) * P] = (kn[:, hh * P:(hh + 1) * P] + kr).astype(BF16)
        v_ref[...] = _dot(c, wuv_ref[...]).astype(BF16)

    tok = pl.BlockSpec((tm, P), lambda i: (i, 0))
    return pl.pallas_call(
        body, name="kv_proj", grid=(T // tm,),
        in_specs=[pl.BlockSpec((tm, D), lambda i: (i, 0)), _full((1, D)), _full((D, P)), _full((D, P)),
                  _full((1, C)), _full(wuk.shape), _full(wuv.shape), tok, tok],
        out_specs=[pl.BlockSpec((tm, N_HEADS * P), lambda i: (i, 0)),
                   pl.BlockSpec((tm, N_HEADS * D_V), lambda i: (i, 0)), pl.BlockSpec((tm, C), lambda i: (i, 0))],
        out_shape=[jax.ShapeDtypeStruct((T, N_HEADS * P), BF16), jax.ShapeDtypeStruct((T, N_HEADS * D_V), BF16),
                   jax.ShapeDtypeStruct((T, C), F32)],
        compiler_params=_cparams("parallel"),
    )(h, g_in, wka, wkb, g_c, wuk, wuv, ca, sb)


def _kv_proj_bwd(dks, dvs, dh, h, g_in, wka, wkb, craw, g_c, wuk, wuv, ca, sb):
    T, D = h.shape
    tm = min(TOKEN_TILE // 2, T)
    P, C = D_HEAD_PAD, D_NOPE
    nl = len(dks)

    def body(*refs):
        dk_refs, dv_refs = refs[:nl], refs[nl:2 * nl]
        (dh_ref, h_ref, gi_ref, wka_ref, wkb_ref, craw_ref, gc_ref, wuk_ref, wuv_ref,
         ca_ref, sb_ref, o_ref, dgi_ref, dwka_ref, dwkb_ref, dgc_ref, dwuk_ref, dwuv_ref) = refs[2 * nl:]

        @pl.when(pl.program_id(0) == 0)
        def _():
            for r in (dgi_ref, dwka_ref, dwkb_ref, dgc_ref, dwuk_ref, dwuv_ref):
                r[...] = jnp.zeros_like(r)

        x = h_ref[...]
        u = _rms_fwd(x, gi_ref[...]).astype(BF16)
        craw = craw_ref[...]
        c = _rms_fwd(craw, gc_ref[...]).astype(BF16)
        dkf = sum(r[...] for r in dk_refs[1:]) + dk_refs[0][...]
        dkb = dkf.astype(BF16)
        dvb = (sum(r[...] for r in dv_refs[1:]) + dv_refs[0][...]).astype(BF16)
        dwuk_ref[...] += _dot_tn(c, dkb)
        dwuv_ref[...] += _dot_tn(c, dvb)
        dc = _dot_nt(dkb, wuk_ref[...]) + _dot_nt(dvb, wuv_ref[...])
        dkr = dkf[:, :P]
        for hh in range(1, N_HEADS):
            dkr = dkr + dkf[:, hh * P:(hh + 1) * P]
        dcraw, dgc = _rms_bwd(dc, craw, gc_ref[...])
        dgc_ref[...] += dgc
        dkva = jnp.concatenate([dcraw, (dkr * ca_ref[...])[:, C:]], axis=1).astype(BF16)
        dkvb = (dkr * sb_ref[...]).astype(BF16)
        dwka_ref[...] += _dot_tn(u, dkva)
        dwkb_ref[...] += _dot_tn(u, dkvb)
        du = _dot_nt(dkva, wka_ref[...]) + _dot_nt(dkvb, wkb_ref[...])
        dx, dgi = _rms_bwd(du, x, gi_ref[...])
        dgi_ref[...] += dgi
        o_ref[...] = dh_ref[...] + dx

    row = pl.BlockSpec((tm, D), lambda i: (i, 0))
    tok = pl.BlockSpec((tm, P), lambda i: (i, 0))
    return pl.pallas_call(
        body, name="kv_proj_bwd", grid=(T // tm,),
        in_specs=[pl.BlockSpec((tm, N_HEADS * P), lambda i: (i, 0))] * nl
        + [pl.BlockSpec((tm, N_HEADS * D_V), lambda i: (i, 0))] * nl
        + [row, row, _full((1, D)), _full((D, P)), _full((D, P)), pl.BlockSpec((tm, C), lambda i: (i, 0)),
           _full((1, C)), _full(wuk.shape), _full(wuv.shape), tok, tok],
        out_specs=[row, _full((1, D)), _full((D, P)), _full((D, P)), _full((1, C)),
                   _full(wuk.shape), _full(wuv.shape)],
        out_shape=[jax.ShapeDtypeStruct((T, D), F32), jax.ShapeDtypeStruct((1, D), F32),
                   jax.ShapeDtypeStruct((D, P), F32), jax.ShapeDtypeStruct((D, P), F32),
                   jax.ShapeDtypeStruct((1, C), F32), jax.ShapeDtypeStruct(wuk.shape, F32),
                   jax.ShapeDtypeStruct(wuv.shape, F32)],
        compiler_params=_cparams("arbitrary"),
    )(*dks, *dvs, dh, h, g_in, wka, wkb, craw, g_c, wuk, wuv, ca, sb)


_ATTN_SCALE = D_QK ** -0.5
_LOG2_E = 1.4426950408889634
_LN_2 = 0.6931471805599453
_ATTN_SCALE_LOG2 = _ATTN_SCALE * _LOG2_E


def _flash_fwd(q, k, v):
    T = q.shape[0]
    t = min(ATTN_TILE, T // 2)
    tq = 2 * t
    P = D_HEAD_PAD

    def body(q_ref, k_ref, v_ref, o_ref, lse_ref, m_sc, l_sc, acc_sc, s0_sc, s1_sc, p0_sc, p1_sc, a0_sc, a1_sc):
        qi = pl.program_id(1)
        n = 2 * (qi + 1)
        s_sc, p_sc, a_sc = (s0_sc, s1_sc), (p0_sc, p1_sc), (a0_sc, a1_sc)
        m_sc[...] = jnp.full_like(m_sc, NEG_BIG)
        l_sc[...] = jnp.zeros_like(l_sc)
        acc_sc[...] = jnp.zeros_like(acc_sc)

        def rows_of(c):
            return pl.ds(pl.multiple_of(c * t, t), t)

        def scores(c, slot):
            s_sc[slot][...] = _dot_nt(k_ref[rows_of(c), :], q_ref[...])

        def softmax(slot, key_offset):
            s_t = s_sc[slot][...]
            if key_offset is not None:
                rows = lax.broadcasted_iota(jnp.int32, (t, tq), 0) + key_offset
                s_t = jnp.where(rows <= lax.broadcasted_iota(jnp.int32, (t, tq), 1), s_t, NEG_BIG)
            m_prev = m_sc[...]
            m_new = jnp.maximum(m_prev, jnp.max(s_t, axis=0, keepdims=True))
            p_t = jnp.exp2(s_t - m_new)
            alpha = jnp.exp2(m_prev - m_new)
            l_sc[...] = alpha * l_sc[...] + jnp.sum(p_t, axis=0, keepdims=True)
            m_sc[...] = m_new
            p_sc[slot][...] = p_t.astype(BF16)
            a_sc[slot][...] = alpha

        def values(c, slot):
            acc_sc[...] = a_sc[slot][...] * acc_sc[...] + _dot_tn(v_ref[rows_of(c), :], p_sc[slot][...])

        def stage(c, slot, first=False, last=False, key_offset=None):
            if not first:
                values(c - 1, 1 - slot)
            if not last:
                scores(c + 1, 1 - slot)
            softmax(slot, key_offset)

        def drain():
            stage(n - 2, 0, key_offset=0)
            stage(n - 1, 1, last=True, key_offset=t)
            values(n - 1, 1)

        scores(0, 0)

        @pl.when(qi == 0)
        def _():
            stage(0, 0, first=True, key_offset=0)
            stage(1, 1, last=True, key_offset=t)
            values(1, 1)

        @pl.when(qi > 0)
        def _():
            stage(0, 0, first=True)

            def pair(j, carry):
                stage(1 + 2 * j, 1)
                stage(2 + 2 * j, 0)
                return carry

            lax.fori_loop(0, qi - 1, pair, 0)
            stage(n - 3, 1)
            drain()

        l = l_sc[...]
        o_ref[...] = (acc_sc[...] / l).T.astype(BF16)
        lse_ref[...] = m_sc[...] + jnp.log(l) * _LOG2_E

    return pl.pallas_call(
        body, name="flash_fwd", grid=(N_HEADS, T // tq),
        in_specs=[pl.BlockSpec((tq, P), lambda hh, i: (i, hh)), pl.BlockSpec((T, P), lambda hh, i: (0, hh)),
                  pl.BlockSpec((T, D_V), lambda hh, i: (0, hh))],
        out_specs=[pl.BlockSpec((tq, D_V), lambda hh, i: (i, hh)),
                   pl.BlockSpec((None, None, 1, tq), lambda hh, i: (hh, i, 0, 0))],
        out_shape=[jax.ShapeDtypeStruct((T, N_HEADS * D_V), BF16),
                   jax.ShapeDtypeStruct((N_HEADS, T // tq, 1, tq), F32)],
        scratch_shapes=[pltpu.VMEM((1, tq), F32), pltpu.VMEM((1, tq), F32), pltpu.VMEM((D_V, tq), F32),
                        pltpu.VMEM((t, tq), F32), pltpu.VMEM((t, tq), F32), pltpu.VMEM((t, tq), BF16),
                        pltpu.VMEM((t, tq), BF16), pltpu.VMEM((1, tq), F32), pltpu.VMEM((1, tq), F32)],
        compiler_params=_cparams("parallel", "arbitrary"),
    )(q, k, v)


def _attn_delta(o, do):
    T = o.shape[0]
    t = min(ATTN_TILE, T)

    def body(o_ref, do_ref, out_ref):
        ones = jnp.ones((8, D_V), BF16)
        for hh in range(N_HEADS):
            cols = slice(hh * D_V, (hh + 1) * D_V)
            hi, lo = _split_bf16(o_ref[:, cols].astype(F32) * do_ref[:, cols].astype(F32))
            out_ref[hh] = (_dot_nt(ones, hi) + _dot_nt(ones, lo))[0:1]

    tok = pl.BlockSpec((t, N_HEADS * D_V), lambda i: (i, 0))
    return pl.pallas_call(
        body, name="attn_delta", grid=(T // t,), in_specs=[tok, tok],
        out_specs=pl.BlockSpec((N_HEADS, None, 1, t), lambda i: (0, i, 0, 0)),
        out_shape=jax.ShapeDtypeStruct((N_HEADS, T // t, 1, t), F32),
        compiler_params=_cparams("parallel"),
    )(o, do)


def _flash_bwd(q, k, v, do, lse_row, delta_row):
    T = q.shape[0]
    t = min(ATTN_TILE, T // 2)
    tk = 2 * t
    nq = T // t
    P = D_HEAD_PAD

    def body(k_ref, v_ref, q_ref, do_ref, lse_ref, delta_ref, dqt_ref, dk_ref, dv_ref, dqt_sc, kt_sc,
             s0_sc, s1_sc, dp0_sc, dp1_sc, p0_sc, p1_sc, ds0_sc, ds1_sc):
        ki = pl.program_id(1)
        q0 = 2 * ki
        n = nq - q0
        s_sc, dp_sc, p_sc, ds_sc = (s0_sc, s1_sc), (dp0_sc, dp1_sc), (p0_sc, p1_sc), (ds0_sc, ds1_sc)
        kt_sc[...] = k_ref[...].astype(F32).T.astype(BF16)
        dk_ref[...] = jnp.zeros_like(dk_ref)
        dv_ref[...] = jnp.zeros_like(dv_ref)

        @pl.when(ki == 0)
        def _():
            dqt_sc[...] = jnp.zeros_like(dqt_sc)

        def rows_of(c):
            return pl.ds(pl.multiple_of((q0 + c) * t, t), t)

        def products(c, slot):
            s_sc[slot][...] = _dot_nt(k_ref[...], q_ref[rows_of(c), :])
            dp_sc[slot][...] = _dot_nt(v_ref[...], do_ref[rows_of(c), :])

        def elementwise(c, slot, query_offset):
            p_t = jnp.exp2(s_sc[slot][...] - lse_ref[q0 + c])
            if query_offset is not None:
                cols = lax.broadcasted_iota(jnp.int32, (tk, t), 1) + query_offset
                p_t = jnp.where(lax.broadcasted_iota(jnp.int32, (tk, t), 0) <= cols, p_t, 0.0)
            p_sc[slot][...] = p_t.astype(BF16)
            ds_sc[slot][...] = (p_t * (dp_sc[slot][...] - delta_ref[q0 + c])).astype(BF16)

        def gradients(c, slot):
            dv_ref[...] += _dot(p_sc[slot][...], do_ref[rows_of(c), :])
            ds_t = ds_sc[slot][...]
            dk_ref[...] += _dot(ds_t, q_ref[rows_of(c), :])
            dqt_sc[q0 + c] += _dot(kt_sc[...], ds_t)

        def stage(c, slot, first=False, last=False, query_offset=None):
            if not first:
                gradients(c - 1, 1 - slot)
            if not last:
                products(c + 1, 1 - slot)
            elementwise(c, slot, query_offset)

        products(0, 0)

        @pl.when(n == 2)
        def _():
            stage(0, 0, first=True, query_offset=0)
            stage(1, 1, last=True, query_offset=t)
            gradients(1, 1)

        @pl.when(n > 2)
        def _():
            stage(0, 0, first=True, query_offset=0)
            stage(1, 1, query_offset=t)

            def pair(j, carry):
                stage(2 + 2 * j, 0)
                stage(3 + 2 * j, 1)
                return carry

            lax.fori_loop(0, (n - 4) // 2, pair, 0)
            stage(n - 2, 0)
            stage(n - 1, 1, last=True)
            gradients(n - 1, 1)

        dqt_ref[:, :t] = dqt_sc[q0] * _ATTN_SCALE
        dqt_ref[:, t:] = dqt_sc[q0 + 1] * _ATTN_SCALE
        dk_ref[...] = dk_ref[...] * _LN_2

    kb = pl.BlockSpec((tk, P), lambda hh, i: (i, hh))
    vb = pl.BlockSpec((tk, D_V), lambda hh, i: (i, hh))
    stat = pl.BlockSpec((None, nq, 1, t), lambda hh, i: (hh, 0, 0, 0))
    return pl.pallas_call(
        body, name="flash_bwd", grid=(N_HEADS, T // tk),
        in_specs=[kb, vb,
                  pl.BlockSpec((T, P), lambda hh, i: (0, hh), pipeline_mode=pl.Buffered(1)),
                  pl.BlockSpec((T, D_V), lambda hh, i: (0, hh), pipeline_mode=pl.Buffered(1)),
                  stat, stat],
        out_specs=[pl.BlockSpec((P, tk), lambda hh, i: (hh, i)), kb, vb],
        out_shape=[jax.ShapeDtypeStruct((N_HEADS * P, T), F32), jax.ShapeDtypeStruct((T, N_HEADS * P), F32),
                   jax.ShapeDtypeStruct((T, N_HEADS * D_V), F32)],
        scratch_shapes=[pltpu.VMEM((nq, P, t), F32), pltpu.VMEM((P, tk), BF16)]
        + [pltpu.VMEM((tk, t), F32)] * 4 + [pltpu.VMEM((tk, t), BF16)] * 4,
        compiler_params=_cparams("arbitrary", "arbitrary"),
    )(k, v, q, do, lse_row, delta_row)


def _loss_head(h, g, target):
    T, D = h.shape
    tm = min(TOKEN_TILE, T)

    def body(h_ref, g_ref, t_ref, dh_ref, loss_ref, dg_ref):
        @pl.when(pl.program_id(0) == 0)
        def _():
            loss_ref[...] = jnp.zeros_like(loss_ref)
            dg_ref[...] = jnp.zeros_like(dg_ref)

        x = h_ref[...]
        err = _rms_fwd(x, g_ref[...]) - t_ref[...]
        per_tok = jnp.mean(err * err, axis=-1, keepdims=True)
        loss_ref[...] += 0.5 * jnp.sum(per_tok, axis=0, keepdims=True)
        dx, dg = _rms_bwd(err * (1.0 / D), x, g_ref[...])
        dh_ref[...] = dx
        dg_ref[...] += dg

    row = pl.BlockSpec((tm, D), lambda i: (i, 0))
    return pl.pallas_call(
        body, name="loss_head", grid=(T // tm,),
        in_specs=[row, _full((1, D)), row], out_specs=[row, _full((1, 128)), _full((1, D))],
        out_shape=[jax.ShapeDtypeStruct((T, D), F32), jax.ShapeDtypeStruct((1, 128), F32),
                   jax.ShapeDtypeStruct((1, D), F32)],
        compiler_params=_cparams("arbitrary"),
    )(h, g, target)


def _sum_parts(parts, tr, name):
    _, R, C = parts.shape

    def body(p_ref, o_ref):
        acc = p_ref[0].astype(F32)
        for j in range(1, N_DEV):
            acc = acc + p_ref[j].astype(F32)
        o_ref[...] = acc

    return pl.pallas_call(
        body, name=name, grid=(R // tr,),
        in_specs=[pl.BlockSpec((N_DEV, tr, C), lambda i: (0, i, 0))],
        out_specs=pl.BlockSpec((tr, C), lambda i: (i, 0)),
        out_shape=jax.ShapeDtypeStruct((R, C), F32),
        compiler_params=_cparams("parallel"),
    )(parts)


def _adamw(w, g, m, v):
    R, C = w.shape
    tr = _row_tile(R, TOKEN_TILE)

    def body(w_ref, g_ref, m_ref, v_ref, d_ref, mo_ref, vo_ref):
        gg = g_ref[...]
        mn = ADAM_B1 * m_ref[...] + (1.0 - ADAM_B1) * gg
        vn = ADAM_B2 * v_ref[...] + (1.0 - ADAM_B2) * (gg * gg)
        m_hat = mn / (1.0 - ADAM_B1 ** ADAM_STEP)
        v_hat = vn / (1.0 - ADAM_B2 ** ADAM_STEP)
        d_ref[...] = -ADAM_LR * (m_hat / (jnp.sqrt(v_hat) + ADAM_EPS) + ADAM_WD * w_ref[...])
        mo_ref[...] = mn
        vo_ref[...] = vn

    blk = pl.BlockSpec((tr, C), lambda i: (i, 0))
    return pl.pallas_call(
        body, name="adamw", grid=(R // tr,), in_specs=[blk] * 4, out_specs=[blk] * 3,
        out_shape=[jax.ShapeDtypeStruct((R, C), F32)] * 3,
        compiler_params=_cparams("parallel"),
    )(w, g, m, v)


def _adamw_nd(w, g, m, v):
    shape = w.shape
    two_d = (1, shape[0]) if len(shape) == 1 else (int(np.prod(shape[:-1])), shape[-1])
    outs = _adamw(w.reshape(two_d), g.reshape(two_d), m.reshape(two_d), v.reshape(two_d))
    return tuple(o.reshape(shape) for o in outs)


def _f32_as_bf16_pairs(a):
    return lax.bitcast_convert_type(a, BF16).reshape(a.shape[:-1] + (a.shape[-1] * 2,))


def _bf16_pairs_as_f32(a):
    return lax.bitcast_convert_type(a.reshape(a.shape[:-1] + (a.shape[-1] // 2, 2)), F32)


def _pack_misc(w_o, w_dq, w_uq, w_dkv, pool_w, pool_scale):
    lead = w_o.shape[:-3]
    rows = [w_o, w_dq, w_uq, w_dkv, pool_w]
    flat = [r.astype(BF16).reshape(lead + (-1, REP_COLS)) for r in rows]
    ps = _f32_as_bf16_pairs(pool_scale.astype(F32)).reshape(lead + (1, -1))
    ps = jnp.concatenate([ps, jnp.zeros(lead + (1, REP_COLS - ps.shape[-1]), BF16)], axis=-1)
    used = sum(f.shape[-2] for f in flat) + 1
    pad = jnp.zeros(lead + (MISC_ROWS - used, REP_COLS), BF16)
    return jnp.concatenate(flat + [ps, pad], axis=-2)


def _unpack_misc(buf, shapes):
    out, r0 = [], 0
    for shp in shapes[:-1]:
        n = int(np.prod(shp)) // REP_COLS
        out.append(buf[:, r0:r0 + n].reshape((N_DEV,) + shp))
        r0 += n
    n_ps = int(np.prod(shapes[-1]))
    out.append(_bf16_pairs_as_f32(buf[:, r0, :2 * n_ps]).reshape((N_DEV,) + shapes[-1]))
    return out


def _cat_dev(a, axis):
    a = jnp.moveaxis(a, 0, axis)
    return a.reshape(a.shape[:axis] + (a.shape[axis] * a.shape[axis + 1],) + a.shape[axis + 2:])


def _split_dev(a, axis):
    a = a.reshape(a.shape[:axis] + (N_DEV, a.shape[axis] // N_DEV) + a.shape[axis + 1:])
    return jnp.moveaxis(a, axis, 0)


def kernel(x, ffn_pre_norm, ffn_pre_wg, ffn_pre_wu, ffn_pre_wd, mix_norm, ffn_post_norm, ffn_post_wg, ffn_post_wu, ffn_post_wd, pool_w, pool_scale, kv_in_norm, w_dkv, ckv_norm, w_uk, w_uv, q_lora_norm, w_dq, w_uq, w_o, final_norm, loss_target, m_ffn_pre_norm, m_ffn_pre_wg, m_ffn_pre_wu, m_ffn_pre_wd, m_mix_norm, m_ffn_post_norm, m_ffn_post_wg, m_ffn_post_wu, m_ffn_post_wd, m_pool_w, m_pool_scale, m_kv_in_norm, m_w_dkv, m_ckv_norm, m_w_uk, m_w_uv, m_q_lora_norm, m_w_dq, m_w_uq, m_w_o, m_final_norm, v_ffn_pre_norm, v_ffn_pre_wg, v_ffn_pre_wu, v_ffn_pre_wd, v_mix_norm, v_ffn_post_norm, v_ffn_post_wg, v_ffn_post_wu, v_ffn_post_wd, v_pool_w, v_pool_scale, v_kv_in_norm, v_w_dkv, v_ckv_norm, v_w_uk, v_w_uv, v_q_lora_norm, v_w_dq, v_w_uq, v_w_o, v_final_norm):
    weights = dict(ffn_pre_norm=ffn_pre_norm, ffn_pre_wg=ffn_pre_wg, ffn_pre_wu=ffn_pre_wu, ffn_pre_wd=ffn_pre_wd,
                   mix_norm=mix_norm, ffn_post_norm=ffn_post_norm, ffn_post_wg=ffn_post_wg,
                   ffn_post_wu=ffn_post_wu, ffn_post_wd=ffn_post_wd, pool_w=pool_w, pool_scale=pool_scale,
                   kv_in_norm=kv_in_norm, w_dkv=w_dkv, ckv_norm=ckv_norm, w_uk=w_uk, w_uv=w_uv,
                   q_lora_norm=q_lora_norm, w_dq=w_dq, w_uq=w_uq, w_o=w_o, final_norm=final_norm)
    moments_m = dict(ffn_pre_norm=m_ffn_pre_norm, ffn_pre_wg=m_ffn_pre_wg, ffn_pre_wu=m_ffn_pre_wu,
                     ffn_pre_wd=m_ffn_pre_wd, mix_norm=m_mix_norm, ffn_post_norm=m_ffn_post_norm,
                     ffn_post_wg=m_ffn_post_wg, ffn_post_wu=m_ffn_post_wu, ffn_post_wd=m_ffn_post_wd,
                     pool_w=m_pool_w, pool_scale=m_pool_scale, kv_in_norm=m_kv_in_norm, w_dkv=m_w_dkv,
                     ckv_norm=m_ckv_norm, w_uk=m_w_uk, w_uv=m_w_uv, q_lora_norm=m_q_lora_norm, w_dq=m_w_dq,
                     w_uq=m_w_uq, w_o=m_w_o, final_norm=m_final_norm)
    moments_v = dict(ffn_pre_norm=v_ffn_pre_norm, ffn_pre_wg=v_ffn_pre_wg, ffn_pre_wu=v_ffn_pre_wu,
                     ffn_pre_wd=v_ffn_pre_wd, mix_norm=v_mix_norm, ffn_post_norm=v_ffn_post_norm,
                     ffn_post_wg=v_ffn_post_wg, ffn_post_wu=v_ffn_post_wu, ffn_post_wd=v_ffn_post_wd,
                     pool_w=v_pool_w, pool_scale=v_pool_scale, kv_in_norm=v_kv_in_norm, w_dkv=v_w_dkv,
                     ckv_norm=v_ckv_norm, w_uk=v_w_uk, w_uv=v_w_uv, q_lora_norm=v_q_lora_norm, w_dq=v_w_dq,
                     w_uq=v_w_uq, w_o=v_w_o, final_norm=v_final_norm)
    order = list(weights)

    T, D = x.shape[1], x.shape[2]
    depth = ffn_pre_norm.shape[0]
    n_a = pool_w.shape[0]
    n_b = depth - n_a
    fs = ffn_pre_wd.shape[1]
    F = fs * N_DEV
    n_ffn = 2 * depth
    t_attn = min(ATTN_TILE, T)

    ffn_local = [
        jnp.stack([jnp.swapaxes(wg[l], 0, 1), jnp.swapaxes(wu[l], 0, 1), wd[l]]).astype(BF16)
        for l in range(depth)
        for wg, wu, wd in ((ffn_pre_wg, ffn_pre_wu, ffn_pre_wd), (ffn_post_wg, ffn_post_wu, ffn_post_wd))
    ]
    misc_local = _pack_misc(w_o, w_dq, w_uq.reshape(n_b, w_uq.shape[1], -1), w_dkv, pool_w, pool_scale)
    misc_shapes = (w_o.shape, w_dq.shape, (n_b, w_uq.shape[1], N_HEADS * D_QK), w_dkv.shape, pool_w.shape,
                   pool_scale.shape)
    (w0_all,) = _exchange([(ffn_local[0], "gather_mid")], "comm_all_gather")
    walls = [w0_all.reshape(3, F, D)] + [None] * (n_ffn - 1)

    def vec(a):
        return a.reshape(1, -1)

    def ffn_stage(e, h_in, norm):
        carry = [(ffn_local[e + 1], "gather_mid")] if e + 1 < n_ffn else []
        if e == 0:
            carry.append((misc_local, "to_all"))
        outs = _ffn_fwd(h_in, norm, walls[e], carry)
        if carry:
            walls[e + 1] = outs[3].reshape(3, F, D)
        return outs

    h = x.reshape(T, D)
    stage0 = ffn_stage(0, h, vec(ffn_pre_norm[0]))
    misc_all = stage0[4]
    o_blk, dq_blk, uq_blk, dkv_blk, pw_blk, ps_blk = _unpack_misc(misc_all, misc_shapes)
    w_o_f = _cat_dev(o_blk, 1)
    w_dq_f = _cat_dev(dq_blk, 1)
    w_uq_f = _cat_dev(uq_blk, 1).reshape(n_b, -1, N_HEADS, D_QK)
    w_dkv_f = _cat_dev(dkv_blk, 0)
    pool_w_f = _cat_dev(pw_blk, 2)
    pool_scale_f = _cat_dev(ps_blk, 1)
    rq = w_dq_f.shape[2]
    wqa, wqb = _rope_weight_pair(w_uq_f)
    wqa = wqa.reshape(n_b, rq, N_HEADS * D_HEAD_PAD)
    wqb = wqb.reshape(n_b, rq, N_HEADS * D_HEAD_PAD)
    wka, wkb = _rope_weight_pair(w_dkv_f)
    wuk = jnp.concatenate([w_uk, jnp.zeros_like(w_uk)], axis=-1).astype(BF16).reshape(D_NOPE, N_HEADS * D_HEAD_PAD)
    wuv = w_uv.astype(BF16).reshape(D_NOPE, N_HEADS * D_V)
    ca_q, ca_k, sb = _rope_tables(T)
    ca_q_scaled, sb_scaled = ca_q * _ATTN_SCALE_LOG2, sb * _ATTN_SCALE_LOG2

    saved = []
    k_all = v_all = craw = h_kv = None
    for l in range(depth):
        s = {"h0": h}
        h, s["g1"], s["u1"] = (stage0 if l == 0 else ffn_stage(2 * l, h, vec(ffn_pre_norm[l])))[:3]
        s["h1"] = h
        if l < n_a:
            h, s["y"] = _pool_fwd(h, vec(mix_norm[l]), pool_w_f[l], vec(pool_scale_f[l]))
        else:
            j = l - n_a
            s["cq"] = _mm_rows(h, w_dq_f[j], nt=False, out_dtype=F32, norm_g=vec(mix_norm[l]), name="q_down")
            s["q"] = _q_proj(s["cq"], vec(q_lora_norm[j]), wqa[j], wqb[j], ca_q_scaled, sb_scaled)
            s["o"], lse = _flash_fwd(s["q"], k_all, v_all)
            s["lse"] = lse.reshape(N_HEADS, T // t_attn, 1, t_attn)
            h = _mm_rows(s["o"], w_o_f[j], nt=False, out_dtype=F32, res=h, name="attn_out")
        s["h2"] = h
        h, s["g2"], s["u2"] = ffn_stage(2 * l + 1, h, vec(ffn_post_norm[l]))[:3]
        if l == n_a - 1:
            h_kv = h
            k_all, v_all, craw = _kv_proj(h, vec(kv_in_norm), wka, wkb, vec(ckv_norm), wuk, wuv, ca_k, sb)
        saved.append(s)

    dh, loss_part, d_final = _loss_head(h, vec(final_norm), loss_target.reshape(T, D))

    slabs, ffn_parts = [None] * n_ffn, [None] * n_ffn

    misc_parts = []
    big_rep_names = ["w_uk", "w_uv"]

    def packed_misc_grads():
        return _pack_misc(_split_dev(jnp.stack(d_wo), 1), _split_dev(jnp.stack(d_wdq), 1),
                          _split_dev(jnp.stack(d_wuq).reshape(n_b, rq, -1), 1), _split_dev(grads["w_dkv"], 0),
                          _split_dev(jnp.stack(d_pool_w), 2),
                          _split_dev(jnp.concatenate(d_pool_scale, axis=0), 1))

    def ffn_stage_bwd(e, dh_out, h_in, norm, gate, up):
        carry = [(slabs[e + 1].reshape(3, N_DEV, fs, D), "scatter_mid")] if e + 1 < n_ffn else []
        if e == 0:
            carry.append((packed_misc_grads(), "scatter_lead"))
        if e == 2 * n_a - 1:
            carry.append((jnp.concatenate([grads[n].reshape(-1, REP_COLS) for n in big_rep_names]), "to_all"))
        outs = _ffn_bwd_dx(dh_out, h_in, norm, gate, up, walls[e], carry)
        if e + 1 < n_ffn:
            ffn_parts[e + 1] = outs[6]
        if e == 0:
            misc_parts.insert(0, outs[7])
        if e == 2 * n_a - 1:
            misc_parts.append(outs[6 + len(carry) - 1])
        dh_in, dgt, dup, u_t, dy_t, dnorm = outs[:6]
        slabs[e] = _ffn_bwd_dw(dgt, dup, gate, up, u_t, dy_t)
        return dh_in, dnorm

    grads = {}
    d_pre, d_post, d_mix = [None] * depth, [None] * depth, [None] * depth
    d_pool_w, d_pool_scale = [None] * n_a, [None] * n_a
    d_qln, d_wdq, d_wuq, d_wo = [None] * n_b, [None] * n_b, [None] * n_b, [None] * n_b
    dks, dvs = [], []
    for l in reversed(range(depth)):
        s = saved[l]
        if l == n_a - 1:
            (dh, grads["kv_in_norm"], dwka, dwkb, grads["ckv_norm"], dwuk, dwuv) = _kv_proj_bwd(
                dks, dvs, dh, h_kv, vec(kv_in_norm), wka, wkb, craw, vec(ckv_norm), wuk, wuv, ca_k, sb)
            grads["w_dkv"] = _rope_weight_pair_grad(dwka, dwkb)
            grads["w_uk"] = dwuk.reshape(D_NOPE, N_HEADS, D_HEAD_PAD)[..., :D_NOPE]
            grads["w_uv"] = dwuv.reshape(D_NOPE, N_HEADS, D_V)
        dh, d_post[l] = ffn_stage_bwd(2 * l + 1, dh, s["h2"], vec(ffn_post_norm[l]), s["g2"], s["u2"])
        if l < n_a:
            dh, d_mix[l], d_pool_w[l], d_pool_scale[l] = _pool_bwd(
                dh, s["h1"], vec(mix_norm[l]), s["y"], pool_w_f[l], vec(pool_scale_f[l]))
        else:
            j = l - n_a
            d_wo[j] = _mm_tn(s["o"], dh, name="attn_out_dw")
            do = _mm_rows(dh, w_o_f[j], nt=True, out_dtype=BF16, name="attn_out_dx")
            delta_row = _attn_delta(s["o"], do)
            dq_t, dk_l, dv_l = _flash_bwd(s["q"], k_all, v_all, do, s["lse"], delta_row)
            dks.append(dk_l)
            dvs.append(dv_l)
            dcq, d_qln[j], dwa, dwb = _q_proj_bwd(dq_t, s["cq"], vec(q_lora_norm[j]), wqa[j], wqb[j], ca_q, sb)
            d_wuq[j] = _rope_weight_pair_grad(dwa.reshape(rq, N_HEADS, D_HEAD_PAD),
                                              dwb.reshape(rq, N_HEADS, D_HEAD_PAD))
            d_wdq[j] = _mm_tn(s["h1"], dcq, norm_g=vec(mix_norm[l]), name="q_down_dw")
            dh, d_mix[l] = _proj_bwd(dcq, w_dq_f[j], s["h1"], vec(mix_norm[l]), dh, "q_down_dx")
        dh, d_pre[l] = ffn_stage_bwd(2 * l, dh, s["h0"], vec(ffn_pre_norm[l]), s["g1"], s["u1"])
    grad_x = dh.reshape(x.shape)

    rep_names = ["ffn_pre_norm", "mix_norm", "ffn_post_norm", "kv_in_norm", "ckv_norm", "q_lora_norm", "final_norm"]
    grads["ffn_pre_norm"] = jnp.concatenate(d_pre, axis=0)
    grads["mix_norm"] = jnp.concatenate(d_mix, axis=0)
    grads["ffn_post_norm"] = jnp.concatenate(d_post, axis=0)
    grads["q_lora_norm"] = jnp.concatenate(d_qln, axis=0)
    grads["final_norm"] = d_final
    rep_flat = jnp.concatenate([grads[n].reshape(-1) for n in rep_names] + [loss_part[0, :1]])
    n_rep = rep_flat.shape[0]
    rep_rows = -(-n_rep // (8 * REP_COLS)) * 8
    rep_g = jnp.concatenate([rep_flat, jnp.zeros((rep_rows * REP_COLS - n_rep,), F32)]).reshape(rep_rows, REP_COLS)
    ffn_parts[0], rep_parts = _exchange([(slabs[0].reshape(3, N_DEV, fs, D), "scatter_mid"), (rep_g, "to_all")],
                                        "comm_grad_exchange")
    ffn_sum = jnp.stack([_sum_parts(p.reshape(N_DEV, 3 * fs, D), fs, "sum_ffn").reshape(3, fs, D)
                         for p in ffn_parts])
    misc_sum_parts = _unpack_misc(misc_parts[0], misc_shapes)
    rep_sum = _sum_parts(rep_parts, _row_tile(rep_rows, 128), "sum_rep").reshape(-1)

    def sum_small(p):
        shp = p.shape[1:]
        two_d = (int(np.prod(shp[:-1])), shp[-1])
        return _sum_parts(p.reshape((N_DEV,) + two_d), two_d[0], "sum_misc").reshape(shp)

    g_wo, g_wdq, g_wuq, g_wdkv, g_pw, g_ps = [sum_small(p) for p in misc_sum_parts]
    grads.update(w_o=g_wo, w_dq=g_wdq, w_uq=g_wuq.reshape(w_uq.shape), w_dkv=g_wdkv, pool_w=g_pw, pool_scale=g_ps)
    for kind, (npre, npost) in enumerate((("ffn_pre_wg", "ffn_post_wg"), ("ffn_pre_wu", "ffn_post_wu"),
                                          ("ffn_pre_wd", "ffn_post_wd"))):
        pre = ffn_sum[0::2, kind]
        post = ffn_sum[1::2, kind]
        if kind < 2:
            pre, post = jnp.swapaxes(pre, 1, 2), jnp.swapaxes(post, 1, 2)
        grads[npre], grads[npost] = pre, post
    off = 0
    for n in rep_names:
        size = int(np.prod(weights[n].shape))
        grads[n] = rep_sum[off:off + size].reshape(weights[n].shape)
        off += size
    loss = rep_sum[off]
    big_rep_sum = _sum_parts(misc_parts[1], _row_tile(misc_parts[1].shape[1], 64), "sum_rep_big")
    off = 0
    for n in big_rep_names:
        rows = int(np.prod(weights[n].shape)) // REP_COLS
        grads[n] = big_rep_sum[off:off + rows].reshape(weights[n].shape)
        off += rows

    deltas, new_m, new_v = {}, {}, {}
    for n in order:
        deltas[n], new_m[n], new_v[n] = _adamw_nd(weights[n], grads[n], moments_m[n], moments_v[n])
    return (loss, grad_x, *[grads[n] for n in order], *[deltas[n] for n in order],
            *[new_m[n] for n in order], *[new_v[n] for n in order])
```

```python
import numpy as np
import jax
import jax.numpy as jnp
from jax import lax
from jax.experimental import pallas as pl
from jax.experimental.pallas import tpu as pltpu

F32, BF16 = jnp.float32, jnp.bfloat16
N_DEV = 8
RMS_EPS = 1e-6
N_HEADS = 16
D_NOPE, D_ROPE, D_V = 128, 64, 128
D_QK = D_NOPE + D_ROPE
D_HEAD_PAD = 256
HEAD_GROUP = 4
ROPE_THETA = 10000.0
POOL_WINDOWS = (2, 4, 8, 16)
POOL_HALO = 16
ADAM_LR, ADAM_B1, ADAM_B2, ADAM_EPS, ADAM_WD, ADAM_STEP = 0.001, 0.9, 0.999, 1e-08, 0.01, 10
NEG_BIG = -1e30
V7X_VMEM_LIMIT = 56 * 1024 * 1024
V7X_MXU_DIM = 256
TOKEN_TILE = 512
ATTN_TILE = 512
MISC_ROWS = 864
REP_COLS = 1024


def _cparams(*sem):
    return pltpu.CompilerParams(dimension_semantics=sem, vmem_limit_bytes=V7X_VMEM_LIMIT)


def _dot(a, b):
    return lax.dot_general(a, b, (((1,), (0,)), ((), ())), preferred_element_type=F32)


def _dot_nt(a, b):
    return lax.dot_general(a, b, (((1,), (1,)), ((), ())), preferred_element_type=F32)


def _dot_tn(a, b):
    return lax.dot_general(a, b, (((0,), (0,)), ((), ())), preferred_element_type=F32)


def _rms_fwd(x, g):
    r = lax.rsqrt(jnp.mean(x * x, axis=-1, keepdims=True) + RMS_EPS)
    return (x * r) * g


def _rms_bwd(du, x, g):
    r = lax.rsqrt(jnp.mean(x * x, axis=-1, keepdims=True) + RMS_EPS)
    xh = x * r
    dg = jnp.sum(du * xh, axis=0, keepdims=True)
    dxh = du * g
    dx = r * (dxh - xh * jnp.mean(dxh * xh, axis=-1, keepdims=True))
    return dx, dg


def _sigmoid(x):
    return 1.0 / (1.0 + jnp.exp(-x))


def _split_bf16(x):
    hi = x.astype(BF16)
    lo = (x - hi.astype(F32)).astype(BF16)
    return hi, lo


def _full(shape):
    return pl.BlockSpec(shape, lambda *_: (0,) * len(shape))


def _resident(shape):
    return pl.BlockSpec(shape, lambda *_: (0,) * len(shape), pipeline_mode=pl.Buffered(1))


def _row_tile(rows, cap):
    for t in range(min(cap, rows) // 8 * 8, 0, -8):
        if rows % t == 0:
            return t
    return rows


def _peers():
    x, y, c = lax.axis_index("x"), lax.axis_index("y"), lax.axis_index("c")
    out = []
    for k in range(1, N_DEV):
        px = 1 - x if (k >> 2) & 1 else x
        py = 1 - y if (k >> 1) & 1 else y
        pc = 1 - c if k & 1 else c
        out.append(((px, py, pc), 4 * px + 2 * py + pc))
    return 4 * x + 2 * y + c, out


_ROUTES = {
    "gather_mid": (lambda ref, idx: ref, lambda ref, idx: ref.at[:, idx], lambda s: s[:1] + (N_DEV,) + s[1:]),
    "to_all": (lambda ref, idx: ref, lambda ref, idx: ref.at[idx], lambda s: (N_DEV,) + s),
    "scatter_mid": (lambda ref, idx: ref.at[:, idx], lambda ref, idx: ref.at[idx],
                    lambda s: (N_DEV, s[0]) + s[2:]),
    "scatter_lead": (lambda ref, idx: ref.at[idx], lambda ref, idx: ref.at[idx], lambda s: s),
}


def _route_fns(items):
    kinds = [kind for _, kind in items]
    return (lambda j, ref, idx: _ROUTES[kinds[j]][0](ref, idx)), (lambda j, ref, idx: _ROUTES[kinds[j]][1](ref, idx))


def _route_out_shapes(items):
    return [jax.ShapeDtypeStruct(_ROUTES[kind][2](arr.shape), arr.dtype) for arr, kind in items]


def _exchange(items, name):
    n = len(items)
    fns = _route_fns(items)

    def body(*refs):
        ins, outs, sems = refs[:n], refs[n:2 * n], refs[2 * n:]
        _exchange_start(ins, outs, sems, *fns)
        _exchange_wait(ins, outs, sems, *fns)

    any_spec = pl.BlockSpec(memory_space=pl.ANY)
    return pl.pallas_call(
        body, name=name, out_shape=_route_out_shapes(items),
        in_specs=[any_spec] * n, out_specs=[any_spec] * n,
        scratch_shapes=_exchange_sems(n),
    )(*[arr for arr, _ in items])


def _exchange_sems(n):
    return [pltpu.SemaphoreType.DMA((n, N_DEV - 1)), pltpu.SemaphoreType.DMA((n, N_DEV - 1)),
            pltpu.SemaphoreType.DMA((n,))]


def _own_copies(ins, outs, sems, src_of, dst_of):
    me, _ = _peers()
    return [pltpu.make_async_copy(src_of(j, ins[j], me), dst_of(j, outs[j], me), sems[2].at[j])
            for j in range(len(ins))]


def _remote_copies(ins, outs, sems, src_of, dst_of, receiving):
    me, peers = _peers()
    return [pltpu.make_async_remote_copy(
        src_ref=src_of(j, ins[j], pidx), dst_ref=dst_of(j, outs[j], pidx if receiving else me),
        send_sem=sems[0].at[j, k], recv_sem=sems[1].at[j, k],
        device_id=peer, device_id_type=pl.DeviceIdType.MESH)
        for k, (peer, pidx) in enumerate(peers) for j in range(len(ins))]


def _exchange_start(ins, outs, sems, src_of, dst_of):
    for cp in _own_copies(ins, outs, sems, src_of, dst_of):
        cp.start()
    for cp in _remote_copies(ins, outs, sems, src_of, dst_of, receiving=False):
        cp.start()


def _exchange_wait(ins, outs, sems, src_of, dst_of):
    for cp in _remote_copies(ins, outs, sems, src_of, dst_of, receiving=True):
        cp.wait_recv()
    for cp in _remote_copies(ins, outs, sems, src_of, dst_of, receiving=False):
        cp.wait_send()
    for cp in _own_copies(ins, outs, sems, src_of, dst_of):
        cp.wait()


def _f_parts(F):
    first = -(-(F // V7X_MXU_DIM) // 2) * V7X_MXU_DIM
    return (slice(0, first), slice(first, F))


def _carried(items):
    if not items:
        return [], [], [], []
    return ([arr for arr, _ in items], [pl.BlockSpec(memory_space=pl.ANY)] * len(items), _route_out_shapes(items),
            _exchange_sems(len(items)))


def _ffn_fwd(h, g, w3, carry=()):
    T, D = h.shape
    F = w3.shape[1]
    tm = min(TOKEN_TILE, T)
    nt = T // tm
    nc = len(carry)
    has_c = nc > 0
    c_in, c_specs, c_out, c_sems = _carried(carry)

    def body(*refs):
        h_ref, g_ref, w_ref = refs[:3]
        ho_ref, gate_ref, up_ref = refs[3 + nc:6 + nc]
        i = pl.program_id(0)
        if has_c:
            comm = (refs[3:3 + nc], refs[6 + nc:6 + 2 * nc], refs[6 + 2 * nc:], *_route_fns(carry))

            @pl.when(i == 0)
            def _():
                _exchange_start(*comm)

        x = h_ref[...]
        u = _rms_fwd(x, g_ref[...]).astype(BF16)
        acc = None
        for cols in _f_parts(F):
            gate = _dot_nt(u, w_ref[0, cols, :])
            up = _dot_nt(u, w_ref[1, cols, :])
            gate_ref[:, cols] = gate.astype(BF16)
            up_ref[:, cols] = up.astype(BF16)
            part = _dot((gate * _sigmoid(gate) * up).astype(BF16), w_ref[2, cols, :])
            acc = part if acc is None else acc + part
        ho_ref[...] = x + 0.5 * acc

        if has_c:
            @pl.when(i == nt - 1)
            def _():
                _exchange_wait(*comm)

    row = pl.BlockSpec((tm, D), lambda i: (i, 0))
    wide = pl.BlockSpec((tm, F), lambda i: (i, 0))
    return pl.pallas_call(
        body, name="ffn_fwd_gather" if has_c else "ffn_fwd", grid=(nt,),
        in_specs=[row, _full((1, D)), _resident(w3.shape)] + c_specs,
        out_specs=[row, wide, wide] + c_specs,
        out_shape=[jax.ShapeDtypeStruct((T, D), F32), jax.ShapeDtypeStruct((T, F), BF16),
                   jax.ShapeDtypeStruct((T, F), BF16)] + c_out,
        scratch_shapes=c_sems,
        compiler_params=_cparams("arbitrary"),
    )(h, g, w3, *c_in)


def _ffn_bwd_dx(dho, h, g, gate, up, w3, carry=()):
    T, D = h.shape
    F = w3.shape[1]
    tm = min(TOKEN_TILE // 2, T)
    nt = T // tm
    nc = len(carry)
    has_c = nc > 0
    c_in, c_specs, c_out, c_sems = _carried(carry)

    def body(*refs):
        dho_ref, h_ref, g_ref, gate_ref, up_ref, w_ref = refs[:6]
        dhi_ref, dgate_ref, dup_ref, u_ref, dy_ref, dg_ref = refs[6 + nc:12 + nc]
        i = pl.program_id(0)
        if has_c:
            comm = (refs[6:6 + nc], refs[12 + nc:12 + 2 * nc], refs[12 + 2 * nc:], *_route_fns(carry))

            @pl.when(i == 0)
            def _():
                _exchange_start(*comm)

        dho = dho_ref[...]
        x = h_ref[...]
        dy_f = 0.5 * dho
        dy = dy_f.astype(BF16)
        dy_ref[...] = dy_f.T.astype(BF16)
        u_ref[...] = _rms_fwd(x, g_ref[...]).T.astype(BF16)
        acc = None
        for cols in _f_parts(F):
            dact = _dot_nt(dy, w_ref[2, cols, :])
            gt = gate_ref[:, cols].astype(F32)
            sig = _sigmoid(gt)
            dup = (dact * (gt * sig)).astype(BF16)
            dgate = (dact * up_ref[:, cols].astype(F32) * (sig * (1.0 + gt * (1.0 - sig)))).astype(BF16)
            dup_ref[:, cols] = dup
            dgate_ref[:, cols] = dgate
            part = _dot(dgate, w_ref[0, cols, :]) + _dot(dup, w_ref[1, cols, :])
            acc = part if acc is None else acc + part
        dx, dg = _rms_bwd(acc, x, g_ref[...])
        dhi_ref[...] = dho + dx

        @pl.when(i == 0)
        def _():
            dg_ref[...] = jnp.zeros_like(dg_ref)

        dg_ref[...] += dg

        if has_c:
            @pl.when(i == nt - 1)
            def _():
                _exchange_wait(*comm)

    row = pl.BlockSpec((tm, D), lambda i: (i, 0))
    col = pl.BlockSpec((D, tm), lambda i: (0, i))
    wide = pl.BlockSpec((tm, F), lambda i: (i, 0))
    return pl.pallas_call(
        body, name="ffn_bwd_dx_scatter" if has_c else "ffn_bwd_dx", grid=(nt,),
        in_specs=[row, row, _full((1, D)), wide, wide, _resident(w3.shape)] + c_specs,
        out_specs=[row, wide, wide, col, col, _full((1, D))] + c_specs,
        out_shape=[jax.ShapeDtypeStruct((T, D), F32), jax.ShapeDtypeStruct((T, F), BF16),
                   jax.ShapeDtypeStruct((T, F), BF16), jax.ShapeDtypeStruct((D, T), BF16),
                   jax.ShapeDtypeStruct((D, T), BF16), jax.ShapeDtypeStruct((1, D), F32)] + c_out,
        scratch_shapes=c_sems,
        compiler_params=_cparams("arbitrary"),
    )(dho, h, g, gate, up, w3, *c_in)


def _ffn_bwd_dw(dgate, dup, gate, up, u_t, dy_t):
    T, F = gate.shape
    D = u_t.shape[0]
    tfw = F // 2
    tk = min(TOKEN_TILE, T)
    nk = T // tk

    def body(dgate_ref, dup_ref, gate_ref, up_ref, ut_ref, dyt_ref, out_ref, acc_sc):
        k = pl.program_id(1)

        @pl.when(k == 0)
        def _():
            acc_sc[...] = jnp.zeros_like(acc_sc)

        uu = ut_ref[...]
        acc_sc[0] += _dot(uu, dgate_ref[...])
        acc_sc[1] += _dot(uu, dup_ref[...])
        gt = gate_ref[...].astype(F32)
        act = (gt * _sigmoid(gt) * up_ref[...].astype(F32)).astype(BF16)
        acc_sc[2] += _dot(dyt_ref[...], act)

        @pl.when(k == nk - 1)
        def _():
            for kind in range(3):
                out_ref[kind] = acc_sc[kind].T.astype(BF16)

    blk = pl.BlockSpec((tk, tfw), lambda j, k: (k, j))
    col = pl.BlockSpec((D, tk), lambda j, k: (0, k))
    return pl.pallas_call(
        body, name="ffn_bwd_dw", grid=(F // tfw, nk),
        in_specs=[blk, blk, blk, blk, col, col],
        out_specs=pl.BlockSpec((3, tfw, D), lambda j, k: (0, j, 0)),
        out_shape=jax.ShapeDtypeStruct((3, F, D), BF16),
        scratch_shapes=[pltpu.VMEM((3, D, tfw), F32)],
        compiler_params=_cparams("parallel", "arbitrary"),
    )(dgate, dup, gate, up, u_t, dy_t)


def _mm_rows(a, b, *, nt, out_dtype, norm_g=None, res=None, name):
    T, K = a.shape
    N = b.shape[0] if nt else b.shape[1]
    tm = min(TOKEN_TILE, T)
    has_g, has_r = norm_g is not None, res is not None

    def body(*refs):
        a_ref, b_ref = refs[0], refs[1]
        o_ref = refs[-1]
        x = a_ref[...]
        if has_g:
            x = _rms_fwd(x, refs[2][...])
        x = x.astype(BF16)
        acc = _dot_nt(x, b_ref[...]) if nt else _dot(x, b_ref[...])
        if has_r:
            acc = refs[2 + has_g][...] + acc
        o_ref[...] = acc.astype(out_dtype)

    ins, specs = [a, b], [pl.BlockSpec((tm, K), lambda i: (i, 0)), _full(b.shape)]
    if has_g:
        ins.append(norm_g)
        specs.append(_full((1, K)))
    if has_r:
        ins.append(res)
        specs.append(pl.BlockSpec((tm, N), lambda i: (i, 0)))
    return pl.pallas_call(
        body, name=name, grid=(T // tm,), in_specs=specs,
        out_specs=pl.BlockSpec((tm, N), lambda i: (i, 0)),
        out_shape=jax.ShapeDtypeStruct((T, N), out_dtype),
        compiler_params=_cparams("parallel"),
    )(*ins)


def _mm_tn(a, b, *, norm_g=None, name):
    T, M = a.shape
    N = b.shape[1]
    tk = min(TOKEN_TILE, T)
    has_g = norm_g is not None

    def body(*refs):
        a_ref, b_ref, o_ref = refs[0], refs[1], refs[-1]

        @pl.when(pl.program_id(0) == 0)
        def _():
            o_ref[...] = jnp.zeros_like(o_ref)

        x = a_ref[...]
        if has_g:
            x = _rms_fwd(x, refs[2][...])
        o_ref[...] += _dot_tn(x.astype(BF16), b_ref[...].astype(BF16))

    ins = [a, b]
    specs = [pl.BlockSpec((tk, M), lambda k: (k, 0)), pl.BlockSpec((tk, N), lambda k: (k, 0))]
    if has_g:
        ins.append(norm_g)
        specs.append(_full((1, M)))
    return pl.pallas_call(
        body, name=name, grid=(T // tk,), in_specs=specs, out_specs=_full((M, N)),
        out_shape=jax.ShapeDtypeStruct((M, N), F32),
        compiler_params=_cparams("arbitrary"),
    )(*ins)


def _proj_bwd(dz, w, h, g, dh, name):
    T, D = h.shape
    N = w.shape[1]
    tm = min(TOKEN_TILE, T)

    def body(dz_ref, w_ref, h_ref, g_ref, dh_ref, o_ref, dg_ref):
        du = _dot_nt(dz_ref[...].astype(BF16), w_ref[...])
        dx, dg = _rms_bwd(du, h_ref[...], g_ref[...])
        o_ref[...] = dh_ref[...] + dx

        @pl.when(pl.program_id(0) == 0)
        def _():
            dg_ref[...] = jnp.zeros_like(dg_ref)

        dg_ref[...] += dg

    row = pl.BlockSpec((tm, D), lambda i: (i, 0))
    return pl.pallas_call(
        body, name=name, grid=(T // tm,),
        in_specs=[pl.BlockSpec((tm, N), lambda i: (i, 0)), _full((D, N)), row, _full((1, D)), row],
        out_specs=[row, _full((1, D))],
        out_shape=[jax.ShapeDtypeStruct((T, D), F32), jax.ShapeDtypeStruct((1, D), F32)],
        compiler_params=_cparams("arbitrary"),
    )(dz, w, h, g, dh)


def _pool_bands(tm):
    r = np.arange(tm)[:, None]
    c = np.arange(tm)[None, :]
    j = np.arange(POOL_HALO)[None, :]
    main, halo, main_t, halo_t = [], [], [], []
    for w in POOL_WINDOWS:
        main.append(((r - c >= 0) & (r - c < w)) / w)
        halo.append((r + POOL_HALO - j < w) / w)
        main_t.append(((c - r >= 0) & (c - r < w)) / w)
        halo_t.append((tm + j - r < w) / w)
    return tuple(jnp.asarray(np.stack(m), BF16) for m in (main, halo, main_t, halo_t))


def _pool_count_scale(i, tm, w):
    t = i * tm + lax.broadcasted_iota(jnp.int32, (tm, 1), 0)
    return w / jnp.minimum(t + 1, w).astype(F32)


def _pool_fwd(h, g, wp, scale):
    T, D = h.shape
    G, dg = len(POOL_WINDOWS), D // len(POOL_WINDOWS)
    tm = min(TOKEN_TILE // 2, T)
    hb = tm // POOL_HALO
    bm, bh, _, _ = _pool_bands(tm)

    def body(h_ref, hh_ref, g_ref, wp_ref, sc_ref, bm_ref, bh_ref, ho_ref, y_ref):
        i = pl.program_id(0)
        x = h_ref[...]
        u = _rms_fwd(x, g_ref[...])
        uh = _rms_fwd(hh_ref[...], g_ref[...]) * (i > 0).astype(F32)
        for gi, w in enumerate(POOL_WINDOWS):
            cols = slice(gi * dg, (gi + 1) * dg)
            ug = u[:, cols]
            hi, lo = _split_bf16(ug)
            hhi, hlo = _split_bf16(uh[:, cols])
            s = (_dot(bm_ref[gi], hi) + _dot(bm_ref[gi], lo)
                 + _dot(bh_ref[gi], hhi) + _dot(bh_ref[gi], hlo))
            y = (s * _pool_count_scale(i, tm, w) - ug).astype(BF16)
            y_ref[:, cols] = y
            ho_ref[:, cols] = x[:, cols] + _dot(y, wp_ref[gi]) * sc_ref[:, cols]

    row = pl.BlockSpec((tm, D), lambda i: (i, 0))
    return pl.pallas_call(
        body, name="pool_fwd", grid=(T // tm,),
        in_specs=[row, pl.BlockSpec((POOL_HALO, D), lambda i: (jnp.maximum(i * hb - 1, 0), 0)),
                  _full((1, D)), _full((G, dg, dg)), _full((1, D)),
                  _full((G, tm, tm)), _full((G, tm, POOL_HALO))],
        out_specs=[row, row],
        out_shape=[jax.ShapeDtypeStruct((T, D), F32), jax.ShapeDtypeStruct((T, D), BF16)],
        compiler_params=_cparams("parallel"),
    )(h, h, g, wp, scale, bm, bh)


def _pool_bwd(dh, h, g, y, wp, scale):
    T, D = h.shape
    G, dg = len(POOL_WINDOWS), D // len(POOL_WINDOWS)
    tm = min(TOKEN_TILE // 2, T)
    hb = tm // POOL_HALO
    nt = T // tm
    _, _, bmt, bht = _pool_bands(tm)

    def body(dh_ref, dhn_ref, h_ref, g_ref, y_ref, wp_ref, sc_ref, bmt_ref, bht_ref,
             o_ref, dg_ref, dwp_ref, dsc_ref, du_sc):
        i = pl.program_id(0)

        @pl.when(i == 0)
        def _():
            dg_ref[...] = jnp.zeros_like(dg_ref)
            dwp_ref[...] = jnp.zeros_like(dwp_ref)
            dsc_ref[...] = jnp.zeros_like(dsc_ref)

        dho = dh_ref[...]
        dz = dho * sc_ref[...]
        dzn = dhn_ref[...] * sc_ref[...] * (i < nt - 1).astype(F32)
        for gi, w in enumerate(POOL_WINDOWS):
            cols = slice(gi * dg, (gi + 1) * dg)
            yg = y_ref[:, cols]
            dzg = dz[:, cols].astype(BF16)
            dsc_ref[:, cols] += jnp.sum(dho[:, cols] * _dot(yg, wp_ref[gi]), axis=0, keepdims=True)
            dwp_ref[gi] += _dot_tn(yg, dzg)
            dy = _dot_nt(dzg, wp_ref[gi])
            dyn = _dot_nt(dzn[:, cols].astype(BF16), wp_ref[gi])
            hi, lo = _split_bf16(dy * _pool_count_scale(i, tm, w))
            nhi, nlo = _split_bf16(dyn)
            du_sc[:, cols] = (_dot(bmt_ref[gi], hi) + _dot(bmt_ref[gi], lo)
                              + _dot(bht_ref[gi], nhi) + _dot(bht_ref[gi], nlo) - dy)
        dx, dgp = _rms_bwd(du_sc[...], h_ref[...], g_ref[...])
        o_ref[...] = dho + dx
        dg_ref[...] += dgp

    row = pl.BlockSpec((tm, D), lambda i: (i, 0))
    return pl.pallas_call(
        body, name="pool_bwd", grid=(nt,),
        in_specs=[row, pl.BlockSpec((POOL_HALO, D), lambda i: (jnp.minimum((i + 1) * hb, T // POOL_HALO - 1), 0)),
                  row, _full((1, D)), row, _full((G, dg, dg)), _full((1, D)),
                  _full((G, tm, tm)), _full((G, tm, POOL_HALO))],
        out_specs=[row, _full((1, D)), _full((G, dg, dg)), _full((1, D))],
        out_shape=[jax.ShapeDtypeStruct((T, D), F32), jax.ShapeDtypeStruct((1, D), F32),
                   jax.ShapeDtypeStruct((G, dg, dg), F32), jax.ShapeDtypeStruct((1, D), F32)],
        scratch_shapes=[pltpu.VMEM((tm, D), F32)],
        compiler_params=_cparams("arbitrary"),
    )(dh, dh, h, g, y, wp, scale, bmt, bht)


def _rope_tables(T):
    pos = jnp.arange(T, dtype=F32)
    inv_freq = ROPE_THETA ** (-jnp.arange(0, D_ROPE, 2, dtype=F32) / D_ROPE)
    ang = pos[:, None] * inv_freq[None, :]
    cos2 = jnp.tile(jnp.cos(ang), (1, 2))
    sin2 = jnp.tile(jnp.sin(ang), (1, 2))
    pad = jnp.zeros((T, D_HEAD_PAD - D_QK), F32)
    ca_q = jnp.concatenate([jnp.ones((T, D_NOPE), F32), cos2, pad], axis=1)
    ca_k = jnp.concatenate([jnp.zeros((T, D_NOPE), F32), cos2, pad], axis=1)
    sb = jnp.concatenate([jnp.zeros((T, D_NOPE), F32), sin2, pad], axis=1)
    return ca_q, ca_k, sb


def _rope_weight_pair(w):
    half = D_ROPE // 2
    z_pad = jnp.zeros(w.shape[:-1] + (D_HEAD_PAD - D_QK,), w.dtype)
    z_nope = jnp.zeros(w.shape[:-1] + (D_NOPE,), w.dtype)
    wa = jnp.concatenate([w, z_pad], axis=-1)
    wb = jnp.concatenate([z_nope, -w[..., D_NOPE + half:], w[..., D_NOPE:D_NOPE + half], z_pad], axis=-1)
    return wa, wb


def _rope_weight_pair_grad(dwa, dwb):
    half = D_ROPE // 2
    d1 = dwa[..., D_NOPE:D_NOPE + half] + dwb[..., D_NOPE + half:D_QK]
    d2 = dwa[..., D_NOPE + half:D_QK] - dwb[..., D_NOPE:D_NOPE + half]
    return jnp.concatenate([dwa[..., :D_NOPE], d1, d2], axis=-1)


def _q_proj(cq, qg, wa, wb, ca, sb):
    T, R = cq.shape
    tm = min(TOKEN_TILE, T)
    P = D_HEAD_PAD
    GP = HEAD_GROUP * P

    def body(cq_ref, qg_ref, wa_ref, wb_ref, ca_ref, sb_ref, q_ref):
        c = _rms_fwd(cq_ref[...], qg_ref[...]).astype(BF16)
        ca = jnp.tile(ca_ref[...], (1, HEAD_GROUP))
        sb = jnp.tile(sb_ref[...], (1, HEAD_GROUP))
        for grp in range(N_HEADS // HEAD_GROUP):
            cols = slice(grp * GP, (grp + 1) * GP)
            q_ref[:, cols] = (_dot(c, wa_ref[:, cols]) * ca + _dot(c, wb_ref[:, cols]) * sb).astype(BF16)

    tok = pl.BlockSpec((tm, P), lambda i: (i, 0))
    wfull = _full((R, N_HEADS * P))
    return pl.pallas_call(
        body, name="q_proj", grid=(T // tm,),
        in_specs=[pl.BlockSpec((tm, R), lambda i: (i, 0)), _full((1, R)), wfull, wfull, tok, tok],
        out_specs=pl.BlockSpec((tm, N_HEADS * P), lambda i: (i, 0)),
        out_shape=jax.ShapeDtypeStruct((T, N_HEADS * P), BF16),
        compiler_params=_cparams("parallel"),
    )(cq, qg, wa, wb, ca, sb)


def _q_proj_bwd(dq, cq, qg, wa, wb, ca, sb):
    T, R = cq.shape
    tm = min(TOKEN_TILE, T)
    P = D_HEAD_PAD
    GP = HEAD_GROUP * P

    def body(dq_ref, cq_ref, qg_ref, wa_ref, wb_ref, ca_ref, sb_ref, dcq_ref, dqg_ref, dwa_ref, dwb_ref):
        @pl.when(pl.program_id(0) == 0)
        def _():
            for r in (dqg_ref, dwa_ref, dwb_ref):
                r[...] = jnp.zeros_like(r)

        cq_f = cq_ref[...]
        cqn = _rms_fwd(cq_f, qg_ref[...]).astype(BF16)
        ca = jnp.tile(ca_ref[...], (1, HEAD_GROUP))
        sb = jnp.tile(sb_ref[...], (1, HEAD_GROUP))
        acc = None
        for grp in range(N_HEADS // HEAD_GROUP):
            cols = slice(grp * GP, (grp + 1) * GP)
            d = dq_ref[cols, :].T
            da = (d * ca).astype(BF16)
            db = (d * sb).astype(BF16)
            part = _dot_nt(da, wa_ref[:, cols]) + _dot_nt(db, wb_ref[:, cols])
            acc = part if acc is None else acc + part
            dwa_ref[:, cols] += _dot_tn(cqn, da)
            dwb_ref[:, cols] += _dot_tn(cqn, db)
        dx, dg = _rms_bwd(acc, cq_f, qg_ref[...])
        dcq_ref[...] = dx
        dqg_ref[...] += dg

    tok = pl.BlockSpec((tm, P), lambda i: (i, 0))
    rr = pl.BlockSpec((tm, R), lambda i: (i, 0))
    wfull = _full((R, N_HEADS * P))
    return pl.pallas_call(
        body, name="q_proj_bwd", grid=(T // tm,),
        in_specs=[pl.BlockSpec((N_HEADS * P, tm), lambda i: (0, i)), rr, _full((1, R)), wfull, wfull, tok, tok],
        out_specs=[rr, _full((1, R)), wfull, wfull],
        out_shape=[jax.ShapeDtypeStruct((T, R), F32), jax.ShapeDtypeStruct((1, R), F32),
                   jax.ShapeDtypeStruct((R, N_HEADS * P), F32), jax.ShapeDtypeStruct((R, N_HEADS * P), F32)],
        compiler_params=_cparams("arbitrary"),
    )(dq, cq, qg, wa, wb, ca, sb)


def _kv_proj(h, g_in, wka, wkb, g_c, wuk, wuv, ca, sb):
    T, D = h.shape
    tm = min(TOKEN_TILE, T)
    P, C = D_HEAD_PAD, D_NOPE

    def body(h_ref, gi_ref, wka_ref, wkb_ref, gc_ref, wuk_ref, wuv_ref, ca_ref, sb_ref, k_ref, v_ref, craw_ref):
        u = _rms_fwd(h_ref[...], gi_ref[...]).astype(BF16)
        kva = _dot(u, wka_ref[...])
        kvb = _dot(u, wkb_ref[...])
        craw = kva[:, :C]
        craw_ref[...] = craw
        c = _rms_fwd(craw, gc_ref[...]).astype(BF16)
        kr = kva * ca_ref[...] + kvb * sb_ref[...]
        kn = _dot(c, wuk_ref[...])
        for hh in range(N_HEADS):
            k_ref[:, hh * P:(hh + 1) * P] = (kn[:, hh * P:(hh + 1) * P] + kr).astype(BF16)
        v_ref[...] = _dot(c, wuv_ref[...]).astype(BF16)

    tok = pl.BlockSpec((tm, P), lambda i: (i, 0))
    return pl.pallas_call(
        body, name="kv_proj", grid=(T // tm,),
        in_specs=[pl.BlockSpec((tm, D), lambda i: (i, 0)), _full((1, D)), _full((D, P)), _full((D, P)),
                  _full((1, C)), _full(wuk.shape), _full(wuv.shape), tok, tok],
        out_specs=[pl.BlockSpec((tm, N_HEADS * P), lambda i: (i, 0)),
                   pl.BlockSpec((tm, N_HEADS * D_V), lambda i: (i, 0)), pl.BlockSpec((tm, C), lambda i: (i, 0))],
        out_shape=[jax.ShapeDtypeStruct((T, N_HEADS * P), BF16), jax.ShapeDtypeStruct((T, N_HEADS * D_V), BF16),
                   jax.ShapeDtypeStruct((T, C), F32)],
        compiler_params=_cparams("parallel"),
    )(h, g_in, wka, wkb, g_c, wuk, wuv, ca, sb)


def _kv_proj_bwd(dks, dvs, dh, h, g_in, wka, wkb, craw, g_c, wuk, wuv, ca, sb):
    T, D = h.shape
    tm = min(TOKEN_TILE // 2, T)
    P, C = D_HEAD_PAD, D_NOPE
    nl = len(dks)

    def body(*refs):
        dk_refs, dv_refs = refs[:nl], refs[nl:2 * nl]
        (dh_ref, h_ref, gi_ref, wka_ref, wkb_ref, craw_ref, gc_ref, wuk_ref, wuv_ref,
         ca_ref, sb_ref, o_ref, dgi_ref, dwka_ref, dwkb_ref, dgc_ref, dwuk_ref, dwuv_ref) = refs[2 * nl:]

        @pl.when(pl.program_id(0) == 0)
        def _():
            for r in (dgi_ref, dwka_ref, dwkb_ref, dgc_ref, dwuk_ref, dwuv_ref):
                r[...] = jnp.zeros_like(r)

        x = h_ref[...]
        u = _rms_fwd(x, gi_ref[...]).astype(BF16)
        craw = craw_ref[...]
        c = _rms_fwd(craw, gc_ref[...]).astype(BF16)
        dkf = sum(r[...] for r in dk_refs[1:]) + dk_refs[0][...]
        dkb = dkf.astype(BF16)
        dvb = (sum(r[...] for r in dv_refs[1:]) + dv_refs[0][...]).astype(BF16)
        dwuk_ref[...] += _dot_tn(c, dkb)
        dwuv_ref[...] += _dot_tn(c, dvb)
        dc = _dot_nt(dkb, wuk_ref[...]) + _dot_nt(dvb, wuv_ref[...])
        dkr = dkf[:, :P]
        for hh in range(1, N_HEADS):
            dkr = dkr + dkf[:, hh * P:(hh + 1) * P]
        dcraw, dgc = _rms_bwd(dc, craw, gc_ref[...])
        dgc_ref[...] += dgc
        dkva = jnp.concatenate([dcraw, (dkr * ca_ref[...])[:, C:]], axis=1).astype(BF16)
        dkvb = (dkr * sb_ref[...]).astype(BF16)
        dwka_ref[...] += _dot_tn(u, dkva)
        dwkb_ref[...] += _dot_tn(u, dkvb)
        du = _dot_nt(dkva, wka_ref[...]) + _dot_nt(dkvb, wkb_ref[...])
        dx, dgi = _rms_bwd(du, x, gi_ref[...])
        dgi_ref[...] += dgi
        o_ref[...] = dh_ref[...] + dx

    row = pl.BlockSpec((tm, D), lambda i: (i, 0))
    tok = pl.BlockSpec((tm, P), lambda i: (i, 0))
    return pl.pallas_call(
        body, name="kv_proj_bwd", grid=(T // tm,),
        in_specs=[pl.BlockSpec((tm, N_HEADS * P), lambda i: (i, 0))] * nl
        + [pl.BlockSpec((tm, N_HEADS * D_V), lambda i: (i, 0))] * nl
        + [row, row, _full((1, D)), _full((D, P)), _full((D, P)), pl.BlockSpec((tm, C), lambda i: (i, 0)),
           _full((1, C)), _full(wuk.shape), _full(wuv.shape), tok, tok],
        out_specs=[row, _full((1, D)), _full((D, P)), _full((D, P)), _full((1, C)),
                   _full(wuk.shape), _full(wuv.shape)],
        out_shape=[jax.ShapeDtypeStruct((T, D), F32), jax.ShapeDtypeStruct((1, D), F32),
                   jax.ShapeDtypeStruct((D, P), F32), jax.ShapeDtypeStruct((D, P), F32),
                   jax.ShapeDtypeStruct((1, C), F32), jax.ShapeDtypeStruct(wuk.shape, F32),
                   jax.ShapeDtypeStruct(wuv.shape, F32)],
        compiler_params=_cparams("arbitrary"),
    )(*dks, *dvs, dh, h, g_in, wka, wkb, craw, g_c, wuk, wuv, ca, sb)


_ATTN_SCALE = D_QK ** -0.5
_LOG2_E = 1.4426950408889634
_LN_2 = 0.6931471805599453
_ATTN_SCALE_LOG2 = _ATTN_SCALE * _LOG2_E


def _flash_fwd(q, k, v):
    T = q.shape[0]
    t = min(ATTN_TILE, T // 2)
    tq = 2 * t
    P = D_HEAD_PAD

    def body(q_ref, k_ref, v_ref, o_ref, lse_ref, m_sc, l_sc, acc_sc, s0_sc, s1_sc, p0_sc, p1_sc, a0_sc, a1_sc):
        qi = pl.program_id(1)
        n = 2 * (qi + 1)
        s_sc, p_sc, a_sc = (s0_sc, s1_sc), (p0_sc, p1_sc), (a0_sc, a1_sc)
        m_sc[...] = jnp.full_like(m_sc, NEG_BIG)
        l_sc[...] = jnp.zeros_like(l_sc)
        acc_sc[...] = jnp.zeros_like(acc_sc)

        def rows_of(c):
            return pl.ds(pl.multiple_of(c * t, t), t)

        def scores(c, slot):
            s_sc[slot][...] = _dot_nt(k_ref[rows_of(c), :], q_ref[...])

        def softmax(slot, key_offset):
            s_t = s_sc[slot][...]
            if key_offset is not None:
                rows = lax.broadcasted_iota(jnp.int32, (t, tq), 0) + key_offset
                s_t = jnp.where(rows <= lax.broadcasted_iota(jnp.int32, (t, tq), 1), s_t, NEG_BIG)
            m_prev = m_sc[...]
            m_new = jnp.maximum(m_prev, jnp.max(s_t, axis=0, keepdims=True))
            p_t = jnp.exp2(s_t - m_new)
            alpha = jnp.exp2(m_prev - m_new)
            l_sc[...] = alpha * l_sc[...] + jnp.sum(p_t, axis=0, keepdims=True)
            m_sc[...] = m_new
            p_sc[slot][...] = p_t.astype(BF16)
            a_sc[slot][...] = alpha

        def values(c, slot):
            acc_sc[...] = a_sc[slot][...] * acc_sc[...] + _dot_tn(v_ref[rows_of(c), :], p_sc[slot][...])

        def stage(c, slot, first=False, last=False, key_offset=None):
            if not first:
                values(c - 1, 1 - slot)
            if not last:
                scores(c + 1, 1 - slot)
            softmax(slot, key_offset)

        def drain():
            stage(n - 2, 0, key_offset=0)
            stage(n - 1, 1, last=True, key_offset=t)
            values(n - 1, 1)

        scores(0, 0)

        @pl.when(qi == 0)
        def _():
            stage(0, 0, first=True, key_offset=0)
            stage(1, 1, last=True, key_offset=t)
            values(1, 1)

        @pl.when(qi > 0)
        def _():
            stage(0, 0, first=True)

            def pair(j, carry):
                stage(1 + 2 * j, 1)
                stage(2 + 2 * j, 0)
                return carry

            lax.fori_loop(0, qi - 1, pair, 0)
            stage(n - 3, 1)
            drain()

        l = l_sc[...]
        o_ref[...] = (acc_sc[...] / l).T.astype(BF16)
        lse_ref[...] = m_sc[...] + jnp.log(l) * _LOG2_E

    return pl.pallas_call(
        body, name="flash_fwd", grid=(N_HEADS, T // tq),
        in_specs=[pl.BlockSpec((tq, P), lambda hh, i: (i, hh)), pl.BlockSpec((T, P), lambda hh, i: (0, hh)),
                  pl.BlockSpec((T, D_V), lambda hh, i: (0, hh))],
        out_specs=[pl.BlockSpec((tq, D_V), lambda hh, i: (i, hh)),
                   pl.BlockSpec((None, None, 1, tq), lambda hh, i: (hh, i, 0, 0))],
        out_shape=[jax.ShapeDtypeStruct((T, N_HEADS * D_V), BF16),
                   jax.ShapeDtypeStruct((N_HEADS, T // tq, 1, tq), F32)],
        scratch_shapes=[pltpu.VMEM((1, tq), F32), pltpu.VMEM((1, tq), F32), pltpu.VMEM((D_V, tq), F32),
                        pltpu.VMEM((t, tq), F32), pltpu.VMEM((t, tq), F32), pltpu.VMEM((t, tq), BF16),
                        pltpu.VMEM((t, tq), BF16), pltpu.VMEM((1, tq), F32), pltpu.VMEM((1, tq), F32)],
        compiler_params=_cparams("parallel", "arbitrary"),
    )(q, k, v)


def _attn_delta(o, do):
    T = o.shape[0]
    t = min(ATTN_TILE, T)

    def body(o_ref, do_ref, out_ref):
        ones = jnp.ones((8, D_V), BF16)
        for hh in range(N_HEADS):
            cols = slice(hh * D_V, (hh + 1) * D_V)
            hi, lo = _split_bf16(o_ref[:, cols].astype(F32) * do_ref[:, cols].astype(F32))
            out_ref[hh] = (_dot_nt(ones, hi) + _dot_nt(ones, lo))[0:1]

    tok = pl.BlockSpec((t, N_HEADS * D_V), lambda i: (i, 0))
    return pl.pallas_call(
        body, name="attn_delta", grid=(T // t,), in_specs=[tok, tok],
        out_specs=pl.BlockSpec((N_HEADS, None, 1, t), lambda i: (0, i, 0, 0)),
        out_shape=jax.ShapeDtypeStruct((N_HEADS, T // t, 1, t), F32),
        compiler_params=_cparams("parallel"),
    )(o, do)


def _flash_bwd(q, k, v, do, lse_row, delta_row):
    T = q.shape[0]
    t = min(ATTN_TILE, T // 2)
    tk = 2 * t
    nq = T // t
    P = D_HEAD_PAD

    def body(k_ref, v_ref, q_ref, do_ref, lse_ref, delta_ref, dqt_ref, dk_ref, dv_ref, dqt_sc, kt_sc,
             s0_sc, s1_sc, dp0_sc, dp1_sc, p0_sc, p1_sc, ds0_sc, ds1_sc):
        ki = pl.program_id(1)
        q0 = 2 * ki
        n = nq - q0
        s_sc, dp_sc, p_sc, ds_sc = (s0_sc, s1_sc), (dp0_sc, dp1_sc), (p0_sc, p1_sc), (ds0_sc, ds1_sc)
        kt_sc[...] = k_ref[...].astype(F32).T.astype(BF16)
        dk_ref[...] = jnp.zeros_like(dk_ref)
        dv_ref[...] = jnp.zeros_like(dv_ref)

        @pl.when(ki == 0)
        def _():
            dqt_sc[...] = jnp.zeros_like(dqt_sc)

        def rows_of(c):
            return pl.ds(pl.multiple_of((q0 + c) * t, t), t)

        def products(c, slot):
            s_sc[slot][...] = _dot_nt(k_ref[...], q_ref[rows_of(c), :])
            dp_sc[slot][...] = _dot_nt(v_ref[...], do_ref[rows_of(c), :])

        def elementwise(c, slot, query_offset):
            p_t = jnp.exp2(s_sc[slot][...] - lse_ref[q0 + c])
            if query_offset is not None:
                cols = lax.broadcasted_iota(jnp.int32, (tk, t), 1) + query_offset
                p_t = jnp.where(lax.broadcasted_iota(jnp.int32, (tk, t), 0) <= cols, p_t, 0.0)
            p_sc[slot][...] = p_t.astype(BF16)
            ds_sc[slot][...] = (p_t * (dp_sc[slot][...] - delta_ref[q0 + c])).astype(BF16)

        def gradients(c, slot):
            dv_ref[...] += _dot(p_sc[slot][...], do_ref[rows_of(c), :])
            ds_t = ds_sc[slot][...]
            dk_ref[...] += _dot(ds_t, q_ref[rows_of(c), :])
            dqt_sc[q0 + c] += _dot(kt_sc[...], ds_t)

        def stage(c, slot, first=False, last=False, query_offset=None):
            if not first:
                gradients(c - 1, 1 - slot)
            if not last:
                products(c + 1, 1 - slot)
            elementwise(c, slot, query_offset)

        products(0, 0)

        @pl.when(n == 2)
        def _():
            stage(0, 0, first=True, query_offset=0)
            stage(1, 1, last=True, query_offset=t)
            gradients(1, 1)

        @pl.when(n > 2)
        def _():
            stage(0, 0, first=True, query_offset=0)
            stage(1, 1, query_offset=t)

            def pair(j, carry):
                stage(2 + 2 * j, 0)
                stage(3 + 2 * j, 1)
                return carry

            lax.fori_loop(0, (n - 4) // 2, pair, 0)
            stage(n - 2, 0)
            stage(n - 1, 1, last=True)
            gradients(n - 1, 1)

        dqt_ref[:, :t] = dqt_sc[q0] * _ATTN_SCALE
        dqt_ref[:, t:] = dqt_sc[q0 + 1] * _ATTN_SCALE
        dk_ref[...] = dk_ref[...] * _LN_2

    kb = pl.BlockSpec((tk, P), lambda hh, i: (i, hh))
    vb = pl.BlockSpec((tk, D_V), lambda hh, i: (i, hh))
    stat = pl.BlockSpec((None, nq, 1, t), lambda hh, i: (hh, 0, 0, 0))
    return pl.pallas_call(
        body, name="flash_bwd", grid=(N_HEADS, T // tk),
        in_specs=[kb, vb,
                  pl.BlockSpec((T, P), lambda hh, i: (0, hh), pipeline_mode=pl.Buffered(1)),
                  pl.BlockSpec((T, D_V), lambda hh, i: (0, hh), pipeline_mode=pl.Buffered(1)),
                  stat, stat],
        out_specs=[pl.BlockSpec((P, tk), lambda hh, i: (hh, i)), kb, vb],
        out_shape=[jax.ShapeDtypeStruct((N_HEADS * P, T), F32), jax.ShapeDtypeStruct((T, N_HEADS * P), F32),
                   jax.ShapeDtypeStruct((T, N_HEADS * D_V), F32)],
        scratch_shapes=[pltpu.VMEM((nq, P, t), F32), pltpu.VMEM((P, tk), BF16)]
        + [pltpu.VMEM((tk, t), F32)] * 4 + [pltpu.VMEM((tk, t), BF16)] * 4,
        compiler_params=_cparams("arbitrary", "arbitrary"),
    )(k, v, q, do, lse_row, delta_row)


def _loss_head(h, g, target):
    T, D = h.shape
    tm = min(TOKEN_TILE, T)

    def body(h_ref, g_ref, t_ref, dh_ref, loss_ref, dg_ref):
        @pl.when(pl.program_id(0) == 0)
        def _():
            loss_ref[...] = jnp.zeros_like(loss_ref)
            dg_ref[...] = jnp.zeros_like(dg_ref)

        x = h_ref[...]
        err = _rms_fwd(x, g_ref[...]) - t_ref[...]
        per_tok = jnp.mean(err * err, axis=-1, keepdims=True)
        loss_ref[...] += 0.5 * jnp.sum(per_tok, axis=0, keepdims=True)
        dx, dg = _rms_bwd(err * (1.0 / D), x, g_ref[...])
        dh_ref[...] = dx
        dg_ref[...] += dg

    row = pl.BlockSpec((tm, D), lambda i: (i, 0))
    return pl.pallas_call(
        body, name="loss_head", grid=(T // tm,),
        in_specs=[row, _full((1, D)), row], out_specs=[row, _full((1, 128)), _full((1, D))],
        out_shape=[jax.ShapeDtypeStruct((T, D), F32), jax.ShapeDtypeStruct((1, 128), F32),
                   jax.ShapeDtypeStruct((1, D), F32)],
        compiler_params=_cparams("arbitrary"),
    )(h, g, target)


def _sum_parts(parts, tr, name):
    _, R, C = parts.shape

    def body(p_ref, o_ref):
        acc = p_ref[0].astype(F32)
        for j in range(1, N_DEV):
            acc = acc + p_ref[j].astype(F32)
        o_ref[...] = acc

    return pl.pallas_call(
        body, name=name, grid=(R // tr,),
        in_specs=[pl.BlockSpec((N_DEV, tr, C), lambda i: (0, i, 0))],
        out_specs=pl.BlockSpec((tr, C), lambda i: (i, 0)),
        out_shape=jax.ShapeDtypeStruct((R, C), F32),
        compiler_params=_cparams("parallel"),
    )(parts)


def _adamw(w, g, m, v):
    R, C = w.shape
    tr = _row_tile(R, TOKEN_TILE)

    def body(w_ref, g_ref, m_ref, v_ref, d_ref, mo_ref, vo_ref):
        gg = g_ref[...]
        mn = ADAM_B1 * m_ref[...] + (1.0 - ADAM_B1) * gg
        vn = ADAM_B2 * v_ref[...] + (1.0 - ADAM_B2) * (gg * gg)
        m_hat = mn / (1.0 - ADAM_B1 ** ADAM_STEP)
        v_hat = vn / (1.0 - ADAM_B2 ** ADAM_STEP)
        d_ref[...] = -ADAM_LR * (m_hat / (jnp.sqrt(v_hat) + ADAM_EPS) + ADAM_WD * w_ref[...])
        mo_ref[...] = mn
        vo_ref[...] = vn

    blk = pl.BlockSpec((tr, C), lambda i: (i, 0))
    return pl.pallas_call(
        body, name="adamw", grid=(R // tr,), in_specs=[blk] * 4, out_specs=[blk] * 3,
        out_shape=[jax.ShapeDtypeStruct((R, C), F32)] * 3,
        compiler_params=_cparams("parallel"),
    )(w, g, m, v)


def _adamw_nd(w, g, m, v):
    shape = w.shape
    two_d = (1, shape[0]) if len(shape) == 1 else (int(np.prod(shape[:-1])), shape[-1])
    outs = _adamw(w.reshape(two_d), g.reshape(two_d), m.reshape(two_d), v.reshape(two_d))
    return tuple(o.reshape(shape) for o in outs)


def _f32_as_bf16_pairs(a):
    return lax.bitcast_convert_type(a, BF16).reshape(a.shape[:-1] + (a.shape[-1] * 2,))


def _bf16_pairs_as_f32(a):
    return lax.bitcast_convert_type(a.reshape(a.shape[:-1] + (a.shape[-1] // 2, 2)), F32)


def _pack_misc(w_o, w_dq, w_uq, w_dkv, pool_w, pool_scale):
    lead = w_o.shape[:-3]
    rows = [w_o, w_dq, w_uq, w_dkv, pool_w]
    flat = [r.astype(BF16).reshape(lead + (-1, REP_COLS)) for r in rows]
    ps = _f32_as_bf16_pairs(pool_scale.astype(F32)).reshape(lead + (1, -1))
    ps = jnp.concatenate([ps, jnp.zeros(lead + (1, REP_COLS - ps.shape[-1]), BF16)], axis=-1)
    used = sum(f.shape[-2] for f in flat) + 1
    pad = jnp.zeros(lead + (MISC_ROWS - used, REP_COLS), BF16)
    return jnp.concatenate(flat + [ps, pad], axis=-2)


def _unpack_misc(buf, shapes):
    out, r0 = [], 0
    for shp in shapes[:-1]:
        n = int(np.prod(shp)) // REP_COLS
        out.append(buf[:, r0:r0 + n].reshape((N_DEV,) + shp))
        r0 += n
    n_ps = int(np.prod(shapes[-1]))
    out.append(_bf16_pairs_as_f32(buf[:, r0, :2 * n_ps]).reshape((N_DEV,) + shapes[-1]))
    return out


def _cat_dev(a, axis):
    a = jnp.moveaxis(a, 0, axis)
    return a.reshape(a.shape[:axis] + (a.shape[axis] * a.shape[axis + 1],) + a.shape[axis + 2:])


def _split_dev(a, axis):
    a = a.reshape(a.shape[:axis] + (N_DEV, a.shape[axis] // N_DEV) + a.shape[axis + 1:])
    return jnp.moveaxis(a, axis, 0)


def kernel(x, ffn_pre_norm, ffn_pre_wg, ffn_pre_wu, ffn_pre_wd, mix_norm, ffn_post_norm, ffn_post_wg, ffn_post_wu, ffn_post_wd, pool_w, pool_scale, kv_in_norm, w_dkv, ckv_norm, w_uk, w_uv, q_lora_norm, w_dq, w_uq, w_o, final_norm, loss_target, m_ffn_pre_norm, m_ffn_pre_wg, m_ffn_pre_wu, m_ffn_pre_wd, m_mix_norm, m_ffn_post_norm, m_ffn_post_wg, m_ffn_post_wu, m_ffn_post_wd, m_pool_w, m_pool_scale, m_kv_in_norm, m_w_dkv, m_ckv_norm, m_w_uk, m_w_uv, m_q_lora_norm, m_w_dq, m_w_uq, m_w_o, m_final_norm, v_ffn_pre_norm, v_ffn_pre_wg, v_ffn_pre_wu, v_ffn_pre_wd, v_mix_norm, v_ffn_post_norm, v_ffn_post_wg, v_ffn_post_wu, v_ffn_post_wd, v_pool_w, v_pool_scale, v_kv_in_norm, v_w_dkv, v_ckv_norm, v_w_uk, v_w_uv, v_q_lora_norm, v_w_dq, v_w_uq, v_w_o, v_final_norm):
    weights = dict(ffn_pre_norm=ffn_pre_norm, ffn_pre_wg=ffn_pre_wg, ffn_pre_wu=ffn_pre_wu, ffn_pre_wd=ffn_pre_wd,
                   mix_norm=mix_norm, ffn_post_norm=ffn_post_norm, ffn_post_wg=ffn_post_wg,
                   ffn_post_wu=ffn_post_wu, ffn_post_wd=ffn_post_wd, pool_w=pool_w, pool_scale=pool_scale,
                   kv_in_norm=kv_in_norm, w_dkv=w_dkv, ckv_norm=ckv_norm, w_uk=w_uk, w_uv=w_uv,
                   q_lora_norm=q_lora_norm, w_dq=w_dq, w_uq=w_uq, w_o=w_o, final_norm=final_norm)
    moments_m = dict(ffn_pre_norm=m_ffn_pre_norm, ffn_pre_wg=m_ffn_pre_wg, ffn_pre_wu=m_ffn_pre_wu,
                     ffn_pre_wd=m_ffn_pre_wd, mix_norm=m_mix_norm, ffn_post_norm=m_ffn_post_norm,
                     ffn_post_wg=m_ffn_post_wg, ffn_post_wu=m_ffn_post_wu, ffn_post_wd=m_ffn_post_wd,
                     pool_w=m_pool_w, pool_scale=m_pool_scale, kv_in_norm=m_kv_in_norm, w_dkv=m_w_dkv,
                     ckv_norm=m_ckv_norm, w_uk=m_w_uk, w_uv=m_w_uv, q_lora_norm=m_q_lora_norm, w_dq=m_w_dq,
                     w_uq=m_w_uq, w_o=m_w_o, final_norm=m_final_norm)
    moments_v = dict(ffn_pre_norm=v_ffn_pre_norm, ffn_pre_wg=v_ffn_pre_wg, ffn_pre_wu=v_ffn_pre_wu,
                     ffn_pre_wd=v_ffn_pre_wd, mix_norm=v_mix_norm, ffn_post_norm=v_ffn_post_norm,
                     ffn_post_wg=v_ffn_post_wg, ffn_post_wu=v_ffn_post_wu, ffn_post_wd=v_ffn_post_wd,
                     pool_w=v_pool_w, pool_scale=v_pool_scale, kv_in_norm=v_kv_in_norm, w_dkv=v_w_dkv,
                     ckv_norm=v_ckv_norm, w_uk=v_w_uk, w_uv=v_w_uv, q_lora_norm=v_q_lora_norm, w_dq=v_w_dq,
                     w_uq=v_w_uq, w_o=v_w_o, final_norm=v_final_norm)
    order = list(weights)

    T, D = x.shape[1], x.shape[2]
    depth = ffn_pre_norm.shape[0]
    n_a = pool_w.shape[0]
    n_b = depth - n_a
    fs = ffn_pre_wd.shape[1]
    F = fs * N_DEV
    n_ffn = 2 * depth
    t_attn = min(ATTN_TILE, T)

    ffn_local = [
        jnp.stack([jnp.swapaxes(wg[l], 0, 1), jnp.swapaxes(wu[l], 0, 1), wd[l]]).astype(BF16)
        for l in range(depth)
        for wg, wu, wd in ((ffn_pre_wg, ffn_pre_wu, ffn_pre_wd), (ffn_post_wg, ffn_post_wu, ffn_post_wd))
    ]
    misc_local = _pack_misc(w_o, w_dq, w_uq.reshape(n_b, w_uq.shape[1], -1), w_dkv, pool_w, pool_scale)
    misc_shapes = (w_o.shape, w_dq.shape, (n_b, w_uq.shape[1], N_HEADS * D_QK), w_dkv.shape, pool_w.shape,
                   pool_scale.shape)
    (w0_all,) = _exchange([(ffn_local[0], "gather_mid")], "comm_all_gather")
    walls = [w0_all.reshape(3, F, D)] + [None] * (n_ffn - 1)

    def vec(a):
        return a.reshape(1, -1)

    def ffn_stage(e, h_in, norm):
        carry = [(ffn_local[e + 1], "gather_mid")] if e + 1 < n_ffn else []
        if e == 0:
            carry.append((misc_local, "to_all"))
        outs = _ffn_fwd(h_in, norm, walls[e], carry)
        if carry:
            walls[e + 1] = outs[3].reshape(3, F, D)
        return outs

    h = x.reshape(T, D)
    stage0 = ffn_stage(0, h, vec(ffn_pre_norm[0]))
    misc_all = stage0[4]
    o_blk, dq_blk, uq_blk, dkv_blk, pw_blk, ps_blk = _unpack_misc(misc_all, misc_shapes)
    w_o_f = _cat_dev(o_blk, 1)
    w_dq_f = _cat_dev(dq_blk, 1)
    w_uq_f = _cat_dev(uq_blk, 1).reshape(n_b, -1, N_HEADS, D_QK)
    w_dkv_f = _cat_dev(dkv_blk, 0)
    pool_w_f = _cat_dev(pw_blk, 2)
    pool_scale_f = _cat_dev(ps_blk, 1)
    rq = w_dq_f.shape[2]
    wqa, wqb = _rope_weight_pair(w_uq_f)
    wqa = wqa.reshape(n_b, rq, N_HEADS * D_HEAD_PAD)
    wqb = wqb.reshape(n_b, rq, N_HEADS * D_HEAD_PAD)
    wka, wkb = _rope_weight_pair(w_dkv_f)
    wuk = jnp.concatenate([w_uk, jnp.zeros_like(w_uk)], axis=-1).astype(BF16).reshape(D_NOPE, N_HEADS * D_HEAD_PAD)
    wuv = w_uv.astype(BF16).reshape(D_NOPE, N_HEADS * D_V)
    ca_q, ca_k, sb = _rope_tables(T)
    ca_q_scaled, sb_scaled = ca_q * _ATTN_SCALE_LOG2, sb * _ATTN_SCALE_LOG2

    saved = []
    k_all = v_all = craw = h_kv = None
    for l in range(depth):
        s = {"h0": h}
        h, s["g1"], s["u1"] = (stage0 if l == 0 else ffn_stage(2 * l, h, vec(ffn_pre_norm[l])))[:3]
        s["h1"] = h
        if l < n_a:
            h, s["y"] = _pool_fwd(h, vec(mix_norm[l]), pool_w_f[l], vec(pool_scale_f[l]))
        else:
            j = l - n_a
            s["cq"] = _mm_rows(h, w_dq_f[j], nt=False, out_dtype=F32, norm_g=vec(mix_norm[l]), name="q_down")
            s["q"] = _q_proj(s["cq"], vec(q_lora_norm[j]), wqa[j], wqb[j], ca_q_scaled, sb_scaled)
            s["o"], lse = _flash_fwd(s["q"], k_all, v_all)
            s["lse"] = lse.reshape(N_HEADS, T // t_attn, 1, t_attn)
            h = _mm_rows(s["o"], w_o_f[j], nt=False, out_dtype=F32, res=h, name="attn_out")
        s["h2"] = h
        h, s["g2"], s["u2"] = ffn_stage(2 * l + 1, h, vec(ffn_post_norm[l]))[:3]
        if l == n_a - 1:
            h_kv = h
            k_all, v_all, craw = _kv_proj(h, vec(kv_in_norm), wka, wkb, vec(ckv_norm), wuk, wuv, ca_k, sb)
        saved.append(s)

    dh, loss_part, d_final = _loss_head(h, vec(final_norm), loss_target.reshape(T, D))

    slabs, ffn_parts = [None] * n_ffn, [None] * n_ffn

    misc_parts = []
    big_rep_names = ["w_uk", "w_uv"]

    def packed_misc_grads():
        return _pack_misc(_split_dev(jnp.stack(d_wo), 1), _split_dev(jnp.stack(d_wdq), 1),
                          _split_dev(jnp.stack(d_wuq).reshape(n_b, rq, -1), 1), _split_dev(grads["w_dkv"], 0),
                          _split_dev(jnp.stack(d_pool_w), 2),
                          _split_dev(jnp.concatenate(d_pool_scale, axis=0), 1))

    def ffn_stage_bwd(e, dh_out, h_in, norm, gate, up):
        carry = [(slabs[e + 1].reshape(3, N_DEV, fs, D), "scatter_mid")] if e + 1 < n_ffn else []
        if e == 0:
            carry.append((packed_misc_grads(), "scatter_lead"))
        if e == 2 * n_a - 1:
            carry.append((jnp.concatenate([grads[n].reshape(-1, REP_COLS) for n in big_rep_names]), "to_all"))
        outs = _ffn_bwd_dx(dh_out, h_in, norm, gate, up, walls[e], carry)
        if e + 1 < n_ffn:
            ffn_parts[e + 1] = outs[6]
        if e == 0:
            misc_parts.insert(0, outs[7])
        if e == 2 * n_a - 1:
            misc_parts.append(outs[6 + len(carry) - 1])
        dh_in, dgt, dup, u_t, dy_t, dnorm = outs[:6]
        slabs[e] = _ffn_bwd_dw(dgt, dup, gate, up, u_t, dy_t)
        return dh_in, dnorm

    grads = {}
    d_pre, d_post, d_mix = [None] * depth, [None] * depth, [None] * depth
    d_pool_w, d_pool_scale = [None] * n_a, [None] * n_a
    d_qln, d_wdq, d_wuq, d_wo = [None] * n_b, [None] * n_b, [None] * n_b, [None] * n_b
    dks, dvs = [], []
    for l in reversed(range(depth)):
        s = saved[l]
        if l == n_a - 1:
            (dh, grads["kv_in_norm"], dwka, dwkb, grads["ckv_norm"], dwuk, dwuv) = _kv_proj_bwd(
                dks, dvs, dh, h_kv, vec(kv_in_norm), wka, wkb, craw, vec(ckv_norm), wuk, wuv, ca_k, sb)
            grads["w_dkv"] = _rope_weight_pair_grad(dwka, dwkb)
            grads["w_uk"] = dwuk.reshape(D_NOPE, N_HEADS, D_HEAD_PAD)[..., :D_NOPE]
            grads["w_uv"] = dwuv.reshape(D_NOPE, N_HEADS, D_V)
        dh, d_post[l] = ffn_stage_bwd(2 * l + 1, dh, s["h2"], vec(ffn_post_norm[l]), s["g2"], s["u2"])
        if l < n_a:
            dh, d_mix[l], d_pool_w[l], d_pool_scale[l] = _pool_bwd(
                dh, s["h1"], vec(mix_norm[l]), s["y"], pool_w_f[l], vec(pool_scale_f[l]))
        else:
            j = l - n_a
            d_wo[j] = _mm_tn(s["o"], dh, name="attn_out_dw")
            do = _mm_rows(dh, w_o_f[j], nt=True, out_dtype=BF16, name="attn_out_dx")
            delta_row = _attn_delta(s["o"], do)
            dq_t, dk_l, dv_l = _flash_bwd(s["q"], k_all, v_all, do, s["lse"], delta_row)
            dks.append(dk_l)
            dvs.append(dv_l)
            dcq, d_qln[j], dwa, dwb = _q_proj_bwd(dq_t, s["cq"], vec(q_lora_norm[j]), wqa[j], wqb[j], ca_q, sb)
            d_wuq[j] = _rope_weight_pair_grad(dwa.reshape(rq, N_HEADS, D_HEAD_PAD),
                                              dwb.reshape(rq, N_HEADS, D_HEAD_PAD))
            d_wdq[j] = _mm_tn(s["h1"], dcq, norm_g=vec(mix_norm[l]), name="q_down_dw")
            dh, d_mix[l] = _proj_bwd(dcq, w_dq_f[j], s["h1"], vec(mix_norm[l]), dh, "q_down_dx")
        dh, d_pre[l] = ffn_stage_bwd(2 * l, dh, s["h0"], vec(ffn_pre_norm[l]), s["g1"], s["u1"])
    grad_x = dh.reshape(x.shape)

    rep_names = ["ffn_pre_norm", "mix_norm", "ffn_post_norm", "kv_in_norm", "ckv_norm", "q_lora_norm", "final_norm"]
    grads["ffn_pre_norm"] = jnp.concatenate(d_pre, axis=0)
    grads["mix_norm"] = jnp.concatenate(d_mix, axis=0)
    grads["ffn_post_norm"] = jnp.concatenate(d_post, axis=0)
    grads["q_lora_norm"] = jnp.concatenate(d_qln, axis=0)
    grads["final_norm"] = d_final
    rep_flat = jnp.concatenate([grads[n].reshape(-1) for n in rep_names] + [loss_part[0, :1]])
    n_rep = rep_flat.shape[0]
    rep_rows = -(-n_rep // (8 * REP_COLS)) * 8
    rep_g = jnp.concatenate([rep_flat, jnp.zeros((rep_rows * REP_COLS - n_rep,), F32)]).reshape(rep_rows, REP_COLS)
    ffn_parts[0], rep_parts = _exchange([(slabs[0].reshape(3, N_DEV, fs, D), "scatter_mid"), (rep_g, "to_all")],
                                        "comm_grad_exchange")
    ffn_sum = jnp.stack([_sum_parts(p.reshape(N_DEV, 3 * fs, D), fs, "sum_ffn").reshape(3, fs, D)
                         for p in ffn_parts])
    misc_sum_parts = _unpack_misc(misc_parts[0], misc_shapes)
    rep_sum = _sum_parts(rep_parts, _row_tile(rep_rows, 128), "sum_rep").reshape(-1)

    def sum_small(p):
        shp = p.shape[1:]
        two_d = (int(np.prod(shp[:-1])), shp[-1])
        return _sum_parts(p.reshape((N_DEV,) + two_d), two_d[0], "sum_misc").reshape(shp)

    g_wo, g_wdq, g_wuq, g_wdkv, g_pw, g_ps = [sum_small(p) for p in misc_sum_parts]
    grads.update(w_o=g_wo, w_dq=g_wdq, w_uq=g_wuq.reshape(w_uq.shape), w_dkv=g_wdkv, pool_w=g_pw, pool_scale=g_ps)
    for kind, (npre, npost) in enumerate((("ffn_pre_wg", "ffn_post_wg"), ("ffn_pre_wu", "ffn_post_wu"),
                                          ("ffn_pre_wd", "ffn_post_wd"))):
        pre = ffn_sum[0::2, kind]
        post = ffn_sum[1::2, kind]
        if kind < 2:
            pre, post = jnp.swapaxes(pre, 1, 2), jnp.swapaxes(post, 1, 2)
        grads[npre], grads[npost] = pre, post
    off = 0
    for n in rep_names:
        size = int(np.prod(weights[n].shape))
        grads[n] = rep_sum[off:off + size].reshape(weights[n].shape)
        off += size
    loss = rep_sum[off]
    big_rep_sum = _sum_parts(misc_parts[1], _row_tile(misc_parts[1].shape[1], 64), "sum_rep_big")
    off = 0
    for n in big_rep_names:
        rows = int(np.prod(weights[n].shape)) // REP_COLS
        grads[n] = big_rep_sum[off:off + rows].reshape(weights[n].shape)
        off += rows

    deltas, new_m, new_v = {}, {}, {}
    for n in order:
        deltas[n], new_m[n], new_v[n] = _adamw_nd(weights[n], grads[n], moments_m[n], moments_v[n])
    return (loss, grad_x, *[grads[n] for n in order], *[deltas[n] for n in order],
            *[new_m[n] for n in order], *[new_v[n] for n in order])
```

```python
import numpy as np
import jax
import jax.numpy as jnp
from jax import lax
from jax.experimental import pallas as pl
from jax.experimental.pallas import tpu as pltpu

F32, BF16 = jnp.float32, jnp.bfloat16
N_DEV = 8
RMS_EPS = 1e-6
N_HEADS = 16
D_NOPE, D_ROPE, D_V = 128, 64, 128
D_QK = D_NOPE + D_ROPE
D_HEAD_PAD = 256
HEAD_GROUP = 4
ROPE_THETA = 10000.0
POOL_WINDOWS = (2, 4, 8, 16)
POOL_HALO = 16
ADAM_LR, ADAM_B1, ADAM_B2, ADAM_EPS, ADAM_WD, ADAM_STEP = 0.001, 0.9, 0.999, 1e-08, 0.01, 10
NEG_BIG = -1e30
V7X_VMEM_LIMIT = 56 * 1024 * 1024
V7X_MXU_DIM = 256
TOKEN_TILE = 512
ATTN_TILE = 512
MISC_ROWS = 864
REP_COLS = 1024


def _cparams(*sem):
    return pltpu.CompilerParams(dimension_semantics=sem, vmem_limit_bytes=V7X_VMEM_LIMIT)


def _dot(a, b):
    return lax.dot_general(a, b, (((1,), (0,)), ((), ())), preferred_element_type=F32)


def _dot_nt(a, b):
    return lax.dot_general(a, b, (((1,), (1,)), ((), ())), preferred_element_type=F32)


def _dot_tn(a, b):
    return lax.dot_general(a, b, (((0,), (0,)), ((), ())), preferred_element_type=F32)


def _rms_fwd(x, g):
    r = lax.rsqrt(jnp.mean(x * x, axis=-1, keepdims=True) + RMS_EPS)
    return (x * r) * g


def _rms_bwd(du, x, g):
    r = lax.rsqrt(jnp.mean(x * x, axis=-1, keepdims=True) + RMS_EPS)
    xh = x * r
    dg = jnp.sum(du * xh, axis=0, keepdims=True)
    dxh = du * g
    dx = r * (dxh - xh * jnp.mean(dxh * xh, axis=-1, keepdims=True))
    return dx, dg


def _sigmoid(x):
    return 1.0 / (1.0 + jnp.exp(-x))


def _split_bf16(x):
    hi = x.astype(BF16)
    lo = (x - hi.astype(F32)).astype(BF16)
    return hi, lo


def _full(shape):
    return pl.BlockSpec(shape, lambda *_: (0,) * len(shape))


def _resident(shape):
    return pl.BlockSpec(shape, lambda *_: (0,) * len(shape), pipeline_mode=pl.Buffered(1))


def _row_tile(rows, cap):
    for t in range(min(cap, rows) // 8 * 8, 0, -8):
        if rows % t == 0:
            return t
    return rows


def _peers():
    x, y, c = lax.axis_index("x"), lax.axis_index("y"), lax.axis_index("c")
    out = []
    for k in range(1, N_DEV):
        px = 1 - x if (k >> 2) & 1 else x
        py = 1 - y if (k >> 1) & 1 else y
        pc = 1 - c if k & 1 else c
        out.append(((px, py, pc), 4 * px + 2 * py + pc))
    return 4 * x + 2 * y + c, out


_ROUTES = {
    "gather_mid": (lambda ref, idx: ref, lambda ref, idx: ref.at[:, idx], lambda s: s[:1] + (N_DEV,) + s[1:]),
    "to_all": (lambda ref, idx: ref, lambda ref, idx: ref.at[idx], lambda s: (N_DEV,) + s),
    "scatter_mid": (lambda ref, idx: ref.at[:, idx], lambda ref, idx: ref.at[idx],
                    lambda s: (N_DEV, s[0]) + s[2:]),
    "scatter_lead": (lambda ref, idx: ref.at[idx], lambda ref, idx: ref.at[idx], lambda s: s),
}


def _route_fns(items):
    kinds = [kind for _, kind in items]
    return (lambda j, ref, idx: _ROUTES[kinds[j]][0](ref, idx)), (lambda j, ref, idx: _ROUTES[kinds[j]][1](ref, idx))


def _route_out_shapes(items):
    return [jax.ShapeDtypeStruct(_ROUTES[kind][2](arr.shape), arr.dtype) for arr, kind in items]


def _exchange(items, name):
    n = len(items)
    fns = _route_fns(items)

    def body(*refs):
        ins, outs, sems = refs[:n], refs[n:2 * n], refs[2 * n:]
        _exchange_start(ins, outs, sems, *fns)
        _exchange_wait(ins, outs, sems, *fns)

    any_spec = pl.BlockSpec(memory_space=pl.ANY)
    return pl.pallas_call(
        body, name=name, out_shape=_route_out_shapes(items),
        in_specs=[any_spec] * n, out_specs=[any_spec] * n,
        scratch_shapes=_exchange_sems(n),
    )(*[arr for arr, _ in items])


def _exchange_sems(n):
    return [pltpu.SemaphoreType.DMA((n, N_DEV - 1)), pltpu.SemaphoreType.DMA((n, N_DEV - 1)),
            pltpu.SemaphoreType.DMA((n,))]


def _own_copies(ins, outs, sems, src_of, dst_of):
    me, _ = _peers()
    return [pltpu.make_async_copy(src_of(j, ins[j], me), dst_of(j, outs[j], me), sems[2].at[j])
            for j in range(len(ins))]


def _remote_copies(ins, outs, sems, src_of, dst_of, receiving):
    me, peers = _peers()
    return [pltpu.make_async_remote_copy(
        src_ref=src_of(j, ins[j], pidx), dst_ref=dst_of(j, outs[j], pidx if receiving else me),
        send_sem=sems[0].at[j, k], recv_sem=sems[1].at[j, k],
        device_id=peer, device_id_type=pl.DeviceIdType.MESH)
        for k, (peer, pidx) in enumerate(peers) for j in range(len(ins))]


def _exchange_start(ins, outs, sems, src_of, dst_of):
    for cp in _own_copies(ins, outs, sems, src_of, dst_of):
        cp.start()
    for cp in _remote_copies(ins, outs, sems, src_of, dst_of, receiving=False):
        cp.start()


def _exchange_wait(ins, outs, sems, src_of, dst_of):
    for cp in _remote_copies(ins, outs, sems, src_of, dst_of, receiving=True):
        cp.wait_recv()
    for cp in _remote_copies(ins, outs, sems, src_of, dst_of, receiving=False):
        cp.wait_send()
    for cp in _own_copies(ins, outs, sems, src_of, dst_of):
        cp.wait()


def _f_parts(F):
    first = -(-(F // V7X_MXU_DIM) // 2) * V7X_MXU_DIM
    return (slice(0, first), slice(first, F))


def _carried(items):
    if not items:
        return [], [], [], []
    return ([arr for arr, _ in items], [pl.BlockSpec(memory_space=pl.ANY)] * len(items), _route_out_shapes(items),
            _exchange_sems(len(items)))


def _ffn_fwd(h, g, w3, carry=()):
    T, D = h.shape
    F = w3.shape[1]
    tm = min(TOKEN_TILE, T)
    nt = T // tm
    nc = len(carry)
    has_c = nc > 0
    c_in, c_specs, c_out, c_sems = _carried(carry)

    def body(*refs):
        h_ref, g_ref, w_ref = refs[:3]
        ho_ref, gate_ref, up_ref = refs[3 + nc:6 + nc]
        i = pl.program_id(0)
        if has_c:
            comm = (refs[3:3 + nc], refs[6 + nc:6 + 2 * nc], refs[6 + 2 * nc:], *_route_fns(carry))

            @pl.when(i == 0)
            def _():
                _exchange_start(*comm)

        x = h_ref[...]
        u = _rms_fwd(x, g_ref[...]).astype(BF16)
        acc = None
        for cols in _f_parts(F):
            gate = _dot_nt(u, w_ref[0, cols, :])
            up = _dot_nt(u, w_ref[1, cols, :])
            gate_ref[:, cols] = gate.astype(BF16)
            up_ref[:, cols] = up.astype(BF16)
            part = _dot((gate * _sigmoid(gate) * up).astype(BF16), w_ref[2, cols, :])
            acc = part if acc is None else acc + part
        ho_ref[...] = x + 0.5 * acc

        if has_c:
            @pl.when(i == nt - 1)
            def _():
                _exchange_wait(*comm)

    row = pl.BlockSpec((tm, D), lambda i: (i, 0))
    wide = pl.BlockSpec((tm, F), lambda i: (i, 0))
    return pl.pallas_call(
        body, name="ffn_fwd_gather" if has_c else "ffn_fwd", grid=(nt,),
        in_specs=[row, _full((1, D)), _resident(w3.shape)] + c_specs,
        out_specs=[row, wide, wide] + c_specs,
        out_shape=[jax.ShapeDtypeStruct((T, D), F32), jax.ShapeDtypeStruct((T, F), BF16),
                   jax.ShapeDtypeStruct((T, F), BF16)] + c_out,
        scratch_shapes=c_sems,
        compiler_params=_cparams("arbitrary"),
    )(h, g, w3, *c_in)


def _ffn_bwd_dx(dho, h, g, gate, up, w3, carry=()):
    T, D = h.shape
    F = w3.shape[1]
    tm = min(TOKEN_TILE // 2, T)
    nt = T // tm
    nc = len(carry)
    has_c = nc > 0
    c_in, c_specs, c_out, c_sems = _carried(carry)

    def body(*refs):
        dho_ref, h_ref, g_ref, gate_ref, up_ref, w_ref = refs[:6]
        dhi_ref, dgate_ref, dup_ref, u_ref, dy_ref, dg_ref = refs[6 + nc:12 + nc]
        i = pl.program_id(0)
        if has_c:
            comm = (refs[6:6 + nc], refs[12 + nc:12 + 2 * nc], refs[12 + 2 * nc:], *_route_fns(carry))

            @pl.when(i == 0)
            def _():
                _exchange_start(*comm)

        dho = dho_ref[...]
        x = h_ref[...]
        dy_f = 0.5 * dho
        dy = dy_f.astype(BF16)
        dy_ref[...] = dy_f.T.astype(BF16)
        u_ref[...] = _rms_fwd(x, g_ref[...]).T.astype(BF16)
        acc = None
        for cols in _f_parts(F):
            dact = _dot_nt(dy, w_ref[2, cols, :])
            gt = gate_ref[:, cols].astype(F32)
            sig = _sigmoid(gt)
            dup = (dact * (gt * sig)).astype(BF16)
            dgate = (dact * up_ref[:, cols].astype(F32) * (sig * (1.0 + gt * (1.0 - sig)))).astype(BF16)
            dup_ref[:, cols] = dup
            dgate_ref[:, cols] = dgate
            part = _dot(dgate, w_ref[0, cols, :]) + _dot(dup, w_ref[1, cols, :])
            acc = part if acc is None else acc + part
        dx, dg = _rms_bwd(acc, x, g_ref[...])
        dhi_ref[...] = dho + dx

        @pl.when(i == 0)
        def _():
            dg_ref[...] = jnp.zeros_like(dg_ref)

        dg_ref[...] += dg

        if has_c:
            @pl.when(i == nt - 1)
            def _():
                _exchange_wait(*comm)

    row = pl.BlockSpec((tm, D), lambda i: (i, 0))
    col = pl.BlockSpec((D, tm), lambda i: (0, i))
    wide = pl.BlockSpec((tm, F), lambda i: (i, 0))
    return pl.pallas_call(
        body, name="ffn_bwd_dx_scatter" if has_c else "ffn_bwd_dx", grid=(nt,),
        in_specs=[row, row, _full((1, D)), wide, wide, _resident(w3.shape)] + c_specs,
        out_specs=[row, wide, wide, col, col, _full((1, D))] + c_specs,
        out_shape=[jax.ShapeDtypeStruct((T, D), F32), jax.ShapeDtypeStruct((T, F), BF16),
                   jax.ShapeDtypeStruct((T, F), BF16), jax.ShapeDtypeStruct((D, T), BF16),
                   jax.ShapeDtypeStruct((D, T), BF16), jax.ShapeDtypeStruct((1, D), F32)] + c_out,
        scratch_shapes=c_sems,
        compiler_params=_cparams("arbitrary"),
    )(dho, h, g, gate, up, w3, *c_in)


def _ffn_bwd_dw(dgate, dup, gate, up, u_t, dy_t):
    T, F = gate.shape
    D = u_t.shape[0]
    tfw = F // 2
    tk = min(TOKEN_TILE, T)
    nk = T // tk

    def body(dgate_ref, dup_ref, gate_ref, up_ref, ut_ref, dyt_ref, out_ref, acc_sc):
        k = pl.program_id(1)

        @pl.when(k == 0)
        def _():
            acc_sc[...] = jnp.zeros_like(acc_sc)

        uu = ut_ref[...]
        acc_sc[0] += _dot(uu, dgate_ref[...])
        acc_sc[1] += _dot(uu, dup_ref[...])
        gt = gate_ref[...].astype(F32)
        act = (gt * _sigmoid(gt) * up_ref[...].astype(F32)).astype(BF16)
        acc_sc[2] += _dot(dyt_ref[...], act)

        @pl.when(k == nk - 1)
        def _():
            for kind in range(3):
                out_ref[kind] = acc_sc[kind].T.astype(BF16)

    blk = pl.BlockSpec((tk, tfw), lambda j, k: (k, j))
    col = pl.BlockSpec((D, tk), lambda j, k: (0, k))
    return pl.pallas_call(
        body, name="ffn_bwd_dw", grid=(F // tfw, nk),
        in_specs=[blk, blk, blk, blk, col, col],
        out_specs=pl.BlockSpec((3, tfw, D), lambda j, k: (0, j, 0)),
        out_shape=jax.ShapeDtypeStruct((3, F, D), BF16),
        scratch_shapes=[pltpu.VMEM((3, D, tfw), F32)],
        compiler_params=_cparams("parallel", "arbitrary"),
    )(dgate, dup, gate, up, u_t, dy_t)


def _mm_rows(a, b, *, nt, out_dtype, norm_g=None, res=None, name):
    T, K = a.shape
    N = b.shape[0] if nt else b.shape[1]
    tm = min(TOKEN_TILE, T)
    has_g, has_r = norm_g is not None, res is not None

    def body(*refs):
        a_ref, b_ref = refs[0], refs[1]
        o_ref = refs[-1]
        x = a_ref[...]
        if has_g:
            x = _rms_fwd(x, refs[2][...])
        x = x.astype(BF16)
        acc = _dot_nt(x, b_ref[...]) if nt else _dot(x, b_ref[...])
        if has_r:
            acc = refs[2 + has_g][...] + acc
        o_ref[...] = acc.astype(out_dtype)

    ins, specs = [a, b], [pl.BlockSpec((tm, K), lambda i: (i, 0)), _full(b.shape)]
    if has_g:
        ins.append(norm_g)
        specs.append(_full((1, K)))
    if has_r:
        ins.append(res)
        specs.append(pl.BlockSpec((tm, N), lambda i: (i, 0)))
    return pl.pallas_call(
        body, name=name, grid=(T // tm,), in_specs=specs,
        out_specs=pl.BlockSpec((tm, N), lambda i: (i, 0)),
        out_shape=jax.ShapeDtypeStruct((T, N), out_dtype),
        compiler_params=_cparams("parallel"),
    )(*ins)


def _mm_tn(a, b, *, norm_g=None, name):
    T, M = a.shape
    N = b.shape[1]
    tk = min(TOKEN_TILE, T)
    has_g = norm_g is not None

    def body(*refs):
        a_ref, b_ref, o_ref = refs[0], refs[1], refs[-1]

        @pl.when(pl.program_id(0) == 0)
        def _():
            o_ref[...] = jnp.zeros_like(o_ref)

        x = a_ref[...]
        if has_g:
            x = _rms_fwd(x, refs[2][...])
        o_ref[...] += _dot_tn(x.astype(BF16), b_ref[...].astype(BF16))

    ins = [a, b]
    specs = [pl.BlockSpec((tk, M), lambda k: (k, 0)), pl.BlockSpec((tk, N), lambda k: (k, 0))]
    if has_g:
        ins.append(norm_g)
        specs.append(_full((1, M)))
    return pl.pallas_call(
        body, name=name, grid=(T // tk,), in_specs=specs, out_specs=_full((M, N)),
        out_shape=jax.ShapeDtypeStruct((M, N), F32),
        compiler_params=_cparams("arbitrary"),
    )(*ins)


def _proj_bwd(dz, w, h, g, dh, name):
    T, D = h.shape
    N = w.shape[1]
    tm = min(TOKEN_TILE, T)

    def body(dz_ref, w_ref, h_ref, g_ref, dh_ref, o_ref, dg_ref):
        du = _dot_nt(dz_ref[...].astype(BF16), w_ref[...])
        dx, dg = _rms_bwd(du, h_ref[...], g_ref[...])
        o_ref[...] = dh_ref[...] + dx

        @pl.when(pl.program_id(0) == 0)
        def _():
            dg_ref[...] = jnp.zeros_like(dg_ref)

        dg_ref[...] += dg

    row = pl.BlockSpec((tm, D), lambda i: (i, 0))
    return pl.pallas_call(
        body, name=name, grid=(T // tm,),
        in_specs=[pl.BlockSpec((tm, N), lambda i: (i, 0)), _full((D, N)), row, _full((1, D)), row],
        out_specs=[row, _full((1, D))],
        out_shape=[jax.ShapeDtypeStruct((T, D), F32), jax.ShapeDtypeStruct((1, D), F32)],
        compiler_params=_cparams("arbitrary"),
    )(dz, w, h, g, dh)


def _pool_bands(tm):
    r = np.arange(tm)[:, None]
    c = np.arange(tm)[None, :]
    j = np.arange(POOL_HALO)[None, :]
    main, halo, main_t, halo_t = [], [], [], []
    for w in POOL_WINDOWS:
        main.append(((r - c >= 0) & (r - c < w)) / w)
        halo.append((r + POOL_HALO - j < w) / w)
        main_t.append(((c - r >= 0) & (c - r < w)) / w)
        halo_t.append((tm + j - r < w) / w)
    return tuple(jnp.asarray(np.stack(m), BF16) for m in (main, halo, main_t, halo_t))


def _pool_count_scale(i, tm, w):
    t = i * tm + lax.broadcasted_iota(jnp.int32, (tm, 1), 0)
    return w / jnp.minimum(t + 1, w).astype(F32)


def _pool_fwd(h, g, wp, scale):
    T, D = h.shape
    G, dg = len(POOL_WINDOWS), D // len(POOL_WINDOWS)
    tm = min(TOKEN_TILE // 2, T)
    hb = tm // POOL_HALO
    bm, bh, _, _ = _pool_bands(tm)

    def body(h_ref, hh_ref, g_ref, wp_ref, sc_ref, bm_ref, bh_ref, ho_ref, y_ref):
        i = pl.program_id(0)
        x = h_ref[...]
        u = _rms_fwd(x, g_ref[...])
        uh = _rms_fwd(hh_ref[...], g_ref[...]) * (i > 0).astype(F32)
        for gi, w in enumerate(POOL_WINDOWS):
            cols = slice(gi * dg, (gi + 1) * dg)
            ug = u[:, cols]
            hi, lo = _split_bf16(ug)
            hhi, hlo = _split_bf16(uh[:, cols])
            s = (_dot(bm_ref[gi], hi) + _dot(bm_ref[gi], lo)
                 + _dot(bh_ref[gi], hhi) + _dot(bh_ref[gi], hlo))
            y = (s * _pool_count_scale(i, tm, w) - ug).astype(BF16)
            y_ref[:, cols] = y
            ho_ref[:, cols] = x[:, cols] + _dot(y, wp_ref[gi]) * sc_ref[:, cols]

    row = pl.BlockSpec((tm, D), lambda i: (i, 0))
    return pl.pallas_call(
        body, name="pool_fwd", grid=(T // tm,),
        in_specs=[row, pl.BlockSpec((POOL_HALO, D), lambda i: (jnp.maximum(i * hb - 1, 0), 0)),
                  _full((1, D)), _full((G, dg, dg)), _full((1, D)),
                  _full((G, tm, tm)), _full((G, tm, POOL_HALO))],
        out_specs=[row, row],
        out_shape=[jax.ShapeDtypeStruct((T, D), F32), jax.ShapeDtypeStruct((T, D), BF16)],
        compiler_params=_cparams("parallel"),
    )(h, h, g, wp, scale, bm, bh)


def _pool_bwd(dh, h, g, y, wp, scale):
    T, D = h.shape
    G, dg = len(POOL_WINDOWS), D // len(POOL_WINDOWS)
    tm = min(TOKEN_TILE // 2, T)
    hb = tm // POOL_HALO
    nt = T // tm
    _, _, bmt, bht = _pool_bands(tm)

    def body(dh_ref, dhn_ref, h_ref, g_ref, y_ref, wp_ref, sc_ref, bmt_ref, bht_ref,
             o_ref, dg_ref, dwp_ref, dsc_ref, du_sc):
        i = pl.program_id(0)

        @pl.when(i == 0)
        def _():
            dg_ref[...] = jnp.zeros_like(dg_ref)
            dwp_ref[...] = jnp.zeros_like(dwp_ref)
            dsc_ref[...] = jnp.zeros_like(dsc_ref)

        dho = dh_ref[...]
        dz = dho * sc_ref[...]
        dzn = dhn_ref[...] * sc_ref[...] * (i < nt - 1).astype(F32)
        for gi, w in enumerate(POOL_WINDOWS):
            cols = slice(gi * dg, (gi + 1) * dg)
            yg = y_ref[:, cols]
            dzg = dz[:, cols].astype(BF16)
            dsc_ref[:, cols] += jnp.sum(dho[:, cols] * _dot(yg, wp_ref[gi]), axis=0, keepdims=True)
            dwp_ref[gi] += _dot_tn(yg, dzg)
            dy = _dot_nt(dzg, wp_ref[gi])
            dyn = _dot_nt(dzn[:, cols].astype(BF16), wp_ref[gi])
            hi, lo = _split_bf16(dy * _pool_count_scale(i, tm, w))
            nhi, nlo = _split_bf16(dyn)
            du_sc[:, cols] = (_dot(bmt_ref[gi], hi) + _dot(bmt_ref[gi], lo)
                              + _dot(bht_ref[gi], nhi) + _dot(bht_ref[gi], nlo) - dy)
        dx, dgp = _rms_bwd(du_sc[...], h_ref[...], g_ref[...])
        o_ref[...] = dho + dx
        dg_ref[...] += dgp

    row = pl.BlockSpec((tm, D), lambda i: (i, 0))
    return pl.pallas_call(
        body, name="pool_bwd", grid=(nt,),
        in_specs=[row, pl.BlockSpec((POOL_HALO, D), lambda i: (jnp.minimum((i + 1) * hb, T // POOL_HALO - 1), 0)),
                  row, _full((1, D)), row, _full((G, dg, dg)), _full((1, D)),
                  _full((G, tm, tm)), _full((G, tm, POOL_HALO))],
        out_specs=[row, _full((1, D)), _full((G, dg, dg)), _full((1, D))],
        out_shape=[jax.ShapeDtypeStruct((T, D), F32), jax.ShapeDtypeStruct((1, D), F32),
                   jax.ShapeDtypeStruct((G, dg, dg), F32), jax.ShapeDtypeStruct((1, D), F32)],
        scratch_shapes=[pltpu.VMEM((tm, D), F32)],
        compiler_params=_cparams("arbitrary"),
    )(dh, dh, h, g, y, wp, scale, bmt, bht)


def _rope_tables(T):
    pos = jnp.arange(T, dtype=F32)
    inv_freq = ROPE_THETA ** (-jnp.arange(0, D_ROPE, 2, dtype=F32) / D_ROPE)
    ang = pos[:, None] * inv_freq[None, :]
    cos2 = jnp.tile(jnp.cos(ang), (1, 2))
    sin2 = jnp.tile(jnp.sin(ang), (1, 2))
    pad = jnp.zeros((T, D_HEAD_PAD - D_QK), F32)
    ca_q = jnp.concatenate([jnp.ones((T, D_NOPE), F32), cos2, pad], axis=1)
    ca_k = jnp.concatenate([jnp.zeros((T, D_NOPE), F32), cos2, pad], axis=1)
    sb = jnp.concatenate([jnp.zeros((T, D_NOPE), F32), sin2, pad], axis=1)
    return ca_q, ca_k, sb


def _rope_weight_pair(w):
    half = D_ROPE // 2
    z_pad = jnp.zeros(w.shape[:-1] + (D_HEAD_PAD - D_QK,), w.dtype)
    z_nope = jnp.zeros(w.shape[:-1] + (D_NOPE,), w.dtype)
    wa = jnp.concatenate([w, z_pad], axis=-1)
    wb = jnp.concatenate([z_nope, -w[..., D_NOPE + half:], w[..., D_NOPE:D_NOPE + half], z_pad], axis=-1)
    return wa, wb


def _rope_weight_pair_grad(dwa, dwb):
    half = D_ROPE // 2
    d1 = dwa[..., D_NOPE:D_NOPE + half] + dwb[..., D_NOPE + half:D_QK]
    d2 = dwa[..., D_NOPE + half:D_QK] - dwb[..., D_NOPE:D_NOPE + half]
    return jnp.concatenate([dwa[..., :D_NOPE], d1, d2], axis=-1)


def _q_proj(cq, qg, wa, wb, ca, sb):
    T, R = cq.shape
    tm = min(TOKEN_TILE, T)
    P = D_HEAD_PAD
    GP = HEAD_GROUP * P

    def body(cq_ref, qg_ref, wa_ref, wb_ref, ca_ref, sb_ref, q_ref):
        c = _rms_fwd(cq_ref[...], qg_ref[...]).astype(BF16)
        ca = jnp.tile(ca_ref[...], (1, HEAD_GROUP))
        sb = jnp.tile(sb_ref[...], (1, HEAD_GROUP))
        for grp in range(N_HEADS // HEAD_GROUP):
            cols = slice(grp * GP, (grp + 1) * GP)
            q_ref[:, cols] = (_dot(c, wa_ref[:, cols]) * ca + _dot(c, wb_ref[:, cols]) * sb).astype(BF16)

    tok = pl.BlockSpec((tm, P), lambda i: (i, 0))
    wfull = _full((R, N_HEADS * P))
    return pl.pallas_call(
        body, name="q_proj", grid=(T // tm,),
        in_specs=[pl.BlockSpec((tm, R), lambda i: (i, 0)), _full((1, R)), wfull, wfull, tok, tok],
        out_specs=pl.BlockSpec((tm, N_HEADS * P), lambda i: (i, 0)),
        out_shape=jax.ShapeDtypeStruct((T, N_HEADS * P), BF16),
        compiler_params=_cparams("parallel"),
    )(cq, qg, wa, wb, ca, sb)


def _q_proj_bwd(dq, cq, qg, wa, wb, ca, sb):
    T, R = cq.shape
    tm = min(TOKEN_TILE, T)
    P = D_HEAD_PAD
    GP = HEAD_GROUP * P

    def body(dq_ref, cq_ref, qg_ref, wa_ref, wb_ref, ca_ref, sb_ref, dcq_ref, dqg_ref, dwa_ref, dwb_ref):
        @pl.when(pl.program_id(0) == 0)
        def _():
            for r in (dqg_ref, dwa_ref, dwb_ref):
                r[...] = jnp.zeros_like(r)

        cq_f = cq_ref[...]
        cqn = _rms_fwd(cq_f, qg_ref[...]).astype(BF16)
        ca = jnp.tile(ca_ref[...], (1, HEAD_GROUP))
        sb = jnp.tile(sb_ref[...], (1, HEAD_GROUP))
        acc = None
        for grp in range(N_HEADS // HEAD_GROUP):
            cols = slice(grp * GP, (grp + 1) * GP)
            d = dq_ref[cols, :].T
            da = (d * ca).astype(BF16)
            db = (d * sb).astype(BF16)
            part = _dot_nt(da, wa_ref[:, cols]) + _dot_nt(db, wb_ref[:, cols])
            acc = part if acc is None else acc + part
            dwa_ref[:, cols] += _dot_tn(cqn, da)
            dwb_ref[:, cols] += _dot_tn(cqn, db)
        dx, dg = _rms_bwd(acc, cq_f, qg_ref[...])
        dcq_ref[...] = dx
        dqg_ref[...] += dg

    tok = pl.BlockSpec((tm, P), lambda i: (i, 0))
    rr = pl.BlockSpec((tm, R), lambda i: (i, 0))
    wfull = _full((R, N_HEADS * P))
    return pl.pallas_call(
        body, name="q_proj_bwd", grid=(T // tm,),
        in_specs=[pl.BlockSpec((N_HEADS * P, tm), lambda i: (0, i)), rr, _full((1, R)), wfull, wfull, tok, tok],
        out_specs=[rr, _full((1, R)), wfull, wfull],
        out_shape=[jax.ShapeDtypeStruct((T, R), F32), jax.ShapeDtypeStruct((1, R), F32),
                   jax.ShapeDtypeStruct((R, N_HEADS * P), F32), jax.ShapeDtypeStruct((R, N_HEADS * P), F32)],
        compiler_params=_cparams("arbitrary"),
    )(dq, cq, qg, wa, wb, ca, sb)


def _kv_proj(h, g_in, wka, wkb, g_c, wuk, wuv, ca, sb):
    T, D = h.shape
    tm = min(TOKEN_TILE, T)
    P, C = D_HEAD_PAD, D_NOPE

    def body(h_ref, gi_ref, wka_ref, wkb_ref, gc_ref, wuk_ref, wuv_ref, ca_ref, sb_ref, k_ref, v_ref, craw_ref):
        u = _rms_fwd(h_ref[...], gi_ref[...]).astype(BF16)
        kva = _dot(u, wka_ref[...])
        kvb = _dot(u, wkb_ref[...])
        craw = kva[:, :C]
        craw_ref[...] = craw
        c = _rms_fwd(craw, gc_ref[...]).astype(BF16)
        kr = kva * ca_ref[...] + kvb * sb_ref[...]
        kn = _dot(c, wuk_ref[...])
        for hh in range(N_HEADS):
            k_ref[:, hh * P:(hh + 1) * P] = (kn[:, hh * P:(hh + 1) * P] + kr).astype(BF16)
        v_ref[...] = _dot(c, wuv_ref[...]).astype(BF16)

    tok = pl.BlockSpec((tm, P), lambda i: (i, 0))
    return pl.pallas_call(
        body, name="kv_proj", grid=(T // tm,),
        in_specs=[pl.BlockSpec((tm, D), lambda i: (i, 0)), _full((1, D)), _full((D, P)), _full((D, P)),
                  _full((1, C)), _full(wuk.shape), _full(wuv.shape), tok, tok],
        out_specs=[pl.BlockSpec((tm, N_HEADS * P), lambda i: (i, 0)),
                   pl.BlockSpec((tm, N_HEADS * D_V), lambda i: (i, 0)), pl.BlockSpec((tm, C), lambda i: (i, 0))],
        out_shape=[jax.ShapeDtypeStruct((T, N_HEADS * P), BF16), jax.ShapeDtypeStruct((T, N_HEADS * D_V), BF16),
                   jax.ShapeDtypeStruct((T, C), F32)],
        compiler_params=_cparams("parallel"),
    )(h, g_in, wka, wkb, g_c, wuk, wuv, ca, sb)


def _kv_proj_bwd(dks, dvs, dh, h, g_in, wka, wkb, craw, g_c, wuk, wuv, ca, sb):
    T, D = h.shape
    tm = min(TOKEN_TILE // 2, T)
    P, C = D_HEAD_PAD, D_NOPE
    nl = len(dks)

    def body(*refs):
        dk_refs, dv_refs = refs[:nl], refs[nl:2 * nl]
        (dh_ref, h_ref, gi_ref, wka_ref, wkb_ref, craw_ref, gc_ref, wuk_ref, wuv_ref,
         ca_ref, sb_ref, o_ref, dgi_ref, dwka_ref, dwkb_ref, dgc_ref, dwuk_ref, dwuv_ref) = refs[2 * nl:]

        @pl.when(pl.program_id(0) == 0)
        def _():
            for r in (dgi_ref, dwka_ref, dwkb_ref, dgc_ref, dwuk_ref, dwuv_ref):
                r[...] = jnp.zeros_like(r)

        x = h_ref[...]
        u = _rms_fwd(x, gi_ref[...]).astype(BF16)
        craw = craw_ref[...]
        c = _rms_fwd(craw, gc_ref[...]).astype(BF16)
        dkf = sum(r[...] for r in dk_refs[1:]) + dk_refs[0][...]
        dkb = dkf.astype(BF16)
        dvb = (sum(r[...] for r in dv_refs[1:]) + dv_refs[0][...]).astype(BF16)
        dwuk_ref[...] += _dot_tn(c, dkb)
        dwuv_ref[...] += _dot_tn(c, dvb)
        dc = _dot_nt(dkb, wuk_ref[...]) + _dot_nt(dvb, wuv_ref[...])
        dkr = dkf[:, :P]
        for hh in range(1, N_HEADS):
            dkr = dkr + dkf[:, hh * P:(hh + 1) * P]
        dcraw, dgc = _rms_bwd(dc, craw, gc_ref[...])
        dgc_ref[...] += dgc
        dkva = jnp.concatenate([dcraw, (dkr * ca_ref[...])[:, C:]], axis=1).astype(BF16)
        dkvb = (dkr * sb_ref[...]).astype(BF16)
        dwka_ref[...] += _dot_tn(u, dkva)
        dwkb_ref[...] += _dot_tn(u, dkvb)
        du = _dot_nt(dkva, wka_ref[...]) + _dot_nt(dkvb, wkb_ref[...])
        dx, dgi = _rms_bwd(du, x, gi_ref[...])
        dgi_ref[...] += dgi
        o_ref[...] = dh_ref[...] + dx

    row = pl.BlockSpec((tm, D), lambda i: (i, 0))
    tok = pl.BlockSpec((tm, P), lambda i: (i, 0))
    return pl.pallas_call(
        body, name="kv_proj_bwd", grid=(T // tm,),
        in_specs=[pl.BlockSpec((tm, N_HEADS * P), lambda i: (i, 0))] * nl
        + [pl.BlockSpec((tm, N_HEADS * D_V), lambda i: (i, 0))] * nl
        + [row, row, _full((1, D)), _full((D, P)), _full((D, P)), pl.BlockSpec((tm, C), lambda i: (i, 0)),
           _full((1, C)), _full(wuk.shape), _full(wuv.shape), tok, tok],
        out_specs=[row, _full((1, D)), _full((D, P)), _full((D, P)), _full((1, C)),
                   _full(wuk.shape), _full(wuv.shape)],
        out_shape=[jax.ShapeDtypeStruct((T, D), F32), jax.ShapeDtypeStruct((1, D), F32),
                   jax.ShapeDtypeStruct((D, P), F32), jax.ShapeDtypeStruct((D, P), F32),
                   jax.ShapeDtypeStruct((1, C), F32), jax.ShapeDtypeStruct(wuk.shape, F32),
                   jax.ShapeDtypeStruct(wuv.shape, F32)],
        compiler_params=_cparams("arbitrary"),
    )(*dks, *dvs, dh, h, g_in, wka, wkb, craw, g_c, wuk, wuv, ca, sb)


_ATTN_SCALE = D_QK ** -0.5
_LOG2_E = 1.4426950408889634
_LN_2 = 0.6931471805599453
_ATTN_SCALE_LOG2 = _ATTN_SCALE * _LOG2_E


def _flash_fwd(q, k, v):
    T = q.shape[0]
    t = min(ATTN_TILE, T // 2)
    tq = 2 * t
    P = D_HEAD_PAD

    def body(q_ref, k_ref, v_ref, o_ref, lse_ref, m_sc, l_sc, acc_sc, s0_sc, s1_sc, p0_sc, p1_sc, a0_sc, a1_sc):
        qi = pl.program_id(1)
        n = 2 * (qi + 1)
        s_sc, p_sc, a_sc = (s0_sc, s1_sc), (p0_sc, p1_sc), (a0_sc, a1_sc)
        m_sc[...] = jnp.full_like(m_sc, NEG_BIG)
        l_sc[...] = jnp.zeros_like(l_sc)
        acc_sc[...] = jnp.zeros_like(acc_sc)

        def rows_of(c):
            return pl.ds(pl.multiple_of(c * t, t), t)

        def scores(c, slot):
            s_sc[slot][...] = _dot_nt(k_ref[rows_of(c), :], q_ref[...])

        def softmax(slot, key_offset):
            for half in range(2):
                cols = slice(half * t, (half + 1) * t)
                s_t = s_sc[slot][:, cols]
                if key_offset is not None:
                    rows = lax.broadcasted_iota(jnp.int32, (t, t), 0) + key_offset
                    q_pos = lax.broadcasted_iota(jnp.int32, (t, t), 1) + half * t
                    s_t = jnp.where(rows <= q_pos, s_t, NEG_BIG)
                m_prev = m_sc[:, cols]
                m_new = jnp.maximum(m_prev, jnp.max(s_t, axis=0, keepdims=True))
                p_t = jnp.exp2(s_t - m_new)
                alpha = jnp.exp2(m_prev - m_new)
                l_sc[:, cols] = alpha * l_sc[:, cols] + jnp.sum(p_t, axis=0, keepdims=True)
                m_sc[:, cols] = m_new
                p_sc[slot][:, cols] = p_t.astype(BF16)
                a_sc[slot][:, cols] = alpha

        def values(c, slot):
            acc_sc[...] = a_sc[slot][...] * acc_sc[...] + _dot_tn(v_ref[rows_of(c), :], p_sc[slot][...])

        def stage(c, slot, first=False, last=False, key_offset=None):
            if not first:
                values(c - 1, 1 - slot)
            if not last:
                scores(c + 1, 1 - slot)
            softmax(slot, key_offset)

        def drain():
            stage(n - 2, 0, key_offset=0)
            stage(n - 1, 1, last=True, key_offset=t)
            values(n - 1, 1)

        scores(0, 0)

        @pl.when(qi == 0)
        def _():
            stage(0, 0, first=True, key_offset=0)
            stage(1, 1, last=True, key_offset=t)
            values(1, 1)

        @pl.when(qi > 0)
        def _():
            stage(0, 0, first=True)

            def pair(j, carry):
                stage(1 + 2 * j, 1)
                stage(2 + 2 * j, 0)
                return carry

            lax.fori_loop(0, qi - 1, pair, 0)
            stage(n - 3, 1)
            drain()

        l = l_sc[...]
        o_ref[...] = (acc_sc[...] / l).T.astype(BF16)
        lse_ref[...] = m_sc[...] + jnp.log(l) * _LOG2_E

    return pl.pallas_call(
        body, name="flash_fwd", grid=(N_HEADS, T // tq),
        in_specs=[pl.BlockSpec((tq, P), lambda hh, i: (i, hh)), pl.BlockSpec((T, P), lambda hh, i: (0, hh)),
                  pl.BlockSpec((T, D_V), lambda hh, i: (0, hh))],
        out_specs=[pl.BlockSpec((tq, D_V), lambda hh, i: (i, hh)),
                   pl.BlockSpec((None, None, 1, tq), lambda hh, i: (hh, i, 0, 0))],
        out_shape=[jax.ShapeDtypeStruct((T, N_HEADS * D_V), BF16),
                   jax.ShapeDtypeStruct((N_HEADS, T // tq, 1, tq), F32)],
        scratch_shapes=[pltpu.VMEM((1, tq), F32), pltpu.VMEM((1, tq), F32), pltpu.VMEM((D_V, tq), F32),
                        pltpu.VMEM((t, tq), F32), pltpu.VMEM((t, tq), F32), pltpu.VMEM((t, tq), BF16),
                        pltpu.VMEM((t, tq), BF16), pltpu.VMEM((1, tq), F32), pltpu.VMEM((1, tq), F32)],
        compiler_params=_cparams("parallel", "arbitrary"),
    )(q, k, v)


def _attn_delta(o, do):
    T = o.shape[0]
    t = min(ATTN_TILE, T)

    def body(o_ref, do_ref, out_ref):
        ones = jnp.ones((8, D_V), BF16)
        for hh in range(N_HEADS):
            cols = slice(hh * D_V, (hh + 1) * D_V)
            hi, lo = _split_bf16(o_ref[:, cols].astype(F32) * do_ref[:, cols].astype(F32))
            out_ref[hh] = (_dot_nt(ones, hi) + _dot_nt(ones, lo))[0:1]

    tok = pl.BlockSpec((t, N_HEADS * D_V), lambda i: (i, 0))
    return pl.pallas_call(
        body, name="attn_delta", grid=(T // t,), in_specs=[tok, tok],
        out_specs=pl.BlockSpec((N_HEADS, None, 1, t), lambda i: (0, i, 0, 0)),
        out_shape=jax.ShapeDtypeStruct((N_HEADS, T // t, 1, t), F32),
        compiler_params=_cparams("parallel"),
    )(o, do)


def _flash_bwd(q, k, v, do, lse_row, delta_row):
    T = q.shape[0]
    t = min(ATTN_TILE, T // 2)
    tk = 2 * t
    nq = T // t
    P = D_HEAD_PAD

    def body(k_ref, v_ref, q_ref, do_ref, lse_ref, delta_ref, dqt_ref, dk_ref, dv_ref, dqt_sc, kt_sc,
             s0_sc, s1_sc, dp0_sc, dp1_sc, p0_sc, p1_sc, ds0_sc, ds1_sc):
        ki = pl.program_id(1)
        q0 = 2 * ki
        n = nq - q0
        s_sc, dp_sc, p_sc, ds_sc = (s0_sc, s1_sc), (dp0_sc, dp1_sc), (p0_sc, p1_sc), (ds0_sc, ds1_sc)
        kt_sc[...] = k_ref[...].astype(F32).T.astype(BF16)
        dk_ref[...] = jnp.zeros_like(dk_ref)
        dv_ref[...] = jnp.zeros_like(dv_ref)

        @pl.when(ki == 0)
        def _():
            dqt_sc[...] = jnp.zeros_like(dqt_sc)

        def rows_of(c):
            return pl.ds(pl.multiple_of((q0 + c) * t, t), t)

        def products(c, slot):
            s_sc[slot][...] = _dot_nt(k_ref[...], q_ref[rows_of(c), :])
            dp_sc[slot][...] = _dot_nt(v_ref[...], do_ref[rows_of(c), :])

        def elementwise(c, slot, query_offset):
            p_t = jnp.exp2(s_sc[slot][...] - lse_ref[q0 + c])
            if query_offset is not None:
                cols = lax.broadcasted_iota(jnp.int32, (tk, t), 1) + query_offset
                p_t = jnp.where(lax.broadcasted_iota(jnp.int32, (tk, t), 0) <= cols, p_t, 0.0)
            p_sc[slot][...] = p_t.astype(BF16)
            ds_sc[slot][...] = (p_t * (dp_sc[slot][...] - delta_ref[q0 + c])).astype(BF16)

        def gradients(c, slot):
            dv_ref[...] += _dot(p_sc[slot][...], do_ref[rows_of(c), :])
            ds_t = ds_sc[slot][...]
            dk_ref[...] += _dot(ds_t, q_ref[rows_of(c), :])
            dqt_sc[q0 + c] += _dot(kt_sc[...], ds_t)

        def stage(c, slot, first=False, last=False, query_offset=None):
            if not first:
                gradients(c - 1, 1 - slot)
            if not last:
                products(c + 1, 1 - slot)
            elementwise(c, slot, query_offset)

        products(0, 0)

        @pl.when(n == 2)
        def _():
            stage(0, 0, first=True, query_offset=0)
            stage(1, 1, last=True, query_offset=t)
            gradients(1, 1)

        @pl.when(n > 2)
        def _():
            stage(0, 0, first=True, query_offset=0)
            stage(1, 1, query_offset=t)

            def pair(j, carry):
                stage(2 + 2 * j, 0)
                stage(3 + 2 * j, 1)
                return carry

            lax.fori_loop(0, (n - 4) // 2, pair, 0)
            stage(n - 2, 0)
            stage(n - 1, 1, last=True)
            gradients(n - 1, 1)

        dqt_ref[:, :t] = dqt_sc[q0] * _ATTN_SCALE
        dqt_ref[:, t:] = dqt_sc[q0 + 1] * _ATTN_SCALE
        dk_ref[...] = dk_ref[...] * _LN_2

    kb = pl.BlockSpec((tk, P), lambda hh, i: (i, hh))
    vb = pl.BlockSpec((tk, D_V), lambda hh, i: (i, hh))
    stat = pl.BlockSpec((None, nq, 1, t), lambda hh, i: (hh, 0, 0, 0))
    return pl.pallas_call(
        body, name="flash_bwd", grid=(N_HEADS, T // tk),
        in_specs=[kb, vb,
                  pl.BlockSpec((T, P), lambda hh, i: (0, hh), pipeline_mode=pl.Buffered(1)),
                  pl.BlockSpec((T, D_V), lambda hh, i: (0, hh), pipeline_mode=pl.Buffered(1)),
                  stat, stat],
        out_specs=[pl.BlockSpec((P, tk), lambda hh, i: (hh, i)), kb, vb],
        out_shape=[jax.ShapeDtypeStruct((N_HEADS * P, T), F32), jax.ShapeDtypeStruct((T, N_HEADS * P), F32),
                   jax.ShapeDtypeStruct((T, N_HEADS * D_V), F32)],
        scratch_shapes=[pltpu.VMEM((nq, P, t), F32), pltpu.VMEM((P, tk), BF16)]
        + [pltpu.VMEM((tk, t), F32)] * 4 + [pltpu.VMEM((tk, t), BF16)] * 4,
        compiler_params=_cparams("arbitrary", "arbitrary"),
    )(k, v, q, do, lse_row, delta_row)


def _loss_head(h, g, target):
    T, D = h.shape
    tm = min(TOKEN_TILE, T)

    def body(h_ref, g_ref, t_ref, dh_ref, loss_ref, dg_ref):
        @pl.when(pl.program_id(0) == 0)
        def _():
            loss_ref[...] = jnp.zeros_like(loss_ref)
            dg_ref[...] = jnp.zeros_like(dg_ref)

        x = h_ref[...]
        err = _rms_fwd(x, g_ref[...]) - t_ref[...]
        per_tok = jnp.mean(err * err, axis=-1, keepdims=True)
        loss_ref[...] += 0.5 * jnp.sum(per_tok, axis=0, keepdims=True)
        dx, dg = _rms_bwd(err * (1.0 / D), x, g_ref[...])
        dh_ref[...] = dx
        dg_ref[...] += dg

    row = pl.BlockSpec((tm, D), lambda i: (i, 0))
    return pl.pallas_call(
        body, name="loss_head", grid=(T // tm,),
        in_specs=[row, _full((1, D)), row], out_specs=[row, _full((1, 128)), _full((1, D))],
        out_shape=[jax.ShapeDtypeStruct((T, D), F32), jax.ShapeDtypeStruct((1, 128), F32),
                   jax.ShapeDtypeStruct((1, D), F32)],
        compiler_params=_cparams("arbitrary"),
    )(h, g, target)


def _sum_parts(parts, tr, name):
    _, R, C = parts.shape

    def body(p_ref, o_ref):
        acc = p_ref[0].astype(F32)
        for j in range(1, N_DEV):
            acc = acc + p_ref[j].astype(F32)
        o_ref[...] = acc

    return pl.pallas_call(
        body, name=name, grid=(R // tr,),
        in_specs=[pl.BlockSpec((N_DEV, tr, C), lambda i: (0, i, 0))],
        out_specs=pl.BlockSpec((tr, C), lambda i: (i, 0)),
        out_shape=jax.ShapeDtypeStruct((R, C), F32),
        compiler_params=_cparams("parallel"),
    )(parts)


def _adamw(w, g, m, v):
    R, C = w.shape
    tr = _row_tile(R, TOKEN_TILE)

    def body(w_ref, g_ref, m_ref, v_ref, d_ref, mo_ref, vo_ref):
        gg = g_ref[...]
        mn = ADAM_B1 * m_ref[...] + (1.0 - ADAM_B1) * gg
        vn = ADAM_B2 * v_ref[...] + (1.0 - ADAM_B2) * (gg * gg)
        m_hat = mn / (1.0 - ADAM_B1 ** ADAM_STEP)
        v_hat = vn / (1.0 - ADAM_B2 ** ADAM_STEP)
        d_ref[...] = -ADAM_LR * (m_hat / (jnp.sqrt(v_hat) + ADAM_EPS) + ADAM_WD * w_ref[...])
        mo_ref[...] = mn
        vo_ref[...] = vn

    blk = pl.BlockSpec((tr, C), lambda i: (i, 0))
    return pl.pallas_call(
        body, name="adamw", grid=(R // tr,), in_specs=[blk] * 4, out_specs=[blk] * 3,
        out_shape=[jax.ShapeDtypeStruct((R, C), F32)] * 3,
        compiler_params=_cparams("parallel"),
    )(w, g, m, v)


def _adamw_nd(w, g, m, v):
    shape = w.shape
    two_d = (1, shape[0]) if len(shape) == 1 else (int(np.prod(shape[:-1])), shape[-1])
    outs = _adamw(w.reshape(two_d), g.reshape(two_d), m.reshape(two_d), v.reshape(two_d))
    return tuple(o.reshape(shape) for o in outs)


def _f32_as_bf16_pairs(a):
    return lax.bitcast_convert_type(a, BF16).reshape(a.shape[:-1] + (a.shape[-1] * 2,))


def _bf16_pairs_as_f32(a):
    return lax.bitcast_convert_type(a.reshape(a.shape[:-1] + (a.shape[-1] // 2, 2)), F32)


def _pack_misc(w_o, w_dq, w_uq, w_dkv, pool_w, pool_scale):
    lead = w_o.shape[:-3]
    rows = [w_o, w_dq, w_uq, w_dkv, pool_w]
    flat = [r.astype(BF16).reshape(lead + (-1, REP_COLS)) for r in rows]
    ps = _f32_as_bf16_pairs(pool_scale.astype(F32)).reshape(lead + (1, -1))
    ps = jnp.concatenate([ps, jnp.zeros(lead + (1, REP_COLS - ps.shape[-1]), BF16)], axis=-1)
    used = sum(f.shape[-2] for f in flat) + 1
    pad = jnp.zeros(lead + (MISC_ROWS - used, REP_COLS), BF16)
    return jnp.concatenate(flat + [ps, pad], axis=-2)


def _unpack_misc(buf, shapes):
    out, r0 = [], 0
    for shp in shapes[:-1]:
        n = int(np.prod(shp)) // REP_COLS
        out.append(buf[:, r0:r0 + n].reshape((N_DEV,) + shp))
        r0 += n
    n_ps = int(np.prod(shapes[-1]))
    out.append(_bf16_pairs_as_f32(buf[:, r0, :2 * n_ps]).reshape((N_DEV,) + shapes[-1]))
    return out


def _cat_dev(a, axis):
    a = jnp.moveaxis(a, 0, axis)
    return a.reshape(a.shape[:axis] + (a.shape[axis] * a.shape[axis + 1],) + a.shape[axis + 2:])


def _split_dev(a, axis):
    a = a.reshape(a.shape[:axis] + (N_DEV, a.shape[axis] // N_DEV) + a.shape[axis + 1:])
    return jnp.moveaxis(a, axis, 0)


def kernel(x, ffn_pre_norm, ffn_pre_wg, ffn_pre_wu, ffn_pre_wd, mix_norm, ffn_post_norm, ffn_post_wg, ffn_post_wu, ffn_post_wd, pool_w, pool_scale, kv_in_norm, w_dkv, ckv_norm, w_uk, w_uv, q_lora_norm, w_dq, w_uq, w_o, final_norm, loss_target, m_ffn_pre_norm, m_ffn_pre_wg, m_ffn_pre_wu, m_ffn_pre_wd, m_mix_norm, m_ffn_post_norm, m_ffn_post_wg, m_ffn_post_wu, m_ffn_post_wd, m_pool_w, m_pool_scale, m_kv_in_norm, m_w_dkv, m_ckv_norm, m_w_uk, m_w_uv, m_q_lora_norm, m_w_dq, m_w_uq, m_w_o, m_final_norm, v_ffn_pre_norm, v_ffn_pre_wg, v_ffn_pre_wu, v_ffn_pre_wd, v_mix_norm, v_ffn_post_norm, v_ffn_post_wg, v_ffn_post_wu, v_ffn_post_wd, v_pool_w, v_pool_scale, v_kv_in_norm, v_w_dkv, v_ckv_norm, v_w_uk, v_w_uv, v_q_lora_norm, v_w_dq, v_w_uq, v_w_o, v_final_norm):
    weights = dict(ffn_pre_norm=ffn_pre_norm, ffn_pre_wg=ffn_pre_wg, ffn_pre_wu=ffn_pre_wu, ffn_pre_wd=ffn_pre_wd,
                   mix_norm=mix_norm, ffn_post_norm=ffn_post_norm, ffn_post_wg=ffn_post_wg,
                   ffn_post_wu=ffn_post_wu, ffn_post_wd=ffn_post_wd, pool_w=pool_w, pool_scale=pool_scale,
                   kv_in_norm=kv_in_norm, w_dkv=w_dkv, ckv_norm=ckv_norm, w_uk=w_uk, w_uv=w_uv,
                   q_lora_norm=q_lora_norm, w_dq=w_dq, w_uq=w_uq, w_o=w_o, final_norm=final_norm)
    moments_m = dict(ffn_pre_norm=m_ffn_pre_norm, ffn_pre_wg=m_ffn_pre_wg, ffn_pre_wu=m_ffn_pre_wu,
                     ffn_pre_wd=m_ffn_pre_wd, mix_norm=m_mix_norm, ffn_post_norm=m_ffn_post_norm,
                     ffn_post_wg=m_ffn_post_wg, ffn_post_wu=m_ffn_post_wu, ffn_post_wd=m_ffn_post_wd,
                     pool_w=m_pool_w, pool_scale=m_pool_scale, kv_in_norm=m_kv_in_norm, w_dkv=m_w_dkv,
                     ckv_norm=m_ckv_norm, w_uk=m_w_uk, w_uv=m_w_uv, q_lora_norm=m_q_lora_norm, w_dq=m_w_dq,
                     w_uq=m_w_uq, w_o=m_w_o, final_norm=m_final_norm)
    moments_v = dict(ffn_pre_norm=v_ffn_pre_norm, ffn_pre_wg=v_ffn_pre_wg, ffn_pre_wu=v_ffn_pre_wu,
                     ffn_pre_wd=v_ffn_pre_wd, mix_norm=v_mix_norm, ffn_post_norm=v_ffn_post_norm,
                     ffn_post_wg=v_ffn_post_wg, ffn_post_wu=v_ffn_post_wu, ffn_post_wd=v_ffn_post_wd,
                     pool_w=v_pool_w, pool_scale=v_pool_scale, kv_in_norm=v_kv_in_norm, w_dkv=v_w_dkv,
                     ckv_norm=v_ckv_norm, w_uk=v_w_uk, w_uv=v_w_uv, q_lora_norm=v_q_lora_norm, w_dq=v_w_dq,
                     w_uq=v_w_uq, w_o=v_w_o, final_norm=v_final_norm)
    order = list(weights)

    T, D = x.shape[1], x.shape[2]
    depth = ffn_pre_norm.shape[0]
    n_a = pool_w.shape[0]
    n_b = depth - n_a
    fs = ffn_pre_wd.shape[1]
    F = fs * N_DEV
    n_ffn = 2 * depth
    t_attn = min(ATTN_TILE, T)

    ffn_local = [
        jnp.stack([jnp.swapaxes(wg[l], 0, 1), jnp.swapaxes(wu[l], 0, 1), wd[l]]).astype(BF16)
        for l in range(depth)
        for wg, wu, wd in ((ffn_pre_wg, ffn_pre_wu, ffn_pre_wd), (ffn_post_wg, ffn_post_wu, ffn_post_wd))
    ]
    misc_local = _pack_misc(w_o, w_dq, w_uq.reshape(n_b, w_uq.shape[1], -1), w_dkv, pool_w, pool_scale)
    misc_shapes = (w_o.shape, w_dq.shape, (n_b, w_uq.shape[1], N_HEADS * D_QK), w_dkv.shape, pool_w.shape,
                   pool_scale.shape)
    (w0_all,) = _exchange([(ffn_local[0], "gather_mid")], "comm_all_gather")
    walls = [w0_all.reshape(3, F, D)] + [None] * (n_ffn - 1)

    def vec(a):
        return a.reshape(1, -1)

    def ffn_stage(e, h_in, norm):
        carry = [(ffn_local[e + 1], "gather_mid")] if e + 1 < n_ffn else []
        if e == 0:
            carry.append((misc_local, "to_all"))
        outs = _ffn_fwd(h_in, norm, walls[e], carry)
        if carry:
            walls[e + 1] = outs[3].reshape(3, F, D)
        return outs

    h = x.reshape(T, D)
    stage0 = ffn_stage(0, h, vec(ffn_pre_norm[0]))
    misc_all = stage0[4]
    o_blk, dq_blk, uq_blk, dkv_blk, pw_blk, ps_blk = _unpack_misc(misc_all, misc_shapes)
    w_o_f = _cat_dev(o_blk, 1)
    w_dq_f = _cat_dev(dq_blk, 1)
    w_uq_f = _cat_dev(uq_blk, 1).reshape(n_b, -1, N_HEADS, D_QK)
    w_dkv_f = _cat_dev(dkv_blk, 0)
    pool_w_f = _cat_dev(pw_blk, 2)
    pool_scale_f = _cat_dev(ps_blk, 1)
    rq = w_dq_f.shape[2]
    wqa, wqb = _rope_weight_pair(w_uq_f)
    wqa = wqa.reshape(n_b, rq, N_HEADS * D_HEAD_PAD)
    wqb = wqb.reshape(n_b, rq, N_HEADS * D_HEAD_PAD)
    wka, wkb = _rope_weight_pair(w_dkv_f)
    wuk = jnp.concatenate([w_uk, jnp.zeros_like(w_uk)], axis=-1).astype(BF16).reshape(D_NOPE, N_HEADS * D_HEAD_PAD)
    wuv = w_uv.astype(BF16).reshape(D_NOPE, N_HEADS * D_V)
    ca_q, ca_k, sb = _rope_tables(T)
    ca_q_scaled, sb_scaled = ca_q * _ATTN_SCALE_LOG2, sb * _ATTN_SCALE_LOG2

    saved = []
    k_all = v_all = craw = h_kv = None
    for l in range(depth):
        s = {"h0": h}
        h, s["g1"], s["u1"] = (stage0 if l == 0 else ffn_stage(2 * l, h, vec(ffn_pre_norm[l])))[:3]
        s["h1"] = h
        if l < n_a:
            h, s["y"] = _pool_fwd(h, vec(mix_norm[l]), pool_w_f[l], vec(pool_scale_f[l]))
        else:
            j = l - n_a
            s["cq"] = _mm_rows(h, w_dq_f[j], nt=False, out_dtype=F32, norm_g=vec(mix_norm[l]), name="q_down")
            s["q"] = _q_proj(s["cq"], vec(q_lora_norm[j]), wqa[j], wqb[j], ca_q_scaled, sb_scaled)
            s["o"], lse = _flash_fwd(s["q"], k_all, v_all)
            s["lse"] = lse.reshape(N_HEADS, T // t_attn, 1, t_attn)
            h = _mm_rows(s["o"], w_o_f[j], nt=False, out_dtype=F32, res=h, name="attn_out")
        s["h2"] = h
        h, s["g2"], s["u2"] = ffn_stage(2 * l + 1, h, vec(ffn_post_norm[l]))[:3]
        if l == n_a - 1:
            h_kv = h
            k_all, v_all, craw = _kv_proj(h, vec(kv_in_norm), wka, wkb, vec(ckv_norm), wuk, wuv, ca_k, sb)
        saved.append(s)

    dh, loss_part, d_final = _loss_head(h, vec(final_norm), loss_target.reshape(T, D))

    slabs, ffn_parts = [None] * n_ffn, [None] * n_ffn

    misc_parts = []
    big_rep_names = ["w_uk", "w_uv"]

    def packed_misc_grads():
        return _pack_misc(_split_dev(jnp.stack(d_wo), 1), _split_dev(jnp.stack(d_wdq), 1),
                          _split_dev(jnp.stack(d_wuq).reshape(n_b, rq, -1), 1), _split_dev(grads["w_dkv"], 0),
                          _split_dev(jnp.stack(d_pool_w), 2),
                          _split_dev(jnp.concatenate(d_pool_scale, axis=0), 1))

    def ffn_stage_bwd(e, dh_out, h_in, norm, gate, up):
        carry = [(slabs[e + 1].reshape(3, N_DEV, fs, D), "scatter_mid")] if e + 1 < n_ffn else []
        if e == 0:
            carry.append((packed_misc_grads(), "scatter_lead"))
        if e == 2 * n_a - 1:
            carry.append((jnp.concatenate([grads[n].reshape(-1, REP_COLS) for n in big_rep_names]), "to_all"))
        outs = _ffn_bwd_dx(dh_out, h_in, norm, gate, up, walls[e], carry)
        if e + 1 < n_ffn:
            ffn_parts[e + 1] = outs[6]
        if e == 0:
            misc_parts.insert(0, outs[7])
        if e == 2 * n_a - 1:
            misc_parts.append(outs[6 + len(carry) - 1])
        dh_in, dgt, dup, u_t, dy_t, dnorm = outs[:6]
        slabs[e] = _ffn_bwd_dw(dgt, dup, gate, up, u_t, dy_t)
        return dh_in, dnorm

    grads = {}
    d_pre, d_post, d_mix = [None] * depth, [None] * depth, [None] * depth
    d_pool_w, d_pool_scale = [None] * n_a, [None] * n_a
    d_qln, d_wdq, d_wuq, d_wo = [None] * n_b, [None] * n_b, [None] * n_b, [None] * n_b
    dks, dvs = [], []
    for l in reversed(range(depth)):
        s = saved[l]
        if l == n_a - 1:
            (dh, grads["kv_in_norm"], dwka, dwkb, grads["ckv_norm"], dwuk, dwuv) = _kv_proj_bwd(
                dks, dvs, dh, h_kv, vec(kv_in_norm), wka, wkb, craw, vec(ckv_norm), wuk, wuv, ca_k, sb)
            grads["w_dkv"] = _rope_weight_pair_grad(dwka, dwkb)
            grads["w_uk"] = dwuk.reshape(D_NOPE, N_HEADS, D_HEAD_PAD)[..., :D_NOPE]
            grads["w_uv"] = dwuv.reshape(D_NOPE, N_HEADS, D_V)
        dh, d_post[l] = ffn_stage_bwd(2 * l + 1, dh, s["h2"], vec(ffn_post_norm[l]), s["g2"], s["u2"])
        if l < n_a:
            dh, d_mix[l], d_pool_w[l], d_pool_scale[l] = _pool_bwd(
                dh, s["h1"], vec(mix_norm[l]), s["y"], pool_w_f[l], vec(pool_scale_f[l]))
        else:
            j = l - n_a
            d_wo[j] = _mm_tn(s["o"], dh, name="attn_out_dw")
            do = _mm_rows(dh, w_o_f[j], nt=True, out_dtype=BF16, name="attn_out_dx")
            delta_row = _attn_delta(s["o"], do)
            dq_t, dk_l, dv_l = _flash_bwd(s["q"], k_all, v_all, do, s["lse"], delta_row)
            dks.append(dk_l)
            dvs.append(dv_l)
            dcq, d_qln[j], dwa, dwb = _q_proj_bwd(dq_t, s["cq"], vec(q_lora_norm[j]), wqa[j], wqb[j], ca_q, sb)
            d_wuq[j] = _rope_weight_pair_grad(dwa.reshape(rq, N_HEADS, D_HEAD_PAD),
                                              dwb.reshape(rq, N_HEADS, D_HEAD_PAD))
            d_wdq[j] = _mm_tn(s["h1"], dcq, norm_g=vec(mix_norm[l]), name="q_down_dw")
            dh, d_mix[l] = _proj_bwd(dcq, w_dq_f[j], s["h1"], vec(mix_norm[l]), dh, "q_down_dx")
        dh, d_pre[l] = ffn_stage_bwd(2 * l, dh, s["h0"], vec(ffn_pre_norm[l]), s["g1"], s["u1"])
    grad_x = dh.reshape(x.shape)

    rep_names = ["ffn_pre_norm", "mix_norm", "ffn_post_norm", "kv_in_norm", "ckv_norm", "q_lora_norm", "final_norm"]
    grads["ffn_pre_norm"] = jnp.concatenate(d_pre, axis=0)
    grads["mix_norm"] = jnp.concatenate(d_mix, axis=0)
    grads["ffn_post_norm"] = jnp.concatenate(d_post, axis=0)
    grads["q_lora_norm"] = jnp.concatenate(d_qln, axis=0)
    grads["final_norm"] = d_final
    rep_flat = jnp.concatenate([grads[n].reshape(-1) for n in rep_names] + [loss_part[0, :1]])
    n_rep = rep_flat.shape[0]
    rep_rows = -(-n_rep // (8 * REP_COLS)) * 8
    rep_g = jnp.concatenate([rep_flat, jnp.zeros((rep_rows * REP_COLS - n_rep,), F32)]).reshape(rep_rows, REP_COLS)
    ffn_parts[0], rep_parts = _exchange([(slabs[0].reshape(3, N_DEV, fs, D), "scatter_mid"), (rep_g, "to_all")],
                                        "comm_grad_exchange")
    ffn_sum = jnp.stack([_sum_parts(p.reshape(N_DEV, 3 * fs, D), fs, "sum_ffn").reshape(3, fs, D)
                         for p in ffn_parts])
    misc_sum_parts = _unpack_misc(misc_parts[0], misc_shapes)
    rep_sum = _sum_parts(rep_parts, _row_tile(rep_rows, 128), "sum_rep").reshape(-1)

    def sum_small(p):
        shp = p.shape[1:]
        two_d = (int(np.prod(shp[:-1])), shp[-1])
        return _sum_parts(p.reshape((N_DEV,) + two_d), two_d[0], "sum_misc").reshape(shp)

    g_wo, g_wdq, g_wuq, g_wdkv, g_pw, g_ps = [sum_small(p) for p in misc_sum_parts]
    grads.update(w_o=g_wo, w_dq=g_wdq, w_uq=g_wuq.reshape(w_uq.shape), w_dkv=g_wdkv, pool_w=g_pw, pool_scale=g_ps)
    for kind, (npre, npost) in enumerate((("ffn_pre_wg", "ffn_post_wg"), ("ffn_pre_wu", "ffn_post_wu"),
                                          ("ffn_pre_wd", "ffn_post_wd"))):
        pre = ffn_sum[0::2, kind]
        post = ffn_sum[1::2, kind]
        if kind < 2:
            pre, post = jnp.swapaxes(pre, 1, 2), jnp.swapaxes(post, 1, 2)
        grads[npre], grads[npost] = pre, post
    off = 0
    for n in rep_names:
        size = int(np.prod(weights[n].shape))
        grads[n] = rep_sum[off:off + size].reshape(weights[n].shape)
        off += size
    loss = rep_sum[off]
    big_rep_sum = _sum_parts(misc_parts[1], _row_tile(misc_parts[1].shape[1], 64), "sum_rep_big")
    off = 0
    for n in big_rep_names:
        rows = int(np.prod(weights[n].shape)) // REP_COLS
        grads[n] = big_rep_sum[off:off + rows].reshape(weights[n].shape)
        off += rows

    deltas, new_m, new_v = {}, {}, {}
    for n in order:
        deltas[n], new_m[n], new_v[n] = _adamw_nd(weights[n], grads[n], moments_m[n], moments_v[n])
    return (loss, grad_x, *[grads[n] for n in order], *[deltas[n] for n in order],
            *[new_m[n] for n in order], *[new_v[n] for n in order])
```
